```python
import math
import jax, jax.numpy as jnp
from jax import lax
import numpy as np

D_MODEL = 1024
BATCH = 8
SEQ = 4096
DEPTH = 4
DEC_BATCH = 4
DEC_SEQ = 8192
PAST_LEN = 128

GRID_W = 64
N_MIXERS = 4
Q_BLOCK = 128
LN_EPS = 1e-5
RMS_EPS = 1e-6
ALPHA = (2.0 * DEPTH) ** 0.25
BETA = (8.0 * DEPTH) ** -0.25

A_HEADS = 16
A_KV_HEADS = 4
A_HEAD_DIM = 64
ROPE_BASE = 10000.0

B_PATTERNS = ((128, 1), (512, 4), (2048, 16))
B_HEADS_PER_GROUP = 4
B_HEADS = B_HEADS_PER_GROUP * len(B_PATTERNS)
B_HEAD_DIM = 64

C_GROUP = 16
C_N_GROUPS = D_MODEL // C_GROUP
C_STATE = 64
C_GROUPS_PER_BLOCK = 8

D_HEADS = 12
D_QK_DIM = 32
D_V_DIM = 64

REL_BUCKETS = 32
REL_MAX_DIST = 128
REL_HEADS = 12

MEM_LEN = 256
X_HEADS = 4
X_HEAD_DIM = D_MODEL // X_HEADS

N_EXPERTS = 16
D_EXPERT = 1024
EC_CAPACITY = 2

N_A = (DEPTH + 3) // N_MIXERS
N_B = (DEPTH + 2) // N_MIXERS
N_C = (DEPTH + 1) // N_MIXERS
N_D = DEPTH // N_MIXERS

kernel_name = 'hybrid_bidir_encoder_two_groups'


def _layer_norm(x, g, b):
    xf = x.astype(jnp.float32)
    mu = jnp.mean(xf, -1, keepdims=True)
    var = jnp.mean(jnp.square(xf - mu), -1, keepdims=True)
    return ((xf - mu) * lax.rsqrt(var + LN_EPS) * g + b).astype(x.dtype)


def _rms_norm(x, g):
    xf = x.astype(jnp.float32)
    return (xf * lax.rsqrt(jnp.mean(xf * xf, -1, keepdims=True) + RMS_EPS) * g).astype(x.dtype)


def _rel_bucket(rel):
    half = REL_BUCKETS // 2
    max_exact = half // 2
    n = jnp.abs(rel)
    large = max_exact + (jnp.log(jnp.maximum(n, 1).astype(jnp.float32) / max_exact)
                         / math.log(REL_MAX_DIST / max_exact) * (half - max_exact)).astype(jnp.int32)
    large = jnp.minimum(large, half - 1)
    return jnp.where(rel > 0, half, 0) + jnp.where(n < max_exact, n, large)


def _blocks(t):
    b, s = t.shape[:2]
    return jnp.moveaxis(t.reshape(b, s // Q_BLOCK, Q_BLOCK, *t.shape[2:]), 1, 0)


def _unblocks(t):
    nb, b, q = t.shape[:3]
    return jnp.moveaxis(t, 0, 1).reshape(b, nb * q, *t.shape[3:])


def _axial_rope_tables(s):
    n_rows = s // GRID_W
    rows, cols = jnp.meshgrid(jnp.arange(n_rows), jnp.arange(GRID_W), indexing='ij')
    half = A_HEAD_DIM // 2
    freqs = ROPE_BASE ** (-jnp.arange(0, half, 2, dtype=jnp.float32) / half)
    ang_r = rows.reshape(-1)[:, None].astype(jnp.float32) * freqs
    ang_c = cols.reshape(-1)[:, None].astype(jnp.float32) * freqs
    return jnp.cos(ang_r), jnp.sin(ang_r), jnp.cos(ang_c), jnp.sin(ang_c)


def _rotate(x, cos, sin):
    x1, x2 = jnp.split(x, 2, axis=-1)
    c, s = cos[:, None, :], sin[:, None, :]
    return jnp.concatenate([x1 * c - x2 * s, x1 * s + x2 * c], -1).astype(x.dtype)


def _axial_rope(x, tabs):
    cr, sr, cc, sc = tabs
    x_row, x_col = jnp.split(x, 2, axis=-1)
    return jnp.concatenate([_rotate(x_row, cr, sr), _rotate(x_col, cc, sc)], -1)


def _axial_gqa(x, w_qkv, q_gain, k_gain, w_o):
    b, s, _ = x.shape
    qkv = x @ w_qkv
    q, k, v = jnp.split(qkv, [A_HEADS * A_HEAD_DIM, (A_HEADS + A_KV_HEADS) * A_HEAD_DIM], -1)
    q = q.reshape(b, s, A_HEADS, A_HEAD_DIM)
    k = k.reshape(b, s, A_KV_HEADS, A_HEAD_DIM)
    v = v.reshape(b, s, A_KV_HEADS, A_HEAD_DIM)
    tabs = _axial_rope_tables(s)
    q = _axial_rope(_rms_norm(q, q_gain), tabs)
    k = _axial_rope(_rms_norm(k, k_gain), tabs)
    q = q.reshape(b, s, A_KV_HEADS, A_HEADS // A_KV_HEADS, A_HEAD_DIM)
    scale = A_HEAD_DIM ** -0.5

    def blk(qb):
        logits = jnp.einsum('bqgrd,bkgd->bgrqk', qb, k).astype(jnp.float32) * scale
        p = jax.nn.softmax(logits, -1).astype(v.dtype)
        return jnp.einsum('bgrqk,bkgd->bqgrd', p, v)

    o = _unblocks(lax.map(blk, _blocks(q)))
    return o.reshape(b, s, A_HEADS * A_HEAD_DIM) @ w_o


def _dilated_attention(x, w_qkv, w_o, rel_bias):
    b, s, _ = x.shape
    n_g = len(B_PATTERNS)
    q, k, v = jnp.split(x @ w_qkv, 3, -1)
    q = q.reshape(b, s, n_g, B_HEADS_PER_GROUP, B_HEAD_DIM)
    k = k.reshape(b, s, n_g, B_HEADS_PER_GROUP, B_HEAD_DIM)
    v = v.reshape(b, s, n_g, B_HEADS_PER_GROUP, B_HEAD_DIM)
    ks = [k[:, :, g] for g in range(n_g)]
    vs = [v[:, :, g] for g in range(n_g)]
    scale = B_HEAD_DIM ** -0.5
    t_blocks = jnp.arange(s, dtype=jnp.int32).reshape(-1, Q_BLOCK)

    def blk(args):
        qb, t = args
        outs, lses = [], []
        for g, (window, dil) in enumerate(B_PATTERNS):
            n_side = window // (2 * dil)
            offs = dil * jnp.arange(-n_side, n_side + 1, dtype=jnp.int32)
            idx = t[:, None] + offs[None, :]
            valid = (idx >= 0) & (idx < s)
            idx_c = jnp.clip(idx, 0, s - 1)
            kg = jnp.take(ks[g], idx_c, axis=1)
            vg = jnp.take(vs[g], idx_c, axis=1)
            bias = rel_bias[_rel_bucket(offs)][:, g * B_HEADS_PER_GROUP:(g + 1) * B_HEADS_PER_GROUP]
            logits = (jnp.einsum('bqhd,bqjhd->bhqj', qb[:, :, g], kg).astype(jnp.float32) * scale
                      + bias.T[None, :, None, :])
            logits = jnp.where(valid[None, None], logits, -jnp.inf)
            lse = jax.nn.logsumexp(logits, -1)
            p = jnp.exp(logits - lse[..., None]).astype(vg.dtype)
            outs.append(jnp.einsum('bhqj,bqjhd->bqhd', p, vg))
            lses.append(lse)
        w = jax.nn.softmax(jnp.stack(lses, 0), axis=0)
        w = jnp.transpose(w, (1, 3, 0, 2))[..., None]
        return (jnp.stack(outs, 2) * w).astype(qb.dtype)

    o = _unblocks(lax.map(blk, (_blocks(q), t_blocks)))
    return o.reshape(b, s, B_HEADS * B_HEAD_DIM) @ w_o


def _s5_direction(u, lam_re, lam_im, log_dt, b_re, b_im, c_re, c_im):
    f32 = jnp.float32
    lr, li = lam_re.astype(f32), lam_im.astype(f32)
    dt = jnp.exp(log_dt.astype(f32))[:, None]
    mag = jnp.exp(lr * dt)
    ar, ai = mag * jnp.cos(li * dt), mag * jnp.sin(li * dt)
    den = lr * lr + li * li
    zr = ((ar - 1.0) * lr + ai * li) / den
    zi = (ai * lr - (ar - 1.0) * li) / den
    br, bi = b_re.astype(f32), b_im.astype(f32)
    bbr = zr[..., None] * br - zi[..., None] * bi
    bbi = zr[..., None] * bi + zi[..., None] * br
    bu_r = jnp.einsum('bsgc,gpc->bsgp', u, bbr)
    bu_i = jnp.einsum('bsgc,gpc->bsgp', u, bbi)
    a_r = jnp.broadcast_to(ar, bu_r.shape)
    a_i = jnp.broadcast_to(ai, bu_i.shape)

    def combine(e1, e2):
        a1r, a1i, b1r, b1i = e1
        a2r, a2i, b2r, b2i = e2
        return (a2r * a1r - a2i * a1i, a2r * a1i + a2i * a1r,
                a2r * b1r - a2i * b1i + b2r, a2r * b1i + a2i * b1r + b2i)

    _, _, hr, hi = lax.associative_scan(combine, (a_r, a_i, bu_r, bu_i), axis=1)
    return (jnp.einsum('bsgp,gcp->bsgc', hr, c_re.astype(f32))
            - jnp.einsum('bsgp,gcp->bsgc', hi, c_im.astype(f32)))


def _group_blocks(t):
    nb = C_N_GROUPS // C_GROUPS_PER_BLOCK
    return jnp.moveaxis(t.reshape(t.shape[0], nb, C_GROUPS_PER_BLOCK, *t.shape[2:]), 1, 0)


def _bidir_s5(x, lam_re, lam_im, log_dt, b_re, b_im, c_re, c_im, d_skip, w_glu):
    b, s, d = x.shape
    nb = C_N_GROUPS // C_GROUPS_PER_BLOCK
    u = x.astype(jnp.float32).reshape(b, s, nb, C_GROUPS_PER_BLOCK, C_GROUP)
    u = jnp.moveaxis(u, 2, 0)

    def blk(args):
        ub, lr, li, ldt, br, bi, cr, ci = args
        fwd = _s5_direction(ub, lr[0], li[0], ldt[0], br[0], bi[0], cr[0], ci[0])
        bwd = jnp.flip(_s5_direction(jnp.flip(ub, 1), lr[1], li[1], ldt[1], br[1], bi[1], cr[1], ci[1]), 1)
        return fwd + bwd

    y = lax.map(blk, (u, _group_blocks(lam_re), _group_blocks(lam_im), _group_blocks(log_dt),
                      _group_blocks(b_re), _group_blocks(b_im), _group_blocks(c_re), _group_blocks(c_im)))
    y = jnp.moveaxis(y, 0, 2).reshape(b, s, d) + d_skip * x
    z = jax.nn.gelu(y).astype(x.dtype)
    a, g = jnp.split(z @ w_glu, 2, -1)
    return a * jax.nn.sigmoid(g)


def _diff_attention(x, w_qkv, lam, norm_gain, w_o, rel_bias, lambda_init):
    b, s, _ = x.shape
    qk_w = D_HEADS * 2 * D_QK_DIM
    q, k, v = jnp.split(x @ w_qkv, [qk_w, 2 * qk_w], -1)
    q = q.reshape(b, s, D_HEADS, 2, D_QK_DIM)
    k = k.reshape(b, s, D_HEADS, 2, D_QK_DIM)
    v = v.reshape(b, s, D_HEADS, D_V_DIM)
    lf = lam.astype(jnp.float32)
    lam_val = jnp.exp(jnp.sum(lf[0] * lf[1])) - jnp.exp(jnp.sum(lf[2] * lf[3])) + lambda_init
    scale = D_QK_DIM ** -0.5
    pos = jnp.arange(s, dtype=jnp.int32)
    t_blocks = pos.reshape(-1, Q_BLOCK)

    def blk(args):
        qb, t = args
        bias = jnp.transpose(rel_bias[_rel_bucket(pos[None, :] - t[:, None])], (2, 0, 1))
        logits = jnp.einsum('bqhmd,bkhmd->bmhqk', qb, k).astype(jnp.float32) * scale + bias
        p = jax.nn.softmax(logits, -1)
        a = (p[:, 0] - lam_val * p[:, 1]).astype(v.dtype)
        return jnp.einsum('bhqk,bkhd->bqhd', a, v)

    o = _unblocks(lax.map(blk, (_blocks(q), t_blocks)))
    o = _rms_norm(o, norm_gain) * (1.0 - lambda_init)
    return o.reshape(b, s, D_HEADS * D_V_DIM) @ w_o


def _memory_attention(x, mem, w_q, w_kv, w_o):
    b, s, d = x.shape
    q = (x @ w_q).reshape(b, s, X_HEADS, X_HEAD_DIM)
    k, v = jnp.split(mem @ w_kv, 2, -1)
    k = k.reshape(b, mem.shape[1], X_HEADS, X_HEAD_DIM)
    v = v.reshape(b, mem.shape[1], X_HEADS, X_HEAD_DIM)
    logits = jnp.einsum('bshd,bmhd->bhsm', q, k).astype(jnp.float32) * (X_HEAD_DIM ** -0.5)
    p = jax.nn.softmax(logits, -1).astype(v.dtype)
    return jnp.einsum('bhsm,bmhd->bshd', p, v).reshape(b, s, d) @ w_o


def _expert_choice_ffn(x, w_router, w_gate, w_up, w_down):
    b, s, d = x.shape
    xt = x.reshape(b * s, d)
    cap = EC_CAPACITY * (b * s) // N_EXPERTS
    aff = jax.nn.softmax((xt @ w_router).astype(jnp.float32), -1)
    gate, idx = lax.top_k(aff.T, cap)
    xe = jnp.take(xt, idx, axis=0)
    h = jax.nn.silu(jnp.einsum('ecd,edf->ecf', xe, w_gate)) * jnp.einsum('ecd,edf->ecf', xe, w_up)
    ye = jnp.einsum('ecf,efd->ecd', h, w_down) * gate[..., None].astype(x.dtype)
    out = jnp.zeros_like(xt).at[idx.reshape(-1)].add(ye.reshape(-1, d).astype(xt.dtype))
    return out.reshape(b, s, d)


def _trunk(x, mem, p):
    for i in range(DEPTH):
        m, j = i % N_MIXERS, i // N_MIXERS
        if m == 0:
            h = _axial_gqa(x, p['a_w_qkv'][j], p['a_q_gain'][j], p['a_k_gain'][j], p['a_w_o'][j])
        elif m == 1:
            h = _dilated_attention(x, p['b_w_qkv'][j], p['b_w_o'][j], p['rel_bias'])
        elif m == 2:
            h = _bidir_s5(x, p['c_lam_re'][j], p['c_lam_im'][j], p['c_log_dt'][j], p['c_b_re'][j],
                          p['c_b_im'][j], p['c_c_re'][j], p['c_c_im'][j], p['c_d'][j], p['c_w_glu'][j])
        else:
            h = _diff_attention(x, p['d_w_qkv'][j], p['d_lam'][j], p['d_norm_gain'][j], p['d_w_o'][j],
                                p['rel_bias'], 0.8 - 0.6 * math.exp(-0.3 * i))
        x = _layer_norm(ALPHA * x + h, p['ln_g'][i, 0], p['ln_b'][i, 0])
        h = _memory_attention(x, mem, p['x_w_q'][i], p['x_w_kv'][i], p['x_w_o'][i])
        x = _layer_norm(ALPHA * x + h, p['ln_g'][i, 1], p['ln_b'][i, 1])
        h = _expert_choice_ffn(x, p['moe_w_router'][i], p['moe_w_gate'][i], p['moe_w_up'][i], p['moe_w_down'][i])
        x = _layer_norm(ALPHA * x + h, p['ln_g'][i, 2], p['ln_b'][i, 2])
    return x


def setup_inputs(seed: int = 0) -> dict:
    key = jax.random.key(seed)
    ks = iter(jax.random.split(key, 40))
    D, P, G = D_MODEL, C_STATE, C_N_GROUPS

    def nrm(shape, scale):
        return jax.random.normal(next(ks), shape, jnp.float32) * scale

    return {
        'x_prompt': nrm((BATCH, SEQ, D), 1.0),
        'x_sample': nrm((DEC_BATCH, DEC_SEQ, D), 1.0),
        'mem_prompt': nrm((BATCH, MEM_LEN, D), 1.0),
        'mem_sample': nrm((DEC_BATCH, MEM_LEN, D), 1.0),
        'rel_bias': nrm((REL_BUCKETS, REL_HEADS), 0.2),
        'ln_g': 1.0 + nrm((DEPTH, 3, D), 0.02),
        'ln_b': nrm((DEPTH, 3, D), 0.02),
        'a_w_qkv': nrm((N_A, D, (A_HEADS + 2 * A_KV_HEADS) * A_HEAD_DIM), D ** -0.5),
        'a_q_gain': 1.0 + nrm((N_A, A_HEAD_DIM), 0.02),
        'a_k_gain': 1.0 + nrm((N_A, A_HEAD_DIM), 0.02),
        'a_w_o': nrm((N_A, A_HEADS * A_HEAD_DIM, D), (A_HEADS * A_HEAD_DIM) ** -0.5 * BETA),
        'b_w_qkv': nrm((N_B, D, 3 * B_HEADS * B_HEAD_DIM), D ** -0.5),
        'b_w_o': nrm((N_B, B_HEADS * B_HEAD_DIM, D), (B_HEADS * B_HEAD_DIM) ** -0.5 * BETA),
        'c_lam_re': -0.5 + nrm((N_C, 2, G, P), 0.01),
        'c_lam_im': jnp.pi * jnp.arange(P, dtype=jnp.float32) + nrm((N_C, 2, G, P), 0.01),
        'c_log_dt': jax.random.uniform(next(ks), (N_C, 2, G), jnp.float32, math.log(1e-3), math.log(1e-1)),
        'c_b_re': nrm((N_C, 2, G, P, C_GROUP), (2.0 * C_GROUP) ** -0.5),
        'c_b_im': nrm((N_C, 2, G, P, C_GROUP), (2.0 * C_GROUP) ** -0.5),
        'c_c_re': nrm((N_C, 2, G, C_GROUP, P), (2.0 * P) ** -0.5),
        'c_c_im': nrm((N_C, 2, G, C_GROUP, P), (2.0 * P) ** -0.5),
        'c_d': nrm((N_C, D), 1.0),
        'c_w_glu': nrm((N_C, D, 2 * D), D ** -0.5 * BETA),
        'd_w_qkv': nrm((N_D, D, 2 * D_HEADS * 2 * D_QK_DIM + D_HEADS * D_V_DIM), D ** -0.5),
        'd_lam': nrm((N_D, 4, D_QK_DIM), 0.1),
        'd_norm_gain': 1.0 + nrm((N_D, D_V_DIM), 0.02),
        'd_w_o': nrm((N_D, D_HEADS * D_V_DIM, D), (D_HEADS * D_V_DIM) ** -0.5 * BETA),
        'x_w_q': nrm((DEPTH, D, D), D ** -0.5),
        'x_w_kv': nrm((DEPTH, D, 2 * D), D ** -0.5),
        'x_w_o': nrm((DEPTH, D, D), D ** -0.5 * BETA),
        'moe_w_router': nrm((DEPTH, D, N_EXPERTS), D ** -0.5),
        'moe_w_gate': nrm((DEPTH, N_EXPERTS, D, D_EXPERT), D ** -0.5),
        'moe_w_up': nrm((DEPTH, N_EXPERTS, D, D_EXPERT), D ** -0.5),
        'moe_w_down': nrm((DEPTH, N_EXPERTS, D_EXPERT, D), D_EXPERT ** -0.5 * BETA),
    }


def reference(x_prompt, x_sample, mem_prompt, mem_sample, rel_bias, ln_g, ln_b,
              a_w_qkv, a_q_gain, a_k_gain, a_w_o, b_w_qkv, b_w_o,
              c_lam_re, c_lam_im, c_log_dt, c_b_re, c_b_im, c_c_re, c_c_im, c_d, c_w_glu,
              d_w_qkv, d_lam, d_norm_gain, d_w_o, x_w_q, x_w_kv, x_w_o,
              moe_w_router, moe_w_gate, moe_w_up, moe_w_down):
    p = dict(rel_bias=rel_bias, ln_g=ln_g, ln_b=ln_b,
             a_w_qkv=a_w_qkv, a_q_gain=a_q_gain, a_k_gain=a_k_gain, a_w_o=a_w_o,
             b_w_qkv=b_w_qkv, b_w_o=b_w_o,
             c_lam_re=c_lam_re, c_lam_im=c_lam_im, c_log_dt=c_log_dt, c_b_re=c_b_re, c_b_im=c_b_im,
             c_c_re=c_c_re, c_c_im=c_c_im, c_d=c_d, c_w_glu=c_w_glu,
             d_w_qkv=d_w_qkv, d_lam=d_lam, d_norm_gain=d_norm_gain, d_w_o=d_w_o,
             x_w_q=x_w_q, x_w_kv=x_w_kv, x_w_o=x_w_o,
             moe_w_router=moe_w_router, moe_w_gate=moe_w_gate, moe_w_up=moe_w_up, moe_w_down=moe_w_down)
    y_prompt = _trunk(x_prompt, mem_prompt, p)
    y_sample = _trunk(x_sample, mem_sample, p)
    return (y_prompt, y_sample)
```

```python
import functools
import math

import numpy as np
import jax
import jax.numpy as jnp
from jax import lax
from jax.experimental import pallas as pl
from jax.experimental.pallas import tpu as pltpu

F32 = jnp.float32
BF = jnp.bfloat16
I32 = jnp.int32

D_MODEL = 1024
DEPTH = 4
GRID_W = 64
N_MIXERS = 4
LN_EPS = 1e-5
RMS_EPS = 1e-6
ALPHA = (2.0 * DEPTH) ** 0.25

A_HEADS = 16
A_KV_HEADS = 4
A_HEAD_DIM = 64
ROPE_BASE = 10000.0

B_PATTERNS = ((128, 1), (512, 4), (2048, 16))
B_HEADS_PER_GROUP = 4
B_HEAD_DIM = 64

C_GROUP = 16
C_N_GROUPS = D_MODEL // C_GROUP
C_STATE = 64
C_CHUNK = 16

D_HEADS = 12
D_QK_DIM = 32
D_V_DIM = 64

REL_BUCKETS = 32
REL_MAX_DIST = 128

X_HEADS = 4
X_HEAD_DIM = D_MODEL // X_HEADS

N_EXPERTS = 16
EC_CAPACITY = 2

LANES = 128
BF16_ROWS = 16
VMEM_LIMIT = 56 * 1024 * 1024
NEG = -1e30


def _params(*sem):
    return pltpu.CompilerParams(dimension_semantics=sem, vmem_limit_bytes=VMEM_LIMIT)


def _tile(n, pref):
    t = min(n, pref)
    assert n % t == 0, (n, pref)
    return t


def _ln_rows(v, g, b):
    mu = jnp.mean(v, -1, keepdims=True)
    c = v - mu
    var = jnp.mean(c * c, -1, keepdims=True)
    return c * lax.rsqrt(var + LN_EPS) * g + b


def _dot_nt(a, b):
    return lax.dot_general(a, b, (((1,), (1,)), ((), ())), preferred_element_type=F32)


def _dot(a, b):
    return jnp.dot(a, b, preferred_element_type=F32)


def _proj_kernel(x_ref, w_ref, o_ref):
    o_ref[...] = _dot(x_ref[...].astype(BF), w_ref[...]).astype(o_ref.dtype)


def _proj(x, w, tm=512):
    n, k = x.shape
    m = w.shape[1]
    tm = _tile(n, tm)
    return pl.pallas_call(
        _proj_kernel,
        grid=(n // tm,),
        in_specs=[pl.BlockSpec((tm, k), lambda i: (i, 0)),
                  pl.BlockSpec((k, m), lambda i: (0, 0))],
        out_specs=pl.BlockSpec((tm, m), lambda i: (i, 0)),
        out_shape=jax.ShapeDtypeStruct((n, m), BF),
        compiler_params=_params("parallel"),
        name="proj",
    )(x, w)


def _post_kernel(x_ref, o_ref, w_ref, g_ref, b_ref, y_ref):
    h = _dot(o_ref[...], w_ref[...])
    y_ref[...] = _ln_rows(ALPHA * x_ref[...] + h, g_ref[...], b_ref[...])


def _post(x, o, w, g, b, tm=512):
    n = x.shape[0]
    ko = o.shape[1]
    tm = _tile(n, tm)
    return pl.pallas_call(
        _post_kernel,
        grid=(n // tm,),
        in_specs=[pl.BlockSpec((tm, D_MODEL), lambda i: (i, 0)),
                  pl.BlockSpec((tm, ko), lambda i: (i, 0)),
                  pl.BlockSpec((ko, D_MODEL), lambda i: (0, 0)),
                  pl.BlockSpec((1, D_MODEL), lambda i: (0, 0)),
                  pl.BlockSpec((1, D_MODEL), lambda i: (0, 0))],
        out_specs=pl.BlockSpec((tm, D_MODEL), lambda i: (i, 0)),
        out_shape=jax.ShapeDtypeStruct((n, D_MODEL), F32),
        compiler_params=_params("parallel"),
        name="post",
    )(x, o, w, g, b)


def _post_b_kernel(x_ref, o0_ref, o1_ref, o2_ref, l0_ref, l1_ref, l2_ref, w_ref, g_ref, b_ref, y_ref):
    l0, l1, l2 = l0_ref[...], l1_ref[...], l2_ref[...]
    m = jnp.maximum(jnp.maximum(l0, l1), l2)
    e0, e1, e2 = jnp.exp(l0 - m), jnp.exp(l1 - m), jnp.exp(l2 - m)
    inv = 1.0 / (e0 + e1 + e2)
    gw = B_HEADS_PER_GROUP * LANES
    h = _dot((o0_ref[...].astype(F32) * (e0 * inv)).astype(BF), w_ref[0:gw, :])
    h = h + _dot((o1_ref[...].astype(F32) * (e1 * inv)).astype(BF), w_ref[gw:2 * gw, :])
    h = h + _dot((o2_ref[...].astype(F32) * (e2 * inv)).astype(BF), w_ref[2 * gw:3 * gw, :])
    y_ref[...] = _ln_rows(ALPHA * x_ref[...] + h, g_ref[...], b_ref[...])


def _post_b(x, os_, ls_, w, g, b, tm=512):
    n = x.shape[0]
    tm = _tile(n, tm)
    gw = B_HEADS_PER_GROUP * LANES
    row = lambda i: (i, 0)
    fixed = lambda i: (0, 0)
    return pl.pallas_call(
        _post_b_kernel,
        grid=(n // tm,),
        in_specs=[pl.BlockSpec((tm, D_MODEL), row)] + [pl.BlockSpec((tm, gw), row)] * 6
        + [pl.BlockSpec((3 * gw, D_MODEL), fixed), pl.BlockSpec((1, D_MODEL), fixed),
           pl.BlockSpec((1, D_MODEL), fixed)],
        out_specs=pl.BlockSpec((tm, D_MODEL), row),
        out_shape=jax.ShapeDtypeStruct((n, D_MODEL), F32),
        compiler_params=_params("parallel"),
        name="post_b",
    )(x, *os_, *ls_, w, g, b)


def _post_c_kernel(x_ref, ys_ref, d_ref, w_ref, g_ref, b_ref, y_ref):
    x = x_ref[...]
    z = jax.nn.gelu(ys_ref[...] + d_ref[...] * x).astype(BF)
    h = _dot(z, w_ref[...])
    hh = h[:, :D_MODEL] * jax.nn.sigmoid(h[:, D_MODEL:])
    y_ref[...] = _ln_rows(ALPHA * x + hh, g_ref[...], b_ref[...])


def _post_c(x, ys, d, w, g, b, tm=512):
    n = x.shape[0]
    tm = _tile(n, tm)
    row = lambda i: (i, 0)
    fixed = lambda i: (0, 0)
    return pl.pallas_call(
        _post_c_kernel,
        grid=(n // tm,),
        in_specs=[pl.BlockSpec((tm, D_MODEL), row), pl.BlockSpec((tm, D_MODEL), row),
                  pl.BlockSpec((1, D_MODEL), fixed), pl.BlockSpec((D_MODEL, 2 * D_MODEL), fixed),
                  pl.BlockSpec((1, D_MODEL), fixed), pl.BlockSpec((1, D_MODEL), fixed)],
        out_specs=pl.BlockSpec((tm, D_MODEL), row),
        out_shape=jax.ShapeDtypeStruct((n, D_MODEL), F32),
        compiler_params=_params("parallel"),
        name="post_c",
    )(x, ys, d, w, g, b)


_A_QK_TILES = A_HEADS + A_KV_HEADS
_A_COLS = (_A_QK_TILES + A_KV_HEADS) * LANES


def _rope_partner():
    d = np.arange(A_HEAD_DIM)
    e = d % (A_HEAD_DIM // 2)
    lo = e < A_HEAD_DIM // 4
    return np.where(lo, d + A_HEAD_DIM // 4, d - A_HEAD_DIM // 4), np.where(lo, -1.0, 1.0).astype(np.float32)


def _prep_a(w_qkv, q_gain, k_gain, w_o):
    partner, sign = _rope_partner()
    nqk = _A_QK_TILES * A_HEAD_DIM
    wqk = w_qkv[:, :nqk].reshape(D_MODEL, _A_QK_TILES, A_HEAD_DIM)
    wsw = wqk[:, :, partner] * sign
    wqk = jnp.concatenate([wqk, wsw], -1).reshape(D_MODEL, _A_QK_TILES * LANES)
    wv = w_qkv[:, nqk:].reshape(D_MODEL, A_KV_HEADS, A_HEAD_DIM)
    wv = jnp.concatenate([wv, jnp.zeros_like(wv)], -1).reshape(D_MODEL, A_KV_HEADS * LANES)
    w = jnp.concatenate([wqk, wv], 1).astype(BF)
    gq = jnp.concatenate([q_gain, q_gain[partner]]) * (A_HEAD_DIM ** -0.5 * 0.5)
    gk = jnp.concatenate([k_gain, k_gain[partner]])
    gains = jnp.stack([gq, gk], 0)
    wo = w_o.reshape(A_HEADS, A_HEAD_DIM, D_MODEL)
    wo = jnp.concatenate([wo, jnp.zeros_like(wo)], 1).reshape(A_HEADS * LANES, D_MODEL).astype(BF)
    return w, gains, wo


def _rope_table(s):
    pos = jnp.arange(s)
    rows, cols = (pos // GRID_W).astype(F32), (pos % GRID_W).astype(F32)
    half = A_HEAD_DIM // 2
    freqs = ROPE_BASE ** (-jnp.arange(0, half, 2, dtype=F32) / half)
    ang_r = rows[:, None] * freqs
    ang_c = cols[:, None] * freqs
    ang = jnp.concatenate([ang_r, ang_r, ang_c, ang_c], -1)
    return jnp.concatenate([jnp.cos(ang), jnp.sin(ang)], -1)


def _proj_a_kernel(x_ref, w_ref, cs_ref, g_ref, o_ref):
    y = _dot(x_ref[...].astype(BF), w_ref[...])
    cs = cs_ref[...]
    gq = cs * g_ref[0:1, :]
    gk = cs * g_ref[1:2, :]
    for h in range(_A_QK_TILES):
        t = y[:, LANES * h:LANES * (h + 1)]
        r = lax.rsqrt(jnp.sum(t * t, -1, keepdims=True) * (1.0 / LANES) + RMS_EPS)
        e = t * r * (gq if h < A_HEADS else gk)
        o_ref[:, LANES * h:LANES * (h + 1)] = (e + pltpu.roll(e, LANES // 2, 1)).astype(BF)
    o_ref[:, _A_QK_TILES * LANES:] = y[:, _A_QK_TILES * LANES:].astype(BF)


def _proj_a(x, w, cs, gains, s, tm=512):
    n = x.shape[0]
    tm = _tile(s, tm)
    per = s // tm
    return pl.pallas_call(
        _proj_a_kernel,
        grid=(n // tm,),
        in_specs=[pl.BlockSpec((tm, D_MODEL), lambda i: (i, 0)),
                  pl.BlockSpec((D_MODEL, _A_COLS), lambda i: (0, 0)),
                  pl.BlockSpec((tm, LANES), lambda i: (i % per, 0)),
                  pl.BlockSpec((2, LANES), lambda i: (0, 0))],
        out_specs=pl.BlockSpec((tm, _A_COLS), lambda i: (i, 0)),
        out_shape=jax.ShapeDtypeStruct((n, _A_COLS), BF),
        compiler_params=_params("parallel"),
        name="proj_a",
    )(x, w, cs, gains)


def _flash_a_kernel(q_ref, k_ref, v_ref, o_ref, acc_ref, *, tk, nk):
    tq = q_ref.shape[0]
    rep = A_HEADS // A_KV_HEADS
    q = jnp.concatenate([q_ref[:, LANES * r:LANES * (r + 1)] for r in range(rep)], axis=0)
    acc_ref[...] = jnp.zeros_like(acc_ref)

    def body(j, carry):
        m, l = carry
        off = pl.multiple_of(j * tk, tk)
        s = _dot_nt(q, k_ref[pl.ds(off, tk), :])
        m_new = jnp.maximum(m, jnp.max(s, -1, keepdims=True))
        a = jnp.exp(m - m_new)
        p = jnp.exp(s - m_new)
        l = a * l + jnp.sum(p, -1, keepdims=True)
        acc_ref[...] = a * acc_ref[...] + _dot(p.astype(BF), v_ref[pl.ds(off, tk), :])
        return m_new, l

    init = (jnp.full((rep * tq, 1), -jnp.inf, F32), jnp.zeros((rep * tq, 1), F32))
    _, l = lax.fori_loop(0, nk, body, init)
    o = acc_ref[...] / l
    for r in range(rep):
        o_ref[:, LANES * r:LANES * (r + 1)] = o[r * tq:(r + 1) * tq].astype(BF)


def _flash_a(qkv, tq=256, tk=512):
    b, s, _ = qkv.shape
    tq, tk = _tile(s, tq), _tile(s, tk)
    rep = A_HEADS // A_KV_HEADS
    gw = rep * LANES
    return pl.pallas_call(
        functools.partial(_flash_a_kernel, tk=tk, nk=s // tk),
        grid=(b, A_KV_HEADS, s // tq),
        in_specs=[pl.BlockSpec((None, tq, gw), lambda bi, g, i: (bi, i, g)),
                  pl.BlockSpec((None, s, LANES), lambda bi, g, i: (bi, 0, A_HEADS + g)),
                  pl.BlockSpec((None, s, LANES), lambda bi, g, i: (bi, 0, _A_QK_TILES + g))],
        out_specs=pl.BlockSpec((None, tq, gw), lambda bi, g, i: (bi, i, g)),
        out_shape=jax.ShapeDtypeStruct((b, s, A_HEADS * LANES), BF),
        scratch_shapes=[pltpu.VMEM((rep * tq, LANES), F32)],
        compiler_params=_params("parallel", "parallel", "arbitrary"),
        name="flash_a",
    )(qkv, qkv, qkv)


def _bucket(rel):
    half = REL_BUCKETS // 2
    max_exact = half // 2
    n = jnp.abs(rel)
    large = max_exact + (jnp.log(jnp.maximum(n, 1).astype(F32) / max_exact)
                         / math.log(REL_MAX_DIST / max_exact) * (half - max_exact)).astype(I32)
    large = jnp.minimum(large, half - 1)
    return jnp.where(rel > 0, half, 0) + jnp.where(n < max_exact, n, large)


_B_TILES = 3 * len(B_PATTERNS) * B_HEADS_PER_GROUP
_B_GW = B_HEADS_PER_GROUP * LANES


def _pad_heads(w, heads, dim):
    w = w.reshape(w.shape[0], heads, dim)
    return jnp.concatenate([w, jnp.zeros((w.shape[0], heads, LANES - dim), w.dtype)], -1).reshape(
        w.shape[0], heads * LANES)


def _prep_b(w_qkv, w_o):
    nh = len(B_PATTERNS) * B_HEADS_PER_GROUP
    c = nh * B_HEAD_DIM
    wq = _pad_heads(w_qkv[:, :c] * (B_HEAD_DIM ** -0.5), nh, B_HEAD_DIM)
    wk = _pad_heads(w_qkv[:, c:2 * c], nh, B_HEAD_DIM)
    wv = _pad_heads(w_qkv[:, 2 * c:], nh, B_HEAD_DIM)
    w = jnp.concatenate([wq, wk, wv], 1).astype(BF)
    wo = w_o.reshape(nh, B_HEAD_DIM, D_MODEL)
    wo = jnp.concatenate([wo, jnp.zeros_like(wo)], 1).reshape(nh * LANES, D_MODEL).astype(BF)
    return w, wo


def _band_bias(rel_bias, g, tm):
    window, dil = B_PATTERNS[g]
    n_side = window // (2 * dil)
    j = jnp.arange(3 * tm)[None, :] - tm - jnp.arange(tm)[:, None]
    inband = jnp.abs(j) <= n_side
    bias = rel_bias[_bucket(dil * j)][:, :, g * B_HEADS_PER_GROUP:(g + 1) * B_HEADS_PER_GROUP]
    bias = jnp.where(inband[:, :, None], bias, NEG)
    return jnp.transpose(bias, (2, 0, 1)).astype(F32)


def _band_kernel(q_ref, kp_ref, kc_ref, kn_ref, vp_ref, vc_ref, vn_ref, bias_ref, o_ref, lse_ref, *, length):
    tm = q_ref.shape[0]
    mi = pl.program_id(2)
    key_pos = (mi - 1) * tm + lax.broadcasted_iota(I32, (1, 3 * tm), 1)
    valid = (key_pos >= 0) & (key_pos < length)
    for h in range(B_HEADS_PER_GROUP):
        sl = slice(LANES * h, LANES * (h + 1))
        k = jnp.concatenate([kp_ref[:, sl], kc_ref[:, sl], kn_ref[:, sl]], axis=0)
        v = jnp.concatenate([vp_ref[:, sl], vc_ref[:, sl], vn_ref[:, sl]], axis=0)
        s = _dot_nt(q_ref[:, sl], k) + bias_ref[h]
        s = jnp.where(valid, s, NEG)
        m = jnp.max(s, -1, keepdims=True)
        p = jnp.exp(s - m)
        l = jnp.sum(p, -1, keepdims=True)
        o_ref[:, sl] = (_dot(p.astype(BF), v) / l).astype(BF)
        lse_ref[:, sl] = jnp.broadcast_to(m + jnp.log(l), (tm, LANES))


def _band_attention(qkv, bias, g, tm):
    b, s, c = qkv.shape
    dil = B_PATTERNS[g][1]
    length = s // dil
    nblk = c // _B_GW
    view = qkv.reshape(b, length, dil * c)
    nt = length // tm
    ng = len(B_PATTERNS)

    def spec(which, shift):
        def imap(bi, r, mi):
            return (bi, jnp.clip(mi + shift, 0, nt - 1), r * nblk + which * ng + g)
        return pl.BlockSpec((None, tm, _B_GW), imap)

    out_spec = pl.BlockSpec((None, tm, _B_GW), lambda bi, r, mi: (bi, mi, r))
    o, lse = pl.pallas_call(
        functools.partial(_band_kernel, length=length),
        grid=(b, dil, nt),
        in_specs=[spec(0, 0), spec(1, -1), spec(1, 0), spec(1, 1), spec(2, -1), spec(2, 0), spec(2, 1),
                  pl.BlockSpec((B_HEADS_PER_GROUP, tm, 3 * tm), lambda bi, r, mi: (0, 0, 0))],
        out_specs=[out_spec, out_spec],
        out_shape=[jax.ShapeDtypeStruct((b, length, dil * _B_GW), BF),
                   jax.ShapeDtypeStruct((b, length, dil * _B_GW), F32)],
        compiler_params=_params("parallel", "parallel", "arbitrary"),
        name="band_%d" % g,
    )(view, view, view, view, view, view, view, bias)
    return o.reshape(b * s, _B_GW), lse.reshape(b * s, _B_GW)


def _prep_c(lam_re, lam_im, log_dt, b_re, b_im, c_re, c_im):
    hp = lax.Precision.HIGHEST
    L, P, C = C_CHUNK, C_STATE, C_GROUP
    lr, li = lam_re.astype(F32), lam_im.astype(F32)
    dt = jnp.exp(log_dt.astype(F32))[..., None]
    mag = jnp.exp(lr * dt)
    ar, ai = mag * jnp.cos(li * dt), mag * jnp.sin(li * dt)
    den = lr * lr + li * li
    zr = ((ar - 1.0) * lr + ai * li) / den
    zi = (ai * lr - (ar - 1.0) * li) / den
    br, bi = b_re.astype(F32), b_im.astype(F32)
    bbr = zr[..., None] * br - zi[..., None] * bi
    bbi = zr[..., None] * bi + zi[..., None] * br
    cr, ci = c_re.astype(F32), c_im.astype(F32)
    prs, pis = [jnp.ones_like(ar)], [jnp.zeros_like(ai)]
    for _ in range(L):
        pr_, pi_ = prs[-1], pis[-1]
        prs.append(ar * pr_ - ai * pi_)
        pis.append(ar * pi_ + ai * pr_)
    pr, pi = jnp.stack(prs, 0), jnp.stack(pis, 0)

    def lag(pr_k, pi_k):
        tr = pr_k[..., None] * bbr - pi_k[..., None] * bbi
        ti = pr_k[..., None] * bbi + pi_k[..., None] * bbr
        return (jnp.einsum('dgop,kdgpi->kdgoi', cr, tr, precision=hp)
                - jnp.einsum('dgop,kdgpi->kdgoi', ci, ti, precision=hp))

    kern = lag(pr[:L], pi[:L])
    jj = np.arange(L)[:, None]
    ii = np.arange(L)[None, :]
    kf = kern[:, 0][np.clip(ii - jj, 0, L - 1)] * jnp.asarray((ii >= jj)[:, :, None, None, None], F32)
    kb = kern[:, 1][np.clip(jj - ii, 0, L - 1)] * jnp.asarray((jj >= ii)[:, :, None, None, None], F32)
    m_intra = jnp.transpose(kf + kb, (2, 0, 4, 1, 3)).reshape(C_N_GROUPS, L * C, L * C)

    def summ(d, powers):
        pr_k, pi_k = pr[powers, d], pi[powers, d]
        sr = pr_k[..., None] * bbr[d] - pi_k[..., None] * bbi[d]
        si = pr_k[..., None] * bbi[d] + pi_k[..., None] * bbr[d]
        s = jnp.concatenate([sr, si], 2)
        return jnp.transpose(s, (1, 0, 3, 2)).reshape(C_N_GROUPS, L * C, 2 * P)

    w_sum = jnp.concatenate([summ(0, np.arange(L - 1, -1, -1)), summ(1, np.arange(L))], -1)

    def outw(d, powers):
        pr_k, pi_k = pr[powers, d], pi[powers, d]
        wr = cr[d][None] * pr_k[:, :, None, :] - ci[d][None] * pi_k[:, :, None, :]
        wi = -(cr[d][None] * pi_k[:, :, None, :] + ci[d][None] * pr_k[:, :, None, :])
        w = jnp.concatenate([wr, wi], -1)
        return jnp.transpose(w, (1, 3, 0, 2)).reshape(C_N_GROUPS, 2 * P, L * C)

    w_state = jnp.concatenate([outw(0, np.arange(1, L + 1)), outw(1, np.arange(L, 0, -1))], 1)
    w_out = jnp.concatenate([m_intra, w_state], 1)

    def decay(d):
        re_ = jnp.concatenate([pr[L, d], pr[L, d]], -1)
        im_ = jnp.concatenate([-pi[L, d], pi[L, d]], -1)
        return re_, im_

    (rf, imf), (rb, imb) = decay(0), decay(1)
    dec_re = jnp.concatenate([rf, rb], -1)
    dec_im = jnp.concatenate([imf, imb], -1)
    return w_sum.astype(BF), w_out.astype(BF), dec_re, dec_im


def _s5_sum_kernel(u_ref, w_ref, s_ref):
    s_ref[...] = _dot(u_ref[...], w_ref[...])


def _s5_scan_kernel(s_ref, re_ref, im_ref, e_ref, *, nc):
    rows = re_ref.shape[0]
    ref_, imf = re_ref[:, :LANES], im_ref[:, :LANES]
    reb, imb = re_ref[:, LANES:], im_ref[:, LANES:]

    def body(t, carry):
        hf, hb = carry
        c = nc - 1 - t
        e_ref[t, :, 0:LANES] = hf
        e_ref[c, :, LANES:2 * LANES] = hb
        hf = hf * ref_ + pltpu.roll(hf, LANES // 2, 1) * imf + s_ref[t, :, 0:LANES]
        hb = hb * reb + pltpu.roll(hb, LANES // 2, 1) * imb + s_ref[c, :, LANES:2 * LANES]
        return hf, hb

    z = jnp.zeros((rows, LANES), F32)
    lax.fori_loop(0, nc, body, (z, z))


def _s5_out_kernel(u_ref, e_ref, w_ref, y_ref):
    k = u_ref.shape[1]
    y_ref[...] = _dot(u_ref[...], w_ref[0:k, :]) + _dot(e_ref[...], w_ref[k:, :])


def _s5(x, b, s, w_sum, w_out, dec_re, dec_im):
    L, G, C = C_CHUNK, C_N_GROUPS, C_GROUP
    nc = s // L
    rows = b * nc
    lc = L * C
    u = jnp.transpose(x.reshape(b, nc, L, G, C), (3, 0, 1, 2, 4)).reshape(G, rows, lc).astype(BF)
    tr = _tile(rows, 1024)
    sums = pl.pallas_call(
        _s5_sum_kernel,
        grid=(G, rows // tr),
        in_specs=[pl.BlockSpec((None, tr, lc), lambda g, i: (g, i, 0)),
                  pl.BlockSpec((None, lc, lc), lambda g, i: (g, 0, 0))],
        out_specs=pl.BlockSpec((None, tr, lc), lambda g, i: (g, i, 0)),
        out_shape=jax.ShapeDtypeStruct((G, rows, lc), F32),
        compiler_params=_params("parallel", "parallel"),
        name="s5_sum",
    )(u, w_sum)
    sums = jnp.transpose(sums.reshape(G, b, nc, lc), (2, 0, 1, 3)).reshape(nc, G * b, lc)
    rb = _tile(G * b, 16)
    re_t = jnp.repeat(dec_re, b, axis=0)
    im_t = jnp.repeat(dec_im, b, axis=0)
    ent = pl.pallas_call(
        functools.partial(_s5_scan_kernel, nc=nc),
        grid=(G * b // rb,),
        in_specs=[pl.BlockSpec((nc, rb, lc), lambda i: (0, i, 0)),
                  pl.BlockSpec((rb, lc), lambda i: (i, 0)),
                  pl.BlockSpec((rb, lc), lambda i: (i, 0))],
        out_specs=pl.BlockSpec((nc, rb, lc), lambda i: (0, i, 0)),
        out_shape=jax.ShapeDtypeStruct((nc, G * b, lc), F32),
        compiler_params=_params("parallel"),
        name="s5_scan",
    )(sums, re_t, im_t)
    ent = jnp.transpose(ent.reshape(nc, G, b, lc), (1, 2, 0, 3)).reshape(G, rows, lc).astype(BF)
    y = pl.pallas_call(
        _s5_out_kernel,
        grid=(G, rows // tr),
        in_specs=[pl.BlockSpec((None, tr, lc), lambda g, i: (g, i, 0)),
                  pl.BlockSpec((None, tr, lc), lambda g, i: (g, i, 0)),
                  pl.BlockSpec((None, 2 * lc, lc), lambda g, i: (g, 0, 0))],
        out_specs=pl.BlockSpec((None, tr, lc), lambda g, i: (g, i, 0)),
        out_shape=jax.ShapeDtypeStruct((G, rows, lc), F32),
        compiler_params=_params("parallel", "parallel"),
        name="s5_out",
    )(u, ent, w_out)
    return jnp.transpose(y.reshape(G, b, nc, L, C), (1, 2, 3, 0, 4)).reshape(b * s, D_MODEL)


_D_T = 256


def _prep_d(w_qkv, norm_gain, w_o, lambda_init):
    qk_w = D_HEADS * 2 * D_QK_DIM
    wq = _pad_heads(w_qkv[:, :qk_w] * (D_QK_DIM ** -0.5), D_HEADS, 2 * D_QK_DIM)
    wk = _pad_heads(w_qkv[:, qk_w:2 * qk_w], D_HEADS, 2 * D_QK_DIM)
    wv = _pad_heads(w_qkv[:, 2 * qk_w:], D_HEADS, D_V_DIM)
    w = jnp.concatenate([wq, wk, wv], 1).astype(BF)
    gain = jnp.concatenate([norm_gain * (1.0 - lambda_init), jnp.zeros((LANES - D_V_DIM,), F32)])[None, :]
    wo = w_o.reshape(D_HEADS, D_V_DIM, D_MODEL)
    wo = jnp.concatenate([wo, jnp.zeros_like(wo)], 1).reshape(D_HEADS * LANES, D_MODEL).astype(BF)
    return w, gain, wo


def _diff_bias(rel_bias, t):
    off = jnp.arange(-2, 3)[:, None, None] * t
    rel = off + jnp.arange(t)[None, None, :] - jnp.arange(t)[None, :, None]
    return jnp.transpose(rel_bias[_bucket(rel)], (3, 0, 1, 2)).astype(F32)


def _flash_d_kernel(q_ref, k_ref, v_ref, bias_ref, lam_ref, gain_ref, o_ref, acc_ref, *, nk, lambda_init):
    t = q_ref.shape[0]
    i = pl.program_id(2)
    qv = q_ref[...].astype(F32)
    lane = lax.broadcasted_iota(I32, (t, LANES), 1)
    q0 = jnp.where(lane < D_QK_DIM, qv, 0.0).astype(BF)
    q1 = jnp.where((lane >= D_QK_DIM) & (lane < 2 * D_QK_DIM), qv, 0.0).astype(BF)
    acc_ref[...] = jnp.zeros_like(acc_ref)

    def update(idx, s, m, l, v):
        m_new = jnp.maximum(m, jnp.max(s, -1, keepdims=True))
        a = jnp.exp(m - m_new)
        p = jnp.exp(s - m_new)
        acc_ref[idx] = a * acc_ref[idx] + _dot(p.astype(BF), v)
        return m_new, a * l + jnp.sum(p, -1, keepdims=True)

    def body(j, carry):
        m0, l0, m1, l1 = carry
        off = pl.multiple_of(j * t, t)
        k = k_ref[pl.ds(off, t), :]
        v = v_ref[pl.ds(off, t), :]
        bias = bias_ref[jnp.clip(j - i, -2, 2) + 2]
        m0, l0 = update(0, _dot_nt(q0, k) + bias, m0, l0, v)
        m1, l1 = update(1, _dot_nt(q1, k) + bias, m1, l1, v)
        return m0, l0, m1, l1

    ninf = jnp.full((t, 1), -jnp.inf, F32)
    z = jnp.zeros((t, 1), F32)
    _, l0, _, l1 = lax.fori_loop(0, nk, body, (ninf, z, ninf, z))
    lf = lam_ref[...]
    lam = (jnp.exp(jnp.sum(lf[0:1] * lf[1:2], keepdims=True))
           - jnp.exp(jnp.sum(lf[2:3] * lf[3:4], keepdims=True)) + lambda_init)
    o = acc_ref[0] / l0 - lam * (acc_ref[1] / l1)
    ms = jnp.sum(o * o, -1, keepdims=True) * (1.0 / D_V_DIM)
    o_ref[...] = (o * lax.rsqrt(ms + RMS_EPS) * gain_ref[...]).astype(BF)


def _flash_d(qkv, bias, lam, gain, lambda_init):
    b, s, _ = qkv.shape
    t = bias.shape[-1]
    return pl.pallas_call(
        functools.partial(_flash_d_kernel, nk=s // t, lambda_init=lambda_init),
        grid=(b, D_HEADS, s // t),
        in_specs=[pl.BlockSpec((None, t, LANES), lambda bi, h, i: (bi, i, h)),
                  pl.BlockSpec((None, s, LANES), lambda bi, h, i: (bi, 0, D_HEADS + h)),
                  pl.BlockSpec((None, s, LANES), lambda bi, h, i: (bi, 0, 2 * D_HEADS + h)),
                  pl.BlockSpec((None, 5, t, t), lambda bi, h, i: (h, 0, 0, 0)),
                  pl.BlockSpec((4, D_QK_DIM), lambda bi, h, i: (0, 0)),
                  pl.BlockSpec((1, LANES), lambda bi, h, i: (0, 0))],
        out_specs=pl.BlockSpec((None, t, LANES), lambda bi, h, i: (bi, i, h)),
        out_shape=jax.ShapeDtypeStruct((b, s, D_HEADS * LANES), BF),
        scratch_shapes=[pltpu.VMEM((2, t, LANES), F32)],
        compiler_params=_params("parallel", "parallel", "arbitrary"),
        name="flash_d",
    )(qkv, qkv, qkv, bias, lam, gain)


def _cross_kernel(x_ref, kv_ref, wq_ref, wo_ref, g_ref, b_ref, wr_ref, y_ref, ybf_ref, aff_ref):
    x = x_ref[...]
    tm = x.shape[0]
    q = (_dot(x.astype(BF), wq_ref[...]) * (X_HEAD_DIM ** -0.5)).astype(BF)
    outs = []
    for h in range(X_HEADS):
        sl = slice(X_HEAD_DIM * h, X_HEAD_DIM * (h + 1))
        s = _dot_nt(q[:, sl], kv_ref[:, sl])
        p = jnp.exp(s - jnp.max(s, -1, keepdims=True))
        l = jnp.sum(p, -1, keepdims=True)
        vh = kv_ref[:, D_MODEL + X_HEAD_DIM * h:D_MODEL + X_HEAD_DIM * (h + 1)]
        outs.append((_dot(p.astype(BF), vh) / l).astype(BF))
    o = jnp.concatenate(outs, axis=1)
    y = _ln_rows(ALPHA * x + _dot(o, wo_ref[...]), g_ref[...], b_ref[...])
    y_ref[...] = y
    yh = y.astype(BF)
    ybf_ref[...] = yh
    yl = (y - yh.astype(F32)).astype(BF)
    wr = wr_ref[...]
    wh = wr.astype(BF)
    wl = (wr - wh.astype(F32)).astype(BF)
    lg = _dot_nt(wh, yh) + _dot_nt(wh, yl) + _dot_nt(wl, yh)
    e = jnp.exp(lg - jnp.max(lg, 0, keepdims=True))
    aff = e / jnp.sum(e, 0, keepdims=True)
    for c in range(tm // LANES):
        aff_ref[c] = aff[:, LANES * c:LANES * (c + 1)]


def _cross(x, kv, wq, wo, g, b, wr_t, s, mem_len, tm=512):
    n = x.shape[0]
    tm = _tile(s, tm)
    per = s // tm
    fixed = lambda i: (0, 0)
    return pl.pallas_call(
        _cross_kernel,
        grid=(n // tm,),
        in_specs=[pl.BlockSpec((tm, D_MODEL), lambda i: (i, 0)),
                  pl.BlockSpec((mem_len, 2 * D_MODEL), lambda i: (i // per, 0)),
                  pl.BlockSpec((D_MODEL, D_MODEL), fixed), pl.BlockSpec((D_MODEL, D_MODEL), fixed),
                  pl.BlockSpec((1, D_MODEL), fixed), pl.BlockSpec((1, D_MODEL), fixed),
                  pl.BlockSpec((N_EXPERTS, D_MODEL), fixed)],
        out_specs=[pl.BlockSpec((tm, D_MODEL), lambda i: (i, 0)),
                   pl.BlockSpec((tm, D_MODEL), lambda i: (i, 0)),
                   pl.BlockSpec((tm // LANES, N_EXPERTS, LANES), lambda i: (i, 0, 0))],
        out_shape=[jax.ShapeDtypeStruct((n, D_MODEL), F32),
                   jax.ShapeDtypeStruct((n, D_MODEL), BF),
                   jax.ShapeDtypeStruct((n // LANES, N_EXPERTS, LANES), F32)],
        compiler_params=_params("parallel"),
        name="cross",
    )(x, kv, wq, wo, g, b, wr_t)


def _select_kernel(a_ref, pos_ref, st_ref, *, k, nbits):
    nt = a_ref.shape[0]
    shape = (nt, N_EXPERTS, LANES)
    kf = float(k)

    def keys():
        return lax.bitcast_convert_type(a_ref[...], I32)

    def count(mask):
        c = jnp.sum(jnp.where(mask, 1.0, 0.0), axis=0, keepdims=True)
        return jnp.sum(c, axis=2, keepdims=True)

    def value_step(it, thr):
        cand = thr | jnp.left_shift(jnp.int32(1), 30 - it)
        return jnp.where(count(keys() >= cand) >= kf, cand, thr)

    thr = lax.fori_loop(0, 31, value_step, jnp.zeros((1, N_EXPERTS, 1), I32))
    need = kf - count(keys() > thr)
    idx = lax.broadcasted_iota(I32, shape, 0) * LANES + lax.broadcasted_iota(I32, shape, 2)

    def index_step(it, ithr):
        cand = ithr | jnp.left_shift(jnp.int32(1), nbits - 1 - it)
        return jnp.where(count((keys() == thr) & (idx < cand)) < need, cand, ithr)

    ithr = lax.fori_loop(0, nbits, index_step, jnp.zeros((1, N_EXPERTS, 1), I32))
    thr2, ithr2 = thr[0], ithr[0]
    upper = jnp.where(lax.broadcasted_iota(I32, (LANES, LANES), 0) <= lax.broadcasted_iota(I32, (LANES, LANES), 1),
                      1.0, 0.0).astype(BF)
    lane = lax.broadcasted_iota(I32, (N_EXPERTS, LANES), 1)

    def tile_step(j, carry):
        kj = lax.bitcast_convert_type(a_ref[j], I32)
        sel = (kj > thr2) | ((kj == thr2) & (j * LANES + lane <= ithr2))
        m = jnp.where(sel, 1.0, 0.0)
        inc = _dot(m.astype(BF), upper)
        pos_ref[j] = jnp.where(sel, inc - m + carry, -1.0).astype(I32)
        st_ref[j] = jnp.broadcast_to(carry, (N_EXPERTS, LANES)).astype(I32)
        return carry + inc[:, LANES - 1:LANES]

    lax.fori_loop(0, nt, tile_step, jnp.zeros((N_EXPERTS, 1), F32))


def _select(aff3, k):
    nt = aff3.shape[0]
    nbits = max(1, int(math.ceil(math.log2(nt * LANES))))
    shp = jax.ShapeDtypeStruct(aff3.shape, I32)
    return pl.pallas_call(
        functools.partial(_select_kernel, k=k, nbits=nbits),
        out_shape=[shp, shp],
        compiler_params=pltpu.CompilerParams(vmem_limit_bytes=VMEM_LIMIT),
        name="select",
    )(aff3)


_GATHER_ROWS = LANES + BF16_ROWS


def _moe_ffn_kernel(st_ref, x_ref, pos_ref, wg_ref, wu_ref, wd_ref, ye_ref, buf_ref, *, nb, ncf, sub, ck):
    e = pl.program_id(0)
    t = pl.program_id(1)

    @pl.when(t == 0)
    def _():
        buf_ref[...] = jnp.zeros_like(buf_ref)

    @pl.when(t < nb)
    def _():
        row = lax.broadcasted_iota(I32, (_GATHER_ROWS, LANES), 0)
        for s in range(sub):
            st = st_ref[e, t * sub + s]
            base = pl.multiple_of((st // BF16_ROWS) * BF16_ROWS, BF16_ROWS)
            rel = pos_ref[s, pl.ds(e, 1), :] - base
            onehot = jnp.where(row == rel, 1.0, 0.0).astype(BF)
            c = _dot(onehot, x_ref[LANES * s:LANES * (s + 1), :])
            buf_ref[pl.ds(base, _GATHER_ROWS), :] = buf_ref[pl.ds(base, _GATHER_ROWS), :] + c.astype(BF)

    @pl.when(t >= nb)
    def _():
        c = t - nb

        @pl.when(c < ncf)
        def _():
            xe = buf_ref[pl.ds(pl.multiple_of(c * ck, ck), ck), :]
            h = (jax.nn.silu(_dot(xe, wg_ref[...])) * _dot(xe, wu_ref[...])).astype(BF)
            ye_ref[...] = _dot(h, wd_ref[...]).astype(BF)

        @pl.when(c >= ncf)
        def _():
            ye_ref[...] = jnp.zeros_like(ye_ref)


def _moe_ffn(starts, xbf, pos3, wg, wu, wd, cap, tb=1024):
    n = xbf.shape[0]
    tb = _tile(n, tb)
    nb = n // tb
    sub = tb // LANES
    ck = _tile(cap, 512)
    ncf = cap // ck
    nch = ncf + 1
    dff = wg.shape[-1]
    blk = lambda e, t, st: (jnp.minimum(t, nb - 1), 0)
    return pl.pallas_call(
        functools.partial(_moe_ffn_kernel, nb=nb, ncf=ncf, sub=sub, ck=ck),
        grid_spec=pltpu.PrefetchScalarGridSpec(
            num_scalar_prefetch=1,
            grid=(N_EXPERTS, nb + nch),
            in_specs=[pl.BlockSpec((tb, D_MODEL), blk),
                      pl.BlockSpec((sub, N_EXPERTS, LANES), lambda e, t, st: (jnp.minimum(t, nb - 1), 0, 0)),
                      pl.BlockSpec((None, D_MODEL, dff), lambda e, t, st: (e, 0, 0)),
                      pl.BlockSpec((None, D_MODEL, dff), lambda e, t, st: (e, 0, 0)),
                      pl.BlockSpec((None, dff, D_MODEL), lambda e, t, st: (e, 0, 0))],
            out_specs=pl.BlockSpec((None, ck, D_MODEL), lambda e, t, st: (e, jnp.maximum(t - nb, 0), 0)),
            scratch_shapes=[pltpu.VMEM((cap + _GATHER_ROWS, D_MODEL), BF)]),
        out_shape=jax.ShapeDtypeStruct((N_EXPERTS, nch * ck, D_MODEL), BF),
        compiler_params=_params("arbitrary", "arbitrary"),
        name="moe_ffn",
    )(starts, xbf, pos3, wg, wu, wd)


def _combine_kernel(st_ref, x_ref, aff_ref, pos_ref, g_ref, b_ref, *rest):
    ye_refs, y_ref = rest[:N_EXPERTS], rest[N_EXPERTS]
    j = pl.program_id(0)
    col = lax.broadcasted_iota(I32, (LANES, _GATHER_ROWS), 1)
    acc = jnp.zeros((LANES, D_MODEL), F32)
    for e in range(N_EXPERTS):
        base = (st_ref[e, j] // BF16_ROWS) * BF16_ROWS
        rel = pos_ref[:, e:e + 1] - base
        onehot = jnp.where(col == rel, 1.0, 0.0).astype(BF)
        acc = acc + aff_ref[:, e:e + 1] * _dot(onehot, ye_refs[e][0])
    y_ref[...] = _ln_rows(ALPHA * x_ref[...] + acc, g_ref[...], b_ref[...])


def _combine(starts, x, aff, pos, g, b, ye):
    n = x.shape[0]
    row = lambda j, st: (j, 0)
    fixed = lambda j, st: (0, 0)

    def ye_spec(e):
        return pl.BlockSpec((pl.Element(1), pl.Element(_GATHER_ROWS), pl.Element(D_MODEL)),
                            lambda j, st: (e, (st[e, j] // BF16_ROWS) * BF16_ROWS, 0))

    return pl.pallas_call(
        _combine_kernel,
        grid_spec=pltpu.PrefetchScalarGridSpec(
            num_scalar_prefetch=1,
            grid=(n // LANES,),
            in_specs=[pl.BlockSpec((LANES, D_MODEL), row),
                      pl.BlockSpec((LANES, N_EXPERTS), row),
                      pl.BlockSpec((LANES, N_EXPERTS), row),
                      pl.BlockSpec((1, D_MODEL), fixed), pl.BlockSpec((1, D_MODEL), fixed)]
            + [ye_spec(e) for e in range(N_EXPERTS)],
            out_specs=pl.BlockSpec((LANES, D_MODEL), row)),
        out_shape=jax.ShapeDtypeStruct((n, D_MODEL), F32),
        compiler_params=_params("arbitrary"),
        name="combine",
    )(starts, x, aff, pos, g, b, *([ye] * N_EXPERTS))


def _moe(x, xbf, aff3, wg, wu, wd, g, b):
    n = x.shape[0]
    cap = EC_CAPACITY * n // N_EXPERTS
    pos3, st3 = _select(aff3, cap)
    starts = jnp.transpose(st3[:, :, 0])
    ye = _moe_ffn(starts, xbf, pos3, wg, wu, wd, cap)
    aff = jnp.transpose(aff3, (0, 2, 1)).reshape(n, N_EXPERTS)
    pos = jnp.transpose(pos3, (0, 2, 1)).reshape(n, N_EXPERTS)
    return _combine(starts, x, aff, pos, g, b, ye)


def _prep_weights(p):
    w = {}
    w['a'] = [_prep_a(p['a_w_qkv'][j], p['a_q_gain'][j], p['a_k_gain'][j], p['a_w_o'][j])
              for j in range(p['a_w_qkv'].shape[0])]
    w['b'] = [_prep_b(p['b_w_qkv'][j], p['b_w_o'][j]) for j in range(p['b_w_qkv'].shape[0])]
    w['c'] = [_prep_c(p['c_lam_re'][j], p['c_lam_im'][j], p['c_log_dt'][j], p['c_b_re'][j], p['c_b_im'][j],
                      p['c_c_re'][j], p['c_c_im'][j]) + (p['c_d'][j][None, :], p['c_w_glu'][j].astype(BF))
              for j in range(p['c_lam_re'].shape[0])]
    w['d'] = []
    for j in range(p['d_w_qkv'].shape[0]):
        layer = N_MIXERS * j + 3
        lambda_init = 0.8 - 0.6 * math.exp(-0.3 * layer)
        w['d'].append(_prep_d(p['d_w_qkv'][j], p['d_norm_gain'][j], p['d_w_o'][j], lambda_init)
                      + (p['d_lam'][j].astype(F32), lambda_init))
    w['diff_bias'] = _diff_bias(p['rel_bias'], _D_T)
    w['x_w_q'] = p['x_w_q'].astype(BF)
    w['x_w_kv'] = p['x_w_kv'].astype(BF)
    w['x_w_o'] = p['x_w_o'].astype(BF)
    w['router_t'] = jnp.transpose(p['moe_w_router'], (0, 2, 1)).astype(F32)
    w['moe_w_gate'] = p['moe_w_gate'].astype(BF)
    w['moe_w_up'] = p['moe_w_up'].astype(BF)
    w['moe_w_down'] = p['moe_w_down'].astype(BF)
    return w


def _trunk(x, mem, p, w):
    b, s, _ = x.shape
    n = b * s
    mem_len = mem.shape[1]
    x = x.reshape(n, D_MODEL)
    mem2 = mem.reshape(b * mem_len, D_MODEL)
    ln_g, ln_b = p['ln_g'], p['ln_b']
    for i in range(DEPTH):
        m, j = i % N_MIXERS, i // N_MIXERS
        g0, b0 = ln_g[i, 0][None, :], ln_b[i, 0][None, :]
        if m == 0:
            wa, gains, wo = w['a'][j]
            qkv = _proj_a(x, wa, _rope_table(s), gains, s)
            o = _flash_a(qkv.reshape(b, s, _A_COLS))
            x = _post(x, o.reshape(n, A_HEADS * LANES), wo, g0, b0)
        elif m == 1:
            wb, wo = w['b'][j]
            qkv = _proj(x, wb).reshape(b, s, _B_TILES * LANES)
            os_, ls_ = [], []
            for g in range(len(B_PATTERNS)):
                tm = min(LANES, s // B_PATTERNS[g][1])
                o, lse = _band_attention(qkv, _band_bias(p['rel_bias'], g, tm), g, tm)
                os_.append(o)
                ls_.append(lse)
            x = _post_b(x, os_, ls_, wo, g0, b0)
        elif m == 2:
            w_sum, w_out, dec_re, dec_im, dskip, wglu = w['c'][j]
            ys = _s5(x, b, s, w_sum, w_out, dec_re, dec_im)
            x = _post_c(x, ys, dskip, wglu, g0, b0)
        else:
            wd, gain, wo, lam, lambda_init = w['d'][j]
            qkv = _proj(x, wd).reshape(b, s, 3 * D_HEADS * LANES)
            o = _flash_d(qkv, w['diff_bias'], lam, gain, lambda_init)
            x = _post(x, o.reshape(n, D_HEADS * LANES), wo, g0, b0)
        kv = _proj(mem2, w['x_w_kv'][i], tm=mem_len)
        x, xbf, aff3 = _cross(x, kv, w['x_w_q'][i], w['x_w_o'][i], ln_g[i, 1][None, :], ln_b[i, 1][None, :],
                              w['router_t'][i], s, mem_len)
        x = _moe(x, xbf, aff3, w['moe_w_gate'][i], w['moe_w_up'][i], w['moe_w_down'][i],
                 ln_g[i, 2][None, :], ln_b[i, 2][None, :])
    return x.reshape(b, s, D_MODEL)


def kernel(x_prompt, x_sample, mem_prompt, mem_sample, rel_bias, ln_g, ln_b, a_w_qkv, a_q_gain, a_k_gain, a_w_o, b_w_qkv, b_w_o, c_lam_re, c_lam_im, c_log_dt, c_b_re, c_b_im, c_c_re, c_c_im, c_d, c_w_glu, d_w_qkv, d_lam, d_norm_gain, d_w_o, x_w_q, x_w_kv, x_w_o, moe_w_router, moe_w_gate, moe_w_up, moe_w_down):
    p = dict(rel_bias=rel_bias, ln_g=ln_g, ln_b=ln_b,
             a_w_qkv=a_w_qkv, a_q_gain=a_q_gain, a_k_gain=a_k_gain, a_w_o=a_w_o,
             b_w_qkv=b_w_qkv, b_w_o=b_w_o,
             c_lam_re=c_lam_re, c_lam_im=c_lam_im, c_log_dt=c_log_dt, c_b_re=c_b_re, c_b_im=c_b_im,
             c_c_re=c_c_re, c_c_im=c_c_im, c_d=c_d, c_w_glu=c_w_glu,
             d_w_qkv=d_w_qkv, d_lam=d_lam, d_norm_gain=d_norm_gain, d_w_o=d_w_o,
             x_w_q=x_w_q, x_w_kv=x_w_kv, x_w_o=x_w_o,
             moe_w_router=moe_w_router, moe_w_gate=moe_w_gate, moe_w_up=moe_w_up, moe_w_down=moe_w_down)
    w = _prep_weights(p)
    return (_trunk(x_prompt, mem_prompt, p, w), _trunk(x_sample, mem_sample, p, w))
```

```python
import functools
import math

import numpy as np
import jax
import jax.numpy as jnp
from jax import lax
from jax.experimental import pallas as pl
from jax.experimental.pallas import tpu as pltpu

F32 = jnp.float32
BF = jnp.bfloat16
I32 = jnp.int32

D_MODEL = 1024
DEPTH = 4
GRID_W = 64
N_MIXERS = 4
LN_EPS = 1e-5
RMS_EPS = 1e-6
ALPHA = (2.0 * DEPTH) ** 0.25

A_HEADS = 16
A_KV_HEADS = 4
A_HEAD_DIM = 64
ROPE_BASE = 10000.0

B_PATTERNS = ((128, 1), (512, 4), (2048, 16))
B_HEADS_PER_GROUP = 4
B_HEAD_DIM = 64

C_GROUP = 16
C_N_GROUPS = D_MODEL // C_GROUP
C_STATE = 64
C_CHUNK = 16

D_HEADS = 12
D_QK_DIM = 32
D_V_DIM = 64

REL_BUCKETS = 32
REL_MAX_DIST = 128

X_HEADS = 4
X_HEAD_DIM = D_MODEL // X_HEADS

N_EXPERTS = 16
EC_CAPACITY = 2

LANES = 128
BF16_ROWS = 16
VMEM_LIMIT = 56 * 1024 * 1024
NEG = -1e30
LOG2E = math.log2(math.e)


def _params(*sem):
    return pltpu.CompilerParams(dimension_semantics=sem, vmem_limit_bytes=VMEM_LIMIT)


def _tile(n, pref):
    t = min(n, pref)
    assert n % t == 0, (n, pref)
    return t


def _ln_rows(v, g, b):
    mu = jnp.mean(v, -1, keepdims=True)
    c = v - mu
    var = jnp.mean(c * c, -1, keepdims=True)
    return c * lax.rsqrt(var + LN_EPS) * g + b


def _dot_nt(a, b):
    return lax.dot_general(a, b, (((1,), (1,)), ((), ())), preferred_element_type=F32)


def _dot(a, b):
    return jnp.dot(a, b, preferred_element_type=F32)


def _proj_kernel(x_ref, w_ref, o_ref):
    o_ref[...] = _dot(x_ref[...].astype(BF), w_ref[...]).astype(o_ref.dtype)


def _proj(x, w, tm=512):
    n, k = x.shape
    m = w.shape[1]
    tm = _tile(n, tm)
    return pl.pallas_call(
        _proj_kernel,
        grid=(n // tm,),
        in_specs=[pl.BlockSpec((tm, k), lambda i: (i, 0)),
                  pl.BlockSpec((k, m), lambda i: (0, 0))],
        out_specs=pl.BlockSpec((tm, m), lambda i: (i, 0)),
        out_shape=jax.ShapeDtypeStruct((n, m), BF),
        compiler_params=_params("parallel"),
        name="proj",
    )(x, w)


def _post_kernel(x_ref, o_ref, w_ref, g_ref, b_ref, y_ref):
    h = _dot(o_ref[...], w_ref[...])
    y_ref[...] = _ln_rows(ALPHA * x_ref[...] + h, g_ref[...], b_ref[...])


def _post(x, o, w, g, b, tm=512):
    n = x.shape[0]
    ko = o.shape[1]
    tm = _tile(n, tm)
    return pl.pallas_call(
        _post_kernel,
        grid=(n // tm,),
        in_specs=[pl.BlockSpec((tm, D_MODEL), lambda i: (i, 0)),
                  pl.BlockSpec((tm, ko), lambda i: (i, 0)),
                  pl.BlockSpec((ko, D_MODEL), lambda i: (0, 0)),
                  pl.BlockSpec((1, D_MODEL), lambda i: (0, 0)),
                  pl.BlockSpec((1, D_MODEL), lambda i: (0, 0))],
        out_specs=pl.BlockSpec((tm, D_MODEL), lambda i: (i, 0)),
        out_shape=jax.ShapeDtypeStruct((n, D_MODEL), F32),
        compiler_params=_params("parallel"),
        name="post",
    )(x, o, w, g, b)


def _post_b_kernel(x_ref, o0_ref, o1_ref, o2_ref, l0_ref, l1_ref, l2_ref, w_ref, g_ref, b_ref, y_ref):
    l0, l1, l2 = l0_ref[...], l1_ref[...], l2_ref[...]
    m = jnp.maximum(jnp.maximum(l0, l1), l2)
    e0, e1, e2 = jnp.exp(l0 - m), jnp.exp(l1 - m), jnp.exp(l2 - m)
    inv = 1.0 / (e0 + e1 + e2)
    gw = B_HEADS_PER_GROUP * LANES
    h = _dot((o0_ref[...].astype(F32) * (e0 * inv)).astype(BF), w_ref[0:gw, :])
    h = h + _dot((o1_ref[...].astype(F32) * (e1 * inv)).astype(BF), w_ref[gw:2 * gw, :])
    h = h + _dot((o2_ref[...].astype(F32) * (e2 * inv)).astype(BF), w_ref[2 * gw:3 * gw, :])
    y_ref[...] = _ln_rows(ALPHA * x_ref[...] + h, g_ref[...], b_ref[...])


def _post_b(x, os_, ls_, w, g, b, tm=512):
    n = x.shape[0]
    tm = _tile(n, tm)
    gw = B_HEADS_PER_GROUP * LANES
    row = lambda i: (i, 0)
    fixed = lambda i: (0, 0)
    return pl.pallas_call(
        _post_b_kernel,
        grid=(n // tm,),
        in_specs=[pl.BlockSpec((tm, D_MODEL), row)] + [pl.BlockSpec((tm, gw), row)] * 6
        + [pl.BlockSpec((3 * gw, D_MODEL), fixed), pl.BlockSpec((1, D_MODEL), fixed),
           pl.BlockSpec((1, D_MODEL), fixed)],
        out_specs=pl.BlockSpec((tm, D_MODEL), row),
        out_shape=jax.ShapeDtypeStruct((n, D_MODEL), F32),
        compiler_params=_params("parallel"),
        name="post_b",
    )(x, *os_, *ls_, w, g, b)


def _post_c_kernel(x_ref, ys_ref, d_ref, w_ref, g_ref, b_ref, y_ref):
    x = x_ref[...]
    z = jax.nn.gelu(ys_ref[...] + d_ref[...] * x).astype(BF)
    h = _dot(z, w_ref[...])
    hh = h[:, :D_MODEL] * jax.nn.sigmoid(h[:, D_MODEL:])
    y_ref[...] = _ln_rows(ALPHA * x + hh, g_ref[...], b_ref[...])


def _post_c(x, ys, d, w, g, b, tm=512):
    n = x.shape[0]
    tm = _tile(n, tm)
    row = lambda i: (i, 0)
    fixed = lambda i: (0, 0)
    return pl.pallas_call(
        _post_c_kernel,
        grid=(n // tm,),
        in_specs=[pl.BlockSpec((tm, D_MODEL), row), pl.BlockSpec((tm, D_MODEL), row),
                  pl.BlockSpec((1, D_MODEL), fixed), pl.BlockSpec((D_MODEL, 2 * D_MODEL), fixed),
                  pl.BlockSpec((1, D_MODEL), fixed), pl.BlockSpec((1, D_MODEL), fixed)],
        out_specs=pl.BlockSpec((tm, D_MODEL), row),
        out_shape=jax.ShapeDtypeStruct((n, D_MODEL), F32),
        compiler_params=_params("parallel"),
        name="post_c",
    )(x, ys, d, w, g, b)


_A_QK_TILES = A_HEADS + A_KV_HEADS
_A_COLS = (_A_QK_TILES + A_KV_HEADS) * LANES


def _rope_partner():
    d = np.arange(A_HEAD_DIM)
    e = d % (A_HEAD_DIM // 2)
    lo = e < A_HEAD_DIM // 4
    return np.where(lo, d + A_HEAD_DIM // 4, d - A_HEAD_DIM // 4), np.where(lo, -1.0, 1.0).astype(np.float32)


def _prep_a(w_qkv, q_gain, k_gain, w_o):
    partner, sign = _rope_partner()
    nqk = _A_QK_TILES * A_HEAD_DIM
    wqk = w_qkv[:, :nqk].reshape(D_MODEL, _A_QK_TILES, A_HEAD_DIM)
    wsw = wqk[:, :, partner] * sign
    wqk = jnp.concatenate([wqk, wsw], -1).reshape(D_MODEL, _A_QK_TILES * LANES)
    wv = w_qkv[:, nqk:].reshape(D_MODEL, A_KV_HEADS, A_HEAD_DIM)
    wv = jnp.concatenate([wv, jnp.zeros_like(wv)], -1).reshape(D_MODEL, A_KV_HEADS * LANES)
    w = jnp.concatenate([wqk, wv], 1).astype(BF)
    gq = jnp.concatenate([q_gain, q_gain[partner]]) * (A_HEAD_DIM ** -0.5 * 0.5 * LOG2E)
    gk = jnp.concatenate([k_gain, k_gain[partner]])
    gains = jnp.stack([gq, gk], 0)
    wo = w_o.reshape(A_HEADS, A_HEAD_DIM, D_MODEL)
    wo = jnp.concatenate([wo, jnp.zeros_like(wo)], 1).reshape(A_HEADS * LANES, D_MODEL).astype(BF)
    return w, gains, wo


def _rope_table(s):
    pos = jnp.arange(s)
    rows, cols = (pos // GRID_W).astype(F32), (pos % GRID_W).astype(F32)
    half = A_HEAD_DIM // 2
    freqs = ROPE_BASE ** (-jnp.arange(0, half, 2, dtype=F32) / half)
    ang_r = rows[:, None] * freqs
    ang_c = cols[:, None] * freqs
    ang = jnp.concatenate([ang_r, ang_r, ang_c, ang_c], -1)
    return jnp.concatenate([jnp.cos(ang), jnp.sin(ang)], -1)


def _proj_a_kernel(x_ref, w_ref, cs_ref, g_ref, o_ref):
    y = _dot(x_ref[...].astype(BF), w_ref[...])
    cs = cs_ref[...]
    gq = cs * g_ref[0:1, :]
    gk = cs * g_ref[1:2, :]
    for h in range(_A_QK_TILES):
        t = y[:, LANES * h:LANES * (h + 1)]
        r = lax.rsqrt(jnp.sum(t * t, -1, keepdims=True) * (1.0 / LANES) + RMS_EPS)
        e = t * r * (gq if h < A_HEADS else gk)
        o_ref[:, LANES * h:LANES * (h + 1)] = (e + pltpu.roll(e, LANES // 2, 1)).astype(BF)
    o_ref[:, _A_QK_TILES * LANES:] = y[:, _A_QK_TILES * LANES:].astype(BF)


def _proj_a(x, w, cs, gains, s, tm=512):
    n = x.shape[0]
    tm = _tile(s, tm)
    per = s // tm
    return pl.pallas_call(
        _proj_a_kernel,
        grid=(n // tm,),
        in_specs=[pl.BlockSpec((tm, D_MODEL), lambda i: (i, 0)),
                  pl.BlockSpec((D_MODEL, _A_COLS), lambda i: (0, 0)),
                  pl.BlockSpec((tm, LANES), lambda i: (i % per, 0)),
                  pl.BlockSpec((2, LANES), lambda i: (0, 0))],
        out_specs=pl.BlockSpec((tm, _A_COLS), lambda i: (i, 0)),
        out_shape=jax.ShapeDtypeStruct((n, _A_COLS), BF),
        compiler_params=_params("parallel"),
        name="proj_a",
    )(x, w, cs, gains)


def _softmax_step(s, v, m_ref, l_ref, acc_ref):
    tk = s.shape[1]
    m_old = m_ref[...]
    m_new = jnp.maximum(m_old, jnp.max(s, -1, keepdims=True))
    a = jnp.exp2(m_old - m_new)
    psum = None
    chunks = []
    for c in range(tk // LANES):
        pc = jnp.exp2(s[:, LANES * c:LANES * (c + 1)] - m_new)
        psum = pc if psum is None else psum + pc
        chunks.append(pc.astype(BF))
    p = jnp.concatenate(chunks, axis=1)
    l_ref[...] = a * l_ref[...] + psum
    acc_ref[...] = a * acc_ref[...] + _dot(p, v)
    m_ref[...] = m_new


def _softmax_init(m_ref, l_ref, acc_ref):
    m_ref[...] = jnp.full(m_ref.shape, -jnp.inf, F32)
    l_ref[...] = jnp.zeros_like(l_ref)
    acc_ref[...] = jnp.zeros_like(acc_ref)


def _flash_a_kernel(q_ref, k_ref, v_ref, o_ref, m_ref, l_ref, acc_ref, *, tk, nk):
    tq = q_ref.shape[0]
    rep = A_HEADS // A_KV_HEADS
    q = jnp.concatenate([q_ref[:, LANES * r:LANES * (r + 1)] for r in range(rep)], axis=0)
    _softmax_init(m_ref, l_ref, acc_ref)

    def body(j, carry):
        off = pl.multiple_of(j * tk, tk)
        _softmax_step(_dot_nt(q, k_ref[pl.ds(off, tk), :]), v_ref[pl.ds(off, tk), :], m_ref, l_ref, acc_ref)
        return carry

    lax.fori_loop(0, nk, body, 0)
    o = acc_ref[...] / jnp.sum(l_ref[...], -1, keepdims=True)
    for r in range(rep):
        o_ref[:, LANES * r:LANES * (r + 1)] = o[r * tq:(r + 1) * tq].astype(BF)


def _flash_a(qkv, tq=256, tk=1024):
    b, s, _ = qkv.shape
    tq, tk = _tile(s, tq), _tile(s, tk)
    rep = A_HEADS // A_KV_HEADS
    gw = rep * LANES
    return pl.pallas_call(
        functools.partial(_flash_a_kernel, tk=tk, nk=s // tk),
        grid=(b, A_KV_HEADS, s // tq),
        in_specs=[pl.BlockSpec((None, tq, gw), lambda bi, g, i: (bi, i, g)),
                  pl.BlockSpec((None, s, LANES), lambda bi, g, i: (bi, 0, A_HEADS + g)),
                  pl.BlockSpec((None, s, LANES), lambda bi, g, i: (bi, 0, _A_QK_TILES + g))],
        out_specs=pl.BlockSpec((None, tq, gw), lambda bi, g, i: (bi, i, g)),
        out_shape=jax.ShapeDtypeStruct((b, s, A_HEADS * LANES), BF),
        scratch_shapes=[pltpu.VMEM((rep * tq, LANES), F32)] * 3,
        compiler_params=_params("parallel", "parallel", "arbitrary"),
        name="flash_a",
    )(qkv, qkv, qkv)


def _bucket(rel):
    half = REL_BUCKETS // 2
    max_exact = half // 2
    n = jnp.abs(rel)
    large = max_exact + (jnp.log(jnp.maximum(n, 1).astype(F32) / max_exact)
                         / math.log(REL_MAX_DIST / max_exact) * (half - max_exact)).astype(I32)
    large = jnp.minimum(large, half - 1)
    return jnp.where(rel > 0, half, 0) + jnp.where(n < max_exact, n, large)


_B_TILES = 3 * len(B_PATTERNS) * B_HEADS_PER_GROUP
_B_GW = B_HEADS_PER_GROUP * LANES


def _pad_heads(w, heads, dim):
    w = w.reshape(w.shape[0], heads, dim)
    return jnp.concatenate([w, jnp.zeros((w.shape[0], heads, LANES - dim), w.dtype)], -1).reshape(
        w.shape[0], heads * LANES)


def _prep_b(w_qkv, w_o):
    nh = len(B_PATTERNS) * B_HEADS_PER_GROUP
    c = nh * B_HEAD_DIM
    wq = _pad_heads(w_qkv[:, :c] * (B_HEAD_DIM ** -0.5), nh, B_HEAD_DIM)
    wk = _pad_heads(w_qkv[:, c:2 * c], nh, B_HEAD_DIM)
    wv = _pad_heads(w_qkv[:, 2 * c:], nh, B_HEAD_DIM)
    w = jnp.concatenate([wq, wk, wv], 1).astype(BF)
    wo = w_o.reshape(nh, B_HEAD_DIM, D_MODEL)
    wo = jnp.concatenate([wo, jnp.zeros_like(wo)], 1).reshape(nh * LANES, D_MODEL).astype(BF)
    return w, wo


def _band_bias(rel_bias, g, tm):
    window, dil = B_PATTERNS[g]
    n_side = window // (2 * dil)
    j = jnp.arange(3 * tm)[None, :] - tm - jnp.arange(tm)[:, None]
    inband = jnp.abs(j) <= n_side
    bias = rel_bias[_bucket(dil * j)][:, :, g * B_HEADS_PER_GROUP:(g + 1) * B_HEADS_PER_GROUP]
    bias = jnp.where(inband[:, :, None], bias, NEG)
    return jnp.transpose(bias, (2, 0, 1)).astype(F32)


def _band_kernel(q_ref, kp_ref, kc_ref, kn_ref, vp_ref, vc_ref, vn_ref, bias_ref, o_ref, lse_ref, *, length):
    tm = q_ref.shape[0]
    mi = pl.program_id(2)
    key_pos = (mi - 1) * tm + lax.broadcasted_iota(I32, (1, 3 * tm), 1)
    valid = (key_pos >= 0) & (key_pos < length)
    for h in range(B_HEADS_PER_GROUP):
        sl = slice(LANES * h, LANES * (h + 1))
        k = jnp.concatenate([kp_ref[:, sl], kc_ref[:, sl], kn_ref[:, sl]], axis=0)
        v = jnp.concatenate([vp_ref[:, sl], vc_ref[:, sl], vn_ref[:, sl]], axis=0)
        s = _dot_nt(q_ref[:, sl], k) + bias_ref[h]
        s = jnp.where(valid, s, NEG)
        m = jnp.max(s, -1, keepdims=True)
        p = jnp.exp(s - m)
        l = jnp.sum(p, -1, keepdims=True)
        o_ref[:, sl] = (_dot(p.astype(BF), v) / l).astype(BF)
        lse_ref[:, sl] = jnp.broadcast_to(m + jnp.log(l), (tm, LANES))


def _band_attention(qkv, bias, g, tm):
    b, s, c = qkv.shape
    dil = B_PATTERNS[g][1]
    length = s // dil
    nblk = c // _B_GW
    view = qkv.reshape(b, length, dil * c)
    nt = length // tm
    ng = len(B_PATTERNS)

    def spec(which, shift):
        def imap(bi, r, mi):
            return (bi, jnp.clip(mi + shift, 0, nt - 1), r * nblk + which * ng + g)
        return pl.BlockSpec((None, tm, _B_GW), imap)

    out_spec = pl.BlockSpec((None, tm, _B_GW), lambda bi, r, mi: (bi, mi, r))
    o, lse = pl.pallas_call(
        functools.partial(_band_kernel, length=length),
        grid=(b, dil, nt),
        in_specs=[spec(0, 0), spec(1, -1), spec(1, 0), spec(1, 1), spec(2, -1), spec(2, 0), spec(2, 1),
                  pl.BlockSpec((B_HEADS_PER_GROUP, tm, 3 * tm), lambda bi, r, mi: (0, 0, 0))],
        out_specs=[out_spec, out_spec],
        out_shape=[jax.ShapeDtypeStruct((b, length, dil * _B_GW), BF),
                   jax.ShapeDtypeStruct((b, length, dil * _B_GW), F32)],
        compiler_params=_params("parallel", "parallel", "arbitrary"),
        name="band_%d" % g,
    )(view, view, view, view, view, view, view, bias)
    return o.reshape(b * s, _B_GW), lse.reshape(b * s, _B_GW)


def _prep_c(lam_re, lam_im, log_dt, b_re, b_im, c_re, c_im):
    hp = lax.Precision.HIGHEST
    L, P, C = C_CHUNK, C_STATE, C_GROUP
    lr, li = lam_re.astype(F32), lam_im.astype(F32)
    dt = jnp.exp(log_dt.astype(F32))[..., None]
    mag = jnp.exp(lr * dt)
    ar, ai = mag * jnp.cos(li * dt), mag * jnp.sin(li * dt)
    den = lr * lr + li * li
    zr = ((ar - 1.0) * lr + ai * li) / den
    zi = (ai * lr - (ar - 1.0) * li) / den
    br, bi = b_re.astype(F32), b_im.astype(F32)
    bbr = zr[..., None] * br - zi[..., None] * bi
    bbi = zr[..., None] * bi + zi[..., None] * br
    cr, ci = c_re.astype(F32), c_im.astype(F32)
    prs, pis = [jnp.ones_like(ar)], [jnp.zeros_like(ai)]
    for _ in range(L):
        pr_, pi_ = prs[-1], pis[-1]
        prs.append(ar * pr_ - ai * pi_)
        pis.append(ar * pi_ + ai * pr_)
    pr, pi = jnp.stack(prs, 0), jnp.stack(pis, 0)

    def lag(pr_k, pi_k):
        tr = pr_k[..., None] * bbr - pi_k[..., None] * bbi
        ti = pr_k[..., None] * bbi + pi_k[..., None] * bbr
        return (jnp.einsum('dgop,kdgpi->kdgoi', cr, tr, precision=hp)
                - jnp.einsum('dgop,kdgpi->kdgoi', ci, ti, precision=hp))

    kern = lag(pr[:L], pi[:L])
    jj = np.arange(L)[:, None]
    ii = np.arange(L)[None, :]
    kf = kern[:, 0][np.clip(ii - jj, 0, L - 1)] * jnp.asarray((ii >= jj)[:, :, None, None, None], F32)
    kb = kern[:, 1][np.clip(jj - ii, 0, L - 1)] * jnp.asarray((jj >= ii)[:, :, None, None, None], F32)
    m_intra = jnp.transpose(kf + kb, (2, 0, 4, 1, 3)).reshape(C_N_GROUPS, L * C, L * C)

    def summ(d, powers):
        pr_k, pi_k = pr[powers, d], pi[powers, d]
        sr = pr_k[..., None] * bbr[d] - pi_k[..., None] * bbi[d]
        si = pr_k[..., None] * bbi[d] + pi_k[..., None] * bbr[d]
        s = jnp.concatenate([sr, si], 2)
        return jnp.transpose(s, (1, 0, 3, 2)).reshape(C_N_GROUPS, L * C, 2 * P)

    w_sum = jnp.concatenate([summ(0, np.arange(L - 1, -1, -1)), summ(1, np.arange(L))], -1)

    def outw(d, powers):
        pr_k, pi_k = pr[powers, d], pi[powers, d]
        wr = cr[d][None] * pr_k[:, :, None, :] - ci[d][None] * pi_k[:, :, None, :]
        wi = -(cr[d][None] * pi_k[:, :, None, :] + ci[d][None] * pr_k[:, :, None, :])
        w = jnp.concatenate([wr, wi], -1)
        return jnp.transpose(w, (1, 3, 0, 2)).reshape(C_N_GROUPS, 2 * P, L * C)

    w_state = jnp.concatenate([outw(0, np.arange(1, L + 1)), outw(1, np.arange(L, 0, -1))], 1)
    w_out = jnp.concatenate([m_intra, w_state], 1)

    def decay(d):
        re_ = jnp.concatenate([pr[L, d], pr[L, d]], -1)
        im_ = jnp.concatenate([-pi[L, d], pi[L, d]], -1)
        return re_, im_

    (rf, imf), (rb, imb) = decay(0), decay(1)
    dec_re = jnp.concatenate([rf, rb], -1)
    dec_im = jnp.concatenate([imf, imb], -1)
    return w_sum.astype(BF), w_out.astype(BF), dec_re, dec_im


def _s5_sum_kernel(u_ref, w_ref, s_ref):
    s_ref[...] = _dot(u_ref[...], w_ref[...])


def _s5_scan_kernel(s_ref, re_ref, im_ref, e_ref, *, nc):
    rows = re_ref.shape[0]
    ref_, imf = re_ref[:, :LANES], im_ref[:, :LANES]
    reb, imb = re_ref[:, LANES:], im_ref[:, LANES:]

    def body(t, carry):
        hf, hb = carry
        c = nc - 1 - t
        e_ref[t, :, 0:LANES] = hf
        e_ref[c, :, LANES:2 * LANES] = hb
        hf = hf * ref_ + pltpu.roll(hf, LANES // 2, 1) * imf + s_ref[t, :, 0:LANES]
        hb = hb * reb + pltpu.roll(hb, LANES // 2, 1) * imb + s_ref[c, :, LANES:2 * LANES]
        return hf, hb

    z = jnp.zeros((rows, LANES), F32)
    lax.fori_loop(0, nc, body, (z, z))


def _s5_out_kernel(u_ref, e_ref, w_ref, y_ref):
    k = u_ref.shape[1]
    y_ref[...] = _dot(u_ref[...], w_ref[0:k, :]) + _dot(e_ref[...], w_ref[k:, :])


def _s5(x, b, s, w_sum, w_out, dec_re, dec_im):
    L, G, C = C_CHUNK, C_N_GROUPS, C_GROUP
    nc = s // L
    rows = b * nc
    lc = L * C
    u = jnp.transpose(x.reshape(b, nc, L, G, C), (3, 0, 1, 2, 4)).reshape(G, rows, lc).astype(BF)
    tr = _tile(rows, 1024)
    sums = pl.pallas_call(
        _s5_sum_kernel,
        grid=(G, rows // tr),
        in_specs=[pl.BlockSpec((None, tr, lc), lambda g, i: (g, i, 0)),
                  pl.BlockSpec((None, lc, lc), lambda g, i: (g, 0, 0))],
        out_specs=pl.BlockSpec((None, tr, lc), lambda g, i: (g, i, 0)),
        out_shape=jax.ShapeDtypeStruct((G, rows, lc), F32),
        compiler_params=_params("parallel", "parallel"),
        name="s5_sum",
    )(u, w_sum)
    sums = jnp.transpose(sums.reshape(G, b, nc, lc), (2, 0, 1, 3)).reshape(nc, G * b, lc)
    rb = _tile(G * b, 16)
    re_t = jnp.repeat(dec_re, b, axis=0)
    im_t = jnp.repeat(dec_im, b, axis=0)
    ent = pl.pallas_call(
        functools.partial(_s5_scan_kernel, nc=nc),
        grid=(G * b // rb,),
        in_specs=[pl.BlockSpec((nc, rb, lc), lambda i: (0, i, 0)),
                  pl.BlockSpec((rb, lc), lambda i: (i, 0)),
                  pl.BlockSpec((rb, lc), lambda i: (i, 0))],
        out_specs=pl.BlockSpec((nc, rb, lc), lambda i: (0, i, 0)),
        out_shape=jax.ShapeDtypeStruct((nc, G * b, lc), F32),
        compiler_params=_params("parallel"),
        name="s5_scan",
    )(sums, re_t, im_t)
    ent = jnp.transpose(ent.reshape(nc, G, b, lc), (1, 2, 0, 3)).reshape(G, rows, lc).astype(BF)
    y = pl.pallas_call(
        _s5_out_kernel,
        grid=(G, rows // tr),
        in_specs=[pl.BlockSpec((None, tr, lc), lambda g, i: (g, i, 0)),
                  pl.BlockSpec((None, tr, lc), lambda g, i: (g, i, 0)),
                  pl.BlockSpec((None, 2 * lc, lc), lambda g, i: (g, 0, 0))],
        out_specs=pl.BlockSpec((None, tr, lc), lambda g, i: (g, i, 0)),
        out_shape=jax.ShapeDtypeStruct((G, rows, lc), F32),
        compiler_params=_params("parallel", "parallel"),
        name="s5_out",
    )(u, ent, w_out)
    return jnp.transpose(y.reshape(G, b, nc, L, C), (1, 2, 3, 0, 4)).reshape(b * s, D_MODEL)


_D_T = 512


def _prep_d(w_qkv, norm_gain, w_o, lambda_init):
    qk_w = D_HEADS * 2 * D_QK_DIM
    wq = _pad_heads(w_qkv[:, :qk_w] * (D_QK_DIM ** -0.5 * LOG2E), D_HEADS, 2 * D_QK_DIM)
    wk = _pad_heads(w_qkv[:, qk_w:2 * qk_w], D_HEADS, 2 * D_QK_DIM)
    wv = _pad_heads(w_qkv[:, 2 * qk_w:], D_HEADS, D_V_DIM)
    w = jnp.concatenate([wq, wk, wv], 1).astype(BF)
    gain = jnp.concatenate([norm_gain * (1.0 - lambda_init), jnp.zeros((LANES - D_V_DIM,), F32)])[None, :]
    wo = w_o.reshape(D_HEADS, D_V_DIM, D_MODEL)
    wo = jnp.concatenate([wo, jnp.zeros_like(wo)], 1).reshape(D_HEADS * LANES, D_MODEL).astype(BF)
    return w, gain, wo


def _diff_bias(rel_bias, t):
    rel0 = (jnp.arange(2 * t) + t) % (2 * t) - t
    rel = jnp.arange(-2, 3)[:, None] * t + rel0[None, :]
    vec = jnp.transpose(rel_bias[_bucket(rel)] * LOG2E, (2, 0, 1)).astype(F32)
    flat = jnp.tile(vec, (1, 1, t))[..., :t * (2 * t - 1)]
    return flat.reshape(vec.shape[0], 5, t, 2 * t - 1)[..., :t]


def _flash_d_kernel(q_ref, k_ref, v_ref, bias_ref, lam_ref, gain_ref, o_ref, m_ref, l_ref, acc_ref, *, nk,
                    lambda_init):
    t = q_ref.shape[0]
    i = pl.program_id(2)
    qv = q_ref[...].astype(F32)
    lane = lax.broadcasted_iota(I32, (t, LANES), 1)
    q0 = jnp.where(lane < D_QK_DIM, qv, 0.0).astype(BF)
    q1 = jnp.where((lane >= D_QK_DIM) & (lane < 2 * D_QK_DIM), qv, 0.0).astype(BF)
    q = jnp.concatenate([q0, q1], axis=0)
    _softmax_init(m_ref, l_ref, acc_ref)

    def body(j, carry):
        off = pl.multiple_of(j * t, t)
        bias = bias_ref[jnp.clip(j - i, -2, 2) + 2]
        s = _dot_nt(q, k_ref[pl.ds(off, t), :])
        s = (s.reshape(2, t, t) + bias[None]).reshape(2 * t, t)
        _softmax_step(s, v_ref[pl.ds(off, t), :], m_ref, l_ref, acc_ref)
        return carry

    lax.fori_loop(0, nk, body, 0)
    lf = lam_ref[...]
    lam = (jnp.exp(jnp.sum(lf[0:1] * lf[1:2], keepdims=True))
           - jnp.exp(jnp.sum(lf[2:3] * lf[3:4], keepdims=True)) + lambda_init)
    on = acc_ref[...] / jnp.sum(l_ref[...], -1, keepdims=True)
    o = on[:t] - lam * on[t:]
    ms = jnp.sum(o * o, -1, keepdims=True) * (1.0 / D_V_DIM)
    o_ref[...] = (o * lax.rsqrt(ms + RMS_EPS) * gain_ref[...]).astype(BF)


def _flash_d(qkv, bias, lam, gain, lambda_init):
    b, s, _ = qkv.shape
    t = bias.shape[-1]
    return pl.pallas_call(
        functools.partial(_flash_d_kernel, nk=s // t, lambda_init=lambda_init),
        grid=(b, D_HEADS, s // t),
        in_specs=[pl.BlockSpec((None, t, LANES), lambda bi, h, i: (bi, i, h)),
                  pl.BlockSpec((None, s, LANES), lambda bi, h, i: (bi, 0, D_HEADS + h)),
                  pl.BlockSpec((None, s, LANES), lambda bi, h, i: (bi, 0, 2 * D_HEADS + h)),
                  pl.BlockSpec((None, 5, t, t), lambda bi, h, i: (h, 0, 0, 0)),
                  pl.BlockSpec((4, D_QK_DIM), lambda bi, h, i: (0, 0)),
                  pl.BlockSpec((1, LANES), lambda bi, h, i: (0, 0))],
        out_specs=pl.BlockSpec((None, t, LANES), lambda bi, h, i: (bi, i, h)),
        out_shape=jax.ShapeDtypeStruct((b, s, D_HEADS * LANES), BF),
        scratch_shapes=[pltpu.VMEM((2 * t, LANES), F32)] * 3,
        compiler_params=_params("parallel", "parallel", "arbitrary"),
        name="flash_d",
    )(qkv, qkv, qkv, bias, lam, gain)


def _cross_kernel(x_ref, kv_ref, wq_ref, wo_ref, g_ref, b_ref, wr_ref, y_ref, ybf_ref, aff_ref):
    x = x_ref[...]
    tm = x.shape[0]
    q = (_dot(x.astype(BF), wq_ref[...]) * (X_HEAD_DIM ** -0.5)).astype(BF)
    outs = []
    for h in range(X_HEADS):
        sl = slice(X_HEAD_DIM * h, X_HEAD_DIM * (h + 1))
        s = _dot_nt(q[:, sl], kv_ref[:, sl])
        p = jnp.exp(s - jnp.max(s, -1, keepdims=True))
        l = jnp.sum(p, -1, keepdims=True)
        vh = kv_ref[:, D_MODEL + X_HEAD_DIM * h:D_MODEL + X_HEAD_DIM * (h + 1)]
        outs.append((_dot(p.astype(BF), vh) / l).astype(BF))
    o = jnp.concatenate(outs, axis=1)
    y = _ln_rows(ALPHA * x + _dot(o, wo_ref[...]), g_ref[...], b_ref[...])
    y_ref[...] = y
    yh = y.astype(BF)
    ybf_ref[...] = yh
    yl = (y - yh.astype(F32)).astype(BF)
    wr = wr_ref[...]
    wh = wr.astype(BF)
    wl = (wr - wh.astype(F32)).astype(BF)
    lg = _dot_nt(wh, yh) + _dot_nt(wh, yl) + _dot_nt(wl, yh)
    e = jnp.exp(lg - jnp.max(lg, 0, keepdims=True))
    aff = e / jnp.sum(e, 0, keepdims=True)
    for c in range(tm // LANES):
        aff_ref[c] = aff[:, LANES * c:LANES * (c + 1)]


def _cross(x, kv, wq, wo, g, b, wr_t, s, mem_len, tm=512):
    n = x.shape[0]
    tm = _tile(s, tm)
    per = s // tm
    fixed = lambda i: (0, 0)
    return pl.pallas_call(
        _cross_kernel,
        grid=(n // tm,),
        in_specs=[pl.BlockSpec((tm, D_MODEL), lambda i: (i, 0)),
                  pl.BlockSpec((mem_len, 2 * D_MODEL), lambda i: (i // per, 0)),
                  pl.BlockSpec((D_MODEL, D_MODEL), fixed), pl.BlockSpec((D_MODEL, D_MODEL), fixed),
                  pl.BlockSpec((1, D_MODEL), fixed), pl.BlockSpec((1, D_MODEL), fixed),
                  pl.BlockSpec((N_EXPERTS, D_MODEL), fixed)],
        out_specs=[pl.BlockSpec((tm, D_MODEL), lambda i: (i, 0)),
                   pl.BlockSpec((tm, D_MODEL), lambda i: (i, 0)),
                   pl.BlockSpec((tm // LANES, N_EXPERTS, LANES), lambda i: (i, 0, 0))],
        out_shape=[jax.ShapeDtypeStruct((n, D_MODEL), F32),
                   jax.ShapeDtypeStruct((n, D_MODEL), BF),
                   jax.ShapeDtypeStruct((n // LANES, N_EXPERTS, LANES), F32)],
        compiler_params=_params("parallel"),
        name="cross",
    )(x, kv, wq, wo, g, b, wr_t)


def _select_kernel(a_ref, pos_ref, st_ref, *, k, nbits):
    nt = a_ref.shape[0]
    shape = (nt, N_EXPERTS, LANES)
    kf = float(k)

    def keys():
        return lax.bitcast_convert_type(a_ref[...], I32)

    def count(mask):
        c = jnp.sum(jnp.where(mask, 1.0, 0.0), axis=0, keepdims=True)
        return jnp.sum(c, axis=2, keepdims=True)

    def value_step(it, thr):
        cand = thr | jnp.left_shift(jnp.int32(1), 30 - it)
        return jnp.where(count(keys() >= cand) >= kf, cand, thr)

    thr = lax.fori_loop(0, 31, value_step, jnp.zeros((1, N_EXPERTS, 1), I32))
    need = kf - count(keys() > thr)
    idx = lax.broadcasted_iota(I32, shape, 0) * LANES + lax.broadcasted_iota(I32, shape, 2)

    def index_step(it, ithr):
        cand = ithr | jnp.left_shift(jnp.int32(1), nbits - 1 - it)
        return jnp.where(count((keys() == thr) & (idx < cand)) < need, cand, ithr)

    ithr = lax.fori_loop(0, nbits, index_step, jnp.zeros((1, N_EXPERTS, 1), I32))
    thr2, ithr2 = thr[0], ithr[0]
    upper = jnp.where(lax.broadcasted_iota(I32, (LANES, LANES), 0) <= lax.broadcasted_iota(I32, (LANES, LANES), 1),
                      1.0, 0.0).astype(BF)
    lane = lax.broadcasted_iota(I32, (N_EXPERTS, LANES), 1)

    def tile_step(j, carry):
        kj = lax.bitcast_convert_type(a_ref[j], I32)
        sel = (kj > thr2) | ((kj == thr2) & (j * LANES + lane <= ithr2))
        m = jnp.where(sel, 1.0, 0.0)
        inc = _dot(m.astype(BF), upper)
        pos_ref[j] = jnp.where(sel, inc - m + carry, -1.0).astype(I32)
        st_ref[j] = jnp.broadcast_to(carry, (N_EXPERTS, LANES)).astype(I32)
        return carry + inc[:, LANES - 1:LANES]

    lax.fori_loop(0, nt, tile_step, jnp.zeros((N_EXPERTS, 1), F32))


def _select(aff3, k):
    nt = aff3.shape[0]
    nbits = max(1, int(math.ceil(math.log2(nt * LANES))))
    shp = jax.ShapeDtypeStruct(aff3.shape, I32)
    return pl.pallas_call(
        functools.partial(_select_kernel, k=k, nbits=nbits),
        out_shape=[shp, shp],
        compiler_params=pltpu.CompilerParams(vmem_limit_bytes=VMEM_LIMIT),
        name="select",
    )(aff3)


_GATHER_ROWS = LANES + BF16_ROWS


def _moe_ffn_kernel(st_ref, x_ref, pos_ref, wg_ref, wu_ref, wd_ref, ye_ref, buf_ref, *, nb, ncf, sub, ck):
    e = pl.program_id(0)
    t = pl.program_id(1)

    @pl.when(t == 0)
    def _():
        buf_ref[...] = jnp.zeros_like(buf_ref)

    @pl.when(t < nb)
    def _():
        row = lax.broadcasted_iota(I32, (_GATHER_ROWS, LANES), 0)
        for s in range(sub):
            st = st_ref[e, t * sub + s]
            base = pl.multiple_of((st // BF16_ROWS) * BF16_ROWS, BF16_ROWS)
            rel = pos_ref[s, pl.ds(e, 1), :] - base
            onehot = jnp.where(row == rel, 1.0, 0.0).astype(BF)
            c = _dot(onehot, x_ref[LANES * s:LANES * (s + 1), :])
            buf_ref[pl.ds(base, _GATHER_ROWS), :] = buf_ref[pl.ds(base, _GATHER_ROWS), :] + c.astype(BF)

    @pl.when(t >= nb)
    def _():
        c = t - nb

        @pl.when(c < ncf)
        def _():
            xe = buf_ref[pl.ds(pl.multiple_of(c * ck, ck), ck), :]
            h = (jax.nn.silu(_dot(xe, wg_ref[...])) * _dot(xe, wu_ref[...])).astype(BF)
            ye_ref[...] = _dot(h, wd_ref[...]).astype(BF)

        @pl.when(c >= ncf)
        def _():
            ye_ref[...] = jnp.zeros_like(ye_ref)


def _moe_ffn(starts, xbf, pos3, wg, wu, wd, cap, tb=1024):
    n = xbf.shape[0]
    tb = _tile(n, tb)
    nb = n // tb
    sub = tb // LANES
    ck = _tile(cap, 512)
    ncf = cap // ck
    nch = ncf + 1
    dff = wg.shape[-1]
    blk = lambda e, t, st: (jnp.minimum(t, nb - 1), 0)
    return pl.pallas_call(
        functools.partial(_moe_ffn_kernel, nb=nb, ncf=ncf, sub=sub, ck=ck),
        grid_spec=pltpu.PrefetchScalarGridSpec(
            num_scalar_prefetch=1,
            grid=(N_EXPERTS, nb + nch),
            in_specs=[pl.BlockSpec((tb, D_MODEL), blk),
                      pl.BlockSpec((sub, N_EXPERTS, LANES), lambda e, t, st: (jnp.minimum(t, nb - 1), 0, 0)),
                      pl.BlockSpec((None, D_MODEL, dff), lambda e, t, st: (e, 0, 0)),
                      pl.BlockSpec((None, D_MODEL, dff), lambda e, t, st: (e, 0, 0)),
                      pl.BlockSpec((None, dff, D_MODEL), lambda e, t, st: (e, 0, 0))],
            out_specs=pl.BlockSpec((None, ck, D_MODEL), lambda e, t, st: (e, jnp.maximum(t - nb, 0), 0)),
            scratch_shapes=[pltpu.VMEM((cap + _GATHER_ROWS, D_MODEL), BF)]),
        out_shape=jax.ShapeDtypeStruct((N_EXPERTS, nch * ck, D_MODEL), BF),
        compiler_params=_params("arbitrary", "arbitrary"),
        name="moe_ffn",
    )(starts, xbf, pos3, wg, wu, wd)


def _combine_kernel(st_ref, x_ref, aff_ref, pos_ref, g_ref, b_ref, *rest):
    ye_refs, y_ref = rest[:N_EXPERTS], rest[N_EXPERTS]
    j = pl.program_id(0)
    col = lax.broadcasted_iota(I32, (LANES, _GATHER_ROWS), 1)
    acc = jnp.zeros((LANES, D_MODEL), F32)
    for e in range(N_EXPERTS):
        base = (st_ref[e, j] // BF16_ROWS) * BF16_ROWS
        rel = pos_ref[:, e:e + 1] - base
        onehot = jnp.where(col == rel, 1.0, 0.0).astype(BF)
        acc = acc + aff_ref[:, e:e + 1] * _dot(onehot, ye_refs[e][0])
    y_ref[...] = _ln_rows(ALPHA * x_ref[...] + acc, g_ref[...], b_ref[...])


def _combine(starts, x, aff, pos, g, b, ye):
    n = x.shape[0]
    row = lambda j, st: (j, 0)
    fixed = lambda j, st: (0, 0)

    def ye_spec(e):
        return pl.BlockSpec((pl.Element(1), pl.Element(_GATHER_ROWS), pl.Element(D_MODEL)),
                            lambda j, st: (e, (st[e, j] // BF16_ROWS) * BF16_ROWS, 0))

    return pl.pallas_call(
        _combine_kernel,
        grid_spec=pltpu.PrefetchScalarGridSpec(
            num_scalar_prefetch=1,
            grid=(n // LANES,),
            in_specs=[pl.BlockSpec((LANES, D_MODEL), row),
                      pl.BlockSpec((LANES, N_EXPERTS), row),
                      pl.BlockSpec((LANES, N_EXPERTS), row),
                      pl.BlockSpec((1, D_MODEL), fixed), pl.BlockSpec((1, D_MODEL), fixed)]
            + [ye_spec(e) for e in range(N_EXPERTS)],
            out_specs=pl.BlockSpec((LANES, D_MODEL), row)),
        out_shape=jax.ShapeDtypeStruct((n, D_MODEL), F32),
        compiler_params=_params("arbitrary"),
        name="combine",
    )(starts, x, aff, pos, g, b, *([ye] * N_EXPERTS))


def _moe(x, xbf, aff3, wg, wu, wd, g, b):
    n = x.shape[0]
    cap = EC_CAPACITY * n // N_EXPERTS
    pos3, st3 = _select(aff3, cap)
    starts = jnp.transpose(st3[:, :, 0])
    ye = _moe_ffn(starts, xbf, pos3, wg, wu, wd, cap)
    aff = jnp.transpose(aff3, (0, 2, 1)).reshape(n, N_EXPERTS)
    pos = jnp.transpose(pos3, (0, 2, 1)).reshape(n, N_EXPERTS)
    return _combine(starts, x, aff, pos, g, b, ye)


def _prep_weights(p):
    w = {}
    w['a'] = [_prep_a(p['a_w_qkv'][j], p['a_q_gain'][j], p['a_k_gain'][j], p['a_w_o'][j])
              for j in range(p['a_w_qkv'].shape[0])]
    w['b'] = [_prep_b(p['b_w_qkv'][j], p['b_w_o'][j]) for j in range(p['b_w_qkv'].shape[0])]
    w['c'] = [_prep_c(p['c_lam_re'][j], p['c_lam_im'][j], p['c_log_dt'][j], p['c_b_re'][j], p['c_b_im'][j],
                      p['c_c_re'][j], p['c_c_im'][j]) + (p['c_d'][j][None, :], p['c_w_glu'][j].astype(BF))
              for j in range(p['c_lam_re'].shape[0])]
    w['d'] = []
    for j in range(p['d_w_qkv'].shape[0]):
        layer = N_MIXERS * j + 3
        lambda_init = 0.8 - 0.6 * math.exp(-0.3 * layer)
        w['d'].append(_prep_d(p['d_w_qkv'][j], p['d_norm_gain'][j], p['d_w_o'][j], lambda_init)
                      + (p['d_lam'][j].astype(F32), lambda_init))
    w['diff_bias'] = _diff_bias(p['rel_bias'], _D_T)
    w['x_w_q'] = p['x_w_q'].astype(BF)
    w['x_w_kv'] = p['x_w_kv'].astype(BF)
    w['x_w_o'] = p['x_w_o'].astype(BF)
    w['router_t'] = jnp.transpose(p['moe_w_router'], (0, 2, 1)).astype(F32)
    w['moe_w_gate'] = p['moe_w_gate'].astype(BF)
    w['moe_w_up'] = p['moe_w_up'].astype(BF)
    w['moe_w_down'] = p['moe_w_down'].astype(BF)
    return w


def _trunk(x, mem, p, w):
    b, s, _ = x.shape
    n = b * s
    mem_len = mem.shape[1]
    x = x.reshape(n, D_MODEL)
    mem2 = mem.reshape(b * mem_len, D_MODEL)
    ln_g, ln_b = p['ln_g'], p['ln_b']
    for i in range(DEPTH):
        m, j = i % N_MIXERS, i // N_MIXERS
        g0, b0 = ln_g[i, 0][None, :], ln_b[i, 0][None, :]
        if m == 0:
            wa, gains, wo = w['a'][j]
            qkv = _proj_a(x, wa, _rope_table(s), gains, s)
            o = _flash_a(qkv.reshape(b, s, _A_COLS))
            x = _post(x, o.reshape(n, A_HEADS * LANES), wo, g0, b0)
        elif m == 1:
            wb, wo = w['b'][j]
            qkv = _proj(x, wb).reshape(b, s, _B_TILES * LANES)
            os_, ls_ = [], []
            for g in range(len(B_PATTERNS)):
                tm = min(LANES, s // B_PATTERNS[g][1])
                o, lse = _band_attention(qkv, _band_bias(p['rel_bias'], g, tm), g, tm)
                os_.append(o)
                ls_.append(lse)
            x = _post_b(x, os_, ls_, wo, g0, b0)
        elif m == 2:
            w_sum, w_out, dec_re, dec_im, dskip, wglu = w['c'][j]
            ys = _s5(x, b, s, w_sum, w_out, dec_re, dec_im)
            x = _post_c(x, ys, dskip, wglu, g0, b0)
        else:
            wd, gain, wo, lam, lambda_init = w['d'][j]
            qkv = _proj(x, wd).reshape(b, s, 3 * D_HEADS * LANES)
            o = _flash_d(qkv, w['diff_bias'], lam, gain, lambda_init)
            x = _post(x, o.reshape(n, D_HEADS * LANES), wo, g0, b0)
        kv = _proj(mem2, w['x_w_kv'][i], tm=mem_len)
        x, xbf, aff3 = _cross(x, kv, w['x_w_q'][i], w['x_w_o'][i], ln_g[i, 1][None, :], ln_b[i, 1][None, :],
                              w['router_t'][i], s, mem_len)
        x = _moe(x, xbf, aff3, w['moe_w_gate'][i], w['moe_w_up'][i], w['moe_w_down'][i],
                 ln_g[i, 2][None, :], ln_b[i, 2][None, :])
    return x.reshape(b, s, D_MODEL)


def kernel(x_prompt, x_sample, mem_prompt, mem_sample, rel_bias, ln_g, ln_b, a_w_qkv, a_q_gain, a_k_gain, a_w_o, b_w_qkv, b_w_o, c_lam_re, c_lam_im, c_log_dt, c_b_re, c_b_im, c_c_re, c_c_im, c_d, c_w_glu, d_w_qkv, d_lam, d_norm_gain, d_w_o, x_w_q, x_w_kv, x_w_o, moe_w_router, moe_w_gate, moe_w_up, moe_w_down):
    p = dict(rel_bias=rel_bias, ln_g=ln_g, ln_b=ln_b,
             a_w_qkv=a_w_qkv, a_q_gain=a_q_gain, a_k_gain=a_k_gain, a_w_o=a_w_o,
             b_w_qkv=b_w_qkv, b_w_o=b_w_o,
             c_lam_re=c_lam_re, c_lam_im=c_lam_im, c_log_dt=c_log_dt, c_b_re=c_b_re, c_b_im=c_b_im,
             c_c_re=c_c_re, c_c_im=c_c_im, c_d=c_d, c_w_glu=c_w_glu,
             d_w_qkv=d_w_qkv, d_lam=d_lam, d_norm_gain=d_norm_gain, d_w_o=d_w_o,
             x_w_q=x_w_q, x_w_kv=x_w_kv, x_w_o=x_w_o,
             moe_w_router=moe_w_router, moe_w_gate=moe_w_gate, moe_w_up=moe_w_up, moe_w_down=moe_w_down)
    w = _prep_weights(p)
    return (_trunk(x_prompt, mem_prompt, p, w), _trunk(x_sample, mem_sample, p, w))
```

```python
import functools
import math

import numpy as np
import jax
import jax.numpy as jnp
from jax import lax
from jax.experimental import pallas as pl
from jax.experimental.pallas import tpu as pltpu

F32 = jnp.float32
BF = jnp.bfloat16
I32 = jnp.int32

D_MODEL = 1024
DEPTH = 4
GRID_W = 64
N_MIXERS = 4
LN_EPS = 1e-5
RMS_EPS = 1e-6
ALPHA = (2.0 * DEPTH) ** 0.25

A_HEADS = 16
A_KV_HEADS = 4
A_HEAD_DIM = 64
ROPE_BASE = 10000.0

B_PATTERNS = ((128, 1), (512, 4), (2048, 16))
B_HEADS_PER_GROUP = 4
B_HEAD_DIM = 64

C_GROUP = 16
C_N_GROUPS = D_MODEL // C_GROUP
C_STATE = 64
C_CHUNK = 16
_C_GL = 128 // C_GROUP

D_HEADS = 12
D_QK_DIM = 32
D_V_DIM = 64

REL_BUCKETS = 32
REL_MAX_DIST = 128

X_HEADS = 4
X_HEAD_DIM = D_MODEL // X_HEADS

N_EXPERTS = 16
EC_CAPACITY = 2

LANES = 128
BF16_ROWS = 16
VMEM_LIMIT = 56 * 1024 * 1024
NEG = -1e30
LOG2E = math.log2(math.e)


def _params(*sem):
    return pltpu.CompilerParams(dimension_semantics=sem, vmem_limit_bytes=VMEM_LIMIT)


def _tile(n, pref):
    t = min(n, pref)
    assert n % t == 0, (n, pref)
    return t


def _ln_rows(v, g, b):
    mu = jnp.mean(v, -1, keepdims=True)
    c = v - mu
    var = jnp.mean(c * c, -1, keepdims=True)
    return c * lax.rsqrt(var + LN_EPS) * g + b


def _dot_nt(a, b):
    return lax.dot_general(a, b, (((1,), (1,)), ((), ())), preferred_element_type=F32)


def _dot(a, b):
    return jnp.dot(a, b, preferred_element_type=F32)


def _proj_kernel(x_ref, w_ref, o_ref):
    o_ref[...] = _dot(x_ref[...].astype(BF), w_ref[...]).astype(o_ref.dtype)


def _proj(x, w, tm=512):
    n, k = x.shape
    m = w.shape[1]
    tm = _tile(n, tm)
    return pl.pallas_call(
        _proj_kernel,
        grid=(n // tm,),
        in_specs=[pl.BlockSpec((tm, k), lambda i: (i, 0)),
                  pl.BlockSpec((k, m), lambda i: (0, 0))],
        out_specs=pl.BlockSpec((tm, m), lambda i: (i, 0)),
        out_shape=jax.ShapeDtypeStruct((n, m), BF),
        compiler_params=_params("parallel"),
        name="proj",
    )(x, w)


def _post_kernel(x_ref, o_ref, w_ref, g_ref, b_ref, y_ref):
    h = _dot(o_ref[...], w_ref[...])
    y_ref[...] = _ln_rows(ALPHA * x_ref[...] + h, g_ref[...], b_ref[...])


def _post(x, o, w, g, b, tm=512):
    n = x.shape[0]
    ko = o.shape[1]
    tm = _tile(n, tm)
    return pl.pallas_call(
        _post_kernel,
        grid=(n // tm,),
        in_specs=[pl.BlockSpec((tm, D_MODEL), lambda i: (i, 0)),
                  pl.BlockSpec((tm, ko), lambda i: (i, 0)),
                  pl.BlockSpec((ko, D_MODEL), lambda i: (0, 0)),
                  pl.BlockSpec((1, D_MODEL), lambda i: (0, 0)),
                  pl.BlockSpec((1, D_MODEL), lambda i: (0, 0))],
        out_specs=pl.BlockSpec((tm, D_MODEL), lambda i: (i, 0)),
        out_shape=jax.ShapeDtypeStruct((n, D_MODEL), F32),
        compiler_params=_params("parallel"),
        name="post",
    )(x, o, w, g, b)


def _post_b_kernel(x_ref, o0_ref, o1_ref, o2_ref, l0_ref, l1_ref, l2_ref, w_ref, g_ref, b_ref, y_ref):
    l0, l1, l2 = l0_ref[...], l1_ref[...], l2_ref[...]
    m = jnp.maximum(jnp.maximum(l0, l1), l2)
    e0, e1, e2 = jnp.exp(l0 - m), jnp.exp(l1 - m), jnp.exp(l2 - m)
    inv = 1.0 / (e0 + e1 + e2)
    gw = B_HEADS_PER_GROUP * LANES
    h = _dot((o0_ref[...].astype(F32) * (e0 * inv)).astype(BF), w_ref[0:gw, :])
    h = h + _dot((o1_ref[...].astype(F32) * (e1 * inv)).astype(BF), w_ref[gw:2 * gw, :])
    h = h + _dot((o2_ref[...].astype(F32) * (e2 * inv)).astype(BF), w_ref[2 * gw:3 * gw, :])
    y_ref[...] = _ln_rows(ALPHA * x_ref[...] + h, g_ref[...], b_ref[...])


def _post_b(x, os_, ls_, w, g, b, tm=512):
    n = x.shape[0]
    tm = _tile(n, tm)
    gw = B_HEADS_PER_GROUP * LANES
    row = lambda i: (i, 0)
    fixed = lambda i: (0, 0)
    return pl.pallas_call(
        _post_b_kernel,
        grid=(n // tm,),
        in_specs=[pl.BlockSpec((tm, D_MODEL), row)] + [pl.BlockSpec((tm, gw), row)] * 6
        + [pl.BlockSpec((3 * gw, D_MODEL), fixed), pl.BlockSpec((1, D_MODEL), fixed),
           pl.BlockSpec((1, D_MODEL), fixed)],
        out_specs=pl.BlockSpec((tm, D_MODEL), row),
        out_shape=jax.ShapeDtypeStruct((n, D_MODEL), F32),
        compiler_params=_params("parallel"),
        name="post_b",
    )(x, *os_, *ls_, w, g, b)


def _post_c_kernel(x_ref, ys_ref, d_ref, w_ref, g_ref, b_ref, y_ref):
    x = x_ref[...]
    z = jax.nn.gelu(ys_ref[...] + d_ref[...] * x).astype(BF)
    h = _dot(z, w_ref[...])
    hh = h[:, :D_MODEL] * jax.nn.sigmoid(h[:, D_MODEL:])
    y_ref[...] = _ln_rows(ALPHA * x + hh, g_ref[...], b_ref[...])


def _post_c(x, ys, d, w, g, b, tm=512):
    n = x.shape[0]
    tm = _tile(n, tm)
    row = lambda i: (i, 0)
    fixed = lambda i: (0, 0)
    return pl.pallas_call(
        _post_c_kernel,
        grid=(n // tm,),
        in_specs=[pl.BlockSpec((tm, D_MODEL), row), pl.BlockSpec((tm, D_MODEL), row),
                  pl.BlockSpec((1, D_MODEL), fixed), pl.BlockSpec((D_MODEL, 2 * D_MODEL), fixed),
                  pl.BlockSpec((1, D_MODEL), fixed), pl.BlockSpec((1, D_MODEL), fixed)],
        out_specs=pl.BlockSpec((tm, D_MODEL), row),
        out_shape=jax.ShapeDtypeStruct((n, D_MODEL), F32),
        compiler_params=_params("parallel"),
        name="post_c",
    )(x, ys, d, w, g, b)


_A_QK_TILES = A_HEADS + A_KV_HEADS
_A_COLS = (_A_QK_TILES + A_KV_HEADS) * LANES


def _rope_partner():
    d = np.arange(A_HEAD_DIM)
    e = d % (A_HEAD_DIM // 2)
    lo = e < A_HEAD_DIM // 4
    return np.where(lo, d + A_HEAD_DIM // 4, d - A_HEAD_DIM // 4), np.where(lo, -1.0, 1.0).astype(np.float32)


def _prep_a(w_qkv, q_gain, k_gain, w_o):
    partner, sign = _rope_partner()
    nqk = _A_QK_TILES * A_HEAD_DIM
    wqk = w_qkv[:, :nqk].reshape(D_MODEL, _A_QK_TILES, A_HEAD_DIM)
    wsw = wqk[:, :, partner] * sign
    wqk = jnp.concatenate([wqk, wsw], -1).reshape(D_MODEL, _A_QK_TILES * LANES)
    wv = w_qkv[:, nqk:].reshape(D_MODEL, A_KV_HEADS, A_HEAD_DIM)
    wv = jnp.concatenate([wv, jnp.zeros_like(wv)], -1).reshape(D_MODEL, A_KV_HEADS * LANES)
    w = jnp.concatenate([wqk, wv], 1).astype(BF)
    gq = jnp.concatenate([q_gain, q_gain[partner]]) * (A_HEAD_DIM ** -0.5 * 0.5 * LOG2E)
    gk = jnp.concatenate([k_gain, k_gain[partner]])
    gains = jnp.stack([gq, gk], 0)
    wo = w_o.reshape(A_HEADS, A_HEAD_DIM, D_MODEL)
    wo = jnp.concatenate([wo, jnp.zeros_like(wo)], 1).reshape(A_HEADS * LANES, D_MODEL).astype(BF)
    return w, gains, wo


def _rope_table(s):
    pos = jnp.arange(s)
    rows, cols = (pos // GRID_W).astype(F32), (pos % GRID_W).astype(F32)
    half = A_HEAD_DIM // 2
    freqs = ROPE_BASE ** (-jnp.arange(0, half, 2, dtype=F32) / half)
    ang_r = rows[:, None] * freqs
    ang_c = cols[:, None] * freqs
    ang = jnp.concatenate([ang_r, ang_r, ang_c, ang_c], -1)
    return jnp.concatenate([jnp.cos(ang), jnp.sin(ang)], -1)


def _proj_a_kernel(x_ref, w_ref, cs_ref, g_ref, o_ref):
    y = _dot(x_ref[...].astype(BF), w_ref[...])
    cs = cs_ref[...]
    gq = cs * g_ref[0:1, :]
    gk = cs * g_ref[1:2, :]
    for h in range(_A_QK_TILES):
        t = y[:, LANES * h:LANES * (h + 1)]
        r = lax.rsqrt(jnp.sum(t * t, -1, keepdims=True) * (1.0 / LANES) + RMS_EPS)
        e = t * r * (gq if h < A_HEADS else gk)
        o_ref[:, LANES * h:LANES * (h + 1)] = (e + pltpu.roll(e, LANES // 2, 1)).astype(BF)
    o_ref[:, _A_QK_TILES * LANES:] = y[:, _A_QK_TILES * LANES:].astype(BF)


def _proj_a(x, w, cs, gains, s, tm=512):
    n = x.shape[0]
    tm = _tile(s, tm)
    per = s // tm
    return pl.pallas_call(
        _proj_a_kernel,
        grid=(n // tm,),
        in_specs=[pl.BlockSpec((tm, D_MODEL), lambda i: (i, 0)),
                  pl.BlockSpec((D_MODEL, _A_COLS), lambda i: (0, 0)),
                  pl.BlockSpec((tm, LANES), lambda i: (i % per, 0)),
                  pl.BlockSpec((2, LANES), lambda i: (0, 0))],
        out_specs=pl.BlockSpec((tm, _A_COLS), lambda i: (i, 0)),
        out_shape=jax.ShapeDtypeStruct((n, _A_COLS), BF),
        compiler_params=_params("parallel"),
        name="proj_a",
    )(x, w, cs, gains)


def _softmax_step(s, v, m_ref, l_ref, acc_ref):
    tk = s.shape[1]
    m_old = m_ref[...]
    m_new = jnp.maximum(m_old, jnp.max(s, -1, keepdims=True))
    a = jnp.exp2(m_old - m_new)
    psum = None
    chunks = []
    for c in range(tk // LANES):
        pc = jnp.exp2(s[:, LANES * c:LANES * (c + 1)] - m_new)
        psum = pc if psum is None else psum + pc
        chunks.append(pc.astype(BF))
    p = jnp.concatenate(chunks, axis=1)
    l_ref[...] = a * l_ref[...] + psum
    acc_ref[...] = a * acc_ref[...] + _dot(p, v)
    m_ref[...] = m_new


def _softmax_init(m_ref, l_ref, acc_ref):
    m_ref[...] = jnp.full(m_ref.shape, -jnp.inf, F32)
    l_ref[...] = jnp.zeros_like(l_ref)
    acc_ref[...] = jnp.zeros_like(acc_ref)


def _flash_a_kernel(q_ref, k_ref, v_ref, o_ref, m_ref, l_ref, acc_ref, *, tk, nk):
    tq = q_ref.shape[0]
    rep = A_HEADS // A_KV_HEADS
    q = jnp.concatenate([q_ref[:, LANES * r:LANES * (r + 1)] for r in range(rep)], axis=0)
    _softmax_init(m_ref, l_ref, acc_ref)

    def body(j, carry):
        off = pl.multiple_of(j * tk, tk)
        _softmax_step(_dot_nt(q, k_ref[pl.ds(off, tk), :]), v_ref[pl.ds(off, tk), :], m_ref, l_ref, acc_ref)
        return carry

    lax.fori_loop(0, nk, body, 0, unroll=2)
    o = acc_ref[...] / jnp.sum(l_ref[...], -1, keepdims=True)
    for r in range(rep):
        o_ref[:, LANES * r:LANES * (r + 1)] = o[r * tq:(r + 1) * tq].astype(BF)


def _flash_a(qkv, tq=256, tk=1024):
    b, s, _ = qkv.shape
    tq, tk = _tile(s, tq), _tile(s, tk)
    rep = A_HEADS // A_KV_HEADS
    gw = rep * LANES
    return pl.pallas_call(
        functools.partial(_flash_a_kernel, tk=tk, nk=s // tk),
        grid=(b, A_KV_HEADS, s // tq),
        in_specs=[pl.BlockSpec((None, tq, gw), lambda bi, g, i: (bi, i, g)),
                  pl.BlockSpec((None, s, LANES), lambda bi, g, i: (bi, 0, A_HEADS + g)),
                  pl.BlockSpec((None, s, LANES), lambda bi, g, i: (bi, 0, _A_QK_TILES + g))],
        out_specs=pl.BlockSpec((None, tq, gw), lambda bi, g, i: (bi, i, g)),
        out_shape=jax.ShapeDtypeStruct((b, s, A_HEADS * LANES), BF),
        scratch_shapes=[pltpu.VMEM((rep * tq, LANES), F32)] * 3,
        compiler_params=_params("parallel", "parallel", "arbitrary"),
        name="flash_a",
    )(qkv, qkv, qkv)


def _bucket(rel):
    half = REL_BUCKETS // 2
    max_exact = half // 2
    n = jnp.abs(rel)
    large = max_exact + (jnp.log(jnp.maximum(n, 1).astype(F32) / max_exact)
                         / math.log(REL_MAX_DIST / max_exact) * (half - max_exact)).astype(I32)
    large = jnp.minimum(large, half - 1)
    return jnp.where(rel > 0, half, 0) + jnp.where(n < max_exact, n, large)


_B_TILES = 3 * len(B_PATTERNS) * B_HEADS_PER_GROUP
_B_GW = B_HEADS_PER_GROUP * LANES


def _pad_heads(w, heads, dim):
    w = w.reshape(w.shape[0], heads, dim)
    return jnp.concatenate([w, jnp.zeros((w.shape[0], heads, LANES - dim), w.dtype)], -1).reshape(
        w.shape[0], heads * LANES)


def _prep_b(w_qkv, w_o):
    nh = len(B_PATTERNS) * B_HEADS_PER_GROUP
    c = nh * B_HEAD_DIM
    wq = _pad_heads(w_qkv[:, :c] * (B_HEAD_DIM ** -0.5), nh, B_HEAD_DIM)
    wk = _pad_heads(w_qkv[:, c:2 * c], nh, B_HEAD_DIM)
    wv = _pad_heads(w_qkv[:, 2 * c:], nh, B_HEAD_DIM)
    w = jnp.concatenate([wq, wk, wv], 1).astype(BF)
    wo = w_o.reshape(nh, B_HEAD_DIM, D_MODEL)
    wo = jnp.concatenate([wo, jnp.zeros_like(wo)], 1).reshape(nh * LANES, D_MODEL).astype(BF)
    return w, wo


def _band_bias(rel_bias, g, tm):
    window, dil = B_PATTERNS[g]
    n_side = window // (2 * dil)
    j = jnp.arange(3 * tm)[None, :] - tm - jnp.arange(tm)[:, None]
    inband = jnp.abs(j) <= n_side
    bias = rel_bias[_bucket(dil * j)][:, :, g * B_HEADS_PER_GROUP:(g + 1) * B_HEADS_PER_GROUP]
    bias = jnp.where(inband[:, :, None], bias, NEG)
    return jnp.transpose(bias, (2, 0, 1)).astype(F32)


def _band_kernel(q_ref, kp_ref, kc_ref, kn_ref, vp_ref, vc_ref, vn_ref, bias_ref, o_ref, lse_ref, *, length):
    tm = q_ref.shape[0]
    mi = pl.program_id(2)
    key_pos = (mi - 1) * tm + lax.broadcasted_iota(I32, (1, 3 * tm), 1)
    valid = (key_pos >= 0) & (key_pos < length)
    for h in range(B_HEADS_PER_GROUP):
        sl = slice(LANES * h, LANES * (h + 1))
        k = jnp.concatenate([kp_ref[:, sl], kc_ref[:, sl], kn_ref[:, sl]], axis=0)
        v = jnp.concatenate([vp_ref[:, sl], vc_ref[:, sl], vn_ref[:, sl]], axis=0)
        s = _dot_nt(q_ref[:, sl], k) + bias_ref[h]
        s = jnp.where(valid, s, NEG)
        m = jnp.max(s, -1, keepdims=True)
        p = jnp.exp(s - m)
        l = jnp.sum(p, -1, keepdims=True)
        o_ref[:, sl] = (_dot(p.astype(BF), v) / l).astype(BF)
        lse_ref[:, sl] = jnp.broadcast_to(m + jnp.log(l), (tm, LANES))


def _band_attention(qkv, bias, g, tm):
    b, s, c = qkv.shape
    dil = B_PATTERNS[g][1]
    length = s // dil
    nblk = c // _B_GW
    view = qkv.reshape(b, length, dil * c)
    nt = length // tm
    ng = len(B_PATTERNS)

    def spec(which, shift):
        def imap(bi, r, mi):
            return (bi, jnp.clip(mi + shift, 0, nt - 1), r * nblk + which * ng + g)
        return pl.BlockSpec((None, tm, _B_GW), imap)

    out_spec = pl.BlockSpec((None, tm, _B_GW), lambda bi, r, mi: (bi, mi, r))
    o, lse = pl.pallas_call(
        functools.partial(_band_kernel, length=length),
        grid=(b, dil, nt),
        in_specs=[spec(0, 0), spec(1, -1), spec(1, 0), spec(1, 1), spec(2, -1), spec(2, 0), spec(2, 1),
                  pl.BlockSpec((B_HEADS_PER_GROUP, tm, 3 * tm), lambda bi, r, mi: (0, 0, 0))],
        out_specs=[out_spec, out_spec],
        out_shape=[jax.ShapeDtypeStruct((b, length, dil * _B_GW), BF),
                   jax.ShapeDtypeStruct((b, length, dil * _B_GW), F32)],
        compiler_params=_params("parallel", "parallel", "arbitrary"),
        name="band_%d" % g,
    )(view, view, view, view, view, view, view, bias)
    return o.reshape(b * s, _B_GW), lse.reshape(b * s, _B_GW)


def _prep_c(lam_re, lam_im, log_dt, b_re, b_im, c_re, c_im):
    hp = lax.Precision.HIGHEST
    L, P, C = C_CHUNK, C_STATE, C_GROUP
    lr, li = lam_re.astype(F32), lam_im.astype(F32)
    dt = jnp.exp(log_dt.astype(F32))[..., None]
    mag = jnp.exp(lr * dt)
    ar, ai = mag * jnp.cos(li * dt), mag * jnp.sin(li * dt)
    den = lr * lr + li * li
    zr = ((ar - 1.0) * lr + ai * li) / den
    zi = (ai * lr - (ar - 1.0) * li) / den
    br, bi = b_re.astype(F32), b_im.astype(F32)
    bbr = zr[..., None] * br - zi[..., None] * bi
    bbi = zr[..., None] * bi + zi[..., None] * br
    cr, ci = c_re.astype(F32), c_im.astype(F32)
    prs, pis = [jnp.ones_like(ar)], [jnp.zeros_like(ai)]
    for _ in range(L):
        pr_, pi_ = prs[-1], pis[-1]
        prs.append(ar * pr_ - ai * pi_)
        pis.append(ar * pi_ + ai * pr_)
    pr, pi = jnp.stack(prs, 0), jnp.stack(pis, 0)

    def lag(pr_k, pi_k):
        tr = pr_k[..., None] * bbr - pi_k[..., None] * bbi
        ti = pr_k[..., None] * bbi + pi_k[..., None] * bbr
        return (jnp.einsum('dgop,kdgpi->kdgoi', cr, tr, precision=hp)
                - jnp.einsum('dgop,kdgpi->kdgoi', ci, ti, precision=hp))

    kern = lag(pr[:L], pi[:L])
    jj = np.arange(L)[:, None]
    ii = np.arange(L)[None, :]
    kf = kern[:, 0][np.clip(ii - jj, 0, L - 1)] * jnp.asarray((ii >= jj)[:, :, None, None, None], F32)
    kb = kern[:, 1][np.clip(jj - ii, 0, L - 1)] * jnp.asarray((jj >= ii)[:, :, None, None, None], F32)
    m_intra = jnp.transpose(kf + kb, (2, 0, 4, 1, 3)).reshape(C_N_GROUPS, L * C, L * C)

    def summ(d, powers):
        pr_k, pi_k = pr[powers, d], pi[powers, d]
        sr = pr_k[..., None] * bbr[d] - pi_k[..., None] * bbi[d]
        si = pr_k[..., None] * bbi[d] + pi_k[..., None] * bbr[d]
        s = jnp.concatenate([sr, si], 2)
        return jnp.transpose(s, (1, 0, 3, 2)).reshape(C_N_GROUPS, L * C, 2 * P)

    w_sum = jnp.concatenate([summ(0, np.arange(L - 1, -1, -1)), summ(1, np.arange(L))], -1)

    def outw(d, powers):
        pr_k, pi_k = pr[powers, d], pi[powers, d]
        wr = cr[d][None] * pr_k[:, :, None, :] - ci[d][None] * pi_k[:, :, None, :]
        wi = -(cr[d][None] * pi_k[:, :, None, :] + ci[d][None] * pr_k[:, :, None, :])
        w = jnp.concatenate([wr, wi], -1)
        return jnp.transpose(w, (1, 3, 0, 2)).reshape(C_N_GROUPS, 2 * P, L * C)

    w_state = jnp.concatenate([outw(0, np.arange(1, L + 1)), outw(1, np.arange(L, 0, -1))], 1)

    nq = C_N_GROUPS // _C_GL
    npair = _C_GL // 2
    eg, ek, e2 = jnp.eye(_C_GL, dtype=F32), jnp.eye(npair, dtype=F32), jnp.eye(2, dtype=F32)
    blk = L * _C_GL * C
    w_intra = jnp.einsum('qgjcid,gh->qjgcihd', m_intra.reshape(nq, _C_GL, L, C, L, C), eg).reshape(nq, blk, blk)
    w_sum = jnp.einsum('qkgjcrp,kK,gH->qjkgcKrHp', w_sum.reshape(nq, npair, 2, L, C, 4, P), ek, e2)
    w_sum = w_sum.reshape(nq, blk, _C_GL * 4 * P)
    w_state = jnp.einsum('qkgrpid,kK,gH->qkrgpiKHd', w_state.reshape(nq, npair, 2, 4, P, L, C), ek, e2)
    w_state = w_state.reshape(nq, _C_GL * 4 * P, blk)
    dec = jnp.stack([pr[L, 0], pi[L, 0], pr[L, 1], pi[L, 1]], 0).reshape(4, nq * npair, 2 * P)
    dec = jnp.broadcast_to(jnp.transpose(dec, (1, 0, 2))[:, :, None, :], (nq * npair, 4, 8, 2 * P))
    return w_sum.astype(BF), w_intra.astype(BF), w_state.astype(BF), dec


def _chunk_rows(x_ref, nb, rc):
    L = C_CHUNK
    rows = [jnp.concatenate([x_ref[b, pl.ds(j, rc, stride=L), :].astype(BF) for j in range(L)], axis=1)
            for b in range(nb)]
    return jnp.concatenate(rows, axis=0)


def _s5_sum_kernel(x_ref, w_ref, s_ref, *, nb, rc):
    s = _dot(_chunk_rows(x_ref, nb, rc), w_ref[...])
    for b in range(nb):
        for t in range(s_ref.shape[0]):
            s_ref[t, pl.ds(b, rc, stride=nb), :] = s[b * rc:(b + 1) * rc, LANES * t:LANES * (t + 1)]


def _s5_scan_kernel(s_ref, dec_ref, e_ref, *, nc, nb):
    units = s_ref.shape[0] // 4
    dec = [[dec_ref[u, k, 0:nb, :] for k in range(4)] for u in range(units)]

    def body(t, carry):
        rf = pl.ds(pl.multiple_of(t * nb, nb), nb)
        rb = pl.ds(pl.multiple_of((nc - 1 - t) * nb, nb), nb)
        new = []
        for u in range(units):
            fr, fi, br, bi = carry[4 * u:4 * u + 4]
            arf, aif, arb, aib = dec[u]
            e_ref[4 * u, rf, :] = fr
            e_ref[4 * u + 1, rf, :] = fi
            e_ref[4 * u + 2, rb, :] = br
            e_ref[4 * u + 3, rb, :] = bi
            new += [fr * arf - fi * aif + s_ref[4 * u, rf, :], fi * arf + fr * aif + s_ref[4 * u + 1, rf, :],
                    br * arb - bi * aib + s_ref[4 * u + 2, rb, :], bi * arb + br * aib + s_ref[4 * u + 3, rb, :]]
        return tuple(new)

    z = jnp.zeros((nb, LANES), F32)
    lax.fori_loop(0, nc, body, (z,) * (4 * units))


def _s5_out_kernel(x_ref, e_ref, wi_ref, wc_ref, y_ref, *, nb, rc):
    L = C_CHUNK
    half = pl.program_id(2)
    xc = _chunk_rows(x_ref, nb, rc)
    ec = jnp.concatenate(
        [jnp.concatenate([e_ref[t, pl.ds(b, rc, stride=nb), :].astype(BF) for t in range(e_ref.shape[0])], axis=1)
         for b in range(nb)], axis=0)
    y = _dot(xc, wi_ref[...]) + _dot(ec, wc_ref[...])
    for b in range(nb):
        for ii in range(L // 2):
            y_ref[b, pl.ds(half * (L // 2) + ii, rc, stride=L), :] = y[b * rc:(b + 1) * rc, LANES * ii:LANES * (ii + 1)]


def _s5(x, b, s, w_sum, w_intra, w_state, dec):
    L = C_CHUNK
    nc = s // L
    nq = D_MODEL // LANES
    nsl = w_sum.shape[2] // LANES
    blk = L * LANES
    rc = _tile(nc, max(8, 512 // b))
    x3 = x.reshape(b, s, D_MODEL)
    x_spec = pl.BlockSpec((b, rc * L, LANES), lambda q, c, *_: (0, c, q))
    sums = pl.pallas_call(
        functools.partial(_s5_sum_kernel, nb=b, rc=rc),
        grid=(nq, nc // rc),
        in_specs=[x_spec, pl.BlockSpec((None, blk, nsl * LANES), lambda q, c: (q, 0, 0))],
        out_specs=pl.BlockSpec((nsl, rc * b, LANES), lambda q, c: (q, c, 0)),
        out_shape=jax.ShapeDtypeStruct((nq * nsl, nc * b, LANES), F32),
        compiler_params=_params("parallel", "arbitrary"),
        name="s5_sum",
    )(x3, w_sum)
    upb = 2
    ent = pl.pallas_call(
        functools.partial(_s5_scan_kernel, nc=nc, nb=b),
        grid=(nq * nsl // (4 * upb),),
        in_specs=[pl.BlockSpec((4 * upb, nc * b, LANES), lambda i: (i, 0, 0)),
                  pl.BlockSpec((upb, 4, 8, LANES), lambda i: (i, 0, 0, 0))],
        out_specs=pl.BlockSpec((4 * upb, nc * b, LANES), lambda i: (i, 0, 0)),
        out_shape=jax.ShapeDtypeStruct((nq * nsl, nc * b, LANES), F32),
        compiler_params=_params("parallel"),
        name="s5_scan",
    )(sums, dec)
    y = pl.pallas_call(
        functools.partial(_s5_out_kernel, nb=b, rc=rc),
        grid=(nq, nc // rc, 2),
        in_specs=[x_spec,
                  pl.BlockSpec((nsl, rc * b, LANES), lambda q, c, h: (q, c, 0)),
                  pl.BlockSpec((None, blk, blk // 2), lambda q, c, h: (q, 0, h)),
                  pl.BlockSpec((None, nsl * LANES, blk // 2), lambda q, c, h: (q, 0, h))],
        out_specs=pl.BlockSpec((b, rc * L, LANES), lambda q, c, h: (0, c, q)),
        out_shape=jax.ShapeDtypeStruct((b, s, D_MODEL), F32),
        compiler_params=_params("parallel", "arbitrary", "arbitrary"),
        name="s5_out",
    )(x3, ent, w_intra, w_state)
    return y.reshape(b * s, D_MODEL)


_D_T = 512


def _prep_d(w_qkv, norm_gain, w_o, lambda_init):
    qk_w = D_HEADS * 2 * D_QK_DIM
    wq = _pad_heads(w_qkv[:, :qk_w] * (D_QK_DIM ** -0.5 * LOG2E), D_HEADS, 2 * D_QK_DIM)
    wk = _pad_heads(w_qkv[:, qk_w:2 * qk_w], D_HEADS, 2 * D_QK_DIM)
    wv = _pad_heads(w_qkv[:, 2 * qk_w:], D_HEADS, D_V_DIM)
    w = jnp.concatenate([wq, wk, wv], 1).astype(BF)
    gain = jnp.concatenate([norm_gain * (1.0 - lambda_init), jnp.zeros((LANES - D_V_DIM,), F32)])[None, :]
    wo = w_o.reshape(D_HEADS, D_V_DIM, D_MODEL)
    wo = jnp.concatenate([wo, jnp.zeros_like(wo)], 1).reshape(D_HEADS * LANES, D_MODEL).astype(BF)
    return w, gain, wo


def _diff_bias(rel_bias, t):
    rel0 = (jnp.arange(2 * t) + t) % (2 * t) - t
    rel = jnp.arange(-2, 3)[:, None] * t + rel0[None, :]
    vec = jnp.transpose(rel_bias[_bucket(rel)] * LOG2E, (2, 0, 1)).astype(F32)
    flat = jnp.tile(vec, (1, 1, t))[..., :t * (2 * t - 1)]
    return flat.reshape(vec.shape[0], 5, t, 2 * t - 1)[..., :t]


def _flash_d_kernel(q_ref, k_ref, v_ref, bias_ref, lam_ref, gain_ref, o_ref, m_ref, l_ref, acc_ref, *, nk,
                    lambda_init):
    t = q_ref.shape[0]
    i = pl.program_id(2)
    qv = q_ref[...].astype(F32)
    lane = lax.broadcasted_iota(I32, (t, LANES), 1)
    q0 = jnp.where(lane < D_QK_DIM, qv, 0.0).astype(BF)
    q1 = jnp.where((lane >= D_QK_DIM) & (lane < 2 * D_QK_DIM), qv, 0.0).astype(BF)
    q = jnp.concatenate([q0, q1], axis=0)
    _softmax_init(m_ref, l_ref, acc_ref)

    def body(j, carry):
        off = pl.multiple_of(j * t, t)
        bias = bias_ref[jnp.clip(j - i, -2, 2) + 2]
        s = _dot_nt(q, k_ref[pl.ds(off, t), :])
        s = (s.reshape(2, t, t) + bias[None]).reshape(2 * t, t)
        _softmax_step(s, v_ref[pl.ds(off, t), :], m_ref, l_ref, acc_ref)
        return carry

    lax.fori_loop(0, nk, body, 0, unroll=2)
    lf = lam_ref[...]
    lam = (jnp.exp(jnp.sum(lf[0:1] * lf[1:2], keepdims=True))
           - jnp.exp(jnp.sum(lf[2:3] * lf[3:4], keepdims=True)) + lambda_init)
    on = acc_ref[...] / jnp.sum(l_ref[...], -1, keepdims=True)
    o = on[:t] - lam * on[t:]
    ms = jnp.sum(o * o, -1, keepdims=True) * (1.0 / D_V_DIM)
    o_ref[...] = (o * lax.rsqrt(ms + RMS_EPS) * gain_ref[...]).astype(BF)


def _flash_d(qkv, bias, lam, gain, lambda_init):
    b, s, _ = qkv.shape
    t = bias.shape[-1]
    return pl.pallas_call(
        functools.partial(_flash_d_kernel, nk=s // t, lambda_init=lambda_init),
        grid=(b, D_HEADS, s // t),
        in_specs=[pl.BlockSpec((None, t, LANES), lambda bi, h, i: (bi, i, h)),
                  pl.BlockSpec((None, s, LANES), lambda bi, h, i: (bi, 0, D_HEADS + h)),
                  pl.BlockSpec((None, s, LANES), lambda bi, h, i: (bi, 0, 2 * D_HEADS + h)),
                  pl.BlockSpec((None, 5, t, t), lambda bi, h, i: (h, 0, 0, 0)),
                  pl.BlockSpec((4, D_QK_DIM), lambda bi, h, i: (0, 0)),
                  pl.BlockSpec((1, LANES), lambda bi, h, i: (0, 0))],
        out_specs=pl.BlockSpec((None, t, LANES), lambda bi, h, i: (bi, i, h)),
        out_shape=jax.ShapeDtypeStruct((b, s, D_HEADS * LANES), BF),
        scratch_shapes=[pltpu.VMEM((2 * t, LANES), F32)] * 3,
        compiler_params=_params("parallel", "parallel", "arbitrary"),
        name="flash_d",
    )(qkv, qkv, qkv, bias, lam, gain)


def _cross_kernel(x_ref, kv_ref, wq_ref, wo_ref, g_ref, b_ref, wr_ref, y_ref, ybf_ref, aff_ref):
    x = x_ref[...]
    tm = x.shape[0]
    q = (_dot(x.astype(BF), wq_ref[...]) * (X_HEAD_DIM ** -0.5)).astype(BF)
    outs = []
    for h in range(X_HEADS):
        sl = slice(X_HEAD_DIM * h, X_HEAD_DIM * (h + 1))
        s = _dot_nt(q[:, sl], kv_ref[:, sl])
        p = jnp.exp(s - jnp.max(s, -1, keepdims=True))
        l = jnp.sum(p, -1, keepdims=True)
        vh = kv_ref[:, D_MODEL + X_HEAD_DIM * h:D_MODEL + X_HEAD_DIM * (h + 1)]
        outs.append((_dot(p.astype(BF), vh) / l).astype(BF))
    o = jnp.concatenate(outs, axis=1)
    y = _ln_rows(ALPHA * x + _dot(o, wo_ref[...]), g_ref[...], b_ref[...])
    y_ref[...] = y
    yh = y.astype(BF)
    ybf_ref[...] = yh
    yl = (y - yh.astype(F32)).astype(BF)
    wr = wr_ref[...]
    wh = wr.astype(BF)
    wl = (wr - wh.astype(F32)).astype(BF)
    lg = _dot_nt(wh, yh) + _dot_nt(wh, yl) + _dot_nt(wl, yh)
    e = jnp.exp(lg - jnp.max(lg, 0, keepdims=True))
    aff = e / jnp.sum(e, 0, keepdims=True)
    for c in range(tm // LANES):
        aff_ref[c] = aff[:, LANES * c:LANES * (c + 1)]


def _cross(x, kv, wq, wo, g, b, wr_t, s, mem_len, tm=512):
    n = x.shape[0]
    tm = _tile(s, tm)
    per = s // tm
    fixed = lambda i: (0, 0)
    return pl.pallas_call(
        _cross_kernel,
        grid=(n // tm,),
        in_specs=[pl.BlockSpec((tm, D_MODEL), lambda i: (i, 0)),
                  pl.BlockSpec((mem_len, 2 * D_MODEL), lambda i: (i // per, 0)),
                  pl.BlockSpec((D_MODEL, D_MODEL), fixed), pl.BlockSpec((D_MODEL, D_MODEL), fixed),
                  pl.BlockSpec((1, D_MODEL), fixed), pl.BlockSpec((1, D_MODEL), fixed),
                  pl.BlockSpec((N_EXPERTS, D_MODEL), fixed)],
        out_specs=[pl.BlockSpec((tm, D_MODEL), lambda i: (i, 0)),
                   pl.BlockSpec((tm, D_MODEL), lambda i: (i, 0)),
                   pl.BlockSpec((tm // LANES, N_EXPERTS, LANES), lambda i: (i, 0, 0))],
        out_shape=[jax.ShapeDtypeStruct((n, D_MODEL), F32),
                   jax.ShapeDtypeStruct((n, D_MODEL), BF),
                   jax.ShapeDtypeStruct((n // LANES, N_EXPERTS, LANES), F32)],
        compiler_params=_params("parallel"),
        name="cross",
    )(x, kv, wq, wo, g, b, wr_t)


def _select_kernel(a_ref, pos_ref, st_ref, *, k, nbits):
    nt = a_ref.shape[0]
    shape = (nt, N_EXPERTS, LANES)
    kf = float(k)

    def keys():
        return lax.bitcast_convert_type(a_ref[...], I32)

    def count(mask):
        c = jnp.sum(jnp.where(mask, 1.0, 0.0), axis=0, keepdims=True)
        return jnp.sum(c, axis=2, keepdims=True)

    def value_step(it, thr):
        cand = thr | jnp.left_shift(jnp.int32(1), 30 - it)
        return jnp.where(count(keys() >= cand) >= kf, cand, thr)

    thr = lax.fori_loop(0, 31, value_step, jnp.zeros((1, N_EXPERTS, 1), I32))
    need = kf - count(keys() > thr)
    idx = lax.broadcasted_iota(I32, shape, 0) * LANES + lax.broadcasted_iota(I32, shape, 2)

    def index_step(it, ithr):
        cand = ithr | jnp.left_shift(jnp.int32(1), nbits - 1 - it)
        return jnp.where(count((keys() == thr) & (idx < cand)) < need, cand, ithr)

    ithr = lax.fori_loop(0, nbits, index_step, jnp.zeros((1, N_EXPERTS, 1), I32))
    thr2, ithr2 = thr[0], ithr[0]
    upper = jnp.where(lax.broadcasted_iota(I32, (LANES, LANES), 0) <= lax.broadcasted_iota(I32, (LANES, LANES), 1),
                      1.0, 0.0).astype(BF)
    lane = lax.broadcasted_iota(I32, (N_EXPERTS, LANES), 1)

    def tile_step(j, carry):
        kj = lax.bitcast_convert_type(a_ref[j], I32)
        sel = (kj > thr2) | ((kj == thr2) & (j * LANES + lane <= ithr2))
        m = jnp.where(sel, 1.0, 0.0)
        inc = _dot(m.astype(BF), upper)
        pos_ref[j] = jnp.where(sel, inc - m + carry, -1.0).astype(I32)
        st_ref[j] = jnp.broadcast_to(carry, (N_EXPERTS, LANES)).astype(I32)
        return carry + inc[:, LANES - 1:LANES]

    lax.fori_loop(0, nt, tile_step, jnp.zeros((N_EXPERTS, 1), F32))


def _select(aff3, k):
    nt = aff3.shape[0]
    nbits = max(1, int(math.ceil(math.log2(nt * LANES))))
    shp = jax.ShapeDtypeStruct(aff3.shape, I32)
    return pl.pallas_call(
        functools.partial(_select_kernel, k=k, nbits=nbits),
        out_shape=[shp, shp],
        compiler_params=pltpu.CompilerParams(vmem_limit_bytes=VMEM_LIMIT),
        name="select",
    )(aff3)


_GATHER_ROWS = LANES + BF16_ROWS


def _moe_ffn_kernel(st_ref, x_ref, pos_ref, wg_ref, wu_ref, wd_ref, ye_ref, buf_ref, *, nb, ncf, sub, ck):
    e = pl.program_id(0)
    t = pl.program_id(1)

    @pl.when(t == 0)
    def _():
        buf_ref[...] = jnp.zeros_like(buf_ref)

    @pl.when(t < nb)
    def _():
        row = lax.broadcasted_iota(I32, (_GATHER_ROWS, LANES), 0)
        for s in range(sub):
            st = st_ref[e, t * sub + s]
            base = pl.multiple_of((st // BF16_ROWS) * BF16_ROWS, BF16_ROWS)
            rel = pos_ref[s, pl.ds(e, 1), :] - base
            onehot = jnp.where(row == rel, 1.0, 0.0).astype(BF)
            c = _dot(onehot, x_ref[LANES * s:LANES * (s + 1), :])
            buf_ref[pl.ds(base, _GATHER_ROWS), :] = buf_ref[pl.ds(base, _GATHER_ROWS), :] + c.astype(BF)

    @pl.when(t >= nb)
    def _():
        c = t - nb

        @pl.when(c < ncf)
        def _():
            xe = buf_ref[pl.ds(pl.multiple_of(c * ck, ck), ck), :]
            h = (jax.nn.silu(_dot(xe, wg_ref[...])) * _dot(xe, wu_ref[...])).astype(BF)
            ye_ref[...] = _dot(h, wd_ref[...]).astype(BF)

        @pl.when(c >= ncf)
        def _():
            ye_ref[...] = jnp.zeros_like(ye_ref)


def _moe_ffn(starts, xbf, pos3, wg, wu, wd, cap, tb=1024):
    n = xbf.shape[0]
    tb = _tile(n, tb)
    nb = n // tb
    sub = tb // LANES
    ck = _tile(cap, 512)
    ncf = cap // ck
    nch = ncf + 1
    dff = wg.shape[-1]
    blk = lambda e, t, st: (jnp.minimum(t, nb - 1), 0)
    return pl.pallas_call(
        functools.partial(_moe_ffn_kernel, nb=nb, ncf=ncf, sub=sub, ck=ck),
        grid_spec=pltpu.PrefetchScalarGridSpec(
            num_scalar_prefetch=1,
            grid=(N_EXPERTS, nb + nch),
            in_specs=[pl.BlockSpec((tb, D_MODEL), blk),
                      pl.BlockSpec((sub, N_EXPERTS, LANES), lambda e, t, st: (jnp.minimum(t, nb - 1), 0, 0)),
                      pl.BlockSpec((None, D_MODEL, dff), lambda e, t, st: (e, 0, 0)),
                      pl.BlockSpec((None, D_MODEL, dff), lambda e, t, st: (e, 0, 0)),
                      pl.BlockSpec((None, dff, D_MODEL), lambda e, t, st: (e, 0, 0))],
            out_specs=pl.BlockSpec((None, ck, D_MODEL), lambda e, t, st: (e, jnp.maximum(t - nb, 0), 0)),
            scratch_shapes=[pltpu.VMEM((cap + _GATHER_ROWS, D_MODEL), BF)]),
        out_shape=jax.ShapeDtypeStruct((N_EXPERTS, nch * ck, D_MODEL), BF),
        compiler_params=_params("arbitrary", "arbitrary"),
        name="moe_ffn",
    )(starts, xbf, pos3, wg, wu, wd)


def _combine_kernel(st_ref, x_ref, aff_ref, pos_ref, g_ref, b_ref, *rest):
    ye_refs, y_ref = rest[:N_EXPERTS], rest[N_EXPERTS]
    j = pl.program_id(0)
    col = lax.broadcasted_iota(I32, (LANES, _GATHER_ROWS), 1)
    acc = jnp.zeros((LANES, D_MODEL), F32)
    for e in range(N_EXPERTS):
        base = (st_ref[e, j] // BF16_ROWS) * BF16_ROWS
        rel = pos_ref[:, e:e + 1] - base
        onehot = jnp.where(col == rel, 1.0, 0.0).astype(BF)
        acc = acc + aff_ref[:, e:e + 1] * _dot(onehot, ye_refs[e][0])
    y_ref[...] = _ln_rows(ALPHA * x_ref[...] + acc, g_ref[...], b_ref[...])


def _combine(starts, x, aff, pos, g, b, ye):
    n = x.shape[0]
    row = lambda j, st: (j, 0)
    fixed = lambda j, st: (0, 0)

    def ye_spec(e):
        return pl.BlockSpec((pl.Element(1), pl.Element(_GATHER_ROWS), pl.Element(D_MODEL)),
                            lambda j, st: (e, (st[e, j] // BF16_ROWS) * BF16_ROWS, 0))

    return pl.pallas_call(
        _combine_kernel,
        grid_spec=pltpu.PrefetchScalarGridSpec(
            num_scalar_prefetch=1,
            grid=(n // LANES,),
            in_specs=[pl.BlockSpec((LANES, D_MODEL), row),
                      pl.BlockSpec((LANES, N_EXPERTS), row),
                      pl.BlockSpec((LANES, N_EXPERTS), row),
                      pl.BlockSpec((1, D_MODEL), fixed), pl.BlockSpec((1, D_MODEL), fixed)]
            + [ye_spec(e) for e in range(N_EXPERTS)],
            out_specs=pl.BlockSpec((LANES, D_MODEL), row)),
        out_shape=jax.ShapeDtypeStruct((n, D_MODEL), F32),
        compiler_params=_params("arbitrary"),
        name="combine",
    )(starts, x, aff, pos, g, b, *([ye] * N_EXPERTS))


def _moe(x, xbf, aff3, wg, wu, wd, g, b):
    n = x.shape[0]
    cap = EC_CAPACITY * n // N_EXPERTS
    pos3, st3 = _select(aff3, cap)
    starts = jnp.transpose(st3[:, :, 0])
    ye = _moe_ffn(starts, xbf, pos3, wg, wu, wd, cap)
    aff = jnp.transpose(aff3, (0, 2, 1)).reshape(n, N_EXPERTS)
    pos = jnp.transpose(pos3, (0, 2, 1)).reshape(n, N_EXPERTS)
    return _combine(starts, x, aff, pos, g, b, ye)


def _prep_weights(p):
    w = {}
    w['a'] = [_prep_a(p['a_w_qkv'][j], p['a_q_gain'][j], p['a_k_gain'][j], p['a_w_o'][j])
              for j in range(p['a_w_qkv'].shape[0])]
    w['b'] = [_prep_b(p['b_w_qkv'][j], p['b_w_o'][j]) for j in range(p['b_w_qkv'].shape[0])]
    w['c'] = [_prep_c(p['c_lam_re'][j], p['c_lam_im'][j], p['c_log_dt'][j], p['c_b_re'][j], p['c_b_im'][j],
                      p['c_c_re'][j], p['c_c_im'][j]) + (p['c_d'][j][None, :], p['c_w_glu'][j].astype(BF))
              for j in range(p['c_lam_re'].shape[0])]
    w['d'] = []
    for j in range(p['d_w_qkv'].shape[0]):
        layer = N_MIXERS * j + 3
        lambda_init = 0.8 - 0.6 * math.exp(-0.3 * layer)
        w['d'].append(_prep_d(p['d_w_qkv'][j], p['d_norm_gain'][j], p['d_w_o'][j], lambda_init)
                      + (p['d_lam'][j].astype(F32), lambda_init))
    w['diff_bias'] = _diff_bias(p['rel_bias'], _D_T)
    w['x_w_q'] = p['x_w_q'].astype(BF)
    w['x_w_kv'] = p['x_w_kv'].astype(BF)
    w['x_w_o'] = p['x_w_o'].astype(BF)
    w['router_t'] = jnp.transpose(p['moe_w_router'], (0, 2, 1)).astype(F32)
    w['moe_w_gate'] = p['moe_w_gate'].astype(BF)
    w['moe_w_up'] = p['moe_w_up'].astype(BF)
    w['moe_w_down'] = p['moe_w_down'].astype(BF)
    return w


def _trunk(x, mem, p, w):
    b, s, _ = x.shape
    n = b * s
    mem_len = mem.shape[1]
    x = x.reshape(n, D_MODEL)
    mem2 = mem.reshape(b * mem_len, D_MODEL)
    ln_g, ln_b = p['ln_g'], p['ln_b']
    for i in range(DEPTH):
        m, j = i % N_MIXERS, i // N_MIXERS
        g0, b0 = ln_g[i, 0][None, :], ln_b[i, 0][None, :]
        if m == 0:
            wa, gains, wo = w['a'][j]
            qkv = _proj_a(x, wa, _rope_table(s), gains, s)
            o = _flash_a(qkv.reshape(b, s, _A_COLS))
            x = _post(x, o.reshape(n, A_HEADS * LANES), wo, g0, b0)
        elif m == 1:
            wb, wo = w['b'][j]
            qkv = _proj(x, wb).reshape(b, s, _B_TILES * LANES)
            os_, ls_ = [], []
            for g in range(len(B_PATTERNS)):
                tm = min(LANES, s // B_PATTERNS[g][1])
                o, lse = _band_attention(qkv, _band_bias(p['rel_bias'], g, tm), g, tm)
                os_.append(o)
                ls_.append(lse)
            x = _post_b(x, os_, ls_, wo, g0, b0)
        elif m == 2:
            w_sum, w_intra, w_state, dec, dskip, wglu = w['c'][j]
            ys = _s5(x, b, s, w_sum, w_intra, w_state, dec)
            x = _post_c(x, ys, dskip, wglu, g0, b0)
        else:
            wd, gain, wo, lam, lambda_init = w['d'][j]
            qkv = _proj(x, wd).reshape(b, s, 3 * D_HEADS * LANES)
            o = _flash_d(qkv, w['diff_bias'], lam, gain, lambda_init)
            x = _post(x, o.reshape(n, D_HEADS * LANES), wo, g0, b0)
        kv = _proj(mem2, w['x_w_kv'][i], tm=mem_len)
        x, xbf, aff3 = _cross(x, kv, w['x_w_q'][i], w['x_w_o'][i], ln_g[i, 1][None, :], ln_b[i, 1][None, :],
                              w['router_t'][i], s, mem_len)
        x = _moe(x, xbf, aff3, w['moe_w_gate'][i], w['moe_w_up'][i], w['moe_w_down'][i],
                 ln_g[i, 2][None, :], ln_b[i, 2][None, :])
    return x.reshape(b, s, D_MODEL)


def kernel(x_prompt, x_sample, mem_prompt, mem_sample, rel_bias, ln_g, ln_b, a_w_qkv, a_q_gain, a_k_gain, a_w_o, b_w_qkv, b_w_o, c_lam_re, c_lam_im, c_log_dt, c_b_re, c_b_im, c_c_re, c_c_im, c_d, c_w_glu, d_w_qkv, d_lam, d_norm_gain, d_w_o, x_w_q, x_w_kv, x_w_o, moe_w_router, moe_w_gate, moe_w_up, moe_w_down):
    p = dict(rel_bias=rel_bias, ln_g=ln_g, ln_b=ln_b,
             a_w_qkv=a_w_qkv, a_q_gain=a_q_gain, a_k_gain=a_k_gain, a_w_o=a_w_o,
             b_w_qkv=b_w_qkv, b_w_o=b_w_o,
             c_lam_re=c_lam_re, c_lam_im=c_lam_im, c_log_dt=c_log_dt, c_b_re=c_b_re, c_b_im=c_b_im,
             c_c_re=c_c_re, c_c_im=c_c_im, c_d=c_d, c_w_glu=c_w_glu,
             d_w_qkv=d_w_qkv, d_lam=d_lam, d_norm_gain=d_norm_gain, d_w_o=d_w_o,
             x_w_q=x_w_q, x_w_kv=x_w_kv, x_w_o=x_w_o,
             moe_w_router=moe_w_router, moe_w_gate=moe_w_gate, moe_w_up=moe_w_up, moe_w_down=moe_w_down)
    w = _prep_weights(p)
    return (_trunk(x_prompt, mem_prompt, p, w), _trunk(x_sample, mem_sample, p, w))
```

```python
import functools
import math

import numpy as np
import jax
import jax.numpy as jnp
from jax import lax
from jax.experimental import pallas as pl
from jax.experimental.pallas import tpu as pltpu

F32 = jnp.float32
BF = jnp.bfloat16
I32 = jnp.int32

D_MODEL = 1024
DEPTH = 4
GRID_W = 64
N_MIXERS = 4
LN_EPS = 1e-5
RMS_EPS = 1e-6
ALPHA = (2.0 * DEPTH) ** 0.25

A_HEADS = 16
A_KV_HEADS = 4
A_HEAD_DIM = 64
ROPE_BASE = 10000.0

B_PATTERNS = ((128, 1), (512, 4), (2048, 16))
B_HEADS_PER_GROUP = 4
B_HEAD_DIM = 64

C_GROUP = 16
C_N_GROUPS = D_MODEL // C_GROUP
C_STATE = 64
C_CHUNK = 16
_C_GL = 128 // C_GROUP

D_HEADS = 12
D_QK_DIM = 32
D_V_DIM = 64

REL_BUCKETS = 32
REL_MAX_DIST = 128

X_HEADS = 4
X_HEAD_DIM = D_MODEL // X_HEADS

N_EXPERTS = 16
EC_CAPACITY = 2

LANES = 128
BF16_ROWS = 16
VMEM_LIMIT = 56 * 1024 * 1024
NEG = -1e30
LOG2E = math.log2(math.e)


def _params(*sem):
    return pltpu.CompilerParams(dimension_semantics=sem, vmem_limit_bytes=VMEM_LIMIT)


def _tile(n, pref):
    t = min(n, pref)
    assert n % t == 0, (n, pref)
    return t


def _ln_rows(v, g, b):
    mu = jnp.mean(v, -1, keepdims=True)
    c = v - mu
    var = jnp.mean(c * c, -1, keepdims=True)
    return c * lax.rsqrt(var + LN_EPS) * g + b


def _dot_nt(a, b):
    return lax.dot_general(a, b, (((1,), (1,)), ((), ())), preferred_element_type=F32)


def _dot(a, b):
    return jnp.dot(a, b, preferred_element_type=F32)


def _proj_kernel(x_ref, w_ref, o_ref):
    o_ref[...] = _dot(x_ref[...].astype(BF), w_ref[...]).astype(o_ref.dtype)


def _proj(x, w, tm=512):
    n, k = x.shape
    m = w.shape[1]
    tm = _tile(n, tm)
    return pl.pallas_call(
        _proj_kernel,
        grid=(n // tm,),
        in_specs=[pl.BlockSpec((tm, k), lambda i: (i, 0)),
                  pl.BlockSpec((k, m), lambda i: (0, 0))],
        out_specs=pl.BlockSpec((tm, m), lambda i: (i, 0)),
        out_shape=jax.ShapeDtypeStruct((n, m), BF),
        compiler_params=_params("parallel"),
        name="proj",
    )(x, w)


def _post_kernel(x_ref, o_ref, w_ref, g_ref, b_ref, y_ref):
    h = _dot(o_ref[...], w_ref[...])
    y_ref[...] = _ln_rows(ALPHA * x_ref[...] + h, g_ref[...], b_ref[...])


def _post(x, o, w, g, b, tm=512):
    n = x.shape[0]
    ko = o.shape[1]
    tm = _tile(n, tm)
    return pl.pallas_call(
        _post_kernel,
        grid=(n // tm,),
        in_specs=[pl.BlockSpec((tm, D_MODEL), lambda i: (i, 0)),
                  pl.BlockSpec((tm, ko), lambda i: (i, 0)),
                  pl.BlockSpec((ko, D_MODEL), lambda i: (0, 0)),
                  pl.BlockSpec((1, D_MODEL), lambda i: (0, 0)),
                  pl.BlockSpec((1, D_MODEL), lambda i: (0, 0))],
        out_specs=pl.BlockSpec((tm, D_MODEL), lambda i: (i, 0)),
        out_shape=jax.ShapeDtypeStruct((n, D_MODEL), F32),
        compiler_params=_params("parallel"),
        name="post",
    )(x, o, w, g, b)


def _post_b_kernel(x_ref, o0_ref, o1_ref, o2_ref, l0_ref, l1_ref, l2_ref, w_ref, g_ref, b_ref, y_ref):
    l0, l1, l2 = l0_ref[...], l1_ref[...], l2_ref[...]
    m = jnp.maximum(jnp.maximum(l0, l1), l2)
    e0, e1, e2 = jnp.exp(l0 - m), jnp.exp(l1 - m), jnp.exp(l2 - m)
    inv = 1.0 / (e0 + e1 + e2)
    gw = B_HEADS_PER_GROUP * LANES
    h = _dot((o0_ref[...].astype(F32) * (e0 * inv)).astype(BF), w_ref[0:gw, :])
    h = h + _dot((o1_ref[...].astype(F32) * (e1 * inv)).astype(BF), w_ref[gw:2 * gw, :])
    h = h + _dot((o2_ref[...].astype(F32) * (e2 * inv)).astype(BF), w_ref[2 * gw:3 * gw, :])
    y_ref[...] = _ln_rows(ALPHA * x_ref[...] + h, g_ref[...], b_ref[...])


def _post_b(x, os_, ls_, w, g, b, tm=512):
    n = x.shape[0]
    tm = _tile(n, tm)
    gw = B_HEADS_PER_GROUP * LANES
    row = lambda i: (i, 0)
    fixed = lambda i: (0, 0)
    return pl.pallas_call(
        _post_b_kernel,
        grid=(n // tm,),
        in_specs=[pl.BlockSpec((tm, D_MODEL), row)] + [pl.BlockSpec((tm, gw), row)] * 6
        + [pl.BlockSpec((3 * gw, D_MODEL), fixed), pl.BlockSpec((1, D_MODEL), fixed),
           pl.BlockSpec((1, D_MODEL), fixed)],
        out_specs=pl.BlockSpec((tm, D_MODEL), row),
        out_shape=jax.ShapeDtypeStruct((n, D_MODEL), F32),
        compiler_params=_params("parallel"),
        name="post_b",
    )(x, *os_, *ls_, w, g, b)


def _post_c_kernel(x_ref, ys_ref, d_ref, w_ref, g_ref, b_ref, y_ref):
    x = x_ref[...]
    z = jax.nn.gelu(ys_ref[...] + d_ref[...] * x).astype(BF)
    h = _dot(z, w_ref[...])
    hh = h[:, :D_MODEL] * jax.nn.sigmoid(h[:, D_MODEL:])
    y_ref[...] = _ln_rows(ALPHA * x + hh, g_ref[...], b_ref[...])


def _post_c(x, ys, d, w, g, b, tm=512):
    n = x.shape[0]
    tm = _tile(n, tm)
    row = lambda i: (i, 0)
    fixed = lambda i: (0, 0)
    return pl.pallas_call(
        _post_c_kernel,
        grid=(n // tm,),
        in_specs=[pl.BlockSpec((tm, D_MODEL), row), pl.BlockSpec((tm, D_MODEL), row),
                  pl.BlockSpec((1, D_MODEL), fixed), pl.BlockSpec((D_MODEL, 2 * D_MODEL), fixed),
                  pl.BlockSpec((1, D_MODEL), fixed), pl.BlockSpec((1, D_MODEL), fixed)],
        out_specs=pl.BlockSpec((tm, D_MODEL), row),
        out_shape=jax.ShapeDtypeStruct((n, D_MODEL), F32),
        compiler_params=_params("parallel"),
        name="post_c",
    )(x, ys, d, w, g, b)


_A_QK_TILES = A_HEADS + A_KV_HEADS
_A_COLS = (_A_QK_TILES + A_KV_HEADS) * LANES


def _rope_partner():
    d = np.arange(A_HEAD_DIM)
    e = d % (A_HEAD_DIM // 2)
    lo = e < A_HEAD_DIM // 4
    return np.where(lo, d + A_HEAD_DIM // 4, d - A_HEAD_DIM // 4), np.where(lo, -1.0, 1.0).astype(np.float32)


def _prep_a(w_qkv, q_gain, k_gain, w_o):
    partner, sign = _rope_partner()
    nqk = _A_QK_TILES * A_HEAD_DIM
    wqk = w_qkv[:, :nqk].reshape(D_MODEL, _A_QK_TILES, A_HEAD_DIM)
    wsw = wqk[:, :, partner] * sign
    wqk = jnp.concatenate([wqk, wsw], -1).reshape(D_MODEL, _A_QK_TILES * LANES)
    wv = w_qkv[:, nqk:].reshape(D_MODEL, A_KV_HEADS, A_HEAD_DIM)
    wv = jnp.concatenate([wv, jnp.zeros_like(wv)], -1).reshape(D_MODEL, A_KV_HEADS * LANES)
    w = jnp.concatenate([wqk, wv], 1).astype(BF)
    gq = jnp.concatenate([q_gain, q_gain[partner]]) * (A_HEAD_DIM ** -0.5 * 0.5 * LOG2E)
    gk = jnp.concatenate([k_gain, k_gain[partner]])
    gains = jnp.stack([gq, gk], 0)
    wo = w_o.reshape(A_HEADS, A_HEAD_DIM, D_MODEL)
    wo = jnp.concatenate([wo, jnp.zeros_like(wo)], 1).reshape(A_HEADS * LANES, D_MODEL).astype(BF)
    return w, gains, wo


def _rope_table(s):
    pos = jnp.arange(s)
    rows, cols = (pos // GRID_W).astype(F32), (pos % GRID_W).astype(F32)
    half = A_HEAD_DIM // 2
    freqs = ROPE_BASE ** (-jnp.arange(0, half, 2, dtype=F32) / half)
    ang_r = rows[:, None] * freqs
    ang_c = cols[:, None] * freqs
    ang = jnp.concatenate([ang_r, ang_r, ang_c, ang_c], -1)
    return jnp.concatenate([jnp.cos(ang), jnp.sin(ang)], -1)


def _proj_a_kernel(x_ref, w_ref, cs_ref, g_ref, o_ref):
    y = _dot(x_ref[...].astype(BF), w_ref[...])
    cs = cs_ref[...]
    gq = cs * g_ref[0:1, :]
    gk = cs * g_ref[1:2, :]
    for h in range(_A_QK_TILES):
        t = y[:, LANES * h:LANES * (h + 1)]
        r = lax.rsqrt(jnp.sum(t * t, -1, keepdims=True) * (1.0 / LANES) + RMS_EPS)
        e = t * r * (gq if h < A_HEADS else gk)
        o_ref[:, LANES * h:LANES * (h + 1)] = (e + pltpu.roll(e, LANES // 2, 1)).astype(BF)
    o_ref[:, _A_QK_TILES * LANES:] = y[:, _A_QK_TILES * LANES:].astype(BF)


def _proj_a(x, w, cs, gains, s, tm=512):
    n = x.shape[0]
    tm = _tile(s, tm)
    per = s // tm
    return pl.pallas_call(
        _proj_a_kernel,
        grid=(n // tm,),
        in_specs=[pl.BlockSpec((tm, D_MODEL), lambda i: (i, 0)),
                  pl.BlockSpec((D_MODEL, _A_COLS), lambda i: (0, 0)),
                  pl.BlockSpec((tm, LANES), lambda i: (i % per, 0)),
                  pl.BlockSpec((2, LANES), lambda i: (0, 0))],
        out_specs=pl.BlockSpec((tm, _A_COLS), lambda i: (i, 0)),
        out_shape=jax.ShapeDtypeStruct((n, _A_COLS), BF),
        compiler_params=_params("parallel"),
        name="proj_a",
    )(x, w, cs, gains)


def _softmax_step(s, v, m_ref, l_ref, acc_ref):
    tk = s.shape[1]
    m_old = m_ref[...]
    m_new = jnp.maximum(m_old, jnp.max(s, -1, keepdims=True))
    a = jnp.exp2(m_old - m_new)
    psum = None
    chunks = []
    for c in range(tk // LANES):
        pc = jnp.exp2(s[:, LANES * c:LANES * (c + 1)] - m_new)
        psum = pc if psum is None else psum + pc
        chunks.append(pc.astype(BF))
    p = jnp.concatenate(chunks, axis=1)
    l_ref[...] = a * l_ref[...] + psum
    acc_ref[...] = a * acc_ref[...] + _dot(p, v)
    m_ref[...] = m_new


def _softmax_init(m_ref, l_ref, acc_ref):
    m_ref[...] = jnp.full(m_ref.shape, -jnp.inf, F32)
    l_ref[...] = jnp.zeros_like(l_ref)
    acc_ref[...] = jnp.zeros_like(acc_ref)


def _flash_a_kernel(q_ref, k_ref, v_ref, o_ref, m_ref, l_ref, acc_ref, *, tk, nk):
    tq = q_ref.shape[0]
    rep = A_HEADS // A_KV_HEADS
    q = jnp.concatenate([q_ref[:, LANES * r:LANES * (r + 1)] for r in range(rep)], axis=0)
    _softmax_init(m_ref, l_ref, acc_ref)

    def body(j, carry):
        off = pl.multiple_of(j * tk, tk)
        _softmax_step(_dot_nt(q, k_ref[pl.ds(off, tk), :]), v_ref[pl.ds(off, tk), :], m_ref, l_ref, acc_ref)
        return carry

    lax.fori_loop(0, nk, body, 0, unroll=2)
    o = acc_ref[...] / jnp.sum(l_ref[...], -1, keepdims=True)
    for r in range(rep):
        o_ref[:, LANES * r:LANES * (r + 1)] = o[r * tq:(r + 1) * tq].astype(BF)


def _flash_a(qkv, tq=256, tk=1024):
    b, s, _ = qkv.shape
    tq, tk = _tile(s, tq), _tile(s, tk)
    rep = A_HEADS // A_KV_HEADS
    gw = rep * LANES
    return pl.pallas_call(
        functools.partial(_flash_a_kernel, tk=tk, nk=s // tk),
        grid=(b, A_KV_HEADS, s // tq),
        in_specs=[pl.BlockSpec((None, tq, gw), lambda bi, g, i: (bi, i, g)),
                  pl.BlockSpec((None, s, LANES), lambda bi, g, i: (bi, 0, A_HEADS + g)),
                  pl.BlockSpec((None, s, LANES), lambda bi, g, i: (bi, 0, _A_QK_TILES + g))],
        out_specs=pl.BlockSpec((None, tq, gw), lambda bi, g, i: (bi, i, g)),
        out_shape=jax.ShapeDtypeStruct((b, s, A_HEADS * LANES), BF),
        scratch_shapes=[pltpu.VMEM((rep * tq, LANES), F32)] * 3,
        compiler_params=_params("parallel", "parallel", "arbitrary"),
        name="flash_a",
    )(qkv, qkv, qkv)


def _bucket(rel):
    half = REL_BUCKETS // 2
    max_exact = half // 2
    n = jnp.abs(rel)
    large = max_exact + (jnp.log(jnp.maximum(n, 1).astype(F32) / max_exact)
                         / math.log(REL_MAX_DIST / max_exact) * (half - max_exact)).astype(I32)
    large = jnp.minimum(large, half - 1)
    return jnp.where(rel > 0, half, 0) + jnp.where(n < max_exact, n, large)


_B_TILES = 3 * len(B_PATTERNS) * B_HEADS_PER_GROUP
_B_GW = B_HEADS_PER_GROUP * LANES


def _pad_heads(w, heads, dim):
    w = w.reshape(w.shape[0], heads, dim)
    return jnp.concatenate([w, jnp.zeros((w.shape[0], heads, LANES - dim), w.dtype)], -1).reshape(
        w.shape[0], heads * LANES)


def _prep_b(w_qkv, w_o):
    nh = len(B_PATTERNS) * B_HEADS_PER_GROUP
    c = nh * B_HEAD_DIM
    wq = _pad_heads(w_qkv[:, :c] * (B_HEAD_DIM ** -0.5), nh, B_HEAD_DIM)
    wk = _pad_heads(w_qkv[:, c:2 * c], nh, B_HEAD_DIM)
    wv = _pad_heads(w_qkv[:, 2 * c:], nh, B_HEAD_DIM)
    w = jnp.concatenate([wq, wk, wv], 1).astype(BF)
    wo = w_o.reshape(nh, B_HEAD_DIM, D_MODEL)
    wo = jnp.concatenate([wo, jnp.zeros_like(wo)], 1).reshape(nh * LANES, D_MODEL).astype(BF)
    return w, wo


def _band_bias(rel_bias, g, tm):
    window, dil = B_PATTERNS[g]
    n_side = window // (2 * dil)
    j = jnp.arange(3 * tm)[None, :] - tm - jnp.arange(tm)[:, None]
    inband = jnp.abs(j) <= n_side
    bias = rel_bias[_bucket(dil * j)][:, :, g * B_HEADS_PER_GROUP:(g + 1) * B_HEADS_PER_GROUP]
    bias = jnp.where(inband[:, :, None], bias, NEG)
    return jnp.transpose(bias, (2, 0, 1)).astype(F32)


def _band_kernel(q_ref, kp_ref, kc_ref, kn_ref, vp_ref, vc_ref, vn_ref, bias_ref, o_ref, lse_ref, *, length):
    tm = q_ref.shape[0]
    mi = pl.program_id(2)
    key_pos = (mi - 1) * tm + lax.broadcasted_iota(I32, (1, 3 * tm), 1)
    valid = (key_pos >= 0) & (key_pos < length)
    for h in range(B_HEADS_PER_GROUP):
        sl = slice(LANES * h, LANES * (h + 1))
        k = jnp.concatenate([kp_ref[:, sl], kc_ref[:, sl], kn_ref[:, sl]], axis=0)
        v = jnp.concatenate([vp_ref[:, sl], vc_ref[:, sl], vn_ref[:, sl]], axis=0)
        s = _dot_nt(q_ref[:, sl], k) + bias_ref[h]
        s = jnp.where(valid, s, NEG)
        m = jnp.max(s, -1, keepdims=True)
        p = jnp.exp(s - m)
        l = jnp.sum(p, -1, keepdims=True)
        o_ref[:, sl] = (_dot(p.astype(BF), v) / l).astype(BF)
        lse_ref[:, sl] = jnp.broadcast_to(m + jnp.log(l), (tm, LANES))


def _band_attention(qkv, bias, g, tm):
    b, s, c = qkv.shape
    dil = B_PATTERNS[g][1]
    length = s // dil
    nblk = c // _B_GW
    view = qkv.reshape(b, length, dil * c)
    nt = length // tm
    ng = len(B_PATTERNS)

    def spec(which, shift):
        def imap(bi, r, mi):
            return (bi, jnp.clip(mi + shift, 0, nt - 1), r * nblk + which * ng + g)
        return pl.BlockSpec((None, tm, _B_GW), imap)

    out_spec = pl.BlockSpec((None, tm, _B_GW), lambda bi, r, mi: (bi, mi, r))
    o, lse = pl.pallas_call(
        functools.partial(_band_kernel, length=length),
        grid=(b, dil, nt),
        in_specs=[spec(0, 0), spec(1, -1), spec(1, 0), spec(1, 1), spec(2, -1), spec(2, 0), spec(2, 1),
                  pl.BlockSpec((B_HEADS_PER_GROUP, tm, 3 * tm), lambda bi, r, mi: (0, 0, 0))],
        out_specs=[out_spec, out_spec],
        out_shape=[jax.ShapeDtypeStruct((b, length, dil * _B_GW), BF),
                   jax.ShapeDtypeStruct((b, length, dil * _B_GW), F32)],
        compiler_params=_params("parallel", "parallel", "arbitrary"),
        name="band_%d" % g,
    )(view, view, view, view, view, view, view, bias)
    return o.reshape(b * s, _B_GW), lse.reshape(b * s, _B_GW)


def _prep_c(lam_re, lam_im, log_dt, b_re, b_im, c_re, c_im):
    hp = lax.Precision.HIGHEST
    L, P, C = C_CHUNK, C_STATE, C_GROUP
    lr, li = lam_re.astype(F32), lam_im.astype(F32)
    dt = jnp.exp(log_dt.astype(F32))[..., None]
    mag = jnp.exp(lr * dt)
    ar, ai = mag * jnp.cos(li * dt), mag * jnp.sin(li * dt)
    den = lr * lr + li * li
    zr = ((ar - 1.0) * lr + ai * li) / den
    zi = (ai * lr - (ar - 1.0) * li) / den
    br, bi = b_re.astype(F32), b_im.astype(F32)
    bbr = zr[..., None] * br - zi[..., None] * bi
    bbi = zr[..., None] * bi + zi[..., None] * br
    cr, ci = c_re.astype(F32), c_im.astype(F32)
    prs, pis = [jnp.ones_like(ar)], [jnp.zeros_like(ai)]
    for _ in range(L):
        pr_, pi_ = prs[-1], pis[-1]
        prs.append(ar * pr_ - ai * pi_)
        pis.append(ar * pi_ + ai * pr_)
    pr, pi = jnp.stack(prs, 0), jnp.stack(pis, 0)

    def lag(pr_k, pi_k):
        tr = pr_k[..., None] * bbr - pi_k[..., None] * bbi
        ti = pr_k[..., None] * bbi + pi_k[..., None] * bbr
        return (jnp.einsum('dgop,kdgpi->kdgoi', cr, tr, precision=hp)
                - jnp.einsum('dgop,kdgpi->kdgoi', ci, ti, precision=hp))

    kern = lag(pr[:L], pi[:L])
    jj = np.arange(L)[:, None]
    ii = np.arange(L)[None, :]
    kf = kern[:, 0][np.clip(ii - jj, 0, L - 1)] * jnp.asarray((ii >= jj)[:, :, None, None, None], F32)
    kb = kern[:, 1][np.clip(jj - ii, 0, L - 1)] * jnp.asarray((jj >= ii)[:, :, None, None, None], F32)
    m_intra = jnp.transpose(kf + kb, (2, 0, 4, 1, 3)).reshape(C_N_GROUPS, L * C, L * C)

    def summ(d, powers):
        pr_k, pi_k = pr[powers, d], pi[powers, d]
        sr = pr_k[..., None] * bbr[d] - pi_k[..., None] * bbi[d]
        si = pr_k[..., None] * bbi[d] + pi_k[..., None] * bbr[d]
        s = jnp.concatenate([sr, si], 2)
        return jnp.transpose(s, (1, 0, 3, 2)).reshape(C_N_GROUPS, L * C, 2 * P)

    w_sum = jnp.concatenate([summ(0, np.arange(L - 1, -1, -1)), summ(1, np.arange(L))], -1)

    def outw(d, powers):
        pr_k, pi_k = pr[powers, d], pi[powers, d]
        wr = cr[d][None] * pr_k[:, :, None, :] - ci[d][None] * pi_k[:, :, None, :]
        wi = -(cr[d][None] * pi_k[:, :, None, :] + ci[d][None] * pr_k[:, :, None, :])
        w = jnp.concatenate([wr, wi], -1)
        return jnp.transpose(w, (1, 3, 0, 2)).reshape(C_N_GROUPS, 2 * P, L * C)

    w_state = jnp.concatenate([outw(0, np.arange(1, L + 1)), outw(1, np.arange(L, 0, -1))], 1)

    nq = C_N_GROUPS // _C_GL
    npair = _C_GL // 2
    blk = L * _C_GL * C
    lane = jnp.arange(LANES)
    gsel = (lane[None, :] // C == jnp.arange(_C_GL)[:, None])
    src = jnp.tile(m_intra.astype(BF).reshape(nq, _C_GL, L, C, L, C), (1, 1, 1, 1, 1, _C_GL))
    src = jnp.where(gsel[None, :, None, None, None, :], src, 0)
    w_intra = jnp.transpose(src, (0, 2, 1, 3, 4, 5)).reshape(nq, blk, blk)
    psel = (lane[None, :] // P == jnp.arange(2)[:, None])
    ksel = jnp.eye(npair, dtype=bool)
    src = jnp.tile(w_sum.astype(BF).reshape(nq, npair, 2, L, C, 1, 4, P), (1, 1, 1, 1, 1, npair, 1, 2))
    src = jnp.where(psel[None, None, :, None, None, None, None, :]
                    & ksel[None, :, None, None, None, :, None, None], src, 0)
    w_sum = jnp.transpose(src, (0, 3, 1, 2, 4, 5, 6, 7)).reshape(nq, blk, _C_GL * 4 * P)
    osel = (lane[None, None, :] // C
            == (2 * jnp.arange(npair)[:, None, None] + jnp.arange(2)[None, :, None]))
    src = jnp.tile(w_state.astype(BF).reshape(nq, npair, 2, 4, P, L, C), (1, 1, 1, 1, 1, 1, _C_GL))
    src = jnp.where(osel[None, :, :, None, None, None, :], src, 0)
    w_state = jnp.transpose(src, (0, 1, 3, 2, 4, 5, 6)).reshape(nq, _C_GL * 4 * P, blk)
    dec = jnp.stack([pr[L, 0], pi[L, 0], pr[L, 1], pi[L, 1]], 0).reshape(4, nq * npair, 2 * P)
    dec = jnp.broadcast_to(jnp.transpose(dec, (1, 0, 2))[:, :, None, :], (nq * npair, 4, 8, 2 * P))
    return w_sum.astype(BF), w_intra.astype(BF), w_state.astype(BF), dec


def _chunk_rows(x_ref, nb, rc):
    L = C_CHUNK
    rows = [jnp.concatenate([x_ref[b, pl.ds(j, rc, stride=L), :].astype(BF) for j in range(L)], axis=1)
            for b in range(nb)]
    return jnp.concatenate(rows, axis=0)


def _s5_sum_kernel(x_ref, w_ref, s_ref, *, nb, rc):
    s = _dot(_chunk_rows(x_ref, nb, rc), w_ref[...])
    for b in range(nb):
        for t in range(s_ref.shape[0]):
            s_ref[t, pl.ds(b, rc, stride=nb), :] = s[b * rc:(b + 1) * rc, LANES * t:LANES * (t + 1)]


def _s5_scan_kernel(s_ref, dec_ref, e_ref, *, nc, nb):
    units = s_ref.shape[0] // 4
    dec = [[dec_ref[u, k, 0:nb, :] for k in range(4)] for u in range(units)]

    def body(t, carry):
        rf = pl.ds(pl.multiple_of(t * nb, nb), nb)
        rb = pl.ds(pl.multiple_of((nc - 1 - t) * nb, nb), nb)
        new = []
        for u in range(units):
            fr, fi, br, bi = carry[4 * u:4 * u + 4]
            arf, aif, arb, aib = dec[u]
            e_ref[4 * u, rf, :] = fr
            e_ref[4 * u + 1, rf, :] = fi
            e_ref[4 * u + 2, rb, :] = br
            e_ref[4 * u + 3, rb, :] = bi
            new += [fr * arf - fi * aif + s_ref[4 * u, rf, :], fi * arf + fr * aif + s_ref[4 * u + 1, rf, :],
                    br * arb - bi * aib + s_ref[4 * u + 2, rb, :], bi * arb + br * aib + s_ref[4 * u + 3, rb, :]]
        return tuple(new)

    z = jnp.zeros((nb, LANES), F32)
    lax.fori_loop(0, nc, body, (z,) * (4 * units))


def _s5_out_kernel(x_ref, e_ref, wi_ref, wc_ref, y_ref, *, nb, rc):
    L = C_CHUNK
    half = pl.program_id(2)
    xc = _chunk_rows(x_ref, nb, rc)
    ec = jnp.concatenate(
        [jnp.concatenate([e_ref[t, pl.ds(b, rc, stride=nb), :].astype(BF) for t in range(e_ref.shape[0])], axis=1)
         for b in range(nb)], axis=0)
    y = _dot(xc, wi_ref[...]) + _dot(ec, wc_ref[...])
    for b in range(nb):
        for ii in range(L // 2):
            y_ref[b, pl.ds(half * (L // 2) + ii, rc, stride=L), :] = y[b * rc:(b + 1) * rc, LANES * ii:LANES * (ii + 1)]


def _s5(x, b, s, w_sum, w_intra, w_state, dec):
    L = C_CHUNK
    nc = s // L
    nq = D_MODEL // LANES
    nsl = w_sum.shape[2] // LANES
    blk = L * LANES
    rc = _tile(nc, max(8, 512 // b))
    x3 = x.reshape(b, s, D_MODEL)
    x_spec = pl.BlockSpec((b, rc * L, LANES), lambda q, c, *_: (0, c, q))
    sums = pl.pallas_call(
        functools.partial(_s5_sum_kernel, nb=b, rc=rc),
        grid=(nq, nc // rc),
        in_specs=[x_spec, pl.BlockSpec((None, blk, nsl * LANES), lambda q, c: (q, 0, 0))],
        out_specs=pl.BlockSpec((nsl, rc * b, LANES), lambda q, c: (q, c, 0)),
        out_shape=jax.ShapeDtypeStruct((nq * nsl, nc * b, LANES), F32),
        compiler_params=_params("parallel", "arbitrary"),
        name="s5_sum",
    )(x3, w_sum)
    upb = 2
    ent = pl.pallas_call(
        functools.partial(_s5_scan_kernel, nc=nc, nb=b),
        grid=(nq * nsl // (4 * upb),),
        in_specs=[pl.BlockSpec((4 * upb, nc * b, LANES), lambda i: (i, 0, 0)),
                  pl.BlockSpec((upb, 4, 8, LANES), lambda i: (i, 0, 0, 0))],
        out_specs=pl.BlockSpec((4 * upb, nc * b, LANES), lambda i: (i, 0, 0)),
        out_shape=jax.ShapeDtypeStruct((nq * nsl, nc * b, LANES), F32),
        compiler_params=_params("parallel"),
        name="s5_scan",
    )(sums, dec)
    y = pl.pallas_call(
        functools.partial(_s5_out_kernel, nb=b, rc=rc),
        grid=(nq, nc // rc, 2),
        in_specs=[x_spec,
                  pl.BlockSpec((nsl, rc * b, LANES), lambda q, c, h: (q, c, 0)),
                  pl.BlockSpec((None, blk, blk // 2), lambda q, c, h: (q, 0, h)),
                  pl.BlockSpec((None, nsl * LANES, blk // 2), lambda q, c, h: (q, 0, h))],
        out_specs=pl.BlockSpec((b, rc * L, LANES), lambda q, c, h: (0, c, q)),
        out_shape=jax.ShapeDtypeStruct((b, s, D_MODEL), F32),
        compiler_params=_params("parallel", "arbitrary", "arbitrary"),
        name="s5_out",
    )(x3, ent, w_intra, w_state)
    return y.reshape(b * s, D_MODEL)


_D_T = 512


def _prep_d(w_qkv, norm_gain, w_o, lambda_init):
    qk_w = D_HEADS * 2 * D_QK_DIM
    wq = _pad_heads(w_qkv[:, :qk_w] * (D_QK_DIM ** -0.5 * LOG2E), D_HEADS, 2 * D_QK_DIM)
    wk = _pad_heads(w_qkv[:, qk_w:2 * qk_w], D_HEADS, 2 * D_QK_DIM)
    wv = _pad_heads(w_qkv[:, 2 * qk_w:], D_HEADS, D_V_DIM)
    w = jnp.concatenate([wq, wk, wv], 1).astype(BF)
    gain = jnp.concatenate([norm_gain * (1.0 - lambda_init), jnp.zeros((LANES - D_V_DIM,), F32)])[None, :]
    wo = w_o.reshape(D_HEADS, D_V_DIM, D_MODEL)
    wo = jnp.concatenate([wo, jnp.zeros_like(wo)], 1).reshape(D_HEADS * LANES, D_MODEL).astype(BF)
    return w, gain, wo


def _diff_bias(rel_bias, t):
    rel0 = (jnp.arange(2 * t) + t) % (2 * t) - t
    rel = jnp.arange(-2, 3)[:, None] * t + rel0[None, :]
    vec = jnp.transpose(rel_bias[_bucket(rel)] * LOG2E, (2, 0, 1)).astype(F32)
    flat = jnp.tile(vec, (1, 1, t))[..., :t * (2 * t - 1)]
    return flat.reshape(vec.shape[0], 5, t, 2 * t - 1)[..., :t]


def _flash_d_kernel(q_ref, k_ref, v_ref, bias_ref, lam_ref, gain_ref, o_ref, m_ref, l_ref, acc_ref, *, nk,
                    lambda_init):
    t = q_ref.shape[0]
    i = pl.program_id(2)
    qv = q_ref[...].astype(F32)
    lane = lax.broadcasted_iota(I32, (t, LANES), 1)
    q0 = jnp.where(lane < D_QK_DIM, qv, 0.0).astype(BF)
    q1 = jnp.where((lane >= D_QK_DIM) & (lane < 2 * D_QK_DIM), qv, 0.0).astype(BF)
    q = jnp.concatenate([q0, q1], axis=0)
    _softmax_init(m_ref, l_ref, acc_ref)

    def body(j, carry):
        off = pl.multiple_of(j * t, t)
        bias = bias_ref[jnp.clip(j - i, -2, 2) + 2]
        s = _dot_nt(q, k_ref[pl.ds(off, t), :])
        s = (s.reshape(2, t, t) + bias[None]).reshape(2 * t, t)
        _softmax_step(s, v_ref[pl.ds(off, t), :], m_ref, l_ref, acc_ref)
        return carry

    lax.fori_loop(0, nk, body, 0, unroll=2)
    lf = lam_ref[...]
    lam = (jnp.exp(jnp.sum(lf[0:1] * lf[1:2], keepdims=True))
           - jnp.exp(jnp.sum(lf[2:3] * lf[3:4], keepdims=True)) + lambda_init)
    on = acc_ref[...] / jnp.sum(l_ref[...], -1, keepdims=True)
    o = on[:t] - lam * on[t:]
    ms = jnp.sum(o * o, -1, keepdims=True) * (1.0 / D_V_DIM)
    o_ref[...] = (o * lax.rsqrt(ms + RMS_EPS) * gain_ref[...]).astype(BF)


def _flash_d(qkv, bias, lam, gain, lambda_init):
    b, s, _ = qkv.shape
    t = bias.shape[-1]
    return pl.pallas_call(
        functools.partial(_flash_d_kernel, nk=s // t, lambda_init=lambda_init),
        grid=(b, D_HEADS, s // t),
        in_specs=[pl.BlockSpec((None, t, LANES), lambda bi, h, i: (bi, i, h)),
                  pl.BlockSpec((None, s, LANES), lambda bi, h, i: (bi, 0, D_HEADS + h)),
                  pl.BlockSpec((None, s, LANES), lambda bi, h, i: (bi, 0, 2 * D_HEADS + h)),
                  pl.BlockSpec((None, 5, t, t), lambda bi, h, i: (h, 0, 0, 0)),
                  pl.BlockSpec((4, D_QK_DIM), lambda bi, h, i: (0, 0)),
                  pl.BlockSpec((1, LANES), lambda bi, h, i: (0, 0))],
        out_specs=pl.BlockSpec((None, t, LANES), lambda bi, h, i: (bi, i, h)),
        out_shape=jax.ShapeDtypeStruct((b, s, D_HEADS * LANES), BF),
        scratch_shapes=[pltpu.VMEM((2 * t, LANES), F32)] * 3,
        compiler_params=_params("parallel", "parallel", "arbitrary"),
        name="flash_d",
    )(qkv, qkv, qkv, bias, lam, gain)


def _cross_kernel(x_ref, kv_ref, wq_ref, wo_ref, g_ref, b_ref, wr_ref, wr2_ref, y_ref, ybf_ref, aff_ref):
    x = x_ref[...]
    tm = x.shape[0]
    q = (_dot(x.astype(BF), wq_ref[...]) * (X_HEAD_DIM ** -0.5)).astype(BF)
    outs = []
    for h in range(X_HEADS):
        sl = slice(X_HEAD_DIM * h, X_HEAD_DIM * (h + 1))
        s = _dot_nt(q[:, sl], kv_ref[:, sl])
        p = jnp.exp(s - jnp.max(s, -1, keepdims=True))
        l = jnp.sum(p, -1, keepdims=True)
        vh = kv_ref[:, D_MODEL + X_HEAD_DIM * h:D_MODEL + X_HEAD_DIM * (h + 1)]
        outs.append((_dot(p.astype(BF), vh) / l).astype(BF))
    o = jnp.concatenate(outs, axis=1)
    y = _ln_rows(ALPHA * x + _dot(o, wo_ref[...]), g_ref[...], b_ref[...])
    y_ref[...] = y
    yh = y.astype(BF)
    ybf_ref[:, :D_MODEL] = yh
    yl = (y - yh.astype(F32)).astype(BF)
    wr = wr_ref[...]
    wh = wr.astype(BF)
    wl = (wr - wh.astype(F32)).astype(BF)
    lg = _dot_nt(wh, yh) + _dot_nt(wh, yl) + _dot_nt(wl, yh)
    e = jnp.exp(lg - jnp.max(lg, 0, keepdims=True))
    aff = e / jnp.sum(e, 0, keepdims=True)
    for c in range(tm // LANES):
        aff_ref[c] = aff[:, LANES * c:LANES * (c + 1)]
    w2 = wr2_ref[...]
    w2h = w2.astype(BF)
    w2l = (w2 - w2h.astype(F32)).astype(BF)
    lt = _dot(yh, w2h) + _dot(yl, w2h) + _dot(yh, w2l)
    lane = lax.broadcasted_iota(I32, (tm, LANES), 1)
    lt = jnp.where(lane < 2 * N_EXPERTS, lt, NEG)
    et = jnp.exp(lt - jnp.max(lt, -1, keepdims=True))
    at = et / (0.5 * jnp.sum(et, -1, keepdims=True))
    hi = at.astype(BF).astype(F32)
    ybf_ref[:, D_MODEL:] = jnp.where(lane < N_EXPERTS, hi, jnp.where(lane < 2 * N_EXPERTS, at - hi, 0.0)).astype(BF)


_XG_COLS = D_MODEL + LANES


def _cross(x, kv, wq, wo, g, b, wr_t, wr_pad, s, mem_len, tm=512):
    n = x.shape[0]
    tm = _tile(s, tm)
    per = s // tm
    fixed = lambda i: (0, 0)
    return pl.pallas_call(
        _cross_kernel,
        grid=(n // tm,),
        in_specs=[pl.BlockSpec((tm, D_MODEL), lambda i: (i, 0)),
                  pl.BlockSpec((mem_len, 2 * D_MODEL), lambda i: (i // per, 0)),
                  pl.BlockSpec((D_MODEL, D_MODEL), fixed), pl.BlockSpec((D_MODEL, D_MODEL), fixed),
                  pl.BlockSpec((1, D_MODEL), fixed), pl.BlockSpec((1, D_MODEL), fixed),
                  pl.BlockSpec((N_EXPERTS, D_MODEL), fixed), pl.BlockSpec((D_MODEL, LANES), fixed)],
        out_specs=[pl.BlockSpec((tm, D_MODEL), lambda i: (i, 0)),
                   pl.BlockSpec((tm, _XG_COLS), lambda i: (i, 0)),
                   pl.BlockSpec((tm // LANES, N_EXPERTS, LANES), lambda i: (i, 0, 0))],
        out_shape=[jax.ShapeDtypeStruct((n, D_MODEL), F32),
                   jax.ShapeDtypeStruct((n, _XG_COLS), BF),
                   jax.ShapeDtypeStruct((n // LANES, N_EXPERTS, LANES), F32)],
        compiler_params=_params("parallel"),
        name="cross",
    )(x, kv, wq, wo, g, b, wr_t, wr_pad)


def _select_kernel(a_ref, pos_ref, st_ref, *, k, nbits):
    nt = a_ref.shape[0]
    shape = (nt, N_EXPERTS, LANES)
    kf = float(k)

    def keys():
        return lax.bitcast_convert_type(a_ref[...], I32)

    def count(mask):
        c = jnp.sum(jnp.where(mask, 1.0, 0.0), axis=0, keepdims=True)
        return jnp.sum(c, axis=2, keepdims=True)

    def value_step(it, thr):
        cand = thr | jnp.left_shift(jnp.int32(1), 30 - it)
        return jnp.where(count(keys() >= cand) >= kf, cand, thr)

    thr = lax.fori_loop(0, 31, value_step, jnp.zeros((1, N_EXPERTS, 1), I32))
    need = kf - count(keys() > thr)
    idx = lax.broadcasted_iota(I32, shape, 0) * LANES + lax.broadcasted_iota(I32, shape, 2)

    def index_step(it, ithr):
        cand = ithr | jnp.left_shift(jnp.int32(1), nbits - 1 - it)
        return jnp.where(count((keys() == thr) & (idx < cand)) < need, cand, ithr)

    ithr = lax.fori_loop(0, nbits, index_step, jnp.zeros((1, N_EXPERTS, 1), I32))
    thr2, ithr2 = thr[0], ithr[0]
    upper = jnp.where(lax.broadcasted_iota(I32, (LANES, LANES), 0) <= lax.broadcasted_iota(I32, (LANES, LANES), 1),
                      1.0, 0.0).astype(BF)
    lane = lax.broadcasted_iota(I32, (N_EXPERTS, LANES), 1)

    def tile_step(j, carry):
        kj = lax.bitcast_convert_type(a_ref[j], I32)
        sel = (kj > thr2) | ((kj == thr2) & (j * LANES + lane <= ithr2))
        m = jnp.where(sel, 1.0, 0.0)
        inc = _dot(m.astype(BF), upper)
        pos_ref[j] = jnp.where(sel, inc - m + carry, -1.0).astype(I32)
        st_ref[j] = jnp.broadcast_to(carry, (N_EXPERTS, LANES)).astype(I32)
        return carry + inc[:, LANES - 1:LANES]

    lax.fori_loop(0, nt, tile_step, jnp.zeros((N_EXPERTS, 1), F32))


def _select(aff3, k):
    nt = aff3.shape[0]
    nbits = max(1, int(math.ceil(math.log2(nt * LANES))))
    shp = jax.ShapeDtypeStruct(aff3.shape, I32)
    return pl.pallas_call(
        functools.partial(_select_kernel, k=k, nbits=nbits),
        out_shape=[shp, shp],
        compiler_params=pltpu.CompilerParams(vmem_limit_bytes=VMEM_LIMIT),
        name="select",
    )(aff3)


_GATHER_ROWS = LANES + BF16_ROWS


def _moe_ffn_kernel(st_ref, x_ref, pos_ref, wg_ref, wu_ref, wd_ref, ye_ref, buf_ref, *, nb, ncf, sub, ck):
    e = pl.program_id(0)
    t = pl.program_id(1)

    @pl.when(t == 0)
    def _():
        buf_ref[...] = jnp.zeros_like(buf_ref)

    @pl.when(t < nb)
    def _():
        row = lax.broadcasted_iota(I32, (_GATHER_ROWS, LANES), 0)
        for s in range(sub):
            st = st_ref[e, t * sub + s]
            base = pl.multiple_of((st // BF16_ROWS) * BF16_ROWS, BF16_ROWS)
            rel = pos_ref[s, pl.ds(e, 1), :] - base
            onehot = jnp.where(row == rel, 1.0, 0.0).astype(BF)
            c = _dot(onehot, x_ref[LANES * s:LANES * (s + 1), :])
            buf_ref[pl.ds(base, _GATHER_ROWS), :] = buf_ref[pl.ds(base, _GATHER_ROWS), :] + c.astype(BF)

    @pl.when(t >= nb)
    def _():
        c = t - nb

        @pl.when(c < ncf)
        def _():
            rows = pl.ds(pl.multiple_of(c * ck, ck), ck)
            xe = buf_ref[rows, 0:D_MODEL]
            h = (jax.nn.silu(_dot(xe, wg_ref[...])) * _dot(xe, wu_ref[...])).astype(BF)
            lane = lax.broadcasted_iota(I32, (ck, LANES), 1)
            mine = (lane == e) | (lane == e + N_EXPERTS)
            gate = jnp.sum(jnp.where(mine, buf_ref[rows, D_MODEL:].astype(F32), 0.0), -1, keepdims=True)
            ye_ref[...] = (_dot(h, wd_ref[...]) * gate).astype(BF)

        @pl.when(c >= ncf)
        def _():
            ye_ref[...] = jnp.zeros_like(ye_ref)


def _moe_ffn(starts, xbf, pos3, wg, wu, wd, cap, tb=1024):
    n = xbf.shape[0]
    tb = _tile(n, tb)
    nb = n // tb
    sub = tb // LANES
    ck = _tile(cap, 512)
    ncf = cap // ck
    nch = ncf + pl.cdiv(_WIN_STEP + _GATHER_ROWS, ck)
    dff = wg.shape[-1]
    blk = lambda e, t, st: (jnp.minimum(t, nb - 1), 0)
    return pl.pallas_call(
        functools.partial(_moe_ffn_kernel, nb=nb, ncf=ncf, sub=sub, ck=ck),
        grid_spec=pltpu.PrefetchScalarGridSpec(
            num_scalar_prefetch=1,
            grid=(N_EXPERTS, nb + nch),
            in_specs=[pl.BlockSpec((tb, _XG_COLS), blk),
                      pl.BlockSpec((sub, N_EXPERTS, LANES), lambda e, t, st: (jnp.minimum(t, nb - 1), 0, 0)),
                      pl.BlockSpec((None, D_MODEL, dff), lambda e, t, st: (e, 0, 0)),
                      pl.BlockSpec((None, D_MODEL, dff), lambda e, t, st: (e, 0, 0)),
                      pl.BlockSpec((None, dff, D_MODEL), lambda e, t, st: (e, 0, 0))],
            out_specs=pl.BlockSpec((None, ck, D_MODEL), lambda e, t, st: (e, jnp.maximum(t - nb, 0), 0)),
            scratch_shapes=[pltpu.VMEM((cap + _GATHER_ROWS, _XG_COLS), BF)]),
        out_shape=jax.ShapeDtypeStruct((N_EXPERTS, nch * ck, D_MODEL), BF),
        compiler_params=_params("arbitrary", "arbitrary"),
        name="moe_ffn",
    )(starts, xbf, pos3, wg, wu, wd)


_WIN_STEP = 256
_WIN_ROWS = _WIN_STEP + _GATHER_ROWS


def _combine_kernel(st_ref, x_ref, pos_ref, g_ref, b_ref, *rest):
    ye_refs, y_ref = rest[:N_EXPERTS], rest[N_EXPERTS]
    j = pl.program_id(0)
    col = lax.broadcasted_iota(I32, (LANES, _GATHER_ROWS), 1)
    acc = None
    for e in range(N_EXPERTS):
        st = st_ref[e, j]
        base = (st // BF16_ROWS) * BF16_ROWS
        sub = pl.multiple_of(base - (st // _WIN_STEP) * _WIN_STEP, BF16_ROWS)
        onehot = jnp.where(col == pos_ref[:, e:e + 1] - base, 1.0, 0.0).astype(BF)
        d = _dot(onehot, ye_refs[e][0, pl.ds(sub, _GATHER_ROWS), :])
        acc = d if acc is None else acc + d
    y_ref[...] = _ln_rows(ALPHA * x_ref[...] + acc, g_ref[...], b_ref[...])


def _combine(starts, x, pos, g, b, ye):
    n = x.shape[0]
    assert ye.shape[1] >= EC_CAPACITY * n // N_EXPERTS + _WIN_ROWS
    row = lambda j, st: (j, 0)
    fixed = lambda j, st: (0, 0)

    def ye_spec(e):
        return pl.BlockSpec((pl.Element(1), pl.Element(_WIN_ROWS), pl.Element(D_MODEL)),
                            lambda j, st: (e, (st[e, j] // _WIN_STEP) * _WIN_STEP, 0))

    return pl.pallas_call(
        _combine_kernel,
        grid_spec=pltpu.PrefetchScalarGridSpec(
            num_scalar_prefetch=1,
            grid=(n // LANES,),
            in_specs=[pl.BlockSpec((LANES, D_MODEL), row),
                      pl.BlockSpec((LANES, N_EXPERTS), row),
                      pl.BlockSpec((1, D_MODEL), fixed), pl.BlockSpec((1, D_MODEL), fixed)]
            + [ye_spec(e) for e in range(N_EXPERTS)],
            out_specs=pl.BlockSpec((LANES, D_MODEL), row)),
        out_shape=jax.ShapeDtypeStruct((n, D_MODEL), F32),
        compiler_params=_params("arbitrary"),
        name="combine",
    )(starts, x, pos, g, b, *([ye] * N_EXPERTS))


def _moe(x, xg, aff3, wg, wu, wd, g, b):
    n = x.shape[0]
    cap = EC_CAPACITY * n // N_EXPERTS
    pos3, st3 = _select(aff3, cap)
    starts = jnp.transpose(st3[:, :, 0])
    ye = _moe_ffn(starts, xg, pos3, wg, wu, wd, cap)
    pos = jnp.transpose(pos3, (0, 2, 1)).reshape(n, N_EXPERTS)
    return _combine(starts, x, pos, g, b, ye)


def _prep_weights(p):
    w = {}
    w['a'] = [_prep_a(p['a_w_qkv'][j], p['a_q_gain'][j], p['a_k_gain'][j], p['a_w_o'][j])
              for j in range(p['a_w_qkv'].shape[0])]
    w['b'] = [_prep_b(p['b_w_qkv'][j], p['b_w_o'][j]) for j in range(p['b_w_qkv'].shape[0])]
    w['c'] = [_prep_c(p['c_lam_re'][j], p['c_lam_im'][j], p['c_log_dt'][j], p['c_b_re'][j], p['c_b_im'][j],
                      p['c_c_re'][j], p['c_c_im'][j]) + (p['c_d'][j][None, :], p['c_w_glu'][j].astype(BF))
              for j in range(p['c_lam_re'].shape[0])]
    w['d'] = []
    for j in range(p['d_w_qkv'].shape[0]):
        layer = N_MIXERS * j + 3
        lambda_init = 0.8 - 0.6 * math.exp(-0.3 * layer)
        w['d'].append(_prep_d(p['d_w_qkv'][j], p['d_norm_gain'][j], p['d_w_o'][j], lambda_init)
                      + (p['d_lam'][j].astype(F32), lambda_init))
    w['diff_bias'] = _diff_bias(p['rel_bias'], _D_T)
    w['x_w_q'] = p['x_w_q'].astype(BF)
    w['x_w_kv'] = p['x_w_kv'].astype(BF)
    w['x_w_o'] = p['x_w_o'].astype(BF)
    w['router_t'] = jnp.transpose(p['moe_w_router'], (0, 2, 1)).astype(F32)
    wr = p['moe_w_router'].astype(F32)
    w['router_pad'] = jnp.concatenate(
        [wr, wr, jnp.zeros(wr.shape[:2] + (LANES - 2 * N_EXPERTS,), F32)], -1)
    w['moe_w_gate'] = p['moe_w_gate'].astype(BF)
    w['moe_w_up'] = p['moe_w_up'].astype(BF)
    w['moe_w_down'] = p['moe_w_down'].astype(BF)
    return w


def _trunk(x, mem, p, w):
    b, s, _ = x.shape
    n = b * s
    mem_len = mem.shape[1]
    x = x.reshape(n, D_MODEL)
    mem2 = mem.reshape(b * mem_len, D_MODEL)
    ln_g, ln_b = p['ln_g'], p['ln_b']
    for i in range(DEPTH):
        m, j = i % N_MIXERS, i // N_MIXERS
        g0, b0 = ln_g[i, 0][None, :], ln_b[i, 0][None, :]
        if m == 0:
            wa, gains, wo = w['a'][j]
            qkv = _proj_a(x, wa, _rope_table(s), gains, s)
            o = _flash_a(qkv.reshape(b, s, _A_COLS))
            x = _post(x, o.reshape(n, A_HEADS * LANES), wo, g0, b0)
        elif m == 1:
            wb, wo = w['b'][j]
            qkv = _proj(x, wb).reshape(b, s, _B_TILES * LANES)
            os_, ls_ = [], []
            for g in range(len(B_PATTERNS)):
                tm = min(LANES, s // B_PATTERNS[g][1])
                o, lse = _band_attention(qkv, _band_bias(p['rel_bias'], g, tm), g, tm)
                os_.append(o)
                ls_.append(lse)
            x = _post_b(x, os_, ls_, wo, g0, b0)
        elif m == 2:
            w_sum, w_intra, w_state, dec, dskip, wglu = w['c'][j]
            ys = _s5(x, b, s, w_sum, w_intra, w_state, dec)
            x = _post_c(x, ys, dskip, wglu, g0, b0)
        else:
            wd, gain, wo, lam, lambda_init = w['d'][j]
            qkv = _proj(x, wd).reshape(b, s, 3 * D_HEADS * LANES)
            o = _flash_d(qkv, w['diff_bias'], lam, gain, lambda_init)
            x = _post(x, o.reshape(n, D_HEADS * LANES), wo, g0, b0)
        kv = _proj(mem2, w['x_w_kv'][i], tm=mem_len)
        x, xbf, aff3 = _cross(x, kv, w['x_w_q'][i], w['x_w_o'][i], ln_g[i, 1][None, :], ln_b[i, 1][None, :],
                              w['router_t'][i], w['router_pad'][i], s, mem_len)
        x = _moe(x, xbf, aff3, w['moe_w_gate'][i], w['moe_w_up'][i], w['moe_w_down'][i],
                 ln_g[i, 2][None, :], ln_b[i, 2][None, :])
    return x.reshape(b, s, D_MODEL)


def kernel(x_prompt, x_sample, mem_prompt, mem_sample, rel_bias, ln_g, ln_b, a_w_qkv, a_q_gain, a_k_gain, a_w_o, b_w_qkv, b_w_o, c_lam_re, c_lam_im, c_log_dt, c_b_re, c_b_im, c_c_re, c_c_im, c_d, c_w_glu, d_w_qkv, d_lam, d_norm_gain, d_w_o, x_w_q, x_w_kv, x_w_o, moe_w_router, moe_w_gate, moe_w_up, moe_w_down):
    p = dict(rel_bias=rel_bias, ln_g=ln_g, ln_b=ln_b,
             a_w_qkv=a_w_qkv, a_q_gain=a_q_gain, a_k_gain=a_k_gain, a_w_o=a_w_o,
             b_w_qkv=b_w_qkv, b_w_o=b_w_o,
             c_lam_re=c_lam_re, c_lam_im=c_lam_im, c_log_dt=c_log_dt, c_b_re=c_b_re, c_b_im=c_b_im,
             c_c_re=c_c_re, c_c_im=c_c_im, c_d=c_d, c_w_glu=c_w_glu,
             d_w_qkv=d_w_qkv, d_lam=d_lam, d_norm_gain=d_norm_gain, d_w_o=d_w_o,
             x_w_q=x_w_q, x_w_kv=x_w_kv, x_w_o=x_w_o,
             moe_w_router=moe_w_router, moe_w_gate=moe_w_gate, moe_w_up=moe_w_up, moe_w_down=moe_w_down)
    w = _prep_weights(p)
    return (_trunk(x_prompt, mem_prompt, p, w), _trunk(x_sample, mem_sample, p, w))
```

```python
import functools
import math

import numpy as np
import jax
import jax.numpy as jnp
from jax import lax
from jax.experimental import pallas as pl
from jax.experimental.pallas import tpu as pltpu

F32 = jnp.float32
BF = jnp.bfloat16
I32 = jnp.int32

D_MODEL = 1024
DEPTH = 4
GRID_W = 64
N_MIXERS = 4
LN_EPS = 1e-5
RMS_EPS = 1e-6
ALPHA = (2.0 * DEPTH) ** 0.25

A_HEADS = 16
A_KV_HEADS = 4
A_HEAD_DIM = 64
ROPE_BASE = 10000.0

B_PATTERNS = ((128, 1), (512, 4), (2048, 16))
B_HEADS_PER_GROUP = 4
B_HEAD_DIM = 64

C_GROUP = 16
C_N_GROUPS = D_MODEL // C_GROUP
C_STATE = 64
C_CHUNK = 16
_C_GL = 128 // C_GROUP

D_HEADS = 12
D_QK_DIM = 32
D_V_DIM = 64

REL_BUCKETS = 32
REL_MAX_DIST = 128

X_HEADS = 4
X_HEAD_DIM = D_MODEL // X_HEADS

N_EXPERTS = 16
EC_CAPACITY = 2

LANES = 128
BF16_ROWS = 16
VMEM_LIMIT = 56 * 1024 * 1024
NEG = -1e30
LOG2E = math.log2(math.e)


def _params(*sem):
    return pltpu.CompilerParams(dimension_semantics=sem, vmem_limit_bytes=VMEM_LIMIT)


def _tile(n, pref):
    t = min(n, pref)
    assert n % t == 0, (n, pref)
    return t


def _ln_rows(v, g, b):
    mu = jnp.mean(v, -1, keepdims=True)
    c = v - mu
    var = jnp.mean(c * c, -1, keepdims=True)
    return c * lax.rsqrt(var + LN_EPS) * g + b


def _dot_nt(a, b):
    return lax.dot_general(a, b, (((1,), (1,)), ((), ())), preferred_element_type=F32)


def _dot(a, b):
    return jnp.dot(a, b, preferred_element_type=F32)


def _proj_kernel(x_ref, w_ref, o_ref):
    o_ref[...] = _dot(x_ref[...].astype(BF), w_ref[...]).astype(o_ref.dtype)


def _proj(x, w, tm=512):
    n, k = x.shape
    m = w.shape[1]
    tm = _tile(n, tm)
    return pl.pallas_call(
        _proj_kernel,
        grid=(n // tm,),
        in_specs=[pl.BlockSpec((tm, k), lambda i: (i, 0)),
                  pl.BlockSpec((k, m), lambda i: (0, 0))],
        out_specs=pl.BlockSpec((tm, m), lambda i: (i, 0)),
        out_shape=jax.ShapeDtypeStruct((n, m), BF),
        compiler_params=_params("parallel"),
        name="proj",
    )(x, w)


def _post_kernel(x_ref, o_ref, w_ref, g_ref, b_ref, y_ref):
    h = _dot(o_ref[...], w_ref[...])
    y_ref[...] = _ln_rows(ALPHA * x_ref[...] + h, g_ref[...], b_ref[...])


def _post(x, o, w, g, b, tm=512):
    n = x.shape[0]
    ko = o.shape[1]
    tm = _tile(n, tm)
    return pl.pallas_call(
        _post_kernel,
        grid=(n // tm,),
        in_specs=[pl.BlockSpec((tm, D_MODEL), lambda i: (i, 0)),
                  pl.BlockSpec((tm, ko), lambda i: (i, 0)),
                  pl.BlockSpec((ko, D_MODEL), lambda i: (0, 0)),
                  pl.BlockSpec((1, D_MODEL), lambda i: (0, 0)),
                  pl.BlockSpec((1, D_MODEL), lambda i: (0, 0))],
        out_specs=pl.BlockSpec((tm, D_MODEL), lambda i: (i, 0)),
        out_shape=jax.ShapeDtypeStruct((n, D_MODEL), F32),
        compiler_params=_params("parallel"),
        name="post",
    )(x, o, w, g, b)


def _post_b_kernel(x_ref, o0_ref, o1_ref, o2_ref, l0_ref, l1_ref, l2_ref, w_ref, g_ref, b_ref, y_ref):
    l0, l1, l2 = l0_ref[...], l1_ref[...], l2_ref[...]
    m = jnp.maximum(jnp.maximum(l0, l1), l2)
    e0, e1, e2 = jnp.exp(l0 - m), jnp.exp(l1 - m), jnp.exp(l2 - m)
    inv = 1.0 / (e0 + e1 + e2)
    gw = B_HEADS_PER_GROUP * LANES
    h = _dot((o0_ref[...].astype(F32) * (e0 * inv)).astype(BF), w_ref[0:gw, :])
    h = h + _dot((o1_ref[...].astype(F32) * (e1 * inv)).astype(BF), w_ref[gw:2 * gw, :])
    h = h + _dot((o2_ref[...].astype(F32) * (e2 * inv)).astype(BF), w_ref[2 * gw:3 * gw, :])
    y_ref[...] = _ln_rows(ALPHA * x_ref[...] + h, g_ref[...], b_ref[...])


def _post_b(x, os_, ls_, w, g, b, tm=512):
    n = x.shape[0]
    tm = _tile(n, tm)
    gw = B_HEADS_PER_GROUP * LANES
    row = lambda i: (i, 0)
    fixed = lambda i: (0, 0)
    return pl.pallas_call(
        _post_b_kernel,
        grid=(n // tm,),
        in_specs=[pl.BlockSpec((tm, D_MODEL), row)] + [pl.BlockSpec((tm, gw), row)] * 6
        + [pl.BlockSpec((3 * gw, D_MODEL), fixed), pl.BlockSpec((1, D_MODEL), fixed),
           pl.BlockSpec((1, D_MODEL), fixed)],
        out_specs=pl.BlockSpec((tm, D_MODEL), row),
        out_shape=jax.ShapeDtypeStruct((n, D_MODEL), F32),
        compiler_params=_params("parallel"),
        name="post_b",
    )(x, *os_, *ls_, w, g, b)


def _post_c_kernel(x_ref, ys_ref, d_ref, w_ref, g_ref, b_ref, y_ref):
    x = x_ref[...]
    z = jax.nn.gelu(ys_ref[...] + d_ref[...] * x).astype(BF)
    h = _dot(z, w_ref[...])
    hh = h[:, :D_MODEL] * jax.nn.sigmoid(h[:, D_MODEL:])
    y_ref[...] = _ln_rows(ALPHA * x + hh, g_ref[...], b_ref[...])


def _post_c(x, ys, d, w, g, b, tm=512):
    n = x.shape[0]
    tm = _tile(n, tm)
    row = lambda i: (i, 0)
    fixed = lambda i: (0, 0)
    return pl.pallas_call(
        _post_c_kernel,
        grid=(n // tm,),
        in_specs=[pl.BlockSpec((tm, D_MODEL), row), pl.BlockSpec((tm, D_MODEL), row),
                  pl.BlockSpec((1, D_MODEL), fixed), pl.BlockSpec((D_MODEL, 2 * D_MODEL), fixed),
                  pl.BlockSpec((1, D_MODEL), fixed), pl.BlockSpec((1, D_MODEL), fixed)],
        out_specs=pl.BlockSpec((tm, D_MODEL), row),
        out_shape=jax.ShapeDtypeStruct((n, D_MODEL), F32),
        compiler_params=_params("parallel"),
        name="post_c",
    )(x, ys, d, w, g, b)


_A_QK_TILES = A_HEADS + A_KV_HEADS
_A_COLS = (_A_QK_TILES + A_KV_HEADS) * LANES


def _rope_partner():
    d = np.arange(A_HEAD_DIM)
    e = d % (A_HEAD_DIM // 2)
    lo = e < A_HEAD_DIM // 4
    return np.where(lo, d + A_HEAD_DIM // 4, d - A_HEAD_DIM // 4), np.where(lo, -1.0, 1.0).astype(np.float32)


def _prep_a(w_qkv, q_gain, k_gain, w_o):
    partner, sign = _rope_partner()
    nqk = _A_QK_TILES * A_HEAD_DIM
    wqk = w_qkv[:, :nqk].reshape(D_MODEL, _A_QK_TILES, A_HEAD_DIM)
    wsw = wqk[:, :, partner] * sign
    wqk = jnp.concatenate([wqk, wsw], -1).reshape(D_MODEL, _A_QK_TILES * LANES)
    wv = w_qkv[:, nqk:].reshape(D_MODEL, A_KV_HEADS, A_HEAD_DIM)
    wv = jnp.concatenate([wv, jnp.zeros_like(wv)], -1).reshape(D_MODEL, A_KV_HEADS * LANES)
    w = jnp.concatenate([wqk, wv], 1).astype(BF)
    gq = jnp.concatenate([q_gain, q_gain[partner]]) * (A_HEAD_DIM ** -0.5 * 0.5 * LOG2E)
    gk = jnp.concatenate([k_gain, k_gain[partner]])
    gains = jnp.stack([gq, gk], 0)
    wo = w_o.reshape(A_HEADS, A_HEAD_DIM, D_MODEL)
    wo = jnp.concatenate([wo, jnp.zeros_like(wo)], 1).reshape(A_HEADS * LANES, D_MODEL).astype(BF)
    return w, gains, wo


def _rope_table(s):
    pos = jnp.arange(s)
    rows, cols = (pos // GRID_W).astype(F32), (pos % GRID_W).astype(F32)
    half = A_HEAD_DIM // 2
    freqs = ROPE_BASE ** (-jnp.arange(0, half, 2, dtype=F32) / half)
    ang_r = rows[:, None] * freqs
    ang_c = cols[:, None] * freqs
    ang = jnp.concatenate([ang_r, ang_r, ang_c, ang_c], -1)
    return jnp.concatenate([jnp.cos(ang), jnp.sin(ang)], -1)


def _proj_a_kernel(x_ref, w_ref, cs_ref, g_ref, o_ref):
    y = _dot(x_ref[...].astype(BF), w_ref[...])
    cs = cs_ref[...]
    gq = cs * g_ref[0:1, :]
    gk = cs * g_ref[1:2, :]
    for h in range(_A_QK_TILES):
        t = y[:, LANES * h:LANES * (h + 1)]
        r = lax.rsqrt(jnp.sum(t * t, -1, keepdims=True) * (1.0 / LANES) + RMS_EPS)
        e = t * r * (gq if h < A_HEADS else gk)
        o_ref[:, LANES * h:LANES * (h + 1)] = (e + pltpu.roll(e, LANES // 2, 1)).astype(BF)
    o_ref[:, _A_QK_TILES * LANES:] = y[:, _A_QK_TILES * LANES:].astype(BF)


def _proj_a(x, w, cs, gains, s, tm=512):
    n = x.shape[0]
    tm = _tile(s, tm)
    per = s // tm
    return pl.pallas_call(
        _proj_a_kernel,
        grid=(n // tm,),
        in_specs=[pl.BlockSpec((tm, D_MODEL), lambda i: (i, 0)),
                  pl.BlockSpec((D_MODEL, _A_COLS), lambda i: (0, 0)),
                  pl.BlockSpec((tm, LANES), lambda i: (i % per, 0)),
                  pl.BlockSpec((2, LANES), lambda i: (0, 0))],
        out_specs=pl.BlockSpec((tm, _A_COLS), lambda i: (i, 0)),
        out_shape=jax.ShapeDtypeStruct((n, _A_COLS), BF),
        compiler_params=_params("parallel"),
        name="proj_a",
    )(x, w, cs, gains)


def _softmax_step(s, v, m_ref, l_ref, acc_ref):
    tk = s.shape[1]
    m_old = m_ref[...]
    m_new = jnp.maximum(m_old, jnp.max(s, -1, keepdims=True))
    a = jnp.exp2(m_old - m_new)
    psum = None
    chunks = []
    for c in range(tk // LANES):
        pc = jnp.exp2(s[:, LANES * c:LANES * (c + 1)] - m_new)
        psum = pc if psum is None else psum + pc
        chunks.append(pc.astype(BF))
    p = jnp.concatenate(chunks, axis=1)
    l_ref[...] = a * l_ref[...] + psum
    acc_ref[...] = a * acc_ref[...] + _dot(p, v)
    m_ref[...] = m_new


def _softmax_init(m_ref, l_ref, acc_ref):
    m_ref[...] = jnp.full(m_ref.shape, -jnp.inf, F32)
    l_ref[...] = jnp.zeros_like(l_ref)
    acc_ref[...] = jnp.zeros_like(acc_ref)


def _flash_a_kernel(q_ref, k_ref, v_ref, o_ref, m_ref, l_ref, acc_ref, *, tk, nk):
    tq = q_ref.shape[0]
    rep = A_HEADS // A_KV_HEADS
    q = jnp.concatenate([q_ref[:, LANES * r:LANES * (r + 1)] for r in range(rep)], axis=0)
    _softmax_init(m_ref, l_ref, acc_ref)

    def body(j, carry):
        off = pl.multiple_of(j * tk, tk)
        _softmax_step(_dot_nt(q, k_ref[pl.ds(off, tk), :]), v_ref[pl.ds(off, tk), :], m_ref, l_ref, acc_ref)
        return carry

    lax.fori_loop(0, nk, body, 0, unroll=2)
    o = acc_ref[...] / jnp.sum(l_ref[...], -1, keepdims=True)
    for r in range(rep):
        o_ref[:, LANES * r:LANES * (r + 1)] = o[r * tq:(r + 1) * tq].astype(BF)


def _flash_a(qkv, tq=256, tk=1024):
    b, s, _ = qkv.shape
    tq, tk = _tile(s, tq), _tile(s, tk)
    rep = A_HEADS // A_KV_HEADS
    gw = rep * LANES
    return pl.pallas_call(
        functools.partial(_flash_a_kernel, tk=tk, nk=s // tk),
        grid=(b, A_KV_HEADS, s // tq),
        in_specs=[pl.BlockSpec((None, tq, gw), lambda bi, g, i: (bi, i, g)),
                  pl.BlockSpec((None, s, LANES), lambda bi, g, i: (bi, 0, A_HEADS + g)),
                  pl.BlockSpec((None, s, LANES), lambda bi, g, i: (bi, 0, _A_QK_TILES + g))],
        out_specs=pl.BlockSpec((None, tq, gw), lambda bi, g, i: (bi, i, g)),
        out_shape=jax.ShapeDtypeStruct((b, s, A_HEADS * LANES), BF),
        scratch_shapes=[pltpu.VMEM((rep * tq, LANES), F32)] * 3,
        compiler_params=_params("parallel", "parallel", "arbitrary"),
        name="flash_a",
    )(qkv, qkv, qkv)


def _bucket(rel):
    half = REL_BUCKETS // 2
    max_exact = half // 2
    n = jnp.abs(rel)
    large = max_exact + (jnp.log(jnp.maximum(n, 1).astype(F32) / max_exact)
                         / math.log(REL_MAX_DIST / max_exact) * (half - max_exact)).astype(I32)
    large = jnp.minimum(large, half - 1)
    return jnp.where(rel > 0, half, 0) + jnp.where(n < max_exact, n, large)


_B_TILES = 3 * len(B_PATTERNS) * B_HEADS_PER_GROUP
_B_GW = B_HEADS_PER_GROUP * LANES
_B_T = 256


def _pad_heads(w, heads, dim):
    w = w.reshape(w.shape[0], heads, dim)
    return jnp.concatenate([w, jnp.zeros((w.shape[0], heads, LANES - dim), w.dtype)], -1).reshape(
        w.shape[0], heads * LANES)


def _prep_b(w_qkv, w_o):
    nh = len(B_PATTERNS) * B_HEADS_PER_GROUP
    c = nh * B_HEAD_DIM
    wq = _pad_heads(w_qkv[:, :c] * (B_HEAD_DIM ** -0.5), nh, B_HEAD_DIM)
    wk = _pad_heads(w_qkv[:, c:2 * c], nh, B_HEAD_DIM)
    wv = _pad_heads(w_qkv[:, 2 * c:], nh, B_HEAD_DIM)
    w = jnp.concatenate([wq, wk, wv], 1).astype(BF)
    wo = w_o.reshape(nh, B_HEAD_DIM, D_MODEL)
    wo = jnp.concatenate([wo, jnp.zeros_like(wo)], 1).reshape(nh * LANES, D_MODEL).astype(BF)
    return w, wo


def _toeplitz(vec, t):
    flat = jnp.tile(vec, (1,) * (vec.ndim - 1) + (t,))[..., :t * (2 * t - 1)]
    return flat.reshape(vec.shape[:-1] + (t, 2 * t - 1))[..., :t]


def _band_tiles(g, t):
    window, dil = B_PATTERNS[g]
    return pl.cdiv((window // (2 * dil)) * dil, t)


def _band_bias(rel_bias, g, t):
    window, dil = B_PATTERNS[g]
    reach = (window // (2 * dil)) * dil
    n = _band_tiles(g, t)
    rel0 = (jnp.arange(2 * t) + t) % (2 * t) - t
    rel = jnp.arange(-n, n + 1)[:, None] * t + rel0[None, :]
    bias = rel_bias[_bucket(rel)][:, :, g * B_HEADS_PER_GROUP:(g + 1) * B_HEADS_PER_GROUP]
    bias = jnp.where(((rel % dil == 0) & (jnp.abs(rel) <= reach))[:, :, None], bias, NEG)
    return _toeplitz(jnp.transpose(bias, (2, 0, 1)).astype(F32), t)


def _band_kernel(q_ref, k_ref, v_ref, bias_ref, o_ref, lse_ref, *, n, nk):
    t = q_ref.shape[0]
    i = pl.program_id(2)
    q = q_ref[...]
    logits, offs = [], []
    for o in range(-n, n + 1):
        j = i + o
        off = pl.multiple_of(jnp.clip(j, 0, nk - 1) * t, t)
        s = _dot_nt(q, k_ref[pl.ds(off, t), :]) + bias_ref[o + n]
        logits.append(jnp.where((j >= 0) & (j < nk), s, NEG))
        offs.append(off)
    m = logits[0].max(-1, keepdims=True)
    for s in logits[1:]:
        m = jnp.maximum(m, s.max(-1, keepdims=True))
    l = jnp.zeros((t, 1), F32)
    acc = jnp.zeros((t, LANES), F32)
    for s, off in zip(logits, offs):
        p = jnp.exp(s - m)
        l = l + jnp.sum(p, -1, keepdims=True)
        acc = acc + _dot(p.astype(BF), v_ref[pl.ds(off, t), :])
    o_ref[...] = (acc / l).astype(BF)
    lse_ref[...] = jnp.broadcast_to(m + jnp.log(l), (t, LANES))


def _band_attention(qkv, bias, g, t):
    b, s, _ = qkv.shape
    nh = len(B_PATTERNS) * B_HEADS_PER_GROUP
    n = _band_tiles(g, t)
    spec = lambda base: pl.BlockSpec((None, s, LANES), lambda bi, h, i: (bi, 0, base + g * B_HEADS_PER_GROUP + h))
    out_spec = pl.BlockSpec((None, t, LANES), lambda bi, h, i: (bi, i, h))
    o, lse = pl.pallas_call(
        functools.partial(_band_kernel, n=n, nk=s // t),
        grid=(b, B_HEADS_PER_GROUP, s // t),
        in_specs=[pl.BlockSpec((None, t, LANES), lambda bi, h, i: (bi, i, g * B_HEADS_PER_GROUP + h)),
                  spec(nh), spec(2 * nh),
                  pl.BlockSpec((None, 2 * n + 1, t, t), lambda bi, h, i: (h, 0, 0, 0))],
        out_specs=[out_spec, out_spec],
        out_shape=[jax.ShapeDtypeStruct((b, s, _B_GW), BF), jax.ShapeDtypeStruct((b, s, _B_GW), F32)],
        compiler_params=_params("parallel", "parallel", "arbitrary"),
        name="band_%d" % g,
    )(qkv, qkv, qkv, bias)
    return o.reshape(b * s, _B_GW), lse.reshape(b * s, _B_GW)


def _prep_c(lam_re, lam_im, log_dt, b_re, b_im, c_re, c_im):
    hp = lax.Precision.HIGHEST
    L, P, C = C_CHUNK, C_STATE, C_GROUP
    lr, li = lam_re.astype(F32), lam_im.astype(F32)
    dt = jnp.exp(log_dt.astype(F32))[..., None]
    mag = jnp.exp(lr * dt)
    ar, ai = mag * jnp.cos(li * dt), mag * jnp.sin(li * dt)
    den = lr * lr + li * li
    zr = ((ar - 1.0) * lr + ai * li) / den
    zi = (ai * lr - (ar - 1.0) * li) / den
    br, bi = b_re.astype(F32), b_im.astype(F32)
    bbr = zr[..., None] * br - zi[..., None] * bi
    bbi = zr[..., None] * bi + zi[..., None] * br
    cr, ci = c_re.astype(F32), c_im.astype(F32)
    prs, pis = [jnp.ones_like(ar)], [jnp.zeros_like(ai)]
    for _ in range(L):
        pr_, pi_ = prs[-1], pis[-1]
        prs.append(ar * pr_ - ai * pi_)
        pis.append(ar * pi_ + ai * pr_)
    pr, pi = jnp.stack(prs, 0), jnp.stack(pis, 0)

    def lag(pr_k, pi_k):
        tr = pr_k[..., None] * bbr - pi_k[..., None] * bbi
        ti = pr_k[..., None] * bbi + pi_k[..., None] * bbr
        return (jnp.einsum('dgop,kdgpi->kdgoi', cr, tr, precision=hp)
                - jnp.einsum('dgop,kdgpi->kdgoi', ci, ti, precision=hp))

    kern = lag(pr[:L], pi[:L])
    jj = np.arange(L)[:, None]
    ii = np.arange(L)[None, :]
    kf = kern[:, 0][np.clip(ii - jj, 0, L - 1)] * jnp.asarray((ii >= jj)[:, :, None, None, None], F32)
    kb = kern[:, 1][np.clip(jj - ii, 0, L - 1)] * jnp.asarray((jj >= ii)[:, :, None, None, None], F32)
    m_intra = jnp.transpose(kf + kb, (2, 0, 4, 1, 3)).reshape(C_N_GROUPS, L * C, L * C)

    def summ(d, powers):
        pr_k, pi_k = pr[powers, d], pi[powers, d]
        sr = pr_k[..., None] * bbr[d] - pi_k[..., None] * bbi[d]
        si = pr_k[..., None] * bbi[d] + pi_k[..., None] * bbr[d]
        s = jnp.concatenate([sr, si], 2)
        return jnp.transpose(s, (1, 0, 3, 2)).reshape(C_N_GROUPS, L * C, 2 * P)

    w_sum = jnp.concatenate([summ(0, np.arange(L - 1, -1, -1)), summ(1, np.arange(L))], -1)

    def outw(d, powers):
        pr_k, pi_k = pr[powers, d], pi[powers, d]
        wr = cr[d][None] * pr_k[:, :, None, :] - ci[d][None] * pi_k[:, :, None, :]
        wi = -(cr[d][None] * pi_k[:, :, None, :] + ci[d][None] * pr_k[:, :, None, :])
        w = jnp.concatenate([wr, wi], -1)
        return jnp.transpose(w, (1, 3, 0, 2)).reshape(C_N_GROUPS, 2 * P, L * C)

    w_state = jnp.concatenate([outw(0, np.arange(1, L + 1)), outw(1, np.arange(L, 0, -1))], 1)

    nq = C_N_GROUPS // _C_GL
    npair = _C_GL // 2
    blk = L * _C_GL * C
    lane = jnp.arange(LANES)
    gsel = (lane[None, :] // C == jnp.arange(_C_GL)[:, None])
    src = jnp.tile(m_intra.astype(BF).reshape(nq, _C_GL, L, C, L, C), (1, 1, 1, 1, 1, _C_GL))
    src = jnp.where(gsel[None, :, None, None, None, :], src, 0)
    w_intra = jnp.transpose(src, (0, 2, 1, 3, 4, 5)).reshape(nq, blk, blk)
    psel = (lane[None, :] // P == jnp.arange(2)[:, None])
    ksel = jnp.eye(npair, dtype=bool)
    src = jnp.tile(w_sum.astype(BF).reshape(nq, npair, 2, L, C, 1, 4, P), (1, 1, 1, 1, 1, npair, 1, 2))
    src = jnp.where(psel[None, None, :, None, None, None, None, :]
                    & ksel[None, :, None, None, None, :, None, None], src, 0)
    w_sum = jnp.transpose(src, (0, 3, 1, 2, 4, 5, 6, 7)).reshape(nq, blk, _C_GL * 4 * P)
    osel = (lane[None, None, :] // C
            == (2 * jnp.arange(npair)[:, None, None] + jnp.arange(2)[None, :, None]))
    src = jnp.tile(w_state.astype(BF).reshape(nq, npair, 2, 4, P, L, C), (1, 1, 1, 1, 1, 1, _C_GL))
    src = jnp.where(osel[None, :, :, None, None, None, :], src, 0)
    w_state = jnp.transpose(src, (0, 1, 3, 2, 4, 5, 6)).reshape(nq, _C_GL * 4 * P, blk)
    dec = jnp.stack([pr[L, 0], pi[L, 0], pr[L, 1], pi[L, 1]], 0).reshape(4, nq * npair, 2 * P)
    dec = jnp.broadcast_to(jnp.transpose(dec, (1, 0, 2))[:, :, None, :], (nq * npair, 4, 8, 2 * P))
    return w_sum.astype(BF), w_intra.astype(BF), w_state.astype(BF), dec


def _chunk_rows(x_ref, nb, rc):
    L = C_CHUNK
    rows = [jnp.concatenate([x_ref[b, pl.ds(j, rc, stride=L), :].astype(BF) for j in range(L)], axis=1)
            for b in range(nb)]
    return jnp.concatenate(rows, axis=0)


def _s5_sum_kernel(x_ref, w_ref, s_ref, *, nb, rc):
    s = _dot(_chunk_rows(x_ref, nb, rc), w_ref[...])
    for b in range(nb):
        for t in range(s_ref.shape[0]):
            s_ref[t, pl.ds(b, rc, stride=nb), :] = s[b * rc:(b + 1) * rc, LANES * t:LANES * (t + 1)]


def _s5_scan_kernel(s_ref, dec_ref, e_ref, *, nc, nb):
    units = s_ref.shape[0] // 4
    dec = [[dec_ref[u, k, 0:nb, :] for k in range(4)] for u in range(units)]

    def body(t, carry):
        rf = pl.ds(pl.multiple_of(t * nb, nb), nb)
        rb = pl.ds(pl.multiple_of((nc - 1 - t) * nb, nb), nb)
        new = []
        for u in range(units):
            fr, fi, br, bi = carry[4 * u:4 * u + 4]
            arf, aif, arb, aib = dec[u]
            e_ref[4 * u, rf, :] = fr
            e_ref[4 * u + 1, rf, :] = fi
            e_ref[4 * u + 2, rb, :] = br
            e_ref[4 * u + 3, rb, :] = bi
            new += [fr * arf - fi * aif + s_ref[4 * u, rf, :], fi * arf + fr * aif + s_ref[4 * u + 1, rf, :],
                    br * arb - bi * aib + s_ref[4 * u + 2, rb, :], bi * arb + br * aib + s_ref[4 * u + 3, rb, :]]
        return tuple(new)

    z = jnp.zeros((nb, LANES), F32)
    lax.fori_loop(0, nc, body, (z,) * (4 * units))


def _s5_out_kernel(x_ref, e_ref, wi_ref, wc_ref, y_ref, *, nb, rc):
    L = C_CHUNK
    half = pl.program_id(2)
    xc = _chunk_rows(x_ref, nb, rc)
    ec = jnp.concatenate(
        [jnp.concatenate([e_ref[t, pl.ds(b, rc, stride=nb), :].astype(BF) for t in range(e_ref.shape[0])], axis=1)
         for b in range(nb)], axis=0)
    y = _dot(xc, wi_ref[...]) + _dot(ec, wc_ref[...])
    for b in range(nb):
        for ii in range(L // 2):
            y_ref[b, pl.ds(half * (L // 2) + ii, rc, stride=L), :] = y[b * rc:(b + 1) * rc, LANES * ii:LANES * (ii + 1)]


def _s5(x, b, s, w_sum, w_intra, w_state, dec):
    L = C_CHUNK
    nc = s // L
    nq = D_MODEL // LANES
    nsl = w_sum.shape[2] // LANES
    blk = L * LANES
    rc = _tile(nc, max(8, 512 // b))
    x3 = x.reshape(b, s, D_MODEL)
    x_spec = pl.BlockSpec((b, rc * L, LANES), lambda q, c, *_: (0, c, q))
    sums = pl.pallas_call(
        functools.partial(_s5_sum_kernel, nb=b, rc=rc),
        grid=(nq, nc // rc),
        in_specs=[x_spec, pl.BlockSpec((None, blk, nsl * LANES), lambda q, c: (q, 0, 0))],
        out_specs=pl.BlockSpec((nsl, rc * b, LANES), lambda q, c: (q, c, 0)),
        out_shape=jax.ShapeDtypeStruct((nq * nsl, nc * b, LANES), F32),
        compiler_params=_params("parallel", "arbitrary"),
        name="s5_sum",
    )(x3, w_sum)
    upb = 2
    ent = pl.pallas_call(
        functools.partial(_s5_scan_kernel, nc=nc, nb=b),
        grid=(nq * nsl // (4 * upb),),
        in_specs=[pl.BlockSpec((4 * upb, nc * b, LANES), lambda i: (i, 0, 0)),
                  pl.BlockSpec((upb, 4, 8, LANES), lambda i: (i, 0, 0, 0))],
        out_specs=pl.BlockSpec((4 * upb, nc * b, LANES), lambda i: (i, 0, 0)),
        out_shape=jax.ShapeDtypeStruct((nq * nsl, nc * b, LANES), F32),
        compiler_params=_params("parallel"),
        name="s5_scan",
    )(sums, dec)
    y = pl.pallas_call(
        functools.partial(_s5_out_kernel, nb=b, rc=rc),
        grid=(nq, nc // rc, 2),
        in_specs=[x_spec,
                  pl.BlockSpec((nsl, rc * b, LANES), lambda q, c, h: (q, c, 0)),
                  pl.BlockSpec((None, blk, blk // 2), lambda q, c, h: (q, 0, h)),
                  pl.BlockSpec((None, nsl * LANES, blk // 2), lambda q, c, h: (q, 0, h))],
        out_specs=pl.BlockSpec((b, rc * L, LANES), lambda q, c, h: (0, c, q)),
        out_shape=jax.ShapeDtypeStruct((b, s, D_MODEL), F32),
        compiler_params=_params("parallel", "arbitrary", "arbitrary"),
        name="s5_out",
    )(x3, ent, w_intra, w_state)
    return y.reshape(b * s, D_MODEL)


_D_T = 512


def _prep_d(w_qkv, norm_gain, w_o, lambda_init):
    qk_w = D_HEADS * 2 * D_QK_DIM
    wq = _pad_heads(w_qkv[:, :qk_w] * (D_QK_DIM ** -0.5 * LOG2E), D_HEADS, 2 * D_QK_DIM)
    wk = _pad_heads(w_qkv[:, qk_w:2 * qk_w], D_HEADS, 2 * D_QK_DIM)
    wv = _pad_heads(w_qkv[:, 2 * qk_w:], D_HEADS, D_V_DIM)
    w = jnp.concatenate([wq, wk, wv], 1).astype(BF)
    gain = jnp.concatenate([norm_gain * (1.0 - lambda_init), jnp.zeros((LANES - D_V_DIM,), F32)])[None, :]
    wo = w_o.reshape(D_HEADS, D_V_DIM, D_MODEL)
    wo = jnp.concatenate([wo, jnp.zeros_like(wo)], 1).reshape(D_HEADS * LANES, D_MODEL).astype(BF)
    return w, gain, wo


def _diff_bias(rel_bias, t):
    rel0 = (jnp.arange(2 * t) + t) % (2 * t) - t
    rel = jnp.arange(-2, 3)[:, None] * t + rel0[None, :]
    vec = jnp.transpose(rel_bias[_bucket(rel)] * LOG2E, (2, 0, 1)).astype(F32)
    return _toeplitz(vec, t)


def _flash_d_kernel(q_ref, k_ref, v_ref, bias_ref, lam_ref, gain_ref, o_ref, m_ref, l_ref, acc_ref, *, nk,
                    lambda_init):
    t = q_ref.shape[0]
    i = pl.program_id(2)
    qv = q_ref[...].astype(F32)
    lane = lax.broadcasted_iota(I32, (t, LANES), 1)
    q0 = jnp.where(lane < D_QK_DIM, qv, 0.0).astype(BF)
    q1 = jnp.where((lane >= D_QK_DIM) & (lane < 2 * D_QK_DIM), qv, 0.0).astype(BF)
    q = jnp.concatenate([q0, q1], axis=0)
    _softmax_init(m_ref, l_ref, acc_ref)

    def body(j, carry):
        off = pl.multiple_of(j * t, t)
        bias = bias_ref[jnp.clip(j - i, -2, 2) + 2]
        s = _dot_nt(q, k_ref[pl.ds(off, t), :])
        s = (s.reshape(2, t, t) + bias[None]).reshape(2 * t, t)
        _softmax_step(s, v_ref[pl.ds(off, t), :], m_ref, l_ref, acc_ref)
        return carry

    lax.fori_loop(0, nk, body, 0, unroll=2)
    lf = lam_ref[...]
    lam = (jnp.exp(jnp.sum(lf[0:1] * lf[1:2], keepdims=True))
           - jnp.exp(jnp.sum(lf[2:3] * lf[3:4], keepdims=True)) + lambda_init)
    on = acc_ref[...] / jnp.sum(l_ref[...], -1, keepdims=True)
    o = on[:t] - lam * on[t:]
    ms = jnp.sum(o * o, -1, keepdims=True) * (1.0 / D_V_DIM)
    o_ref[...] = (o * lax.rsqrt(ms + RMS_EPS) * gain_ref[...]).astype(BF)


def _flash_d(qkv, bias, lam, gain, lambda_init):
    b, s, _ = qkv.shape
    t = bias.shape[-1]
    return pl.pallas_call(
        functools.partial(_flash_d_kernel, nk=s // t, lambda_init=lambda_init),
        grid=(b, D_HEADS, s // t),
        in_specs=[pl.BlockSpec((None, t, LANES), lambda bi, h, i: (bi, i, h)),
                  pl.BlockSpec((None, s, LANES), lambda bi, h, i: (bi, 0, D_HEADS + h)),
                  pl.BlockSpec((None, s, LANES), lambda bi, h, i: (bi, 0, 2 * D_HEADS + h)),
                  pl.BlockSpec((None, 5, t, t), lambda bi, h, i: (h, 0, 0, 0)),
                  pl.BlockSpec((4, D_QK_DIM), lambda bi, h, i: (0, 0)),
                  pl.BlockSpec((1, LANES), lambda bi, h, i: (0, 0))],
        out_specs=pl.BlockSpec((None, t, LANES), lambda bi, h, i: (bi, i, h)),
        out_shape=jax.ShapeDtypeStruct((b, s, D_HEADS * LANES), BF),
        scratch_shapes=[pltpu.VMEM((2 * t, LANES), F32)] * 3,
        compiler_params=_params("parallel", "parallel", "arbitrary"),
        name="flash_d",
    )(qkv, qkv, qkv, bias, lam, gain)


def _cross_kernel(x_ref, kv_ref, wq_ref, wo_ref, g_ref, b_ref, wr_ref, y_ref, ybf_ref, aff_ref):
    x = x_ref[...]
    tm = x.shape[0]
    q = (_dot(x.astype(BF), wq_ref[...]) * (X_HEAD_DIM ** -0.5)).astype(BF)
    outs = []
    for h in range(X_HEADS):
        sl = slice(X_HEAD_DIM * h, X_HEAD_DIM * (h + 1))
        s = _dot_nt(q[:, sl], kv_ref[:, sl])
        p = jnp.exp(s - jnp.max(s, -1, keepdims=True))
        l = jnp.sum(p, -1, keepdims=True)
        vh = kv_ref[:, D_MODEL + X_HEAD_DIM * h:D_MODEL + X_HEAD_DIM * (h + 1)]
        outs.append((_dot(p.astype(BF), vh) / l).astype(BF))
    o = jnp.concatenate(outs, axis=1)
    y = _ln_rows(ALPHA * x + _dot(o, wo_ref[...]), g_ref[...], b_ref[...])
    y_ref[...] = y
    yh = y.astype(BF)
    ybf_ref[...] = yh
    yl = (y - yh.astype(F32)).astype(BF)
    wr = wr_ref[...]
    wh = wr.astype(BF)
    wl = (wr - wh.astype(F32)).astype(BF)
    lg = _dot_nt(wh, yh) + _dot_nt(wh, yl) + _dot_nt(wl, yh)
    e = jnp.exp(lg - jnp.max(lg, 0, keepdims=True))
    aff = e / jnp.sum(e, 0, keepdims=True)
    for c in range(tm // LANES):
        aff_ref[c] = aff[:, LANES * c:LANES * (c + 1)]


def _cross(x, kv, wq, wo, g, b, wr_t, s, mem_len, tm=512):
    n = x.shape[0]
    tm = _tile(s, tm)
    per = s // tm
    fixed = lambda i: (0, 0)
    return pl.pallas_call(
        _cross_kernel,
        grid=(n // tm,),
        in_specs=[pl.BlockSpec((tm, D_MODEL), lambda i: (i, 0)),
                  pl.BlockSpec((mem_len, 2 * D_MODEL), lambda i: (i // per, 0)),
                  pl.BlockSpec((D_MODEL, D_MODEL), fixed), pl.BlockSpec((D_MODEL, D_MODEL), fixed),
                  pl.BlockSpec((1, D_MODEL), fixed), pl.BlockSpec((1, D_MODEL), fixed),
                  pl.BlockSpec((N_EXPERTS, D_MODEL), fixed)],
        out_specs=[pl.BlockSpec((tm, D_MODEL), lambda i: (i, 0)),
                   pl.BlockSpec((tm, D_MODEL), lambda i: (i, 0)),
                   pl.BlockSpec((tm // LANES, N_EXPERTS, LANES), lambda i: (i, 0, 0))],
        out_shape=[jax.ShapeDtypeStruct((n, D_MODEL), F32),
                   jax.ShapeDtypeStruct((n, D_MODEL), BF),
                   jax.ShapeDtypeStruct((n // LANES, N_EXPERTS, LANES), F32)],
        compiler_params=_params("parallel"),
        name="cross",
    )(x, kv, wq, wo, g, b, wr_t)


def _select_kernel(a_ref, pos_ref, st_ref, *, k, nbits):
    nt = a_ref.shape[0]
    shape = (nt, N_EXPERTS, LANES)
    kf = float(k)

    def keys():
        return lax.bitcast_convert_type(a_ref[...], I32)

    def count(mask):
        c = jnp.sum(jnp.where(mask, 1.0, 0.0), axis=0, keepdims=True)
        return jnp.sum(c, axis=2, keepdims=True)

    def value_step(it, thr):
        cand = thr | jnp.left_shift(jnp.int32(1), 30 - it)
        return jnp.where(count(keys() >= cand) >= kf, cand, thr)

    thr = lax.fori_loop(0, 31, value_step, jnp.zeros((1, N_EXPERTS, 1), I32))
    need = kf - count(keys() > thr)
    idx = lax.broadcasted_iota(I32, shape, 0) * LANES + lax.broadcasted_iota(I32, shape, 2)

    def index_step(it, ithr):
        cand = ithr | jnp.left_shift(jnp.int32(1), nbits - 1 - it)
        return jnp.where(count((keys() == thr) & (idx < cand)) < need, cand, ithr)

    ithr = lax.fori_loop(0, nbits, index_step, jnp.zeros((1, N_EXPERTS, 1), I32))
    thr2, ithr2 = thr[0], ithr[0]
    upper = jnp.where(lax.broadcasted_iota(I32, (LANES, LANES), 0) <= lax.broadcasted_iota(I32, (LANES, LANES), 1),
                      1.0, 0.0).astype(BF)
    lane = lax.broadcasted_iota(I32, (N_EXPERTS, LANES), 1)

    def tile_step(j, carry):
        kj = lax.bitcast_convert_type(a_ref[j], I32)
        sel = (kj > thr2) | ((kj == thr2) & (j * LANES + lane <= ithr2))
        m = jnp.where(sel, 1.0, 0.0)
        inc = _dot(m.astype(BF), upper)
        pos_ref[j] = jnp.where(sel, inc - m + carry, -1.0).astype(I32)
        st_ref[j] = jnp.broadcast_to(carry, (N_EXPERTS, LANES)).astype(I32)
        return carry + inc[:, LANES - 1:LANES]

    lax.fori_loop(0, nt, tile_step, jnp.zeros((N_EXPERTS, 1), F32))


def _select(aff3, k):
    nt = aff3.shape[0]
    nbits = max(1, int(math.ceil(math.log2(nt * LANES))))
    shp = jax.ShapeDtypeStruct(aff3.shape, I32)
    return pl.pallas_call(
        functools.partial(_select_kernel, k=k, nbits=nbits),
        out_shape=[shp, shp],
        compiler_params=pltpu.CompilerParams(vmem_limit_bytes=VMEM_LIMIT),
        name="select",
    )(aff3)


_GATHER_ROWS = LANES + BF16_ROWS


def _moe_ffn_kernel(st_ref, x_ref, pos_ref, aff_ref, wg_ref, wu_ref, wd_ref, ye_ref, buf_ref, gate_ref, *, nb, ncf,
                    sub, ck):
    e = pl.program_id(0)
    t = pl.program_id(1)

    @pl.when(t == 0)
    def _():
        buf_ref[...] = jnp.zeros_like(buf_ref)
        gate_ref[...] = jnp.zeros_like(gate_ref)

    @pl.when(t < nb)
    def _():
        row = lax.broadcasted_iota(I32, (_GATHER_ROWS, LANES), 0)
        for s in range(sub):
            st = st_ref[e, t * sub + s]
            base = pl.multiple_of((st // BF16_ROWS) * BF16_ROWS, BF16_ROWS)
            hit = row == pos_ref[s, pl.ds(e, 1), :] - base
            c = _dot(jnp.where(hit, 1.0, 0.0).astype(BF), x_ref[LANES * s:LANES * (s + 1), :])
            buf_ref[pl.ds(base, _GATHER_ROWS), :] = buf_ref[pl.ds(base, _GATHER_ROWS), :] + c.astype(BF)
            g = jnp.sum(jnp.where(hit, aff_ref[s, pl.ds(e, 1), :], 0.0), -1, keepdims=True)
            gate_ref[pl.ds(base, _GATHER_ROWS), :] = gate_ref[pl.ds(base, _GATHER_ROWS), :] + g

    @pl.when(t >= nb)
    def _():
        c = t - nb

        @pl.when(c < ncf)
        def _():
            rows = pl.ds(pl.multiple_of(c * ck, ck), ck)
            xe = buf_ref[rows, :]
            h = (jax.nn.silu(_dot(xe, wg_ref[...])) * _dot(xe, wu_ref[...])).astype(BF)
            ye_ref[...] = (_dot(h, wd_ref[...]) * gate_ref[rows, :]).astype(BF)

        @pl.when(c >= ncf)
        def _():
            ye_ref[...] = jnp.zeros_like(ye_ref)


def _moe_ffn(starts, xbf, pos3, aff3, wg, wu, wd, cap, tb=1024):
    n = xbf.shape[0]
    tb = _tile(n, tb)
    nb = n // tb
    sub = tb // LANES
    ck = _tile(cap, 512)
    ncf = cap // ck
    nch = ncf + pl.cdiv(_WIN_STEP + _GATHER_ROWS, ck)
    dff = wg.shape[-1]
    blk = lambda e, t, st: (jnp.minimum(t, nb - 1), 0)
    return pl.pallas_call(
        functools.partial(_moe_ffn_kernel, nb=nb, ncf=ncf, sub=sub, ck=ck),
        grid_spec=pltpu.PrefetchScalarGridSpec(
            num_scalar_prefetch=1,
            grid=(N_EXPERTS, nb + nch),
            in_specs=[pl.BlockSpec((tb, D_MODEL), blk),
                      pl.BlockSpec((sub, N_EXPERTS, LANES), lambda e, t, st: (jnp.minimum(t, nb - 1), 0, 0)),
                      pl.BlockSpec((sub, N_EXPERTS, LANES), lambda e, t, st: (jnp.minimum(t, nb - 1), 0, 0)),
                      pl.BlockSpec((None, D_MODEL, dff), lambda e, t, st: (e, 0, 0)),
                      pl.BlockSpec((None, D_MODEL, dff), lambda e, t, st: (e, 0, 0)),
                      pl.BlockSpec((None, dff, D_MODEL), lambda e, t, st: (e, 0, 0))],
            out_specs=pl.BlockSpec((None, ck, D_MODEL), lambda e, t, st: (e, jnp.maximum(t - nb, 0), 0)),
            scratch_shapes=[pltpu.VMEM((cap + _GATHER_ROWS, D_MODEL), BF),
                            pltpu.VMEM((cap + _GATHER_ROWS, 1), F32)]),
        out_shape=jax.ShapeDtypeStruct((N_EXPERTS, nch * ck, D_MODEL), BF),
        compiler_params=_params("arbitrary", "arbitrary"),
        name="moe_ffn",
    )(starts, xbf, pos3, aff3, wg, wu, wd)


_WIN_STEP = 256
_WIN_ROWS = _WIN_STEP + _GATHER_ROWS


def _combine_kernel(st_ref, x_ref, pos_ref, g_ref, b_ref, *rest):
    ye_refs, y_ref = rest[:N_EXPERTS], rest[N_EXPERTS]
    j = pl.program_id(0)
    col = lax.broadcasted_iota(I32, (LANES, _GATHER_ROWS), 1)
    acc = None
    for e in range(N_EXPERTS):
        st = st_ref[e, j]
        base = (st // BF16_ROWS) * BF16_ROWS
        sub = pl.multiple_of(base - (st // _WIN_STEP) * _WIN_STEP, BF16_ROWS)
        onehot = jnp.where(col == pos_ref[:, e:e + 1] - base, 1.0, 0.0).astype(BF)
        d = _dot(onehot, ye_refs[e][0, pl.ds(sub, _GATHER_ROWS), :])
        acc = d if acc is None else acc + d
    y_ref[...] = _ln_rows(ALPHA * x_ref[...] + acc, g_ref[...], b_ref[...])


def _combine(starts, x, pos, g, b, ye):
    n = x.shape[0]
    assert ye.shape[1] >= EC_CAPACITY * n // N_EXPERTS + _WIN_ROWS
    row = lambda j, st: (j, 0)
    fixed = lambda j, st: (0, 0)

    def ye_spec(e):
        return pl.BlockSpec((pl.Element(1), pl.Element(_WIN_ROWS), pl.Element(D_MODEL)),
                            lambda j, st: (e, (st[e, j] // _WIN_STEP) * _WIN_STEP, 0))

    return pl.pallas_call(
        _combine_kernel,
        grid_spec=pltpu.PrefetchScalarGridSpec(
            num_scalar_prefetch=1,
            grid=(n // LANES,),
            in_specs=[pl.BlockSpec((LANES, D_MODEL), row),
                      pl.BlockSpec((LANES, N_EXPERTS), row),
                      pl.BlockSpec((1, D_MODEL), fixed), pl.BlockSpec((1, D_MODEL), fixed)]
            + [ye_spec(e) for e in range(N_EXPERTS)],
            out_specs=pl.BlockSpec((LANES, D_MODEL), row)),
        out_shape=jax.ShapeDtypeStruct((n, D_MODEL), F32),
        compiler_params=_params("arbitrary"),
        name="combine",
    )(starts, x, pos, g, b, *([ye] * N_EXPERTS))


def _moe(x, xg, aff3, wg, wu, wd, g, b):
    n = x.shape[0]
    cap = EC_CAPACITY * n // N_EXPERTS
    pos3, st3 = _select(aff3, cap)
    starts = jnp.transpose(st3[:, :, 0])
    ye = _moe_ffn(starts, xg, pos3, aff3, wg, wu, wd, cap)
    pos = jnp.transpose(pos3, (0, 2, 1)).reshape(n, N_EXPERTS)
    return _combine(starts, x, pos, g, b, ye)


def _prep_weights(p):
    w = {}
    w['a'] = [_prep_a(p['a_w_qkv'][j], p['a_q_gain'][j], p['a_k_gain'][j], p['a_w_o'][j])
              for j in range(p['a_w_qkv'].shape[0])]
    w['b'] = [_prep_b(p['b_w_qkv'][j], p['b_w_o'][j]) for j in range(p['b_w_qkv'].shape[0])]
    w['c'] = [_prep_c(p['c_lam_re'][j], p['c_lam_im'][j], p['c_log_dt'][j], p['c_b_re'][j], p['c_b_im'][j],
                      p['c_c_re'][j], p['c_c_im'][j]) + (p['c_d'][j][None, :], p['c_w_glu'][j].astype(BF))
              for j in range(p['c_lam_re'].shape[0])]
    w['d'] = []
    for j in range(p['d_w_qkv'].shape[0]):
        layer = N_MIXERS * j + 3
        lambda_init = 0.8 - 0.6 * math.exp(-0.3 * layer)
        w['d'].append(_prep_d(p['d_w_qkv'][j], p['d_norm_gain'][j], p['d_w_o'][j], lambda_init)
                      + (p['d_lam'][j].astype(F32), lambda_init))
    w['diff_bias'] = _diff_bias(p['rel_bias'], _D_T)
    w['x_w_q'] = p['x_w_q'].astype(BF)
    w['x_w_kv'] = p['x_w_kv'].astype(BF)
    w['x_w_o'] = p['x_w_o'].astype(BF)
    w['router_t'] = jnp.transpose(p['moe_w_router'], (0, 2, 1)).astype(F32)
    w['moe_w_gate'] = p['moe_w_gate'].astype(BF)
    w['moe_w_up'] = p['moe_w_up'].astype(BF)
    w['moe_w_down'] = p['moe_w_down'].astype(BF)
    return w


def _trunk(x, mem, p, w):
    b, s, _ = x.shape
    n = b * s
    mem_len = mem.shape[1]
    x = x.reshape(n, D_MODEL)
    mem2 = mem.reshape(b * mem_len, D_MODEL)
    ln_g, ln_b = p['ln_g'], p['ln_b']
    for i in range(DEPTH):
        m, j = i % N_MIXERS, i // N_MIXERS
        g0, b0 = ln_g[i, 0][None, :], ln_b[i, 0][None, :]
        if m == 0:
            wa, gains, wo = w['a'][j]
            qkv = _proj_a(x, wa, _rope_table(s), gains, s)
            o = _flash_a(qkv.reshape(b, s, _A_COLS))
            x = _post(x, o.reshape(n, A_HEADS * LANES), wo, g0, b0)
        elif m == 1:
            wb, wo = w['b'][j]
            qkv = _proj(x, wb).reshape(b, s, _B_TILES * LANES)
            os_, ls_ = [], []
            for g in range(len(B_PATTERNS)):
                t = _tile(s, _B_T)
                o, lse = _band_attention(qkv, _band_bias(p['rel_bias'], g, t), g, t)
                os_.append(o)
                ls_.append(lse)
            x = _post_b(x, os_, ls_, wo, g0, b0)
        elif m == 2:
            w_sum, w_intra, w_state, dec, dskip, wglu = w['c'][j]
            ys = _s5(x, b, s, w_sum, w_intra, w_state, dec)
            x = _post_c(x, ys, dskip, wglu, g0, b0)
        else:
            wd, gain, wo, lam, lambda_init = w['d'][j]
            qkv = _proj(x, wd).reshape(b, s, 3 * D_HEADS * LANES)
            o = _flash_d(qkv, w['diff_bias'], lam, gain, lambda_init)
            x = _post(x, o.reshape(n, D_HEADS * LANES), wo, g0, b0)
        kv = _proj(mem2, w['x_w_kv'][i], tm=mem_len)
        x, xbf, aff3 = _cross(x, kv, w['x_w_q'][i], w['x_w_o'][i], ln_g[i, 1][None, :], ln_b[i, 1][None, :],
                              w['router_t'][i], s, mem_len)
        x = _moe(x, xbf, aff3, w['moe_w_gate'][i], w['moe_w_up'][i], w['moe_w_down'][i],
                 ln_g[i, 2][None, :], ln_b[i, 2][None, :])
    return x.reshape(b, s, D_MODEL)


def kernel(x_prompt, x_sample, mem_prompt, mem_sample, rel_bias, ln_g, ln_b, a_w_qkv, a_q_gain, a_k_gain, a_w_o, b_w_qkv, b_w_o, c_lam_re, c_lam_im, c_log_dt, c_b_re, c_b_im, c_c_re, c_c_im, c_d, c_w_glu, d_w_qkv, d_lam, d_norm_gain, d_w_o, x_w_q, x_w_kv, x_w_o, moe_w_router, moe_w_gate, moe_w_up, moe_w_down):
    p = dict(rel_bias=rel_bias, ln_g=ln_g, ln_b=ln_b,
             a_w_qkv=a_w_qkv, a_q_gain=a_q_gain, a_k_gain=a_k_gain, a_w_o=a_w_o,
             b_w_qkv=b_w_qkv, b_w_o=b_w_o,
             c_lam_re=c_lam_re, c_lam_im=c_lam_im, c_log_dt=c_log_dt, c_b_re=c_b_re, c_b_im=c_b_im,
             c_c_re=c_c_re, c_c_im=c_c_im, c_d=c_d, c_w_glu=c_w_glu,
             d_w_qkv=d_w_qkv, d_lam=d_lam, d_norm_gain=d_norm_gain, d_w_o=d_w_o,
             x_w_q=x_w_q, x_w_kv=x_w_kv, x_w_o=x_w_o,
             moe_w_router=moe_w_router, moe_w_gate=moe_w_gate, moe_w_up=moe_w_up, moe_w_down=moe_w_down)
    w = _prep_weights(p)
    return (_trunk(x_prompt, mem_prompt, p, w), _trunk(x_sample, mem_sample, p, w))
```

```python
import functools
import math

import numpy as np
import jax
import jax.numpy as jnp
from jax import lax
from jax.experimental import pallas as pl
from jax.experimental.pallas import tpu as pltpu

F32 = jnp.float32
BF = jnp.bfloat16
I32 = jnp.int32

D_MODEL = 1024
DEPTH = 4
GRID_W = 64
N_MIXERS = 4
LN_EPS = 1e-5
RMS_EPS = 1e-6
ALPHA = (2.0 * DEPTH) ** 0.25

A_HEADS = 16
A_KV_HEADS = 4
A_HEAD_DIM = 64
ROPE_BASE = 10000.0

B_PATTERNS = ((128, 1), (512, 4), (2048, 16))
B_HEADS_PER_GROUP = 4
B_HEAD_DIM = 64

C_GROUP = 16
C_N_GROUPS = D_MODEL // C_GROUP
C_STATE = 64
C_CHUNK = 16
_C_GL = 128 // C_GROUP

D_HEADS = 12
D_QK_DIM = 32
D_V_DIM = 64

REL_BUCKETS = 32
REL_MAX_DIST = 128

X_HEADS = 4
X_HEAD_DIM = D_MODEL // X_HEADS

N_EXPERTS = 16
EC_CAPACITY = 2

LANES = 128
BF16_ROWS = 16
VMEM_LIMIT = 56 * 1024 * 1024
NEG = -1e30
LOG2E = math.log2(math.e)


def _params(*sem):
    return pltpu.CompilerParams(dimension_semantics=sem, vmem_limit_bytes=VMEM_LIMIT)


def _tile(n, pref):
    t = min(n, pref)
    assert n % t == 0, (n, pref)
    return t


def _ln_rows(v, g, b):
    mu = jnp.mean(v, -1, keepdims=True)
    c = v - mu
    var = jnp.mean(c * c, -1, keepdims=True)
    return c * lax.rsqrt(var + LN_EPS) * g + b


def _dot_nt(a, b):
    return lax.dot_general(a, b, (((1,), (1,)), ((), ())), preferred_element_type=F32)


def _dot(a, b):
    return jnp.dot(a, b, preferred_element_type=F32)


_ONE_LANE = 64


def _with_ones_lane(y):
    lane = lax.broadcasted_iota(I32, y.shape, 1)
    return jnp.where(lane % LANES == _ONE_LANE, 1.0, y)


def _proj_kernel(x_ref, w_ref, o_ref, *, ones_from):
    y = _dot(x_ref[...].astype(BF), w_ref[...])
    if ones_from is None:
        o_ref[...] = y.astype(o_ref.dtype)
    else:
        o_ref[:, :ones_from] = y[:, :ones_from].astype(o_ref.dtype)
        o_ref[:, ones_from:] = _with_ones_lane(y[:, ones_from:]).astype(o_ref.dtype)


def _proj(x, w, tm=512, ones_from=None):
    n, k = x.shape
    m = w.shape[1]
    tm = _tile(n, tm)
    return pl.pallas_call(
        functools.partial(_proj_kernel, ones_from=ones_from),
        grid=(n // tm,),
        in_specs=[pl.BlockSpec((tm, k), lambda i: (i, 0)),
                  pl.BlockSpec((k, m), lambda i: (0, 0))],
        out_specs=pl.BlockSpec((tm, m), lambda i: (i, 0)),
        out_shape=jax.ShapeDtypeStruct((n, m), BF),
        compiler_params=_params("parallel"),
        name="proj",
    )(x, w)


def _post_kernel(x_ref, o_ref, w_ref, g_ref, b_ref, y_ref):
    h = _dot(o_ref[...], w_ref[...])
    y_ref[...] = _ln_rows(ALPHA * x_ref[...] + h, g_ref[...], b_ref[...])


def _post(x, o, w, g, b, tm=512):
    n = x.shape[0]
    ko = o.shape[1]
    tm = _tile(n, tm)
    return pl.pallas_call(
        _post_kernel,
        grid=(n // tm,),
        in_specs=[pl.BlockSpec((tm, D_MODEL), lambda i: (i, 0)),
                  pl.BlockSpec((tm, ko), lambda i: (i, 0)),
                  pl.BlockSpec((ko, D_MODEL), lambda i: (0, 0)),
                  pl.BlockSpec((1, D_MODEL), lambda i: (0, 0)),
                  pl.BlockSpec((1, D_MODEL), lambda i: (0, 0))],
        out_specs=pl.BlockSpec((tm, D_MODEL), lambda i: (i, 0)),
        out_shape=jax.ShapeDtypeStruct((n, D_MODEL), F32),
        compiler_params=_params("parallel"),
        name="post",
    )(x, o, w, g, b)


def _post_b_kernel(x_ref, o0_ref, o1_ref, o2_ref, l0_ref, l1_ref, l2_ref, w_ref, g_ref, b_ref, y_ref):
    l0, l1, l2 = l0_ref[...], l1_ref[...], l2_ref[...]
    m = jnp.maximum(jnp.maximum(l0, l1), l2)
    e0, e1, e2 = jnp.exp(l0 - m), jnp.exp(l1 - m), jnp.exp(l2 - m)
    inv = 1.0 / (e0 + e1 + e2)
    gw = B_HEADS_PER_GROUP * LANES
    h = _dot((o0_ref[...].astype(F32) * (e0 * inv)).astype(BF), w_ref[0:gw, :])
    h = h + _dot((o1_ref[...].astype(F32) * (e1 * inv)).astype(BF), w_ref[gw:2 * gw, :])
    h = h + _dot((o2_ref[...].astype(F32) * (e2 * inv)).astype(BF), w_ref[2 * gw:3 * gw, :])
    y_ref[...] = _ln_rows(ALPHA * x_ref[...] + h, g_ref[...], b_ref[...])


def _post_b(x, os_, ls_, w, g, b, tm=512):
    n = x.shape[0]
    tm = _tile(n, tm)
    gw = B_HEADS_PER_GROUP * LANES
    row = lambda i: (i, 0)
    fixed = lambda i: (0, 0)
    return pl.pallas_call(
        _post_b_kernel,
        grid=(n // tm,),
        in_specs=[pl.BlockSpec((tm, D_MODEL), row)] + [pl.BlockSpec((tm, gw), row)] * 6
        + [pl.BlockSpec((3 * gw, D_MODEL), fixed), pl.BlockSpec((1, D_MODEL), fixed),
           pl.BlockSpec((1, D_MODEL), fixed)],
        out_specs=pl.BlockSpec((tm, D_MODEL), row),
        out_shape=jax.ShapeDtypeStruct((n, D_MODEL), F32),
        compiler_params=_params("parallel"),
        name="post_b",
    )(x, *os_, *ls_, w, g, b)


def _post_c_kernel(x_ref, ys_ref, d_ref, w_ref, g_ref, b_ref, y_ref):
    x = x_ref[...]
    z = jax.nn.gelu(ys_ref[...] + d_ref[...] * x).astype(BF)
    h = _dot(z, w_ref[...])
    hh = h[:, :D_MODEL] * jax.nn.sigmoid(h[:, D_MODEL:])
    y_ref[...] = _ln_rows(ALPHA * x + hh, g_ref[...], b_ref[...])


def _post_c(x, ys, d, w, g, b, tm=512):
    n = x.shape[0]
    tm = _tile(n, tm)
    row = lambda i: (i, 0)
    fixed = lambda i: (0, 0)
    return pl.pallas_call(
        _post_c_kernel,
        grid=(n // tm,),
        in_specs=[pl.BlockSpec((tm, D_MODEL), row), pl.BlockSpec((tm, D_MODEL), row),
                  pl.BlockSpec((1, D_MODEL), fixed), pl.BlockSpec((D_MODEL, 2 * D_MODEL), fixed),
                  pl.BlockSpec((1, D_MODEL), fixed), pl.BlockSpec((1, D_MODEL), fixed)],
        out_specs=pl.BlockSpec((tm, D_MODEL), row),
        out_shape=jax.ShapeDtypeStruct((n, D_MODEL), F32),
        compiler_params=_params("parallel"),
        name="post_c",
    )(x, ys, d, w, g, b)


_A_QK_TILES = A_HEADS + A_KV_HEADS
_A_COLS = (_A_QK_TILES + A_KV_HEADS) * LANES


def _rope_partner():
    d = np.arange(A_HEAD_DIM)
    e = d % (A_HEAD_DIM // 2)
    lo = e < A_HEAD_DIM // 4
    return np.where(lo, d + A_HEAD_DIM // 4, d - A_HEAD_DIM // 4), np.where(lo, -1.0, 1.0).astype(np.float32)


def _prep_a(w_qkv, q_gain, k_gain, w_o):
    partner, sign = _rope_partner()
    nqk = _A_QK_TILES * A_HEAD_DIM
    wqk = w_qkv[:, :nqk].reshape(D_MODEL, _A_QK_TILES, A_HEAD_DIM)
    wsw = wqk[:, :, partner] * sign
    wqk = jnp.concatenate([wqk, wsw], -1).reshape(D_MODEL, _A_QK_TILES * LANES)
    wv = w_qkv[:, nqk:].reshape(D_MODEL, A_KV_HEADS, A_HEAD_DIM)
    wv = jnp.concatenate([wv, jnp.zeros_like(wv)], -1).reshape(D_MODEL, A_KV_HEADS * LANES)
    w = jnp.concatenate([wqk, wv], 1).astype(BF)
    gq = jnp.concatenate([q_gain, q_gain[partner]]) * (A_HEAD_DIM ** -0.5 * 0.5 * LOG2E)
    gk = jnp.concatenate([k_gain, k_gain[partner]])
    gains = jnp.stack([gq, gk], 0)
    wo = w_o.reshape(A_HEADS, A_HEAD_DIM, D_MODEL)
    wo = jnp.concatenate([wo, jnp.zeros_like(wo)], 1).reshape(A_HEADS * LANES, D_MODEL).astype(BF)
    return w, gains, wo


def _rope_table(s):
    pos = jnp.arange(s)
    rows, cols = (pos // GRID_W).astype(F32), (pos % GRID_W).astype(F32)
    half = A_HEAD_DIM // 2
    freqs = ROPE_BASE ** (-jnp.arange(0, half, 2, dtype=F32) / half)
    ang_r = rows[:, None] * freqs
    ang_c = cols[:, None] * freqs
    ang = jnp.concatenate([ang_r, ang_r, ang_c, ang_c], -1)
    return jnp.concatenate([jnp.cos(ang), jnp.sin(ang)], -1)


def _proj_a_kernel(x_ref, w_ref, cs_ref, g_ref, o_ref):
    y = _dot(x_ref[...].astype(BF), w_ref[...])
    cs = cs_ref[...]
    gq = cs * g_ref[0:1, :]
    gk = cs * g_ref[1:2, :]
    for h in range(_A_QK_TILES):
        t = y[:, LANES * h:LANES * (h + 1)]
        r = lax.rsqrt(jnp.sum(t * t, -1, keepdims=True) * (1.0 / LANES) + RMS_EPS)
        e = t * r * (gq if h < A_HEADS else gk)
        o_ref[:, LANES * h:LANES * (h + 1)] = (e + pltpu.roll(e, LANES // 2, 1)).astype(BF)
    o_ref[:, _A_QK_TILES * LANES:] = _with_ones_lane(y[:, _A_QK_TILES * LANES:]).astype(BF)


def _proj_a(x, w, cs, gains, s, tm=512):
    n = x.shape[0]
    tm = _tile(s, tm)
    per = s // tm
    return pl.pallas_call(
        _proj_a_kernel,
        grid=(n // tm,),
        in_specs=[pl.BlockSpec((tm, D_MODEL), lambda i: (i, 0)),
                  pl.BlockSpec((D_MODEL, _A_COLS), lambda i: (0, 0)),
                  pl.BlockSpec((tm, LANES), lambda i: (i % per, 0)),
                  pl.BlockSpec((2, LANES), lambda i: (0, 0))],
        out_specs=pl.BlockSpec((tm, _A_COLS), lambda i: (i, 0)),
        out_shape=jax.ShapeDtypeStruct((n, _A_COLS), BF),
        compiler_params=_params("parallel"),
        name="proj_a",
    )(x, w, cs, gains)


def _softmax_step(s, v, m_ref, acc_ref):
    tk = s.shape[1]
    m_old = m_ref[...]
    m_new = jnp.maximum(m_old, jnp.max(s, -1, keepdims=True))
    p = jnp.concatenate([jnp.exp2(s[:, LANES * c:LANES * (c + 1)] - m_new).astype(BF) for c in range(tk // LANES)],
                        axis=1)
    acc_ref[...] = jnp.exp2(m_old - m_new) * acc_ref[...] + _dot(p, v)
    m_ref[...] = m_new


def _softmax_init(m_ref, acc_ref):
    m_ref[...] = jnp.full(m_ref.shape, -jnp.inf, F32)
    acc_ref[...] = jnp.zeros_like(acc_ref)


def _flash_a_kernel(q_ref, k_ref, v_ref, o_ref, m_ref, acc_ref, *, tk, nk):
    tq = q_ref.shape[0]
    rep = A_HEADS // A_KV_HEADS
    q = jnp.concatenate([q_ref[:, LANES * r:LANES * (r + 1)] for r in range(rep)], axis=0)
    _softmax_init(m_ref, acc_ref)

    def body(j, carry):
        off = pl.multiple_of(j * tk, tk)
        _softmax_step(_dot_nt(q, k_ref[pl.ds(off, tk), :]), v_ref[pl.ds(off, tk), :], m_ref, acc_ref)
        return carry

    lax.fori_loop(0, nk, body, 0, unroll=2)
    acc = acc_ref[...]
    o = acc / acc[:, _ONE_LANE:_ONE_LANE + 1]
    for r in range(rep):
        o_ref[:, LANES * r:LANES * (r + 1)] = o[r * tq:(r + 1) * tq].astype(BF)


def _flash_a(qkv, tq=256, tk=1024):
    b, s, _ = qkv.shape
    tq, tk = _tile(s, tq), _tile(s, tk)
    rep = A_HEADS // A_KV_HEADS
    gw = rep * LANES
    return pl.pallas_call(
        functools.partial(_flash_a_kernel, tk=tk, nk=s // tk),
        grid=(b, A_KV_HEADS, s // tq),
        in_specs=[pl.BlockSpec((None, tq, gw), lambda bi, g, i: (bi, i, g)),
                  pl.BlockSpec((None, s, LANES), lambda bi, g, i: (bi, 0, A_HEADS + g)),
                  pl.BlockSpec((None, s, LANES), lambda bi, g, i: (bi, 0, _A_QK_TILES + g))],
        out_specs=pl.BlockSpec((None, tq, gw), lambda bi, g, i: (bi, i, g)),
        out_shape=jax.ShapeDtypeStruct((b, s, A_HEADS * LANES), BF),
        scratch_shapes=[pltpu.VMEM((rep * tq, LANES), F32)] * 2,
        compiler_params=_params("parallel", "parallel", "arbitrary"),
        name="flash_a",
    )(qkv, qkv, qkv)


def _bucket(rel):
    half = REL_BUCKETS // 2
    max_exact = half // 2
    n = jnp.abs(rel)
    large = max_exact + (jnp.log(jnp.maximum(n, 1).astype(F32) / max_exact)
                         / math.log(REL_MAX_DIST / max_exact) * (half - max_exact)).astype(I32)
    large = jnp.minimum(large, half - 1)
    return jnp.where(rel > 0, half, 0) + jnp.where(n < max_exact, n, large)


_B_TILES = 3 * len(B_PATTERNS) * B_HEADS_PER_GROUP
_B_GW = B_HEADS_PER_GROUP * LANES
_B_T = 256


def _pad_heads(w, heads, dim):
    w = w.reshape(w.shape[0], heads, dim)
    return jnp.concatenate([w, jnp.zeros((w.shape[0], heads, LANES - dim), w.dtype)], -1).reshape(
        w.shape[0], heads * LANES)


def _prep_b(w_qkv, w_o):
    nh = len(B_PATTERNS) * B_HEADS_PER_GROUP
    c = nh * B_HEAD_DIM
    wq = _pad_heads(w_qkv[:, :c] * (B_HEAD_DIM ** -0.5), nh, B_HEAD_DIM)
    wk = _pad_heads(w_qkv[:, c:2 * c], nh, B_HEAD_DIM)
    wv = _pad_heads(w_qkv[:, 2 * c:], nh, B_HEAD_DIM)
    w = jnp.concatenate([wq, wk, wv], 1).astype(BF)
    wo = w_o.reshape(nh, B_HEAD_DIM, D_MODEL)
    wo = jnp.concatenate([wo, jnp.zeros_like(wo)], 1).reshape(nh * LANES, D_MODEL).astype(BF)
    return w, wo


def _toeplitz(vec, t):
    flat = jnp.tile(vec, (1,) * (vec.ndim - 1) + (t,))[..., :t * (2 * t - 1)]
    return flat.reshape(vec.shape[:-1] + (t, 2 * t - 1))[..., :t]


def _band_tiles(g, t):
    window, dil = B_PATTERNS[g]
    return pl.cdiv((window // (2 * dil)) * dil, t)


def _band_bias(rel_bias, g, t):
    window, dil = B_PATTERNS[g]
    reach = (window // (2 * dil)) * dil
    n = _band_tiles(g, t)
    rel0 = (jnp.arange(2 * t) + t) % (2 * t) - t
    rel = jnp.arange(-n, n + 1)[:, None] * t + rel0[None, :]
    bias = rel_bias[_bucket(rel)][:, :, g * B_HEADS_PER_GROUP:(g + 1) * B_HEADS_PER_GROUP]
    bias = jnp.where(((rel % dil == 0) & (jnp.abs(rel) <= reach))[:, :, None], bias, NEG)
    return _toeplitz(jnp.transpose(bias, (2, 0, 1)).astype(F32), t)


def _band_kernel(q_ref, k_ref, v_ref, bias_ref, o_ref, lse_ref, *, n, nk):
    t = q_ref.shape[0]
    i = pl.program_id(2)
    q = q_ref[...]
    logits, offs = [], []
    for o in range(-n, n + 1):
        j = i + o
        off = pl.multiple_of(jnp.clip(j, 0, nk - 1) * t, t)
        s = _dot_nt(q, k_ref[pl.ds(off, t), :]) + bias_ref[o + n]
        logits.append(jnp.where((j >= 0) & (j < nk), s, NEG))
        offs.append(off)
    m = logits[0].max(-1, keepdims=True)
    for s in logits[1:]:
        m = jnp.maximum(m, s.max(-1, keepdims=True))
    l = jnp.zeros((t, 1), F32)
    acc = jnp.zeros((t, LANES), F32)
    for s, off in zip(logits, offs):
        p = jnp.exp(s - m)
        l = l + jnp.sum(p, -1, keepdims=True)
        acc = acc + _dot(p.astype(BF), v_ref[pl.ds(off, t), :])
    o_ref[...] = (acc / l).astype(BF)
    lse_ref[...] = jnp.broadcast_to(m + jnp.log(l), (t, LANES))


def _band_attention(qkv, bias, g, t):
    b, s, _ = qkv.shape
    nh = len(B_PATTERNS) * B_HEADS_PER_GROUP
    n = _band_tiles(g, t)
    spec = lambda base: pl.BlockSpec((None, s, LANES), lambda bi, h, i: (bi, 0, base + g * B_HEADS_PER_GROUP + h))
    out_spec = pl.BlockSpec((None, t, LANES), lambda bi, h, i: (bi, i, h))
    o, lse = pl.pallas_call(
        functools.partial(_band_kernel, n=n, nk=s // t),
        grid=(b, B_HEADS_PER_GROUP, s // t),
        in_specs=[pl.BlockSpec((None, t, LANES), lambda bi, h, i: (bi, i, g * B_HEADS_PER_GROUP + h)),
                  spec(nh), spec(2 * nh),
                  pl.BlockSpec((None, 2 * n + 1, t, t), lambda bi, h, i: (h, 0, 0, 0))],
        out_specs=[out_spec, out_spec],
        out_shape=[jax.ShapeDtypeStruct((b, s, _B_GW), BF), jax.ShapeDtypeStruct((b, s, _B_GW), F32)],
        compiler_params=_params("parallel", "parallel", "arbitrary"),
        name="band_%d" % g,
    )(qkv, qkv, qkv, bias)
    return o.reshape(b * s, _B_GW), lse.reshape(b * s, _B_GW)


def _prep_c(lam_re, lam_im, log_dt, b_re, b_im, c_re, c_im):
    hp = lax.Precision.HIGHEST
    L, P, C = C_CHUNK, C_STATE, C_GROUP
    lr, li = lam_re.astype(F32), lam_im.astype(F32)
    dt = jnp.exp(log_dt.astype(F32))[..., None]
    mag = jnp.exp(lr * dt)
    ar, ai = mag * jnp.cos(li * dt), mag * jnp.sin(li * dt)
    den = lr * lr + li * li
    zr = ((ar - 1.0) * lr + ai * li) / den
    zi = (ai * lr - (ar - 1.0) * li) / den
    br, bi = b_re.astype(F32), b_im.astype(F32)
    bbr = zr[..., None] * br - zi[..., None] * bi
    bbi = zr[..., None] * bi + zi[..., None] * br
    cr, ci = c_re.astype(F32), c_im.astype(F32)
    prs, pis = [jnp.ones_like(ar)], [jnp.zeros_like(ai)]
    for _ in range(L):
        pr_, pi_ = prs[-1], pis[-1]
        prs.append(ar * pr_ - ai * pi_)
        pis.append(ar * pi_ + ai * pr_)
    pr, pi = jnp.stack(prs, 0), jnp.stack(pis, 0)

    def lag(pr_k, pi_k):
        tr = pr_k[..., None] * bbr - pi_k[..., None] * bbi
        ti = pr_k[..., None] * bbi + pi_k[..., None] * bbr
        return (jnp.einsum('dgop,kdgpi->kdgoi', cr, tr, precision=hp)
                - jnp.einsum('dgop,kdgpi->kdgoi', ci, ti, precision=hp))

    kern = lag(pr[:L], pi[:L])
    jj = np.arange(L)[:, None]
    ii = np.arange(L)[None, :]
    kf = kern[:, 0][np.clip(ii - jj, 0, L - 1)] * jnp.asarray((ii >= jj)[:, :, None, None, None], F32)
    kb = kern[:, 1][np.clip(jj - ii, 0, L - 1)] * jnp.asarray((jj >= ii)[:, :, None, None, None], F32)
    m_intra = jnp.transpose(kf + kb, (2, 0, 4, 1, 3)).reshape(C_N_GROUPS, L * C, L * C)

    def summ(d, powers):
        pr_k, pi_k = pr[powers, d], pi[powers, d]
        sr = pr_k[..., None] * bbr[d] - pi_k[..., None] * bbi[d]
        si = pr_k[..., None] * bbi[d] + pi_k[..., None] * bbr[d]
        s = jnp.concatenate([sr, si], 2)
        return jnp.transpose(s, (1, 0, 3, 2)).reshape(C_N_GROUPS, L * C, 2 * P)

    w_sum = jnp.concatenate([summ(0, np.arange(L - 1, -1, -1)), summ(1, np.arange(L))], -1)

    def outw(d, powers):
        pr_k, pi_k = pr[powers, d], pi[powers, d]
        wr = cr[d][None] * pr_k[:, :, None, :] - ci[d][None] * pi_k[:, :, None, :]
        wi = -(cr[d][None] * pi_k[:, :, None, :] + ci[d][None] * pr_k[:, :, None, :])
        w = jnp.concatenate([wr, wi], -1)
        return jnp.transpose(w, (1, 3, 0, 2)).reshape(C_N_GROUPS, 2 * P, L * C)

    w_state = jnp.concatenate([outw(0, np.arange(1, L + 1)), outw(1, np.arange(L, 0, -1))], 1)

    nq = C_N_GROUPS // _C_GL
    npair = _C_GL // 2
    blk = L * _C_GL * C
    lane = jnp.arange(LANES)
    gsel = (lane[None, :] // C == jnp.arange(_C_GL)[:, None])
    src = jnp.tile(m_intra.astype(BF).reshape(nq, _C_GL, L, C, L, C), (1, 1, 1, 1, 1, _C_GL))
    src = jnp.where(gsel[None, :, None, None, None, :], src, 0)
    w_intra = jnp.transpose(src, (0, 2, 1, 3, 4, 5)).reshape(nq, blk, blk)
    psel = (lane[None, :] // P == jnp.arange(2)[:, None])
    ksel = jnp.eye(npair, dtype=bool)
    src = jnp.tile(w_sum.astype(BF).reshape(nq, npair, 2, L, C, 1, 4, P), (1, 1, 1, 1, 1, npair, 1, 2))
    src = jnp.where(psel[None, None, :, None, None, None, None, :]
                    & ksel[None, :, None, None, None, :, None, None], src, 0)
    w_sum = jnp.transpose(src, (0, 3, 1, 2, 4, 5, 6, 7)).reshape(nq, blk, _C_GL * 4 * P)
    osel = (lane[None, None, :] // C
            == (2 * jnp.arange(npair)[:, None, None] + jnp.arange(2)[None, :, None]))
    src = jnp.tile(w_state.astype(BF).reshape(nq, npair, 2, 4, P, L, C), (1, 1, 1, 1, 1, 1, _C_GL))
    src = jnp.where(osel[None, :, :, None, None, None, :], src, 0)
    w_state = jnp.transpose(src, (0, 1, 3, 2, 4, 5, 6)).reshape(nq, _C_GL * 4 * P, blk)
    dec = jnp.stack([pr[L, 0], pi[L, 0], pr[L, 1], pi[L, 1]], 0).reshape(4, nq * npair, 2 * P)
    dec = jnp.broadcast_to(jnp.transpose(dec, (1, 0, 2))[:, :, None, :], (nq * npair, 4, 8, 2 * P))
    return w_sum.astype(BF), w_intra.astype(BF), w_state.astype(BF), dec


def _chunk_rows(x_ref, nb, rc):
    L = C_CHUNK
    rows = [jnp.concatenate([x_ref[b, pl.ds(j, rc, stride=L), :].astype(BF) for j in range(L)], axis=1)
            for b in range(nb)]
    return jnp.concatenate(rows, axis=0)


def _s5_sum_kernel(x_ref, w_ref, s_ref, *, nb, rc):
    s = _dot(_chunk_rows(x_ref, nb, rc), w_ref[...])
    for b in range(nb):
        for t in range(s_ref.shape[0]):
            s_ref[t, pl.ds(b, rc, stride=nb), :] = s[b * rc:(b + 1) * rc, LANES * t:LANES * (t + 1)]


def _s5_scan_kernel(s_ref, dec_ref, e_ref, *, nc, nb):
    units = s_ref.shape[0] // 4
    dec = [[dec_ref[u, k, 0:nb, :] for k in range(4)] for u in range(units)]

    def body(t, carry):
        rf = pl.ds(pl.multiple_of(t * nb, nb), nb)
        rb = pl.ds(pl.multiple_of((nc - 1 - t) * nb, nb), nb)
        new = []
        for u in range(units):
            fr, fi, br, bi = carry[4 * u:4 * u + 4]
            arf, aif, arb, aib = dec[u]
            e_ref[4 * u, rf, :] = fr
            e_ref[4 * u + 1, rf, :] = fi
            e_ref[4 * u + 2, rb, :] = br
            e_ref[4 * u + 3, rb, :] = bi
            new += [fr * arf - fi * aif + s_ref[4 * u, rf, :], fi * arf + fr * aif + s_ref[4 * u + 1, rf, :],
                    br * arb - bi * aib + s_ref[4 * u + 2, rb, :], bi * arb + br * aib + s_ref[4 * u + 3, rb, :]]
        return tuple(new)

    z = jnp.zeros((nb, LANES), F32)
    lax.fori_loop(0, nc, body, (z,) * (4 * units))


def _s5_out_kernel(x_ref, e_ref, wi_ref, wc_ref, y_ref, *, nb, rc):
    L = C_CHUNK
    half = pl.program_id(2)
    xc = _chunk_rows(x_ref, nb, rc)
    ec = jnp.concatenate(
        [jnp.concatenate([e_ref[t, pl.ds(b, rc, stride=nb), :].astype(BF) for t in range(e_ref.shape[0])], axis=1)
         for b in range(nb)], axis=0)
    y = _dot(xc, wi_ref[...]) + _dot(ec, wc_ref[...])
    for b in range(nb):
        for ii in range(L // 2):
            y_ref[b, pl.ds(half * (L // 2) + ii, rc, stride=L), :] = y[b * rc:(b + 1) * rc, LANES * ii:LANES * (ii + 1)]


def _s5(x, b, s, w_sum, w_intra, w_state, dec):
    L = C_CHUNK
    nc = s // L
    nq = D_MODEL // LANES
    nsl = w_sum.shape[2] // LANES
    blk = L * LANES
    rc = _tile(nc, max(8, 512 // b))
    x3 = x.reshape(b, s, D_MODEL)
    x_spec = pl.BlockSpec((b, rc * L, LANES), lambda q, c, *_: (0, c, q))
    sums = pl.pallas_call(
        functools.partial(_s5_sum_kernel, nb=b, rc=rc),
        grid=(nq, nc // rc),
        in_specs=[x_spec, pl.BlockSpec((None, blk, nsl * LANES), lambda q, c: (q, 0, 0))],
        out_specs=pl.BlockSpec((nsl, rc * b, LANES), lambda q, c: (q, c, 0)),
        out_shape=jax.ShapeDtypeStruct((nq * nsl, nc * b, LANES), F32),
        compiler_params=_params("parallel", "arbitrary"),
        name="s5_sum",
    )(x3, w_sum)
    upb = 2
    ent = pl.pallas_call(
        functools.partial(_s5_scan_kernel, nc=nc, nb=b),
        grid=(nq * nsl // (4 * upb),),
        in_specs=[pl.BlockSpec((4 * upb, nc * b, LANES), lambda i: (i, 0, 0)),
                  pl.BlockSpec((upb, 4, 8, LANES), lambda i: (i, 0, 0, 0))],
        out_specs=pl.BlockSpec((4 * upb, nc * b, LANES), lambda i: (i, 0, 0)),
        out_shape=jax.ShapeDtypeStruct((nq * nsl, nc * b, LANES), F32),
        compiler_params=_params("parallel"),
        name="s5_scan",
    )(sums, dec)
    y = pl.pallas_call(
        functools.partial(_s5_out_kernel, nb=b, rc=rc),
        grid=(nq, nc // rc, 2),
        in_specs=[x_spec,
                  pl.BlockSpec((nsl, rc * b, LANES), lambda q, c, h: (q, c, 0)),
                  pl.BlockSpec((None, blk, blk // 2), lambda q, c, h: (q, 0, h)),
                  pl.BlockSpec((None, nsl * LANES, blk // 2), lambda q, c, h: (q, 0, h))],
        out_specs=pl.BlockSpec((b, rc * L, LANES), lambda q, c, h: (0, c, q)),
        out_shape=jax.ShapeDtypeStruct((b, s, D_MODEL), F32),
        compiler_params=_params("parallel", "arbitrary", "arbitrary"),
        name="s5_out",
    )(x3, ent, w_intra, w_state)
    return y.reshape(b * s, D_MODEL)


_D_T = 512


def _prep_d(w_qkv, norm_gain, w_o, lambda_init):
    qk_w = D_HEADS * 2 * D_QK_DIM
    wq = _pad_heads(w_qkv[:, :qk_w] * (D_QK_DIM ** -0.5 * LOG2E), D_HEADS, 2 * D_QK_DIM)
    wk = _pad_heads(w_qkv[:, qk_w:2 * qk_w], D_HEADS, 2 * D_QK_DIM)
    wv = _pad_heads(w_qkv[:, 2 * qk_w:], D_HEADS, D_V_DIM)
    w = jnp.concatenate([wq, wk, wv], 1).astype(BF)
    gain = jnp.concatenate([norm_gain * (1.0 - lambda_init), jnp.zeros((LANES - D_V_DIM,), F32)])[None, :]
    wo = w_o.reshape(D_HEADS, D_V_DIM, D_MODEL)
    wo = jnp.concatenate([wo, jnp.zeros_like(wo)], 1).reshape(D_HEADS * LANES, D_MODEL).astype(BF)
    return w, gain, wo


def _diff_bias(rel_bias, t):
    rel0 = (jnp.arange(2 * t) + t) % (2 * t) - t
    rel = jnp.arange(-2, 3)[:, None] * t + rel0[None, :]
    vec = jnp.transpose(rel_bias[_bucket(rel)] * LOG2E, (2, 0, 1)).astype(F32)
    return _toeplitz(vec, t)


def _flash_d_kernel(q_ref, k_ref, v_ref, bias_ref, lam_ref, gain_ref, o_ref, m_ref, acc_ref, *, kt, nk, lambda_init):
    t = q_ref.shape[0]
    i = pl.program_id(2)
    qv = q_ref[...].astype(F32)
    lane = lax.broadcasted_iota(I32, (t, LANES), 1)
    q0 = jnp.where(lane < D_QK_DIM, qv, 0.0).astype(BF)
    q1 = jnp.where((lane >= D_QK_DIM) & (lane < 2 * D_QK_DIM), qv, 0.0).astype(BF)
    q = jnp.concatenate([q0, q1], axis=0)
    _softmax_init(m_ref, acc_ref)

    def body(j, carry):
        off = pl.multiple_of(j * (kt * t), kt * t)
        bias = jnp.concatenate([bias_ref[jnp.clip(j * kt + c - i, -2, 2) + 2] for c in range(kt)], axis=1)
        s = _dot_nt(q, k_ref[pl.ds(off, kt * t), :])
        s = (s.reshape(2, t, kt * t) + bias[None]).reshape(2 * t, kt * t)
        _softmax_step(s, v_ref[pl.ds(off, kt * t), :], m_ref, acc_ref)
        return carry

    lax.fori_loop(0, nk, body, 0, unroll=2)
    lf = lam_ref[...]
    lam = (jnp.exp(jnp.sum(lf[0:1] * lf[1:2], keepdims=True))
           - jnp.exp(jnp.sum(lf[2:3] * lf[3:4], keepdims=True)) + lambda_init)
    acc = acc_ref[...]
    on = acc / acc[:, _ONE_LANE:_ONE_LANE + 1]
    o = jnp.where(lane < D_V_DIM, on[:t] - lam * on[t:], 0.0)
    ms = jnp.sum(o * o, -1, keepdims=True) * (1.0 / D_V_DIM)
    o_ref[...] = (o * lax.rsqrt(ms + RMS_EPS) * gain_ref[...]).astype(BF)


def _flash_d(qkv, bias, lam, gain, lambda_init):
    b, s, _ = qkv.shape
    t = bias.shape[-1]
    kt = 2 if s % (2 * t) == 0 else 1
    return pl.pallas_call(
        functools.partial(_flash_d_kernel, kt=kt, nk=s // (kt * t), lambda_init=lambda_init),
        grid=(b, D_HEADS, s // t),
        in_specs=[pl.BlockSpec((None, t, LANES), lambda bi, h, i: (bi, i, h)),
                  pl.BlockSpec((None, s, LANES), lambda bi, h, i: (bi, 0, D_HEADS + h)),
                  pl.BlockSpec((None, s, LANES), lambda bi, h, i: (bi, 0, 2 * D_HEADS + h)),
                  pl.BlockSpec((None, 5, t, t), lambda bi, h, i: (h, 0, 0, 0)),
                  pl.BlockSpec((4, D_QK_DIM), lambda bi, h, i: (0, 0)),
                  pl.BlockSpec((1, LANES), lambda bi, h, i: (0, 0))],
        out_specs=pl.BlockSpec((None, t, LANES), lambda bi, h, i: (bi, i, h)),
        out_shape=jax.ShapeDtypeStruct((b, s, D_HEADS * LANES), BF),
        scratch_shapes=[pltpu.VMEM((2 * t, LANES), F32)] * 2,
        compiler_params=_params("parallel", "parallel", "arbitrary"),
        name="flash_d",
    )(qkv, qkv, qkv, bias, lam, gain)


def _cross_kernel(x_ref, kv_ref, wq_ref, wo_ref, g_ref, b_ref, wr_ref, y_ref, ybf_ref, aff_ref):
    x = x_ref[...]
    tm = x.shape[0]
    q = (_dot(x.astype(BF), wq_ref[...]) * (X_HEAD_DIM ** -0.5)).astype(BF)
    outs = []
    for h in range(X_HEADS):
        sl = slice(X_HEAD_DIM * h, X_HEAD_DIM * (h + 1))
        s = _dot_nt(q[:, sl], kv_ref[:, sl])
        p = jnp.exp(s - jnp.max(s, -1, keepdims=True))
        l = jnp.sum(p, -1, keepdims=True)
        vh = kv_ref[:, D_MODEL + X_HEAD_DIM * h:D_MODEL + X_HEAD_DIM * (h + 1)]
        outs.append((_dot(p.astype(BF), vh) / l).astype(BF))
    o = jnp.concatenate(outs, axis=1)
    y = _ln_rows(ALPHA * x + _dot(o, wo_ref[...]), g_ref[...], b_ref[...])
    y_ref[...] = y
    yh = y.astype(BF)
    ybf_ref[...] = yh
    yl = (y - yh.astype(F32)).astype(BF)
    wr = wr_ref[...]
    wh = wr.astype(BF)
    wl = (wr - wh.astype(F32)).astype(BF)
    lg = _dot_nt(wh, yh) + _dot_nt(wh, yl) + _dot_nt(wl, yh)
    e = jnp.exp(lg - jnp.max(lg, 0, keepdims=True))
    aff = e / jnp.sum(e, 0, keepdims=True)
    for c in range(tm // LANES):
        aff_ref[c] = aff[:, LANES * c:LANES * (c + 1)]


def _cross(x, kv, wq, wo, g, b, wr_t, s, mem_len, tm=512):
    n = x.shape[0]
    tm = _tile(s, tm)
    per = s // tm
    fixed = lambda i: (0, 0)
    return pl.pallas_call(
        _cross_kernel,
        grid=(n // tm,),
        in_specs=[pl.BlockSpec((tm, D_MODEL), lambda i: (i, 0)),
                  pl.BlockSpec((mem_len, 2 * D_MODEL), lambda i: (i // per, 0)),
                  pl.BlockSpec((D_MODEL, D_MODEL), fixed), pl.BlockSpec((D_MODEL, D_MODEL), fixed),
                  pl.BlockSpec((1, D_MODEL), fixed), pl.BlockSpec((1, D_MODEL), fixed),
                  pl.BlockSpec((N_EXPERTS, D_MODEL), fixed)],
        out_specs=[pl.BlockSpec((tm, D_MODEL), lambda i: (i, 0)),
                   pl.BlockSpec((tm, D_MODEL), lambda i: (i, 0)),
                   pl.BlockSpec((tm // LANES, N_EXPERTS, LANES), lambda i: (i, 0, 0))],
        out_shape=[jax.ShapeDtypeStruct((n, D_MODEL), F32),
                   jax.ShapeDtypeStruct((n, D_MODEL), BF),
                   jax.ShapeDtypeStruct((n // LANES, N_EXPERTS, LANES), F32)],
        compiler_params=_params("parallel"),
        name="cross",
    )(x, kv, wq, wo, g, b, wr_t)


def _select_kernel(a_ref, pos_ref, st_ref, *, k, nbits):
    nt = a_ref.shape[0]
    shape = (nt, N_EXPERTS, LANES)
    kf = float(k)

    def keys():
        return lax.bitcast_convert_type(a_ref[...], I32)

    def count(mask):
        c = jnp.sum(jnp.where(mask, 1.0, 0.0), axis=0, keepdims=True)
        return jnp.sum(c, axis=2, keepdims=True)

    def value_step(it, thr):
        cand = thr | jnp.left_shift(jnp.int32(1), 30 - it)
        return jnp.where(count(keys() >= cand) >= kf, cand, thr)

    thr = lax.fori_loop(0, 31, value_step, jnp.zeros((1, N_EXPERTS, 1), I32))
    need = kf - count(keys() > thr)
    idx = lax.broadcasted_iota(I32, shape, 0) * LANES + lax.broadcasted_iota(I32, shape, 2)

    def index_step(it, ithr):
        cand = ithr | jnp.left_shift(jnp.int32(1), nbits - 1 - it)
        return jnp.where(count((keys() == thr) & (idx < cand)) < need, cand, ithr)

    ithr = lax.fori_loop(0, nbits, index_step, jnp.zeros((1, N_EXPERTS, 1), I32))
    thr2, ithr2 = thr[0], ithr[0]
    upper = jnp.where(lax.broadcasted_iota(I32, (LANES, LANES), 0) <= lax.broadcasted_iota(I32, (LANES, LANES), 1),
                      1.0, 0.0).astype(BF)
    lane = lax.broadcasted_iota(I32, (N_EXPERTS, LANES), 1)

    def tile_step(j, carry):
        kj = lax.bitcast_convert_type(a_ref[j], I32)
        sel = (kj > thr2) | ((kj == thr2) & (j * LANES + lane <= ithr2))
        m = jnp.where(sel, 1.0, 0.0)
        inc = _dot(m.astype(BF), upper)
        pos_ref[j] = jnp.where(sel, inc - m + carry, -1.0).astype(I32)
        st_ref[j] = jnp.broadcast_to(carry, (N_EXPERTS, LANES)).astype(I32)
        return carry + inc[:, LANES - 1:LANES]

    lax.fori_loop(0, nt, tile_step, jnp.zeros((N_EXPERTS, 1), F32))


def _select(aff3, k):
    nt = aff3.shape[0]
    nbits = max(1, int(math.ceil(math.log2(nt * LANES))))
    shp = jax.ShapeDtypeStruct(aff3.shape, I32)
    return pl.pallas_call(
        functools.partial(_select_kernel, k=k, nbits=nbits),
        out_shape=[shp, shp],
        compiler_params=pltpu.CompilerParams(vmem_limit_bytes=VMEM_LIMIT),
        name="select",
    )(aff3)


_GATHER_ROWS = LANES + BF16_ROWS


def _moe_ffn_kernel(st_ref, x_ref, pos_ref, aff_ref, wg_ref, wu_ref, wd_ref, ye_ref, buf_ref, gate_ref, *, nb, ncf,
                    sub, ck):
    e = pl.program_id(0)
    t = pl.program_id(1)

    @pl.when(t == 0)
    def _():
        buf_ref[...] = jnp.zeros_like(buf_ref)
        gate_ref[...] = jnp.zeros_like(gate_ref)

    @pl.when(t < nb)
    def _():
        row = lax.broadcasted_iota(I32, (_GATHER_ROWS, LANES), 0)
        for s in range(sub):
            st = st_ref[e, t * sub + s]
            base = pl.multiple_of((st // BF16_ROWS) * BF16_ROWS, BF16_ROWS)
            hit = row == pos_ref[s, pl.ds(e, 1), :] - base
            c = _dot(jnp.where(hit, 1.0, 0.0).astype(BF), x_ref[LANES * s:LANES * (s + 1), :])
            buf_ref[pl.ds(base, _GATHER_ROWS), :] = buf_ref[pl.ds(base, _GATHER_ROWS), :] + c.astype(BF)
            g = jnp.sum(jnp.where(hit, aff_ref[s, pl.ds(e, 1), :], 0.0), -1, keepdims=True)
            gate_ref[pl.ds(base, _GATHER_ROWS), :] = gate_ref[pl.ds(base, _GATHER_ROWS), :] + g

    @pl.when(t >= nb)
    def _():
        c = t - nb

        @pl.when(c < ncf)
        def _():
            rows = pl.ds(pl.multiple_of(c * ck, ck), ck)
            xe = buf_ref[rows, :]
            h = (jax.nn.silu(_dot(xe, wg_ref[...])) * _dot(xe, wu_ref[...])).astype(BF)
            ye_ref[...] = (_dot(h, wd_ref[...]) * gate_ref[rows, :]).astype(BF)

        @pl.when(c >= ncf)
        def _():
            ye_ref[...] = jnp.zeros_like(ye_ref)


def _moe_ffn(starts, xbf, pos3, aff3, wg, wu, wd, cap, tb=1024):
    n = xbf.shape[0]
    tb = _tile(n, tb)
    nb = n // tb
    sub = tb // LANES
    ck = _tile(cap, 512)
    ncf = cap // ck
    nch = ncf + pl.cdiv(_WIN_STEP + _GATHER_ROWS, ck)
    dff = wg.shape[-1]
    blk = lambda e, t, st: (jnp.minimum(t, nb - 1), 0)
    return pl.pallas_call(
        functools.partial(_moe_ffn_kernel, nb=nb, ncf=ncf, sub=sub, ck=ck),
        grid_spec=pltpu.PrefetchScalarGridSpec(
            num_scalar_prefetch=1,
            grid=(N_EXPERTS, nb + nch),
            in_specs=[pl.BlockSpec((tb, D_MODEL), blk),
                      pl.BlockSpec((sub, N_EXPERTS, LANES), lambda e, t, st: (jnp.minimum(t, nb - 1), 0, 0)),
                      pl.BlockSpec((sub, N_EXPERTS, LANES), lambda e, t, st: (jnp.minimum(t, nb - 1), 0, 0)),
                      pl.BlockSpec((None, D_MODEL, dff), lambda e, t, st: (e, 0, 0)),
                      pl.BlockSpec((None, D_MODEL, dff), lambda e, t, st: (e, 0, 0)),
                      pl.BlockSpec((None, dff, D_MODEL), lambda e, t, st: (e, 0, 0))],
            out_specs=pl.BlockSpec((None, ck, D_MODEL), lambda e, t, st: (e, jnp.maximum(t - nb, 0), 0)),
            scratch_shapes=[pltpu.VMEM((cap + _GATHER_ROWS, D_MODEL), BF),
                            pltpu.VMEM((cap + _GATHER_ROWS, 1), F32)]),
        out_shape=jax.ShapeDtypeStruct((N_EXPERTS, nch * ck, D_MODEL), BF),
        compiler_params=_params("arbitrary", "arbitrary"),
        name="moe_ffn",
    )(starts, xbf, pos3, aff3, wg, wu, wd)


_WIN_STEP = 256
_WIN_ROWS = _WIN_STEP + _GATHER_ROWS


def _combine_kernel(st_ref, x_ref, pos_ref, g_ref, b_ref, *rest):
    ye_refs, y_ref = rest[:N_EXPERTS], rest[N_EXPERTS]
    j = pl.program_id(0)
    col = lax.broadcasted_iota(I32, (LANES, _GATHER_ROWS), 1)
    acc = None
    for e in range(N_EXPERTS):
        st = st_ref[e, j]
        base = (st // BF16_ROWS) * BF16_ROWS
        sub = pl.multiple_of(base - (st // _WIN_STEP) * _WIN_STEP, BF16_ROWS)
        onehot = jnp.where(col == pos_ref[:, e:e + 1] - base, 1.0, 0.0).astype(BF)
        d = _dot(onehot, ye_refs[e][0, pl.ds(sub, _GATHER_ROWS), :])
        acc = d if acc is None else acc + d
    y_ref[...] = _ln_rows(ALPHA * x_ref[...] + acc, g_ref[...], b_ref[...])


def _combine(starts, x, pos, g, b, ye):
    n = x.shape[0]
    assert ye.shape[1] >= EC_CAPACITY * n // N_EXPERTS + _WIN_ROWS
    row = lambda j, st: (j, 0)
    fixed = lambda j, st: (0, 0)

    def ye_spec(e):
        return pl.BlockSpec((pl.Element(1), pl.Element(_WIN_ROWS), pl.Element(D_MODEL)),
                            lambda j, st: (e, (st[e, j] // _WIN_STEP) * _WIN_STEP, 0))

    return pl.pallas_call(
        _combine_kernel,
        grid_spec=pltpu.PrefetchScalarGridSpec(
            num_scalar_prefetch=1,
            grid=(n // LANES,),
            in_specs=[pl.BlockSpec((LANES, D_MODEL), row),
                      pl.BlockSpec((LANES, N_EXPERTS), row),
                      pl.BlockSpec((1, D_MODEL), fixed), pl.BlockSpec((1, D_MODEL), fixed)]
            + [ye_spec(e) for e in range(N_EXPERTS)],
            out_specs=pl.BlockSpec((LANES, D_MODEL), row)),
        out_shape=jax.ShapeDtypeStruct((n, D_MODEL), F32),
        compiler_params=_params("arbitrary"),
        name="combine",
    )(starts, x, pos, g, b, *([ye] * N_EXPERTS))


def _moe(x, xg, aff3, wg, wu, wd, g, b):
    n = x.shape[0]
    cap = EC_CAPACITY * n // N_EXPERTS
    pos3, st3 = _select(aff3, cap)
    starts = jnp.transpose(st3[:, :, 0])
    ye = _moe_ffn(starts, xg, pos3, aff3, wg, wu, wd, cap)
    pos = jnp.transpose(pos3, (0, 2, 1)).reshape(n, N_EXPERTS)
    return _combine(starts, x, pos, g, b, ye)


def _prep_weights(p):
    w = {}
    w['a'] = [_prep_a(p['a_w_qkv'][j], p['a_q_gain'][j], p['a_k_gain'][j], p['a_w_o'][j])
              for j in range(p['a_w_qkv'].shape[0])]
    w['b'] = [_prep_b(p['b_w_qkv'][j], p['b_w_o'][j]) for j in range(p['b_w_qkv'].shape[0])]
    w['c'] = [_prep_c(p['c_lam_re'][j], p['c_lam_im'][j], p['c_log_dt'][j], p['c_b_re'][j], p['c_b_im'][j],
                      p['c_c_re'][j], p['c_c_im'][j]) + (p['c_d'][j][None, :], p['c_w_glu'][j].astype(BF))
              for j in range(p['c_lam_re'].shape[0])]
    w['d'] = []
    for j in range(p['d_w_qkv'].shape[0]):
        layer = N_MIXERS * j + 3
        lambda_init = 0.8 - 0.6 * math.exp(-0.3 * layer)
        w['d'].append(_prep_d(p['d_w_qkv'][j], p['d_norm_gain'][j], p['d_w_o'][j], lambda_init)
                      + (p['d_lam'][j].astype(F32), lambda_init))
    w['diff_bias'] = _diff_bias(p['rel_bias'], _D_T)
    w['x_w_q'] = p['x_w_q'].astype(BF)
    w['x_w_kv'] = p['x_w_kv'].astype(BF)
    w['x_w_o'] = p['x_w_o'].astype(BF)
    w['router_t'] = jnp.transpose(p['moe_w_router'], (0, 2, 1)).astype(F32)
    w['moe_w_gate'] = p['moe_w_gate'].astype(BF)
    w['moe_w_up'] = p['moe_w_up'].astype(BF)
    w['moe_w_down'] = p['moe_w_down'].astype(BF)
    return w


def _trunk(x, mem, p, w):
    b, s, _ = x.shape
    n = b * s
    mem_len = mem.shape[1]
    x = x.reshape(n, D_MODEL)
    mem2 = mem.reshape(b * mem_len, D_MODEL)
    ln_g, ln_b = p['ln_g'], p['ln_b']
    for i in range(DEPTH):
        m, j = i % N_MIXERS, i // N_MIXERS
        g0, b0 = ln_g[i, 0][None, :], ln_b[i, 0][None, :]
        if m == 0:
            wa, gains, wo = w['a'][j]
            qkv = _proj_a(x, wa, _rope_table(s), gains, s)
            o = _flash_a(qkv.reshape(b, s, _A_COLS))
            x = _post(x, o.reshape(n, A_HEADS * LANES), wo, g0, b0)
        elif m == 1:
            wb, wo = w['b'][j]
            qkv = _proj(x, wb).reshape(b, s, _B_TILES * LANES)
            os_, ls_ = [], []
            for g in range(len(B_PATTERNS)):
                t = _tile(s, _B_T)
                o, lse = _band_attention(qkv, _band_bias(p['rel_bias'], g, t), g, t)
                os_.append(o)
                ls_.append(lse)
            x = _post_b(x, os_, ls_, wo, g0, b0)
        elif m == 2:
            w_sum, w_intra, w_state, dec, dskip, wglu = w['c'][j]
            ys = _s5(x, b, s, w_sum, w_intra, w_state, dec)
            x = _post_c(x, ys, dskip, wglu, g0, b0)
        else:
            wd, gain, wo, lam, lambda_init = w['d'][j]
            qkv = _proj(x, wd, ones_from=2 * D_HEADS * LANES).reshape(b, s, 3 * D_HEADS * LANES)
            o = _flash_d(qkv, w['diff_bias'], lam, gain, lambda_init)
            x = _post(x, o.reshape(n, D_HEADS * LANES), wo, g0, b0)
        kv = _proj(mem2, w['x_w_kv'][i], tm=mem_len)
        x, xbf, aff3 = _cross(x, kv, w['x_w_q'][i], w['x_w_o'][i], ln_g[i, 1][None, :], ln_b[i, 1][None, :],
                              w['router_t'][i], s, mem_len)
        x = _moe(x, xbf, aff3, w['moe_w_gate'][i], w['moe_w_up'][i], w['moe_w_down'][i],
                 ln_g[i, 2][None, :], ln_b[i, 2][None, :])
    return x.reshape(b, s, D_MODEL)


def kernel(x_prompt, x_sample, mem_prompt, mem_sample, rel_bias, ln_g, ln_b, a_w_qkv, a_q_gain, a_k_gain, a_w_o, b_w_qkv, b_w_o, c_lam_re, c_lam_im, c_log_dt, c_b_re, c_b_im, c_c_re, c_c_im, c_d, c_w_glu, d_w_qkv, d_lam, d_norm_gain, d_w_o, x_w_q, x_w_kv, x_w_o, moe_w_router, moe_w_gate, moe_w_up, moe_w_down):
    p = dict(rel_bias=rel_bias, ln_g=ln_g, ln_b=ln_b,
             a_w_qkv=a_w_qkv, a_q_gain=a_q_gain, a_k_gain=a_k_gain, a_w_o=a_w_o,
             b_w_qkv=b_w_qkv, b_w_o=b_w_o,
             c_lam_re=c_lam_re, c_lam_im=c_lam_im, c_log_dt=c_log_dt, c_b_re=c_b_re, c_b_im=c_b_im,
             c_c_re=c_c_re, c_c_im=c_c_im, c_d=c_d, c_w_glu=c_w_glu,
             d_w_qkv=d_w_qkv, d_lam=d_lam, d_norm_gain=d_norm_gain, d_w_o=d_w_o,
             x_w_q=x_w_q, x_w_kv=x_w_kv, x_w_o=x_w_o,
             moe_w_router=moe_w_router, moe_w_gate=moe_w_gate, moe_w_up=moe_w_up, moe_w_down=moe_w_down)
    w = _prep_weights(p)
    return (_trunk(x_prompt, mem_prompt, p, w), _trunk(x_sample, mem_sample, p, w))
```

```python
import functools
import math

import numpy as np
import jax
import jax.numpy as jnp
from jax import lax
from jax.experimental import pallas as pl
from jax.experimental.pallas import tpu as pltpu

F32 = jnp.float32
BF = jnp.bfloat16
I32 = jnp.int32

D_MODEL = 1024
DEPTH = 4
GRID_W = 64
N_MIXERS = 4
LN_EPS = 1e-5
RMS_EPS = 1e-6
ALPHA = (2.0 * DEPTH) ** 0.25

A_HEADS = 16
A_KV_HEADS = 4
A_HEAD_DIM = 64
ROPE_BASE = 10000.0

B_PATTERNS = ((128, 1), (512, 4), (2048, 16))
B_HEADS_PER_GROUP = 4
B_HEAD_DIM = 64

C_GROUP = 16
C_N_GROUPS = D_MODEL // C_GROUP
C_STATE = 64
C_CHUNK = 16
_C_GL = 128 // C_GROUP

D_HEADS = 12
D_QK_DIM = 32
D_V_DIM = 64

REL_BUCKETS = 32
REL_MAX_DIST = 128

X_HEADS = 4
X_HEAD_DIM = D_MODEL // X_HEADS

N_EXPERTS = 16
EC_CAPACITY = 2

LANES = 128
BF16_ROWS = 16
VMEM_LIMIT = 56 * 1024 * 1024
NEG = -1e30
LOG2E = math.log2(math.e)


def _params(*sem):
    return pltpu.CompilerParams(dimension_semantics=sem, vmem_limit_bytes=VMEM_LIMIT)


def _tile(n, pref):
    t = min(n, pref)
    assert n % t == 0, (n, pref)
    return t


def _ln_rows(v, g, b):
    mu = jnp.mean(v, -1, keepdims=True)
    c = v - mu
    var = jnp.mean(c * c, -1, keepdims=True)
    return c * lax.rsqrt(var + LN_EPS) * g + b


def _dot_nt(a, b):
    return lax.dot_general(a, b, (((1,), (1,)), ((), ())), preferred_element_type=F32)


def _dot(a, b):
    return jnp.dot(a, b, preferred_element_type=F32)


_ONE_LANE = 64


def _with_ones_lane(y):
    lane = lax.broadcasted_iota(I32, y.shape, 1)
    return jnp.where(lane % LANES == _ONE_LANE, 1.0, y)


def _proj_kernel(x_ref, w_ref, o_ref, *, ones_from):
    y = _dot(x_ref[...].astype(BF), w_ref[...])
    if ones_from is None:
        o_ref[...] = y.astype(o_ref.dtype)
    else:
        o_ref[:, :ones_from] = y[:, :ones_from].astype(o_ref.dtype)
        o_ref[:, ones_from:] = _with_ones_lane(y[:, ones_from:]).astype(o_ref.dtype)


def _proj(x, w, tm=512, ones_from=None):
    n, k = x.shape
    m = w.shape[1]
    tm = _tile(n, tm)
    return pl.pallas_call(
        functools.partial(_proj_kernel, ones_from=ones_from),
        grid=(n // tm,),
        in_specs=[pl.BlockSpec((tm, k), lambda i: (i, 0)),
                  pl.BlockSpec((k, m), lambda i: (0, 0))],
        out_specs=pl.BlockSpec((tm, m), lambda i: (i, 0)),
        out_shape=jax.ShapeDtypeStruct((n, m), BF),
        compiler_params=_params("parallel"),
        name="proj",
    )(x, w)


def _post_kernel(x_ref, o_ref, w_ref, g_ref, b_ref, y_ref):
    h = _dot(o_ref[...], w_ref[...])
    y_ref[...] = _ln_rows(ALPHA * x_ref[...] + h, g_ref[...], b_ref[...])


def _post(x, o, w, g, b, tm=512):
    n = x.shape[0]
    ko = o.shape[1]
    tm = _tile(n, tm)
    return pl.pallas_call(
        _post_kernel,
        grid=(n // tm,),
        in_specs=[pl.BlockSpec((tm, D_MODEL), lambda i: (i, 0)),
                  pl.BlockSpec((tm, ko), lambda i: (i, 0)),
                  pl.BlockSpec((ko, D_MODEL), lambda i: (0, 0)),
                  pl.BlockSpec((1, D_MODEL), lambda i: (0, 0)),
                  pl.BlockSpec((1, D_MODEL), lambda i: (0, 0))],
        out_specs=pl.BlockSpec((tm, D_MODEL), lambda i: (i, 0)),
        out_shape=jax.ShapeDtypeStruct((n, D_MODEL), F32),
        compiler_params=_params("parallel"),
        name="post",
    )(x, o, w, g, b)


def _post_b_kernel(x_ref, o0_ref, o1_ref, o2_ref, l0_ref, l1_ref, l2_ref, w_ref, g_ref, b_ref, y_ref):
    l0, l1, l2 = l0_ref[...], l1_ref[...], l2_ref[...]
    m = jnp.maximum(jnp.maximum(l0, l1), l2)
    e0, e1, e2 = jnp.exp(l0 - m), jnp.exp(l1 - m), jnp.exp(l2 - m)
    inv = 1.0 / (e0 + e1 + e2)
    gw = B_HEADS_PER_GROUP * LANES
    h = _dot((o0_ref[...].astype(F32) * (e0 * inv)).astype(BF), w_ref[0:gw, :])
    h = h + _dot((o1_ref[...].astype(F32) * (e1 * inv)).astype(BF), w_ref[gw:2 * gw, :])
    h = h + _dot((o2_ref[...].astype(F32) * (e2 * inv)).astype(BF), w_ref[2 * gw:3 * gw, :])
    y_ref[...] = _ln_rows(ALPHA * x_ref[...] + h, g_ref[...], b_ref[...])


def _post_b(x, os_, ls_, w, g, b, tm=512):
    n = x.shape[0]
    tm = _tile(n, tm)
    gw = B_HEADS_PER_GROUP * LANES
    row = lambda i: (i, 0)
    fixed = lambda i: (0, 0)
    return pl.pallas_call(
        _post_b_kernel,
        grid=(n // tm,),
        in_specs=[pl.BlockSpec((tm, D_MODEL), row)] + [pl.BlockSpec((tm, gw), row)] * 6
        + [pl.BlockSpec((3 * gw, D_MODEL), fixed), pl.BlockSpec((1, D_MODEL), fixed),
           pl.BlockSpec((1, D_MODEL), fixed)],
        out_specs=pl.BlockSpec((tm, D_MODEL), row),
        out_shape=jax.ShapeDtypeStruct((n, D_MODEL), F32),
        compiler_params=_params("parallel"),
        name="post_b",
    )(x, *os_, *ls_, w, g, b)


def _post_c_kernel(x_ref, ys_ref, d_ref, w_ref, g_ref, b_ref, y_ref):
    x = x_ref[...]
    z = jax.nn.gelu(ys_ref[...] + d_ref[...] * x).astype(BF)
    h = _dot(z, w_ref[...])
    hh = h[:, :D_MODEL] * jax.nn.sigmoid(h[:, D_MODEL:])
    y_ref[...] = _ln_rows(ALPHA * x + hh, g_ref[...], b_ref[...])


def _post_c(x, ys, d, w, g, b, tm=512):
    n = x.shape[0]
    tm = _tile(n, tm)
    row = lambda i: (i, 0)
    fixed = lambda i: (0, 0)
    return pl.pallas_call(
        _post_c_kernel,
        grid=(n // tm,),
        in_specs=[pl.BlockSpec((tm, D_MODEL), row), pl.BlockSpec((tm, D_MODEL), row),
                  pl.BlockSpec((1, D_MODEL), fixed), pl.BlockSpec((D_MODEL, 2 * D_MODEL), fixed),
                  pl.BlockSpec((1, D_MODEL), fixed), pl.BlockSpec((1, D_MODEL), fixed)],
        out_specs=pl.BlockSpec((tm, D_MODEL), row),
        out_shape=jax.ShapeDtypeStruct((n, D_MODEL), F32),
        compiler_params=_params("parallel"),
        name="post_c",
    )(x, ys, d, w, g, b)


_A_QK_TILES = A_HEADS + A_KV_HEADS
_A_COLS = (_A_QK_TILES + A_KV_HEADS) * LANES


def _rope_partner():
    d = np.arange(A_HEAD_DIM)
    e = d % (A_HEAD_DIM // 2)
    lo = e < A_HEAD_DIM // 4
    return np.where(lo, d + A_HEAD_DIM // 4, d - A_HEAD_DIM // 4), np.where(lo, -1.0, 1.0).astype(np.float32)


def _prep_a(w_qkv, q_gain, k_gain, w_o):
    partner, sign = _rope_partner()
    nqk = _A_QK_TILES * A_HEAD_DIM
    wqk = w_qkv[:, :nqk].reshape(D_MODEL, _A_QK_TILES, A_HEAD_DIM)
    wsw = wqk[:, :, partner] * sign
    wqk = jnp.concatenate([wqk, wsw], -1).reshape(D_MODEL, _A_QK_TILES * LANES)
    wv = w_qkv[:, nqk:].reshape(D_MODEL, A_KV_HEADS, A_HEAD_DIM)
    wv = jnp.concatenate([wv, jnp.zeros_like(wv)], -1).reshape(D_MODEL, A_KV_HEADS * LANES)
    w = jnp.concatenate([wqk, wv], 1).astype(BF)
    gq = jnp.concatenate([q_gain, q_gain[partner]]) * (A_HEAD_DIM ** -0.5 * 0.5 * LOG2E)
    gk = jnp.concatenate([k_gain, k_gain[partner]])
    gains = jnp.stack([gq, gk], 0)
    wo = w_o.reshape(A_HEADS, A_HEAD_DIM, D_MODEL)
    wo = jnp.concatenate([wo, jnp.zeros_like(wo)], 1).reshape(A_HEADS * LANES, D_MODEL).astype(BF)
    return w, gains, wo


def _rope_table(s):
    pos = jnp.arange(s)
    rows, cols = (pos // GRID_W).astype(F32), (pos % GRID_W).astype(F32)
    half = A_HEAD_DIM // 2
    freqs = ROPE_BASE ** (-jnp.arange(0, half, 2, dtype=F32) / half)
    ang_r = rows[:, None] * freqs
    ang_c = cols[:, None] * freqs
    ang = jnp.concatenate([ang_r, ang_r, ang_c, ang_c], -1)
    return jnp.concatenate([jnp.cos(ang), jnp.sin(ang)], -1)


def _proj_a_kernel(x_ref, w_ref, cs_ref, g_ref, o_ref):
    y = _dot(x_ref[...].astype(BF), w_ref[...])
    cs = cs_ref[...]
    gq = cs * g_ref[0:1, :]
    gk = cs * g_ref[1:2, :]
    for h in range(_A_QK_TILES):
        t = y[:, LANES * h:LANES * (h + 1)]
        r = lax.rsqrt(jnp.sum(t * t, -1, keepdims=True) * (1.0 / LANES) + RMS_EPS)
        e = t * r * (gq if h < A_HEADS else gk)
        o_ref[:, LANES * h:LANES * (h + 1)] = (e + pltpu.roll(e, LANES // 2, 1)).astype(BF)
    o_ref[:, _A_QK_TILES * LANES:] = _with_ones_lane(y[:, _A_QK_TILES * LANES:]).astype(BF)


def _proj_a(x, w, cs, gains, s, tm=512):
    n = x.shape[0]
    tm = _tile(s, tm)
    per = s // tm
    return pl.pallas_call(
        _proj_a_kernel,
        grid=(n // tm,),
        in_specs=[pl.BlockSpec((tm, D_MODEL), lambda i: (i, 0)),
                  pl.BlockSpec((D_MODEL, _A_COLS), lambda i: (0, 0)),
                  pl.BlockSpec((tm, LANES), lambda i: (i % per, 0)),
                  pl.BlockSpec((2, LANES), lambda i: (0, 0))],
        out_specs=pl.BlockSpec((tm, _A_COLS), lambda i: (i, 0)),
        out_shape=jax.ShapeDtypeStruct((n, _A_COLS), BF),
        compiler_params=_params("parallel"),
        name="proj_a",
    )(x, w, cs, gains)


def _softmax_step(s, v, m_ref, acc_ref):
    tk = s.shape[1]
    m_old = m_ref[...]
    m_new = jnp.maximum(m_old, jnp.max(s, -1, keepdims=True))
    p = jnp.concatenate([jnp.exp2(s[:, LANES * c:LANES * (c + 1)] - m_new).astype(BF) for c in range(tk // LANES)],
                        axis=1)
    acc_ref[...] = jnp.exp2(m_old - m_new) * acc_ref[...] + _dot(p, v)
    m_ref[...] = m_new


def _softmax_init(m_ref, acc_ref):
    m_ref[...] = jnp.full(m_ref.shape, -jnp.inf, F32)
    acc_ref[...] = jnp.zeros_like(acc_ref)


def _flash_a_kernel(q_ref, k_ref, v_ref, o_ref, m_ref, acc_ref, *, tk, nk):
    tq = q_ref.shape[0]
    rep = A_HEADS // A_KV_HEADS
    q = jnp.concatenate([q_ref[:, LANES * r:LANES * (r + 1)] for r in range(rep)], axis=0)
    _softmax_init(m_ref, acc_ref)

    def body(j, carry):
        off = pl.multiple_of(j * tk, tk)
        _softmax_step(_dot_nt(q, k_ref[pl.ds(off, tk), :]), v_ref[pl.ds(off, tk), :], m_ref, acc_ref)
        return carry

    lax.fori_loop(0, nk, body, 0, unroll=2)
    acc = acc_ref[...]
    o = acc / acc[:, _ONE_LANE:_ONE_LANE + 1]
    for r in range(rep):
        o_ref[:, LANES * r:LANES * (r + 1)] = o[r * tq:(r + 1) * tq].astype(BF)


def _flash_a(qkv, tq=256, tk=1024):
    b, s, _ = qkv.shape
    tq, tk = _tile(s, tq), _tile(s, tk)
    rep = A_HEADS // A_KV_HEADS
    gw = rep * LANES
    return pl.pallas_call(
        functools.partial(_flash_a_kernel, tk=tk, nk=s // tk),
        grid=(b, A_KV_HEADS, s // tq),
        in_specs=[pl.BlockSpec((None, tq, gw), lambda bi, g, i: (bi, i, g)),
                  pl.BlockSpec((None, s, LANES), lambda bi, g, i: (bi, 0, A_HEADS + g)),
                  pl.BlockSpec((None, s, LANES), lambda bi, g, i: (bi, 0, _A_QK_TILES + g))],
        out_specs=pl.BlockSpec((None, tq, gw), lambda bi, g, i: (bi, i, g)),
        out_shape=jax.ShapeDtypeStruct((b, s, A_HEADS * LANES), BF),
        scratch_shapes=[pltpu.VMEM((rep * tq, LANES), F32)] * 2,
        compiler_params=_params("parallel", "parallel", "arbitrary"),
        name="flash_a",
    )(qkv, qkv, qkv)


def _bucket(rel):
    half = REL_BUCKETS // 2
    max_exact = half // 2
    n = jnp.abs(rel)
    large = max_exact + (jnp.log(jnp.maximum(n, 1).astype(F32) / max_exact)
                         / math.log(REL_MAX_DIST / max_exact) * (half - max_exact)).astype(I32)
    large = jnp.minimum(large, half - 1)
    return jnp.where(rel > 0, half, 0) + jnp.where(n < max_exact, n, large)


_B_TILES = 3 * len(B_PATTERNS) * B_HEADS_PER_GROUP
_B_GW = B_HEADS_PER_GROUP * LANES
_B_T = 256


def _pad_heads(w, heads, dim):
    w = w.reshape(w.shape[0], heads, dim)
    return jnp.concatenate([w, jnp.zeros((w.shape[0], heads, LANES - dim), w.dtype)], -1).reshape(
        w.shape[0], heads * LANES)


def _prep_b(w_qkv, w_o):
    nh = len(B_PATTERNS) * B_HEADS_PER_GROUP
    c = nh * B_HEAD_DIM
    wq = _pad_heads(w_qkv[:, :c] * (B_HEAD_DIM ** -0.5), nh, B_HEAD_DIM)
    wk = _pad_heads(w_qkv[:, c:2 * c], nh, B_HEAD_DIM)
    wv = _pad_heads(w_qkv[:, 2 * c:], nh, B_HEAD_DIM)
    w = jnp.concatenate([wq, wk, wv], 1).astype(BF)
    wo = w_o.reshape(nh, B_HEAD_DIM, D_MODEL)
    wo = jnp.concatenate([wo, jnp.zeros_like(wo)], 1).reshape(nh * LANES, D_MODEL).astype(BF)
    return w, wo


def _toeplitz(vec, t):
    flat = jnp.tile(vec, (1,) * (vec.ndim - 1) + (t,))[..., :t * (2 * t - 1)]
    return flat.reshape(vec.shape[:-1] + (t, 2 * t - 1))[..., :t]


def _band_tiles(g, t):
    window, dil = B_PATTERNS[g]
    return pl.cdiv((window // (2 * dil)) * dil, t)


def _band_bias(rel_bias, g, t):
    window, dil = B_PATTERNS[g]
    reach = (window // (2 * dil)) * dil
    n = _band_tiles(g, t)
    rel0 = (jnp.arange(2 * t) + t) % (2 * t) - t
    rel = jnp.arange(-n, n + 1)[:, None] * t + rel0[None, :]
    bias = rel_bias[_bucket(rel)][:, :, g * B_HEADS_PER_GROUP:(g + 1) * B_HEADS_PER_GROUP]
    bias = jnp.where(((rel % dil == 0) & (jnp.abs(rel) <= reach))[:, :, None], bias, NEG)
    return _toeplitz(jnp.transpose(bias, (2, 0, 1)).astype(F32), t)


def _band_kernel(q_ref, k_ref, v_ref, bias_ref, o_ref, lse_ref, *, n, nk):
    t = q_ref.shape[0]
    i = pl.program_id(2)
    q = q_ref[...]
    logits, offs = [], []
    for o in range(-n, n + 1):
        j = i + o
        off = pl.multiple_of(jnp.clip(j, 0, nk - 1) * t, t)
        s = _dot_nt(q, k_ref[pl.ds(off, t), :]) + bias_ref[o + n]
        logits.append(jnp.where((j >= 0) & (j < nk), s, NEG))
        offs.append(off)
    m = logits[0].max(-1, keepdims=True)
    for s in logits[1:]:
        m = jnp.maximum(m, s.max(-1, keepdims=True))
    l = jnp.zeros((t, 1), F32)
    acc = jnp.zeros((t, LANES), F32)
    for s, off in zip(logits, offs):
        p = jnp.exp(s - m)
        l = l + jnp.sum(p, -1, keepdims=True)
        acc = acc + _dot(p.astype(BF), v_ref[pl.ds(off, t), :])
    o_ref[...] = (acc / l).astype(BF)
    lse_ref[...] = jnp.broadcast_to(m + jnp.log(l), (t, LANES))


def _band_attention(qkv, bias, g, t):
    b, s, _ = qkv.shape
    nh = len(B_PATTERNS) * B_HEADS_PER_GROUP
    n = _band_tiles(g, t)
    spec = lambda base: pl.BlockSpec((None, s, LANES), lambda bi, h, i: (bi, 0, base + g * B_HEADS_PER_GROUP + h))
    out_spec = pl.BlockSpec((None, t, LANES), lambda bi, h, i: (bi, i, h))
    o, lse = pl.pallas_call(
        functools.partial(_band_kernel, n=n, nk=s // t),
        grid=(b, B_HEADS_PER_GROUP, s // t),
        in_specs=[pl.BlockSpec((None, t, LANES), lambda bi, h, i: (bi, i, g * B_HEADS_PER_GROUP + h)),
                  spec(nh), spec(2 * nh),
                  pl.BlockSpec((None, 2 * n + 1, t, t), lambda bi, h, i: (h, 0, 0, 0))],
        out_specs=[out_spec, out_spec],
        out_shape=[jax.ShapeDtypeStruct((b, s, _B_GW), BF), jax.ShapeDtypeStruct((b, s, _B_GW), F32)],
        compiler_params=_params("parallel", "parallel", "arbitrary"),
        name="band_%d" % g,
    )(qkv, qkv, qkv, bias)
    return o.reshape(b * s, _B_GW), lse.reshape(b * s, _B_GW)


def _prep_c(lam_re, lam_im, log_dt, b_re, b_im, c_re, c_im):
    hp = lax.Precision.HIGHEST
    L, P, C = C_CHUNK, C_STATE, C_GROUP
    lr, li = lam_re.astype(F32), lam_im.astype(F32)
    dt = jnp.exp(log_dt.astype(F32))[..., None]
    mag = jnp.exp(lr * dt)
    ar, ai = mag * jnp.cos(li * dt), mag * jnp.sin(li * dt)
    den = lr * lr + li * li
    zr = ((ar - 1.0) * lr + ai * li) / den
    zi = (ai * lr - (ar - 1.0) * li) / den
    br, bi = b_re.astype(F32), b_im.astype(F32)
    bbr = zr[..., None] * br - zi[..., None] * bi
    bbi = zr[..., None] * bi + zi[..., None] * br
    cr, ci = c_re.astype(F32), c_im.astype(F32)
    prs, pis = [jnp.ones_like(ar)], [jnp.zeros_like(ai)]
    for _ in range(L):
        pr_, pi_ = prs[-1], pis[-1]
        prs.append(ar * pr_ - ai * pi_)
        pis.append(ar * pi_ + ai * pr_)
    pr, pi = jnp.stack(prs, 0), jnp.stack(pis, 0)

    def lag(pr_k, pi_k):
        tr = pr_k[..., None] * bbr - pi_k[..., None] * bbi
        ti = pr_k[..., None] * bbi + pi_k[..., None] * bbr
        return (jnp.einsum('dgop,kdgpi->kdgoi', cr, tr, precision=hp)
                - jnp.einsum('dgop,kdgpi->kdgoi', ci, ti, precision=hp))

    kern = lag(pr[:L], pi[:L])
    jj = np.arange(L)[:, None]
    ii = np.arange(L)[None, :]
    kf = kern[:, 0][np.clip(ii - jj, 0, L - 1)] * jnp.asarray((ii >= jj)[:, :, None, None, None], F32)
    kb = kern[:, 1][np.clip(jj - ii, 0, L - 1)] * jnp.asarray((jj >= ii)[:, :, None, None, None], F32)
    m_intra = jnp.transpose(kf + kb, (2, 0, 4, 1, 3)).reshape(C_N_GROUPS, L * C, L * C)

    def summ(d, powers):
        pr_k, pi_k = pr[powers, d], pi[powers, d]
        sr = pr_k[..., None] * bbr[d] - pi_k[..., None] * bbi[d]
        si = pr_k[..., None] * bbi[d] + pi_k[..., None] * bbr[d]
        s = jnp.concatenate([sr, si], 2)
        return jnp.transpose(s, (1, 0, 3, 2)).reshape(C_N_GROUPS, L * C, 2 * P)

    w_sum = jnp.concatenate([summ(0, np.arange(L - 1, -1, -1)), summ(1, np.arange(L))], -1)

    def outw(d, powers):
        pr_k, pi_k = pr[powers, d], pi[powers, d]
        wr = cr[d][None] * pr_k[:, :, None, :] - ci[d][None] * pi_k[:, :, None, :]
        wi = -(cr[d][None] * pi_k[:, :, None, :] + ci[d][None] * pr_k[:, :, None, :])
        w = jnp.concatenate([wr, wi], -1)
        return jnp.transpose(w, (1, 3, 0, 2)).reshape(C_N_GROUPS, 2 * P, L * C)

    w_state = jnp.concatenate([outw(0, np.arange(1, L + 1)), outw(1, np.arange(L, 0, -1))], 1)

    nq = C_N_GROUPS // _C_GL
    npair = _C_GL // 2
    blk = L * _C_GL * C
    lane = jnp.arange(LANES)
    gsel = (lane[None, :] // C == jnp.arange(_C_GL)[:, None])
    src = jnp.tile(m_intra.astype(BF).reshape(nq, _C_GL, L, C, L, C), (1, 1, 1, 1, 1, _C_GL))
    src = jnp.where(gsel[None, :, None, None, None, :], src, 0)
    w_intra = jnp.transpose(src, (0, 2, 1, 3, 4, 5)).reshape(nq, blk, blk)
    psel = (lane[None, :] // P == jnp.arange(2)[:, None])
    ksel = jnp.eye(npair, dtype=bool)
    src = jnp.tile(w_sum.astype(BF).reshape(nq, npair, 2, L, C, 1, 4, P), (1, 1, 1, 1, 1, npair, 1, 2))
    src = jnp.where(psel[None, None, :, None, None, None, None, :]
                    & ksel[None, :, None, None, None, :, None, None], src, 0)
    w_sum = jnp.transpose(src, (0, 3, 1, 2, 4, 5, 6, 7)).reshape(nq, blk, _C_GL * 4 * P)
    osel = (lane[None, None, :] // C
            == (2 * jnp.arange(npair)[:, None, None] + jnp.arange(2)[None, :, None]))
    src = jnp.tile(w_state.astype(BF).reshape(nq, npair, 2, 4, P, L, C), (1, 1, 1, 1, 1, 1, _C_GL))
    src = jnp.where(osel[None, :, :, None, None, None, :], src, 0)
    w_state = jnp.transpose(src, (0, 1, 3, 2, 4, 5, 6)).reshape(nq, _C_GL * 4 * P, blk)
    dec = jnp.stack([pr[L, 0], pi[L, 0], pr[L, 1], pi[L, 1]], 0).reshape(4, nq * npair, 2 * P)
    dec = jnp.broadcast_to(jnp.transpose(dec, (1, 0, 2))[:, :, None, :], (nq * npair, 4, 8, 2 * P))
    return w_sum.astype(BF), w_intra.astype(BF), w_state.astype(BF), dec


def _chunk_rows(x_ref, nb, rc):
    L = C_CHUNK
    rows = [jnp.concatenate([x_ref[b, pl.ds(j, rc, stride=L), :].astype(BF) for j in range(L)], axis=1)
            for b in range(nb)]
    return jnp.concatenate(rows, axis=0)


def _s5_sum_kernel(x_ref, w_ref, s_ref, *, nb, rc):
    s = _dot(_chunk_rows(x_ref, nb, rc), w_ref[...])
    for b in range(nb):
        for t in range(s_ref.shape[0]):
            s_ref[t, pl.ds(b, rc, stride=nb), :] = s[b * rc:(b + 1) * rc, LANES * t:LANES * (t + 1)]


def _s5_scan_kernel(s_ref, dec_ref, e_ref, *, nc, nb):
    units = s_ref.shape[0] // 4
    dec = [[dec_ref[u, k, 0:nb, :] for k in range(4)] for u in range(units)]

    def body(t, carry):
        rf = pl.ds(pl.multiple_of(t * nb, nb), nb)
        rb = pl.ds(pl.multiple_of((nc - 1 - t) * nb, nb), nb)
        new = []
        for u in range(units):
            fr, fi, br, bi = carry[4 * u:4 * u + 4]
            arf, aif, arb, aib = dec[u]
            e_ref[4 * u, rf, :] = fr
            e_ref[4 * u + 1, rf, :] = fi
            e_ref[4 * u + 2, rb, :] = br
            e_ref[4 * u + 3, rb, :] = bi
            new += [fr * arf - fi * aif + s_ref[4 * u, rf, :], fi * arf + fr * aif + s_ref[4 * u + 1, rf, :],
                    br * arb - bi * aib + s_ref[4 * u + 2, rb, :], bi * arb + br * aib + s_ref[4 * u + 3, rb, :]]
        return tuple(new)

    z = jnp.zeros((nb, LANES), F32)
    lax.fori_loop(0, nc, body, (z,) * (4 * units))


def _s5_out_kernel(x_ref, e_ref, wi_ref, wc_ref, y_ref, *, nb, rc):
    L = C_CHUNK
    half = pl.program_id(2)
    xc = _chunk_rows(x_ref, nb, rc)
    ec = jnp.concatenate(
        [jnp.concatenate([e_ref[t, pl.ds(b, rc, stride=nb), :].astype(BF) for t in range(e_ref.shape[0])], axis=1)
         for b in range(nb)], axis=0)
    y = _dot(xc, wi_ref[...]) + _dot(ec, wc_ref[...])
    for b in range(nb):
        for ii in range(L // 2):
            y_ref[b, pl.ds(half * (L // 2) + ii, rc, stride=L), :] = y[b * rc:(b + 1) * rc, LANES * ii:LANES * (ii + 1)]


def _s5(x, b, s, w_sum, w_intra, w_state, dec):
    L = C_CHUNK
    nc = s // L
    nq = D_MODEL // LANES
    nsl = w_sum.shape[2] // LANES
    blk = L * LANES
    rc = _tile(nc, max(8, 512 // b))
    x3 = x.reshape(b, s, D_MODEL)
    x_spec = pl.BlockSpec((b, rc * L, LANES), lambda q, c, *_: (0, c, q))
    sums = pl.pallas_call(
        functools.partial(_s5_sum_kernel, nb=b, rc=rc),
        grid=(nq, nc // rc),
        in_specs=[x_spec, pl.BlockSpec((None, blk, nsl * LANES), lambda q, c: (q, 0, 0))],
        out_specs=pl.BlockSpec((nsl, rc * b, LANES), lambda q, c: (q, c, 0)),
        out_shape=jax.ShapeDtypeStruct((nq * nsl, nc * b, LANES), F32),
        compiler_params=_params("parallel", "arbitrary"),
        name="s5_sum",
    )(x3, w_sum)
    upb = 2
    ent = pl.pallas_call(
        functools.partial(_s5_scan_kernel, nc=nc, nb=b),
        grid=(nq * nsl // (4 * upb),),
        in_specs=[pl.BlockSpec((4 * upb, nc * b, LANES), lambda i: (i, 0, 0)),
                  pl.BlockSpec((upb, 4, 8, LANES), lambda i: (i, 0, 0, 0))],
        out_specs=pl.BlockSpec((4 * upb, nc * b, LANES), lambda i: (i, 0, 0)),
        out_shape=jax.ShapeDtypeStruct((nq * nsl, nc * b, LANES), F32),
        compiler_params=_params("parallel"),
        name="s5_scan",
    )(sums, dec)
    y = pl.pallas_call(
        functools.partial(_s5_out_kernel, nb=b, rc=rc),
        grid=(nq, nc // rc, 2),
        in_specs=[x_spec,
                  pl.BlockSpec((nsl, rc * b, LANES), lambda q, c, h: (q, c, 0)),
                  pl.BlockSpec((None, blk, blk // 2), lambda q, c, h: (q, 0, h)),
                  pl.BlockSpec((None, nsl * LANES, blk // 2), lambda q, c, h: (q, 0, h))],
        out_specs=pl.BlockSpec((b, rc * L, LANES), lambda q, c, h: (0, c, q)),
        out_shape=jax.ShapeDtypeStruct((b, s, D_MODEL), F32),
        compiler_params=_params("parallel", "arbitrary", "arbitrary"),
        name="s5_out",
    )(x3, ent, w_intra, w_state)
    return y.reshape(b * s, D_MODEL)


_D_T = 512


def _prep_d(w_qkv, norm_gain, w_o, lambda_init):
    qk_w = D_HEADS * 2 * D_QK_DIM
    wq = _pad_heads(w_qkv[:, :qk_w] * (D_QK_DIM ** -0.5 * LOG2E), D_HEADS, 2 * D_QK_DIM)
    wk = _pad_heads(w_qkv[:, qk_w:2 * qk_w], D_HEADS, 2 * D_QK_DIM)
    wv = _pad_heads(w_qkv[:, 2 * qk_w:], D_HEADS, D_V_DIM)
    w = jnp.concatenate([wq, wk, wv], 1).astype(BF)
    gain = jnp.concatenate([norm_gain * (1.0 - lambda_init), jnp.zeros((LANES - D_V_DIM,), F32)])[None, :]
    wo = w_o.reshape(D_HEADS, D_V_DIM, D_MODEL)
    wo = jnp.concatenate([wo, jnp.zeros_like(wo)], 1).reshape(D_HEADS * LANES, D_MODEL).astype(BF)
    return w, gain, wo


def _diff_bias(rel_bias, t):
    rel0 = (jnp.arange(2 * t) + t) % (2 * t) - t
    rel = jnp.arange(-2, 3)[:, None] * t + rel0[None, :]
    return jnp.transpose(rel_bias[_bucket(rel)] * LOG2E, (2, 0, 1)).astype(F32)


def _flash_d_kernel(q_ref, k_ref, v_ref, vec_ref, lam_ref, gain_ref, o_ref, m_ref, acc_ref, bias_ref, *, kt, nk,
                    lambda_init):
    t = q_ref.shape[0]
    i = pl.program_id(2)

    @pl.when(i == 0)
    def _():
        for d in range(bias_ref.shape[0]):
            full = jnp.broadcast_to(vec_ref[d:d + 1, :], (t, 2 * t))
            bias_ref[d] = pltpu.roll(full, 0, 1, stride=1, stride_axis=0)[:, :t]

    qv = q_ref[...].astype(F32)
    lane = lax.broadcasted_iota(I32, (t, LANES), 1)
    q0 = jnp.where(lane < D_QK_DIM, qv, 0.0).astype(BF)
    q1 = jnp.where((lane >= D_QK_DIM) & (lane < 2 * D_QK_DIM), qv, 0.0).astype(BF)
    q = jnp.concatenate([q0, q1], axis=0)
    _softmax_init(m_ref, acc_ref)

    def body(j, carry):
        off = pl.multiple_of(j * (kt * t), kt * t)
        bias = jnp.concatenate([bias_ref[jnp.clip(j * kt + c - i, -2, 2) + 2] for c in range(kt)], axis=1)
        s = _dot_nt(q, k_ref[pl.ds(off, kt * t), :])
        s = (s.reshape(2, t, kt * t) + bias[None]).reshape(2 * t, kt * t)
        _softmax_step(s, v_ref[pl.ds(off, kt * t), :], m_ref, acc_ref)
        return carry

    lax.fori_loop(0, nk, body, 0, unroll=2)
    lf = lam_ref[...]
    lam = (jnp.exp(jnp.sum(lf[0:1] * lf[1:2], keepdims=True))
           - jnp.exp(jnp.sum(lf[2:3] * lf[3:4], keepdims=True)) + lambda_init)
    acc = acc_ref[...]
    on = acc / acc[:, _ONE_LANE:_ONE_LANE + 1]
    o = jnp.where(lane < D_V_DIM, on[:t] - lam * on[t:], 0.0)
    ms = jnp.sum(o * o, -1, keepdims=True) * (1.0 / D_V_DIM)
    o_ref[...] = (o * lax.rsqrt(ms + RMS_EPS) * gain_ref[...]).astype(BF)


def _flash_d(qkv, bias, lam, gain, lambda_init):
    b, s, _ = qkv.shape
    t = bias.shape[-1] // 2
    kt = 2 if s % (2 * t) == 0 else 1
    return pl.pallas_call(
        functools.partial(_flash_d_kernel, kt=kt, nk=s // (kt * t), lambda_init=lambda_init),
        grid=(b, D_HEADS, s // t),
        in_specs=[pl.BlockSpec((None, t, LANES), lambda bi, h, i: (bi, i, h)),
                  pl.BlockSpec((None, s, LANES), lambda bi, h, i: (bi, 0, D_HEADS + h)),
                  pl.BlockSpec((None, s, LANES), lambda bi, h, i: (bi, 0, 2 * D_HEADS + h)),
                  pl.BlockSpec((None, 5, 2 * t), lambda bi, h, i: (h, 0, 0)),
                  pl.BlockSpec((4, D_QK_DIM), lambda bi, h, i: (0, 0)),
                  pl.BlockSpec((1, LANES), lambda bi, h, i: (0, 0))],
        out_specs=pl.BlockSpec((None, t, LANES), lambda bi, h, i: (bi, i, h)),
        out_shape=jax.ShapeDtypeStruct((b, s, D_HEADS * LANES), BF),
        scratch_shapes=[pltpu.VMEM((2 * t, LANES), F32)] * 2 + [pltpu.VMEM((5, t, t), F32)],
        compiler_params=_params("parallel", "parallel", "arbitrary"),
        name="flash_d",
    )(qkv, qkv, qkv, bias, lam, gain)


def _cross_kernel(x_ref, kv_ref, wq_ref, wo_ref, g_ref, b_ref, wr_ref, y_ref, ybf_ref, aff_ref):
    x = x_ref[...]
    tm = x.shape[0]
    q = (_dot(x.astype(BF), wq_ref[...]) * (X_HEAD_DIM ** -0.5)).astype(BF)
    outs = []
    for h in range(X_HEADS):
        sl = slice(X_HEAD_DIM * h, X_HEAD_DIM * (h + 1))
        s = _dot_nt(q[:, sl], kv_ref[:, sl])
        p = jnp.exp(s - jnp.max(s, -1, keepdims=True))
        l = jnp.sum(p, -1, keepdims=True)
        vh = kv_ref[:, D_MODEL + X_HEAD_DIM * h:D_MODEL + X_HEAD_DIM * (h + 1)]
        outs.append((_dot(p.astype(BF), vh) / l).astype(BF))
    o = jnp.concatenate(outs, axis=1)
    y = _ln_rows(ALPHA * x + _dot(o, wo_ref[...]), g_ref[...], b_ref[...])
    y_ref[...] = y
    yh = y.astype(BF)
    ybf_ref[...] = yh
    yl = (y - yh.astype(F32)).astype(BF)
    wr = wr_ref[...]
    wh = wr.astype(BF)
    wl = (wr - wh.astype(F32)).astype(BF)
    lg = _dot_nt(wh, yh) + _dot_nt(wh, yl) + _dot_nt(wl, yh)
    e = jnp.exp(lg - jnp.max(lg, 0, keepdims=True))
    aff = e / jnp.sum(e, 0, keepdims=True)
    for c in range(tm // LANES):
        aff_ref[c] = aff[:, LANES * c:LANES * (c + 1)]


def _cross(x, kv, wq, wo, g, b, wr_t, s, mem_len, tm=512):
    n = x.shape[0]
    tm = _tile(s, tm)
    per = s // tm
    fixed = lambda i: (0, 0)
    return pl.pallas_call(
        _cross_kernel,
        grid=(n // tm,),
        in_specs=[pl.BlockSpec((tm, D_MODEL), lambda i: (i, 0)),
                  pl.BlockSpec((mem_len, 2 * D_MODEL), lambda i: (i // per, 0)),
                  pl.BlockSpec((D_MODEL, D_MODEL), fixed), pl.BlockSpec((D_MODEL, D_MODEL), fixed),
                  pl.BlockSpec((1, D_MODEL), fixed), pl.BlockSpec((1, D_MODEL), fixed),
                  pl.BlockSpec((N_EXPERTS, D_MODEL), fixed)],
        out_specs=[pl.BlockSpec((tm, D_MODEL), lambda i: (i, 0)),
                   pl.BlockSpec((tm, D_MODEL), lambda i: (i, 0)),
                   pl.BlockSpec((tm // LANES, N_EXPERTS, LANES), lambda i: (i, 0, 0))],
        out_shape=[jax.ShapeDtypeStruct((n, D_MODEL), F32),
                   jax.ShapeDtypeStruct((n, D_MODEL), BF),
                   jax.ShapeDtypeStruct((n // LANES, N_EXPERTS, LANES), F32)],
        compiler_params=_params("parallel"),
        name="cross",
    )(x, kv, wq, wo, g, b, wr_t)


def _select_kernel(a_ref, pos_ref, st_ref, *, k, nbits):
    nt = a_ref.shape[0]
    shape = (nt, N_EXPERTS, LANES)
    kf = float(k)

    def keys():
        return lax.bitcast_convert_type(a_ref[...], I32)

    def count(mask):
        c = jnp.sum(jnp.where(mask, 1.0, 0.0), axis=0, keepdims=True)
        return jnp.sum(c, axis=2, keepdims=True)

    def value_step(it, thr):
        cand = thr | jnp.left_shift(jnp.int32(1), 30 - it)
        return jnp.where(count(keys() >= cand) >= kf, cand, thr)

    thr = lax.fori_loop(0, 31, value_step, jnp.zeros((1, N_EXPERTS, 1), I32))
    need = kf - count(keys() > thr)
    idx = lax.broadcasted_iota(I32, shape, 0) * LANES + lax.broadcasted_iota(I32, shape, 2)

    def index_step(it, ithr):
        cand = ithr | jnp.left_shift(jnp.int32(1), nbits - 1 - it)
        return jnp.where(count((keys() == thr) & (idx < cand)) < need, cand, ithr)

    ithr = lax.fori_loop(0, nbits, index_step, jnp.zeros((1, N_EXPERTS, 1), I32))
    thr2, ithr2 = thr[0], ithr[0]
    upper = jnp.where(lax.broadcasted_iota(I32, (LANES, LANES), 0) <= lax.broadcasted_iota(I32, (LANES, LANES), 1),
                      1.0, 0.0).astype(BF)
    lane = lax.broadcasted_iota(I32, (N_EXPERTS, LANES), 1)

    def tile_step(j, carry):
        kj = lax.bitcast_convert_type(a_ref[j], I32)
        sel = (kj > thr2) | ((kj == thr2) & (j * LANES + lane <= ithr2))
        m = jnp.where(sel, 1.0, 0.0)
        inc = _dot(m.astype(BF), upper)
        pos_ref[j] = jnp.where(sel, inc - m + carry, -1.0).astype(I32)
        st_ref[j] = jnp.broadcast_to(carry, (N_EXPERTS, LANES)).astype(I32)
        return carry + inc[:, LANES - 1:LANES]

    lax.fori_loop(0, nt, tile_step, jnp.zeros((N_EXPERTS, 1), F32))


def _select(aff3, k):
    nt = aff3.shape[0]
    nbits = max(1, int(math.ceil(math.log2(nt * LANES))))
    shp = jax.ShapeDtypeStruct(aff3.shape, I32)
    return pl.pallas_call(
        functools.partial(_select_kernel, k=k, nbits=nbits),
        out_shape=[shp, shp],
        compiler_params=pltpu.CompilerParams(vmem_limit_bytes=VMEM_LIMIT),
        name="select",
    )(aff3)


_GATHER_ROWS = LANES + BF16_ROWS


_MOE_GROUP = 2


def _moe_ffn_kernel(st_ref, x_ref, pos_ref, aff_ref, wg_ref, wu_ref, wd_ref, ye_ref, buf_ref, gate_ref, *, nb, ncf,
                    nch, sub, ck):
    ep = buf_ref.shape[0]
    grp = pl.program_id(0)
    t = pl.program_id(1)

    @pl.when(t == 0)
    def _():
        buf_ref[...] = jnp.zeros_like(buf_ref)
        gate_ref[...] = jnp.zeros_like(gate_ref)

    @pl.when(t < nb)
    def _():
        row = lax.broadcasted_iota(I32, (_GATHER_ROWS, LANES), 0)
        for s in range(sub):
            xs = x_ref[LANES * s:LANES * (s + 1), :]
            for k in range(ep):
                e = grp * ep + k
                st = st_ref[e, t * sub + s]
                base = pl.multiple_of((st // BF16_ROWS) * BF16_ROWS, BF16_ROWS)
                rows = pl.ds(base, _GATHER_ROWS)
                hit = row == pos_ref[s, pl.ds(e, 1), :] - base
                buf_ref[k, rows, :] = buf_ref[k, rows, :] + _dot(jnp.where(hit, 1.0, 0.0).astype(BF), xs).astype(BF)
                gate_ref[k, rows, :] = gate_ref[k, rows, :] + jnp.sum(
                    jnp.where(hit, aff_ref[s, pl.ds(e, 1), :], 0.0), -1, keepdims=True)

    @pl.when(t >= nb)
    def _():
        k = (t - nb) // nch
        c = (t - nb) - k * nch

        @pl.when(c < ncf)
        def _():
            rows = pl.ds(pl.multiple_of(c * ck, ck), ck)
            xe = buf_ref[k, rows, :]
            h = (jax.nn.silu(_dot(xe, wg_ref[...])) * _dot(xe, wu_ref[...])).astype(BF)
            ye_ref[...] = (_dot(h, wd_ref[...]) * gate_ref[k, rows, :]).astype(BF)

        @pl.when(c >= ncf)
        def _():
            ye_ref[...] = jnp.zeros_like(ye_ref)


def _moe_ffn(starts, xbf, pos3, aff3, wg, wu, wd, cap, tb=1024):
    n = xbf.shape[0]
    tb = _tile(n, tb)
    nb = n // tb
    sub = tb // LANES
    ck = _tile(cap, 512)
    ncf = cap // ck
    nch = ncf + pl.cdiv(_WIN_ROWS, ck)
    dff = wg.shape[-1]
    ep = _MOE_GROUP
    blk = lambda g, t, st: (jnp.minimum(t, nb - 1), 0)
    blk3 = lambda g, t, st: (jnp.minimum(t, nb - 1), 0, 0)
    expert = lambda g, t: g * ep + jnp.clip((t - nb) // nch, 0, ep - 1)
    wmap = lambda g, t, st: (expert(g, t), 0, 0)
    return pl.pallas_call(
        functools.partial(_moe_ffn_kernel, nb=nb, ncf=ncf, nch=nch, sub=sub, ck=ck),
        grid_spec=pltpu.PrefetchScalarGridSpec(
            num_scalar_prefetch=1,
            grid=(N_EXPERTS // ep, nb + ep * nch),
            in_specs=[pl.BlockSpec((tb, D_MODEL), blk),
                      pl.BlockSpec((sub, N_EXPERTS, LANES), blk3),
                      pl.BlockSpec((sub, N_EXPERTS, LANES), blk3),
                      pl.BlockSpec((None, D_MODEL, dff), wmap),
                      pl.BlockSpec((None, D_MODEL, dff), wmap),
                      pl.BlockSpec((None, dff, D_MODEL), wmap)],
            out_specs=pl.BlockSpec((None, ck, D_MODEL),
                                   lambda g, t, st: (expert(g, t), jnp.maximum(t - nb, 0) % nch, 0)),
            scratch_shapes=[pltpu.VMEM((ep, cap + _GATHER_ROWS, D_MODEL), BF),
                            pltpu.VMEM((ep, cap + _GATHER_ROWS, 1), F32)]),
        out_shape=jax.ShapeDtypeStruct((N_EXPERTS, nch * ck, D_MODEL), BF),
        compiler_params=_params("arbitrary", "arbitrary"),
        name="moe_ffn",
    )(starts, xbf, pos3, aff3, wg, wu, wd)


_WIN_STEP = 128
_COMB_TILES = 2
_WIN_ROWS = _WIN_STEP + (_COMB_TILES - 1) * LANES + _GATHER_ROWS


def _combine_kernel(st_ref, x_ref, pos_ref, g_ref, b_ref, *rest):
    ye_refs, y_ref = rest[:N_EXPERTS], rest[N_EXPERTS]
    j = pl.program_id(0)
    col = lax.broadcasted_iota(I32, (LANES, _GATHER_ROWS), 1)
    for u in range(_COMB_TILES):
        tok = slice(LANES * u, LANES * (u + 1))
        acc = None
        for e in range(N_EXPERTS):
            st = st_ref[e, j * _COMB_TILES + u]
            base = (st // BF16_ROWS) * BF16_ROWS
            win = (st_ref[e, j * _COMB_TILES] // _WIN_STEP) * _WIN_STEP
            sub = pl.multiple_of(base - win, BF16_ROWS)
            onehot = jnp.where(col == pos_ref[tok, e:e + 1] - base, 1.0, 0.0).astype(BF)
            d = _dot(onehot, ye_refs[e][0, pl.ds(sub, _GATHER_ROWS), :])
            acc = d if acc is None else acc + d
        y_ref[tok, :] = _ln_rows(ALPHA * x_ref[tok, :] + acc, g_ref[...], b_ref[...])


def _combine(starts, x, pos, g, b, ye):
    n = x.shape[0]
    assert ye.shape[1] >= EC_CAPACITY * n // N_EXPERTS + _WIN_ROWS
    tt = _COMB_TILES * LANES
    row = lambda j, st: (j, 0)
    fixed = lambda j, st: (0, 0)

    def ye_spec(e):
        return pl.BlockSpec((pl.Element(1), pl.Element(_WIN_ROWS), pl.Element(D_MODEL)),
                            lambda j, st: (e, (st[e, j * _COMB_TILES] // _WIN_STEP) * _WIN_STEP, 0))

    return pl.pallas_call(
        _combine_kernel,
        grid_spec=pltpu.PrefetchScalarGridSpec(
            num_scalar_prefetch=1,
            grid=(n // tt,),
            in_specs=[pl.BlockSpec((tt, D_MODEL), row),
                      pl.BlockSpec((tt, N_EXPERTS), row),
                      pl.BlockSpec((1, D_MODEL), fixed), pl.BlockSpec((1, D_MODEL), fixed)]
            + [ye_spec(e) for e in range(N_EXPERTS)],
            out_specs=pl.BlockSpec((tt, D_MODEL), row)),
        out_shape=jax.ShapeDtypeStruct((n, D_MODEL), F32),
        compiler_params=_params("arbitrary"),
        name="combine",
    )(starts, x, pos, g, b, *([ye] * N_EXPERTS))


def _moe(x, xg, aff3, wg, wu, wd, g, b):
    n = x.shape[0]
    cap = EC_CAPACITY * n // N_EXPERTS
    pos3, st3 = _select(aff3, cap)
    starts = jnp.transpose(st3[:, :, 0])
    ye = _moe_ffn(starts, xg, pos3, aff3, wg, wu, wd, cap)
    pos = jnp.transpose(pos3, (0, 2, 1)).reshape(n, N_EXPERTS)
    return _combine(starts, x, pos, g, b, ye)


def _prep_weights(p):
    w = {}
    w['a'] = [_prep_a(p['a_w_qkv'][j], p['a_q_gain'][j], p['a_k_gain'][j], p['a_w_o'][j])
              for j in range(p['a_w_qkv'].shape[0])]
    w['b'] = [_prep_b(p['b_w_qkv'][j], p['b_w_o'][j]) for j in range(p['b_w_qkv'].shape[0])]
    w['c'] = [_prep_c(p['c_lam_re'][j], p['c_lam_im'][j], p['c_log_dt'][j], p['c_b_re'][j], p['c_b_im'][j],
                      p['c_c_re'][j], p['c_c_im'][j]) + (p['c_d'][j][None, :], p['c_w_glu'][j].astype(BF))
              for j in range(p['c_lam_re'].shape[0])]
    w['d'] = []
    for j in range(p['d_w_qkv'].shape[0]):
        layer = N_MIXERS * j + 3
        lambda_init = 0.8 - 0.6 * math.exp(-0.3 * layer)
        w['d'].append(_prep_d(p['d_w_qkv'][j], p['d_norm_gain'][j], p['d_w_o'][j], lambda_init)
                      + (p['d_lam'][j].astype(F32), lambda_init))
    w['diff_bias'] = _diff_bias(p['rel_bias'], _D_T)
    w['x_w_q'] = p['x_w_q'].astype(BF)
    w['x_w_kv'] = p['x_w_kv'].astype(BF)
    w['x_w_o'] = p['x_w_o'].astype(BF)
    w['router_t'] = jnp.transpose(p['moe_w_router'], (0, 2, 1)).astype(F32)
    w['moe_w_gate'] = p['moe_w_gate'].astype(BF)
    w['moe_w_up'] = p['moe_w_up'].astype(BF)
    w['moe_w_down'] = p['moe_w_down'].astype(BF)
    return w


def _trunk(x, mem, p, w):
    b, s, _ = x.shape
    n = b * s
    mem_len = mem.shape[1]
    x = x.reshape(n, D_MODEL)
    mem2 = mem.reshape(b * mem_len, D_MODEL)
    ln_g, ln_b = p['ln_g'], p['ln_b']
    for i in range(DEPTH):
        m, j = i % N_MIXERS, i // N_MIXERS
        g0, b0 = ln_g[i, 0][None, :], ln_b[i, 0][None, :]
        if m == 0:
            wa, gains, wo = w['a'][j]
            qkv = _proj_a(x, wa, _rope_table(s), gains, s)
            o = _flash_a(qkv.reshape(b, s, _A_COLS))
            x = _post(x, o.reshape(n, A_HEADS * LANES), wo, g0, b0)
        elif m == 1:
            wb, wo = w['b'][j]
            qkv = _proj(x, wb).reshape(b, s, _B_TILES * LANES)
            os_, ls_ = [], []
            for g in range(len(B_PATTERNS)):
                t = _tile(s, _B_T)
                o, lse = _band_attention(qkv, _band_bias(p['rel_bias'], g, t), g, t)
                os_.append(o)
                ls_.append(lse)
            x = _post_b(x, os_, ls_, wo, g0, b0)
        elif m == 2:
            w_sum, w_intra, w_state, dec, dskip, wglu = w['c'][j]
            ys = _s5(x, b, s, w_sum, w_intra, w_state, dec)
            x = _post_c(x, ys, dskip, wglu, g0, b0)
        else:
            wd, gain, wo, lam, lambda_init = w['d'][j]
            qkv = _proj(x, wd, ones_from=2 * D_HEADS * LANES).reshape(b, s, 3 * D_HEADS * LANES)
            o = _flash_d(qkv, w['diff_bias'], lam, gain, lambda_init)
            x = _post(x, o.reshape(n, D_HEADS * LANES), wo, g0, b0)
        kv = _proj(mem2, w['x_w_kv'][i], tm=mem_len)
        x, xbf, aff3 = _cross(x, kv, w['x_w_q'][i], w['x_w_o'][i], ln_g[i, 1][None, :], ln_b[i, 1][None, :],
                              w['router_t'][i], s, mem_len)
        x = _moe(x, xbf, aff3, w['moe_w_gate'][i], w['moe_w_up'][i], w['moe_w_down'][i],
                 ln_g[i, 2][None, :], ln_b[i, 2][None, :])
    return x.reshape(b, s, D_MODEL)


def kernel(x_prompt, x_sample, mem_prompt, mem_sample, rel_bias, ln_g, ln_b, a_w_qkv, a_q_gain, a_k_gain, a_w_o, b_w_qkv, b_w_o, c_lam_re, c_lam_im, c_log_dt, c_b_re, c_b_im, c_c_re, c_c_im, c_d, c_w_glu, d_w_qkv, d_lam, d_norm_gain, d_w_o, x_w_q, x_w_kv, x_w_o, moe_w_router, moe_w_gate, moe_w_up, moe_w_down):
    p = dict(rel_bias=rel_bias, ln_g=ln_g, ln_b=ln_b,
             a_w_qkv=a_w_qkv, a_q_gain=a_q_gain, a_k_gain=a_k_gain, a_w_o=a_w_o,
             b_w_qkv=b_w_qkv, b_w_o=b_w_o,
             c_lam_re=c_lam_re, c_lam_im=c_lam_im, c_log_dt=c_log_dt, c_b_re=c_b_re, c_b_im=c_b_im,
             c_c_re=c_c_re, c_c_im=c_c_im, c_d=c_d, c_w_glu=c_w_glu,
             d_w_qkv=d_w_qkv, d_lam=d_lam, d_norm_gain=d_norm_gain, d_w_o=d_w_o,
             x_w_q=x_w_q, x_w_kv=x_w_kv, x_w_o=x_w_o,
             moe_w_router=moe_w_router, moe_w_gate=moe_w_gate, moe_w_up=moe_w_up, moe_w_down=moe_w_down)
    w = _prep_weights(p)
    return (_trunk(x_prompt, mem_prompt, p, w), _trunk(x_sample, mem_sample, p, w))
```

```python
import functools
import math

import numpy as np
import jax
import jax.numpy as jnp
from jax import lax
from jax.experimental import pallas as pl
from jax.experimental.pallas import tpu as pltpu

F32 = jnp.float32
BF = jnp.bfloat16
I32 = jnp.int32

D_MODEL = 1024
DEPTH = 4
GRID_W = 64
N_MIXERS = 4
LN_EPS = 1e-5
RMS_EPS = 1e-6
ALPHA = (2.0 * DEPTH) ** 0.25

A_HEADS = 16
A_KV_HEADS = 4
A_HEAD_DIM = 64
ROPE_BASE = 10000.0

B_PATTERNS = ((128, 1), (512, 4), (2048, 16))
B_HEADS_PER_GROUP = 4
B_HEAD_DIM = 64

C_GROUP = 16
C_N_GROUPS = D_MODEL // C_GROUP
C_STATE = 64
C_CHUNK = 16
_C_GL = 128 // C_GROUP

D_HEADS = 12
D_QK_DIM = 32
D_V_DIM = 64

REL_BUCKETS = 32
REL_MAX_DIST = 128

X_HEADS = 4
X_HEAD_DIM = D_MODEL // X_HEADS

N_EXPERTS = 16
EC_CAPACITY = 2

LANES = 128
BF16_ROWS = 16
VMEM_LIMIT = 56 * 1024 * 1024
NEG = -1e30
LOG2E = math.log2(math.e)


def _params(*sem):
    return pltpu.CompilerParams(dimension_semantics=sem, vmem_limit_bytes=VMEM_LIMIT)


def _tile(n, pref):
    t = min(n, pref)
    assert n % t == 0, (n, pref)
    return t


def _ln_rows(v, g, b):
    mu = jnp.mean(v, -1, keepdims=True)
    c = v - mu
    var = jnp.mean(c * c, -1, keepdims=True)
    return c * lax.rsqrt(var + LN_EPS) * g + b


def _dot_nt(a, b):
    return lax.dot_general(a, b, (((1,), (1,)), ((), ())), preferred_element_type=F32)


def _dot(a, b):
    return jnp.dot(a, b, preferred_element_type=F32)


_ONE_LANE = 64


def _with_ones_lane(y):
    lane = lax.broadcasted_iota(I32, y.shape, 1)
    return jnp.where(lane % LANES == _ONE_LANE, 1.0, y)


def _proj_kernel(x_ref, w_ref, o_ref, *, ones_from):
    y = _dot(x_ref[...].astype(BF), w_ref[...])
    if ones_from is None:
        o_ref[...] = y.astype(o_ref.dtype)
    else:
        o_ref[:, :ones_from] = y[:, :ones_from].astype(o_ref.dtype)
        o_ref[:, ones_from:] = _with_ones_lane(y[:, ones_from:]).astype(o_ref.dtype)


def _proj(x, w, tm=512, ones_from=None):
    n, k = x.shape
    m = w.shape[1]
    tm = _tile(n, tm)
    return pl.pallas_call(
        functools.partial(_proj_kernel, ones_from=ones_from),
        grid=(n // tm,),
        in_specs=[pl.BlockSpec((tm, k), lambda i: (i, 0)),
                  pl.BlockSpec((k, m), lambda i: (0, 0))],
        out_specs=pl.BlockSpec((tm, m), lambda i: (i, 0)),
        out_shape=jax.ShapeDtypeStruct((n, m), BF),
        compiler_params=_params("parallel"),
        name="proj",
    )(x, w)


def _post_kernel(x_ref, o_ref, w_ref, g_ref, b_ref, y_ref):
    h = _dot(o_ref[...], w_ref[...])
    y_ref[...] = _ln_rows(ALPHA * x_ref[...] + h, g_ref[...], b_ref[...])


def _post(x, o, w, g, b, tm=512):
    n = x.shape[0]
    ko = o.shape[1]
    tm = _tile(n, tm)
    return pl.pallas_call(
        _post_kernel,
        grid=(n // tm,),
        in_specs=[pl.BlockSpec((tm, D_MODEL), lambda i: (i, 0)),
                  pl.BlockSpec((tm, ko), lambda i: (i, 0)),
                  pl.BlockSpec((ko, D_MODEL), lambda i: (0, 0)),
                  pl.BlockSpec((1, D_MODEL), lambda i: (0, 0)),
                  pl.BlockSpec((1, D_MODEL), lambda i: (0, 0))],
        out_specs=pl.BlockSpec((tm, D_MODEL), lambda i: (i, 0)),
        out_shape=jax.ShapeDtypeStruct((n, D_MODEL), F32),
        compiler_params=_params("parallel"),
        name="post",
    )(x, o, w, g, b)


def _post_b_kernel(x_ref, o0_ref, o1_ref, o2_ref, l0_ref, l1_ref, l2_ref, w_ref, g_ref, b_ref, y_ref):
    l0, l1, l2 = l0_ref[...], l1_ref[...], l2_ref[...]
    m = jnp.maximum(jnp.maximum(l0, l1), l2)
    e0, e1, e2 = jnp.exp(l0 - m), jnp.exp(l1 - m), jnp.exp(l2 - m)
    inv = 1.0 / (e0 + e1 + e2)
    gw = B_HEADS_PER_GROUP * LANES
    h = _dot((o0_ref[...].astype(F32) * (e0 * inv)).astype(BF), w_ref[0:gw, :])
    h = h + _dot((o1_ref[...].astype(F32) * (e1 * inv)).astype(BF), w_ref[gw:2 * gw, :])
    h = h + _dot((o2_ref[...].astype(F32) * (e2 * inv)).astype(BF), w_ref[2 * gw:3 * gw, :])
    y_ref[...] = _ln_rows(ALPHA * x_ref[...] + h, g_ref[...], b_ref[...])


def _post_b(x, os_, ls_, w, g, b, tm=512):
    n = x.shape[0]
    tm = _tile(n, tm)
    gw = B_HEADS_PER_GROUP * LANES
    row = lambda i: (i, 0)
    fixed = lambda i: (0, 0)
    return pl.pallas_call(
        _post_b_kernel,
        grid=(n // tm,),
        in_specs=[pl.BlockSpec((tm, D_MODEL), row)] + [pl.BlockSpec((tm, gw), row)] * 6
        + [pl.BlockSpec((3 * gw, D_MODEL), fixed), pl.BlockSpec((1, D_MODEL), fixed),
           pl.BlockSpec((1, D_MODEL), fixed)],
        out_specs=pl.BlockSpec((tm, D_MODEL), row),
        out_shape=jax.ShapeDtypeStruct((n, D_MODEL), F32),
        compiler_params=_params("parallel"),
        name="post_b",
    )(x, *os_, *ls_, w, g, b)


def _post_c_kernel(x_ref, ys_ref, d_ref, w_ref, g_ref, b_ref, y_ref):
    x = x_ref[...]
    z = jax.nn.gelu(ys_ref[...] + d_ref[...] * x).astype(BF)
    h = _dot(z, w_ref[...])
    hh = h[:, :D_MODEL] * jax.nn.sigmoid(h[:, D_MODEL:])
    y_ref[...] = _ln_rows(ALPHA * x + hh, g_ref[...], b_ref[...])


def _post_c(x, ys, d, w, g, b, tm=512):
    n = x.shape[0]
    tm = _tile(n, tm)
    row = lambda i: (i, 0)
    fixed = lambda i: (0, 0)
    return pl.pallas_call(
        _post_c_kernel,
        grid=(n // tm,),
        in_specs=[pl.BlockSpec((tm, D_MODEL), row), pl.BlockSpec((tm, D_MODEL), row),
                  pl.BlockSpec((1, D_MODEL), fixed), pl.BlockSpec((D_MODEL, 2 * D_MODEL), fixed),
                  pl.BlockSpec((1, D_MODEL), fixed), pl.BlockSpec((1, D_MODEL), fixed)],
        out_specs=pl.BlockSpec((tm, D_MODEL), row),
        out_shape=jax.ShapeDtypeStruct((n, D_MODEL), F32),
        compiler_params=_params("parallel"),
        name="post_c",
    )(x, ys, d, w, g, b)


_A_QK_TILES = A_HEADS + A_KV_HEADS
_A_COLS = (_A_QK_TILES + A_KV_HEADS) * LANES


def _rope_partner():
    d = np.arange(A_HEAD_DIM)
    e = d % (A_HEAD_DIM // 2)
    lo = e < A_HEAD_DIM // 4
    return np.where(lo, d + A_HEAD_DIM // 4, d - A_HEAD_DIM // 4), np.where(lo, -1.0, 1.0).astype(np.float32)


def _prep_a(w_qkv, q_gain, k_gain, w_o):
    partner, sign = _rope_partner()
    nqk = _A_QK_TILES * A_HEAD_DIM
    wqk = w_qkv[:, :nqk].reshape(D_MODEL, _A_QK_TILES, A_HEAD_DIM)
    wsw = wqk[:, :, partner] * sign
    wqk = jnp.concatenate([wqk, wsw], -1).reshape(D_MODEL, _A_QK_TILES * LANES)
    wv = w_qkv[:, nqk:].reshape(D_MODEL, A_KV_HEADS, A_HEAD_DIM)
    wv = jnp.concatenate([wv, jnp.zeros_like(wv)], -1).reshape(D_MODEL, A_KV_HEADS * LANES)
    w = jnp.concatenate([wqk, wv], 1).astype(BF)
    gq = jnp.concatenate([q_gain, q_gain[partner]]) * (A_HEAD_DIM ** -0.5 * 0.5 * LOG2E)
    gk = jnp.concatenate([k_gain, k_gain[partner]])
    gains = jnp.stack([gq, gk], 0)
    wo = w_o.reshape(A_HEADS, A_HEAD_DIM, D_MODEL)
    wo = jnp.concatenate([wo, jnp.zeros_like(wo)], 1).reshape(A_HEADS * LANES, D_MODEL).astype(BF)
    return w, gains, wo


def _rope_table(s):
    pos = jnp.arange(s)
    rows, cols = (pos // GRID_W).astype(F32), (pos % GRID_W).astype(F32)
    half = A_HEAD_DIM // 2
    freqs = ROPE_BASE ** (-jnp.arange(0, half, 2, dtype=F32) / half)
    ang_r = rows[:, None] * freqs
    ang_c = cols[:, None] * freqs
    ang = jnp.concatenate([ang_r, ang_r, ang_c, ang_c], -1)
    return jnp.concatenate([jnp.cos(ang), jnp.sin(ang)], -1)


def _proj_a_kernel(x_ref, w_ref, cs_ref, g_ref, o_ref):
    y = _dot(x_ref[...].astype(BF), w_ref[...])
    cs = cs_ref[...]
    gq = cs * g_ref[0:1, :]
    gk = cs * g_ref[1:2, :]
    for h in range(_A_QK_TILES):
        t = y[:, LANES * h:LANES * (h + 1)]
        r = lax.rsqrt(jnp.sum(t * t, -1, keepdims=True) * (1.0 / LANES) + RMS_EPS)
        e = t * r * (gq if h < A_HEADS else gk)
        o_ref[:, LANES * h:LANES * (h + 1)] = (e + pltpu.roll(e, LANES // 2, 1)).astype(BF)
    o_ref[:, _A_QK_TILES * LANES:] = _with_ones_lane(y[:, _A_QK_TILES * LANES:]).astype(BF)


def _proj_a(x, w, cs, gains, s, tm=512):
    n = x.shape[0]
    tm = _tile(s, tm)
    per = s // tm
    return pl.pallas_call(
        _proj_a_kernel,
        grid=(n // tm,),
        in_specs=[pl.BlockSpec((tm, D_MODEL), lambda i: (i, 0)),
                  pl.BlockSpec((D_MODEL, _A_COLS), lambda i: (0, 0)),
                  pl.BlockSpec((tm, LANES), lambda i: (i % per, 0)),
                  pl.BlockSpec((2, LANES), lambda i: (0, 0))],
        out_specs=pl.BlockSpec((tm, _A_COLS), lambda i: (i, 0)),
        out_shape=jax.ShapeDtypeStruct((n, _A_COLS), BF),
        compiler_params=_params("parallel"),
        name="proj_a",
    )(x, w, cs, gains)


def _softmax_step(s, v, m_ref, acc_ref):
    tk = s.shape[1]
    m_old = m_ref[...]
    m_new = jnp.maximum(m_old, jnp.max(s, -1, keepdims=True))
    p = jnp.concatenate([jnp.exp2(s[:, LANES * c:LANES * (c + 1)] - m_new).astype(BF) for c in range(tk // LANES)],
                        axis=1)
    acc_ref[...] = jnp.exp2(m_old - m_new) * acc_ref[...] + _dot(p, v)
    m_ref[...] = m_new


def _softmax_init(m_ref, acc_ref):
    m_ref[...] = jnp.full(m_ref.shape, -jnp.inf, F32)
    acc_ref[...] = jnp.zeros_like(acc_ref)


def _flash_a_kernel(q_ref, k_ref, v_ref, o_ref, m_ref, acc_ref, *, tk, nk):
    tq = q_ref.shape[0]
    rep = A_HEADS // A_KV_HEADS
    q = jnp.concatenate([q_ref[:, LANES * r:LANES * (r + 1)] for r in range(rep)], axis=0)
    _softmax_init(m_ref, acc_ref)

    def body(j, carry):
        off = pl.multiple_of(j * tk, tk)
        _softmax_step(_dot_nt(q, k_ref[pl.ds(off, tk), :]), v_ref[pl.ds(off, tk), :], m_ref, acc_ref)
        return carry

    lax.fori_loop(0, nk, body, 0, unroll=2)
    acc = acc_ref[...]
    o = acc / acc[:, _ONE_LANE:_ONE_LANE + 1]
    for r in range(rep):
        o_ref[:, LANES * r:LANES * (r + 1)] = o[r * tq:(r + 1) * tq].astype(BF)


def _flash_a(qkv, tq=256, tk=1024):
    b, s, _ = qkv.shape
    tq, tk = _tile(s, tq), _tile(s, tk)
    rep = A_HEADS // A_KV_HEADS
    gw = rep * LANES
    return pl.pallas_call(
        functools.partial(_flash_a_kernel, tk=tk, nk=s // tk),
        grid=(b, A_KV_HEADS, s // tq),
        in_specs=[pl.BlockSpec((None, tq, gw), lambda bi, g, i: (bi, i, g)),
                  pl.BlockSpec((None, s, LANES), lambda bi, g, i: (bi, 0, A_HEADS + g)),
                  pl.BlockSpec((None, s, LANES), lambda bi, g, i: (bi, 0, _A_QK_TILES + g))],
        out_specs=pl.BlockSpec((None, tq, gw), lambda bi, g, i: (bi, i, g)),
        out_shape=jax.ShapeDtypeStruct((b, s, A_HEADS * LANES), BF),
        scratch_shapes=[pltpu.VMEM((rep * tq, LANES), F32)] * 2,
        compiler_params=_params("parallel", "parallel", "arbitrary"),
        name="flash_a",
    )(qkv, qkv, qkv)


def _bucket(rel):
    half = REL_BUCKETS // 2
    max_exact = half // 2
    n = jnp.abs(rel)
    large = max_exact + (jnp.log(jnp.maximum(n, 1).astype(F32) / max_exact)
                         / math.log(REL_MAX_DIST / max_exact) * (half - max_exact)).astype(I32)
    large = jnp.minimum(large, half - 1)
    return jnp.where(rel > 0, half, 0) + jnp.where(n < max_exact, n, large)


_B_TILES = 3 * len(B_PATTERNS) * B_HEADS_PER_GROUP
_B_GW = B_HEADS_PER_GROUP * LANES
_B_T = 256


def _pad_heads(w, heads, dim):
    w = w.reshape(w.shape[0], heads, dim)
    return jnp.concatenate([w, jnp.zeros((w.shape[0], heads, LANES - dim), w.dtype)], -1).reshape(
        w.shape[0], heads * LANES)


def _prep_b(w_qkv, w_o):
    nh = len(B_PATTERNS) * B_HEADS_PER_GROUP
    c = nh * B_HEAD_DIM
    wq = _pad_heads(w_qkv[:, :c] * (B_HEAD_DIM ** -0.5), nh, B_HEAD_DIM)
    wk = _pad_heads(w_qkv[:, c:2 * c], nh, B_HEAD_DIM)
    wv = _pad_heads(w_qkv[:, 2 * c:], nh, B_HEAD_DIM)
    w = jnp.concatenate([wq, wk, wv], 1).astype(BF)
    wo = w_o.reshape(nh, B_HEAD_DIM, D_MODEL)
    wo = jnp.concatenate([wo, jnp.zeros_like(wo)], 1).reshape(nh * LANES, D_MODEL).astype(BF)
    return w, wo


def _toeplitz(vec, t):
    flat = jnp.tile(vec, (1,) * (vec.ndim - 1) + (t,))[..., :t * (2 * t - 1)]
    return flat.reshape(vec.shape[:-1] + (t, 2 * t - 1))[..., :t]


def _band_tiles(g, t):
    window, dil = B_PATTERNS[g]
    return pl.cdiv((window // (2 * dil)) * dil, t)


def _band_bias(rel_bias, g, t):
    window, dil = B_PATTERNS[g]
    reach = (window // (2 * dil)) * dil
    n = _band_tiles(g, t)
    rel0 = (jnp.arange(2 * t) + t) % (2 * t) - t
    rel = jnp.arange(-n, n + 1)[:, None] * t + rel0[None, :]
    bias = rel_bias[_bucket(rel)][:, :, g * B_HEADS_PER_GROUP:(g + 1) * B_HEADS_PER_GROUP]
    bias = jnp.where(((rel % dil == 0) & (jnp.abs(rel) <= reach))[:, :, None], bias, NEG)
    return _toeplitz(jnp.transpose(bias, (2, 0, 1)).astype(F32), t)


def _band_kernel(q_ref, k_ref, v_ref, bias_ref, o_ref, lse_ref, *, n, nk):
    t = q_ref.shape[0]
    i = pl.program_id(2)
    q = q_ref[...]
    logits, offs = [], []
    for o in range(-n, n + 1):
        j = i + o
        off = pl.multiple_of(jnp.clip(j, 0, nk - 1) * t, t)
        s = _dot_nt(q, k_ref[pl.ds(off, t), :]) + bias_ref[o + n]
        logits.append(jnp.where((j >= 0) & (j < nk), s, NEG))
        offs.append(off)
    m = logits[0].max(-1, keepdims=True)
    for s in logits[1:]:
        m = jnp.maximum(m, s.max(-1, keepdims=True))
    l = jnp.zeros((t, 1), F32)
    acc = jnp.zeros((t, LANES), F32)
    for s, off in zip(logits, offs):
        p = jnp.exp(s - m)
        l = l + jnp.sum(p, -1, keepdims=True)
        acc = acc + _dot(p.astype(BF), v_ref[pl.ds(off, t), :])
    o_ref[...] = (acc / l).astype(BF)
    lse_ref[...] = jnp.broadcast_to(m + jnp.log(l), (t, LANES))


def _band_attention(qkv, bias, g, t):
    b, s, _ = qkv.shape
    nh = len(B_PATTERNS) * B_HEADS_PER_GROUP
    n = _band_tiles(g, t)
    spec = lambda base: pl.BlockSpec((None, s, LANES), lambda bi, h, i: (bi, 0, base + g * B_HEADS_PER_GROUP + h))
    out_spec = pl.BlockSpec((None, t, LANES), lambda bi, h, i: (bi, i, h))
    o, lse = pl.pallas_call(
        functools.partial(_band_kernel, n=n, nk=s // t),
        grid=(b, B_HEADS_PER_GROUP, s // t),
        in_specs=[pl.BlockSpec((None, t, LANES), lambda bi, h, i: (bi, i, g * B_HEADS_PER_GROUP + h)),
                  spec(nh), spec(2 * nh),
                  pl.BlockSpec((None, 2 * n + 1, t, t), lambda bi, h, i: (h, 0, 0, 0))],
        out_specs=[out_spec, out_spec],
        out_shape=[jax.ShapeDtypeStruct((b, s, _B_GW), BF), jax.ShapeDtypeStruct((b, s, _B_GW), F32)],
        compiler_params=_params("parallel", "parallel", "arbitrary"),
        name="band_%d" % g,
    )(qkv, qkv, qkv, bias)
    return o.reshape(b * s, _B_GW), lse.reshape(b * s, _B_GW)


def _prep_c(lam_re, lam_im, log_dt, b_re, b_im, c_re, c_im):
    hp = lax.Precision.HIGHEST
    L, P, C = C_CHUNK, C_STATE, C_GROUP
    lr, li = lam_re.astype(F32), lam_im.astype(F32)
    dt = jnp.exp(log_dt.astype(F32))[..., None]
    mag = jnp.exp(lr * dt)
    ar, ai = mag * jnp.cos(li * dt), mag * jnp.sin(li * dt)
    den = lr * lr + li * li
    zr = ((ar - 1.0) * lr + ai * li) / den
    zi = (ai * lr - (ar - 1.0) * li) / den
    br, bi = b_re.astype(F32), b_im.astype(F32)
    bbr = zr[..., None] * br - zi[..., None] * bi
    bbi = zr[..., None] * bi + zi[..., None] * br
    cr, ci = c_re.astype(F32), c_im.astype(F32)
    prs, pis = [jnp.ones_like(ar)], [jnp.zeros_like(ai)]
    for _ in range(L):
        pr_, pi_ = prs[-1], pis[-1]
        prs.append(ar * pr_ - ai * pi_)
        pis.append(ar * pi_ + ai * pr_)
    pr, pi = jnp.stack(prs, 0), jnp.stack(pis, 0)

    def lag(pr_k, pi_k):
        tr = pr_k[..., None] * bbr - pi_k[..., None] * bbi
        ti = pr_k[..., None] * bbi + pi_k[..., None] * bbr
        return (jnp.einsum('dgop,kdgpi->kdgoi', cr, tr, precision=hp)
                - jnp.einsum('dgop,kdgpi->kdgoi', ci, ti, precision=hp))

    kern = lag(pr[:L], pi[:L])
    jj = np.arange(L)[:, None]
    ii = np.arange(L)[None, :]
    kf = kern[:, 0][np.clip(ii - jj, 0, L - 1)] * jnp.asarray((ii >= jj)[:, :, None, None, None], F32)
    kb = kern[:, 1][np.clip(jj - ii, 0, L - 1)] * jnp.asarray((jj >= ii)[:, :, None, None, None], F32)
    m_intra = jnp.transpose(kf + kb, (2, 0, 4, 1, 3)).reshape(C_N_GROUPS, L * C, L * C)

    def summ(d, powers):
        pr_k, pi_k = pr[powers, d], pi[powers, d]
        sr = pr_k[..., None] * bbr[d] - pi_k[..., None] * bbi[d]
        si = pr_k[..., None] * bbi[d] + pi_k[..., None] * bbr[d]
        s = jnp.concatenate([sr, si], 2)
        return jnp.transpose(s, (1, 0, 3, 2)).reshape(C_N_GROUPS, L * C, 2 * P)

    w_sum = jnp.concatenate([summ(0, np.arange(L - 1, -1, -1)), summ(1, np.arange(L))], -1)

    def outw(d, powers):
        pr_k, pi_k = pr[powers, d], pi[powers, d]
        wr = cr[d][None] * pr_k[:, :, None, :] - ci[d][None] * pi_k[:, :, None, :]
        wi = -(cr[d][None] * pi_k[:, :, None, :] + ci[d][None] * pr_k[:, :, None, :])
        w = jnp.concatenate([wr, wi], -1)
        return jnp.transpose(w, (1, 3, 0, 2)).reshape(C_N_GROUPS, 2 * P, L * C)

    w_state = jnp.concatenate([outw(0, np.arange(1, L + 1)), outw(1, np.arange(L, 0, -1))], 1)

    nq = C_N_GROUPS // _C_GL
    npair = _C_GL // 2
    blk = L * _C_GL * C
    lane = jnp.arange(LANES)
    gsel = (lane[None, :] // C == jnp.arange(_C_GL)[:, None])
    src = jnp.tile(m_intra.astype(BF).reshape(nq, _C_GL, L, C, L, C), (1, 1, 1, 1, 1, _C_GL))
    src = jnp.where(gsel[None, :, None, None, None, :], src, 0)
    w_intra = jnp.transpose(src, (0, 2, 1, 3, 4, 5)).reshape(nq, blk, blk)
    psel = (lane[None, :] // P == jnp.arange(2)[:, None])
    ksel = jnp.eye(npair, dtype=bool)
    src = jnp.tile(w_sum.astype(BF).reshape(nq, npair, 2, L, C, 1, 4, P), (1, 1, 1, 1, 1, npair, 1, 2))
    src = jnp.where(psel[None, None, :, None, None, None, None, :]
                    & ksel[None, :, None, None, None, :, None, None], src, 0)
    w_sum = jnp.transpose(src, (0, 3, 1, 2, 4, 5, 6, 7)).reshape(nq, blk, _C_GL * 4 * P)
    osel = (lane[None, None, :] // C
            == (2 * jnp.arange(npair)[:, None, None] + jnp.arange(2)[None, :, None]))
    src = jnp.tile(w_state.astype(BF).reshape(nq, npair, 2, 4, P, L, C), (1, 1, 1, 1, 1, 1, _C_GL))
    src = jnp.where(osel[None, :, :, None, None, None, :], src, 0)
    w_state = jnp.transpose(src, (0, 1, 3, 2, 4, 5, 6)).reshape(nq, _C_GL * 4 * P, blk)
    dec = jnp.stack([pr[L, 0], pi[L, 0], pr[L, 1], pi[L, 1]], 0).reshape(4, nq * npair, 2 * P)
    dec = jnp.broadcast_to(jnp.transpose(dec, (1, 0, 2))[:, :, None, :], (nq * npair, 4, 8, 2 * P))
    return w_sum.astype(BF), w_intra.astype(BF), w_state.astype(BF), dec


def _chunk_rows(x_ref, nb, rc):
    L = C_CHUNK
    rows = [jnp.concatenate([x_ref[b, pl.ds(j, rc, stride=L), :].astype(BF) for j in range(L)], axis=1)
            for b in range(nb)]
    return jnp.concatenate(rows, axis=0)


def _s5_sum_kernel(x_ref, w_ref, s_ref, *, nb, rc):
    s = _dot(_chunk_rows(x_ref, nb, rc), w_ref[...])
    for b in range(nb):
        for t in range(s_ref.shape[0]):
            s_ref[t, pl.ds(b, rc, stride=nb), :] = s[b * rc:(b + 1) * rc, LANES * t:LANES * (t + 1)]


def _s5_scan_kernel(s_ref, dec_ref, e_ref, *, nc, nb):
    units = s_ref.shape[0] // 4
    dec = [[dec_ref[u, k, 0:nb, :] for k in range(4)] for u in range(units)]

    def body(t, carry):
        rf = pl.ds(pl.multiple_of(t * nb, nb), nb)
        rb = pl.ds(pl.multiple_of((nc - 1 - t) * nb, nb), nb)
        new = []
        for u in range(units):
            fr, fi, br, bi = carry[4 * u:4 * u + 4]
            arf, aif, arb, aib = dec[u]
            e_ref[4 * u, rf, :] = fr
            e_ref[4 * u + 1, rf, :] = fi
            e_ref[4 * u + 2, rb, :] = br
            e_ref[4 * u + 3, rb, :] = bi
            new += [fr * arf - fi * aif + s_ref[4 * u, rf, :], fi * arf + fr * aif + s_ref[4 * u + 1, rf, :],
                    br * arb - bi * aib + s_ref[4 * u + 2, rb, :], bi * arb + br * aib + s_ref[4 * u + 3, rb, :]]
        return tuple(new)

    z = jnp.zeros((nb, LANES), F32)
    lax.fori_loop(0, nc, body, (z,) * (4 * units))


def _s5_out_kernel(x_ref, e_ref, wi_ref, wc_ref, y_ref, *, nb, rc):
    L = C_CHUNK
    half = pl.program_id(2)
    xc = _chunk_rows(x_ref, nb, rc)
    ec = jnp.concatenate(
        [jnp.concatenate([e_ref[t, pl.ds(b, rc, stride=nb), :].astype(BF) for t in range(e_ref.shape[0])], axis=1)
         for b in range(nb)], axis=0)
    y = _dot(xc, wi_ref[...]) + _dot(ec, wc_ref[...])
    for b in range(nb):
        for ii in range(L // 2):
            y_ref[b, pl.ds(half * (L // 2) + ii, rc, stride=L), :] = y[b * rc:(b + 1) * rc, LANES * ii:LANES * (ii + 1)]


def _s5(x, b, s, w_sum, w_intra, w_state, dec):
    L = C_CHUNK
    nc = s // L
    nq = D_MODEL // LANES
    nsl = w_sum.shape[2] // LANES
    blk = L * LANES
    rc = _tile(nc, max(8, 512 // b))
    x3 = x.reshape(b, s, D_MODEL)
    x_spec = pl.BlockSpec((b, rc * L, LANES), lambda q, c, *_: (0, c, q))
    sums = pl.pallas_call(
        functools.partial(_s5_sum_kernel, nb=b, rc=rc),
        grid=(nq, nc // rc),
        in_specs=[x_spec, pl.BlockSpec((None, blk, nsl * LANES), lambda q, c: (q, 0, 0))],
        out_specs=pl.BlockSpec((nsl, rc * b, LANES), lambda q, c: (q, c, 0)),
        out_shape=jax.ShapeDtypeStruct((nq * nsl, nc * b, LANES), F32),
        compiler_params=_params("parallel", "arbitrary"),
        name="s5_sum",
    )(x3, w_sum)
    upb = 2
    ent = pl.pallas_call(
        functools.partial(_s5_scan_kernel, nc=nc, nb=b),
        grid=(nq * nsl // (4 * upb),),
        in_specs=[pl.BlockSpec((4 * upb, nc * b, LANES), lambda i: (i, 0, 0)),
                  pl.BlockSpec((upb, 4, 8, LANES), lambda i: (i, 0, 0, 0))],
        out_specs=pl.BlockSpec((4 * upb, nc * b, LANES), lambda i: (i, 0, 0)),
        out_shape=jax.ShapeDtypeStruct((nq * nsl, nc * b, LANES), F32),
        compiler_params=_params("parallel"),
        name="s5_scan",
    )(sums, dec)
    y = pl.pallas_call(
        functools.partial(_s5_out_kernel, nb=b, rc=rc),
        grid=(nq, nc // rc, 2),
        in_specs=[x_spec,
                  pl.BlockSpec((nsl, rc * b, LANES), lambda q, c, h: (q, c, 0)),
                  pl.BlockSpec((None, blk, blk // 2), lambda q, c, h: (q, 0, h)),
                  pl.BlockSpec((None, nsl * LANES, blk // 2), lambda q, c, h: (q, 0, h))],
        out_specs=pl.BlockSpec((b, rc * L, LANES), lambda q, c, h: (0, c, q)),
        out_shape=jax.ShapeDtypeStruct((b, s, D_MODEL), F32),
        compiler_params=_params("parallel", "arbitrary", "arbitrary"),
        name="s5_out",
    )(x3, ent, w_intra, w_state)
    return y.reshape(b * s, D_MODEL)


_D_T = 512


def _prep_d(w_qkv, norm_gain, w_o, lambda_init):
    qk_w = D_HEADS * 2 * D_QK_DIM
    wq = _pad_heads(w_qkv[:, :qk_w] * (D_QK_DIM ** -0.5 * LOG2E), D_HEADS, 2 * D_QK_DIM)
    wk = _pad_heads(w_qkv[:, qk_w:2 * qk_w], D_HEADS, 2 * D_QK_DIM)
    wv = _pad_heads(w_qkv[:, 2 * qk_w:], D_HEADS, D_V_DIM)
    w = jnp.concatenate([wq, wk, wv], 1).astype(BF)
    gain = jnp.concatenate([norm_gain * (1.0 - lambda_init), jnp.zeros((LANES - D_V_DIM,), F32)])[None, :]
    wo = w_o.reshape(D_HEADS, D_V_DIM, D_MODEL)
    wo = jnp.concatenate([wo, jnp.zeros_like(wo)], 1).reshape(D_HEADS * LANES, D_MODEL).astype(BF)
    return w, gain, wo


def _diff_bias(rel_bias, t):
    rel0 = (jnp.arange(2 * t) + t) % (2 * t) - t
    rel = jnp.arange(-2, 3)[:, None] * t + rel0[None, :]
    return jnp.transpose(rel_bias[_bucket(rel)] * LOG2E, (2, 0, 1)).astype(F32)


def _flash_d_kernel(q_ref, k_ref, v_ref, vec_ref, lam_ref, gain_ref, o_ref, m_ref, acc_ref, bias_ref, *, kt, nk,
                    lambda_init):
    t = q_ref.shape[0]
    i = pl.program_id(2)

    @pl.when(i == 0)
    def _():
        for d in range(bias_ref.shape[0]):
            full = jnp.broadcast_to(vec_ref[d:d + 1, :], (t, 2 * t))
            bias_ref[d] = pltpu.roll(full, 0, 1, stride=1, stride_axis=0)[:, :t]

    qv = q_ref[...].astype(F32)
    lane = lax.broadcasted_iota(I32, (t, LANES), 1)
    q0 = jnp.where(lane < D_QK_DIM, qv, 0.0).astype(BF)
    q1 = jnp.where((lane >= D_QK_DIM) & (lane < 2 * D_QK_DIM), qv, 0.0).astype(BF)
    q = jnp.concatenate([q0, q1], axis=0)
    _softmax_init(m_ref, acc_ref)

    def body(j, carry):
        off = pl.multiple_of(j * (kt * t), kt * t)
        bias = jnp.concatenate([bias_ref[jnp.clip(j * kt + c - i, -2, 2) + 2] for c in range(kt)], axis=1)
        s = _dot_nt(q, k_ref[pl.ds(off, kt * t), :])
        s = (s.reshape(2, t, kt * t) + bias[None]).reshape(2 * t, kt * t)
        _softmax_step(s, v_ref[pl.ds(off, kt * t), :], m_ref, acc_ref)
        return carry

    lax.fori_loop(0, nk, body, 0, unroll=2)
    lf = lam_ref[...]
    lam = (jnp.exp(jnp.sum(lf[0:1] * lf[1:2], keepdims=True))
           - jnp.exp(jnp.sum(lf[2:3] * lf[3:4], keepdims=True)) + lambda_init)
    acc = acc_ref[...]
    on = acc / acc[:, _ONE_LANE:_ONE_LANE + 1]
    o = jnp.where(lane < D_V_DIM, on[:t] - lam * on[t:], 0.0)
    ms = jnp.sum(o * o, -1, keepdims=True) * (1.0 / D_V_DIM)
    o_ref[...] = (o * lax.rsqrt(ms + RMS_EPS) * gain_ref[...]).astype(BF)


def _flash_d(qkv, bias, lam, gain, lambda_init):
    b, s, _ = qkv.shape
    t = bias.shape[-1] // 2
    kt = 2 if s % (2 * t) == 0 else 1
    return pl.pallas_call(
        functools.partial(_flash_d_kernel, kt=kt, nk=s // (kt * t), lambda_init=lambda_init),
        grid=(b, D_HEADS, s // t),
        in_specs=[pl.BlockSpec((None, t, LANES), lambda bi, h, i: (bi, i, h)),
                  pl.BlockSpec((None, s, LANES), lambda bi, h, i: (bi, 0, D_HEADS + h)),
                  pl.BlockSpec((None, s, LANES), lambda bi, h, i: (bi, 0, 2 * D_HEADS + h)),
                  pl.BlockSpec((None, 5, 2 * t), lambda bi, h, i: (h, 0, 0)),
                  pl.BlockSpec((4, D_QK_DIM), lambda bi, h, i: (0, 0)),
                  pl.BlockSpec((1, LANES), lambda bi, h, i: (0, 0))],
        out_specs=pl.BlockSpec((None, t, LANES), lambda bi, h, i: (bi, i, h)),
        out_shape=jax.ShapeDtypeStruct((b, s, D_HEADS * LANES), BF),
        scratch_shapes=[pltpu.VMEM((2 * t, LANES), F32)] * 2 + [pltpu.VMEM((5, t, t), F32)],
        compiler_params=_params("parallel", "parallel", "arbitrary"),
        name="flash_d",
    )(qkv, qkv, qkv, bias, lam, gain)


def _cross_kernel(x_ref, kv_ref, wq_ref, wo_ref, g_ref, b_ref, wr_ref, y_ref, ybf_ref, aff_ref):
    x = x_ref[...]
    tm = x.shape[0]
    q = (_dot(x.astype(BF), wq_ref[...]) * (X_HEAD_DIM ** -0.5)).astype(BF)
    outs = []
    for h in range(X_HEADS):
        sl = slice(X_HEAD_DIM * h, X_HEAD_DIM * (h + 1))
        s = _dot_nt(q[:, sl], kv_ref[:, sl])
        p = jnp.exp(s - jnp.max(s, -1, keepdims=True))
        l = jnp.sum(p, -1, keepdims=True)
        vh = kv_ref[:, D_MODEL + X_HEAD_DIM * h:D_MODEL + X_HEAD_DIM * (h + 1)]
        outs.append((_dot(p.astype(BF), vh) / l).astype(BF))
    o = jnp.concatenate(outs, axis=1)
    y = _ln_rows(ALPHA * x + _dot(o, wo_ref[...]), g_ref[...], b_ref[...])
    y_ref[...] = y
    yh = y.astype(BF)
    ybf_ref[...] = yh
    yl = (y - yh.astype(F32)).astype(BF)
    wr = wr_ref[...]
    wh = wr.astype(BF)
    wl = (wr - wh.astype(F32)).astype(BF)
    lg = _dot_nt(wh, yh) + _dot_nt(wh, yl) + _dot_nt(wl, yh)
    e = jnp.exp(lg - jnp.max(lg, 0, keepdims=True))
    aff = e / jnp.sum(e, 0, keepdims=True)
    for c in range(tm // LANES):
        aff_ref[c] = aff[:, LANES * c:LANES * (c + 1)]


def _cross(x, kv, wq, wo, g, b, wr_t, s, mem_len, tm=512):
    n = x.shape[0]
    tm = _tile(s, tm)
    per = s // tm
    fixed = lambda i: (0, 0)
    return pl.pallas_call(
        _cross_kernel,
        grid=(n // tm,),
        in_specs=[pl.BlockSpec((tm, D_MODEL), lambda i: (i, 0)),
                  pl.BlockSpec((mem_len, 2 * D_MODEL), lambda i: (i // per, 0)),
                  pl.BlockSpec((D_MODEL, D_MODEL), fixed), pl.BlockSpec((D_MODEL, D_MODEL), fixed),
                  pl.BlockSpec((1, D_MODEL), fixed), pl.BlockSpec((1, D_MODEL), fixed),
                  pl.BlockSpec((N_EXPERTS, D_MODEL), fixed)],
        out_specs=[pl.BlockSpec((tm, D_MODEL), lambda i: (i, 0)),
                   pl.BlockSpec((tm, D_MODEL), lambda i: (i, 0)),
                   pl.BlockSpec((tm // LANES, N_EXPERTS, LANES), lambda i: (i, 0, 0))],
        out_shape=[jax.ShapeDtypeStruct((n, D_MODEL), F32),
                   jax.ShapeDtypeStruct((n, D_MODEL), BF),
                   jax.ShapeDtypeStruct((n // LANES, N_EXPERTS, LANES), F32)],
        compiler_params=_params("parallel"),
        name="cross",
    )(x, kv, wq, wo, g, b, wr_t)


def _select_kernel(a_ref, pos_ref, st_ref, *, k, nbits):
    nt = a_ref.shape[0]
    shape = (nt, N_EXPERTS, LANES)
    kf = float(k)

    def keys():
        return lax.bitcast_convert_type(a_ref[...], I32)

    def count(mask):
        c = jnp.sum(jnp.where(mask, 1.0, 0.0), axis=0, keepdims=True)
        return jnp.sum(c, axis=2, keepdims=True)

    def value_step(it, thr):
        cand = thr | jnp.left_shift(jnp.int32(1), 30 - it)
        return jnp.where(count(keys() >= cand) >= kf, cand, thr)

    thr = lax.fori_loop(0, 31, value_step, jnp.zeros((1, N_EXPERTS, 1), I32))
    need = kf - count(keys() > thr)
    idx = lax.broadcasted_iota(I32, shape, 0) * LANES + lax.broadcasted_iota(I32, shape, 2)

    def index_step(it, ithr):
        cand = ithr | jnp.left_shift(jnp.int32(1), nbits - 1 - it)
        return jnp.where(count((keys() == thr) & (idx < cand)) < need, cand, ithr)

    ithr = lax.fori_loop(0, nbits, index_step, jnp.zeros((1, N_EXPERTS, 1), I32))
    thr2, ithr2 = thr[0], ithr[0]
    upper = jnp.where(lax.broadcasted_iota(I32, (LANES, LANES), 0) <= lax.broadcasted_iota(I32, (LANES, LANES), 1),
                      1.0, 0.0).astype(BF)
    lane = lax.broadcasted_iota(I32, (N_EXPERTS, LANES), 1)

    def tile_step(j, carry):
        kj = lax.bitcast_convert_type(a_ref[j], I32)
        sel = (kj > thr2) | ((kj == thr2) & (j * LANES + lane <= ithr2))
        m = jnp.where(sel, 1.0, 0.0)
        inc = _dot(m.astype(BF), upper)
        pos_ref[j] = jnp.where(sel, inc - m + carry, -1.0).astype(I32)
        st_ref[j] = jnp.broadcast_to(carry, (N_EXPERTS, LANES)).astype(I32)
        return carry + inc[:, LANES - 1:LANES]

    lax.fori_loop(0, nt, tile_step, jnp.zeros((N_EXPERTS, 1), F32))


def _select(aff3, k):
    nt = aff3.shape[0]
    nbits = max(1, int(math.ceil(math.log2(nt * LANES))))
    shp = jax.ShapeDtypeStruct(aff3.shape, I32)
    return pl.pallas_call(
        functools.partial(_select_kernel, k=k, nbits=nbits),
        out_shape=[shp, shp],
        compiler_params=pltpu.CompilerParams(vmem_limit_bytes=VMEM_LIMIT),
        name="select",
    )(aff3)


_GATHER_ROWS = LANES + BF16_ROWS
_GATHER_ROWS_SMALL = 3 * BF16_ROWS


_MOE_GROUP = 2


def _moe_ffn_kernel(st_ref, x_ref, pos_ref, aff_ref, wg_ref, wu_ref, wd_ref, ye_ref, buf_ref, gate_ref, *, nb, ncf,
                    nch, sub, ck):
    ep = buf_ref.shape[0]
    grp = pl.program_id(0)
    t = pl.program_id(1)

    @pl.when(t == 0)
    def _():
        buf_ref[...] = jnp.zeros_like(buf_ref)
        gate_ref[...] = jnp.zeros_like(gate_ref)

    @pl.when(t < nb)
    def _():
        def place(s, k, e, base, nrows):
            rows = pl.ds(base, nrows)
            hit = lax.broadcasted_iota(I32, (nrows, LANES), 0) == pos_ref[s, pl.ds(e, 1), :] - base
            xs = x_ref[LANES * s:LANES * (s + 1), :]
            buf_ref[k, rows, :] = buf_ref[k, rows, :] + _dot(jnp.where(hit, 1.0, 0.0).astype(BF), xs).astype(BF)
            gate_ref[k, rows, :] = gate_ref[k, rows, :] + jnp.sum(
                jnp.where(hit, aff_ref[s, pl.ds(e, 1), :], 0.0), -1, keepdims=True)

        for s in range(sub):
            for k in range(ep):
                e = grp * ep + k
                st = st_ref[e, t * sub + s]
                base = pl.multiple_of((st // BF16_ROWS) * BF16_ROWS, BF16_ROWS)
                few = st_ref[e, t * sub + s + 1] - base <= _GATHER_ROWS_SMALL

                @pl.when(few)
                def _():
                    place(s, k, e, base, _GATHER_ROWS_SMALL)

                @pl.when(jnp.logical_not(few))
                def _():
                    place(s, k, e, base, _GATHER_ROWS)

    @pl.when(t >= nb)
    def _():
        k = (t - nb) // nch
        c = (t - nb) - k * nch

        @pl.when(c < ncf)
        def _():
            rows = pl.ds(pl.multiple_of(c * ck, ck), ck)
            xe = buf_ref[k, rows, :]
            h = (jax.nn.silu(_dot(xe, wg_ref[...])) * _dot(xe, wu_ref[...])).astype(BF)
            ye_ref[...] = (_dot(h, wd_ref[...]) * gate_ref[k, rows, :]).astype(BF)

        @pl.when(c >= ncf)
        def _():
            ye_ref[...] = jnp.zeros_like(ye_ref)


def _moe_ffn(starts, xbf, pos3, aff3, wg, wu, wd, cap, tb=1024):
    n = xbf.shape[0]
    tb = _tile(n, tb)
    nb = n // tb
    sub = tb // LANES
    ck = _tile(cap, 512)
    ncf = cap // ck
    nch = ncf + pl.cdiv(_WIN_ROWS, ck)
    dff = wg.shape[-1]
    ep = _MOE_GROUP
    blk = lambda g, t, st: (jnp.minimum(t, nb - 1), 0)
    blk3 = lambda g, t, st: (jnp.minimum(t, nb - 1), 0, 0)
    expert = lambda g, t: g * ep + jnp.clip((t - nb) // nch, 0, ep - 1)
    wmap = lambda g, t, st: (expert(g, t), 0, 0)
    return pl.pallas_call(
        functools.partial(_moe_ffn_kernel, nb=nb, ncf=ncf, nch=nch, sub=sub, ck=ck),
        grid_spec=pltpu.PrefetchScalarGridSpec(
            num_scalar_prefetch=1,
            grid=(N_EXPERTS // ep, nb + ep * nch),
            in_specs=[pl.BlockSpec((tb, D_MODEL), blk),
                      pl.BlockSpec((sub, N_EXPERTS, LANES), blk3),
                      pl.BlockSpec((sub, N_EXPERTS, LANES), blk3),
                      pl.BlockSpec((None, D_MODEL, dff), wmap),
                      pl.BlockSpec((None, D_MODEL, dff), wmap),
                      pl.BlockSpec((None, dff, D_MODEL), wmap)],
            out_specs=pl.BlockSpec((None, ck, D_MODEL),
                                   lambda g, t, st: (expert(g, t), jnp.maximum(t - nb, 0) % nch, 0)),
            scratch_shapes=[pltpu.VMEM((ep, cap + _GATHER_ROWS, D_MODEL), BF),
                            pltpu.VMEM((ep, cap + _GATHER_ROWS, 1), F32)]),
        out_shape=jax.ShapeDtypeStruct((N_EXPERTS, nch * ck, D_MODEL), BF),
        compiler_params=_params("arbitrary", "arbitrary"),
        name="moe_ffn",
    )(starts, xbf, pos3, aff3, wg, wu, wd)


_WIN_STEP = 128
_COMB_TILES = 2
_WIN_ROWS = _WIN_STEP + (_COMB_TILES - 1) * LANES + _GATHER_ROWS


def _combine_kernel(st_ref, x_ref, pos_ref, g_ref, b_ref, *rest):
    ye_refs, y_ref = rest[:N_EXPERTS], rest[N_EXPERTS]
    j = pl.program_id(0)
    col = lax.broadcasted_iota(I32, (LANES, _GATHER_ROWS), 1)
    for u in range(_COMB_TILES):
        tok = slice(LANES * u, LANES * (u + 1))
        acc = None
        for e in range(N_EXPERTS):
            st = st_ref[e, j * _COMB_TILES + u]
            base = (st // BF16_ROWS) * BF16_ROWS
            win = (st_ref[e, j * _COMB_TILES] // _WIN_STEP) * _WIN_STEP
            sub = pl.multiple_of(base - win, BF16_ROWS)
            onehot = jnp.where(col == pos_ref[tok, e:e + 1] - base, 1.0, 0.0).astype(BF)
            d = _dot(onehot, ye_refs[e][0, pl.ds(sub, _GATHER_ROWS), :])
            acc = d if acc is None else acc + d
        y_ref[tok, :] = _ln_rows(ALPHA * x_ref[tok, :] + acc, g_ref[...], b_ref[...])


def _combine(starts, x, pos, g, b, ye):
    n = x.shape[0]
    assert ye.shape[1] >= EC_CAPACITY * n // N_EXPERTS + _WIN_ROWS
    tt = _COMB_TILES * LANES
    row = lambda j, st: (j, 0)
    fixed = lambda j, st: (0, 0)

    def ye_spec(e):
        return pl.BlockSpec((pl.Element(1), pl.Element(_WIN_ROWS), pl.Element(D_MODEL)),
                            lambda j, st: (e, (st[e, j * _COMB_TILES] // _WIN_STEP) * _WIN_STEP, 0))

    return pl.pallas_call(
        _combine_kernel,
        grid_spec=pltpu.PrefetchScalarGridSpec(
            num_scalar_prefetch=1,
            grid=(n // tt,),
            in_specs=[pl.BlockSpec((tt, D_MODEL), row),
                      pl.BlockSpec((tt, N_EXPERTS), row),
                      pl.BlockSpec((1, D_MODEL), fixed), pl.BlockSpec((1, D_MODEL), fixed)]
            + [ye_spec(e) for e in range(N_EXPERTS)],
            out_specs=pl.BlockSpec((tt, D_MODEL), row)),
        out_shape=jax.ShapeDtypeStruct((n, D_MODEL), F32),
        compiler_params=_params("arbitrary"),
        name="combine",
    )(starts, x, pos, g, b, *([ye] * N_EXPERTS))


def _moe(x, xg, aff3, wg, wu, wd, g, b):
    n = x.shape[0]
    cap = EC_CAPACITY * n // N_EXPERTS
    pos3, st3 = _select(aff3, cap)
    starts = jnp.concatenate([jnp.transpose(st3[:, :, 0]), jnp.full((N_EXPERTS, 1), cap, I32)], axis=1)
    ye = _moe_ffn(starts, xg, pos3, aff3, wg, wu, wd, cap)
    pos = jnp.transpose(pos3, (0, 2, 1)).reshape(n, N_EXPERTS)
    return _combine(starts, x, pos, g, b, ye)


def _prep_weights(p):
    w = {}
    w['a'] = [_prep_a(p['a_w_qkv'][j], p['a_q_gain'][j], p['a_k_gain'][j], p['a_w_o'][j])
              for j in range(p['a_w_qkv'].shape[0])]
    w['b'] = [_prep_b(p['b_w_qkv'][j], p['b_w_o'][j]) for j in range(p['b_w_qkv'].shape[0])]
    w['c'] = [_prep_c(p['c_lam_re'][j], p['c_lam_im'][j], p['c_log_dt'][j], p['c_b_re'][j], p['c_b_im'][j],
                      p['c_c_re'][j], p['c_c_im'][j]) + (p['c_d'][j][None, :], p['c_w_glu'][j].astype(BF))
              for j in range(p['c_lam_re'].shape[0])]
    w['d'] = []
    for j in range(p['d_w_qkv'].shape[0]):
        layer = N_MIXERS * j + 3
        lambda_init = 0.8 - 0.6 * math.exp(-0.3 * layer)
        w['d'].append(_prep_d(p['d_w_qkv'][j], p['d_norm_gain'][j], p['d_w_o'][j], lambda_init)
                      + (p['d_lam'][j].astype(F32), lambda_init))
    w['diff_bias'] = _diff_bias(p['rel_bias'], _D_T)
    w['x_w_q'] = p['x_w_q'].astype(BF)
    w['x_w_kv'] = p['x_w_kv'].astype(BF)
    w['x_w_o'] = p['x_w_o'].astype(BF)
    w['router_t'] = jnp.transpose(p['moe_w_router'], (0, 2, 1)).astype(F32)
    w['moe_w_gate'] = p['moe_w_gate'].astype(BF)
    w['moe_w_up'] = p['moe_w_up'].astype(BF)
    w['moe_w_down'] = p['moe_w_down'].astype(BF)
    return w


def _trunk(x, mem, p, w):
    b, s, _ = x.shape
    n = b * s
    mem_len = mem.shape[1]
    x = x.reshape(n, D_MODEL)
    mem2 = mem.reshape(b * mem_len, D_MODEL)
    ln_g, ln_b = p['ln_g'], p['ln_b']
    for i in range(DEPTH):
        m, j = i % N_MIXERS, i // N_MIXERS
        g0, b0 = ln_g[i, 0][None, :], ln_b[i, 0][None, :]
        if m == 0:
            wa, gains, wo = w['a'][j]
            qkv = _proj_a(x, wa, _rope_table(s), gains, s)
            o = _flash_a(qkv.reshape(b, s, _A_COLS))
            x = _post(x, o.reshape(n, A_HEADS * LANES), wo, g0, b0)
        elif m == 1:
            wb, wo = w['b'][j]
            qkv = _proj(x, wb).reshape(b, s, _B_TILES * LANES)
            os_, ls_ = [], []
            for g in range(len(B_PATTERNS)):
                t = _tile(s, _B_T)
                o, lse = _band_attention(qkv, _band_bias(p['rel_bias'], g, t), g, t)
                os_.append(o)
                ls_.append(lse)
            x = _post_b(x, os_, ls_, wo, g0, b0)
        elif m == 2:
            w_sum, w_intra, w_state, dec, dskip, wglu = w['c'][j]
            ys = _s5(x, b, s, w_sum, w_intra, w_state, dec)
            x = _post_c(x, ys, dskip, wglu, g0, b0)
        else:
            wd, gain, wo, lam, lambda_init = w['d'][j]
            qkv = _proj(x, wd, ones_from=2 * D_HEADS * LANES).reshape(b, s, 3 * D_HEADS * LANES)
            o = _flash_d(qkv, w['diff_bias'], lam, gain, lambda_init)
            x = _post(x, o.reshape(n, D_HEADS * LANES), wo, g0, b0)
        kv = _proj(mem2, w['x_w_kv'][i], tm=mem_len)
        x, xbf, aff3 = _cross(x, kv, w['x_w_q'][i], w['x_w_o'][i], ln_g[i, 1][None, :], ln_b[i, 1][None, :],
                              w['router_t'][i], s, mem_len)
        x = _moe(x, xbf, aff3, w['moe_w_gate'][i], w['moe_w_up'][i], w['moe_w_down'][i],
                 ln_g[i, 2][None, :], ln_b[i, 2][None, :])
    return x.reshape(b, s, D_MODEL)


def kernel(x_prompt, x_sample, mem_prompt, mem_sample, rel_bias, ln_g, ln_b, a_w_qkv, a_q_gain, a_k_gain, a_w_o, b_w_qkv, b_w_o, c_lam_re, c_lam_im, c_log_dt, c_b_re, c_b_im, c_c_re, c_c_im, c_d, c_w_glu, d_w_qkv, d_lam, d_norm_gain, d_w_o, x_w_q, x_w_kv, x_w_o, moe_w_router, moe_w_gate, moe_w_up, moe_w_down):
    p = dict(rel_bias=rel_bias, ln_g=ln_g, ln_b=ln_b,
             a_w_qkv=a_w_qkv, a_q_gain=a_q_gain, a_k_gain=a_k_gain, a_w_o=a_w_o,
             b_w_qkv=b_w_qkv, b_w_o=b_w_o,
             c_lam_re=c_lam_re, c_lam_im=c_lam_im, c_log_dt=c_log_dt, c_b_re=c_b_re, c_b_im=c_b_im,
             c_c_re=c_c_re, c_c_im=c_c_im, c_d=c_d, c_w_glu=c_w_glu,
             d_w_qkv=d_w_qkv, d_lam=d_lam, d_norm_gain=d_norm_gain, d_w_o=d_w_o,
             x_w_q=x_w_q, x_w_kv=x_w_kv, x_w_o=x_w_o,
             moe_w_router=moe_w_router, moe_w_gate=moe_w_gate, moe_w_up=moe_w_up, moe_w_down=moe_w_down)
    w = _prep_weights(p)
    return (_trunk(x_prompt, mem_prompt, p, w), _trunk(x_sample, mem_sample, p, w))
```

```python
import functools
import math

import numpy as np
import jax
import jax.numpy as jnp
from jax import lax
from jax.experimental import pallas as pl
from jax.experimental.pallas import tpu as pltpu

F32 = jnp.float32
BF = jnp.bfloat16
I32 = jnp.int32

D_MODEL = 1024
DEPTH = 4
GRID_W = 64
N_MIXERS = 4
LN_EPS = 1e-5
RMS_EPS = 1e-6
ALPHA = (2.0 * DEPTH) ** 0.25

A_HEADS = 16
A_KV_HEADS = 4
A_HEAD_DIM = 64
ROPE_BASE = 10000.0

B_PATTERNS = ((128, 1), (512, 4), (2048, 16))
B_HEADS_PER_GROUP = 4
B_HEAD_DIM = 64

C_GROUP = 16
C_N_GROUPS = D_MODEL // C_GROUP
C_STATE = 64
C_CHUNK = 16
_C_GL = 128 // C_GROUP

D_HEADS = 12
D_QK_DIM = 32
D_V_DIM = 64

REL_BUCKETS = 32
REL_MAX_DIST = 128

X_HEADS = 4
X_HEAD_DIM = D_MODEL // X_HEADS

N_EXPERTS = 16
EC_CAPACITY = 2

LANES = 128
BF16_ROWS = 16
VMEM_LIMIT = 56 * 1024 * 1024
NEG = -1e30
LOG2E = math.log2(math.e)


def _params(*sem):
    return pltpu.CompilerParams(dimension_semantics=sem, vmem_limit_bytes=VMEM_LIMIT)


def _tile(n, pref):
    t = min(n, pref)
    assert n % t == 0, (n, pref)
    return t


def _ln_rows(v, g, b):
    mu = jnp.mean(v, -1, keepdims=True)
    c = v - mu
    var = jnp.mean(c * c, -1, keepdims=True)
    return c * lax.rsqrt(var + LN_EPS) * g + b


def _dot_nt(a, b):
    return lax.dot_general(a, b, (((1,), (1,)), ((), ())), preferred_element_type=F32)


def _dot(a, b):
    return jnp.dot(a, b, preferred_element_type=F32)


_ONE_LANE = 64


def _with_ones_lane(y):
    lane = lax.broadcasted_iota(I32, y.shape, 1)
    return jnp.where(lane % LANES == _ONE_LANE, 1.0, y)


def _proj_kernel(x_ref, w_ref, o_ref, *, ones_from):
    y = _dot(x_ref[...].astype(BF), w_ref[...])
    if ones_from is None:
        o_ref[...] = y.astype(o_ref.dtype)
    else:
        o_ref[:, :ones_from] = y[:, :ones_from].astype(o_ref.dtype)
        o_ref[:, ones_from:] = _with_ones_lane(y[:, ones_from:]).astype(o_ref.dtype)


def _proj(x, w, tm=512, ones_from=None):
    n, k = x.shape
    m = w.shape[1]
    tm = _tile(n, tm)
    return pl.pallas_call(
        functools.partial(_proj_kernel, ones_from=ones_from),
        grid=(n // tm,),
        in_specs=[pl.BlockSpec((tm, k), lambda i: (i, 0)),
                  pl.BlockSpec((k, m), lambda i: (0, 0))],
        out_specs=pl.BlockSpec((tm, m), lambda i: (i, 0)),
        out_shape=jax.ShapeDtypeStruct((n, m), BF),
        compiler_params=_params("parallel"),
        name="proj",
    )(x, w)


def _post_kernel(x_ref, o_ref, w_ref, g_ref, b_ref, y_ref):
    h = _dot(o_ref[...], w_ref[...])
    y_ref[...] = _ln_rows(ALPHA * x_ref[...] + h, g_ref[...], b_ref[...])


def _post(x, o, w, g, b, tm=512):
    n = x.shape[0]
    ko = o.shape[1]
    tm = _tile(n, tm)
    return pl.pallas_call(
        _post_kernel,
        grid=(n // tm,),
        in_specs=[pl.BlockSpec((tm, D_MODEL), lambda i: (i, 0)),
                  pl.BlockSpec((tm, ko), lambda i: (i, 0)),
                  pl.BlockSpec((ko, D_MODEL), lambda i: (0, 0)),
                  pl.BlockSpec((1, D_MODEL), lambda i: (0, 0)),
                  pl.BlockSpec((1, D_MODEL), lambda i: (0, 0))],
        out_specs=pl.BlockSpec((tm, D_MODEL), lambda i: (i, 0)),
        out_shape=jax.ShapeDtypeStruct((n, D_MODEL), F32),
        compiler_params=_params("parallel"),
        name="post",
    )(x, o, w, g, b)


def _post_b_kernel(x_ref, o0_ref, o1_ref, o2_ref, l0_ref, l1_ref, l2_ref, w_ref, g_ref, b_ref, y_ref):
    l0, l1, l2 = l0_ref[...], l1_ref[...], l2_ref[...]
    m = jnp.maximum(jnp.maximum(l0, l1), l2)
    e0, e1, e2 = jnp.exp(l0 - m), jnp.exp(l1 - m), jnp.exp(l2 - m)
    inv = 1.0 / (e0 + e1 + e2)
    gw = B_HEADS_PER_GROUP * LANES
    h = _dot((o0_ref[...].astype(F32) * (e0 * inv)).astype(BF), w_ref[0:gw, :])
    h = h + _dot((o1_ref[...].astype(F32) * (e1 * inv)).astype(BF), w_ref[gw:2 * gw, :])
    h = h + _dot((o2_ref[...].astype(F32) * (e2 * inv)).astype(BF), w_ref[2 * gw:3 * gw, :])
    y_ref[...] = _ln_rows(ALPHA * x_ref[...] + h, g_ref[...], b_ref[...])


def _post_b(x, os_, ls_, w, g, b, tm=512):
    n = x.shape[0]
    tm = _tile(n, tm)
    gw = B_HEADS_PER_GROUP * LANES
    row = lambda i: (i, 0)
    fixed = lambda i: (0, 0)
    return pl.pallas_call(
        _post_b_kernel,
        grid=(n // tm,),
        in_specs=[pl.BlockSpec((tm, D_MODEL), row)] + [pl.BlockSpec((tm, gw), row)] * 6
        + [pl.BlockSpec((3 * gw, D_MODEL), fixed), pl.BlockSpec((1, D_MODEL), fixed),
           pl.BlockSpec((1, D_MODEL), fixed)],
        out_specs=pl.BlockSpec((tm, D_MODEL), row),
        out_shape=jax.ShapeDtypeStruct((n, D_MODEL), F32),
        compiler_params=_params("parallel"),
        name="post_b",
    )(x, *os_, *ls_, w, g, b)


def _post_c_kernel(x_ref, ys_ref, d_ref, w_ref, g_ref, b_ref, y_ref):
    x = x_ref[...]
    z = jax.nn.gelu(ys_ref[...] + d_ref[...] * x).astype(BF)
    h = _dot(z, w_ref[...])
    hh = h[:, :D_MODEL] * jax.nn.sigmoid(h[:, D_MODEL:])
    y_ref[...] = _ln_rows(ALPHA * x + hh, g_ref[...], b_ref[...])


def _post_c(x, ys, d, w, g, b, tm=512):
    n = x.shape[0]
    tm = _tile(n, tm)
    row = lambda i: (i, 0)
    fixed = lambda i: (0, 0)
    return pl.pallas_call(
        _post_c_kernel,
        grid=(n // tm,),
        in_specs=[pl.BlockSpec((tm, D_MODEL), row), pl.BlockSpec((tm, D_MODEL), row),
                  pl.BlockSpec((1, D_MODEL), fixed), pl.BlockSpec((D_MODEL, 2 * D_MODEL), fixed),
                  pl.BlockSpec((1, D_MODEL), fixed), pl.BlockSpec((1, D_MODEL), fixed)],
        out_specs=pl.BlockSpec((tm, D_MODEL), row),
        out_shape=jax.ShapeDtypeStruct((n, D_MODEL), F32),
        compiler_params=_params("parallel"),
        name="post_c",
    )(x, ys, d, w, g, b)


_A_QK_TILES = A_HEADS + A_KV_HEADS
_A_COLS = (_A_QK_TILES + A_KV_HEADS) * LANES


def _rope_partner():
    d = np.arange(A_HEAD_DIM)
    e = d % (A_HEAD_DIM // 2)
    lo = e < A_HEAD_DIM // 4
    return np.where(lo, d + A_HEAD_DIM // 4, d - A_HEAD_DIM // 4), np.where(lo, -1.0, 1.0).astype(np.float32)


def _prep_a(w_qkv, q_gain, k_gain, w_o):
    partner, sign = _rope_partner()
    nqk = _A_QK_TILES * A_HEAD_DIM
    wqk = w_qkv[:, :nqk].reshape(D_MODEL, _A_QK_TILES, A_HEAD_DIM)
    wsw = wqk[:, :, partner] * sign
    wqk = jnp.concatenate([wqk, wsw], -1).reshape(D_MODEL, _A_QK_TILES * LANES)
    wv = w_qkv[:, nqk:].reshape(D_MODEL, A_KV_HEADS, A_HEAD_DIM)
    wv = jnp.concatenate([wv, jnp.zeros_like(wv)], -1).reshape(D_MODEL, A_KV_HEADS * LANES)
    w = jnp.concatenate([wqk, wv], 1).astype(BF)
    gq = jnp.concatenate([q_gain, q_gain[partner]]) * (A_HEAD_DIM ** -0.5 * 0.5 * LOG2E)
    gk = jnp.concatenate([k_gain, k_gain[partner]])
    gains = jnp.stack([gq, gk], 0)
    wo = w_o.reshape(A_HEADS, A_HEAD_DIM, D_MODEL)
    wo = jnp.concatenate([wo, jnp.zeros_like(wo)], 1).reshape(A_HEADS * LANES, D_MODEL).astype(BF)
    return w, gains, wo


def _rope_table(s):
    pos = jnp.arange(s)
    rows, cols = (pos // GRID_W).astype(F32), (pos % GRID_W).astype(F32)
    half = A_HEAD_DIM // 2
    freqs = ROPE_BASE ** (-jnp.arange(0, half, 2, dtype=F32) / half)
    ang_r = rows[:, None] * freqs
    ang_c = cols[:, None] * freqs
    ang = jnp.concatenate([ang_r, ang_r, ang_c, ang_c], -1)
    return jnp.concatenate([jnp.cos(ang), jnp.sin(ang)], -1)


def _proj_a_kernel(x_ref, w_ref, cs_ref, g_ref, o_ref):
    y = _dot(x_ref[...].astype(BF), w_ref[...])
    cs = cs_ref[...]
    gq = cs * g_ref[0:1, :]
    gk = cs * g_ref[1:2, :]
    for h in range(_A_QK_TILES):
        t = y[:, LANES * h:LANES * (h + 1)]
        r = lax.rsqrt(jnp.sum(t * t, -1, keepdims=True) * (1.0 / LANES) + RMS_EPS)
        e = t * r * (gq if h < A_HEADS else gk)
        o_ref[:, LANES * h:LANES * (h + 1)] = (e + pltpu.roll(e, LANES // 2, 1)).astype(BF)
    o_ref[:, _A_QK_TILES * LANES:] = _with_ones_lane(y[:, _A_QK_TILES * LANES:]).astype(BF)


def _proj_a(x, w, cs, gains, s, tm=512):
    n = x.shape[0]
    tm = _tile(s, tm)
    per = s // tm
    return pl.pallas_call(
        _proj_a_kernel,
        grid=(n // tm,),
        in_specs=[pl.BlockSpec((tm, D_MODEL), lambda i: (i, 0)),
                  pl.BlockSpec((D_MODEL, _A_COLS), lambda i: (0, 0)),
                  pl.BlockSpec((tm, LANES), lambda i: (i % per, 0)),
                  pl.BlockSpec((2, LANES), lambda i: (0, 0))],
        out_specs=pl.BlockSpec((tm, _A_COLS), lambda i: (i, 0)),
        out_shape=jax.ShapeDtypeStruct((n, _A_COLS), BF),
        compiler_params=_params("parallel"),
        name="proj_a",
    )(x, w, cs, gains)


def _softmax_step(s, v, m_ref, acc_ref):
    tk = s.shape[1]
    m_old = m_ref[...]
    m_new = jnp.maximum(m_old, jnp.max(s, -1, keepdims=True))
    p = jnp.concatenate([jnp.exp2(s[:, LANES * c:LANES * (c + 1)] - m_new).astype(BF) for c in range(tk // LANES)],
                        axis=1)
    acc_ref[...] = jnp.exp2(m_old - m_new) * acc_ref[...] + _dot(p, v)
    m_ref[...] = m_new


def _softmax_init(m_ref, acc_ref):
    m_ref[...] = jnp.full(m_ref.shape, -jnp.inf, F32)
    acc_ref[...] = jnp.zeros_like(acc_ref)


def _flash_a_kernel(q_ref, k_ref, v_ref, o_ref, m_ref, acc_ref, *, tk, nk):
    tq = q_ref.shape[0]
    rep = A_HEADS // A_KV_HEADS
    q = jnp.concatenate([q_ref[:, LANES * r:LANES * (r + 1)] for r in range(rep)], axis=0)
    _softmax_init(m_ref, acc_ref)

    def body(j, carry):
        off = pl.multiple_of(j * tk, tk)
        _softmax_step(_dot_nt(q, k_ref[pl.ds(off, tk), :]), v_ref[pl.ds(off, tk), :], m_ref, acc_ref)
        return carry

    lax.fori_loop(0, nk, body, 0, unroll=4)
    acc = acc_ref[...]
    o = acc / acc[:, _ONE_LANE:_ONE_LANE + 1]
    for r in range(rep):
        o_ref[:, LANES * r:LANES * (r + 1)] = o[r * tq:(r + 1) * tq].astype(BF)


def _flash_a(qkv, tq=256, tk=1024):
    b, s, _ = qkv.shape
    tq, tk = _tile(s, tq), _tile(s, tk)
    rep = A_HEADS // A_KV_HEADS
    gw = rep * LANES
    return pl.pallas_call(
        functools.partial(_flash_a_kernel, tk=tk, nk=s // tk),
        grid=(b, A_KV_HEADS, s // tq),
        in_specs=[pl.BlockSpec((None, tq, gw), lambda bi, g, i: (bi, i, g)),
                  pl.BlockSpec((None, s, LANES), lambda bi, g, i: (bi, 0, A_HEADS + g)),
                  pl.BlockSpec((None, s, LANES), lambda bi, g, i: (bi, 0, _A_QK_TILES + g))],
        out_specs=pl.BlockSpec((None, tq, gw), lambda bi, g, i: (bi, i, g)),
        out_shape=jax.ShapeDtypeStruct((b, s, A_HEADS * LANES), BF),
        scratch_shapes=[pltpu.VMEM((rep * tq, LANES), F32)] * 2,
        compiler_params=_params("parallel", "parallel", "arbitrary"),
        name="flash_a",
    )(qkv, qkv, qkv)


def _bucket(rel):
    half = REL_BUCKETS // 2
    max_exact = half // 2
    n = jnp.abs(rel)
    large = max_exact + (jnp.log(jnp.maximum(n, 1).astype(F32) / max_exact)
                         / math.log(REL_MAX_DIST / max_exact) * (half - max_exact)).astype(I32)
    large = jnp.minimum(large, half - 1)
    return jnp.where(rel > 0, half, 0) + jnp.where(n < max_exact, n, large)


_B_TILES = 3 * len(B_PATTERNS) * B_HEADS_PER_GROUP
_B_GW = B_HEADS_PER_GROUP * LANES
_B_T = 256


def _pad_heads(w, heads, dim):
    w = w.reshape(w.shape[0], heads, dim)
    return jnp.concatenate([w, jnp.zeros((w.shape[0], heads, LANES - dim), w.dtype)], -1).reshape(
        w.shape[0], heads * LANES)


def _prep_b(w_qkv, w_o):
    nh = len(B_PATTERNS) * B_HEADS_PER_GROUP
    c = nh * B_HEAD_DIM
    wq = _pad_heads(w_qkv[:, :c] * (B_HEAD_DIM ** -0.5), nh, B_HEAD_DIM)
    wk = _pad_heads(w_qkv[:, c:2 * c], nh, B_HEAD_DIM)
    wv = _pad_heads(w_qkv[:, 2 * c:], nh, B_HEAD_DIM)
    w = jnp.concatenate([wq, wk, wv], 1).astype(BF)
    wo = w_o.reshape(nh, B_HEAD_DIM, D_MODEL)
    wo = jnp.concatenate([wo, jnp.zeros_like(wo)], 1).reshape(nh * LANES, D_MODEL).astype(BF)
    return w, wo


def _toeplitz(vec, t):
    flat = jnp.tile(vec, (1,) * (vec.ndim - 1) + (t,))[..., :t * (2 * t - 1)]
    return flat.reshape(vec.shape[:-1] + (t, 2 * t - 1))[..., :t]


def _band_tiles(g, t):
    window, dil = B_PATTERNS[g]
    return pl.cdiv((window // (2 * dil)) * dil, t)


def _band_bias(rel_bias, g, t):
    window, dil = B_PATTERNS[g]
    reach = (window // (2 * dil)) * dil
    n = _band_tiles(g, t)
    rel0 = (jnp.arange(2 * t) + t) % (2 * t) - t
    rel = jnp.arange(-n, n + 1)[:, None] * t + rel0[None, :]
    bias = rel_bias[_bucket(rel)][:, :, g * B_HEADS_PER_GROUP:(g + 1) * B_HEADS_PER_GROUP]
    bias = jnp.where(((rel % dil == 0) & (jnp.abs(rel) <= reach))[:, :, None], bias, NEG)
    return _toeplitz(jnp.transpose(bias, (2, 0, 1)).astype(F32), t)


def _band_kernel(q_ref, k_ref, v_ref, bias_ref, o_ref, lse_ref, *, n, nk):
    t = q_ref.shape[0]
    i = pl.program_id(2)
    q = q_ref[...]
    logits, offs = [], []
    for o in range(-n, n + 1):
        j = i + o
        off = pl.multiple_of(jnp.clip(j, 0, nk - 1) * t, t)
        s = _dot_nt(q, k_ref[pl.ds(off, t), :]) + bias_ref[o + n]
        logits.append(jnp.where((j >= 0) & (j < nk), s, NEG))
        offs.append(off)
    m = logits[0].max(-1, keepdims=True)
    for s in logits[1:]:
        m = jnp.maximum(m, s.max(-1, keepdims=True))
    l = jnp.zeros((t, 1), F32)
    acc = jnp.zeros((t, LANES), F32)
    for s, off in zip(logits, offs):
        p = jnp.exp(s - m)
        l = l + jnp.sum(p, -1, keepdims=True)
        acc = acc + _dot(p.astype(BF), v_ref[pl.ds(off, t), :])
    o_ref[...] = (acc / l).astype(BF)
    lse_ref[...] = jnp.broadcast_to(m + jnp.log(l), (t, LANES))


def _band_attention(qkv, bias, g, t):
    b, s, _ = qkv.shape
    nh = len(B_PATTERNS) * B_HEADS_PER_GROUP
    n = _band_tiles(g, t)
    spec = lambda base: pl.BlockSpec((None, s, LANES), lambda bi, h, i: (bi, 0, base + g * B_HEADS_PER_GROUP + h))
    out_spec = pl.BlockSpec((None, t, LANES), lambda bi, h, i: (bi, i, h))
    o, lse = pl.pallas_call(
        functools.partial(_band_kernel, n=n, nk=s // t),
        grid=(b, B_HEADS_PER_GROUP, s // t),
        in_specs=[pl.BlockSpec((None, t, LANES), lambda bi, h, i: (bi, i, g * B_HEADS_PER_GROUP + h)),
                  spec(nh), spec(2 * nh),
                  pl.BlockSpec((None, 2 * n + 1, t, t), lambda bi, h, i: (h, 0, 0, 0))],
        out_specs=[out_spec, out_spec],
        out_shape=[jax.ShapeDtypeStruct((b, s, _B_GW), BF), jax.ShapeDtypeStruct((b, s, _B_GW), F32)],
        compiler_params=_params("parallel", "parallel", "arbitrary"),
        name="band_%d" % g,
    )(qkv, qkv, qkv, bias)
    return o.reshape(b * s, _B_GW), lse.reshape(b * s, _B_GW)


def _prep_c(lam_re, lam_im, log_dt, b_re, b_im, c_re, c_im):
    hp = lax.Precision.HIGHEST
    L, P, C = C_CHUNK, C_STATE, C_GROUP
    lr, li = lam_re.astype(F32), lam_im.astype(F32)
    dt = jnp.exp(log_dt.astype(F32))[..., None]
    mag = jnp.exp(lr * dt)
    ar, ai = mag * jnp.cos(li * dt), mag * jnp.sin(li * dt)
    den = lr * lr + li * li
    zr = ((ar - 1.0) * lr + ai * li) / den
    zi = (ai * lr - (ar - 1.0) * li) / den
    br, bi = b_re.astype(F32), b_im.astype(F32)
    bbr = zr[..., None] * br - zi[..., None] * bi
    bbi = zr[..., None] * bi + zi[..., None] * br
    cr, ci = c_re.astype(F32), c_im.astype(F32)
    prs, pis = [jnp.ones_like(ar)], [jnp.zeros_like(ai)]
    for _ in range(L):
        pr_, pi_ = prs[-1], pis[-1]
        prs.append(ar * pr_ - ai * pi_)
        pis.append(ar * pi_ + ai * pr_)
    pr, pi = jnp.stack(prs, 0), jnp.stack(pis, 0)

    def lag(pr_k, pi_k):
        tr = pr_k[..., None] * bbr - pi_k[..., None] * bbi
        ti = pr_k[..., None] * bbi + pi_k[..., None] * bbr
        return (jnp.einsum('dgop,kdgpi->kdgoi', cr, tr, precision=hp)
                - jnp.einsum('dgop,kdgpi->kdgoi', ci, ti, precision=hp))

    kern = lag(pr[:L], pi[:L])
    jj = np.arange(L)[:, None]
    ii = np.arange(L)[None, :]
    kf = kern[:, 0][np.clip(ii - jj, 0, L - 1)] * jnp.asarray((ii >= jj)[:, :, None, None, None], F32)
    kb = kern[:, 1][np.clip(jj - ii, 0, L - 1)] * jnp.asarray((jj >= ii)[:, :, None, None, None], F32)
    m_intra = jnp.transpose(kf + kb, (2, 0, 4, 1, 3)).reshape(C_N_GROUPS, L * C, L * C)

    def summ(d, powers):
        pr_k, pi_k = pr[powers, d], pi[powers, d]
        sr = pr_k[..., None] * bbr[d] - pi_k[..., None] * bbi[d]
        si = pr_k[..., None] * bbi[d] + pi_k[..., None] * bbr[d]
        s = jnp.concatenate([sr, si], 2)
        return jnp.transpose(s, (1, 0, 3, 2)).reshape(C_N_GROUPS, L * C, 2 * P)

    w_sum = jnp.concatenate([summ(0, np.arange(L - 1, -1, -1)), summ(1, np.arange(L))], -1)

    def outw(d, powers):
        pr_k, pi_k = pr[powers, d], pi[powers, d]
        wr = cr[d][None] * pr_k[:, :, None, :] - ci[d][None] * pi_k[:, :, None, :]
        wi = -(cr[d][None] * pi_k[:, :, None, :] + ci[d][None] * pr_k[:, :, None, :])
        w = jnp.concatenate([wr, wi], -1)
        return jnp.transpose(w, (1, 3, 0, 2)).reshape(C_N_GROUPS, 2 * P, L * C)

    w_state = jnp.concatenate([outw(0, np.arange(1, L + 1)), outw(1, np.arange(L, 0, -1))], 1)

    nq = C_N_GROUPS // _C_GL
    npair = _C_GL // 2
    blk = L * _C_GL * C
    lane = jnp.arange(LANES)
    gsel = (lane[None, :] // C == jnp.arange(_C_GL)[:, None])
    src = jnp.tile(m_intra.astype(BF).reshape(nq, _C_GL, L, C, L, C), (1, 1, 1, 1, 1, _C_GL))
    src = jnp.where(gsel[None, :, None, None, None, :], src, 0)
    w_intra = jnp.transpose(src, (0, 2, 1, 3, 4, 5)).reshape(nq, blk, blk)
    psel = (lane[None, :] // P == jnp.arange(2)[:, None])
    ksel = jnp.eye(npair, dtype=bool)
    src = jnp.tile(w_sum.astype(BF).reshape(nq, npair, 2, L, C, 1, 4, P), (1, 1, 1, 1, 1, npair, 1, 2))
    src = jnp.where(psel[None, None, :, None, None, None, None, :]
                    & ksel[None, :, None, None, None, :, None, None], src, 0)
    w_sum = jnp.transpose(src, (0, 3, 1, 2, 4, 5, 6, 7)).reshape(nq, blk, _C_GL * 4 * P)
    osel = (lane[None, None, :] // C
            == (2 * jnp.arange(npair)[:, None, None] + jnp.arange(2)[None, :, None]))
    src = jnp.tile(w_state.astype(BF).reshape(nq, npair, 2, 4, P, L, C), (1, 1, 1, 1, 1, 1, _C_GL))
    src = jnp.where(osel[None, :, :, None, None, None, :], src, 0)
    w_state = jnp.transpose(src, (0, 1, 3, 2, 4, 5, 6)).reshape(nq, _C_GL * 4 * P, blk)
    dec = jnp.stack([pr[L, 0], pi[L, 0], pr[L, 1], pi[L, 1]], 0).reshape(4, nq * npair, 2 * P)
    dec = jnp.broadcast_to(jnp.transpose(dec, (1, 0, 2))[:, :, None, :], (nq * npair, 4, 8, 2 * P))
    return w_sum.astype(BF), w_intra.astype(BF), w_state.astype(BF), dec


def _chunk_rows(x_ref, nb, rc):
    L = C_CHUNK
    rows = [jnp.concatenate([x_ref[b, pl.ds(j, rc, stride=L), :].astype(BF) for j in range(L)], axis=1)
            for b in range(nb)]
    return jnp.concatenate(rows, axis=0)


def _s5_sum_kernel(x_ref, w_ref, s_ref, *, nb, rc):
    s = _dot(_chunk_rows(x_ref, nb, rc), w_ref[...])
    for b in range(nb):
        for t in range(s_ref.shape[0]):
            s_ref[t, pl.ds(b, rc, stride=nb), :] = s[b * rc:(b + 1) * rc, LANES * t:LANES * (t + 1)]


def _s5_scan_kernel(s_ref, dec_ref, e_ref, *, nc, nb):
    units = s_ref.shape[0] // 4
    dec = [[dec_ref[u, k, 0:nb, :] for k in range(4)] for u in range(units)]

    def body(t, carry):
        rf = pl.ds(pl.multiple_of(t * nb, nb), nb)
        rb = pl.ds(pl.multiple_of((nc - 1 - t) * nb, nb), nb)
        new = []
        for u in range(units):
            fr, fi, br, bi = carry[4 * u:4 * u + 4]
            arf, aif, arb, aib = dec[u]
            e_ref[4 * u, rf, :] = fr
            e_ref[4 * u + 1, rf, :] = fi
            e_ref[4 * u + 2, rb, :] = br
            e_ref[4 * u + 3, rb, :] = bi
            new += [fr * arf - fi * aif + s_ref[4 * u, rf, :], fi * arf + fr * aif + s_ref[4 * u + 1, rf, :],
                    br * arb - bi * aib + s_ref[4 * u + 2, rb, :], bi * arb + br * aib + s_ref[4 * u + 3, rb, :]]
        return tuple(new)

    z = jnp.zeros((nb, LANES), F32)
    lax.fori_loop(0, nc, body, (z,) * (4 * units))


def _s5_out_kernel(x_ref, e_ref, wi_ref, wc_ref, y_ref, *, nb, rc):
    L = C_CHUNK
    half = pl.program_id(2)
    xc = _chunk_rows(x_ref, nb, rc)
    ec = jnp.concatenate(
        [jnp.concatenate([e_ref[t, pl.ds(b, rc, stride=nb), :].astype(BF) for t in range(e_ref.shape[0])], axis=1)
         for b in range(nb)], axis=0)
    y = _dot(xc, wi_ref[...]) + _dot(ec, wc_ref[...])
    for b in range(nb):
        for ii in range(L // 2):
            y_ref[b, pl.ds(half * (L // 2) + ii, rc, stride=L), :] = y[b * rc:(b + 1) * rc, LANES * ii:LANES * (ii + 1)]


def _s5(x, b, s, w_sum, w_intra, w_state, dec):
    L = C_CHUNK
    nc = s // L
    nq = D_MODEL // LANES
    nsl = w_sum.shape[2] // LANES
    blk = L * LANES
    rc = _tile(nc, max(8, 512 // b))
    x3 = x.reshape(b, s, D_MODEL)
    x_spec = pl.BlockSpec((b, rc * L, LANES), lambda q, c, *_: (0, c, q))
    sums = pl.pallas_call(
        functools.partial(_s5_sum_kernel, nb=b, rc=rc),
        grid=(nq, nc // rc),
        in_specs=[x_spec, pl.BlockSpec((None, blk, nsl * LANES), lambda q, c: (q, 0, 0))],
        out_specs=pl.BlockSpec((nsl, rc * b, LANES), lambda q, c: (q, c, 0)),
        out_shape=jax.ShapeDtypeStruct((nq * nsl, nc * b, LANES), F32),
        compiler_params=_params("parallel", "arbitrary"),
        name="s5_sum",
    )(x3, w_sum)
    upb = 2
    ent = pl.pallas_call(
        functools.partial(_s5_scan_kernel, nc=nc, nb=b),
        grid=(nq * nsl // (4 * upb),),
        in_specs=[pl.BlockSpec((4 * upb, nc * b, LANES), lambda i: (i, 0, 0)),
                  pl.BlockSpec((upb, 4, 8, LANES), lambda i: (i, 0, 0, 0))],
        out_specs=pl.BlockSpec((4 * upb, nc * b, LANES), lambda i: (i, 0, 0)),
        out_shape=jax.ShapeDtypeStruct((nq * nsl, nc * b, LANES), F32),
        compiler_params=_params("parallel"),
        name="s5_scan",
    )(sums, dec)
    y = pl.pallas_call(
        functools.partial(_s5_out_kernel, nb=b, rc=rc),
        grid=(nq, nc // rc, 2),
        in_specs=[x_spec,
                  pl.BlockSpec((nsl, rc * b, LANES), lambda q, c, h: (q, c, 0)),
                  pl.BlockSpec((None, blk, blk // 2), lambda q, c, h: (q, 0, h)),
                  pl.BlockSpec((None, nsl * LANES, blk // 2), lambda q, c, h: (q, 0, h))],
        out_specs=pl.BlockSpec((b, rc * L, LANES), lambda q, c, h: (0, c, q)),
        out_shape=jax.ShapeDtypeStruct((b, s, D_MODEL), F32),
        compiler_params=_params("parallel", "arbitrary", "arbitrary"),
        name="s5_out",
    )(x3, ent, w_intra, w_state)
    return y.reshape(b * s, D_MODEL)


_D_T = 512


def _prep_d(w_qkv, norm_gain, w_o, lambda_init):
    qk_w = D_HEADS * 2 * D_QK_DIM
    wq = _pad_heads(w_qkv[:, :qk_w] * (D_QK_DIM ** -0.5 * LOG2E), D_HEADS, 2 * D_QK_DIM)
    wk = _pad_heads(w_qkv[:, qk_w:2 * qk_w], D_HEADS, 2 * D_QK_DIM)
    wv = _pad_heads(w_qkv[:, 2 * qk_w:], D_HEADS, D_V_DIM)
    w = jnp.concatenate([wq, wk, wv], 1).astype(BF)
    gain = jnp.concatenate([norm_gain * (1.0 - lambda_init), jnp.zeros((LANES - D_V_DIM,), F32)])[None, :]
    wo = w_o.reshape(D_HEADS, D_V_DIM, D_MODEL)
    wo = jnp.concatenate([wo, jnp.zeros_like(wo)], 1).reshape(D_HEADS * LANES, D_MODEL).astype(BF)
    return w, gain, wo


def _diff_bias(rel_bias, t):
    rel0 = (jnp.arange(2 * t) + t) % (2 * t) - t
    rel = jnp.arange(-2, 3)[:, None] * t + rel0[None, :]
    return jnp.transpose(rel_bias[_bucket(rel)] * LOG2E, (2, 0, 1)).astype(F32)


def _flash_d_kernel(q_ref, k_ref, v_ref, vec_ref, lam_ref, gain_ref, o_ref, m_ref, acc_ref, bias_ref, *, kt, nk,
                    lambda_init):
    t = q_ref.shape[0]
    i = pl.program_id(2)

    @pl.when(i == 0)
    def _():
        for d in range(bias_ref.shape[0]):
            full = jnp.broadcast_to(vec_ref[d:d + 1, :], (t, 2 * t))
            bias_ref[d] = pltpu.roll(full, 0, 1, stride=1, stride_axis=0)[:, :t]

    qv = q_ref[...].astype(F32)
    lane = lax.broadcasted_iota(I32, (t, LANES), 1)
    q0 = jnp.where(lane < D_QK_DIM, qv, 0.0).astype(BF)
    q1 = jnp.where((lane >= D_QK_DIM) & (lane < 2 * D_QK_DIM), qv, 0.0).astype(BF)
    q = jnp.concatenate([q0, q1], axis=0)
    _softmax_init(m_ref, acc_ref)

    def body(j, carry):
        off = pl.multiple_of(j * (kt * t), kt * t)
        bias = jnp.concatenate([bias_ref[jnp.clip(j * kt + c - i, -2, 2) + 2] for c in range(kt)], axis=1)
        s = _dot_nt(q, k_ref[pl.ds(off, kt * t), :])
        s = (s.reshape(2, t, kt * t) + bias[None]).reshape(2 * t, kt * t)
        _softmax_step(s, v_ref[pl.ds(off, kt * t), :], m_ref, acc_ref)
        return carry

    lax.fori_loop(0, nk, body, 0, unroll=4)
    lf = lam_ref[...]
    lam = (jnp.exp(jnp.sum(lf[0:1] * lf[1:2], keepdims=True))
           - jnp.exp(jnp.sum(lf[2:3] * lf[3:4], keepdims=True)) + lambda_init)
    acc = acc_ref[...]
    on = acc / acc[:, _ONE_LANE:_ONE_LANE + 1]
    o = jnp.where(lane < D_V_DIM, on[:t] - lam * on[t:], 0.0)
    ms = jnp.sum(o * o, -1, keepdims=True) * (1.0 / D_V_DIM)
    o_ref[...] = (o * lax.rsqrt(ms + RMS_EPS) * gain_ref[...]).astype(BF)


def _flash_d(qkv, bias, lam, gain, lambda_init):
    b, s, _ = qkv.shape
    t = bias.shape[-1] // 2
    kt = 2 if s % (2 * t) == 0 else 1
    return pl.pallas_call(
        functools.partial(_flash_d_kernel, kt=kt, nk=s // (kt * t), lambda_init=lambda_init),
        grid=(b, D_HEADS, s // t),
        in_specs=[pl.BlockSpec((None, t, LANES), lambda bi, h, i: (bi, i, h)),
                  pl.BlockSpec((None, s, LANES), lambda bi, h, i: (bi, 0, D_HEADS + h)),
                  pl.BlockSpec((None, s, LANES), lambda bi, h, i: (bi, 0, 2 * D_HEADS + h)),
                  pl.BlockSpec((None, 5, 2 * t), lambda bi, h, i: (h, 0, 0)),
                  pl.BlockSpec((4, D_QK_DIM), lambda bi, h, i: (0, 0)),
                  pl.BlockSpec((1, LANES), lambda bi, h, i: (0, 0))],
        out_specs=pl.BlockSpec((None, t, LANES), lambda bi, h, i: (bi, i, h)),
        out_shape=jax.ShapeDtypeStruct((b, s, D_HEADS * LANES), BF),
        scratch_shapes=[pltpu.VMEM((2 * t, LANES), F32)] * 2 + [pltpu.VMEM((5, t, t), F32)],
        compiler_params=_params("parallel", "parallel", "arbitrary"),
        name="flash_d",
    )(qkv, qkv, qkv, bias, lam, gain)


def _cross_kernel(x_ref, kv_ref, wq_ref, wo_ref, g_ref, b_ref, wr_ref, y_ref, ybf_ref, aff_ref):
    x = x_ref[...]
    tm = x.shape[0]
    q = (_dot(x.astype(BF), wq_ref[...]) * (X_HEAD_DIM ** -0.5)).astype(BF)
    outs = []
    for h in range(X_HEADS):
        sl = slice(X_HEAD_DIM * h, X_HEAD_DIM * (h + 1))
        s = _dot_nt(q[:, sl], kv_ref[:, sl])
        p = jnp.exp(s - jnp.max(s, -1, keepdims=True))
        l = jnp.sum(p, -1, keepdims=True)
        vh = kv_ref[:, D_MODEL + X_HEAD_DIM * h:D_MODEL + X_HEAD_DIM * (h + 1)]
        outs.append((_dot(p.astype(BF), vh) / l).astype(BF))
    o = jnp.concatenate(outs, axis=1)
    y = _ln_rows(ALPHA * x + _dot(o, wo_ref[...]), g_ref[...], b_ref[...])
    y_ref[...] = y
    yh = y.astype(BF)
    ybf_ref[...] = yh
    yl = (y - yh.astype(F32)).astype(BF)
    wr = wr_ref[...]
    wh = wr.astype(BF)
    wl = (wr - wh.astype(F32)).astype(BF)
    lg = _dot_nt(wh, yh) + _dot_nt(wh, yl) + _dot_nt(wl, yh)
    e = jnp.exp(lg - jnp.max(lg, 0, keepdims=True))
    aff = e / jnp.sum(e, 0, keepdims=True)
    for c in range(tm // LANES):
        aff_ref[c] = aff[:, LANES * c:LANES * (c + 1)]


def _cross(x, kv, wq, wo, g, b, wr_t, s, mem_len, tm=512):
    n = x.shape[0]
    tm = _tile(s, tm)
    per = s // tm
    fixed = lambda i: (0, 0)
    return pl.pallas_call(
        _cross_kernel,
        grid=(n // tm,),
        in_specs=[pl.BlockSpec((tm, D_MODEL), lambda i: (i, 0)),
                  pl.BlockSpec((mem_len, 2 * D_MODEL), lambda i: (i // per, 0)),
                  pl.BlockSpec((D_MODEL, D_MODEL), fixed), pl.BlockSpec((D_MODEL, D_MODEL), fixed),
                  pl.BlockSpec((1, D_MODEL), fixed), pl.BlockSpec((1, D_MODEL), fixed),
                  pl.BlockSpec((N_EXPERTS, D_MODEL), fixed)],
        out_specs=[pl.BlockSpec((tm, D_MODEL), lambda i: (i, 0)),
                   pl.BlockSpec((tm, D_MODEL), lambda i: (i, 0)),
                   pl.BlockSpec((tm // LANES, N_EXPERTS, LANES), lambda i: (i, 0, 0))],
        out_shape=[jax.ShapeDtypeStruct((n, D_MODEL), F32),
                   jax.ShapeDtypeStruct((n, D_MODEL), BF),
                   jax.ShapeDtypeStruct((n // LANES, N_EXPERTS, LANES), F32)],
        compiler_params=_params("parallel"),
        name="cross",
    )(x, kv, wq, wo, g, b, wr_t)


def _select_kernel(a_ref, pos_ref, st_ref, *, k, nbits):
    nt = a_ref.shape[0]
    shape = (nt, N_EXPERTS, LANES)
    kf = float(k)

    def keys():
        return lax.bitcast_convert_type(a_ref[...], I32)

    def count(mask):
        c = jnp.sum(jnp.where(mask, 1.0, 0.0), axis=0, keepdims=True)
        return jnp.sum(c, axis=2, keepdims=True)

    def value_step(it, thr):
        cand = thr | jnp.left_shift(jnp.int32(1), 30 - it)
        return jnp.where(count(keys() >= cand) >= kf, cand, thr)

    thr = lax.fori_loop(0, 31, value_step, jnp.zeros((1, N_EXPERTS, 1), I32))
    need = kf - count(keys() > thr)
    idx = lax.broadcasted_iota(I32, shape, 0) * LANES + lax.broadcasted_iota(I32, shape, 2)

    def index_step(it, ithr):
        cand = ithr | jnp.left_shift(jnp.int32(1), nbits - 1 - it)
        return jnp.where(count((keys() == thr) & (idx < cand)) < need, cand, ithr)

    ithr = lax.fori_loop(0, nbits, index_step, jnp.zeros((1, N_EXPERTS, 1), I32))
    thr2, ithr2 = thr[0], ithr[0]
    upper = jnp.where(lax.broadcasted_iota(I32, (LANES, LANES), 0) <= lax.broadcasted_iota(I32, (LANES, LANES), 1),
                      1.0, 0.0).astype(BF)
    lane = lax.broadcasted_iota(I32, (N_EXPERTS, LANES), 1)

    def tile_step(j, carry):
        kj = lax.bitcast_convert_type(a_ref[j], I32)
        sel = (kj > thr2) | ((kj == thr2) & (j * LANES + lane <= ithr2))
        m = jnp.where(sel, 1.0, 0.0)
        inc = _dot(m.astype(BF), upper)
        pos_ref[j] = jnp.where(sel, inc - m + carry, -1.0).astype(I32)
        st_ref[j] = jnp.broadcast_to(carry, (N_EXPERTS, LANES)).astype(I32)
        return carry + inc[:, LANES - 1:LANES]

    lax.fori_loop(0, nt, tile_step, jnp.zeros((N_EXPERTS, 1), F32))


def _select(aff3, k):
    nt = aff3.shape[0]
    nbits = max(1, int(math.ceil(math.log2(nt * LANES))))
    shp = jax.ShapeDtypeStruct(aff3.shape, I32)
    return pl.pallas_call(
        functools.partial(_select_kernel, k=k, nbits=nbits),
        out_shape=[shp, shp],
        compiler_params=pltpu.CompilerParams(vmem_limit_bytes=VMEM_LIMIT),
        name="select",
    )(aff3)


_GATHER_ROWS = LANES + BF16_ROWS
_GATHER_ROWS_SMALL = 3 * BF16_ROWS

_MOE_GROUP = 2


def _moe_ffn_kernel(st_ref, x_ref, pos_ref, aff_ref, wg_ref, wu_ref, wd_ref, ye_ref, buf_ref, gate_ref, *, nb, ncf,
                    nch, sub, ck):
    ep = buf_ref.shape[0]
    grp = pl.program_id(0)
    t = pl.program_id(1)

    @pl.when(t == 0)
    def _():
        buf_ref[...] = jnp.zeros_like(buf_ref)
        gate_ref[...] = jnp.zeros_like(gate_ref)

    @pl.when(t < nb)
    def _():
        def place(s, k, e, base, nrows):
            rows = pl.ds(base, nrows)
            hit = lax.broadcasted_iota(I32, (nrows, LANES), 0) == pos_ref[s, pl.ds(e, 1), :] - base
            xs = x_ref[LANES * s:LANES * (s + 1), :]
            buf_ref[k, rows, :] = buf_ref[k, rows, :] + _dot(jnp.where(hit, 1.0, 0.0).astype(BF), xs).astype(BF)
            gate_ref[k, rows, :] = gate_ref[k, rows, :] + jnp.sum(
                jnp.where(hit, aff_ref[s, pl.ds(e, 1), :], 0.0), -1, keepdims=True)

        def base_of(s, k):
            st = st_ref[grp * ep + k, t * sub + s]
            return pl.multiple_of((st // BF16_ROWS) * BF16_ROWS, BF16_ROWS)

        pairs = [(s, k) for s in range(sub) for k in range(ep)]
        few = functools.reduce(
            jnp.logical_and,
            [st_ref[grp * ep + k, t * sub + s + 1] - base_of(s, k) <= _GATHER_ROWS_SMALL for s, k in pairs])

        @pl.when(few)
        def _():
            for s, k in pairs:
                place(s, k, grp * ep + k, base_of(s, k), _GATHER_ROWS_SMALL)

        @pl.when(jnp.logical_not(few))
        def _():
            for s, k in pairs:
                place(s, k, grp * ep + k, base_of(s, k), _GATHER_ROWS)

    @pl.when(t >= nb)
    def _():
        k = (t - nb) // nch
        c = (t - nb) - k * nch

        @pl.when(c < ncf)
        def _():
            rows = pl.ds(pl.multiple_of(c * ck, ck), ck)
            xe = buf_ref[k, rows, :]
            h = (jax.nn.silu(_dot(xe, wg_ref[...])) * _dot(xe, wu_ref[...])).astype(BF)
            ye_ref[...] = (_dot(h, wd_ref[...]) * gate_ref[k, rows, :]).astype(BF)

        @pl.when(c >= ncf)
        def _():
            ye_ref[...] = jnp.zeros_like(ye_ref)


def _moe_ffn(starts, xbf, pos3, aff3, wg, wu, wd, cap, tb=1024):
    n = xbf.shape[0]
    tb = _tile(n, tb)
    nb = n // tb
    sub = tb // LANES
    ck = _tile(cap, 512)
    ncf = cap // ck
    nch = ncf + pl.cdiv(_WIN_ROWS, ck)
    dff = wg.shape[-1]
    ep = _MOE_GROUP
    blk = lambda g, t, st: (jnp.minimum(t, nb - 1), 0)
    blk3 = lambda g, t, st: (jnp.minimum(t, nb - 1), 0, 0)
    expert = lambda g, t: g * ep + jnp.clip((t - nb) // nch, 0, ep - 1)
    wmap = lambda g, t, st: (expert(g, t), 0, 0)
    return pl.pallas_call(
        functools.partial(_moe_ffn_kernel, nb=nb, ncf=ncf, nch=nch, sub=sub, ck=ck),
        grid_spec=pltpu.PrefetchScalarGridSpec(
            num_scalar_prefetch=1,
            grid=(N_EXPERTS // ep, nb + ep * nch),
            in_specs=[pl.BlockSpec((tb, D_MODEL), blk),
                      pl.BlockSpec((sub, N_EXPERTS, LANES), blk3),
                      pl.BlockSpec((sub, N_EXPERTS, LANES), blk3),
                      pl.BlockSpec((None, D_MODEL, dff), wmap),
                      pl.BlockSpec((None, D_MODEL, dff), wmap),
                      pl.BlockSpec((None, dff, D_MODEL), wmap)],
            out_specs=pl.BlockSpec((None, ck, D_MODEL),
                                   lambda g, t, st: (expert(g, t), jnp.maximum(t - nb, 0) % nch, 0)),
            scratch_shapes=[pltpu.VMEM((ep, cap + _GATHER_ROWS, D_MODEL), BF),
                            pltpu.VMEM((ep, cap + _GATHER_ROWS, 1), F32)]),
        out_shape=jax.ShapeDtypeStruct((N_EXPERTS, nch * ck, D_MODEL), BF),
        compiler_params=_params("arbitrary", "arbitrary"),
        name="moe_ffn",
    )(starts, xbf, pos3, aff3, wg, wu, wd)


_WIN_STEP = 128
_COMB_TILES = 2
_WIN_ROWS = _WIN_STEP + (_COMB_TILES - 1) * LANES + _GATHER_ROWS


def _combine_kernel(st_ref, x_ref, pos_ref, g_ref, b_ref, *rest):
    ye_refs, y_ref = rest[:N_EXPERTS], rest[N_EXPERTS]
    j = pl.program_id(0)
    col = lax.broadcasted_iota(I32, (LANES, _GATHER_ROWS), 1)
    for u in range(_COMB_TILES):
        tok = slice(LANES * u, LANES * (u + 1))
        acc = None
        for e in range(N_EXPERTS):
            st = st_ref[e, j * _COMB_TILES + u]
            base = (st // BF16_ROWS) * BF16_ROWS
            win = (st_ref[e, j * _COMB_TILES] // _WIN_STEP) * _WIN_STEP
            sub = pl.multiple_of(base - win, BF16_ROWS)
            onehot = jnp.where(col == pos_ref[tok, e:e + 1] - base, 1.0, 0.0).astype(BF)
            d = _dot(onehot, ye_refs[e][0, pl.ds(sub, _GATHER_ROWS), :])
            acc = d if acc is None else acc + d
        y_ref[tok, :] = _ln_rows(ALPHA * x_ref[tok, :] + acc, g_ref[...], b_ref[...])


def _combine(starts, x, pos, g, b, ye):
    n = x.shape[0]
    assert ye.shape[1] >= EC_CAPACITY * n // N_EXPERTS + _WIN_ROWS
    tt = _COMB_TILES * LANES
    row = lambda j, st: (j, 0)
    fixed = lambda j, st: (0, 0)

    def ye_spec(e):
        return pl.BlockSpec((pl.Element(1), pl.Element(_WIN_ROWS), pl.Element(D_MODEL)),
                            lambda j, st: (e, (st[e, j * _COMB_TILES] // _WIN_STEP) * _WIN_STEP, 0))

    return pl.pallas_call(
        _combine_kernel,
        grid_spec=pltpu.PrefetchScalarGridSpec(
            num_scalar_prefetch=1,
            grid=(n // tt,),
            in_specs=[pl.BlockSpec((tt, D_MODEL), row),
                      pl.BlockSpec((tt, N_EXPERTS), row),
                      pl.BlockSpec((1, D_MODEL), fixed), pl.BlockSpec((1, D_MODEL), fixed)]
            + [ye_spec(e) for e in range(N_EXPERTS)],
            out_specs=pl.BlockSpec((tt, D_MODEL), row)),
        out_shape=jax.ShapeDtypeStruct((n, D_MODEL), F32),
        compiler_params=_params("arbitrary"),
        name="combine",
    )(starts, x, pos, g, b, *([ye] * N_EXPERTS))


def _moe(x, xg, aff3, wg, wu, wd, g, b):
    n = x.shape[0]
    cap = EC_CAPACITY * n // N_EXPERTS
    pos3, st3 = _select(aff3, cap)
    starts = jnp.concatenate([jnp.transpose(st3[:, :, 0]), jnp.full((N_EXPERTS, 1), cap, I32)], axis=1)
    ye = _moe_ffn(starts, xg, pos3, aff3, wg, wu, wd, cap)
    pos = jnp.transpose(pos3, (0, 2, 1)).reshape(n, N_EXPERTS)
    return _combine(starts, x, pos, g, b, ye)


def _prep_weights(p):
    w = {}
    w['a'] = [_prep_a(p['a_w_qkv'][j], p['a_q_gain'][j], p['a_k_gain'][j], p['a_w_o'][j])
              for j in range(p['a_w_qkv'].shape[0])]
    w['b'] = [_prep_b(p['b_w_qkv'][j], p['b_w_o'][j]) for j in range(p['b_w_qkv'].shape[0])]
    w['c'] = [_prep_c(p['c_lam_re'][j], p['c_lam_im'][j], p['c_log_dt'][j], p['c_b_re'][j], p['c_b_im'][j],
                      p['c_c_re'][j], p['c_c_im'][j]) + (p['c_d'][j][None, :], p['c_w_glu'][j].astype(BF))
              for j in range(p['c_lam_re'].shape[0])]
    w['d'] = []
    for j in range(p['d_w_qkv'].shape[0]):
        layer = N_MIXERS * j + 3
        lambda_init = 0.8 - 0.6 * math.exp(-0.3 * layer)
        w['d'].append(_prep_d(p['d_w_qkv'][j], p['d_norm_gain'][j], p['d_w_o'][j], lambda_init)
                      + (p['d_lam'][j].astype(F32), lambda_init))
    w['diff_bias'] = _diff_bias(p['rel_bias'], _D_T)
    w['x_w_q'] = p['x_w_q'].astype(BF)
    w['x_w_kv'] = p['x_w_kv'].astype(BF)
    w['x_w_o'] = p['x_w_o'].astype(BF)
    w['router_t'] = jnp.transpose(p['moe_w_router'], (0, 2, 1)).astype(F32)
    w['moe_w_gate'] = p['moe_w_gate'].astype(BF)
    w['moe_w_up'] = p['moe_w_up'].astype(BF)
    w['moe_w_down'] = p['moe_w_down'].astype(BF)
    return w


def _trunk(x, mem, p, w):
    b, s, _ = x.shape
    n = b * s
    mem_len = mem.shape[1]
    x = x.reshape(n, D_MODEL)
    mem2 = mem.reshape(b * mem_len, D_MODEL)
    ln_g, ln_b = p['ln_g'], p['ln_b']
    for i in range(DEPTH):
        m, j = i % N_MIXERS, i // N_MIXERS
        g0, b0 = ln_g[i, 0][None, :], ln_b[i, 0][None, :]
        if m == 0:
            wa, gains, wo = w['a'][j]
            qkv = _proj_a(x, wa, _rope_table(s), gains, s)
            o = _flash_a(qkv.reshape(b, s, _A_COLS))
            x = _post(x, o.reshape(n, A_HEADS * LANES), wo, g0, b0)
        elif m == 1:
            wb, wo = w['b'][j]
            qkv = _proj(x, wb).reshape(b, s, _B_TILES * LANES)
            os_, ls_ = [], []
            for g in range(len(B_PATTERNS)):
                t = _tile(s, _B_T)
                o, lse = _band_attention(qkv, _band_bias(p['rel_bias'], g, t), g, t)
                os_.append(o)
                ls_.append(lse)
            x = _post_b(x, os_, ls_, wo, g0, b0)
        elif m == 2:
            w_sum, w_intra, w_state, dec, dskip, wglu = w['c'][j]
            ys = _s5(x, b, s, w_sum, w_intra, w_state, dec)
            x = _post_c(x, ys, dskip, wglu, g0, b0)
        else:
            wd, gain, wo, lam, lambda_init = w['d'][j]
            qkv = _proj(x, wd, ones_from=2 * D_HEADS * LANES).reshape(b, s, 3 * D_HEADS * LANES)
            o = _flash_d(qkv, w['diff_bias'], lam, gain, lambda_init)
            x = _post(x, o.reshape(n, D_HEADS * LANES), wo, g0, b0)
        kv = _proj(mem2, w['x_w_kv'][i], tm=mem_len)
        x, xbf, aff3 = _cross(x, kv, w['x_w_q'][i], w['x_w_o'][i], ln_g[i, 1][None, :], ln_b[i, 1][None, :],
                              w['router_t'][i], s, mem_len)
        x = _moe(x, xbf, aff3, w['moe_w_gate'][i], w['moe_w_up'][i], w['moe_w_down'][i],
                 ln_g[i, 2][None, :], ln_b[i, 2][None, :])
    return x.reshape(b, s, D_MODEL)


def kernel(x_prompt, x_sample, mem_prompt, mem_sample, rel_bias, ln_g, ln_b, a_w_qkv, a_q_gain, a_k_gain, a_w_o, b_w_qkv, b_w_o, c_lam_re, c_lam_im, c_log_dt, c_b_re, c_b_im, c_c_re, c_c_im, c_d, c_w_glu, d_w_qkv, d_lam, d_norm_gain, d_w_o, x_w_q, x_w_kv, x_w_o, moe_w_router, moe_w_gate, moe_w_up, moe_w_down):
    p = dict(rel_bias=rel_bias, ln_g=ln_g, ln_b=ln_b,
             a_w_qkv=a_w_qkv, a_q_gain=a_q_gain, a_k_gain=a_k_gain, a_w_o=a_w_o,
             b_w_qkv=b_w_qkv, b_w_o=b_w_o,
             c_lam_re=c_lam_re, c_lam_im=c_lam_im, c_log_dt=c_log_dt, c_b_re=c_b_re, c_b_im=c_b_im,
             c_c_re=c_c_re, c_c_im=c_c_im, c_d=c_d, c_w_glu=c_w_glu,
             d_w_qkv=d_w_qkv, d_lam=d_lam, d_norm_gain=d_norm_gain, d_w_o=d_w_o,
             x_w_q=x_w_q, x_w_kv=x_w_kv, x_w_o=x_w_o,
             moe_w_router=moe_w_router, moe_w_gate=moe_w_gate, moe_w_up=moe_w_up, moe_w_down=moe_w_down)
    w = _prep_weights(p)
    return (_trunk(x_prompt, mem_prompt, p, w), _trunk(x_sample, mem_sample, p, w))
```

```python
import functools
import math

import numpy as np
import jax
import jax.numpy as jnp
from jax import lax
from jax.experimental import pallas as pl
from jax.experimental.pallas import tpu as pltpu

F32 = jnp.float32
BF = jnp.bfloat16
I32 = jnp.int32

D_MODEL = 1024
DEPTH = 4
GRID_W = 64
N_MIXERS = 4
LN_EPS = 1e-5
RMS_EPS = 1e-6
ALPHA = (2.0 * DEPTH) ** 0.25

A_HEADS = 16
A_KV_HEADS = 4
A_HEAD_DIM = 64
ROPE_BASE = 10000.0

B_PATTERNS = ((128, 1), (512, 4), (2048, 16))
B_HEADS_PER_GROUP = 4
B_HEAD_DIM = 64

C_GROUP = 16
C_N_GROUPS = D_MODEL // C_GROUP
C_STATE = 64
C_CHUNK = 16
_C_GL = 128 // C_GROUP

D_HEADS = 12
D_QK_DIM = 32
D_V_DIM = 64

REL_BUCKETS = 32
REL_MAX_DIST = 128

X_HEADS = 4
X_HEAD_DIM = D_MODEL // X_HEADS

N_EXPERTS = 16
EC_CAPACITY = 2

LANES = 128
BF16_ROWS = 16
VMEM_LIMIT = 56 * 1024 * 1024
NEG = -1e30
LOG2E = math.log2(math.e)


def _params(*sem):
    return pltpu.CompilerParams(dimension_semantics=sem, vmem_limit_bytes=VMEM_LIMIT)


def _tile(n, pref):
    t = min(n, pref)
    assert n % t == 0, (n, pref)
    return t


def _ln_rows(v, g, b):
    mu = jnp.mean(v, -1, keepdims=True)
    c = v - mu
    var = jnp.mean(c * c, -1, keepdims=True)
    return c * lax.rsqrt(var + LN_EPS) * g + b


def _dot_nt(a, b):
    return lax.dot_general(a, b, (((1,), (1,)), ((), ())), preferred_element_type=F32)


def _dot(a, b):
    return jnp.dot(a, b, preferred_element_type=F32)


_ONE_LANE = 64


def _with_ones_lane(y):
    lane = lax.broadcasted_iota(I32, y.shape, 1)
    return jnp.where(lane % LANES == _ONE_LANE, 1.0, y)


def _proj_kernel(x_ref, w_ref, o_ref, *, ones_from):
    y = _dot(x_ref[...].astype(BF), w_ref[...])
    if ones_from is None:
        o_ref[...] = y.astype(o_ref.dtype)
    else:
        o_ref[:, :ones_from] = y[:, :ones_from].astype(o_ref.dtype)
        o_ref[:, ones_from:] = _with_ones_lane(y[:, ones_from:]).astype(o_ref.dtype)


def _proj(x, w, tm=512, ones_from=None):
    n, k = x.shape
    m = w.shape[1]
    tm = _tile(n, tm)
    return pl.pallas_call(
        functools.partial(_proj_kernel, ones_from=ones_from),
        grid=(n // tm,),
        in_specs=[pl.BlockSpec((tm, k), lambda i: (i, 0)),
                  pl.BlockSpec((k, m), lambda i: (0, 0))],
        out_specs=pl.BlockSpec((tm, m), lambda i: (i, 0)),
        out_shape=jax.ShapeDtypeStruct((n, m), BF),
        compiler_params=_params("parallel"),
        name="proj",
    )(x, w)


def _post_kernel(x_ref, o_ref, w_ref, g_ref, b_ref, y_ref):
    h = _dot(o_ref[...], w_ref[...])
    y_ref[...] = _ln_rows(ALPHA * x_ref[...] + h, g_ref[...], b_ref[...])


def _post(x, o, w, g, b, tm=512):
    n = x.shape[0]
    ko = o.shape[1]
    tm = _tile(n, tm)
    return pl.pallas_call(
        _post_kernel,
        grid=(n // tm,),
        in_specs=[pl.BlockSpec((tm, D_MODEL), lambda i: (i, 0)),
                  pl.BlockSpec((tm, ko), lambda i: (i, 0)),
                  pl.BlockSpec((ko, D_MODEL), lambda i: (0, 0)),
                  pl.BlockSpec((1, D_MODEL), lambda i: (0, 0)),
                  pl.BlockSpec((1, D_MODEL), lambda i: (0, 0))],
        out_specs=pl.BlockSpec((tm, D_MODEL), lambda i: (i, 0)),
        out_shape=jax.ShapeDtypeStruct((n, D_MODEL), F32),
        compiler_params=_params("parallel"),
        name="post",
    )(x, o, w, g, b)


def _post_b_kernel(x_ref, o0_ref, o1_ref, o2_ref, l0_ref, l1_ref, l2_ref, w_ref, g_ref, b_ref, y_ref):
    l0, l1, l2 = l0_ref[...], l1_ref[...], l2_ref[...]
    m = jnp.maximum(jnp.maximum(l0, l1), l2)
    e0, e1, e2 = jnp.exp(l0 - m), jnp.exp(l1 - m), jnp.exp(l2 - m)
    inv = 1.0 / (e0 + e1 + e2)
    gw = B_HEADS_PER_GROUP * LANES
    h = _dot((o0_ref[...].astype(F32) * (e0 * inv)).astype(BF), w_ref[0:gw, :])
    h = h + _dot((o1_ref[...].astype(F32) * (e1 * inv)).astype(BF), w_ref[gw:2 * gw, :])
    h = h + _dot((o2_ref[...].astype(F32) * (e2 * inv)).astype(BF), w_ref[2 * gw:3 * gw, :])
    y_ref[...] = _ln_rows(ALPHA * x_ref[...] + h, g_ref[...], b_ref[...])


def _post_b(x, os_, ls_, w, g, b, tm=512):
    n = x.shape[0]
    tm = _tile(n, tm)
    gw = B_HEADS_PER_GROUP * LANES
    row = lambda i: (i, 0)
    fixed = lambda i: (0, 0)
    return pl.pallas_call(
        _post_b_kernel,
        grid=(n // tm,),
        in_specs=[pl.BlockSpec((tm, D_MODEL), row)] + [pl.BlockSpec((tm, gw), row)] * 6
        + [pl.BlockSpec((3 * gw, D_MODEL), fixed), pl.BlockSpec((1, D_MODEL), fixed),
           pl.BlockSpec((1, D_MODEL), fixed)],
        out_specs=pl.BlockSpec((tm, D_MODEL), row),
        out_shape=jax.ShapeDtypeStruct((n, D_MODEL), F32),
        compiler_params=_params("parallel"),
        name="post_b",
    )(x, *os_, *ls_, w, g, b)


def _post_c_kernel(x_ref, ys_ref, d_ref, w_ref, g_ref, b_ref, y_ref):
    x = x_ref[...]
    z = jax.nn.gelu(ys_ref[...] + d_ref[...] * x).astype(BF)
    h = _dot(z, w_ref[...])
    hh = h[:, :D_MODEL] * jax.nn.sigmoid(h[:, D_MODEL:])
    y_ref[...] = _ln_rows(ALPHA * x + hh, g_ref[...], b_ref[...])


def _post_c(x, ys, d, w, g, b, tm=512):
    n = x.shape[0]
    tm = _tile(n, tm)
    row = lambda i: (i, 0)
    fixed = lambda i: (0, 0)
    return pl.pallas_call(
        _post_c_kernel,
        grid=(n // tm,),
        in_specs=[pl.BlockSpec((tm, D_MODEL), row), pl.BlockSpec((tm, D_MODEL), row),
                  pl.BlockSpec((1, D_MODEL), fixed), pl.BlockSpec((D_MODEL, 2 * D_MODEL), fixed),
                  pl.BlockSpec((1, D_MODEL), fixed), pl.BlockSpec((1, D_MODEL), fixed)],
        out_specs=pl.BlockSpec((tm, D_MODEL), row),
        out_shape=jax.ShapeDtypeStruct((n, D_MODEL), F32),
        compiler_params=_params("parallel"),
        name="post_c",
    )(x, ys, d, w, g, b)


_A_QK_TILES = A_HEADS + A_KV_HEADS
_A_COLS = (_A_QK_TILES + A_KV_HEADS) * LANES


def _rope_partner():
    d = np.arange(A_HEAD_DIM)
    e = d % (A_HEAD_DIM // 2)
    lo = e < A_HEAD_DIM // 4
    return np.where(lo, d + A_HEAD_DIM // 4, d - A_HEAD_DIM // 4), np.where(lo, -1.0, 1.0).astype(np.float32)


def _prep_a(w_qkv, q_gain, k_gain, w_o):
    partner, sign = _rope_partner()
    nqk = _A_QK_TILES * A_HEAD_DIM
    wqk = w_qkv[:, :nqk].reshape(D_MODEL, _A_QK_TILES, A_HEAD_DIM)
    wsw = wqk[:, :, partner] * sign
    wqk = jnp.concatenate([wqk, wsw], -1).reshape(D_MODEL, _A_QK_TILES * LANES)
    wv = w_qkv[:, nqk:].reshape(D_MODEL, A_KV_HEADS, A_HEAD_DIM)
    wv = jnp.concatenate([wv, jnp.zeros_like(wv)], -1).reshape(D_MODEL, A_KV_HEADS * LANES)
    w = jnp.concatenate([wqk, wv], 1).astype(BF)
    gq = jnp.concatenate([q_gain, q_gain[partner]]) * (A_HEAD_DIM ** -0.5 * 0.5 * LOG2E)
    gk = jnp.concatenate([k_gain, k_gain[partner]])
    gains = jnp.stack([gq, gk], 0)
    wo = w_o.reshape(A_HEADS, A_HEAD_DIM, D_MODEL)
    wo = jnp.concatenate([wo, jnp.zeros_like(wo)], 1).reshape(A_HEADS * LANES, D_MODEL).astype(BF)
    return w, gains, wo


def _rope_table(s):
    pos = jnp.arange(s)
    rows, cols = (pos // GRID_W).astype(F32), (pos % GRID_W).astype(F32)
    half = A_HEAD_DIM // 2
    freqs = ROPE_BASE ** (-jnp.arange(0, half, 2, dtype=F32) / half)
    ang_r = rows[:, None] * freqs
    ang_c = cols[:, None] * freqs
    ang = jnp.concatenate([ang_r, ang_r, ang_c, ang_c], -1)
    return jnp.concatenate([jnp.cos(ang), jnp.sin(ang)], -1)


def _proj_a_kernel(x_ref, w_ref, cs_ref, g_ref, o_ref):
    xb = x_ref[...].astype(BF)
    cs = cs_ref[...]
    gq = cs * g_ref[0:1, :]
    gk = cs * g_ref[1:2, :]
    per = 4
    for g in range(_A_QK_TILES // per):
        y = _dot(xb, w_ref[:, LANES * per * g:LANES * per * (g + 1)])
        for hh in range(per):
            h = per * g + hh
            t = y[:, LANES * hh:LANES * (hh + 1)]
            r = lax.rsqrt(jnp.sum(t * t, -1, keepdims=True) * (1.0 / LANES) + RMS_EPS)
            e = t * r * (gq if h < A_HEADS else gk)
            o_ref[:, LANES * h:LANES * (h + 1)] = (e + pltpu.roll(e, LANES // 2, 1)).astype(BF)
    yv = _dot(xb, w_ref[:, _A_QK_TILES * LANES:])
    o_ref[:, _A_QK_TILES * LANES:] = _with_ones_lane(yv).astype(BF)


def _proj_a(x, w, cs, gains, s, tm=512):
    n = x.shape[0]
    tm = _tile(s, tm)
    per = s // tm
    return pl.pallas_call(
        _proj_a_kernel,
        grid=(n // tm,),
        in_specs=[pl.BlockSpec((tm, D_MODEL), lambda i: (i, 0)),
                  pl.BlockSpec((D_MODEL, _A_COLS), lambda i: (0, 0)),
                  pl.BlockSpec((tm, LANES), lambda i: (i % per, 0)),
                  pl.BlockSpec((2, LANES), lambda i: (0, 0))],
        out_specs=pl.BlockSpec((tm, _A_COLS), lambda i: (i, 0)),
        out_shape=jax.ShapeDtypeStruct((n, _A_COLS), BF),
        compiler_params=_params("parallel"),
        name="proj_a",
    )(x, w, cs, gains)


def _softmax_step(s, v, m_ref, acc_ref):
    tk = s.shape[1]
    m_old = m_ref[...]
    m_new = jnp.maximum(m_old, jnp.max(s, -1, keepdims=True))
    p = jnp.concatenate([jnp.exp2(s[:, LANES * c:LANES * (c + 1)] - m_new).astype(BF) for c in range(tk // LANES)],
                        axis=1)
    acc_ref[...] = jnp.exp2(m_old - m_new) * acc_ref[...] + _dot(p, v)
    m_ref[...] = m_new


def _softmax_init(m_ref, acc_ref):
    m_ref[...] = jnp.full(m_ref.shape, -jnp.inf, F32)
    acc_ref[...] = jnp.zeros_like(acc_ref)


def _flash_a_kernel(q_ref, k_ref, v_ref, o_ref, m_ref, acc_ref, *, tk, nk):
    tq = q_ref.shape[0]
    rep = A_HEADS // A_KV_HEADS
    q = jnp.concatenate([q_ref[:, LANES * r:LANES * (r + 1)] for r in range(rep)], axis=0)
    _softmax_init(m_ref, acc_ref)

    def body(j, carry):
        off = pl.multiple_of(j * tk, tk)
        _softmax_step(_dot_nt(q, k_ref[pl.ds(off, tk), :]), v_ref[pl.ds(off, tk), :], m_ref, acc_ref)
        return carry

    lax.fori_loop(0, nk, body, 0, unroll=4)
    acc = acc_ref[...]
    o = acc / acc[:, _ONE_LANE:_ONE_LANE + 1]
    for r in range(rep):
        o_ref[:, LANES * r:LANES * (r + 1)] = o[r * tq:(r + 1) * tq].astype(BF)


def _flash_a(qkv, tq=256, tk=1024):
    b, s, _ = qkv.shape
    tq, tk = _tile(s, tq), _tile(s, tk)
    rep = A_HEADS // A_KV_HEADS
    gw = rep * LANES
    return pl.pallas_call(
        functools.partial(_flash_a_kernel, tk=tk, nk=s // tk),
        grid=(b, A_KV_HEADS, s // tq),
        in_specs=[pl.BlockSpec((None, tq, gw), lambda bi, g, i: (bi, i, g)),
                  pl.BlockSpec((None, s, LANES), lambda bi, g, i: (bi, 0, A_HEADS + g)),
                  pl.BlockSpec((None, s, LANES), lambda bi, g, i: (bi, 0, _A_QK_TILES + g))],
        out_specs=pl.BlockSpec((None, tq, gw), lambda bi, g, i: (bi, i, g)),
        out_shape=jax.ShapeDtypeStruct((b, s, A_HEADS * LANES), BF),
        scratch_shapes=[pltpu.VMEM((rep * tq, LANES), F32)] * 2,
        compiler_params=_params("parallel", "parallel", "arbitrary"),
        name="flash_a",
    )(qkv, qkv, qkv)


def _bucket(rel):
    half = REL_BUCKETS // 2
    max_exact = half // 2
    n = jnp.abs(rel)
    large = max_exact + (jnp.log(jnp.maximum(n, 1).astype(F32) / max_exact)
                         / math.log(REL_MAX_DIST / max_exact) * (half - max_exact)).astype(I32)
    large = jnp.minimum(large, half - 1)
    return jnp.where(rel > 0, half, 0) + jnp.where(n < max_exact, n, large)


_B_TILES = 3 * len(B_PATTERNS) * B_HEADS_PER_GROUP
_B_GW = B_HEADS_PER_GROUP * LANES
_B_T = 256


def _pad_heads(w, heads, dim):
    w = w.reshape(w.shape[0], heads, dim)
    return jnp.concatenate([w, jnp.zeros((w.shape[0], heads, LANES - dim), w.dtype)], -1).reshape(
        w.shape[0], heads * LANES)


def _prep_b(w_qkv, w_o):
    nh = len(B_PATTERNS) * B_HEADS_PER_GROUP
    c = nh * B_HEAD_DIM
    wq = _pad_heads(w_qkv[:, :c] * (B_HEAD_DIM ** -0.5), nh, B_HEAD_DIM)
    wk = _pad_heads(w_qkv[:, c:2 * c], nh, B_HEAD_DIM)
    wv = _pad_heads(w_qkv[:, 2 * c:], nh, B_HEAD_DIM)
    w = jnp.concatenate([wq, wk, wv], 1).astype(BF)
    wo = w_o.reshape(nh, B_HEAD_DIM, D_MODEL)
    wo = jnp.concatenate([wo, jnp.zeros_like(wo)], 1).reshape(nh * LANES, D_MODEL).astype(BF)
    return w, wo


def _toeplitz(vec, t):
    flat = jnp.tile(vec, (1,) * (vec.ndim - 1) + (t,))[..., :t * (2 * t - 1)]
    return flat.reshape(vec.shape[:-1] + (t, 2 * t - 1))[..., :t]


def _band_tiles(g, t):
    window, dil = B_PATTERNS[g]
    return pl.cdiv((window // (2 * dil)) * dil, t)


def _band_bias(rel_bias, g, t):
    window, dil = B_PATTERNS[g]
    reach = (window // (2 * dil)) * dil
    n = _band_tiles(g, t)
    rel0 = (jnp.arange(2 * t) + t) % (2 * t) - t
    rel = jnp.arange(-n, n + 1)[:, None] * t + rel0[None, :]
    bias = rel_bias[_bucket(rel)][:, :, g * B_HEADS_PER_GROUP:(g + 1) * B_HEADS_PER_GROUP]
    bias = jnp.where(((rel % dil == 0) & (jnp.abs(rel) <= reach))[:, :, None], bias, NEG)
    return _toeplitz(jnp.transpose(bias, (2, 0, 1)).astype(F32), t)


def _band_kernel(q_ref, k_ref, v_ref, bias_ref, o_ref, lse_ref, *, n, nk):
    t = q_ref.shape[0]
    i = pl.program_id(2)
    q = q_ref[...]
    logits, offs = [], []
    for o in range(-n, n + 1):
        j = i + o
        off = pl.multiple_of(jnp.clip(j, 0, nk - 1) * t, t)
        s = _dot_nt(q, k_ref[pl.ds(off, t), :]) + bias_ref[o + n]
        logits.append(jnp.where((j >= 0) & (j < nk), s, NEG))
        offs.append(off)
    m = logits[0].max(-1, keepdims=True)
    for s in logits[1:]:
        m = jnp.maximum(m, s.max(-1, keepdims=True))
    l = jnp.zeros((t, 1), F32)
    acc = jnp.zeros((t, LANES), F32)
    for s, off in zip(logits, offs):
        p = jnp.exp(s - m)
        l = l + jnp.sum(p, -1, keepdims=True)
        acc = acc + _dot(p.astype(BF), v_ref[pl.ds(off, t), :])
    o_ref[...] = (acc / l).astype(BF)
    lse_ref[...] = jnp.broadcast_to(m + jnp.log(l), (t, LANES))


def _band_attention(qkv, bias, g, t):
    b, s, _ = qkv.shape
    nh = len(B_PATTERNS) * B_HEADS_PER_GROUP
    n = _band_tiles(g, t)
    spec = lambda base: pl.BlockSpec((None, s, LANES), lambda bi, h, i: (bi, 0, base + g * B_HEADS_PER_GROUP + h))
    out_spec = pl.BlockSpec((None, t, LANES), lambda bi, h, i: (bi, i, h))
    o, lse = pl.pallas_call(
        functools.partial(_band_kernel, n=n, nk=s // t),
        grid=(b, B_HEADS_PER_GROUP, s // t),
        in_specs=[pl.BlockSpec((None, t, LANES), lambda bi, h, i: (bi, i, g * B_HEADS_PER_GROUP + h)),
                  spec(nh), spec(2 * nh),
                  pl.BlockSpec((None, 2 * n + 1, t, t), lambda bi, h, i: (h, 0, 0, 0))],
        out_specs=[out_spec, out_spec],
        out_shape=[jax.ShapeDtypeStruct((b, s, _B_GW), BF), jax.ShapeDtypeStruct((b, s, _B_GW), F32)],
        compiler_params=_params("parallel", "parallel", "arbitrary"),
        name="band_%d" % g,
    )(qkv, qkv, qkv, bias)
    return o.reshape(b * s, _B_GW), lse.reshape(b * s, _B_GW)


def _prep_c(lam_re, lam_im, log_dt, b_re, b_im, c_re, c_im):
    hp = lax.Precision.HIGHEST
    L, P, C = C_CHUNK, C_STATE, C_GROUP
    lr, li = lam_re.astype(F32), lam_im.astype(F32)
    dt = jnp.exp(log_dt.astype(F32))[..., None]
    mag = jnp.exp(lr * dt)
    ar, ai = mag * jnp.cos(li * dt), mag * jnp.sin(li * dt)
    den = lr * lr + li * li
    zr = ((ar - 1.0) * lr + ai * li) / den
    zi = (ai * lr - (ar - 1.0) * li) / den
    br, bi = b_re.astype(F32), b_im.astype(F32)
    bbr = zr[..., None] * br - zi[..., None] * bi
    bbi = zr[..., None] * bi + zi[..., None] * br
    cr, ci = c_re.astype(F32), c_im.astype(F32)
    prs, pis = [jnp.ones_like(ar)], [jnp.zeros_like(ai)]
    for _ in range(L):
        pr_, pi_ = prs[-1], pis[-1]
        prs.append(ar * pr_ - ai * pi_)
        pis.append(ar * pi_ + ai * pr_)
    pr, pi = jnp.stack(prs, 0), jnp.stack(pis, 0)

    def lag(pr_k, pi_k):
        tr = pr_k[..., None] * bbr - pi_k[..., None] * bbi
        ti = pr_k[..., None] * bbi + pi_k[..., None] * bbr
        return (jnp.einsum('dgop,kdgpi->kdgoi', cr, tr, precision=hp)
                - jnp.einsum('dgop,kdgpi->kdgoi', ci, ti, precision=hp))

    kern = lag(pr[:L], pi[:L])
    jj = np.arange(L)[:, None]
    ii = np.arange(L)[None, :]
    kf = kern[:, 0][np.clip(ii - jj, 0, L - 1)] * jnp.asarray((ii >= jj)[:, :, None, None, None], F32)
    kb = kern[:, 1][np.clip(jj - ii, 0, L - 1)] * jnp.asarray((jj >= ii)[:, :, None, None, None], F32)
    m_intra = jnp.transpose(kf + kb, (2, 0, 4, 1, 3)).reshape(C_N_GROUPS, L * C, L * C)

    def summ(d, powers):
        pr_k, pi_k = pr[powers, d], pi[powers, d]
        sr = pr_k[..., None] * bbr[d] - pi_k[..., None] * bbi[d]
        si = pr_k[..., None] * bbi[d] + pi_k[..., None] * bbr[d]
        s = jnp.concatenate([sr, si], 2)
        return jnp.transpose(s, (1, 0, 3, 2)).reshape(C_N_GROUPS, L * C, 2 * P)

    w_sum = jnp.concatenate([summ(0, np.arange(L - 1, -1, -1)), summ(1, np.arange(L))], -1)

    def outw(d, powers):
        pr_k, pi_k = pr[powers, d], pi[powers, d]
        wr = cr[d][None] * pr_k[:, :, None, :] - ci[d][None] * pi_k[:, :, None, :]
        wi = -(cr[d][None] * pi_k[:, :, None, :] + ci[d][None] * pr_k[:, :, None, :])
        w = jnp.concatenate([wr, wi], -1)
        return jnp.transpose(w, (1, 3, 0, 2)).reshape(C_N_GROUPS, 2 * P, L * C)

    w_state = jnp.concatenate([outw(0, np.arange(1, L + 1)), outw(1, np.arange(L, 0, -1))], 1)

    nq = C_N_GROUPS // _C_GL
    npair = _C_GL // 2
    blk = L * _C_GL * C
    lane = jnp.arange(LANES)
    gsel = (lane[None, :] // C == jnp.arange(_C_GL)[:, None])
    src = jnp.tile(m_intra.astype(BF).reshape(nq, _C_GL, L, C, L, C), (1, 1, 1, 1, 1, _C_GL))
    src = jnp.where(gsel[None, :, None, None, None, :], src, 0)
    w_intra = jnp.transpose(src, (0, 2, 1, 3, 4, 5)).reshape(nq, blk, blk)
    psel = (lane[None, :] // P == jnp.arange(2)[:, None])
    ksel = jnp.eye(npair, dtype=bool)
    src = jnp.tile(w_sum.astype(BF).reshape(nq, npair, 2, L, C, 1, 4, P), (1, 1, 1, 1, 1, npair, 1, 2))
    src = jnp.where(psel[None, None, :, None, None, None, None, :]
                    & ksel[None, :, None, None, None, :, None, None], src, 0)
    w_sum = jnp.transpose(src, (0, 3, 1, 2, 4, 5, 6, 7)).reshape(nq, blk, _C_GL * 4 * P)
    osel = (lane[None, None, :] // C
            == (2 * jnp.arange(npair)[:, None, None] + jnp.arange(2)[None, :, None]))
    src = jnp.tile(w_state.astype(BF).reshape(nq, npair, 2, 4, P, L, C), (1, 1, 1, 1, 1, 1, _C_GL))
    src = jnp.where(osel[None, :, :, None, None, None, :], src, 0)
    w_state = jnp.transpose(src, (0, 1, 3, 2, 4, 5, 6)).reshape(nq, _C_GL * 4 * P, blk)
    dec = jnp.stack([pr[L, 0], pi[L, 0], pr[L, 1], pi[L, 1]], 0).reshape(4, nq * npair, 2 * P)
    dec = jnp.broadcast_to(jnp.transpose(dec, (1, 0, 2))[:, :, None, :], (nq * npair, 4, 8, 2 * P))
    return w_sum.astype(BF), w_intra.astype(BF), w_state.astype(BF), dec


def _chunk_rows(x_ref, nb, rc):
    L = C_CHUNK
    rows = [jnp.concatenate([x_ref[b, pl.ds(j, rc, stride=L), :].astype(BF) for j in range(L)], axis=1)
            for b in range(nb)]
    return jnp.concatenate(rows, axis=0)


def _s5_sum_kernel(x_ref, w_ref, s_ref, *, nb, rc):
    s = _dot(_chunk_rows(x_ref, nb, rc), w_ref[...])
    for b in range(nb):
        for t in range(s_ref.shape[0]):
            s_ref[t, pl.ds(b, rc, stride=nb), :] = s[b * rc:(b + 1) * rc, LANES * t:LANES * (t + 1)]


def _s5_scan_kernel(s_ref, dec_ref, e_ref, *, nc, nb):
    units = s_ref.shape[0] // 4
    dec = [[dec_ref[u, k, 0:nb, :] for k in range(4)] for u in range(units)]

    def body(t, carry):
        rf = pl.ds(pl.multiple_of(t * nb, nb), nb)
        rb = pl.ds(pl.multiple_of((nc - 1 - t) * nb, nb), nb)
        new = []
        for u in range(units):
            fr, fi, br, bi = carry[4 * u:4 * u + 4]
            arf, aif, arb, aib = dec[u]
            e_ref[4 * u, rf, :] = fr
            e_ref[4 * u + 1, rf, :] = fi
            e_ref[4 * u + 2, rb, :] = br
            e_ref[4 * u + 3, rb, :] = bi
            new += [fr * arf - fi * aif + s_ref[4 * u, rf, :], fi * arf + fr * aif + s_ref[4 * u + 1, rf, :],
                    br * arb - bi * aib + s_ref[4 * u + 2, rb, :], bi * arb + br * aib + s_ref[4 * u + 3, rb, :]]
        return tuple(new)

    z = jnp.zeros((nb, LANES), F32)
    lax.fori_loop(0, nc, body, (z,) * (4 * units))


def _s5_out_kernel(x_ref, e_ref, wi_ref, wc_ref, y_ref, *, nb, rc):
    L = C_CHUNK
    half = pl.program_id(2)
    xc = _chunk_rows(x_ref, nb, rc)
    ec = jnp.concatenate(
        [jnp.concatenate([e_ref[t, pl.ds(b, rc, stride=nb), :].astype(BF) for t in range(e_ref.shape[0])], axis=1)
         for b in range(nb)], axis=0)
    y = _dot(xc, wi_ref[...]) + _dot(ec, wc_ref[...])
    for b in range(nb):
        for ii in range(L // 2):
            y_ref[b, pl.ds(half * (L // 2) + ii, rc, stride=L), :] = y[b * rc:(b + 1) * rc, LANES * ii:LANES * (ii + 1)]


def _s5(x, b, s, w_sum, w_intra, w_state, dec):
    L = C_CHUNK
    nc = s // L
    nq = D_MODEL // LANES
    nsl = w_sum.shape[2] // LANES
    blk = L * LANES
    rc = _tile(nc, max(8, 512 // b))
    x3 = x.reshape(b, s, D_MODEL)
    x_spec = pl.BlockSpec((b, rc * L, LANES), lambda q, c, *_: (0, c, q))
    sums = pl.pallas_call(
        functools.partial(_s5_sum_kernel, nb=b, rc=rc),
        grid=(nq, nc // rc),
        in_specs=[x_spec, pl.BlockSpec((None, blk, nsl * LANES), lambda q, c: (q, 0, 0))],
        out_specs=pl.BlockSpec((nsl, rc * b, LANES), lambda q, c: (q, c, 0)),
        out_shape=jax.ShapeDtypeStruct((nq * nsl, nc * b, LANES), F32),
        compiler_params=_params("parallel", "arbitrary"),
        name="s5_sum",
    )(x3, w_sum)
    upb = 2
    ent = pl.pallas_call(
        functools.partial(_s5_scan_kernel, nc=nc, nb=b),
        grid=(nq * nsl // (4 * upb),),
        in_specs=[pl.BlockSpec((4 * upb, nc * b, LANES), lambda i: (i, 0, 0)),
                  pl.BlockSpec((upb, 4, 8, LANES), lambda i: (i, 0, 0, 0))],
        out_specs=pl.BlockSpec((4 * upb, nc * b, LANES), lambda i: (i, 0, 0)),
        out_shape=jax.ShapeDtypeStruct((nq * nsl, nc * b, LANES), F32),
        compiler_params=_params("parallel"),
        name="s5_scan",
    )(sums, dec)
    y = pl.pallas_call(
        functools.partial(_s5_out_kernel, nb=b, rc=rc),
        grid=(nq, nc // rc, 2),
        in_specs=[x_spec,
                  pl.BlockSpec((nsl, rc * b, LANES), lambda q, c, h: (q, c, 0)),
                  pl.BlockSpec((None, blk, blk // 2), lambda q, c, h: (q, 0, h)),
                  pl.BlockSpec((None, nsl * LANES, blk // 2), lambda q, c, h: (q, 0, h))],
        out_specs=pl.BlockSpec((b, rc * L, LANES), lambda q, c, h: (0, c, q)),
        out_shape=jax.ShapeDtypeStruct((b, s, D_MODEL), F32),
        compiler_params=_params("parallel", "arbitrary", "arbitrary"),
        name="s5_out",
    )(x3, ent, w_intra, w_state)
    return y.reshape(b * s, D_MODEL)


_D_T = 512


def _prep_d(w_qkv, norm_gain, w_o, lambda_init):
    qk_w = D_HEADS * 2 * D_QK_DIM
    wq = _pad_heads(w_qkv[:, :qk_w] * (D_QK_DIM ** -0.5 * LOG2E), D_HEADS, 2 * D_QK_DIM)
    wk = _pad_heads(w_qkv[:, qk_w:2 * qk_w], D_HEADS, 2 * D_QK_DIM)
    wv = _pad_heads(w_qkv[:, 2 * qk_w:], D_HEADS, D_V_DIM)
    w = jnp.concatenate([wq, wk, wv], 1).astype(BF)
    gain = jnp.concatenate([norm_gain * (1.0 - lambda_init), jnp.zeros((LANES - D_V_DIM,), F32)])[None, :]
    wo = w_o.reshape(D_HEADS, D_V_DIM, D_MODEL)
    wo = jnp.concatenate([wo, jnp.zeros_like(wo)], 1).reshape(D_HEADS * LANES, D_MODEL).astype(BF)
    return w, gain, wo


def _diff_bias(rel_bias, t):
    rel0 = (jnp.arange(2 * t) + t) % (2 * t) - t
    rel = jnp.arange(-2, 3)[:, None] * t + rel0[None, :]
    return jnp.transpose(rel_bias[_bucket(rel)] * LOG2E, (2, 0, 1)).astype(F32)


def _flash_d_kernel(q_ref, k_ref, v_ref, vec_ref, lam_ref, gain_ref, o_ref, m_ref, acc_ref, bias_ref, *, kt, nk,
                    lambda_init):
    t = q_ref.shape[0]
    i = pl.program_id(2)

    @pl.when(i == 0)
    def _():
        for d in range(bias_ref.shape[0]):
            full = jnp.broadcast_to(vec_ref[d:d + 1, :], (t, 2 * t))
            bias_ref[d] = pltpu.roll(full, 0, 1, stride=1, stride_axis=0)[:, :t]

    qv = q_ref[...].astype(F32)
    lane = lax.broadcasted_iota(I32, (t, LANES), 1)
    q0 = jnp.where(lane < D_QK_DIM, qv, 0.0).astype(BF)
    q1 = jnp.where((lane >= D_QK_DIM) & (lane < 2 * D_QK_DIM), qv, 0.0).astype(BF)
    q = jnp.concatenate([q0, q1], axis=0)
    _softmax_init(m_ref, acc_ref)

    def body(j, carry):
        off = pl.multiple_of(j * (kt * t), kt * t)
        bias = jnp.concatenate([bias_ref[jnp.clip(j * kt + c - i, -2, 2) + 2] for c in range(kt)], axis=1)
        s = _dot_nt(q, k_ref[pl.ds(off, kt * t), :])
        s = (s.reshape(2, t, kt * t) + bias[None]).reshape(2 * t, kt * t)
        _softmax_step(s, v_ref[pl.ds(off, kt * t), :], m_ref, acc_ref)
        return carry

    lax.fori_loop(0, nk, body, 0, unroll=4)
    lf = lam_ref[...]
    lam = (jnp.exp(jnp.sum(lf[0:1] * lf[1:2], keepdims=True))
           - jnp.exp(jnp.sum(lf[2:3] * lf[3:4], keepdims=True)) + lambda_init)
    acc = acc_ref[...]
    on = acc / acc[:, _ONE_LANE:_ONE_LANE + 1]
    o = jnp.where(lane < D_V_DIM, on[:t] - lam * on[t:], 0.0)
    ms = jnp.sum(o * o, -1, keepdims=True) * (1.0 / D_V_DIM)
    o_ref[...] = (o * lax.rsqrt(ms + RMS_EPS) * gain_ref[...]).astype(BF)


def _flash_d(qkv, bias, lam, gain, lambda_init):
    b, s, _ = qkv.shape
    t = bias.shape[-1] // 2
    kt = 2 if s % (2 * t) == 0 else 1
    return pl.pallas_call(
        functools.partial(_flash_d_kernel, kt=kt, nk=s // (kt * t), lambda_init=lambda_init),
        grid=(b, D_HEADS, s // t),
        in_specs=[pl.BlockSpec((None, t, LANES), lambda bi, h, i: (bi, i, h)),
                  pl.BlockSpec((None, s, LANES), lambda bi, h, i: (bi, 0, D_HEADS + h)),
                  pl.BlockSpec((None, s, LANES), lambda bi, h, i: (bi, 0, 2 * D_HEADS + h)),
                  pl.BlockSpec((None, 5, 2 * t), lambda bi, h, i: (h, 0, 0)),
                  pl.BlockSpec((4, D_QK_DIM), lambda bi, h, i: (0, 0)),
                  pl.BlockSpec((1, LANES), lambda bi, h, i: (0, 0))],
        out_specs=pl.BlockSpec((None, t, LANES), lambda bi, h, i: (bi, i, h)),
        out_shape=jax.ShapeDtypeStruct((b, s, D_HEADS * LANES), BF),
        scratch_shapes=[pltpu.VMEM((2 * t, LANES), F32)] * 2 + [pltpu.VMEM((5, t, t), F32)],
        compiler_params=_params("parallel", "parallel", "arbitrary"),
        name="flash_d",
    )(qkv, qkv, qkv, bias, lam, gain)


def _cross_kernel(x_ref, kv_ref, wq_ref, wo_ref, g_ref, b_ref, wr_ref, y_ref, ybf_ref, aff_ref, *, parts):
    tp = x_ref.shape[0] // parts
    wr = wr_ref[...]
    wh = wr.astype(BF)
    wl = (wr - wh.astype(F32)).astype(BF)
    for part in range(parts):
        rows = slice(tp * part, tp * (part + 1))
        x = x_ref[rows, :]
        q = (_dot(x.astype(BF), wq_ref[...]) * (X_HEAD_DIM ** -0.5)).astype(BF)
        outs = []
        for h in range(X_HEADS):
            sl = slice(X_HEAD_DIM * h, X_HEAD_DIM * (h + 1))
            s = _dot_nt(q[:, sl], kv_ref[:, sl])
            p = jnp.exp(s - jnp.max(s, -1, keepdims=True))
            l = jnp.sum(p, -1, keepdims=True)
            vh = kv_ref[:, D_MODEL + X_HEAD_DIM * h:D_MODEL + X_HEAD_DIM * (h + 1)]
            outs.append((_dot(p.astype(BF), vh) / l).astype(BF))
        o = jnp.concatenate(outs, axis=1)
        y = _ln_rows(ALPHA * x + _dot(o, wo_ref[...]), g_ref[...], b_ref[...])
        y_ref[rows, :] = y
        yh = y.astype(BF)
        ybf_ref[rows, :] = yh
        yl = (y - yh.astype(F32)).astype(BF)
        lg = _dot_nt(wh, yh) + _dot_nt(wh, yl) + _dot_nt(wl, yh)
        e = jnp.exp(lg - jnp.max(lg, 0, keepdims=True))
        aff = e / jnp.sum(e, 0, keepdims=True)
        for c in range(tp // LANES):
            aff_ref[part * (tp // LANES) + c] = aff[:, LANES * c:LANES * (c + 1)]


def _cross(x, kv, wq, wo, g, b, wr_t, s, mem_len, tm=1024, parts=2):
    n = x.shape[0]
    tm = _tile(s, tm)
    per = s // tm
    fixed = lambda i: (0, 0)
    return pl.pallas_call(
        functools.partial(_cross_kernel, parts=parts),
        grid=(n // tm,),
        in_specs=[pl.BlockSpec((tm, D_MODEL), lambda i: (i, 0)),
                  pl.BlockSpec((mem_len, 2 * D_MODEL), lambda i: (i // per, 0)),
                  pl.BlockSpec((D_MODEL, D_MODEL), fixed), pl.BlockSpec((D_MODEL, D_MODEL), fixed),
                  pl.BlockSpec((1, D_MODEL), fixed), pl.BlockSpec((1, D_MODEL), fixed),
                  pl.BlockSpec((N_EXPERTS, D_MODEL), fixed)],
        out_specs=[pl.BlockSpec((tm, D_MODEL), lambda i: (i, 0)),
                   pl.BlockSpec((tm, D_MODEL), lambda i: (i, 0)),
                   pl.BlockSpec((tm // LANES, N_EXPERTS, LANES), lambda i: (i, 0, 0))],
        out_shape=[jax.ShapeDtypeStruct((n, D_MODEL), F32),
                   jax.ShapeDtypeStruct((n, D_MODEL), BF),
                   jax.ShapeDtypeStruct((n // LANES, N_EXPERTS, LANES), F32)],
        compiler_params=_params("parallel"),
        name="cross",
    )(x, kv, wq, wo, g, b, wr_t)


def _select_kernel(a_ref, pos_ref, st_ref, *, k, nbits):
    nt = a_ref.shape[0]
    shape = (nt, N_EXPERTS, LANES)
    kf = float(k)

    def keys():
        return lax.bitcast_convert_type(a_ref[...], I32)

    def count(mask):
        c = jnp.sum(jnp.where(mask, 1.0, 0.0), axis=0, keepdims=True)
        return jnp.sum(c, axis=2, keepdims=True)

    def value_step(it, thr):
        cand = thr | jnp.left_shift(jnp.int32(1), 30 - it)
        return jnp.where(count(keys() >= cand) >= kf, cand, thr)

    thr = lax.fori_loop(0, 31, value_step, jnp.zeros((1, N_EXPERTS, 1), I32))
    need = kf - count(keys() > thr)
    idx = lax.broadcasted_iota(I32, shape, 0) * LANES + lax.broadcasted_iota(I32, shape, 2)

    def index_step(it, ithr):
        cand = ithr | jnp.left_shift(jnp.int32(1), nbits - 1 - it)
        return jnp.where(count((keys() == thr) & (idx < cand)) < need, cand, ithr)

    ithr = lax.fori_loop(0, nbits, index_step, jnp.zeros((1, N_EXPERTS, 1), I32))
    thr2, ithr2 = thr[0], ithr[0]
    upper = jnp.where(lax.broadcasted_iota(I32, (LANES, LANES), 0) <= lax.broadcasted_iota(I32, (LANES, LANES), 1),
                      1.0, 0.0).astype(BF)
    lane = lax.broadcasted_iota(I32, (N_EXPERTS, LANES), 1)

    def tile_step(j, carry):
        kj = lax.bitcast_convert_type(a_ref[j], I32)
        sel = (kj > thr2) | ((kj == thr2) & (j * LANES + lane <= ithr2))
        m = jnp.where(sel, 1.0, 0.0)
        inc = _dot(m.astype(BF), upper)
        pos_ref[j] = jnp.where(sel, inc - m + carry, -1.0).astype(I32)
        st_ref[j] = jnp.broadcast_to(carry, (N_EXPERTS, LANES)).astype(I32)
        return carry + inc[:, LANES - 1:LANES]

    lax.fori_loop(0, nt, tile_step, jnp.zeros((N_EXPERTS, 1), F32))


def _select(aff3, k):
    nt = aff3.shape[0]
    nbits = max(1, int(math.ceil(math.log2(nt * LANES))))
    shp = jax.ShapeDtypeStruct(aff3.shape, I32)
    return pl.pallas_call(
        functools.partial(_select_kernel, k=k, nbits=nbits),
        out_shape=[shp, shp],
        compiler_params=pltpu.CompilerParams(vmem_limit_bytes=VMEM_LIMIT),
        name="select",
    )(aff3)


_GATHER_ROWS = LANES + BF16_ROWS
_GATHER_ROWS_SMALL = 3 * BF16_ROWS

_MOE_GROUP = 2


def _moe_ffn_kernel(st_ref, x_ref, pos_ref, aff_ref, wg_ref, wu_ref, wd_ref, ye_ref, buf_ref, gate_ref, *, nb, ncf,
                    nch, sub, ck):
    ep = buf_ref.shape[0]
    grp = pl.program_id(0)
    t = pl.program_id(1)

    @pl.when(t == 0)
    def _():
        buf_ref[...] = jnp.zeros_like(buf_ref)
        gate_ref[...] = jnp.zeros_like(gate_ref)

    @pl.when(t < nb)
    def _():
        def place(s, k, e, base, nrows):
            rows = pl.ds(base, nrows)
            hit = lax.broadcasted_iota(I32, (nrows, LANES), 0) == pos_ref[s, pl.ds(e, 1), :] - base
            xs = x_ref[LANES * s:LANES * (s + 1), :]
            buf_ref[k, rows, :] = buf_ref[k, rows, :] + _dot(jnp.where(hit, 1.0, 0.0).astype(BF), xs).astype(BF)
            gate_ref[k, rows, :] = gate_ref[k, rows, :] + jnp.sum(
                jnp.where(hit, aff_ref[s, pl.ds(e, 1), :], 0.0), -1, keepdims=True)

        def base_of(s, k):
            st = st_ref[grp * ep + k, t * sub + s]
            return pl.multiple_of((st // BF16_ROWS) * BF16_ROWS, BF16_ROWS)

        pairs = [(s, k) for s in range(sub) for k in range(ep)]
        few = functools.reduce(
            jnp.logical_and,
            [st_ref[grp * ep + k, t * sub + s + 1] - base_of(s, k) <= _GATHER_ROWS_SMALL for s, k in pairs])

        @pl.when(few)
        def _():
            for s, k in pairs:
                place(s, k, grp * ep + k, base_of(s, k), _GATHER_ROWS_SMALL)

        @pl.when(jnp.logical_not(few))
        def _():
            for s, k in pairs:
                place(s, k, grp * ep + k, base_of(s, k), _GATHER_ROWS)

    @pl.when(t >= nb)
    def _():
        k = (t - nb) // nch
        c = (t - nb) - k * nch

        @pl.when(c < ncf)
        def _():
            rows = pl.ds(pl.multiple_of(c * ck, ck), ck)
            xe = buf_ref[k, rows, :]
            h = (jax.nn.silu(_dot(xe, wg_ref[...])) * _dot(xe, wu_ref[...])).astype(BF)
            ye_ref[...] = (_dot(h, wd_ref[...]) * gate_ref[k, rows, :]).astype(BF)

        @pl.when(c >= ncf)
        def _():
            ye_ref[...] = jnp.zeros_like(ye_ref)


def _moe_ffn(starts, xbf, pos3, aff3, wg, wu, wd, cap, tb=1024):
    n = xbf.shape[0]
    tb = _tile(n, tb)
    nb = n // tb
    sub = tb // LANES
    ck = _tile(cap, 512)
    ncf = cap // ck
    nch = ncf + pl.cdiv(_WIN_ROWS, ck)
    dff = wg.shape[-1]
    ep = _MOE_GROUP
    blk = lambda g, t, st: (jnp.minimum(t, nb - 1), 0)
    blk3 = lambda g, t, st: (jnp.minimum(t, nb - 1), 0, 0)
    expert = lambda g, t: g * ep + jnp.clip((t - nb) // nch, 0, ep - 1)
    wmap = lambda g, t, st: (expert(g, t), 0, 0)
    return pl.pallas_call(
        functools.partial(_moe_ffn_kernel, nb=nb, ncf=ncf, nch=nch, sub=sub, ck=ck),
        grid_spec=pltpu.PrefetchScalarGridSpec(
            num_scalar_prefetch=1,
            grid=(N_EXPERTS // ep, nb + ep * nch),
            in_specs=[pl.BlockSpec((tb, D_MODEL), blk),
                      pl.BlockSpec((sub, N_EXPERTS, LANES), blk3),
                      pl.BlockSpec((sub, N_EXPERTS, LANES), blk3),
                      pl.BlockSpec((None, D_MODEL, dff), wmap),
                      pl.BlockSpec((None, D_MODEL, dff), wmap),
                      pl.BlockSpec((None, dff, D_MODEL), wmap)],
            out_specs=pl.BlockSpec((None, ck, D_MODEL),
                                   lambda g, t, st: (expert(g, t), jnp.maximum(t - nb, 0) % nch, 0)),
            scratch_shapes=[pltpu.VMEM((ep, cap + _GATHER_ROWS, D_MODEL), BF),
                            pltpu.VMEM((ep, cap + _GATHER_ROWS, 1), F32)]),
        out_shape=jax.ShapeDtypeStruct((N_EXPERTS, nch * ck, D_MODEL), BF),
        compiler_params=_params("arbitrary", "arbitrary"),
        name="moe_ffn",
    )(starts, xbf, pos3, aff3, wg, wu, wd)


_WIN_STEP = 128
_COMB_TILES = 2
_WIN_ROWS = _WIN_STEP + (_COMB_TILES - 1) * LANES + _GATHER_ROWS


def _combine_kernel(st_ref, x_ref, pos_ref, g_ref, b_ref, *rest):
    ye_refs, y_ref = rest[:N_EXPERTS], rest[N_EXPERTS]
    j = pl.program_id(0)
    col = lax.broadcasted_iota(I32, (LANES, _GATHER_ROWS), 1)
    for u in range(_COMB_TILES):
        tok = slice(LANES * u, LANES * (u + 1))
        acc = None
        for e in range(N_EXPERTS):
            st = st_ref[e, j * _COMB_TILES + u]
            base = (st // BF16_ROWS) * BF16_ROWS
            win = (st_ref[e, j * _COMB_TILES] // _WIN_STEP) * _WIN_STEP
            sub = pl.multiple_of(base - win, BF16_ROWS)
            onehot = jnp.where(col == pos_ref[tok, e:e + 1] - base, 1.0, 0.0).astype(BF)
            d = _dot(onehot, ye_refs[e][0, pl.ds(sub, _GATHER_ROWS), :])
            acc = d if acc is None else acc + d
        y_ref[tok, :] = _ln_rows(ALPHA * x_ref[tok, :] + acc, g_ref[...], b_ref[...])


def _combine(starts, x, pos, g, b, ye):
    n = x.shape[0]
    assert ye.shape[1] >= EC_CAPACITY * n // N_EXPERTS + _WIN_ROWS
    tt = _COMB_TILES * LANES
    row = lambda j, st: (j, 0)
    fixed = lambda j, st: (0, 0)

    def ye_spec(e):
        return pl.BlockSpec((pl.Element(1), pl.Element(_WIN_ROWS), pl.Element(D_MODEL)),
                            lambda j, st: (e, (st[e, j * _COMB_TILES] // _WIN_STEP) * _WIN_STEP, 0))

    return pl.pallas_call(
        _combine_kernel,
        grid_spec=pltpu.PrefetchScalarGridSpec(
            num_scalar_prefetch=1,
            grid=(n // tt,),
            in_specs=[pl.BlockSpec((tt, D_MODEL), row),
                      pl.BlockSpec((tt, N_EXPERTS), row),
                      pl.BlockSpec((1, D_MODEL), fixed), pl.BlockSpec((1, D_MODEL), fixed)]
            + [ye_spec(e) for e in range(N_EXPERTS)],
            out_specs=pl.BlockSpec((tt, D_MODEL), row)),
        out_shape=jax.ShapeDtypeStruct((n, D_MODEL), F32),
        compiler_params=_params("arbitrary"),
        name="combine",
    )(starts, x, pos, g, b, *([ye] * N_EXPERTS))


def _moe(x, xg, aff3, wg, wu, wd, g, b):
    n = x.shape[0]
    cap = EC_CAPACITY * n // N_EXPERTS
    pos3, st3 = _select(aff3, cap)
    starts = jnp.concatenate([jnp.transpose(st3[:, :, 0]), jnp.full((N_EXPERTS, 1), cap, I32)], axis=1)
    ye = _moe_ffn(starts, xg, pos3, aff3, wg, wu, wd, cap)
    pos = jnp.transpose(pos3, (0, 2, 1)).reshape(n, N_EXPERTS)
    return _combine(starts, x, pos, g, b, ye)


def _prep_weights(p):
    w = {}
    w['a'] = [_prep_a(p['a_w_qkv'][j], p['a_q_gain'][j], p['a_k_gain'][j], p['a_w_o'][j])
              for j in range(p['a_w_qkv'].shape[0])]
    w['b'] = [_prep_b(p['b_w_qkv'][j], p['b_w_o'][j]) for j in range(p['b_w_qkv'].shape[0])]
    w['c'] = [_prep_c(p['c_lam_re'][j], p['c_lam_im'][j], p['c_log_dt'][j], p['c_b_re'][j], p['c_b_im'][j],
                      p['c_c_re'][j], p['c_c_im'][j]) + (p['c_d'][j][None, :], p['c_w_glu'][j].astype(BF))
              for j in range(p['c_lam_re'].shape[0])]
    w['d'] = []
    for j in range(p['d_w_qkv'].shape[0]):
        layer = N_MIXERS * j + 3
        lambda_init = 0.8 - 0.6 * math.exp(-0.3 * layer)
        w['d'].append(_prep_d(p['d_w_qkv'][j], p['d_norm_gain'][j], p['d_w_o'][j], lambda_init)
                      + (p['d_lam'][j].astype(F32), lambda_init))
    w['diff_bias'] = _diff_bias(p['rel_bias'], _D_T)
    w['x_w_q'] = p['x_w_q'].astype(BF)
    w['x_w_kv'] = p['x_w_kv'].astype(BF)
    w['x_w_o'] = p['x_w_o'].astype(BF)
    w['router_t'] = jnp.transpose(p['moe_w_router'], (0, 2, 1)).astype(F32)
    w['moe_w_gate'] = p['moe_w_gate'].astype(BF)
    w['moe_w_up'] = p['moe_w_up'].astype(BF)
    w['moe_w_down'] = p['moe_w_down'].astype(BF)
    return w


def _trunk(x, mem, p, w):
    b, s, _ = x.shape
    n = b * s
    mem_len = mem.shape[1]
    x = x.reshape(n, D_MODEL)
    mem2 = mem.reshape(b * mem_len, D_MODEL)
    ln_g, ln_b = p['ln_g'], p['ln_b']
    for i in range(DEPTH):
        m, j = i % N_MIXERS, i // N_MIXERS
        g0, b0 = ln_g[i, 0][None, :], ln_b[i, 0][None, :]
        if m == 0:
            wa, gains, wo = w['a'][j]
            qkv = _proj_a(x, wa, _rope_table(s), gains, s)
            o = _flash_a(qkv.reshape(b, s, _A_COLS))
            x = _post(x, o.reshape(n, A_HEADS * LANES), wo, g0, b0)
        elif m == 1:
            wb, wo = w['b'][j]
            qkv = _proj(x, wb).reshape(b, s, _B_TILES * LANES)
            os_, ls_ = [], []
            for g in range(len(B_PATTERNS)):
                t = _tile(s, _B_T)
                o, lse = _band_attention(qkv, _band_bias(p['rel_bias'], g, t), g, t)
                os_.append(o)
                ls_.append(lse)
            x = _post_b(x, os_, ls_, wo, g0, b0)
        elif m == 2:
            w_sum, w_intra, w_state, dec, dskip, wglu = w['c'][j]
            ys = _s5(x, b, s, w_sum, w_intra, w_state, dec)
            x = _post_c(x, ys, dskip, wglu, g0, b0)
        else:
            wd, gain, wo, lam, lambda_init = w['d'][j]
            qkv = _proj(x, wd, ones_from=2 * D_HEADS * LANES).reshape(b, s, 3 * D_HEADS * LANES)
            o = _flash_d(qkv, w['diff_bias'], lam, gain, lambda_init)
            x = _post(x, o.reshape(n, D_HEADS * LANES), wo, g0, b0)
        kv = _proj(mem2, w['x_w_kv'][i], tm=mem_len)
        x, xbf, aff3 = _cross(x, kv, w['x_w_q'][i], w['x_w_o'][i], ln_g[i, 1][None, :], ln_b[i, 1][None, :],
                              w['router_t'][i], s, mem_len)
        x = _moe(x, xbf, aff3, w['moe_w_gate'][i], w['moe_w_up'][i], w['moe_w_down'][i],
                 ln_g[i, 2][None, :], ln_b[i, 2][None, :])
    return x.reshape(b, s, D_MODEL)


def kernel(x_prompt, x_sample, mem_prompt, mem_sample, rel_bias, ln_g, ln_b, a_w_qkv, a_q_gain, a_k_gain, a_w_o, b_w_qkv, b_w_o, c_lam_re, c_lam_im, c_log_dt, c_b_re, c_b_im, c_c_re, c_c_im, c_d, c_w_glu, d_w_qkv, d_lam, d_norm_gain, d_w_o, x_w_q, x_w_kv, x_w_o, moe_w_router, moe_w_gate, moe_w_up, moe_w_down):
    p = dict(rel_bias=rel_bias, ln_g=ln_g, ln_b=ln_b,
             a_w_qkv=a_w_qkv, a_q_gain=a_q_gain, a_k_gain=a_k_gain, a_w_o=a_w_o,
             b_w_qkv=b_w_qkv, b_w_o=b_w_o,
             c_lam_re=c_lam_re, c_lam_im=c_lam_im, c_log_dt=c_log_dt, c_b_re=c_b_re, c_b_im=c_b_im,
             c_c_re=c_c_re, c_c_im=c_c_im, c_d=c_d, c_w_glu=c_w_glu,
             d_w_qkv=d_w_qkv, d_lam=d_lam, d_norm_gain=d_norm_gain, d_w_o=d_w_o,
             x_w_q=x_w_q, x_w_kv=x_w_kv, x_w_o=x_w_o,
             moe_w_router=moe_w_router, moe_w_gate=moe_w_gate, moe_w_up=moe_w_up, moe_w_down=moe_w_down)
    w = _prep_weights(p)
    return (_trunk(x_prompt, mem_prompt, p, w), _trunk(x_sample, mem_sample, p, w))
```

```python
import functools
import math

import numpy as np
import jax
import jax.numpy as jnp
from jax import lax
from jax.experimental import pallas as pl
from jax.experimental.pallas import tpu as pltpu

F32 = jnp.float32
BF = jnp.bfloat16
I32 = jnp.int32

D_MODEL = 1024
DEPTH = 4
GRID_W = 64
N_MIXERS = 4
LN_EPS = 1e-5
RMS_EPS = 1e-6
ALPHA = (2.0 * DEPTH) ** 0.25

A_HEADS = 16
A_KV_HEADS = 4
A_HEAD_DIM = 64
ROPE_BASE = 10000.0

B_PATTERNS = ((128, 1), (512, 4), (2048, 16))
B_HEADS_PER_GROUP = 4
B_HEAD_DIM = 64

C_GROUP = 16
C_N_GROUPS = D_MODEL // C_GROUP
C_STATE = 64
C_CHUNK = 16
_C_GL = 128 // C_GROUP

D_HEADS = 12
D_QK_DIM = 32
D_V_DIM = 64

REL_BUCKETS = 32
REL_MAX_DIST = 128

X_HEADS = 4
X_HEAD_DIM = D_MODEL // X_HEADS

N_EXPERTS = 16
EC_CAPACITY = 2

LANES = 128
BF16_ROWS = 16
VMEM_LIMIT = 56 * 1024 * 1024
NEG = -1e30
LOG2E = math.log2(math.e)


def _params(*sem):
    return pltpu.CompilerParams(dimension_semantics=sem, vmem_limit_bytes=VMEM_LIMIT)


def _tile(n, pref):
    t = min(n, pref)
    assert n % t == 0, (n, pref)
    return t


def _ln_rows(v, g, b):
    mu = jnp.mean(v, -1, keepdims=True)
    c = v - mu
    var = jnp.mean(c * c, -1, keepdims=True)
    return c * lax.rsqrt(var + LN_EPS) * g + b


def _dot_nt(a, b):
    return lax.dot_general(a, b, (((1,), (1,)), ((), ())), preferred_element_type=F32)


def _dot(a, b):
    return jnp.dot(a, b, preferred_element_type=F32)


_ONE_LANE = 64


def _with_ones_lane(y):
    lane = lax.broadcasted_iota(I32, y.shape, 1)
    return jnp.where(lane % LANES == _ONE_LANE, 1.0, y)


def _proj_kernel(x_ref, w_ref, o_ref, *, ones_from):
    y = _dot(x_ref[...].astype(BF), w_ref[...])
    if ones_from is None:
        o_ref[...] = y.astype(o_ref.dtype)
    else:
        o_ref[:, :ones_from] = y[:, :ones_from].astype(o_ref.dtype)
        o_ref[:, ones_from:] = _with_ones_lane(y[:, ones_from:]).astype(o_ref.dtype)


def _proj(x, w, tm=512, ones_from=None):
    n, k = x.shape
    m = w.shape[1]
    tm = _tile(n, tm)
    return pl.pallas_call(
        functools.partial(_proj_kernel, ones_from=ones_from),
        grid=(n // tm,),
        in_specs=[pl.BlockSpec((tm, k), lambda i: (i, 0)),
                  pl.BlockSpec((k, m), lambda i: (0, 0))],
        out_specs=pl.BlockSpec((tm, m), lambda i: (i, 0)),
        out_shape=jax.ShapeDtypeStruct((n, m), BF),
        compiler_params=_params("parallel"),
        name="proj",
    )(x, w)


def _post_kernel(x_ref, o_ref, w_ref, g_ref, b_ref, y_ref):
    h = _dot(o_ref[...], w_ref[...])
    y_ref[...] = _ln_rows(ALPHA * x_ref[...] + h, g_ref[...], b_ref[...])


def _post(x, o, w, g, b, tm=512):
    n = x.shape[0]
    ko = o.shape[1]
    tm = _tile(n, tm)
    return pl.pallas_call(
        _post_kernel,
        grid=(n // tm,),
        in_specs=[pl.BlockSpec((tm, D_MODEL), lambda i: (i, 0)),
                  pl.BlockSpec((tm, ko), lambda i: (i, 0)),
                  pl.BlockSpec((ko, D_MODEL), lambda i: (0, 0)),
                  pl.BlockSpec((1, D_MODEL), lambda i: (0, 0)),
                  pl.BlockSpec((1, D_MODEL), lambda i: (0, 0))],
        out_specs=pl.BlockSpec((tm, D_MODEL), lambda i: (i, 0)),
        out_shape=jax.ShapeDtypeStruct((n, D_MODEL), F32),
        compiler_params=_params("parallel"),
        name="post",
    )(x, o, w, g, b)


def _post_b_kernel(x_ref, o0_ref, o1_ref, o2_ref, l0_ref, l1_ref, l2_ref, w_ref, g_ref, b_ref, y_ref):
    l0, l1, l2 = l0_ref[...], l1_ref[...], l2_ref[...]
    m = jnp.maximum(jnp.maximum(l0, l1), l2)
    e0, e1, e2 = jnp.exp(l0 - m), jnp.exp(l1 - m), jnp.exp(l2 - m)
    inv = 1.0 / (e0 + e1 + e2)
    gw = B_HEADS_PER_GROUP * LANES
    h = _dot((o0_ref[...].astype(F32) * (e0 * inv)).astype(BF), w_ref[0:gw, :])
    h = h + _dot((o1_ref[...].astype(F32) * (e1 * inv)).astype(BF), w_ref[gw:2 * gw, :])
    h = h + _dot((o2_ref[...].astype(F32) * (e2 * inv)).astype(BF), w_ref[2 * gw:3 * gw, :])
    y_ref[...] = _ln_rows(ALPHA * x_ref[...] + h, g_ref[...], b_ref[...])


def _post_b(x, os_, ls_, w, g, b, tm=512):
    n = x.shape[0]
    tm = _tile(n, tm)
    gw = B_HEADS_PER_GROUP * LANES
    row = lambda i: (i, 0)
    fixed = lambda i: (0, 0)
    return pl.pallas_call(
        _post_b_kernel,
        grid=(n // tm,),
        in_specs=[pl.BlockSpec((tm, D_MODEL), row)] + [pl.BlockSpec((tm, gw), row)] * 6
        + [pl.BlockSpec((3 * gw, D_MODEL), fixed), pl.BlockSpec((1, D_MODEL), fixed),
           pl.BlockSpec((1, D_MODEL), fixed)],
        out_specs=pl.BlockSpec((tm, D_MODEL), row),
        out_shape=jax.ShapeDtypeStruct((n, D_MODEL), F32),
        compiler_params=_params("parallel"),
        name="post_b",
    )(x, *os_, *ls_, w, g, b)


def _post_c_kernel(x_ref, ys_ref, d_ref, w_ref, g_ref, b_ref, y_ref):
    x = x_ref[...]
    z = jax.nn.gelu(ys_ref[...] + d_ref[...] * x).astype(BF)
    h = _dot(z, w_ref[...])
    hh = h[:, :D_MODEL] * jax.nn.sigmoid(h[:, D_MODEL:])
    y_ref[...] = _ln_rows(ALPHA * x + hh, g_ref[...], b_ref[...])


def _post_c(x, ys, d, w, g, b, tm=512):
    n = x.shape[0]
    tm = _tile(n, tm)
    row = lambda i: (i, 0)
    fixed = lambda i: (0, 0)
    return pl.pallas_call(
        _post_c_kernel,
        grid=(n // tm,),
        in_specs=[pl.BlockSpec((tm, D_MODEL), row), pl.BlockSpec((tm, D_MODEL), row),
                  pl.BlockSpec((1, D_MODEL), fixed), pl.BlockSpec((D_MODEL, 2 * D_MODEL), fixed),
                  pl.BlockSpec((1, D_MODEL), fixed), pl.BlockSpec((1, D_MODEL), fixed)],
        out_specs=pl.BlockSpec((tm, D_MODEL), row),
        out_shape=jax.ShapeDtypeStruct((n, D_MODEL), F32),
        compiler_params=_params("parallel"),
        name="post_c",
    )(x, ys, d, w, g, b)


_A_QK_TILES = A_HEADS + A_KV_HEADS
_A_COLS = (_A_QK_TILES + A_KV_HEADS) * LANES


def _rope_partner():
    d = np.arange(A_HEAD_DIM)
    e = d % (A_HEAD_DIM // 2)
    lo = e < A_HEAD_DIM // 4
    return np.where(lo, d + A_HEAD_DIM // 4, d - A_HEAD_DIM // 4), np.where(lo, -1.0, 1.0).astype(np.float32)


def _prep_a(w_qkv, q_gain, k_gain, w_o):
    partner, sign = _rope_partner()
    nqk = _A_QK_TILES * A_HEAD_DIM
    wqk = w_qkv[:, :nqk].reshape(D_MODEL, _A_QK_TILES, A_HEAD_DIM)
    wsw = wqk[:, :, partner] * sign
    wqk = jnp.concatenate([wqk, wsw], -1).reshape(D_MODEL, _A_QK_TILES * LANES)
    wv = w_qkv[:, nqk:].reshape(D_MODEL, A_KV_HEADS, A_HEAD_DIM)
    wv = jnp.concatenate([wv, jnp.zeros_like(wv)], -1).reshape(D_MODEL, A_KV_HEADS * LANES)
    w = jnp.concatenate([wqk, wv], 1).astype(BF)
    gq = jnp.concatenate([q_gain, q_gain[partner]]) * (A_HEAD_DIM ** -0.5 * 0.5 * LOG2E)
    gk = jnp.concatenate([k_gain, k_gain[partner]])
    gains = jnp.stack([gq, gk], 0)
    wo = w_o.reshape(A_HEADS, A_HEAD_DIM, D_MODEL)
    wo = jnp.concatenate([wo, jnp.zeros_like(wo)], 1).reshape(A_HEADS * LANES, D_MODEL).astype(BF)
    return w, gains, wo


def _rope_table(s):
    pos = jnp.arange(s)
    rows, cols = (pos // GRID_W).astype(F32), (pos % GRID_W).astype(F32)
    half = A_HEAD_DIM // 2
    freqs = ROPE_BASE ** (-jnp.arange(0, half, 2, dtype=F32) / half)
    ang_r = rows[:, None] * freqs
    ang_c = cols[:, None] * freqs
    ang = jnp.concatenate([ang_r, ang_r, ang_c, ang_c], -1)
    return jnp.concatenate([jnp.cos(ang), jnp.sin(ang)], -1)


def _proj_a_kernel(x_ref, w_ref, cs_ref, g_ref, o_ref):
    xb = x_ref[...].astype(BF)
    cs = cs_ref[...]
    gq = cs * g_ref[0:1, :]
    gk = cs * g_ref[1:2, :]
    per = 4
    for g in range(_A_QK_TILES // per):
        y = _dot(xb, w_ref[:, LANES * per * g:LANES * per * (g + 1)])
        for hh in range(per):
            h = per * g + hh
            t = y[:, LANES * hh:LANES * (hh + 1)]
            r = lax.rsqrt(jnp.sum(t * t, -1, keepdims=True) * (1.0 / LANES) + RMS_EPS)
            e = t * r * (gq if h < A_HEADS else gk)
            o_ref[:, LANES * h:LANES * (h + 1)] = (e + pltpu.roll(e, LANES // 2, 1)).astype(BF)
    yv = _dot(xb, w_ref[:, _A_QK_TILES * LANES:])
    o_ref[:, _A_QK_TILES * LANES:] = _with_ones_lane(yv).astype(BF)


def _proj_a(x, w, cs, gains, s, tm=512):
    n = x.shape[0]
    tm = _tile(s, tm)
    per = s // tm
    return pl.pallas_call(
        _proj_a_kernel,
        grid=(n // tm,),
        in_specs=[pl.BlockSpec((tm, D_MODEL), lambda i: (i, 0)),
                  pl.BlockSpec((D_MODEL, _A_COLS), lambda i: (0, 0)),
                  pl.BlockSpec((tm, LANES), lambda i: (i % per, 0)),
                  pl.BlockSpec((2, LANES), lambda i: (0, 0))],
        out_specs=pl.BlockSpec((tm, _A_COLS), lambda i: (i, 0)),
        out_shape=jax.ShapeDtypeStruct((n, _A_COLS), BF),
        compiler_params=_params("parallel"),
        name="proj_a",
    )(x, w, cs, gains)


def _softmax_step(s, v, m_ref, acc_ref):
    tk = s.shape[1]
    m_old = m_ref[...]
    m_new = jnp.maximum(m_old, jnp.max(s, -1, keepdims=True))
    p = jnp.concatenate([jnp.exp2(s[:, LANES * c:LANES * (c + 1)] - m_new).astype(BF) for c in range(tk // LANES)],
                        axis=1)
    acc_ref[...] = jnp.exp2(m_old - m_new) * acc_ref[...] + _dot(p, v)
    m_ref[...] = m_new


def _softmax_init(m_ref, acc_ref):
    m_ref[...] = jnp.full(m_ref.shape, -jnp.inf, F32)
    acc_ref[...] = jnp.zeros_like(acc_ref)


def _flash_a_kernel(q_ref, k_ref, v_ref, o_ref, m_ref, acc_ref, *, tk, nk):
    tq = q_ref.shape[0]
    rep = A_HEADS // A_KV_HEADS
    q = jnp.concatenate([q_ref[:, LANES * r:LANES * (r + 1)] for r in range(rep)], axis=0)
    _softmax_init(m_ref, acc_ref)

    def body(j, carry):
        off = pl.multiple_of(j * tk, tk)
        _softmax_step(_dot_nt(q, k_ref[pl.ds(off, tk), :]), v_ref[pl.ds(off, tk), :], m_ref, acc_ref)
        return carry

    lax.fori_loop(0, nk, body, 0, unroll=4)
    acc = acc_ref[...]
    o = acc / acc[:, _ONE_LANE:_ONE_LANE + 1]
    for r in range(rep):
        o_ref[:, LANES * r:LANES * (r + 1)] = o[r * tq:(r + 1) * tq].astype(BF)


def _flash_a(qkv, tq=256, tk=1024):
    b, s, _ = qkv.shape
    tq, tk = _tile(s, tq), _tile(s, tk)
    rep = A_HEADS // A_KV_HEADS
    gw = rep * LANES
    return pl.pallas_call(
        functools.partial(_flash_a_kernel, tk=tk, nk=s // tk),
        grid=(b, A_KV_HEADS, s // tq),
        in_specs=[pl.BlockSpec((None, tq, gw), lambda bi, g, i: (bi, i, g)),
                  pl.BlockSpec((None, s, LANES), lambda bi, g, i: (bi, 0, A_HEADS + g)),
                  pl.BlockSpec((None, s, LANES), lambda bi, g, i: (bi, 0, _A_QK_TILES + g))],
        out_specs=pl.BlockSpec((None, tq, gw), lambda bi, g, i: (bi, i, g)),
        out_shape=jax.ShapeDtypeStruct((b, s, A_HEADS * LANES), BF),
        scratch_shapes=[pltpu.VMEM((rep * tq, LANES), F32)] * 2,
        compiler_params=_params("parallel", "parallel", "arbitrary"),
        name="flash_a",
    )(qkv, qkv, qkv)


def _bucket(rel):
    half = REL_BUCKETS // 2
    max_exact = half // 2
    n = jnp.abs(rel)
    large = max_exact + (jnp.log(jnp.maximum(n, 1).astype(F32) / max_exact)
                         / math.log(REL_MAX_DIST / max_exact) * (half - max_exact)).astype(I32)
    large = jnp.minimum(large, half - 1)
    return jnp.where(rel > 0, half, 0) + jnp.where(n < max_exact, n, large)


_B_TILES = 3 * len(B_PATTERNS) * B_HEADS_PER_GROUP
_B_GW = B_HEADS_PER_GROUP * LANES
_B_T = 256


def _pad_heads(w, heads, dim):
    w = w.reshape(w.shape[0], heads, dim)
    return jnp.concatenate([w, jnp.zeros((w.shape[0], heads, LANES - dim), w.dtype)], -1).reshape(
        w.shape[0], heads * LANES)


def _prep_b(w_qkv, w_o):
    nh = len(B_PATTERNS) * B_HEADS_PER_GROUP
    c = nh * B_HEAD_DIM
    wq = _pad_heads(w_qkv[:, :c] * (B_HEAD_DIM ** -0.5), nh, B_HEAD_DIM)
    wk = _pad_heads(w_qkv[:, c:2 * c], nh, B_HEAD_DIM)
    wv = _pad_heads(w_qkv[:, 2 * c:], nh, B_HEAD_DIM)
    w = jnp.concatenate([wq, wk, wv], 1).astype(BF)
    wo = w_o.reshape(nh, B_HEAD_DIM, D_MODEL)
    wo = jnp.concatenate([wo, jnp.zeros_like(wo)], 1).reshape(nh * LANES, D_MODEL).astype(BF)
    return w, wo


def _toeplitz(vec, t):
    flat = jnp.tile(vec, (1,) * (vec.ndim - 1) + (t,))[..., :t * (2 * t - 1)]
    return flat.reshape(vec.shape[:-1] + (t, 2 * t - 1))[..., :t]


def _band_tiles(g, t):
    window, dil = B_PATTERNS[g]
    return pl.cdiv((window // (2 * dil)) * dil, t)


def _band_bias(rel_bias, g, t):
    window, dil = B_PATTERNS[g]
    reach = (window // (2 * dil)) * dil
    n = _band_tiles(g, t)
    rel0 = (jnp.arange(2 * t) + t) % (2 * t) - t
    rel = jnp.arange(-n, n + 1)[:, None] * t + rel0[None, :]
    bias = rel_bias[_bucket(rel)][:, :, g * B_HEADS_PER_GROUP:(g + 1) * B_HEADS_PER_GROUP]
    bias = jnp.where(((rel % dil == 0) & (jnp.abs(rel) <= reach))[:, :, None], bias, NEG)
    return _toeplitz(jnp.transpose(bias, (2, 0, 1)).astype(F32), t)


def _band_kernel(q_ref, k_ref, v_ref, bias_ref, o_ref, lse_ref, *, n, nk):
    t = q_ref.shape[0]
    i = pl.program_id(2)
    q = q_ref[...]
    logits, offs = [], []
    for o in range(-n, n + 1):
        j = i + o
        off = pl.multiple_of(jnp.clip(j, 0, nk - 1) * t, t)
        s = _dot_nt(q, k_ref[pl.ds(off, t), :]) + bias_ref[o + n]
        logits.append(jnp.where((j >= 0) & (j < nk), s, NEG))
        offs.append(off)
    m = logits[0].max(-1, keepdims=True)
    for s in logits[1:]:
        m = jnp.maximum(m, s.max(-1, keepdims=True))
    l = jnp.zeros((t, 1), F32)
    acc = jnp.zeros((t, LANES), F32)
    for s, off in zip(logits, offs):
        p = jnp.exp(s - m)
        l = l + jnp.sum(p, -1, keepdims=True)
        acc = acc + _dot(p.astype(BF), v_ref[pl.ds(off, t), :])
    o_ref[...] = (acc / l).astype(BF)
    lse_ref[...] = jnp.broadcast_to(m + jnp.log(l), (t, LANES))


def _band_attention(qkv, bias, g, t):
    b, s, _ = qkv.shape
    nh = len(B_PATTERNS) * B_HEADS_PER_GROUP
    n = _band_tiles(g, t)
    spec = lambda base: pl.BlockSpec((None, s, LANES), lambda bi, h, i: (bi, 0, base + g * B_HEADS_PER_GROUP + h))
    out_spec = pl.BlockSpec((None, t, LANES), lambda bi, h, i: (bi, i, h))
    o, lse = pl.pallas_call(
        functools.partial(_band_kernel, n=n, nk=s // t),
        grid=(b, B_HEADS_PER_GROUP, s // t),
        in_specs=[pl.BlockSpec((None, t, LANES), lambda bi, h, i: (bi, i, g * B_HEADS_PER_GROUP + h)),
                  spec(nh), spec(2 * nh),
                  pl.BlockSpec((None, 2 * n + 1, t, t), lambda bi, h, i: (h, 0, 0, 0))],
        out_specs=[out_spec, out_spec],
        out_shape=[jax.ShapeDtypeStruct((b, s, _B_GW), BF), jax.ShapeDtypeStruct((b, s, _B_GW), F32)],
        compiler_params=_params("parallel", "parallel", "arbitrary"),
        name="band_%d" % g,
    )(qkv, qkv, qkv, bias)
    return o.reshape(b * s, _B_GW), lse.reshape(b * s, _B_GW)


def _prep_c(lam_re, lam_im, log_dt, b_re, b_im, c_re, c_im):
    hp = lax.Precision.HIGHEST
    L, P, C = C_CHUNK, C_STATE, C_GROUP
    lr, li = lam_re.astype(F32), lam_im.astype(F32)
    dt = jnp.exp(log_dt.astype(F32))[..., None]
    mag = jnp.exp(lr * dt)
    ar, ai = mag * jnp.cos(li * dt), mag * jnp.sin(li * dt)
    den = lr * lr + li * li
    zr = ((ar - 1.0) * lr + ai * li) / den
    zi = (ai * lr - (ar - 1.0) * li) / den
    br, bi = b_re.astype(F32), b_im.astype(F32)
    bbr = zr[..., None] * br - zi[..., None] * bi
    bbi = zr[..., None] * bi + zi[..., None] * br
    cr, ci = c_re.astype(F32), c_im.astype(F32)
    prs, pis = [jnp.ones_like(ar)], [jnp.zeros_like(ai)]
    for _ in range(L):
        pr_, pi_ = prs[-1], pis[-1]
        prs.append(ar * pr_ - ai * pi_)
        pis.append(ar * pi_ + ai * pr_)
    pr, pi = jnp.stack(prs, 0), jnp.stack(pis, 0)

    def lag(pr_k, pi_k):
        tr = pr_k[..., None] * bbr - pi_k[..., None] * bbi
        ti = pr_k[..., None] * bbi + pi_k[..., None] * bbr
        return (jnp.einsum('dgop,kdgpi->kdgoi', cr, tr, precision=hp)
                - jnp.einsum('dgop,kdgpi->kdgoi', ci, ti, precision=hp))

    kern = lag(pr[:L], pi[:L])
    jj = np.arange(L)[:, None]
    ii = np.arange(L)[None, :]
    kf = kern[:, 0][np.clip(ii - jj, 0, L - 1)] * jnp.asarray((ii >= jj)[:, :, None, None, None], F32)
    kb = kern[:, 1][np.clip(jj - ii, 0, L - 1)] * jnp.asarray((jj >= ii)[:, :, None, None, None], F32)
    m_intra = jnp.transpose(kf + kb, (2, 0, 4, 1, 3)).reshape(C_N_GROUPS, L * C, L * C)

    def summ(d, powers):
        pr_k, pi_k = pr[powers, d], pi[powers, d]
        sr = pr_k[..., None] * bbr[d] - pi_k[..., None] * bbi[d]
        si = pr_k[..., None] * bbi[d] + pi_k[..., None] * bbr[d]
        s = jnp.concatenate([sr, si], 2)
        return jnp.transpose(s, (1, 0, 3, 2)).reshape(C_N_GROUPS, L * C, 2 * P)

    w_sum = jnp.concatenate([summ(0, np.arange(L - 1, -1, -1)), summ(1, np.arange(L))], -1)

    def outw(d, powers):
        pr_k, pi_k = pr[powers, d], pi[powers, d]
        wr = cr[d][None] * pr_k[:, :, None, :] - ci[d][None] * pi_k[:, :, None, :]
        wi = -(cr[d][None] * pi_k[:, :, None, :] + ci[d][None] * pr_k[:, :, None, :])
        w = jnp.concatenate([wr, wi], -1)
        return jnp.transpose(w, (1, 3, 0, 2)).reshape(C_N_GROUPS, 2 * P, L * C)

    w_state = jnp.concatenate([outw(0, np.arange(1, L + 1)), outw(1, np.arange(L, 0, -1))], 1)

    nq = C_N_GROUPS // _C_GL
    npair = _C_GL // 2
    blk = L * _C_GL * C
    lane = jnp.arange(LANES)
    gsel = (lane[None, :] // C == jnp.arange(_C_GL)[:, None])
    src = jnp.tile(m_intra.astype(BF).reshape(nq, _C_GL, L, C, L, C), (1, 1, 1, 1, 1, _C_GL))
    src = jnp.where(gsel[None, :, None, None, None, :], src, 0)
    w_intra = jnp.transpose(src, (0, 2, 1, 3, 4, 5)).reshape(nq, blk, blk)
    psel = (lane[None, :] // P == jnp.arange(2)[:, None])
    ksel = jnp.eye(npair, dtype=bool)
    src = jnp.tile(w_sum.astype(BF).reshape(nq, npair, 2, L, C, 1, 4, P), (1, 1, 1, 1, 1, npair, 1, 2))
    src = jnp.where(psel[None, None, :, None, None, None, None, :]
                    & ksel[None, :, None, None, None, :, None, None], src, 0)
    w_sum = jnp.transpose(src, (0, 3, 1, 2, 4, 5, 6, 7)).reshape(nq, blk, _C_GL * 4 * P)
    osel = (lane[None, None, :] // C
            == (2 * jnp.arange(npair)[:, None, None] + jnp.arange(2)[None, :, None]))
    src = jnp.tile(w_state.astype(BF).reshape(nq, npair, 2, 4, P, L, C), (1, 1, 1, 1, 1, 1, _C_GL))
    src = jnp.where(osel[None, :, :, None, None, None, :], src, 0)
    w_state = jnp.transpose(src, (0, 1, 3, 2, 4, 5, 6)).reshape(nq, _C_GL * 4 * P, blk)
    dec = jnp.stack([pr[L, 0], pi[L, 0], pr[L, 1], pi[L, 1]], 0).reshape(4, nq * npair, 2 * P)
    dec = jnp.broadcast_to(jnp.transpose(dec, (1, 0, 2))[:, :, None, :], (nq * npair, 4, 8, 2 * P))
    return w_sum.astype(BF), w_intra.astype(BF), w_state.astype(BF), dec


def _chunk_rows(x_ref, nb, rc):
    L = C_CHUNK
    rows = [jnp.concatenate([x_ref[b, pl.ds(j, rc, stride=L), :].astype(BF) for j in range(L)], axis=1)
            for b in range(nb)]
    return jnp.concatenate(rows, axis=0)


def _s5_sum_kernel(x_ref, w_ref, s_ref, *, nb, rc):
    s = _dot(_chunk_rows(x_ref, nb, rc), w_ref[...])
    for b in range(nb):
        for t in range(s_ref.shape[0]):
            s_ref[t, pl.ds(b, rc, stride=nb), :] = s[b * rc:(b + 1) * rc, LANES * t:LANES * (t + 1)]


def _s5_scan_kernel(s_ref, dec_ref, e_ref, *, nc, nb):
    units = s_ref.shape[0] // 4
    dec = [[dec_ref[u, k, 0:nb, :] for k in range(4)] for u in range(units)]

    def body(t, carry):
        rf = pl.ds(pl.multiple_of(t * nb, nb), nb)
        rb = pl.ds(pl.multiple_of((nc - 1 - t) * nb, nb), nb)
        new = []
        for u in range(units):
            fr, fi, br, bi = carry[4 * u:4 * u + 4]
            arf, aif, arb, aib = dec[u]
            e_ref[4 * u, rf, :] = fr
            e_ref[4 * u + 1, rf, :] = fi
            e_ref[4 * u + 2, rb, :] = br
            e_ref[4 * u + 3, rb, :] = bi
            new += [fr * arf - fi * aif + s_ref[4 * u, rf, :], fi * arf + fr * aif + s_ref[4 * u + 1, rf, :],
                    br * arb - bi * aib + s_ref[4 * u + 2, rb, :], bi * arb + br * aib + s_ref[4 * u + 3, rb, :]]
        return tuple(new)

    z = jnp.zeros((nb, LANES), F32)
    lax.fori_loop(0, nc, body, (z,) * (4 * units))


def _s5_out_kernel(x_ref, e_ref, wi_ref, wc_ref, y_ref, *, nb, rc):
    L = C_CHUNK
    half = pl.program_id(2)
    xc = _chunk_rows(x_ref, nb, rc)
    ec = jnp.concatenate(
        [jnp.concatenate([e_ref[t, pl.ds(b, rc, stride=nb), :].astype(BF) for t in range(e_ref.shape[0])], axis=1)
         for b in range(nb)], axis=0)
    y = _dot(xc, wi_ref[...]) + _dot(ec, wc_ref[...])
    for b in range(nb):
        for ii in range(L // 2):
            y_ref[b, pl.ds(half * (L // 2) + ii, rc, stride=L), :] = y[b * rc:(b + 1) * rc, LANES * ii:LANES * (ii + 1)]


def _s5(x, b, s, w_sum, w_intra, w_state, dec):
    L = C_CHUNK
    nc = s // L
    nq = D_MODEL // LANES
    nsl = w_sum.shape[2] // LANES
    blk = L * LANES
    rc = _tile(nc, max(8, 512 // b))
    x3 = x.reshape(b, s, D_MODEL)
    x_spec = pl.BlockSpec((b, rc * L, LANES), lambda q, c, *_: (0, c, q))
    sums = pl.pallas_call(
        functools.partial(_s5_sum_kernel, nb=b, rc=rc),
        grid=(nq, nc // rc),
        in_specs=[x_spec, pl.BlockSpec((None, blk, nsl * LANES), lambda q, c: (q, 0, 0))],
        out_specs=pl.BlockSpec((nsl, rc * b, LANES), lambda q, c: (q, c, 0)),
        out_shape=jax.ShapeDtypeStruct((nq * nsl, nc * b, LANES), F32),
        compiler_params=_params("parallel", "arbitrary"),
        name="s5_sum",
    )(x3, w_sum)
    upb = 2
    ent = pl.pallas_call(
        functools.partial(_s5_scan_kernel, nc=nc, nb=b),
        grid=(nq * nsl // (4 * upb),),
        in_specs=[pl.BlockSpec((4 * upb, nc * b, LANES), lambda i: (i, 0, 0)),
                  pl.BlockSpec((upb, 4, 8, LANES), lambda i: (i, 0, 0, 0))],
        out_specs=pl.BlockSpec((4 * upb, nc * b, LANES), lambda i: (i, 0, 0)),
        out_shape=jax.ShapeDtypeStruct((nq * nsl, nc * b, LANES), F32),
        compiler_params=_params("parallel"),
        name="s5_scan",
    )(sums, dec)
    y = pl.pallas_call(
        functools.partial(_s5_out_kernel, nb=b, rc=rc),
        grid=(nq, nc // rc, 2),
        in_specs=[x_spec,
                  pl.BlockSpec((nsl, rc * b, LANES), lambda q, c, h: (q, c, 0)),
                  pl.BlockSpec((None, blk, blk // 2), lambda q, c, h: (q, 0, h)),
                  pl.BlockSpec((None, nsl * LANES, blk // 2), lambda q, c, h: (q, 0, h))],
        out_specs=pl.BlockSpec((b, rc * L, LANES), lambda q, c, h: (0, c, q)),
        out_shape=jax.ShapeDtypeStruct((b, s, D_MODEL), F32),
        compiler_params=_params("parallel", "arbitrary", "arbitrary"),
        name="s5_out",
    )(x3, ent, w_intra, w_state)
    return y.reshape(b * s, D_MODEL)


_D_T = 512


def _prep_d(w_qkv, norm_gain, w_o, lambda_init):
    qk_w = D_HEADS * 2 * D_QK_DIM
    wq = _pad_heads(w_qkv[:, :qk_w] * (D_QK_DIM ** -0.5 * LOG2E), D_HEADS, 2 * D_QK_DIM)
    wk = _pad_heads(w_qkv[:, qk_w:2 * qk_w], D_HEADS, 2 * D_QK_DIM)
    wv = _pad_heads(w_qkv[:, 2 * qk_w:], D_HEADS, D_V_DIM)
    w = jnp.concatenate([wq, wk, wv], 1).astype(BF)
    gain = jnp.concatenate([norm_gain * (1.0 - lambda_init), jnp.zeros((LANES - D_V_DIM,), F32)])[None, :]
    wo = w_o.reshape(D_HEADS, D_V_DIM, D_MODEL)
    wo = jnp.concatenate([wo, jnp.zeros_like(wo)], 1).reshape(D_HEADS * LANES, D_MODEL).astype(BF)
    return w, gain, wo


def _diff_bias(rel_bias, t):
    rel0 = (jnp.arange(2 * t) + t) % (2 * t) - t
    rel = jnp.arange(-2, 3)[:, None] * t + rel0[None, :]
    return jnp.transpose(rel_bias[_bucket(rel)] * LOG2E, (2, 0, 1)).astype(F32)


def _flash_d_kernel(q_ref, k_ref, v_ref, vec_ref, lam_ref, gain_ref, o_ref, m_ref, acc_ref, bias_ref, *, kt, nk,
                    lambda_init):
    t = q_ref.shape[0]
    i = pl.program_id(2)

    @pl.when(i == 0)
    def _():
        for d in range(bias_ref.shape[0]):
            full = jnp.broadcast_to(vec_ref[d:d + 1, :], (t, 2 * t))
            bias_ref[d] = pltpu.roll(full, 0, 1, stride=1, stride_axis=0)[:, :t]

    qv = q_ref[...].astype(F32)
    lane = lax.broadcasted_iota(I32, (t, LANES), 1)
    q0 = jnp.where(lane < D_QK_DIM, qv, 0.0).astype(BF)
    q1 = jnp.where((lane >= D_QK_DIM) & (lane < 2 * D_QK_DIM), qv, 0.0).astype(BF)
    q = jnp.concatenate([q0, q1], axis=0)
    _softmax_init(m_ref, acc_ref)

    def body(j, carry):
        off = pl.multiple_of(j * (kt * t), kt * t)
        bias = jnp.concatenate([bias_ref[jnp.clip(j * kt + c - i, -2, 2) + 2] for c in range(kt)], axis=1)
        s = _dot_nt(q, k_ref[pl.ds(off, kt * t), :])
        s = (s.reshape(2, t, kt * t) + bias[None]).reshape(2 * t, kt * t)
        _softmax_step(s, v_ref[pl.ds(off, kt * t), :], m_ref, acc_ref)
        return carry

    lax.fori_loop(0, nk, body, 0, unroll=4)
    lf = lam_ref[...]
    lam = (jnp.exp(jnp.sum(lf[0:1] * lf[1:2], keepdims=True))
           - jnp.exp(jnp.sum(lf[2:3] * lf[3:4], keepdims=True)) + lambda_init)
    acc = acc_ref[...]
    on = acc / acc[:, _ONE_LANE:_ONE_LANE + 1]
    o = jnp.where(lane < D_V_DIM, on[:t] - lam * on[t:], 0.0)
    ms = jnp.sum(o * o, -1, keepdims=True) * (1.0 / D_V_DIM)
    o_ref[...] = (o * lax.rsqrt(ms + RMS_EPS) * gain_ref[...]).astype(BF)


def _flash_d(qkv, bias, lam, gain, lambda_init):
    b, s, _ = qkv.shape
    t = bias.shape[-1] // 2
    kt = 2 if s % (2 * t) == 0 else 1
    return pl.pallas_call(
        functools.partial(_flash_d_kernel, kt=kt, nk=s // (kt * t), lambda_init=lambda_init),
        grid=(b, D_HEADS, s // t),
        in_specs=[pl.BlockSpec((None, t, LANES), lambda bi, h, i: (bi, i, h)),
                  pl.BlockSpec((None, s, LANES), lambda bi, h, i: (bi, 0, D_HEADS + h)),
                  pl.BlockSpec((None, s, LANES), lambda bi, h, i: (bi, 0, 2 * D_HEADS + h)),
                  pl.BlockSpec((None, 5, 2 * t), lambda bi, h, i: (h, 0, 0)),
                  pl.BlockSpec((4, D_QK_DIM), lambda bi, h, i: (0, 0)),
                  pl.BlockSpec((1, LANES), lambda bi, h, i: (0, 0))],
        out_specs=pl.BlockSpec((None, t, LANES), lambda bi, h, i: (bi, i, h)),
        out_shape=jax.ShapeDtypeStruct((b, s, D_HEADS * LANES), BF),
        scratch_shapes=[pltpu.VMEM((2 * t, LANES), F32)] * 2 + [pltpu.VMEM((5, t, t), F32)],
        compiler_params=_params("parallel", "parallel", "arbitrary"),
        name="flash_d",
    )(qkv, qkv, qkv, bias, lam, gain)


def _cross_kernel(x_ref, kv_ref, wq_ref, wo_ref, g_ref, b_ref, wr_ref, y_ref, ybf_ref, aff_ref, *, parts):
    tp = x_ref.shape[0] // parts
    wr = wr_ref[...]
    wh = wr.astype(BF)
    wl = (wr - wh.astype(F32)).astype(BF)
    for part in range(parts):
        rows = slice(tp * part, tp * (part + 1))
        x = x_ref[rows, :]
        q = (_dot(x.astype(BF), wq_ref[...]) * (X_HEAD_DIM ** -0.5)).astype(BF)
        outs = []
        for h in range(X_HEADS):
            sl = slice(X_HEAD_DIM * h, X_HEAD_DIM * (h + 1))
            s = _dot_nt(q[:, sl], kv_ref[:, sl])
            p = jnp.exp(s - jnp.max(s, -1, keepdims=True))
            l = jnp.sum(p, -1, keepdims=True)
            vh = kv_ref[:, D_MODEL + X_HEAD_DIM * h:D_MODEL + X_HEAD_DIM * (h + 1)]
            outs.append((_dot(p.astype(BF), vh) / l).astype(BF))
        o = jnp.concatenate(outs, axis=1)
        y = _ln_rows(ALPHA * x + _dot(o, wo_ref[...]), g_ref[...], b_ref[...])
        y_ref[rows, :] = y
        yh = y.astype(BF)
        ybf_ref[rows, :] = yh
        yl = (y - yh.astype(F32)).astype(BF)
        lg = _dot_nt(wh, yh) + _dot_nt(wh, yl) + _dot_nt(wl, yh)
        e = jnp.exp(lg - jnp.max(lg, 0, keepdims=True))
        aff = e / jnp.sum(e, 0, keepdims=True)
        for c in range(tp // LANES):
            aff_ref[part * (tp // LANES) + c] = aff[:, LANES * c:LANES * (c + 1)]


def _cross(x, kv, wq, wo, g, b, wr_t, s, mem_len, tm=1024, parts=2):
    n = x.shape[0]
    tm = _tile(s, tm)
    per = s // tm
    fixed = lambda i: (0, 0)
    return pl.pallas_call(
        functools.partial(_cross_kernel, parts=parts),
        grid=(n // tm,),
        in_specs=[pl.BlockSpec((tm, D_MODEL), lambda i: (i, 0)),
                  pl.BlockSpec((mem_len, 2 * D_MODEL), lambda i: (i // per, 0)),
                  pl.BlockSpec((D_MODEL, D_MODEL), fixed), pl.BlockSpec((D_MODEL, D_MODEL), fixed),
                  pl.BlockSpec((1, D_MODEL), fixed), pl.BlockSpec((1, D_MODEL), fixed),
                  pl.BlockSpec((N_EXPERTS, D_MODEL), fixed)],
        out_specs=[pl.BlockSpec((tm, D_MODEL), lambda i: (i, 0)),
                   pl.BlockSpec((tm, D_MODEL), lambda i: (i, 0)),
                   pl.BlockSpec((tm // LANES, N_EXPERTS, LANES), lambda i: (i, 0, 0))],
        out_shape=[jax.ShapeDtypeStruct((n, D_MODEL), F32),
                   jax.ShapeDtypeStruct((n, D_MODEL), BF),
                   jax.ShapeDtypeStruct((n // LANES, N_EXPERTS, LANES), F32)],
        compiler_params=_params("parallel"),
        name="cross",
    )(x, kv, wq, wo, g, b, wr_t)


def _select_kernel(a_ref, pos_ref, st_ref, *, k, nbits):
    nt = a_ref.shape[0]
    shape = (nt, N_EXPERTS, LANES)
    kf = float(k)

    def keys():
        return lax.bitcast_convert_type(a_ref[...], I32)

    def count(mask):
        c = jnp.sum(jnp.where(mask, 1.0, 0.0), axis=0, keepdims=True)
        return jnp.sum(c, axis=2, keepdims=True)

    def value_step(it, thr):
        cand = thr | jnp.left_shift(jnp.int32(1), 30 - it)
        return jnp.where(count(keys() >= cand) >= kf, cand, thr)

    thr = lax.fori_loop(0, 31, value_step, jnp.zeros((1, N_EXPERTS, 1), I32))
    need = kf - count(keys() > thr)
    idx = lax.broadcasted_iota(I32, shape, 0) * LANES + lax.broadcasted_iota(I32, shape, 2)

    def index_step(it, ithr):
        cand = ithr | jnp.left_shift(jnp.int32(1), nbits - 1 - it)
        return jnp.where(count((keys() == thr) & (idx < cand)) < need, cand, ithr)

    ithr = lax.fori_loop(0, nbits, index_step, jnp.zeros((1, N_EXPERTS, 1), I32))
    thr2, ithr2 = thr[0], ithr[0]
    upper = jnp.where(lax.broadcasted_iota(I32, (LANES, LANES), 0) <= lax.broadcasted_iota(I32, (LANES, LANES), 1),
                      1.0, 0.0).astype(BF)
    lane = lax.broadcasted_iota(I32, (N_EXPERTS, LANES), 1)

    def tile_step(j, carry):
        kj = lax.bitcast_convert_type(a_ref[j], I32)
        sel = (kj > thr2) | ((kj == thr2) & (j * LANES + lane <= ithr2))
        m = jnp.where(sel, 1.0, 0.0)
        inc = _dot(m.astype(BF), upper)
        pos_ref[j] = jnp.where(sel, inc - m + carry, -1.0).astype(I32)
        st_ref[j] = jnp.broadcast_to(carry, (N_EXPERTS, LANES)).astype(I32)
        return carry + inc[:, LANES - 1:LANES]

    lax.fori_loop(0, nt, tile_step, jnp.zeros((N_EXPERTS, 1), F32))


def _select(aff3, k):
    nt = aff3.shape[0]
    nbits = max(1, int(math.ceil(math.log2(nt * LANES))))
    shp = jax.ShapeDtypeStruct(aff3.shape, I32)
    return pl.pallas_call(
        functools.partial(_select_kernel, k=k, nbits=nbits),
        out_shape=[shp, shp],
        compiler_params=pltpu.CompilerParams(vmem_limit_bytes=VMEM_LIMIT),
        name="select",
    )(aff3)


_GATHER_ROWS = LANES + BF16_ROWS
_GATHER_ROWS_SMALL = 3 * BF16_ROWS

_MOE_GROUP = 2


def _moe_ffn_kernel(st_ref, x_ref, pos_ref, aff_ref, wg_ref, wu_ref, wd_ref, ye_ref, buf_ref, gate_ref, *, nb, ncf,
                    nch, sub, ck):
    ep = buf_ref.shape[0]
    grp = pl.program_id(0)
    t = pl.program_id(1)

    @pl.when(t == 0)
    def _():
        buf_ref[...] = jnp.zeros_like(buf_ref)
        gate_ref[...] = jnp.zeros_like(gate_ref)

    @pl.when(t < nb)
    def _():
        def place(s, k, e, base, nrows):
            rows = pl.ds(base, nrows)
            hit = lax.broadcasted_iota(I32, (nrows, LANES), 0) == pos_ref[s, pl.ds(e, 1), :] - base
            xs = x_ref[LANES * s:LANES * (s + 1), :]
            buf_ref[k, rows, :] = buf_ref[k, rows, :] + _dot(jnp.where(hit, 1.0, 0.0).astype(BF), xs).astype(BF)
            gate_ref[k, rows, :] = gate_ref[k, rows, :] + jnp.sum(
                jnp.where(hit, aff_ref[s, pl.ds(e, 1), :], 0.0), -1, keepdims=True)

        def base_of(s, k):
            st = st_ref[grp * ep + k, t * sub + s]
            return pl.multiple_of((st // BF16_ROWS) * BF16_ROWS, BF16_ROWS)

        pairs = [(s, k) for s in range(sub) for k in range(ep)]
        few = functools.reduce(
            jnp.logical_and,
            [st_ref[grp * ep + k, t * sub + s + 1] - base_of(s, k) <= _GATHER_ROWS_SMALL for s, k in pairs])

        @pl.when(few)
        def _():
            for s, k in pairs:
                place(s, k, grp * ep + k, base_of(s, k), _GATHER_ROWS_SMALL)

        @pl.when(jnp.logical_not(few))
        def _():
            for s, k in pairs:
                place(s, k, grp * ep + k, base_of(s, k), _GATHER_ROWS)

    @pl.when(t >= nb)
    def _():
        k = (t - nb) // nch
        c = (t - nb) - k * nch

        @pl.when(c < ncf)
        def _():
            rows = pl.ds(pl.multiple_of(c * ck, ck), ck)
            xe = buf_ref[k, rows, :]
            h = (jax.nn.silu(_dot(xe, wg_ref[...])) * _dot(xe, wu_ref[...])).astype(BF)
            ye_ref[...] = (_dot(h, wd_ref[...]) * gate_ref[k, rows, :]).astype(BF)

        @pl.when(c >= ncf)
        def _():
            ye_ref[...] = jnp.zeros_like(ye_ref)


def _moe_ffn(starts, xbf, pos3, aff3, wg, wu, wd, cap, tb=1024):
    n = xbf.shape[0]
    tb = _tile(n, tb)
    nb = n // tb
    sub = tb // LANES
    ck = _tile(cap, 1024)
    ncf = cap // ck
    nch = ncf + pl.cdiv(_WIN_ROWS, ck)
    dff = wg.shape[-1]
    ep = _MOE_GROUP
    blk = lambda g, t, st: (jnp.minimum(t, nb - 1), 0)
    blk3 = lambda g, t, st: (jnp.minimum(t, nb - 1), 0, 0)
    expert = lambda g, t: g * ep + jnp.clip((t - nb) // nch, 0, ep - 1)
    wmap = lambda g, t, st: (expert(g, t), 0, 0)
    return pl.pallas_call(
        functools.partial(_moe_ffn_kernel, nb=nb, ncf=ncf, nch=nch, sub=sub, ck=ck),
        grid_spec=pltpu.PrefetchScalarGridSpec(
            num_scalar_prefetch=1,
            grid=(N_EXPERTS // ep, nb + ep * nch),
            in_specs=[pl.BlockSpec((tb, D_MODEL), blk),
                      pl.BlockSpec((sub, N_EXPERTS, LANES), blk3),
                      pl.BlockSpec((sub, N_EXPERTS, LANES), blk3),
                      pl.BlockSpec((None, D_MODEL, dff), wmap),
                      pl.BlockSpec((None, D_MODEL, dff), wmap),
                      pl.BlockSpec((None, dff, D_MODEL), wmap)],
            out_specs=pl.BlockSpec((None, ck, D_MODEL),
                                   lambda g, t, st: (expert(g, t), jnp.maximum(t - nb, 0) % nch, 0)),
            scratch_shapes=[pltpu.VMEM((ep, cap + _GATHER_ROWS, D_MODEL), BF),
                            pltpu.VMEM((ep, cap + _GATHER_ROWS, 1), F32)]),
        out_shape=jax.ShapeDtypeStruct((N_EXPERTS, nch * ck, D_MODEL), BF),
        compiler_params=_params("arbitrary", "arbitrary"),
        name="moe_ffn",
    )(starts, xbf, pos3, aff3, wg, wu, wd)


_WIN_STEP = 128
_COMB_TILES = 2
_WIN_ROWS = _WIN_STEP + (_COMB_TILES - 1) * LANES + _GATHER_ROWS


def _combine_kernel(st_ref, x_ref, pos_ref, g_ref, b_ref, *rest):
    ye_refs, y_ref = rest[:N_EXPERTS], rest[N_EXPERTS]
    j = pl.program_id(0)
    col = lax.broadcasted_iota(I32, (LANES, _GATHER_ROWS), 1)
    for u in range(_COMB_TILES):
        tok = slice(LANES * u, LANES * (u + 1))
        acc = None
        for e in range(N_EXPERTS):
            st = st_ref[e, j * _COMB_TILES + u]
            base = (st // BF16_ROWS) * BF16_ROWS
            win = (st_ref[e, j * _COMB_TILES] // _WIN_STEP) * _WIN_STEP
            sub = pl.multiple_of(base - win, BF16_ROWS)
            onehot = jnp.where(col == pos_ref[tok, e:e + 1] - base, 1.0, 0.0).astype(BF)
            d = _dot(onehot, ye_refs[e][0, pl.ds(sub, _GATHER_ROWS), :])
            acc = d if acc is None else acc + d
        y_ref[tok, :] = _ln_rows(ALPHA * x_ref[tok, :] + acc, g_ref[...], b_ref[...])


def _combine(starts, x, pos, g, b, ye):
    n = x.shape[0]
    assert ye.shape[1] >= EC_CAPACITY * n // N_EXPERTS + _WIN_ROWS
    tt = _COMB_TILES * LANES
    row = lambda j, st: (j, 0)
    fixed = lambda j, st: (0, 0)

    def ye_spec(e):
        return pl.BlockSpec((pl.Element(1), pl.Element(_WIN_ROWS), pl.Element(D_MODEL)),
                            lambda j, st: (e, (st[e, j * _COMB_TILES] // _WIN_STEP) * _WIN_STEP, 0))

    return pl.pallas_call(
        _combine_kernel,
        grid_spec=pltpu.PrefetchScalarGridSpec(
            num_scalar_prefetch=1,
            grid=(n // tt,),
            in_specs=[pl.BlockSpec((tt, D_MODEL), row),
                      pl.BlockSpec((tt, N_EXPERTS), row),
                      pl.BlockSpec((1, D_MODEL), fixed), pl.BlockSpec((1, D_MODEL), fixed)]
            + [ye_spec(e) for e in range(N_EXPERTS)],
            out_specs=pl.BlockSpec((tt, D_MODEL), row)),
        out_shape=jax.ShapeDtypeStruct((n, D_MODEL), F32),
        compiler_params=_params("arbitrary"),
        name="combine",
    )(starts, x, pos, g, b, *([ye] * N_EXPERTS))


def _moe(x, xg, aff3, wg, wu, wd, g, b):
    n = x.shape[0]
    cap = EC_CAPACITY * n // N_EXPERTS
    pos3, st3 = _select(aff3, cap)
    starts = jnp.concatenate([jnp.transpose(st3[:, :, 0]), jnp.full((N_EXPERTS, 1), cap, I32)], axis=1)
    ye = _moe_ffn(starts, xg, pos3, aff3, wg, wu, wd, cap)
    pos = jnp.transpose(pos3, (0, 2, 1)).reshape(n, N_EXPERTS)
    return _combine(starts, x, pos, g, b, ye)


def _prep_weights(p):
    w = {}
    w['a'] = [_prep_a(p['a_w_qkv'][j], p['a_q_gain'][j], p['a_k_gain'][j], p['a_w_o'][j])
              for j in range(p['a_w_qkv'].shape[0])]
    w['b'] = [_prep_b(p['b_w_qkv'][j], p['b_w_o'][j]) for j in range(p['b_w_qkv'].shape[0])]
    w['c'] = [_prep_c(p['c_lam_re'][j], p['c_lam_im'][j], p['c_log_dt'][j], p['c_b_re'][j], p['c_b_im'][j],
                      p['c_c_re'][j], p['c_c_im'][j]) + (p['c_d'][j][None, :], p['c_w_glu'][j].astype(BF))
              for j in range(p['c_lam_re'].shape[0])]
    w['d'] = []
    for j in range(p['d_w_qkv'].shape[0]):
        layer = N_MIXERS * j + 3
        lambda_init = 0.8 - 0.6 * math.exp(-0.3 * layer)
        w['d'].append(_prep_d(p['d_w_qkv'][j], p['d_norm_gain'][j], p['d_w_o'][j], lambda_init)
                      + (p['d_lam'][j].astype(F32), lambda_init))
    w['diff_bias'] = _diff_bias(p['rel_bias'], _D_T)
    w['x_w_q'] = p['x_w_q'].astype(BF)
    w['x_w_kv'] = p['x_w_kv'].astype(BF)
    w['x_w_o'] = p['x_w_o'].astype(BF)
    w['router_t'] = jnp.transpose(p['moe_w_router'], (0, 2, 1)).astype(F32)
    w['moe_w_gate'] = p['moe_w_gate'].astype(BF)
    w['moe_w_up'] = p['moe_w_up'].astype(BF)
    w['moe_w_down'] = p['moe_w_down'].astype(BF)
    return w


def _trunk(x, mem, p, w):
    b, s, _ = x.shape
    n = b * s
    mem_len = mem.shape[1]
    x = x.reshape(n, D_MODEL)
    mem2 = mem.reshape(b * mem_len, D_MODEL)
    ln_g, ln_b = p['ln_g'], p['ln_b']
    for i in range(DEPTH):
        m, j = i % N_MIXERS, i // N_MIXERS
        g0, b0 = ln_g[i, 0][None, :], ln_b[i, 0][None, :]
        if m == 0:
            wa, gains, wo = w['a'][j]
            qkv = _proj_a(x, wa, _rope_table(s), gains, s)
            o = _flash_a(qkv.reshape(b, s, _A_COLS))
            x = _post(x, o.reshape(n, A_HEADS * LANES), wo, g0, b0)
        elif m == 1:
            wb, wo = w['b'][j]
            qkv = _proj(x, wb).reshape(b, s, _B_TILES * LANES)
            os_, ls_ = [], []
            for g in range(len(B_PATTERNS)):
                t = _tile(s, _B_T)
                o, lse = _band_attention(qkv, _band_bias(p['rel_bias'], g, t), g, t)
                os_.append(o)
                ls_.append(lse)
            x = _post_b(x, os_, ls_, wo, g0, b0)
        elif m == 2:
            w_sum, w_intra, w_state, dec, dskip, wglu = w['c'][j]
            ys = _s5(x, b, s, w_sum, w_intra, w_state, dec)
            x = _post_c(x, ys, dskip, wglu, g0, b0)
        else:
            wd, gain, wo, lam, lambda_init = w['d'][j]
            qkv = _proj(x, wd, ones_from=2 * D_HEADS * LANES).reshape(b, s, 3 * D_HEADS * LANES)
            o = _flash_d(qkv, w['diff_bias'], lam, gain, lambda_init)
            x = _post(x, o.reshape(n, D_HEADS * LANES), wo, g0, b0)
        kv = _proj(mem2, w['x_w_kv'][i], tm=mem_len)
        x, xbf, aff3 = _cross(x, kv, w['x_w_q'][i], w['x_w_o'][i], ln_g[i, 1][None, :], ln_b[i, 1][None, :],
                              w['router_t'][i], s, mem_len)
        x = _moe(x, xbf, aff3, w['moe_w_gate'][i], w['moe_w_up'][i], w['moe_w_down'][i],
                 ln_g[i, 2][None, :], ln_b[i, 2][None, :])
    return x.reshape(b, s, D_MODEL)


def kernel(x_prompt, x_sample, mem_prompt, mem_sample, rel_bias, ln_g, ln_b, a_w_qkv, a_q_gain, a_k_gain, a_w_o, b_w_qkv, b_w_o, c_lam_re, c_lam_im, c_log_dt, c_b_re, c_b_im, c_c_re, c_c_im, c_d, c_w_glu, d_w_qkv, d_lam, d_norm_gain, d_w_o, x_w_q, x_w_kv, x_w_o, moe_w_router, moe_w_gate, moe_w_up, moe_w_down):
    p = dict(rel_bias=rel_bias, ln_g=ln_g, ln_b=ln_b,
             a_w_qkv=a_w_qkv, a_q_gain=a_q_gain, a_k_gain=a_k_gain, a_w_o=a_w_o,
             b_w_qkv=b_w_qkv, b_w_o=b_w_o,
             c_lam_re=c_lam_re, c_lam_im=c_lam_im, c_log_dt=c_log_dt, c_b_re=c_b_re, c_b_im=c_b_im,
             c_c_re=c_c_re, c_c_im=c_c_im, c_d=c_d, c_w_glu=c_w_glu,
             d_w_qkv=d_w_qkv, d_lam=d_lam, d_norm_gain=d_norm_gain, d_w_o=d_w_o,
             x_w_q=x_w_q, x_w_kv=x_w_kv, x_w_o=x_w_o,
             moe_w_router=moe_w_router, moe_w_gate=moe_w_gate, moe_w_up=moe_w_up, moe_w_down=moe_w_down)
    w = _prep_weights(p)
    return (_trunk(x_prompt, mem_prompt, p, w), _trunk(x_sample, mem_sample, p, w))
```

```python
import functools
import math

import numpy as np
import jax
import jax.numpy as jnp
from jax import lax
from jax.experimental import pallas as pl
from jax.experimental.pallas import tpu as pltpu

F32 = jnp.float32
BF = jnp.bfloat16
I32 = jnp.int32

D_MODEL = 1024
DEPTH = 4
GRID_W = 64
N_MIXERS = 4
LN_EPS = 1e-5
RMS_EPS = 1e-6
ALPHA = (2.0 * DEPTH) ** 0.25

A_HEADS = 16
A_KV_HEADS = 4
A_HEAD_DIM = 64
ROPE_BASE = 10000.0

B_PATTERNS = ((128, 1), (512, 4), (2048, 16))
B_HEADS_PER_GROUP = 4
B_HEAD_DIM = 64

C_GROUP = 16
C_N_GROUPS = D_MODEL // C_GROUP
C_STATE = 64
C_CHUNK = 16
_C_GL = 128 // C_GROUP

D_HEADS = 12
D_QK_DIM = 32
D_V_DIM = 64

REL_BUCKETS = 32
REL_MAX_DIST = 128

X_HEADS = 4
X_HEAD_DIM = D_MODEL // X_HEADS

N_EXPERTS = 16
EC_CAPACITY = 2

LANES = 128
BF16_ROWS = 16
VMEM_LIMIT = 56 * 1024 * 1024
NEG = -1e30
LOG2E = math.log2(math.e)


def _params(*sem):
    return pltpu.CompilerParams(dimension_semantics=sem, vmem_limit_bytes=VMEM_LIMIT)


def _tile(n, pref):
    t = min(n, pref)
    assert n % t == 0, (n, pref)
    return t


def _ln_rows(v, g, b):
    mu = jnp.mean(v, -1, keepdims=True)
    c = v - mu
    var = jnp.mean(c * c, -1, keepdims=True)
    return c * lax.rsqrt(var + LN_EPS) * g + b


def _dot_nt(a, b):
    return lax.dot_general(a, b, (((1,), (1,)), ((), ())), preferred_element_type=F32)


def _dot(a, b):
    return jnp.dot(a, b, preferred_element_type=F32)


_ONE_LANE = 64


def _with_ones_lane(y):
    lane = lax.broadcasted_iota(I32, y.shape, 1)
    return jnp.where(lane % LANES == _ONE_LANE, 1.0, y)


def _proj_kernel(x_ref, w_ref, o_ref, *, ones_from):
    y = _dot(x_ref[...].astype(BF), w_ref[...])
    if ones_from is None:
        o_ref[...] = y.astype(o_ref.dtype)
    else:
        o_ref[:, :ones_from] = y[:, :ones_from].astype(o_ref.dtype)
        o_ref[:, ones_from:] = _with_ones_lane(y[:, ones_from:]).astype(o_ref.dtype)


def _proj(x, w, tm=512, ones_from=None):
    n, k = x.shape
    m = w.shape[1]
    tm = _tile(n, tm)
    return pl.pallas_call(
        functools.partial(_proj_kernel, ones_from=ones_from),
        grid=(n // tm,),
        in_specs=[pl.BlockSpec((tm, k), lambda i: (i, 0)),
                  pl.BlockSpec((k, m), lambda i: (0, 0))],
        out_specs=pl.BlockSpec((tm, m), lambda i: (i, 0)),
        out_shape=jax.ShapeDtypeStruct((n, m), BF),
        compiler_params=_params("parallel"),
        name="proj",
    )(x, w)


def _post_kernel(x_ref, o_ref, w_ref, g_ref, b_ref, y_ref):
    h = _dot(o_ref[...], w_ref[...])
    y_ref[...] = _ln_rows(ALPHA * x_ref[...] + h, g_ref[...], b_ref[...])


def _post(x, o, w, g, b, tm=512):
    n = x.shape[0]
    ko = o.shape[1]
    tm = _tile(n, tm)
    return pl.pallas_call(
        _post_kernel,
        grid=(n // tm,),
        in_specs=[pl.BlockSpec((tm, D_MODEL), lambda i: (i, 0)),
                  pl.BlockSpec((tm, ko), lambda i: (i, 0)),
                  pl.BlockSpec((ko, D_MODEL), lambda i: (0, 0)),
                  pl.BlockSpec((1, D_MODEL), lambda i: (0, 0)),
                  pl.BlockSpec((1, D_MODEL), lambda i: (0, 0))],
        out_specs=pl.BlockSpec((tm, D_MODEL), lambda i: (i, 0)),
        out_shape=jax.ShapeDtypeStruct((n, D_MODEL), F32),
        compiler_params=_params("parallel"),
        name="post",
    )(x, o, w, g, b)


def _post_b_kernel(x_ref, o0_ref, o1_ref, o2_ref, l0_ref, l1_ref, l2_ref, w_ref, g_ref, b_ref, y_ref):
    l0, l1, l2 = l0_ref[...], l1_ref[...], l2_ref[...]
    m = jnp.maximum(jnp.maximum(l0, l1), l2)
    e0, e1, e2 = jnp.exp(l0 - m), jnp.exp(l1 - m), jnp.exp(l2 - m)
    inv = 1.0 / (e0 + e1 + e2)
    gw = B_HEADS_PER_GROUP * LANES
    h = _dot((o0_ref[...].astype(F32) * (e0 * inv)).astype(BF), w_ref[0:gw, :])
    h = h + _dot((o1_ref[...].astype(F32) * (e1 * inv)).astype(BF), w_ref[gw:2 * gw, :])
    h = h + _dot((o2_ref[...].astype(F32) * (e2 * inv)).astype(BF), w_ref[2 * gw:3 * gw, :])
    y_ref[...] = _ln_rows(ALPHA * x_ref[...] + h, g_ref[...], b_ref[...])


def _post_b(x, os_, ls_, w, g, b, tm=512):
    n = x.shape[0]
    tm = _tile(n, tm)
    gw = B_HEADS_PER_GROUP * LANES
    row = lambda i: (i, 0)
    fixed = lambda i: (0, 0)
    return pl.pallas_call(
        _post_b_kernel,
        grid=(n // tm,),
        in_specs=[pl.BlockSpec((tm, D_MODEL), row)] + [pl.BlockSpec((tm, gw), row)] * 6
        + [pl.BlockSpec((3 * gw, D_MODEL), fixed), pl.BlockSpec((1, D_MODEL), fixed),
           pl.BlockSpec((1, D_MODEL), fixed)],
        out_specs=pl.BlockSpec((tm, D_MODEL), row),
        out_shape=jax.ShapeDtypeStruct((n, D_MODEL), F32),
        compiler_params=_params("parallel"),
        name="post_b",
    )(x, *os_, *ls_, w, g, b)


def _post_c_kernel(x_ref, ys_ref, d_ref, w_ref, g_ref, b_ref, y_ref):
    x = x_ref[...]
    z = jax.nn.gelu(ys_ref[...] + d_ref[...] * x).astype(BF)
    h = _dot(z, w_ref[...])
    hh = h[:, :D_MODEL] * jax.nn.sigmoid(h[:, D_MODEL:])
    y_ref[...] = _ln_rows(ALPHA * x + hh, g_ref[...], b_ref[...])


def _post_c(x, ys, d, w, g, b, tm=512):
    n = x.shape[0]
    tm = _tile(n, tm)
    row = lambda i: (i, 0)
    fixed = lambda i: (0, 0)
    return pl.pallas_call(
        _post_c_kernel,
        grid=(n // tm,),
        in_specs=[pl.BlockSpec((tm, D_MODEL), row), pl.BlockSpec((tm, D_MODEL), row),
                  pl.BlockSpec((1, D_MODEL), fixed), pl.BlockSpec((D_MODEL, 2 * D_MODEL), fixed),
                  pl.BlockSpec((1, D_MODEL), fixed), pl.BlockSpec((1, D_MODEL), fixed)],
        out_specs=pl.BlockSpec((tm, D_MODEL), row),
        out_shape=jax.ShapeDtypeStruct((n, D_MODEL), F32),
        compiler_params=_params("parallel"),
        name="post_c",
    )(x, ys, d, w, g, b)


_A_QK_TILES = A_HEADS + A_KV_HEADS
_A_COLS = (_A_QK_TILES + A_KV_HEADS) * LANES


def _rope_partner():
    d = np.arange(A_HEAD_DIM)
    e = d % (A_HEAD_DIM // 2)
    lo = e < A_HEAD_DIM // 4
    return np.where(lo, d + A_HEAD_DIM // 4, d - A_HEAD_DIM // 4), np.where(lo, -1.0, 1.0).astype(np.float32)


def _prep_a(w_qkv, q_gain, k_gain, w_o):
    partner, sign = _rope_partner()
    nqk = _A_QK_TILES * A_HEAD_DIM
    wqk = w_qkv[:, :nqk].reshape(D_MODEL, _A_QK_TILES, A_HEAD_DIM)
    wsw = wqk[:, :, partner] * sign
    wqk = jnp.concatenate([wqk, wsw], -1).reshape(D_MODEL, _A_QK_TILES * LANES)
    wv = w_qkv[:, nqk:].reshape(D_MODEL, A_KV_HEADS, A_HEAD_DIM)
    wv = jnp.concatenate([wv, jnp.zeros_like(wv)], -1).reshape(D_MODEL, A_KV_HEADS * LANES)
    w = jnp.concatenate([wqk, wv], 1).astype(BF)
    gq = jnp.concatenate([q_gain, q_gain[partner]]) * (A_HEAD_DIM ** -0.5 * 0.5 * LOG2E)
    gk = jnp.concatenate([k_gain, k_gain[partner]])
    gains = jnp.stack([gq, gk], 0)
    wo = w_o.reshape(A_HEADS, A_HEAD_DIM, D_MODEL)
    wo = jnp.concatenate([wo, jnp.zeros_like(wo)], 1).reshape(A_HEADS * LANES, D_MODEL).astype(BF)
    return w, gains, wo


def _rope_table(s):
    pos = jnp.arange(s)
    rows, cols = (pos // GRID_W).astype(F32), (pos % GRID_W).astype(F32)
    half = A_HEAD_DIM // 2
    freqs = ROPE_BASE ** (-jnp.arange(0, half, 2, dtype=F32) / half)
    ang_r = rows[:, None] * freqs
    ang_c = cols[:, None] * freqs
    ang = jnp.concatenate([ang_r, ang_r, ang_c, ang_c], -1)
    return jnp.concatenate([jnp.cos(ang), jnp.sin(ang)], -1)


def _proj_a_kernel(x_ref, w_ref, cs_ref, g_ref, o_ref):
    xb = x_ref[...].astype(BF)
    cs = cs_ref[...]
    gq = cs * g_ref[0:1, :]
    gk = cs * g_ref[1:2, :]
    per = 4
    for g in range(_A_QK_TILES // per):
        y = _dot(xb, w_ref[:, LANES * per * g:LANES * per * (g + 1)])
        for hh in range(per):
            h = per * g + hh
            t = y[:, LANES * hh:LANES * (hh + 1)]
            r = lax.rsqrt(jnp.sum(t * t, -1, keepdims=True) * (1.0 / LANES) + RMS_EPS)
            e = t * r * (gq if h < A_HEADS else gk)
            o_ref[:, LANES * h:LANES * (h + 1)] = (e + pltpu.roll(e, LANES // 2, 1)).astype(BF)
    yv = _dot(xb, w_ref[:, _A_QK_TILES * LANES:])
    o_ref[:, _A_QK_TILES * LANES:] = _with_ones_lane(yv).astype(BF)


def _proj_a(x, w, cs, gains, s, tm=512):
    n = x.shape[0]
    tm = _tile(s, tm)
    per = s // tm
    return pl.pallas_call(
        _proj_a_kernel,
        grid=(n // tm,),
        in_specs=[pl.BlockSpec((tm, D_MODEL), lambda i: (i, 0)),
                  pl.BlockSpec((D_MODEL, _A_COLS), lambda i: (0, 0)),
                  pl.BlockSpec((tm, LANES), lambda i: (i % per, 0)),
                  pl.BlockSpec((2, LANES), lambda i: (0, 0))],
        out_specs=pl.BlockSpec((tm, _A_COLS), lambda i: (i, 0)),
        out_shape=jax.ShapeDtypeStruct((n, _A_COLS), BF),
        compiler_params=_params("parallel"),
        name="proj_a",
    )(x, w, cs, gains)


def _softmax_step(s, v, m_ref, acc_ref):
    tk = s.shape[1]
    m_old = m_ref[...]
    m_new = jnp.maximum(m_old, jnp.max(s, -1, keepdims=True))
    p = jnp.concatenate([jnp.exp2(s[:, LANES * c:LANES * (c + 1)] - m_new).astype(BF) for c in range(tk // LANES)],
                        axis=1)
    acc_ref[...] = jnp.exp2(m_old - m_new) * acc_ref[...] + _dot(p, v)
    m_ref[...] = m_new


def _softmax_init(m_ref, acc_ref):
    m_ref[...] = jnp.full(m_ref.shape, -jnp.inf, F32)
    acc_ref[...] = jnp.zeros_like(acc_ref)


def _flash_a_kernel(q_ref, k_ref, v_ref, o_ref, m_ref, acc_ref, *, tk, nk):
    tq = q_ref.shape[0]
    rep = A_HEADS // A_KV_HEADS
    q = jnp.concatenate([q_ref[:, LANES * r:LANES * (r + 1)] for r in range(rep)], axis=0)
    _softmax_init(m_ref, acc_ref)

    def body(j, carry):
        off = pl.multiple_of(j * tk, tk)
        _softmax_step(_dot_nt(q, k_ref[pl.ds(off, tk), :]), v_ref[pl.ds(off, tk), :], m_ref, acc_ref)
        return carry

    lax.fori_loop(0, nk, body, 0, unroll=4)
    acc = acc_ref[...]
    o = acc / acc[:, _ONE_LANE:_ONE_LANE + 1]
    for r in range(rep):
        o_ref[:, LANES * r:LANES * (r + 1)] = o[r * tq:(r + 1) * tq].astype(BF)


def _flash_a(qkv, tq=256, tk=2048):
    b, s, _ = qkv.shape
    tq, tk = _tile(s, tq), _tile(s, tk)
    rep = A_HEADS // A_KV_HEADS
    gw = rep * LANES
    return pl.pallas_call(
        functools.partial(_flash_a_kernel, tk=tk, nk=s // tk),
        grid=(b, A_KV_HEADS, s // tq),
        in_specs=[pl.BlockSpec((None, tq, gw), lambda bi, g, i: (bi, i, g)),
                  pl.BlockSpec((None, s, LANES), lambda bi, g, i: (bi, 0, A_HEADS + g)),
                  pl.BlockSpec((None, s, LANES), lambda bi, g, i: (bi, 0, _A_QK_TILES + g))],
        out_specs=pl.BlockSpec((None, tq, gw), lambda bi, g, i: (bi, i, g)),
        out_shape=jax.ShapeDtypeStruct((b, s, A_HEADS * LANES), BF),
        scratch_shapes=[pltpu.VMEM((rep * tq, LANES), F32)] * 2,
        compiler_params=_params("parallel", "parallel", "arbitrary"),
        name="flash_a",
    )(qkv, qkv, qkv)


def _bucket(rel):
    half = REL_BUCKETS // 2
    max_exact = half // 2
    n = jnp.abs(rel)
    large = max_exact + (jnp.log(jnp.maximum(n, 1).astype(F32) / max_exact)
                         / math.log(REL_MAX_DIST / max_exact) * (half - max_exact)).astype(I32)
    large = jnp.minimum(large, half - 1)
    return jnp.where(rel > 0, half, 0) + jnp.where(n < max_exact, n, large)


_B_TILES = 3 * len(B_PATTERNS) * B_HEADS_PER_GROUP
_B_GW = B_HEADS_PER_GROUP * LANES
_B_T = 256


def _pad_heads(w, heads, dim):
    w = w.reshape(w.shape[0], heads, dim)
    return jnp.concatenate([w, jnp.zeros((w.shape[0], heads, LANES - dim), w.dtype)], -1).reshape(
        w.shape[0], heads * LANES)


def _prep_b(w_qkv, w_o):
    nh = len(B_PATTERNS) * B_HEADS_PER_GROUP
    c = nh * B_HEAD_DIM
    wq = _pad_heads(w_qkv[:, :c] * (B_HEAD_DIM ** -0.5), nh, B_HEAD_DIM)
    wk = _pad_heads(w_qkv[:, c:2 * c], nh, B_HEAD_DIM)
    wv = _pad_heads(w_qkv[:, 2 * c:], nh, B_HEAD_DIM)
    w = jnp.concatenate([wq, wk, wv], 1).astype(BF)
    wo = w_o.reshape(nh, B_HEAD_DIM, D_MODEL)
    wo = jnp.concatenate([wo, jnp.zeros_like(wo)], 1).reshape(nh * LANES, D_MODEL).astype(BF)
    return w, wo


def _toeplitz(vec, t):
    flat = jnp.tile(vec, (1,) * (vec.ndim - 1) + (t,))[..., :t * (2 * t - 1)]
    return flat.reshape(vec.shape[:-1] + (t, 2 * t - 1))[..., :t]


def _band_tiles(g, t):
    window, dil = B_PATTERNS[g]
    return pl.cdiv((window // (2 * dil)) * dil, t)


def _band_bias(rel_bias, g, t):
    window, dil = B_PATTERNS[g]
    reach = (window // (2 * dil)) * dil
    n = _band_tiles(g, t)
    rel0 = (jnp.arange(2 * t) + t) % (2 * t) - t
    rel = jnp.arange(-n, n + 1)[:, None] * t + rel0[None, :]
    bias = rel_bias[_bucket(rel)][:, :, g * B_HEADS_PER_GROUP:(g + 1) * B_HEADS_PER_GROUP]
    bias = jnp.where(((rel % dil == 0) & (jnp.abs(rel) <= reach))[:, :, None], bias, NEG)
    return _toeplitz(jnp.transpose(bias, (2, 0, 1)).astype(F32), t)


def _band_kernel(q_ref, k_ref, v_ref, bias_ref, o_ref, lse_ref, *, n, nk):
    t = q_ref.shape[0]
    i = pl.program_id(2)
    q = q_ref[...]
    logits, offs = [], []
    for o in range(-n, n + 1):
        j = i + o
        off = pl.multiple_of(jnp.clip(j, 0, nk - 1) * t, t)
        s = _dot_nt(q, k_ref[pl.ds(off, t), :]) + bias_ref[o + n]
        logits.append(jnp.where((j >= 0) & (j < nk), s, NEG))
        offs.append(off)
    m = logits[0].max(-1, keepdims=True)
    for s in logits[1:]:
        m = jnp.maximum(m, s.max(-1, keepdims=True))
    l = jnp.zeros((t, 1), F32)
    acc = jnp.zeros((t, LANES), F32)
    for s, off in zip(logits, offs):
        p = jnp.exp(s - m)
        l = l + jnp.sum(p, -1, keepdims=True)
        acc = acc + _dot(p.astype(BF), v_ref[pl.ds(off, t), :])
    o_ref[...] = (acc / l).astype(BF)
    lse_ref[...] = jnp.broadcast_to(m + jnp.log(l), (t, LANES))


def _band_attention(qkv, bias, g, t):
    b, s, _ = qkv.shape
    nh = len(B_PATTERNS) * B_HEADS_PER_GROUP
    n = _band_tiles(g, t)
    spec = lambda base: pl.BlockSpec((None, s, LANES), lambda bi, h, i: (bi, 0, base + g * B_HEADS_PER_GROUP + h))
    out_spec = pl.BlockSpec((None, t, LANES), lambda bi, h, i: (bi, i, h))
    o, lse = pl.pallas_call(
        functools.partial(_band_kernel, n=n, nk=s // t),
        grid=(b, B_HEADS_PER_GROUP, s // t),
        in_specs=[pl.BlockSpec((None, t, LANES), lambda bi, h, i: (bi, i, g * B_HEADS_PER_GROUP + h)),
                  spec(nh), spec(2 * nh),
                  pl.BlockSpec((None, 2 * n + 1, t, t), lambda bi, h, i: (h, 0, 0, 0))],
        out_specs=[out_spec, out_spec],
        out_shape=[jax.ShapeDtypeStruct((b, s, _B_GW), BF), jax.ShapeDtypeStruct((b, s, _B_GW), F32)],
        compiler_params=_params("parallel", "parallel", "arbitrary"),
        name="band_%d" % g,
    )(qkv, qkv, qkv, bias)
    return o.reshape(b * s, _B_GW), lse.reshape(b * s, _B_GW)


def _prep_c(lam_re, lam_im, log_dt, b_re, b_im, c_re, c_im):
    hp = lax.Precision.HIGHEST
    L, P, C = C_CHUNK, C_STATE, C_GROUP
    lr, li = lam_re.astype(F32), lam_im.astype(F32)
    dt = jnp.exp(log_dt.astype(F32))[..., None]
    mag = jnp.exp(lr * dt)
    ar, ai = mag * jnp.cos(li * dt), mag * jnp.sin(li * dt)
    den = lr * lr + li * li
    zr = ((ar - 1.0) * lr + ai * li) / den
    zi = (ai * lr - (ar - 1.0) * li) / den
    br, bi = b_re.astype(F32), b_im.astype(F32)
    bbr = zr[..., None] * br - zi[..., None] * bi
    bbi = zr[..., None] * bi + zi[..., None] * br
    cr, ci = c_re.astype(F32), c_im.astype(F32)
    prs, pis = [jnp.ones_like(ar)], [jnp.zeros_like(ai)]
    for _ in range(L):
        pr_, pi_ = prs[-1], pis[-1]
        prs.append(ar * pr_ - ai * pi_)
        pis.append(ar * pi_ + ai * pr_)
    pr, pi = jnp.stack(prs, 0), jnp.stack(pis, 0)

    def lag(pr_k, pi_k):
        tr = pr_k[..., None] * bbr - pi_k[..., None] * bbi
        ti = pr_k[..., None] * bbi + pi_k[..., None] * bbr
        return (jnp.einsum('dgop,kdgpi->kdgoi', cr, tr, precision=hp)
                - jnp.einsum('dgop,kdgpi->kdgoi', ci, ti, precision=hp))

    kern = lag(pr[:L], pi[:L])
    lags = jnp.concatenate([kern[:0:-1, 1], (kern[0, 0] + kern[0, 1])[None], kern[1:, 0]], 0)

    def summ(d, powers):
        pr_k, pi_k = pr[powers, d], pi[powers, d]
        sr = pr_k[..., None] * bbr[d] - pi_k[..., None] * bbi[d]
        si = pr_k[..., None] * bbi[d] + pi_k[..., None] * bbr[d]
        s = jnp.concatenate([sr, si], 2)
        return jnp.transpose(s, (1, 0, 3, 2)).reshape(C_N_GROUPS, L * C, 2 * P)

    w_sum = jnp.concatenate([summ(0, np.arange(L - 1, -1, -1)), summ(1, np.arange(L))], -1)

    def outw(d, powers):
        pr_k, pi_k = pr[powers, d], pi[powers, d]
        wr = cr[d][None] * pr_k[:, :, None, :] - ci[d][None] * pi_k[:, :, None, :]
        wi = -(cr[d][None] * pi_k[:, :, None, :] + ci[d][None] * pr_k[:, :, None, :])
        w = jnp.concatenate([wr, wi], -1)
        return jnp.transpose(w, (1, 3, 0, 2)).reshape(C_N_GROUPS, 2 * P, L * C)

    w_state = jnp.concatenate([outw(0, np.arange(1, L + 1)), outw(1, np.arange(L, 0, -1))], 1)

    nq = C_N_GROUPS // _C_GL
    npair = _C_GL // 2
    blk = L * _C_GL * C
    lane = jnp.arange(LANES)
    gsel = (lane[None, :] // C == jnp.arange(_C_GL)[:, None])
    src = jnp.swapaxes(lags, -1, -2).astype(BF).reshape(2 * L - 1, nq, _C_GL, C, C)
    src = jnp.where(gsel[None, None, :, None, :], jnp.tile(src, (1, 1, 1, 1, _C_GL)), 0)
    w_lag = jnp.transpose(src.reshape(2 * L - 1, nq, LANES, LANES), (1, 0, 2, 3))
    psel = (lane[None, :] // P == jnp.arange(2)[:, None])
    ksel = jnp.eye(npair, dtype=bool)
    src = jnp.tile(w_sum.astype(BF).reshape(nq, npair, 2, L, C, 1, 4, P), (1, 1, 1, 1, 1, npair, 1, 2))
    src = jnp.where(psel[None, None, :, None, None, None, None, :]
                    & ksel[None, :, None, None, None, :, None, None], src, 0)
    w_sum = jnp.transpose(src, (0, 3, 1, 2, 4, 5, 6, 7)).reshape(nq, blk, _C_GL * 4 * P)
    osel = (lane[None, None, :] // C
            == (2 * jnp.arange(npair)[:, None, None] + jnp.arange(2)[None, :, None]))
    src = jnp.tile(w_state.astype(BF).reshape(nq, npair, 2, 4, P, L, C), (1, 1, 1, 1, 1, 1, _C_GL))
    src = jnp.where(osel[None, :, :, None, None, None, :], src, 0)
    w_state = jnp.transpose(src, (0, 1, 3, 2, 4, 5, 6)).reshape(nq, _C_GL * 4 * P, blk)
    dec = jnp.stack([pr[L, 0], pi[L, 0], pr[L, 1], pi[L, 1]], 0).reshape(4, nq * npair, 2 * P)
    dec = jnp.broadcast_to(jnp.transpose(dec, (1, 0, 2))[:, :, None, :], (nq * npair, 4, 8, 2 * P))
    return w_sum.astype(BF), w_lag, w_state.astype(BF), dec


def _chunk_rows(x_ref, nb, rc):
    L = C_CHUNK
    rows = [jnp.concatenate([x_ref[b, pl.ds(j, rc, stride=L), :].astype(BF) for j in range(L)], axis=1)
            for b in range(nb)]
    return jnp.concatenate(rows, axis=0)


def _s5_sum_kernel(x_ref, w_ref, s_ref, *, nb, rc):
    s = _dot(_chunk_rows(x_ref, nb, rc), w_ref[...])
    for b in range(nb):
        for t in range(s_ref.shape[0]):
            s_ref[t, pl.ds(b, rc, stride=nb), :] = s[b * rc:(b + 1) * rc, LANES * t:LANES * (t + 1)]


def _s5_scan_kernel(s_ref, dec_ref, e_ref, *, nc, nb):
    units = s_ref.shape[0] // 4
    dec = [[dec_ref[u, k, 0:nb, :] for k in range(4)] for u in range(units)]

    def body(t, carry):
        rf = pl.ds(pl.multiple_of(t * nb, nb), nb)
        rb = pl.ds(pl.multiple_of((nc - 1 - t) * nb, nb), nb)
        new = []
        for u in range(units):
            fr, fi, br, bi = carry[4 * u:4 * u + 4]
            arf, aif, arb, aib = dec[u]
            e_ref[4 * u, rf, :] = fr
            e_ref[4 * u + 1, rf, :] = fi
            e_ref[4 * u + 2, rb, :] = br
            e_ref[4 * u + 3, rb, :] = bi
            new += [fr * arf - fi * aif + s_ref[4 * u, rf, :], fi * arf + fr * aif + s_ref[4 * u + 1, rf, :],
                    br * arb - bi * aib + s_ref[4 * u + 2, rb, :], bi * arb + br * aib + s_ref[4 * u + 3, rb, :]]
        return tuple(new)

    z = jnp.zeros((nb, LANES), F32)
    lax.fori_loop(0, nc, body, (z,) * (4 * units))


def _s5_out_kernel(x_ref, e_ref, lag_ref, wc_ref, y_ref, wi_ref, *, nb, rc):
    L = C_CHUNK
    half = pl.program_id(2)

    @pl.when((pl.program_id(1) == 0) & (half == 0))
    def _():
        for j in range(L):
            for i in range(L):
                wi_ref[i // (L // 2), LANES * j:LANES * (j + 1), LANES * (i % (L // 2)):LANES * (i % (L // 2) + 1)] = (
                    lag_ref[i - j + L - 1])

    xc = _chunk_rows(x_ref, nb, rc)
    ec = jnp.concatenate(
        [jnp.concatenate([e_ref[t, pl.ds(b, rc, stride=nb), :].astype(BF) for t in range(e_ref.shape[0])], axis=1)
         for b in range(nb)], axis=0)
    y = _dot(xc, wi_ref[half]) + _dot(ec, wc_ref[...])
    for b in range(nb):
        for ii in range(L // 2):
            y_ref[b, pl.ds(half * (L // 2) + ii, rc, stride=L), :] = y[b * rc:(b + 1) * rc, LANES * ii:LANES * (ii + 1)]


def _s5(x, b, s, w_sum, w_lag, w_state, dec):
    L = C_CHUNK
    nc = s // L
    nq = D_MODEL // LANES
    nsl = w_sum.shape[2] // LANES
    blk = L * LANES
    rc = _tile(nc, max(8, 512 // b))
    x3 = x.reshape(b, s, D_MODEL)
    x_spec = pl.BlockSpec((b, rc * L, LANES), lambda q, c, *_: (0, c, q))
    sums = pl.pallas_call(
        functools.partial(_s5_sum_kernel, nb=b, rc=rc),
        grid=(nq, nc // rc),
        in_specs=[x_spec, pl.BlockSpec((None, blk, nsl * LANES), lambda q, c: (q, 0, 0))],
        out_specs=pl.BlockSpec((nsl, rc * b, LANES), lambda q, c: (q, c, 0)),
        out_shape=jax.ShapeDtypeStruct((nq * nsl, nc * b, LANES), F32),
        compiler_params=_params("parallel", "arbitrary"),
        name="s5_sum",
    )(x3, w_sum)
    upb = 2
    ent = pl.pallas_call(
        functools.partial(_s5_scan_kernel, nc=nc, nb=b),
        grid=(nq * nsl // (4 * upb),),
        in_specs=[pl.BlockSpec((4 * upb, nc * b, LANES), lambda i: (i, 0, 0)),
                  pl.BlockSpec((upb, 4, 8, LANES), lambda i: (i, 0, 0, 0))],
        out_specs=pl.BlockSpec((4 * upb, nc * b, LANES), lambda i: (i, 0, 0)),
        out_shape=jax.ShapeDtypeStruct((nq * nsl, nc * b, LANES), F32),
        compiler_params=_params("parallel"),
        name="s5_scan",
    )(sums, dec)
    y = pl.pallas_call(
        functools.partial(_s5_out_kernel, nb=b, rc=rc),
        grid=(nq, nc // rc, 2),
        in_specs=[x_spec,
                  pl.BlockSpec((nsl, rc * b, LANES), lambda q, c, h: (q, c, 0)),
                  pl.BlockSpec((None, 2 * L - 1, LANES, LANES), lambda q, c, h: (q, 0, 0, 0)),
                  pl.BlockSpec((None, nsl * LANES, blk // 2), lambda q, c, h: (q, 0, h))],
        out_specs=pl.BlockSpec((b, rc * L, LANES), lambda q, c, h: (0, c, q)),
        out_shape=jax.ShapeDtypeStruct((b, s, D_MODEL), F32),
        scratch_shapes=[pltpu.VMEM((2, blk, blk // 2), BF)],
        compiler_params=_params("arbitrary", "arbitrary", "arbitrary"),
        name="s5_out",
    )(x3, ent, w_lag, w_state)
    return y.reshape(b * s, D_MODEL)


_D_T = 512


def _prep_d(w_qkv, norm_gain, w_o, lambda_init):
    qk_w = D_HEADS * 2 * D_QK_DIM
    wq = _pad_heads(w_qkv[:, :qk_w] * (D_QK_DIM ** -0.5 * LOG2E), D_HEADS, 2 * D_QK_DIM)
    wk = _pad_heads(w_qkv[:, qk_w:2 * qk_w], D_HEADS, 2 * D_QK_DIM)
    wv = _pad_heads(w_qkv[:, 2 * qk_w:], D_HEADS, D_V_DIM)
    w = jnp.concatenate([wq, wk, wv], 1).astype(BF)
    gain = jnp.concatenate([norm_gain * (1.0 - lambda_init), jnp.zeros((LANES - D_V_DIM,), F32)])[None, :]
    wo = w_o.reshape(D_HEADS, D_V_DIM, D_MODEL)
    wo = jnp.concatenate([wo, jnp.zeros_like(wo)], 1).reshape(D_HEADS * LANES, D_MODEL).astype(BF)
    return w, gain, wo


def _diff_bias(rel_bias, t):
    rel0 = (jnp.arange(2 * t) + t) % (2 * t) - t
    rel = jnp.arange(-2, 3)[:, None] * t + rel0[None, :]
    return jnp.transpose(rel_bias[_bucket(rel)] * LOG2E, (2, 0, 1)).astype(F32)


def _flash_d_kernel(q_ref, k_ref, v_ref, vec_ref, lam_ref, gain_ref, o_ref, m_ref, acc_ref, bias_ref, *, kt, nk,
                    lambda_init):
    t = q_ref.shape[0]
    i = pl.program_id(2)

    @pl.when(i == 0)
    def _():
        for d in range(bias_ref.shape[0]):
            full = jnp.broadcast_to(vec_ref[d:d + 1, :], (t, 2 * t))
            bias_ref[d] = pltpu.roll(full, 0, 1, stride=1, stride_axis=0)[:, :t]

    qv = q_ref[...].astype(F32)
    lane = lax.broadcasted_iota(I32, (t, LANES), 1)
    q0 = jnp.where(lane < D_QK_DIM, qv, 0.0).astype(BF)
    q1 = jnp.where((lane >= D_QK_DIM) & (lane < 2 * D_QK_DIM), qv, 0.0).astype(BF)
    q = jnp.concatenate([q0, q1], axis=0)
    _softmax_init(m_ref, acc_ref)

    def body(j, carry):
        off = pl.multiple_of(j * (kt * t), kt * t)
        bias = jnp.concatenate([bias_ref[jnp.clip(j * kt + c - i, -2, 2) + 2] for c in range(kt)], axis=1)
        s = _dot_nt(q, k_ref[pl.ds(off, kt * t), :])
        s = (s.reshape(2, t, kt * t) + bias[None]).reshape(2 * t, kt * t)
        _softmax_step(s, v_ref[pl.ds(off, kt * t), :], m_ref, acc_ref)
        return carry

    lax.fori_loop(0, nk, body, 0, unroll=4)
    lf = lam_ref[...]
    lam = (jnp.exp(jnp.sum(lf[0:1] * lf[1:2], keepdims=True))
           - jnp.exp(jnp.sum(lf[2:3] * lf[3:4], keepdims=True)) + lambda_init)
    acc = acc_ref[...]
    on = acc / acc[:, _ONE_LANE:_ONE_LANE + 1]
    o = jnp.where(lane < D_V_DIM, on[:t] - lam * on[t:], 0.0)
    ms = jnp.sum(o * o, -1, keepdims=True) * (1.0 / D_V_DIM)
    o_ref[...] = (o * lax.rsqrt(ms + RMS_EPS) * gain_ref[...]).astype(BF)


def _flash_d(qkv, bias, lam, gain, lambda_init):
    b, s, _ = qkv.shape
    t = bias.shape[-1] // 2
    kt = 2 if s % (2 * t) == 0 else 1
    return pl.pallas_call(
        functools.partial(_flash_d_kernel, kt=kt, nk=s // (kt * t), lambda_init=lambda_init),
        grid=(b, D_HEADS, s // t),
        in_specs=[pl.BlockSpec((None, t, LANES), lambda bi, h, i: (bi, i, h)),
                  pl.BlockSpec((None, s, LANES), lambda bi, h, i: (bi, 0, D_HEADS + h)),
                  pl.BlockSpec((None, s, LANES), lambda bi, h, i: (bi, 0, 2 * D_HEADS + h)),
                  pl.BlockSpec((None, 5, 2 * t), lambda bi, h, i: (h, 0, 0)),
                  pl.BlockSpec((4, D_QK_DIM), lambda bi, h, i: (0, 0)),
                  pl.BlockSpec((1, LANES), lambda bi, h, i: (0, 0))],
        out_specs=pl.BlockSpec((None, t, LANES), lambda bi, h, i: (bi, i, h)),
        out_shape=jax.ShapeDtypeStruct((b, s, D_HEADS * LANES), BF),
        scratch_shapes=[pltpu.VMEM((2 * t, LANES), F32)] * 2 + [pltpu.VMEM((5, t, t), F32)],
        compiler_params=_params("parallel", "parallel", "arbitrary"),
        name="flash_d",
    )(qkv, qkv, qkv, bias, lam, gain)


def _cross_kernel(x_ref, kv_ref, wq_ref, wo_ref, g_ref, b_ref, wr_ref, y_ref, ybf_ref, aff_ref, *, parts):
    tp = x_ref.shape[0] // parts
    wr = wr_ref[...]
    wh = wr.astype(BF)
    wl = (wr - wh.astype(F32)).astype(BF)
    for part in range(parts):
        rows = slice(tp * part, tp * (part + 1))
        x = x_ref[rows, :]
        q = (_dot(x.astype(BF), wq_ref[...]) * (X_HEAD_DIM ** -0.5)).astype(BF)
        outs = []
        for h in range(X_HEADS):
            sl = slice(X_HEAD_DIM * h, X_HEAD_DIM * (h + 1))
            s = _dot_nt(q[:, sl], kv_ref[:, sl])
            p = jnp.exp(s - jnp.max(s, -1, keepdims=True))
            l = jnp.sum(p, -1, keepdims=True)
            vh = kv_ref[:, D_MODEL + X_HEAD_DIM * h:D_MODEL + X_HEAD_DIM * (h + 1)]
            outs.append((_dot(p.astype(BF), vh) / l).astype(BF))
        o = jnp.concatenate(outs, axis=1)
        y = _ln_rows(ALPHA * x + _dot(o, wo_ref[...]), g_ref[...], b_ref[...])
        y_ref[rows, :] = y
        yh = y.astype(BF)
        ybf_ref[rows, :] = yh
        yl = (y - yh.astype(F32)).astype(BF)
        lg = _dot_nt(wh, yh) + _dot_nt(wh, yl) + _dot_nt(wl, yh)
        e = jnp.exp(lg - jnp.max(lg, 0, keepdims=True))
        aff = e / jnp.sum(e, 0, keepdims=True)
        for c in range(tp // LANES):
            aff_ref[part * (tp // LANES) + c] = aff[:, LANES * c:LANES * (c + 1)]


def _cross(x, kv, wq, wo, g, b, wr_t, s, mem_len, tm=1024, parts=2):
    n = x.shape[0]
    tm = _tile(s, tm)
    per = s // tm
    fixed = lambda i: (0, 0)
    return pl.pallas_call(
        functools.partial(_cross_kernel, parts=parts),
        grid=(n // tm,),
        in_specs=[pl.BlockSpec((tm, D_MODEL), lambda i: (i, 0)),
                  pl.BlockSpec((mem_len, 2 * D_MODEL), lambda i: (i // per, 0)),
                  pl.BlockSpec((D_MODEL, D_MODEL), fixed), pl.BlockSpec((D_MODEL, D_MODEL), fixed),
                  pl.BlockSpec((1, D_MODEL), fixed), pl.BlockSpec((1, D_MODEL), fixed),
                  pl.BlockSpec((N_EXPERTS, D_MODEL), fixed)],
        out_specs=[pl.BlockSpec((tm, D_MODEL), lambda i: (i, 0)),
                   pl.BlockSpec((tm, D_MODEL), lambda i: (i, 0)),
                   pl.BlockSpec((tm // LANES, N_EXPERTS, LANES), lambda i: (i, 0, 0))],
        out_shape=[jax.ShapeDtypeStruct((n, D_MODEL), F32),
                   jax.ShapeDtypeStruct((n, D_MODEL), BF),
                   jax.ShapeDtypeStruct((n // LANES, N_EXPERTS, LANES), F32)],
        compiler_params=_params("parallel"),
        name="cross",
    )(x, kv, wq, wo, g, b, wr_t)


def _select_kernel(a_ref, pos_ref, st_ref, *, k, nbits):
    nt = a_ref.shape[0]
    shape = (nt, N_EXPERTS, LANES)
    kf = float(k)

    def keys():
        return lax.bitcast_convert_type(a_ref[...], I32)

    def count(mask):
        c = jnp.sum(jnp.where(mask, 1.0, 0.0), axis=0, keepdims=True)
        return jnp.sum(c, axis=2, keepdims=True)

    def value_step(it, thr):
        cand = thr | jnp.left_shift(jnp.int32(1), 30 - it)
        return jnp.where(count(keys() >= cand) >= kf, cand, thr)

    thr = lax.fori_loop(0, 31, value_step, jnp.zeros((1, N_EXPERTS, 1), I32))
    need = kf - count(keys() > thr)
    idx = lax.broadcasted_iota(I32, shape, 0) * LANES + lax.broadcasted_iota(I32, shape, 2)

    def index_step(it, ithr):
        cand = ithr | jnp.left_shift(jnp.int32(1), nbits - 1 - it)
        return jnp.where(count((keys() == thr) & (idx < cand)) < need, cand, ithr)

    ithr = lax.fori_loop(0, nbits, index_step, jnp.zeros((1, N_EXPERTS, 1), I32))
    thr2, ithr2 = thr[0], ithr[0]
    upper = jnp.where(lax.broadcasted_iota(I32, (LANES, LANES), 0) <= lax.broadcasted_iota(I32, (LANES, LANES), 1),
                      1.0, 0.0).astype(BF)
    lane = lax.broadcasted_iota(I32, (N_EXPERTS, LANES), 1)

    def tile_step(j, carry):
        kj = lax.bitcast_convert_type(a_ref[j], I32)
        sel = (kj > thr2) | ((kj == thr2) & (j * LANES + lane <= ithr2))
        m = jnp.where(sel, 1.0, 0.0)
        inc = _dot(m.astype(BF), upper)
        pos_ref[j] = jnp.where(sel, inc - m + carry, -1.0).astype(I32)
        st_ref[j] = jnp.broadcast_to(carry, (N_EXPERTS, LANES)).astype(I32)
        return carry + inc[:, LANES - 1:LANES]

    lax.fori_loop(0, nt, tile_step, jnp.zeros((N_EXPERTS, 1), F32))


def _select(aff3, k):
    nt = aff3.shape[0]
    nbits = max(1, int(math.ceil(math.log2(nt * LANES))))
    shp = jax.ShapeDtypeStruct(aff3.shape, I32)
    return pl.pallas_call(
        functools.partial(_select_kernel, k=k, nbits=nbits),
        out_shape=[shp, shp],
        compiler_params=pltpu.CompilerParams(vmem_limit_bytes=VMEM_LIMIT),
        name="select",
    )(aff3)


_GATHER_ROWS = LANES + BF16_ROWS
_GATHER_ROWS_SMALL = 3 * BF16_ROWS

_MOE_GROUP = 2


def _moe_ffn_kernel(st_ref, x_ref, pos_ref, aff_ref, wg_ref, wu_ref, wd_ref, ye_ref, buf_ref, gate_ref, *, nb, ncf,
                    nch, sub, ck):
    ep = buf_ref.shape[0]
    grp = pl.program_id(0)
    t = pl.program_id(1)

    @pl.when(t == 0)
    def _():
        buf_ref[...] = jnp.zeros_like(buf_ref)
        gate_ref[...] = jnp.zeros_like(gate_ref)

    @pl.when(t < nb)
    def _():
        def place(s, k, e, base, nrows):
            rows = pl.ds(base, nrows)
            hit = lax.broadcasted_iota(I32, (nrows, LANES), 0) == pos_ref[s, pl.ds(e, 1), :] - base
            xs = x_ref[LANES * s:LANES * (s + 1), :]
            buf_ref[k, rows, :] = buf_ref[k, rows, :] + _dot(jnp.where(hit, 1.0, 0.0).astype(BF), xs).astype(BF)
            gate_ref[k, rows, :] = gate_ref[k, rows, :] + jnp.sum(
                jnp.where(hit, aff_ref[s, pl.ds(e, 1), :], 0.0), -1, keepdims=True)

        def base_of(s, k):
            st = st_ref[grp * ep + k, t * sub + s]
            return pl.multiple_of((st // BF16_ROWS) * BF16_ROWS, BF16_ROWS)

        pairs = [(s, k) for s in range(sub) for k in range(ep)]
        few = functools.reduce(
            jnp.logical_and,
            [st_ref[grp * ep + k, t * sub + s + 1] - base_of(s, k) <= _GATHER_ROWS_SMALL for s, k in pairs])

        @pl.when(few)
        def _():
            for s, k in pairs:
                place(s, k, grp * ep + k, base_of(s, k), _GATHER_ROWS_SMALL)

        @pl.when(jnp.logical_not(few))
        def _():
            for s, k in pairs:
                place(s, k, grp * ep + k, base_of(s, k), _GATHER_ROWS)

    @pl.when(t >= nb)
    def _():
        k = (t - nb) // nch
        c = (t - nb) - k * nch

        @pl.when(c < ncf)
        def _():
            rows = pl.ds(pl.multiple_of(c * ck, ck), ck)
            xe = buf_ref[k, rows, :]
            h = (jax.nn.silu(_dot(xe, wg_ref[...])) * _dot(xe, wu_ref[...])).astype(BF)
            ye_ref[...] = (_dot(h, wd_ref[...]) * gate_ref[k, rows, :]).astype(BF)

        @pl.when(c >= ncf)
        def _():
            ye_ref[...] = jnp.zeros_like(ye_ref)


def _moe_ffn(starts, xbf, pos3, aff3, wg, wu, wd, cap, tb=1024):
    n = xbf.shape[0]
    tb = _tile(n, tb)
    nb = n // tb
    sub = tb // LANES
    ck = _tile(cap, 1024)
    ncf = cap // ck
    nch = ncf + pl.cdiv(_WIN_ROWS, ck)
    dff = wg.shape[-1]
    ep = _MOE_GROUP
    blk = lambda g, t, st: (jnp.minimum(t, nb - 1), 0)
    blk3 = lambda g, t, st: (jnp.minimum(t, nb - 1), 0, 0)
    expert = lambda g, t: g * ep + jnp.clip((t - nb) // nch, 0, ep - 1)
    wmap = lambda g, t, st: (expert(g, t), 0, 0)
    return pl.pallas_call(
        functools.partial(_moe_ffn_kernel, nb=nb, ncf=ncf, nch=nch, sub=sub, ck=ck),
        grid_spec=pltpu.PrefetchScalarGridSpec(
            num_scalar_prefetch=1,
            grid=(N_EXPERTS // ep, nb + ep * nch),
            in_specs=[pl.BlockSpec((tb, D_MODEL), blk),
                      pl.BlockSpec((sub, N_EXPERTS, LANES), blk3),
                      pl.BlockSpec((sub, N_EXPERTS, LANES), blk3),
                      pl.BlockSpec((None, D_MODEL, dff), wmap),
                      pl.BlockSpec((None, D_MODEL, dff), wmap),
                      pl.BlockSpec((None, dff, D_MODEL), wmap)],
            out_specs=pl.BlockSpec((None, ck, D_MODEL),
                                   lambda g, t, st: (expert(g, t), jnp.maximum(t - nb, 0) % nch, 0)),
            scratch_shapes=[pltpu.VMEM((ep, cap + _GATHER_ROWS, D_MODEL), BF),
                            pltpu.VMEM((ep, cap + _GATHER_ROWS, 1), F32)]),
        out_shape=jax.ShapeDtypeStruct((N_EXPERTS, nch * ck, D_MODEL), BF),
        compiler_params=_params("arbitrary", "arbitrary"),
        name="moe_ffn",
    )(starts, xbf, pos3, aff3, wg, wu, wd)


_WIN_STEP = 128
_COMB_TILES = 2
_WIN_ROWS = _WIN_STEP + (_COMB_TILES - 1) * LANES + _GATHER_ROWS


def _combine_kernel(st_ref, x_ref, pos_ref, g_ref, b_ref, *rest):
    ye_refs, y_ref = rest[:N_EXPERTS], rest[N_EXPERTS]
    j = pl.program_id(0)
    col = lax.broadcasted_iota(I32, (LANES, _GATHER_ROWS), 1)
    for u in range(_COMB_TILES):
        tok = slice(LANES * u, LANES * (u + 1))
        acc = None
        for e in range(N_EXPERTS):
            st = st_ref[e, j * _COMB_TILES + u]
            base = (st // BF16_ROWS) * BF16_ROWS
            win = (st_ref[e, j * _COMB_TILES] // _WIN_STEP) * _WIN_STEP
            sub = pl.multiple_of(base - win, BF16_ROWS)
            onehot = jnp.where(col == pos_ref[tok, e:e + 1] - base, 1.0, 0.0).astype(BF)
            d = _dot(onehot, ye_refs[e][0, pl.ds(sub, _GATHER_ROWS), :])
            acc = d if acc is None else acc + d
        y_ref[tok, :] = _ln_rows(ALPHA * x_ref[tok, :] + acc, g_ref[...], b_ref[...])


def _combine(starts, x, pos, g, b, ye):
    n = x.shape[0]
    assert ye.shape[1] >= EC_CAPACITY * n // N_EXPERTS + _WIN_ROWS
    tt = _COMB_TILES * LANES
    row = lambda j, st: (j, 0)
    fixed = lambda j, st: (0, 0)

    def ye_spec(e):
        return pl.BlockSpec((pl.Element(1), pl.Element(_WIN_ROWS), pl.Element(D_MODEL)),
                            lambda j, st: (e, (st[e, j * _COMB_TILES] // _WIN_STEP) * _WIN_STEP, 0))

    return pl.pallas_call(
        _combine_kernel,
        grid_spec=pltpu.PrefetchScalarGridSpec(
            num_scalar_prefetch=1,
            grid=(n // tt,),
            in_specs=[pl.BlockSpec((tt, D_MODEL), row),
                      pl.BlockSpec((tt, N_EXPERTS), row),
                      pl.BlockSpec((1, D_MODEL), fixed), pl.BlockSpec((1, D_MODEL), fixed)]
            + [ye_spec(e) for e in range(N_EXPERTS)],
            out_specs=pl.BlockSpec((tt, D_MODEL), row)),
        out_shape=jax.ShapeDtypeStruct((n, D_MODEL), F32),
        compiler_params=_params("arbitrary"),
        name="combine",
    )(starts, x, pos, g, b, *([ye] * N_EXPERTS))


def _moe(x, xg, aff3, wg, wu, wd, g, b):
    n = x.shape[0]
    cap = EC_CAPACITY * n // N_EXPERTS
    pos3, st3 = _select(aff3, cap)
    starts = jnp.concatenate([jnp.transpose(st3[:, :, 0]), jnp.full((N_EXPERTS, 1), cap, I32)], axis=1)
    ye = _moe_ffn(starts, xg, pos3, aff3, wg, wu, wd, cap)
    pos = jnp.transpose(pos3, (0, 2, 1)).reshape(n, N_EXPERTS)
    return _combine(starts, x, pos, g, b, ye)


def _prep_weights(p):
    w = {}
    w['a'] = [_prep_a(p['a_w_qkv'][j], p['a_q_gain'][j], p['a_k_gain'][j], p['a_w_o'][j])
              for j in range(p['a_w_qkv'].shape[0])]
    w['b'] = [_prep_b(p['b_w_qkv'][j], p['b_w_o'][j]) for j in range(p['b_w_qkv'].shape[0])]
    w['c'] = [_prep_c(p['c_lam_re'][j], p['c_lam_im'][j], p['c_log_dt'][j], p['c_b_re'][j], p['c_b_im'][j],
                      p['c_c_re'][j], p['c_c_im'][j]) + (p['c_d'][j][None, :], p['c_w_glu'][j].astype(BF))
              for j in range(p['c_lam_re'].shape[0])]
    w['d'] = []
    for j in range(p['d_w_qkv'].shape[0]):
        layer = N_MIXERS * j + 3
        lambda_init = 0.8 - 0.6 * math.exp(-0.3 * layer)
        w['d'].append(_prep_d(p['d_w_qkv'][j], p['d_norm_gain'][j], p['d_w_o'][j], lambda_init)
                      + (p['d_lam'][j].astype(F32), lambda_init))
    w['diff_bias'] = _diff_bias(p['rel_bias'], _D_T)
    w['x_w_q'] = p['x_w_q'].astype(BF)
    w['x_w_kv'] = p['x_w_kv'].astype(BF)
    w['x_w_o'] = p['x_w_o'].astype(BF)
    w['router_t'] = jnp.transpose(p['moe_w_router'], (0, 2, 1)).astype(F32)
    w['moe_w_gate'] = p['moe_w_gate'].astype(BF)
    w['moe_w_up'] = p['moe_w_up'].astype(BF)
    w['moe_w_down'] = p['moe_w_down'].astype(BF)
    return w


def _trunk(x, mem, p, w):
    b, s, _ = x.shape
    n = b * s
    mem_len = mem.shape[1]
    x = x.reshape(n, D_MODEL)
    mem2 = mem.reshape(b * mem_len, D_MODEL)
    ln_g, ln_b = p['ln_g'], p['ln_b']
    for i in range(DEPTH):
        m, j = i % N_MIXERS, i // N_MIXERS
        g0, b0 = ln_g[i, 0][None, :], ln_b[i, 0][None, :]
        if m == 0:
            wa, gains, wo = w['a'][j]
            qkv = _proj_a(x, wa, _rope_table(s), gains, s)
            o = _flash_a(qkv.reshape(b, s, _A_COLS))
            x = _post(x, o.reshape(n, A_HEADS * LANES), wo, g0, b0)
        elif m == 1:
            wb, wo = w['b'][j]
            qkv = _proj(x, wb).reshape(b, s, _B_TILES * LANES)
            os_, ls_ = [], []
            for g in range(len(B_PATTERNS)):
                t = _tile(s, _B_T)
                o, lse = _band_attention(qkv, _band_bias(p['rel_bias'], g, t), g, t)
                os_.append(o)
                ls_.append(lse)
            x = _post_b(x, os_, ls_, wo, g0, b0)
        elif m == 2:
            w_sum, w_lag, w_state, dec, dskip, wglu = w['c'][j]
            ys = _s5(x, b, s, w_sum, w_lag, w_state, dec)
            x = _post_c(x, ys, dskip, wglu, g0, b0)
        else:
            wd, gain, wo, lam, lambda_init = w['d'][j]
            qkv = _proj(x, wd, ones_from=2 * D_HEADS * LANES).reshape(b, s, 3 * D_HEADS * LANES)
            o = _flash_d(qkv, w['diff_bias'], lam, gain, lambda_init)
            x = _post(x, o.reshape(n, D_HEADS * LANES), wo, g0, b0)
        kv = _proj(mem2, w['x_w_kv'][i], tm=mem_len)
        x, xbf, aff3 = _cross(x, kv, w['x_w_q'][i], w['x_w_o'][i], ln_g[i, 1][None, :], ln_b[i, 1][None, :],
                              w['router_t'][i], s, mem_len)
        x = _moe(x, xbf, aff3, w['moe_w_gate'][i], w['moe_w_up'][i], w['moe_w_down'][i],
                 ln_g[i, 2][None, :], ln_b[i, 2][None, :])
    return x.reshape(b, s, D_MODEL)


def kernel(x_prompt, x_sample, mem_prompt, mem_sample, rel_bias, ln_g, ln_b, a_w_qkv, a_q_gain, a_k_gain, a_w_o, b_w_qkv, b_w_o, c_lam_re, c_lam_im, c_log_dt, c_b_re, c_b_im, c_c_re, c_c_im, c_d, c_w_glu, d_w_qkv, d_lam, d_norm_gain, d_w_o, x_w_q, x_w_kv, x_w_o, moe_w_router, moe_w_gate, moe_w_up, moe_w_down):
    p = dict(rel_bias=rel_bias, ln_g=ln_g, ln_b=ln_b,
             a_w_qkv=a_w_qkv, a_q_gain=a_q_gain, a_k_gain=a_k_gain, a_w_o=a_w_o,
             b_w_qkv=b_w_qkv, b_w_o=b_w_o,
             c_lam_re=c_lam_re, c_lam_im=c_lam_im, c_log_dt=c_log_dt, c_b_re=c_b_re, c_b_im=c_b_im,
             c_c_re=c_c_re, c_c_im=c_c_im, c_d=c_d, c_w_glu=c_w_glu,
             d_w_qkv=d_w_qkv, d_lam=d_lam, d_norm_gain=d_norm_gain, d_w_o=d_w_o,
             x_w_q=x_w_q, x_w_kv=x_w_kv, x_w_o=x_w_o,
             moe_w_router=moe_w_router, moe_w_gate=moe_w_gate, moe_w_up=moe_w_up, moe_w_down=moe_w_down)
    w = _prep_weights(p)
    return (_trunk(x_prompt, mem_prompt, p, w), _trunk(x_sample, mem_sample, p, w))
```

```python
import functools
import math

import numpy as np
import jax
import jax.numpy as jnp
from jax import lax
from jax.experimental import pallas as pl
from jax.experimental.pallas import tpu as pltpu

F32 = jnp.float32
BF = jnp.bfloat16
I32 = jnp.int32

D_MODEL = 1024
DEPTH = 4
GRID_W = 64
N_MIXERS = 4
LN_EPS = 1e-5
RMS_EPS = 1e-6
ALPHA = (2.0 * DEPTH) ** 0.25

A_HEADS = 16
A_KV_HEADS = 4
A_HEAD_DIM = 64
ROPE_BASE = 10000.0

B_PATTERNS = ((128, 1), (512, 4), (2048, 16))
B_HEADS_PER_GROUP = 4
B_HEAD_DIM = 64

C_GROUP = 16
C_N_GROUPS = D_MODEL // C_GROUP
C_STATE = 64
C_CHUNK = 16
_C_GL = 128 // C_GROUP

D_HEADS = 12
D_QK_DIM = 32
D_V_DIM = 64

REL_BUCKETS = 32
REL_MAX_DIST = 128

X_HEADS = 4
X_HEAD_DIM = D_MODEL // X_HEADS

N_EXPERTS = 16
EC_CAPACITY = 2

LANES = 128
BF16_ROWS = 16
VMEM_LIMIT = 56 * 1024 * 1024
NEG = -1e30
LOG2E = math.log2(math.e)


def _params(*sem):
    return pltpu.CompilerParams(dimension_semantics=sem, vmem_limit_bytes=VMEM_LIMIT)


def _tile(n, pref):
    t = min(n, pref)
    assert n % t == 0, (n, pref)
    return t


def _ln_rows(v, g, b):
    mu = jnp.mean(v, -1, keepdims=True)
    c = v - mu
    var = jnp.mean(c * c, -1, keepdims=True)
    return c * lax.rsqrt(var + LN_EPS) * g + b


def _dot_nt(a, b):
    return lax.dot_general(a, b, (((1,), (1,)), ((), ())), preferred_element_type=F32)


def _dot(a, b):
    return jnp.dot(a, b, preferred_element_type=F32)


_ONE_LANE = 64


def _with_ones_lane(y):
    lane = lax.broadcasted_iota(I32, y.shape, 1)
    return jnp.where(lane % LANES == _ONE_LANE, 1.0, y)


def _proj_kernel(x_ref, w_ref, o_ref, *, ones_from):
    y = _dot(x_ref[...].astype(BF), w_ref[...])
    if ones_from is None:
        o_ref[...] = y.astype(o_ref.dtype)
    else:
        o_ref[:, :ones_from] = y[:, :ones_from].astype(o_ref.dtype)
        o_ref[:, ones_from:] = _with_ones_lane(y[:, ones_from:]).astype(o_ref.dtype)


def _proj(x, w, tm=512, ones_from=None):
    n, k = x.shape
    m = w.shape[1]
    tm = _tile(n, tm)
    return pl.pallas_call(
        functools.partial(_proj_kernel, ones_from=ones_from),
        grid=(n // tm,),
        in_specs=[pl.BlockSpec((tm, k), lambda i: (i, 0)),
                  pl.BlockSpec((k, m), lambda i: (0, 0))],
        out_specs=pl.BlockSpec((tm, m), lambda i: (i, 0)),
        out_shape=jax.ShapeDtypeStruct((n, m), BF),
        compiler_params=_params("parallel"),
        name="proj",
    )(x, w)


def _post_kernel(x_ref, o_ref, w_ref, g_ref, b_ref, y_ref):
    h = _dot(o_ref[...], w_ref[...])
    y_ref[...] = _ln_rows(ALPHA * x_ref[...] + h, g_ref[...], b_ref[...])


def _post(x, o, w, g, b, tm=512):
    n = x.shape[0]
    ko = o.shape[1]
    tm = _tile(n, tm)
    return pl.pallas_call(
        _post_kernel,
        grid=(n // tm,),
        in_specs=[pl.BlockSpec((tm, D_MODEL), lambda i: (i, 0)),
                  pl.BlockSpec((tm, ko), lambda i: (i, 0)),
                  pl.BlockSpec((ko, D_MODEL), lambda i: (0, 0)),
                  pl.BlockSpec((1, D_MODEL), lambda i: (0, 0)),
                  pl.BlockSpec((1, D_MODEL), lambda i: (0, 0))],
        out_specs=pl.BlockSpec((tm, D_MODEL), lambda i: (i, 0)),
        out_shape=jax.ShapeDtypeStruct((n, D_MODEL), F32),
        compiler_params=_params("parallel"),
        name="post",
    )(x, o, w, g, b)


def _post_b_kernel(x_ref, o0_ref, o1_ref, o2_ref, l0_ref, l1_ref, l2_ref, w_ref, g_ref, b_ref, y_ref):
    l0, l1, l2 = l0_ref[...], l1_ref[...], l2_ref[...]
    m = jnp.maximum(jnp.maximum(l0, l1), l2)
    e0, e1, e2 = jnp.exp(l0 - m), jnp.exp(l1 - m), jnp.exp(l2 - m)
    inv = 1.0 / (e0 + e1 + e2)
    gw = B_HEADS_PER_GROUP * B_HEAD_DIM
    h = _dot((o0_ref[...].astype(F32) * (e0 * inv)).astype(BF), w_ref[0:gw, :])
    h = h + _dot((o1_ref[...].astype(F32) * (e1 * inv)).astype(BF), w_ref[gw:2 * gw, :])
    h = h + _dot((o2_ref[...].astype(F32) * (e2 * inv)).astype(BF), w_ref[2 * gw:3 * gw, :])
    y_ref[...] = _ln_rows(ALPHA * x_ref[...] + h, g_ref[...], b_ref[...])


def _post_b(x, os_, ls_, w, g, b, tm=512):
    n = x.shape[0]
    tm = _tile(n, tm)
    gw = B_HEADS_PER_GROUP * B_HEAD_DIM
    row = lambda i: (i, 0)
    fixed = lambda i: (0, 0)
    return pl.pallas_call(
        _post_b_kernel,
        grid=(n // tm,),
        in_specs=[pl.BlockSpec((tm, D_MODEL), row)] + [pl.BlockSpec((tm, gw), row)] * 6
        + [pl.BlockSpec((3 * gw, D_MODEL), fixed), pl.BlockSpec((1, D_MODEL), fixed),
           pl.BlockSpec((1, D_MODEL), fixed)],
        out_specs=pl.BlockSpec((tm, D_MODEL), row),
        out_shape=jax.ShapeDtypeStruct((n, D_MODEL), F32),
        compiler_params=_params("parallel"),
        name="post_b",
    )(x, *os_, *ls_, w, g, b)


def _post_c_kernel(x_ref, ys_ref, d_ref, w_ref, g_ref, b_ref, y_ref):
    x = x_ref[...]
    z = jax.nn.gelu(ys_ref[...] + d_ref[...] * x).astype(BF)
    h = _dot(z, w_ref[...])
    hh = h[:, :D_MODEL] * jax.nn.sigmoid(h[:, D_MODEL:])
    y_ref[...] = _ln_rows(ALPHA * x + hh, g_ref[...], b_ref[...])


def _post_c(x, ys, d, w, g, b, tm=512):
    n = x.shape[0]
    tm = _tile(n, tm)
    row = lambda i: (i, 0)
    fixed = lambda i: (0, 0)
    return pl.pallas_call(
        _post_c_kernel,
        grid=(n // tm,),
        in_specs=[pl.BlockSpec((tm, D_MODEL), row), pl.BlockSpec((tm, D_MODEL), row),
                  pl.BlockSpec((1, D_MODEL), fixed), pl.BlockSpec((D_MODEL, 2 * D_MODEL), fixed),
                  pl.BlockSpec((1, D_MODEL), fixed), pl.BlockSpec((1, D_MODEL), fixed)],
        out_specs=pl.BlockSpec((tm, D_MODEL), row),
        out_shape=jax.ShapeDtypeStruct((n, D_MODEL), F32),
        compiler_params=_params("parallel"),
        name="post_c",
    )(x, ys, d, w, g, b)


_A_QK_TILES = A_HEADS + A_KV_HEADS
_A_COLS = (_A_QK_TILES + A_KV_HEADS) * LANES


def _rope_partner():
    d = np.arange(A_HEAD_DIM)
    e = d % (A_HEAD_DIM // 2)
    lo = e < A_HEAD_DIM // 4
    return np.where(lo, d + A_HEAD_DIM // 4, d - A_HEAD_DIM // 4), np.where(lo, -1.0, 1.0).astype(np.float32)


def _prep_a(w_qkv, q_gain, k_gain, w_o):
    partner, sign = _rope_partner()
    nqk = _A_QK_TILES * A_HEAD_DIM
    wqk = w_qkv[:, :nqk].reshape(D_MODEL, _A_QK_TILES, A_HEAD_DIM)
    wsw = wqk[:, :, partner] * sign
    wqk = jnp.concatenate([wqk, wsw], -1).reshape(D_MODEL, _A_QK_TILES * LANES)
    wv = w_qkv[:, nqk:].reshape(D_MODEL, A_KV_HEADS, A_HEAD_DIM)
    wv = jnp.concatenate([wv, jnp.zeros_like(wv)], -1).reshape(D_MODEL, A_KV_HEADS * LANES)
    w = jnp.concatenate([wqk, wv], 1).astype(BF)
    gq = jnp.concatenate([q_gain, q_gain[partner]]) * (A_HEAD_DIM ** -0.5 * 0.5 * LOG2E)
    gk = jnp.concatenate([k_gain, k_gain[partner]])
    gains = jnp.stack([gq, gk], 0)
    wo = w_o.reshape(A_HEADS, A_HEAD_DIM, D_MODEL)
    wo = jnp.concatenate([wo, jnp.zeros_like(wo)], 1).reshape(A_HEADS * LANES, D_MODEL).astype(BF)
    return w, gains, wo


def _rope_table(s):
    pos = jnp.arange(s)
    rows, cols = (pos // GRID_W).astype(F32), (pos % GRID_W).astype(F32)
    half = A_HEAD_DIM // 2
    freqs = ROPE_BASE ** (-jnp.arange(0, half, 2, dtype=F32) / half)
    ang_r = rows[:, None] * freqs
    ang_c = cols[:, None] * freqs
    ang = jnp.concatenate([ang_r, ang_r, ang_c, ang_c], -1)
    return jnp.concatenate([jnp.cos(ang), jnp.sin(ang)], -1)


def _proj_a_kernel(x_ref, w_ref, cs_ref, g_ref, o_ref):
    xb = x_ref[...].astype(BF)
    cs = cs_ref[...]
    gq = cs * g_ref[0:1, :]
    gk = cs * g_ref[1:2, :]
    per = 4
    for g in range(_A_QK_TILES // per):
        y = _dot(xb, w_ref[:, LANES * per * g:LANES * per * (g + 1)])
        for hh in range(per):
            h = per * g + hh
            t = y[:, LANES * hh:LANES * (hh + 1)]
            r = lax.rsqrt(jnp.sum(t * t, -1, keepdims=True) * (1.0 / LANES) + RMS_EPS)
            e = t * r * (gq if h < A_HEADS else gk)
            o_ref[:, LANES * h:LANES * (h + 1)] = (e + pltpu.roll(e, LANES // 2, 1)).astype(BF)
    yv = _dot(xb, w_ref[:, _A_QK_TILES * LANES:])
    o_ref[:, _A_QK_TILES * LANES:] = _with_ones_lane(yv).astype(BF)


def _proj_a(x, w, cs, gains, s, tm=512):
    n = x.shape[0]
    tm = _tile(s, tm)
    per = s // tm
    return pl.pallas_call(
        _proj_a_kernel,
        grid=(n // tm,),
        in_specs=[pl.BlockSpec((tm, D_MODEL), lambda i: (i, 0)),
                  pl.BlockSpec((D_MODEL, _A_COLS), lambda i: (0, 0)),
                  pl.BlockSpec((tm, LANES), lambda i: (i % per, 0)),
                  pl.BlockSpec((2, LANES), lambda i: (0, 0))],
        out_specs=pl.BlockSpec((tm, _A_COLS), lambda i: (i, 0)),
        out_shape=jax.ShapeDtypeStruct((n, _A_COLS), BF),
        compiler_params=_params("parallel"),
        name="proj_a",
    )(x, w, cs, gains)


def _softmax_step(s, v, m_ref, acc_ref):
    tk = s.shape[1]
    m_old = m_ref[...]
    m_new = jnp.maximum(m_old, jnp.max(s, -1, keepdims=True))
    p = jnp.concatenate([jnp.exp2(s[:, LANES * c:LANES * (c + 1)] - m_new).astype(BF) for c in range(tk // LANES)],
                        axis=1)
    acc_ref[...] = jnp.exp2(m_old - m_new) * acc_ref[...] + _dot(p, v)
    m_ref[...] = m_new


def _softmax_init(m_ref, acc_ref):
    m_ref[...] = jnp.full(m_ref.shape, -jnp.inf, F32)
    acc_ref[...] = jnp.zeros_like(acc_ref)


def _flash_a_kernel(q_ref, k_ref, v_ref, o_ref, m_ref, acc_ref, *, tk, nk):
    tq = q_ref.shape[0]
    rep = A_HEADS // A_KV_HEADS
    q = jnp.concatenate([q_ref[:, LANES * r:LANES * (r + 1)] for r in range(rep)], axis=0)
    _softmax_init(m_ref, acc_ref)

    def body(j, carry):
        off = pl.multiple_of(j * tk, tk)
        _softmax_step(_dot_nt(q, k_ref[pl.ds(off, tk), :]), v_ref[pl.ds(off, tk), :], m_ref, acc_ref)
        return carry

    lax.fori_loop(0, nk, body, 0, unroll=4)
    acc = acc_ref[...]
    o = acc / acc[:, _ONE_LANE:_ONE_LANE + 1]
    for r in range(rep):
        o_ref[:, LANES * r:LANES * (r + 1)] = o[r * tq:(r + 1) * tq].astype(BF)


def _flash_a(qkv, tq=256, tk=2048):
    b, s, _ = qkv.shape
    tq, tk = _tile(s, tq), _tile(s, tk)
    rep = A_HEADS // A_KV_HEADS
    gw = rep * LANES
    return pl.pallas_call(
        functools.partial(_flash_a_kernel, tk=tk, nk=s // tk),
        grid=(b, A_KV_HEADS, s // tq),
        in_specs=[pl.BlockSpec((None, tq, gw), lambda bi, g, i: (bi, i, g)),
                  pl.BlockSpec((None, s, LANES), lambda bi, g, i: (bi, 0, A_HEADS + g)),
                  pl.BlockSpec((None, s, LANES), lambda bi, g, i: (bi, 0, _A_QK_TILES + g))],
        out_specs=pl.BlockSpec((None, tq, gw), lambda bi, g, i: (bi, i, g)),
        out_shape=jax.ShapeDtypeStruct((b, s, A_HEADS * LANES), BF),
        scratch_shapes=[pltpu.VMEM((rep * tq, LANES), F32)] * 2,
        compiler_params=_params("parallel", "parallel", "arbitrary"),
        name="flash_a",
    )(qkv, qkv, qkv)


def _bucket(rel):
    half = REL_BUCKETS // 2
    max_exact = half // 2
    n = jnp.abs(rel)
    large = max_exact + (jnp.log(jnp.maximum(n, 1).astype(F32) / max_exact)
                         / math.log(REL_MAX_DIST / max_exact) * (half - max_exact)).astype(I32)
    large = jnp.minimum(large, half - 1)
    return jnp.where(rel > 0, half, 0) + jnp.where(n < max_exact, n, large)


_B_GW = B_HEADS_PER_GROUP * B_HEAD_DIM
_B_T = 256


def _pad_heads(w, heads, dim):
    w = w.reshape(w.shape[0], heads, dim)
    return jnp.concatenate([w, jnp.zeros((w.shape[0], heads, LANES - dim), w.dtype)], -1).reshape(
        w.shape[0], heads * LANES)


def _prep_b(w_qkv, w_o):
    c = len(B_PATTERNS) * B_HEADS_PER_GROUP * B_HEAD_DIM
    w = jnp.concatenate([w_qkv[:, :c] * (B_HEAD_DIM ** -0.5), w_qkv[:, c:]], 1).astype(BF)
    return w, w_o.astype(BF)


def _toeplitz(vec, t):
    flat = jnp.tile(vec, (1,) * (vec.ndim - 1) + (t,))[..., :t * (2 * t - 1)]
    return flat.reshape(vec.shape[:-1] + (t, 2 * t - 1))[..., :t]


def _band_tiles(g, t):
    window, dil = B_PATTERNS[g]
    return pl.cdiv((window // (2 * dil)) * dil, t)


def _band_bias(rel_bias, g, t):
    window, dil = B_PATTERNS[g]
    reach = (window // (2 * dil)) * dil
    n = _band_tiles(g, t)
    rel0 = (jnp.arange(2 * t) + t) % (2 * t) - t
    rel = jnp.arange(-n, n + 1)[:, None] * t + rel0[None, :]
    bias = rel_bias[_bucket(rel)][:, :, g * B_HEADS_PER_GROUP:(g + 1) * B_HEADS_PER_GROUP]
    bias = jnp.where(((rel % dil == 0) & (jnp.abs(rel) <= reach))[:, :, None], bias, NEG)
    return _toeplitz(jnp.transpose(bias, (2, 0, 1)).astype(F32), t)


def _band_kernel(q_ref, k_ref, v_ref, bias_ref, o_ref, lse_ref, *, n, nk):
    t = q_ref.shape[0]
    i = pl.program_id(2)
    qf = q_ref[...].astype(F32)
    lane = lax.broadcasted_iota(I32, (t, LANES), 1)
    offs = [pl.multiple_of(jnp.clip(i + o, 0, nk - 1) * t, t) for o in range(-n, n + 1)]
    inside = [(i + o >= 0) & (i + o < nk) for o in range(-n, n + 1)]
    outs, lses = [], []
    for hh in range(2):
        q = jnp.where((lane >= B_HEAD_DIM * hh) & (lane < B_HEAD_DIM * (hh + 1)), qf, 0.0).astype(BF)
        logits = [jnp.where(ok, _dot_nt(q, k_ref[pl.ds(off, t), :]) + bias_ref[hh, c], NEG)
                  for c, (off, ok) in enumerate(zip(offs, inside))]
        m = logits[0].max(-1, keepdims=True)
        for s in logits[1:]:
            m = jnp.maximum(m, s.max(-1, keepdims=True))
        l = jnp.zeros((t, 1), F32)
        acc = jnp.zeros((t, LANES), F32)
        for s, off in zip(logits, offs):
            p = jnp.exp(s - m)
            l = l + jnp.sum(p, -1, keepdims=True)
            acc = acc + _dot(p.astype(BF), v_ref[pl.ds(off, t), :])
        outs.append(acc / l)
        lses.append(jnp.broadcast_to(m + jnp.log(l), (t, LANES)))
    first = lane < B_HEAD_DIM
    o_ref[...] = jnp.where(first, outs[0], outs[1]).astype(BF)
    lse_ref[...] = jnp.where(first, lses[0], lses[1])


def _band_attention(qkv, bias, g, t):
    b, s, c = qkv.shape
    third = c // 3 // LANES
    pairs = B_HEADS_PER_GROUP // 2
    n = _band_tiles(g, t)
    spec = lambda base: pl.BlockSpec((None, s, LANES), lambda bi, hp, i: (bi, 0, base + g * pairs + hp))
    out_spec = pl.BlockSpec((None, t, LANES), lambda bi, hp, i: (bi, i, hp))
    o, lse = pl.pallas_call(
        functools.partial(_band_kernel, n=n, nk=s // t),
        grid=(b, pairs, s // t),
        in_specs=[pl.BlockSpec((None, t, LANES), lambda bi, hp, i: (bi, i, g * pairs + hp)),
                  spec(third), spec(2 * third),
                  pl.BlockSpec((2, 2 * n + 1, t, t), lambda bi, hp, i: (hp, 0, 0, 0))],
        out_specs=[out_spec, out_spec],
        out_shape=[jax.ShapeDtypeStruct((b, s, _B_GW), BF), jax.ShapeDtypeStruct((b, s, _B_GW), F32)],
        compiler_params=_params("parallel", "parallel", "arbitrary"),
        name="band_%d" % g,
    )(qkv, qkv, qkv, bias)
    return o.reshape(b * s, _B_GW), lse.reshape(b * s, _B_GW)


def _prep_c(lam_re, lam_im, log_dt, b_re, b_im, c_re, c_im):
    hp = lax.Precision.HIGHEST
    L, P, C = C_CHUNK, C_STATE, C_GROUP
    lr, li = lam_re.astype(F32), lam_im.astype(F32)
    dt = jnp.exp(log_dt.astype(F32))[..., None]
    mag = jnp.exp(lr * dt)
    ar, ai = mag * jnp.cos(li * dt), mag * jnp.sin(li * dt)
    den = lr * lr + li * li
    zr = ((ar - 1.0) * lr + ai * li) / den
    zi = (ai * lr - (ar - 1.0) * li) / den
    br, bi = b_re.astype(F32), b_im.astype(F32)
    bbr = zr[..., None] * br - zi[..., None] * bi
    bbi = zr[..., None] * bi + zi[..., None] * br
    cr, ci = c_re.astype(F32), c_im.astype(F32)
    prs, pis = [jnp.ones_like(ar)], [jnp.zeros_like(ai)]
    for _ in range(L):
        pr_, pi_ = prs[-1], pis[-1]
        prs.append(ar * pr_ - ai * pi_)
        pis.append(ar * pi_ + ai * pr_)
    pr, pi = jnp.stack(prs, 0), jnp.stack(pis, 0)

    def lag(pr_k, pi_k):
        tr = pr_k[..., None] * bbr - pi_k[..., None] * bbi
        ti = pr_k[..., None] * bbi + pi_k[..., None] * bbr
        return (jnp.einsum('dgop,kdgpi->kdgoi', cr, tr, precision=hp)
                - jnp.einsum('dgop,kdgpi->kdgoi', ci, ti, precision=hp))

    kern = lag(pr[:L], pi[:L])
    lags = jnp.concatenate([kern[:0:-1, 1], (kern[0, 0] + kern[0, 1])[None], kern[1:, 0]], 0)

    def summ(d, powers):
        pr_k, pi_k = pr[powers, d], pi[powers, d]
        sr = pr_k[..., None] * bbr[d] - pi_k[..., None] * bbi[d]
        si = pr_k[..., None] * bbi[d] + pi_k[..., None] * bbr[d]
        s = jnp.concatenate([sr, si], 2)
        return jnp.transpose(s, (1, 0, 3, 2)).reshape(C_N_GROUPS, L * C, 2 * P)

    w_sum = jnp.concatenate([summ(0, np.arange(L - 1, -1, -1)), summ(1, np.arange(L))], -1)

    def outw(d, powers):
        pr_k, pi_k = pr[powers, d], pi[powers, d]
        wr = cr[d][None] * pr_k[:, :, None, :] - ci[d][None] * pi_k[:, :, None, :]
        wi = -(cr[d][None] * pi_k[:, :, None, :] + ci[d][None] * pr_k[:, :, None, :])
        w = jnp.concatenate([wr, wi], -1)
        return jnp.transpose(w, (1, 3, 0, 2)).reshape(C_N_GROUPS, 2 * P, L * C)

    w_state = jnp.concatenate([outw(0, np.arange(1, L + 1)), outw(1, np.arange(L, 0, -1))], 1)

    nq = C_N_GROUPS // _C_GL
    npair = _C_GL // 2
    blk = L * _C_GL * C
    lane = jnp.arange(LANES)
    gsel = (lane[None, :] // C == jnp.arange(_C_GL)[:, None])
    src = jnp.swapaxes(lags, -1, -2).astype(BF).reshape(2 * L - 1, nq, _C_GL, C, C)
    src = jnp.where(gsel[None, None, :, None, :], jnp.tile(src, (1, 1, 1, 1, _C_GL)), 0)
    w_lag = jnp.transpose(src.reshape(2 * L - 1, nq, LANES, LANES), (1, 0, 2, 3))
    psel = (lane[None, :] // P == jnp.arange(2)[:, None])
    ksel = jnp.eye(npair, dtype=bool)
    src = jnp.tile(w_sum.astype(BF).reshape(nq, npair, 2, L, C, 1, 4, P), (1, 1, 1, 1, 1, npair, 1, 2))
    src = jnp.where(psel[None, None, :, None, None, None, None, :]
                    & ksel[None, :, None, None, None, :, None, None], src, 0)
    w_sum = jnp.transpose(src, (0, 3, 1, 2, 4, 5, 6, 7)).reshape(nq, blk, _C_GL * 4 * P)
    osel = (lane[None, None, :] // C
            == (2 * jnp.arange(npair)[:, None, None] + jnp.arange(2)[None, :, None]))
    src = jnp.tile(w_state.astype(BF).reshape(nq, npair, 2, 4, P, L, C), (1, 1, 1, 1, 1, 1, _C_GL))
    src = jnp.where(osel[None, :, :, None, None, None, :], src, 0)
    w_state = jnp.transpose(src, (0, 1, 3, 2, 4, 5, 6)).reshape(nq, _C_GL * 4 * P, blk)
    dec = jnp.stack([pr[L, 0], pi[L, 0], pr[L, 1], pi[L, 1]], 0).reshape(4, nq * npair, 2 * P)
    dec = jnp.broadcast_to(jnp.transpose(dec, (1, 0, 2))[:, :, None, :], (nq * npair, 4, 8, 2 * P))
    return w_sum.astype(BF), w_lag, w_state.astype(BF), dec


def _chunk_rows(x_ref, nb, rc):
    L = C_CHUNK
    rows = [jnp.concatenate([x_ref[b, pl.ds(j, rc, stride=L), :].astype(BF) for j in range(L)], axis=1)
            for b in range(nb)]
    return jnp.concatenate(rows, axis=0)


def _s5_sum_kernel(x_ref, w_ref, s_ref, *, nb, rc):
    s = _dot(_chunk_rows(x_ref, nb, rc), w_ref[...])
    for b in range(nb):
        for t in range(s_ref.shape[0]):
            s_ref[t, pl.ds(b, rc, stride=nb), :] = s[b * rc:(b + 1) * rc, LANES * t:LANES * (t + 1)]


def _s5_scan_kernel(s_ref, dec_ref, e_ref, *, nc, nb):
    units = s_ref.shape[0] // 4
    dec = [[dec_ref[u, k, 0:nb, :] for k in range(4)] for u in range(units)]

    def body(t, carry):
        rf = pl.ds(pl.multiple_of(t * nb, nb), nb)
        rb = pl.ds(pl.multiple_of((nc - 1 - t) * nb, nb), nb)
        new = []
        for u in range(units):
            fr, fi, br, bi = carry[4 * u:4 * u + 4]
            arf, aif, arb, aib = dec[u]
            e_ref[4 * u, rf, :] = fr
            e_ref[4 * u + 1, rf, :] = fi
            e_ref[4 * u + 2, rb, :] = br
            e_ref[4 * u + 3, rb, :] = bi
            new += [fr * arf - fi * aif + s_ref[4 * u, rf, :], fi * arf + fr * aif + s_ref[4 * u + 1, rf, :],
                    br * arb - bi * aib + s_ref[4 * u + 2, rb, :], bi * arb + br * aib + s_ref[4 * u + 3, rb, :]]
        return tuple(new)

    z = jnp.zeros((nb, LANES), F32)
    lax.fori_loop(0, nc, body, (z,) * (4 * units))


def _s5_out_kernel(x_ref, e_ref, lag_ref, wc_ref, y_ref, wi_ref, *, nb, rc):
    L = C_CHUNK
    half = pl.program_id(2)

    @pl.when((pl.program_id(1) == 0) & (half == 0))
    def _():
        for j in range(L):
            for i in range(L):
                wi_ref[i // (L // 2), LANES * j:LANES * (j + 1), LANES * (i % (L // 2)):LANES * (i % (L // 2) + 1)] = (
                    lag_ref[i - j + L - 1])

    xc = _chunk_rows(x_ref, nb, rc)
    ec = jnp.concatenate(
        [jnp.concatenate([e_ref[t, pl.ds(b, rc, stride=nb), :].astype(BF) for t in range(e_ref.shape[0])], axis=1)
         for b in range(nb)], axis=0)
    y = _dot(xc, wi_ref[half]) + _dot(ec, wc_ref[...])
    for b in range(nb):
        for ii in range(L // 2):
            y_ref[b, pl.ds(half * (L // 2) + ii, rc, stride=L), :] = y[b * rc:(b + 1) * rc, LANES * ii:LANES * (ii + 1)]


def _s5(x, b, s, w_sum, w_lag, w_state, dec):
    L = C_CHUNK
    nc = s // L
    nq = D_MODEL // LANES
    nsl = w_sum.shape[2] // LANES
    blk = L * LANES
    rc = _tile(nc, max(8, 512 // b))
    x3 = x.reshape(b, s, D_MODEL)
    x_spec = pl.BlockSpec((b, rc * L, LANES), lambda q, c, *_: (0, c, q))
    sums = pl.pallas_call(
        functools.partial(_s5_sum_kernel, nb=b, rc=rc),
        grid=(nq, nc // rc),
        in_specs=[x_spec, pl.BlockSpec((None, blk, nsl * LANES), lambda q, c: (q, 0, 0))],
        out_specs=pl.BlockSpec((nsl, rc * b, LANES), lambda q, c: (q, c, 0)),
        out_shape=jax.ShapeDtypeStruct((nq * nsl, nc * b, LANES), F32),
        compiler_params=_params("parallel", "arbitrary"),
        name="s5_sum",
    )(x3, w_sum)
    upb = 2
    ent = pl.pallas_call(
        functools.partial(_s5_scan_kernel, nc=nc, nb=b),
        grid=(nq * nsl // (4 * upb),),
        in_specs=[pl.BlockSpec((4 * upb, nc * b, LANES), lambda i: (i, 0, 0)),
                  pl.BlockSpec((upb, 4, 8, LANES), lambda i: (i, 0, 0, 0))],
        out_specs=pl.BlockSpec((4 * upb, nc * b, LANES), lambda i: (i, 0, 0)),
        out_shape=jax.ShapeDtypeStruct((nq * nsl, nc * b, LANES), F32),
        compiler_params=_params("parallel"),
        name="s5_scan",
    )(sums, dec)
    y = pl.pallas_call(
        functools.partial(_s5_out_kernel, nb=b, rc=rc),
        grid=(nq, nc // rc, 2),
        in_specs=[x_spec,
                  pl.BlockSpec((nsl, rc * b, LANES), lambda q, c, h: (q, c, 0)),
                  pl.BlockSpec((None, 2 * L - 1, LANES, LANES), lambda q, c, h: (q, 0, 0, 0)),
                  pl.BlockSpec((None, nsl * LANES, blk // 2), lambda q, c, h: (q, 0, h))],
        out_specs=pl.BlockSpec((b, rc * L, LANES), lambda q, c, h: (0, c, q)),
        out_shape=jax.ShapeDtypeStruct((b, s, D_MODEL), F32),
        scratch_shapes=[pltpu.VMEM((2, blk, blk // 2), BF)],
        compiler_params=_params("arbitrary", "arbitrary", "arbitrary"),
        name="s5_out",
    )(x3, ent, w_lag, w_state)
    return y.reshape(b * s, D_MODEL)


_D_T = 512


def _prep_d(w_qkv, norm_gain, w_o, lambda_init):
    qk_w = D_HEADS * 2 * D_QK_DIM
    wq = _pad_heads(w_qkv[:, :qk_w] * (D_QK_DIM ** -0.5 * LOG2E), D_HEADS, 2 * D_QK_DIM)
    wk = _pad_heads(w_qkv[:, qk_w:2 * qk_w], D_HEADS, 2 * D_QK_DIM)
    wv = _pad_heads(w_qkv[:, 2 * qk_w:], D_HEADS, D_V_DIM)
    w = jnp.concatenate([wq, wk, wv], 1).astype(BF)
    gain = jnp.concatenate([norm_gain * (1.0 - lambda_init), jnp.zeros((LANES - D_V_DIM,), F32)])[None, :]
    wo = w_o.reshape(D_HEADS, D_V_DIM, D_MODEL)
    wo = jnp.concatenate([wo, jnp.zeros_like(wo)], 1).reshape(D_HEADS * LANES, D_MODEL).astype(BF)
    return w, gain, wo


def _diff_bias(rel_bias, t):
    rel0 = (jnp.arange(2 * t) + t) % (2 * t) - t
    rel = jnp.arange(-2, 3)[:, None] * t + rel0[None, :]
    return jnp.transpose(rel_bias[_bucket(rel)] * LOG2E, (2, 0, 1)).astype(F32)


def _flash_d_kernel(q_ref, k_ref, v_ref, vec_ref, lam_ref, gain_ref, o_ref, m_ref, acc_ref, bias_ref, *, kt, nk,
                    lambda_init):
    t = q_ref.shape[0]
    i = pl.program_id(2)

    @pl.when(i == 0)
    def _():
        for d in range(bias_ref.shape[0]):
            full = jnp.broadcast_to(vec_ref[d:d + 1, :], (t, 2 * t))
            bias_ref[d] = pltpu.roll(full, 0, 1, stride=1, stride_axis=0)[:, :t]

    qv = q_ref[...].astype(F32)
    lane = lax.broadcasted_iota(I32, (t, LANES), 1)
    q0 = jnp.where(lane < D_QK_DIM, qv, 0.0).astype(BF)
    q1 = jnp.where((lane >= D_QK_DIM) & (lane < 2 * D_QK_DIM), qv, 0.0).astype(BF)
    q = jnp.concatenate([q0, q1], axis=0)
    _softmax_init(m_ref, acc_ref)

    def body(j, carry):
        off = pl.multiple_of(j * (kt * t), kt * t)
        bias = jnp.concatenate([bias_ref[jnp.clip(j * kt + c - i, -2, 2) + 2] for c in range(kt)], axis=1)
        s = _dot_nt(q, k_ref[pl.ds(off, kt * t), :])
        s = (s.reshape(2, t, kt * t) + bias[None]).reshape(2 * t, kt * t)
        _softmax_step(s, v_ref[pl.ds(off, kt * t), :], m_ref, acc_ref)
        return carry

    lax.fori_loop(0, nk, body, 0, unroll=4)
    lf = lam_ref[...]
    lam = (jnp.exp(jnp.sum(lf[0:1] * lf[1:2], keepdims=True))
           - jnp.exp(jnp.sum(lf[2:3] * lf[3:4], keepdims=True)) + lambda_init)
    acc = acc_ref[...]
    on = acc / acc[:, _ONE_LANE:_ONE_LANE + 1]
    o = jnp.where(lane < D_V_DIM, on[:t] - lam * on[t:], 0.0)
    ms = jnp.sum(o * o, -1, keepdims=True) * (1.0 / D_V_DIM)
    o_ref[...] = (o * lax.rsqrt(ms + RMS_EPS) * gain_ref[...]).astype(BF)


def _flash_d(qkv, bias, lam, gain, lambda_init):
    b, s, _ = qkv.shape
    t = bias.shape[-1] // 2
    kt = 2 if s % (2 * t) == 0 else 1
    return pl.pallas_call(
        functools.partial(_flash_d_kernel, kt=kt, nk=s // (kt * t), lambda_init=lambda_init),
        grid=(b, D_HEADS, s // t),
        in_specs=[pl.BlockSpec((None, t, LANES), lambda bi, h, i: (bi, i, h)),
                  pl.BlockSpec((None, s, LANES), lambda bi, h, i: (bi, 0, D_HEADS + h)),
                  pl.BlockSpec((None, s, LANES), lambda bi, h, i: (bi, 0, 2 * D_HEADS + h)),
                  pl.BlockSpec((None, 5, 2 * t), lambda bi, h, i: (h, 0, 0)),
                  pl.BlockSpec((4, D_QK_DIM), lambda bi, h, i: (0, 0)),
                  pl.BlockSpec((1, LANES), lambda bi, h, i: (0, 0))],
        out_specs=pl.BlockSpec((None, t, LANES), lambda bi, h, i: (bi, i, h)),
        out_shape=jax.ShapeDtypeStruct((b, s, D_HEADS * LANES), BF),
        scratch_shapes=[pltpu.VMEM((2 * t, LANES), F32)] * 2 + [pltpu.VMEM((5, t, t), F32)],
        compiler_params=_params("parallel", "parallel", "arbitrary"),
        name="flash_d",
    )(qkv, qkv, qkv, bias, lam, gain)


def _cross_kernel(x_ref, kv_ref, wq_ref, wo_ref, g_ref, b_ref, wr_ref, y_ref, ybf_ref, aff_ref, *, parts):
    tp = x_ref.shape[0] // parts
    wr = wr_ref[...]
    wh = wr.astype(BF)
    wl = (wr - wh.astype(F32)).astype(BF)
    for part in range(parts):
        rows = slice(tp * part, tp * (part + 1))
        x = x_ref[rows, :]
        q = (_dot(x.astype(BF), wq_ref[...]) * (X_HEAD_DIM ** -0.5)).astype(BF)
        outs = []
        for h in range(X_HEADS):
            sl = slice(X_HEAD_DIM * h, X_HEAD_DIM * (h + 1))
            s = _dot_nt(q[:, sl], kv_ref[:, sl])
            p = jnp.exp(s - jnp.max(s, -1, keepdims=True))
            l = jnp.sum(p, -1, keepdims=True)
            vh = kv_ref[:, D_MODEL + X_HEAD_DIM * h:D_MODEL + X_HEAD_DIM * (h + 1)]
            outs.append((_dot(p.astype(BF), vh) / l).astype(BF))
        o = jnp.concatenate(outs, axis=1)
        y = _ln_rows(ALPHA * x + _dot(o, wo_ref[...]), g_ref[...], b_ref[...])
        y_ref[rows, :] = y
        yh = y.astype(BF)
        ybf_ref[rows, :] = yh
        yl = (y - yh.astype(F32)).astype(BF)
        lg = _dot_nt(wh, yh) + _dot_nt(wh, yl) + _dot_nt(wl, yh)
        e = jnp.exp(lg - jnp.max(lg, 0, keepdims=True))
        aff = e / jnp.sum(e, 0, keepdims=True)
        for c in range(tp // LANES):
            aff_ref[part * (tp // LANES) + c] = aff[:, LANES * c:LANES * (c + 1)]


def _cross(x, kv, wq, wo, g, b, wr_t, s, mem_len, tm=1024, parts=2):
    n = x.shape[0]
    tm = _tile(s, tm)
    per = s // tm
    fixed = lambda i: (0, 0)
    return pl.pallas_call(
        functools.partial(_cross_kernel, parts=parts),
        grid=(n // tm,),
        in_specs=[pl.BlockSpec((tm, D_MODEL), lambda i: (i, 0)),
                  pl.BlockSpec((mem_len, 2 * D_MODEL), lambda i: (i // per, 0)),
                  pl.BlockSpec((D_MODEL, D_MODEL), fixed), pl.BlockSpec((D_MODEL, D_MODEL), fixed),
                  pl.BlockSpec((1, D_MODEL), fixed), pl.BlockSpec((1, D_MODEL), fixed),
                  pl.BlockSpec((N_EXPERTS, D_MODEL), fixed)],
        out_specs=[pl.BlockSpec((tm, D_MODEL), lambda i: (i, 0)),
                   pl.BlockSpec((tm, D_MODEL), lambda i: (i, 0)),
                   pl.BlockSpec((tm // LANES, N_EXPERTS, LANES), lambda i: (i, 0, 0))],
        out_shape=[jax.ShapeDtypeStruct((n, D_MODEL), F32),
                   jax.ShapeDtypeStruct((n, D_MODEL), BF),
                   jax.ShapeDtypeStruct((n // LANES, N_EXPERTS, LANES), F32)],
        compiler_params=_params("parallel"),
        name="cross",
    )(x, kv, wq, wo, g, b, wr_t)


def _select_kernel(a_ref, pos_ref, st_ref, *, k, nbits):
    nt = a_ref.shape[0]
    shape = (nt, N_EXPERTS, LANES)
    kf = float(k)

    def keys():
        return lax.bitcast_convert_type(a_ref[...], I32)

    def count(mask):
        c = jnp.sum(jnp.where(mask, 1.0, 0.0), axis=0, keepdims=True)
        return jnp.sum(c, axis=2, keepdims=True)

    def value_step(it, thr):
        cand = thr | jnp.left_shift(jnp.int32(1), 30 - it)
        return jnp.where(count(keys() >= cand) >= kf, cand, thr)

    thr = lax.fori_loop(0, 31, value_step, jnp.zeros((1, N_EXPERTS, 1), I32))
    need = kf - count(keys() > thr)
    idx = lax.broadcasted_iota(I32, shape, 0) * LANES + lax.broadcasted_iota(I32, shape, 2)

    def index_step(it, ithr):
        cand = ithr | jnp.left_shift(jnp.int32(1), nbits - 1 - it)
        return jnp.where(count((keys() == thr) & (idx < cand)) < need, cand, ithr)

    ithr = lax.fori_loop(0, nbits, index_step, jnp.zeros((1, N_EXPERTS, 1), I32))
    thr2, ithr2 = thr[0], ithr[0]
    upper = jnp.where(lax.broadcasted_iota(I32, (LANES, LANES), 0) <= lax.broadcasted_iota(I32, (LANES, LANES), 1),
                      1.0, 0.0).astype(BF)
    lane = lax.broadcasted_iota(I32, (N_EXPERTS, LANES), 1)

    def tile_step(j, carry):
        kj = lax.bitcast_convert_type(a_ref[j], I32)
        sel = (kj > thr2) | ((kj == thr2) & (j * LANES + lane <= ithr2))
        m = jnp.where(sel, 1.0, 0.0)
        inc = _dot(m.astype(BF), upper)
        pos_ref[j] = jnp.where(sel, inc - m + carry, -1.0).astype(I32)
        st_ref[j] = jnp.broadcast_to(carry, (N_EXPERTS, LANES)).astype(I32)
        return carry + inc[:, LANES - 1:LANES]

    lax.fori_loop(0, nt, tile_step, jnp.zeros((N_EXPERTS, 1), F32))


def _select(aff3, k):
    nt = aff3.shape[0]
    nbits = max(1, int(math.ceil(math.log2(nt * LANES))))
    shp = jax.ShapeDtypeStruct(aff3.shape, I32)
    return pl.pallas_call(
        functools.partial(_select_kernel, k=k, nbits=nbits),
        out_shape=[shp, shp],
        compiler_params=pltpu.CompilerParams(vmem_limit_bytes=VMEM_LIMIT),
        name="select",
    )(aff3)


_GATHER_ROWS = LANES + BF16_ROWS
_GATHER_ROWS_SMALL = 3 * BF16_ROWS

_MOE_GROUP = 2


def _moe_ffn_kernel(st_ref, x_ref, pos_ref, aff_ref, wg_ref, wu_ref, wd_ref, ye_ref, buf_ref, gate_ref, *, nb, ncf,
                    nch, sub, ck):
    ep = buf_ref.shape[0]
    grp = pl.program_id(0)
    t = pl.program_id(1)

    @pl.when(t == 0)
    def _():
        buf_ref[...] = jnp.zeros_like(buf_ref)
        gate_ref[...] = jnp.zeros_like(gate_ref)

    @pl.when(t < nb)
    def _():
        def place(s, k, e, base, nrows):
            rows = pl.ds(base, nrows)
            hit = lax.broadcasted_iota(I32, (nrows, LANES), 0) == pos_ref[s, pl.ds(e, 1), :] - base
            xs = x_ref[LANES * s:LANES * (s + 1), :]
            buf_ref[k, rows, :] = buf_ref[k, rows, :] + _dot(jnp.where(hit, 1.0, 0.0).astype(BF), xs).astype(BF)
            gate_ref[k, rows, :] = gate_ref[k, rows, :] + jnp.sum(
                jnp.where(hit, aff_ref[s, pl.ds(e, 1), :], 0.0), -1, keepdims=True)

        def base_of(s, k):
            st = st_ref[grp * ep + k, t * sub + s]
            return pl.multiple_of((st // BF16_ROWS) * BF16_ROWS, BF16_ROWS)

        pairs = [(s, k) for s in range(sub) for k in range(ep)]
        few = functools.reduce(
            jnp.logical_and,
            [st_ref[grp * ep + k, t * sub + s + 1] - base_of(s, k) <= _GATHER_ROWS_SMALL for s, k in pairs])

        @pl.when(few)
        def _():
            for s, k in pairs:
                place(s, k, grp * ep + k, base_of(s, k), _GATHER_ROWS_SMALL)

        @pl.when(jnp.logical_not(few))
        def _():
            for s, k in pairs:
                place(s, k, grp * ep + k, base_of(s, k), _GATHER_ROWS)

    @pl.when(t >= nb)
    def _():
        k = (t - nb) // nch
        c = (t - nb) - k * nch

        @pl.when(c < ncf)
        def _():
            rows = pl.ds(pl.multiple_of(c * ck, ck), ck)
            xe = buf_ref[k, rows, :]
            h = (jax.nn.silu(_dot(xe, wg_ref[...])) * _dot(xe, wu_ref[...])).astype(BF)
            ye_ref[...] = (_dot(h, wd_ref[...]) * gate_ref[k, rows, :]).astype(BF)

        @pl.when(c >= ncf)
        def _():
            ye_ref[...] = jnp.zeros_like(ye_ref)


def _moe_ffn(starts, xbf, pos3, aff3, wg, wu, wd, cap, tb=1024):
    n = xbf.shape[0]
    tb = _tile(n, tb)
    nb = n // tb
    sub = tb // LANES
    ck = _tile(cap, 1024)
    ncf = cap // ck
    nch = ncf + pl.cdiv(_WIN_ROWS, ck)
    dff = wg.shape[-1]
    ep = _MOE_GROUP
    blk = lambda g, t, st: (jnp.minimum(t, nb - 1), 0)
    blk3 = lambda g, t, st: (jnp.minimum(t, nb - 1), 0, 0)
    expert = lambda g, t: g * ep + jnp.clip((t - nb) // nch, 0, ep - 1)
    wmap = lambda g, t, st: (expert(g, t), 0, 0)
    return pl.pallas_call(
        functools.partial(_moe_ffn_kernel, nb=nb, ncf=ncf, nch=nch, sub=sub, ck=ck),
        grid_spec=pltpu.PrefetchScalarGridSpec(
            num_scalar_prefetch=1,
            grid=(N_EXPERTS // ep, nb + ep * nch),
            in_specs=[pl.BlockSpec((tb, D_MODEL), blk),
                      pl.BlockSpec((sub, N_EXPERTS, LANES), blk3),
                      pl.BlockSpec((sub, N_EXPERTS, LANES), blk3),
                      pl.BlockSpec((None, D_MODEL, dff), wmap),
                      pl.BlockSpec((None, D_MODEL, dff), wmap),
                      pl.BlockSpec((None, dff, D_MODEL), wmap)],
            out_specs=pl.BlockSpec((None, ck, D_MODEL),
                                   lambda g, t, st: (expert(g, t), jnp.maximum(t - nb, 0) % nch, 0)),
            scratch_shapes=[pltpu.VMEM((ep, cap + _GATHER_ROWS, D_MODEL), BF),
                            pltpu.VMEM((ep, cap + _GATHER_ROWS, 1), F32)]),
        out_shape=jax.ShapeDtypeStruct((N_EXPERTS, nch * ck, D_MODEL), BF),
        compiler_params=_params("arbitrary", "arbitrary"),
        name="moe_ffn",
    )(starts, xbf, pos3, aff3, wg, wu, wd)


_WIN_STEP = 128
_COMB_TILES = 2
_WIN_ROWS = _WIN_STEP + (_COMB_TILES - 1) * LANES + _GATHER_ROWS


def _combine_kernel(st_ref, x_ref, pos_ref, g_ref, b_ref, *rest):
    ye_refs, y_ref = rest[:N_EXPERTS], rest[N_EXPERTS]
    j = pl.program_id(0)
    col = lax.broadcasted_iota(I32, (LANES, _GATHER_ROWS), 1)
    for u in range(_COMB_TILES):
        tok = slice(LANES * u, LANES * (u + 1))
        acc = None
        for e in range(N_EXPERTS):
            st = st_ref[e, j * _COMB_TILES + u]
            base = (st // BF16_ROWS) * BF16_ROWS
            win = (st_ref[e, j * _COMB_TILES] // _WIN_STEP) * _WIN_STEP
            sub = pl.multiple_of(base - win, BF16_ROWS)
            onehot = jnp.where(col == pos_ref[tok, e:e + 1] - base, 1.0, 0.0).astype(BF)
            d = _dot(onehot, ye_refs[e][0, pl.ds(sub, _GATHER_ROWS), :])
            acc = d if acc is None else acc + d
        y_ref[tok, :] = _ln_rows(ALPHA * x_ref[tok, :] + acc, g_ref[...], b_ref[...])


def _combine(starts, x, pos, g, b, ye):
    n = x.shape[0]
    assert ye.shape[1] >= EC_CAPACITY * n // N_EXPERTS + _WIN_ROWS
    tt = _COMB_TILES * LANES
    row = lambda j, st: (j, 0)
    fixed = lambda j, st: (0, 0)

    def ye_spec(e):
        return pl.BlockSpec((pl.Element(1), pl.Element(_WIN_ROWS), pl.Element(D_MODEL)),
                            lambda j, st: (e, (st[e, j * _COMB_TILES] // _WIN_STEP) * _WIN_STEP, 0))

    return pl.pallas_call(
        _combine_kernel,
        grid_spec=pltpu.PrefetchScalarGridSpec(
            num_scalar_prefetch=1,
            grid=(n // tt,),
            in_specs=[pl.BlockSpec((tt, D_MODEL), row),
                      pl.BlockSpec((tt, N_EXPERTS), row),
                      pl.BlockSpec((1, D_MODEL), fixed), pl.BlockSpec((1, D_MODEL), fixed)]
            + [ye_spec(e) for e in range(N_EXPERTS)],
            out_specs=pl.BlockSpec((tt, D_MODEL), row)),
        out_shape=jax.ShapeDtypeStruct((n, D_MODEL), F32),
        compiler_params=_params("arbitrary"),
        name="combine",
    )(starts, x, pos, g, b, *([ye] * N_EXPERTS))


def _moe(x, xg, aff3, wg, wu, wd, g, b):
    n = x.shape[0]
    cap = EC_CAPACITY * n // N_EXPERTS
    pos3, st3 = _select(aff3, cap)
    starts = jnp.concatenate([jnp.transpose(st3[:, :, 0]), jnp.full((N_EXPERTS, 1), cap, I32)], axis=1)
    ye = _moe_ffn(starts, xg, pos3, aff3, wg, wu, wd, cap)
    pos = jnp.transpose(pos3, (0, 2, 1)).reshape(n, N_EXPERTS)
    return _combine(starts, x, pos, g, b, ye)


def _prep_weights(p):
    w = {}
    w['a'] = [_prep_a(p['a_w_qkv'][j], p['a_q_gain'][j], p['a_k_gain'][j], p['a_w_o'][j])
              for j in range(p['a_w_qkv'].shape[0])]
    w['b'] = [_prep_b(p['b_w_qkv'][j], p['b_w_o'][j]) for j in range(p['b_w_qkv'].shape[0])]
    w['c'] = [_prep_c(p['c_lam_re'][j], p['c_lam_im'][j], p['c_log_dt'][j], p['c_b_re'][j], p['c_b_im'][j],
                      p['c_c_re'][j], p['c_c_im'][j]) + (p['c_d'][j][None, :], p['c_w_glu'][j].astype(BF))
              for j in range(p['c_lam_re'].shape[0])]
    w['d'] = []
    for j in range(p['d_w_qkv'].shape[0]):
        layer = N_MIXERS * j + 3
        lambda_init = 0.8 - 0.6 * math.exp(-0.3 * layer)
        w['d'].append(_prep_d(p['d_w_qkv'][j], p['d_norm_gain'][j], p['d_w_o'][j], lambda_init)
                      + (p['d_lam'][j].astype(F32), lambda_init))
    w['diff_bias'] = _diff_bias(p['rel_bias'], _D_T)
    w['x_w_q'] = p['x_w_q'].astype(BF)
    w['x_w_kv'] = p['x_w_kv'].astype(BF)
    w['x_w_o'] = p['x_w_o'].astype(BF)
    w['router_t'] = jnp.transpose(p['moe_w_router'], (0, 2, 1)).astype(F32)
    w['moe_w_gate'] = p['moe_w_gate'].astype(BF)
    w['moe_w_up'] = p['moe_w_up'].astype(BF)
    w['moe_w_down'] = p['moe_w_down'].astype(BF)
    return w


def _trunk(x, mem, p, w):
    b, s, _ = x.shape
    n = b * s
    mem_len = mem.shape[1]
    x = x.reshape(n, D_MODEL)
    mem2 = mem.reshape(b * mem_len, D_MODEL)
    ln_g, ln_b = p['ln_g'], p['ln_b']
    for i in range(DEPTH):
        m, j = i % N_MIXERS, i // N_MIXERS
        g0, b0 = ln_g[i, 0][None, :], ln_b[i, 0][None, :]
        if m == 0:
            wa, gains, wo = w['a'][j]
            qkv = _proj_a(x, wa, _rope_table(s), gains, s)
            o = _flash_a(qkv.reshape(b, s, _A_COLS))
            x = _post(x, o.reshape(n, A_HEADS * LANES), wo, g0, b0)
        elif m == 1:
            wb, wo = w['b'][j]
            qkv = _proj(x, wb).reshape(b, s, wb.shape[1])
            os_, ls_ = [], []
            for g in range(len(B_PATTERNS)):
                t = _tile(s, _B_T)
                o, lse = _band_attention(qkv, _band_bias(p['rel_bias'], g, t), g, t)
                os_.append(o)
                ls_.append(lse)
            x = _post_b(x, os_, ls_, wo, g0, b0)
        elif m == 2:
            w_sum, w_lag, w_state, dec, dskip, wglu = w['c'][j]
            ys = _s5(x, b, s, w_sum, w_lag, w_state, dec)
            x = _post_c(x, ys, dskip, wglu, g0, b0)
        else:
            wd, gain, wo, lam, lambda_init = w['d'][j]
            qkv = _proj(x, wd, ones_from=2 * D_HEADS * LANES).reshape(b, s, 3 * D_HEADS * LANES)
            o = _flash_d(qkv, w['diff_bias'], lam, gain, lambda_init)
            x = _post(x, o.reshape(n, D_HEADS * LANES), wo, g0, b0)
        kv = _proj(mem2, w['x_w_kv'][i], tm=mem_len)
        x, xbf, aff3 = _cross(x, kv, w['x_w_q'][i], w['x_w_o'][i], ln_g[i, 1][None, :], ln_b[i, 1][None, :],
                              w['router_t'][i], s, mem_len)
        x = _moe(x, xbf, aff3, w['moe_w_gate'][i], w['moe_w_up'][i], w['moe_w_down'][i],
                 ln_g[i, 2][None, :], ln_b[i, 2][None, :])
    return x.reshape(b, s, D_MODEL)


def kernel(x_prompt, x_sample, mem_prompt, mem_sample, rel_bias, ln_g, ln_b, a_w_qkv, a_q_gain, a_k_gain, a_w_o, b_w_qkv, b_w_o, c_lam_re, c_lam_im, c_log_dt, c_b_re, c_b_im, c_c_re, c_c_im, c_d, c_w_glu, d_w_qkv, d_lam, d_norm_gain, d_w_o, x_w_q, x_w_kv, x_w_o, moe_w_router, moe_w_gate, moe_w_up, moe_w_down):
    p = dict(rel_bias=rel_bias, ln_g=ln_g, ln_b=ln_b,
             a_w_qkv=a_w_qkv, a_q_gain=a_q_gain, a_k_gain=a_k_gain, a_w_o=a_w_o,
             b_w_qkv=b_w_qkv, b_w_o=b_w_o,
             c_lam_re=c_lam_re, c_lam_im=c_lam_im, c_log_dt=c_log_dt, c_b_re=c_b_re, c_b_im=c_b_im,
             c_c_re=c_c_re, c_c_im=c_c_im, c_d=c_d, c_w_glu=c_w_glu,
             d_w_qkv=d_w_qkv, d_lam=d_lam, d_norm_gain=d_norm_gain, d_w_o=d_w_o,
             x_w_q=x_w_q, x_w_kv=x_w_kv, x_w_o=x_w_o,
             moe_w_router=moe_w_router, moe_w_gate=moe_w_gate, moe_w_up=moe_w_up, moe_w_down=moe_w_down)
    w = _prep_weights(p)
    return (_trunk(x_prompt, mem_prompt, p, w), _trunk(x_sample, mem_sample, p, w))
```

```python
import functools
import math

import numpy as np
import jax
import jax.numpy as jnp
from jax import lax
from jax.experimental import pallas as pl
from jax.experimental.pallas import tpu as pltpu

F32 = jnp.float32
BF = jnp.bfloat16
I32 = jnp.int32

D_MODEL = 1024
DEPTH = 4
GRID_W = 64
N_MIXERS = 4
LN_EPS = 1e-5
RMS_EPS = 1e-6
ALPHA = (2.0 * DEPTH) ** 0.25

A_HEADS = 16
A_KV_HEADS = 4
A_HEAD_DIM = 64
ROPE_BASE = 10000.0

B_PATTERNS = ((128, 1), (512, 4), (2048, 16))
B_HEADS_PER_GROUP = 4
B_HEAD_DIM = 64

C_GROUP = 16
C_N_GROUPS = D_MODEL // C_GROUP
C_STATE = 64
C_CHUNK = 16
_C_GL = 128 // C_GROUP

D_HEADS = 12
D_QK_DIM = 32
D_V_DIM = 64

REL_BUCKETS = 32
REL_MAX_DIST = 128

X_HEADS = 4
X_HEAD_DIM = D_MODEL // X_HEADS

N_EXPERTS = 16
EC_CAPACITY = 2

LANES = 128
BF16_ROWS = 16
VMEM_LIMIT = 56 * 1024 * 1024
NEG = -1e30
LOG2E = math.log2(math.e)


def _params(*sem):
    return pltpu.CompilerParams(dimension_semantics=sem, vmem_limit_bytes=VMEM_LIMIT)


def _tile(n, pref):
    t = min(n, pref)
    assert n % t == 0, (n, pref)
    return t


def _ln_rows(v, g, b):
    mu = jnp.mean(v, -1, keepdims=True)
    c = v - mu
    var = jnp.mean(c * c, -1, keepdims=True)
    return c * lax.rsqrt(var + LN_EPS) * g + b


def _dot_nt(a, b):
    return lax.dot_general(a, b, (((1,), (1,)), ((), ())), preferred_element_type=F32)


def _dot(a, b):
    return jnp.dot(a, b, preferred_element_type=F32)


_ONE_LANE = 64


def _with_ones_lane(y):
    lane = lax.broadcasted_iota(I32, y.shape, 1)
    return jnp.where(lane % LANES == _ONE_LANE, 1.0, y)


def _proj_kernel(x_ref, w_ref, o_ref, *, ones_from):
    y = _dot(x_ref[...].astype(BF), w_ref[...])
    if ones_from is None:
        o_ref[...] = y.astype(o_ref.dtype)
    else:
        o_ref[:, :ones_from] = y[:, :ones_from].astype(o_ref.dtype)
        o_ref[:, ones_from:] = _with_ones_lane(y[:, ones_from:]).astype(o_ref.dtype)


def _proj(x, w, tm=512, ones_from=None):
    n, k = x.shape
    m = w.shape[1]
    tm = _tile(n, tm)
    return pl.pallas_call(
        functools.partial(_proj_kernel, ones_from=ones_from),
        grid=(n // tm,),
        in_specs=[pl.BlockSpec((tm, k), lambda i: (i, 0)),
                  pl.BlockSpec((k, m), lambda i: (0, 0))],
        out_specs=pl.BlockSpec((tm, m), lambda i: (i, 0)),
        out_shape=jax.ShapeDtypeStruct((n, m), BF),
        compiler_params=_params("parallel"),
        name="proj",
    )(x, w)


def _post_kernel(x_ref, o_ref, w_ref, g_ref, b_ref, y_ref):
    h = _dot(o_ref[...], w_ref[...])
    y_ref[...] = _ln_rows(ALPHA * x_ref[...] + h, g_ref[...], b_ref[...])


def _post(x, o, w, g, b, tm=512):
    n = x.shape[0]
    ko = o.shape[1]
    tm = _tile(n, tm)
    return pl.pallas_call(
        _post_kernel,
        grid=(n // tm,),
        in_specs=[pl.BlockSpec((tm, D_MODEL), lambda i: (i, 0)),
                  pl.BlockSpec((tm, ko), lambda i: (i, 0)),
                  pl.BlockSpec((ko, D_MODEL), lambda i: (0, 0)),
                  pl.BlockSpec((1, D_MODEL), lambda i: (0, 0)),
                  pl.BlockSpec((1, D_MODEL), lambda i: (0, 0))],
        out_specs=pl.BlockSpec((tm, D_MODEL), lambda i: (i, 0)),
        out_shape=jax.ShapeDtypeStruct((n, D_MODEL), F32),
        compiler_params=_params("parallel"),
        name="post",
    )(x, o, w, g, b)


def _post_b_kernel(x_ref, o0_ref, o1_ref, o2_ref, l0_ref, l1_ref, l2_ref, w_ref, g_ref, b_ref, y_ref):
    l0, l1, l2 = l0_ref[...], l1_ref[...], l2_ref[...]
    m = jnp.maximum(jnp.maximum(l0, l1), l2)
    e0, e1, e2 = jnp.exp(l0 - m), jnp.exp(l1 - m), jnp.exp(l2 - m)
    inv = 1.0 / (e0 + e1 + e2)
    gw = B_HEADS_PER_GROUP * B_HEAD_DIM
    h = _dot((o0_ref[...].astype(F32) * (e0 * inv)).astype(BF), w_ref[0:gw, :])
    h = h + _dot((o1_ref[...].astype(F32) * (e1 * inv)).astype(BF), w_ref[gw:2 * gw, :])
    h = h + _dot((o2_ref[...].astype(F32) * (e2 * inv)).astype(BF), w_ref[2 * gw:3 * gw, :])
    y_ref[...] = _ln_rows(ALPHA * x_ref[...] + h, g_ref[...], b_ref[...])


def _post_b(x, os_, ls_, w, g, b, tm=512):
    n = x.shape[0]
    tm = _tile(n, tm)
    gw = B_HEADS_PER_GROUP * B_HEAD_DIM
    row = lambda i: (i, 0)
    fixed = lambda i: (0, 0)
    return pl.pallas_call(
        _post_b_kernel,
        grid=(n // tm,),
        in_specs=[pl.BlockSpec((tm, D_MODEL), row)] + [pl.BlockSpec((tm, gw), row)] * 6
        + [pl.BlockSpec((3 * gw, D_MODEL), fixed), pl.BlockSpec((1, D_MODEL), fixed),
           pl.BlockSpec((1, D_MODEL), fixed)],
        out_specs=pl.BlockSpec((tm, D_MODEL), row),
        out_shape=jax.ShapeDtypeStruct((n, D_MODEL), F32),
        compiler_params=_params("parallel"),
        name="post_b",
    )(x, *os_, *ls_, w, g, b)


def _post_c_kernel(x_ref, ys_ref, d_ref, w_ref, g_ref, b_ref, y_ref):
    x = x_ref[...]
    z = jax.nn.gelu(ys_ref[...] + d_ref[...] * x).astype(BF)
    h = _dot(z, w_ref[...])
    hh = h[:, :D_MODEL] * jax.nn.sigmoid(h[:, D_MODEL:])
    y_ref[...] = _ln_rows(ALPHA * x + hh, g_ref[...], b_ref[...])


def _post_c(x, ys, d, w, g, b, tm=512):
    n = x.shape[0]
    tm = _tile(n, tm)
    row = lambda i: (i, 0)
    fixed = lambda i: (0, 0)
    return pl.pallas_call(
        _post_c_kernel,
        grid=(n // tm,),
        in_specs=[pl.BlockSpec((tm, D_MODEL), row), pl.BlockSpec((tm, D_MODEL), row),
                  pl.BlockSpec((1, D_MODEL), fixed), pl.BlockSpec((D_MODEL, 2 * D_MODEL), fixed),
                  pl.BlockSpec((1, D_MODEL), fixed), pl.BlockSpec((1, D_MODEL), fixed)],
        out_specs=pl.BlockSpec((tm, D_MODEL), row),
        out_shape=jax.ShapeDtypeStruct((n, D_MODEL), F32),
        compiler_params=_params("parallel"),
        name="post_c",
    )(x, ys, d, w, g, b)


_A_QK_TILES = A_HEADS + A_KV_HEADS
_A_COLS = (_A_QK_TILES + A_KV_HEADS) * LANES


def _rope_partner():
    d = np.arange(A_HEAD_DIM)
    e = d % (A_HEAD_DIM // 2)
    lo = e < A_HEAD_DIM // 4
    return np.where(lo, d + A_HEAD_DIM // 4, d - A_HEAD_DIM // 4), np.where(lo, -1.0, 1.0).astype(np.float32)


def _prep_a(w_qkv, q_gain, k_gain, w_o):
    partner, sign = _rope_partner()
    nqk = _A_QK_TILES * A_HEAD_DIM
    wqk = w_qkv[:, :nqk].reshape(D_MODEL, _A_QK_TILES, A_HEAD_DIM)
    wsw = wqk[:, :, partner] * sign
    wqk = jnp.concatenate([wqk, wsw], -1).reshape(D_MODEL, _A_QK_TILES * LANES)
    wv = w_qkv[:, nqk:].reshape(D_MODEL, A_KV_HEADS, A_HEAD_DIM)
    wv = jnp.concatenate([wv, jnp.zeros_like(wv)], -1).reshape(D_MODEL, A_KV_HEADS * LANES)
    w = jnp.concatenate([wqk, wv], 1).astype(BF)
    gq = jnp.concatenate([q_gain, q_gain[partner]]) * (A_HEAD_DIM ** -0.5 * 0.5 * LOG2E)
    gk = jnp.concatenate([k_gain, k_gain[partner]])
    gains = jnp.stack([gq, gk], 0)
    wo = w_o.reshape(A_HEADS, A_HEAD_DIM, D_MODEL)
    wo = jnp.concatenate([wo, jnp.zeros_like(wo)], 1).reshape(A_HEADS * LANES, D_MODEL).astype(BF)
    return w, gains, wo


def _rope_table(s):
    pos = jnp.arange(s)
    rows, cols = (pos // GRID_W).astype(F32), (pos % GRID_W).astype(F32)
    half = A_HEAD_DIM // 2
    freqs = ROPE_BASE ** (-jnp.arange(0, half, 2, dtype=F32) / half)
    ang_r = rows[:, None] * freqs
    ang_c = cols[:, None] * freqs
    ang = jnp.concatenate([ang_r, ang_r, ang_c, ang_c], -1)
    return jnp.concatenate([jnp.cos(ang), jnp.sin(ang)], -1)


def _proj_a_kernel(x_ref, w_ref, cs_ref, g_ref, o_ref):
    xb = x_ref[...].astype(BF)
    cs = cs_ref[...]
    gq = cs * g_ref[0:1, :]
    gk = cs * g_ref[1:2, :]
    per = 4
    for g in range(_A_QK_TILES // per):
        y = _dot(xb, w_ref[:, LANES * per * g:LANES * per * (g + 1)])
        for hh in range(per):
            h = per * g + hh
            t = y[:, LANES * hh:LANES * (hh + 1)]
            r = lax.rsqrt(jnp.sum(t * t, -1, keepdims=True) * (1.0 / LANES) + RMS_EPS)
            e = t * r * (gq if h < A_HEADS else gk)
            o_ref[:, LANES * h:LANES * (h + 1)] = (e + pltpu.roll(e, LANES // 2, 1)).astype(BF)
    yv = _dot(xb, w_ref[:, _A_QK_TILES * LANES:])
    o_ref[:, _A_QK_TILES * LANES:] = _with_ones_lane(yv).astype(BF)


def _proj_a(x, w, cs, gains, s, tm=512):
    n = x.shape[0]
    tm = _tile(s, tm)
    per = s // tm
    return pl.pallas_call(
        _proj_a_kernel,
        grid=(n // tm,),
        in_specs=[pl.BlockSpec((tm, D_MODEL), lambda i: (i, 0)),
                  pl.BlockSpec((D_MODEL, _A_COLS), lambda i: (0, 0)),
                  pl.BlockSpec((tm, LANES), lambda i: (i % per, 0)),
                  pl.BlockSpec((2, LANES), lambda i: (0, 0))],
        out_specs=pl.BlockSpec((tm, _A_COLS), lambda i: (i, 0)),
        out_shape=jax.ShapeDtypeStruct((n, _A_COLS), BF),
        compiler_params=_params("parallel"),
        name="proj_a",
    )(x, w, cs, gains)


def _softmax_step(s, v, m_ref, acc_ref):
    tk = s.shape[1]
    m_old = m_ref[...]
    m_new = jnp.maximum(m_old, jnp.max(s, -1, keepdims=True))
    p = jnp.concatenate([jnp.exp2(s[:, LANES * c:LANES * (c + 1)] - m_new).astype(BF) for c in range(tk // LANES)],
                        axis=1)
    acc_ref[...] = jnp.exp2(m_old - m_new) * acc_ref[...] + _dot(p, v)
    m_ref[...] = m_new


def _softmax_init(m_ref, acc_ref):
    m_ref[...] = jnp.full(m_ref.shape, -jnp.inf, F32)
    acc_ref[...] = jnp.zeros_like(acc_ref)


def _flash_a_kernel(q_ref, k_ref, v_ref, o_ref, m_ref, acc_ref, *, tk, nk):
    tq = q_ref.shape[0]
    rep = A_HEADS // A_KV_HEADS
    q = jnp.concatenate([q_ref[:, LANES * r:LANES * (r + 1)] for r in range(rep)], axis=0)
    _softmax_init(m_ref, acc_ref)

    def body(j, carry):
        off = pl.multiple_of(j * tk, tk)
        _softmax_step(_dot_nt(q, k_ref[pl.ds(off, tk), :]), v_ref[pl.ds(off, tk), :], m_ref, acc_ref)
        return carry

    lax.fori_loop(0, nk, body, 0, unroll=4)
    acc = acc_ref[...]
    o = acc / acc[:, _ONE_LANE:_ONE_LANE + 1]
    for r in range(rep):
        o_ref[:, LANES * r:LANES * (r + 1)] = o[r * tq:(r + 1) * tq].astype(BF)


def _flash_a(qkv, tq=256, tk=2048):
    b, s, _ = qkv.shape
    tq, tk = _tile(s, tq), _tile(s, tk)
    rep = A_HEADS // A_KV_HEADS
    gw = rep * LANES
    return pl.pallas_call(
        functools.partial(_flash_a_kernel, tk=tk, nk=s // tk),
        grid=(b, A_KV_HEADS, s // tq),
        in_specs=[pl.BlockSpec((None, tq, gw), lambda bi, g, i: (bi, i, g)),
                  pl.BlockSpec((None, s, LANES), lambda bi, g, i: (bi, 0, A_HEADS + g)),
                  pl.BlockSpec((None, s, LANES), lambda bi, g, i: (bi, 0, _A_QK_TILES + g))],
        out_specs=pl.BlockSpec((None, tq, gw), lambda bi, g, i: (bi, i, g)),
        out_shape=jax.ShapeDtypeStruct((b, s, A_HEADS * LANES), BF),
        scratch_shapes=[pltpu.VMEM((rep * tq, LANES), F32)] * 2,
        compiler_params=_params("parallel", "parallel", "arbitrary"),
        name="flash_a",
    )(qkv, qkv, qkv)


def _bucket(rel):
    half = REL_BUCKETS // 2
    max_exact = half // 2
    n = jnp.abs(rel)
    large = max_exact + (jnp.log(jnp.maximum(n, 1).astype(F32) / max_exact)
                         / math.log(REL_MAX_DIST / max_exact) * (half - max_exact)).astype(I32)
    large = jnp.minimum(large, half - 1)
    return jnp.where(rel > 0, half, 0) + jnp.where(n < max_exact, n, large)


_B_GW = B_HEADS_PER_GROUP * B_HEAD_DIM
_B_T = 256


def _pad_heads(w, heads, dim):
    w = w.reshape(w.shape[0], heads, dim)
    return jnp.concatenate([w, jnp.zeros((w.shape[0], heads, LANES - dim), w.dtype)], -1).reshape(
        w.shape[0], heads * LANES)


def _prep_b(w_qkv, w_o):
    c = len(B_PATTERNS) * B_HEADS_PER_GROUP * B_HEAD_DIM
    w = jnp.concatenate([w_qkv[:, :c] * (B_HEAD_DIM ** -0.5), w_qkv[:, c:]], 1).astype(BF)
    return w, w_o.astype(BF)


def _toeplitz(vec, t):
    flat = jnp.tile(vec, (1,) * (vec.ndim - 1) + (t,))[..., :t * (2 * t - 1)]
    return flat.reshape(vec.shape[:-1] + (t, 2 * t - 1))[..., :t]


def _band_tiles(g, t):
    window, dil = B_PATTERNS[g]
    return pl.cdiv((window // (2 * dil)) * dil, t)


def _band_bias(rel_bias, g, t):
    window, dil = B_PATTERNS[g]
    reach = (window // (2 * dil)) * dil
    n = _band_tiles(g, t)
    rel0 = (jnp.arange(2 * t) + t) % (2 * t) - t
    rel = jnp.arange(-n, n + 1)[:, None] * t + rel0[None, :]
    bias = rel_bias[_bucket(rel)][:, :, g * B_HEADS_PER_GROUP:(g + 1) * B_HEADS_PER_GROUP]
    bias = jnp.where(((rel % dil == 0) & (jnp.abs(rel) <= reach))[:, :, None], bias, NEG)
    return _toeplitz(jnp.transpose(bias, (2, 0, 1)).astype(F32), t)


def _band_kernel(q_ref, k_ref, v_ref, bias_ref, o_ref, lse_ref, *, n, nk):
    t = q_ref.shape[0]
    i = pl.program_id(2)
    qf = q_ref[...].astype(F32)
    lane = lax.broadcasted_iota(I32, (t, LANES), 1)
    offs = [pl.multiple_of(jnp.clip(i + o, 0, nk - 1) * t, t) for o in range(-n, n + 1)]
    inside = [(i + o >= 0) & (i + o < nk) for o in range(-n, n + 1)]
    outs, lses = [], []
    for hh in range(2):
        q = jnp.where((lane >= B_HEAD_DIM * hh) & (lane < B_HEAD_DIM * (hh + 1)), qf, 0.0).astype(BF)
        logits = [jnp.where(ok, _dot_nt(q, k_ref[pl.ds(off, t), :]) + bias_ref[hh, c], NEG)
                  for c, (off, ok) in enumerate(zip(offs, inside))]
        m = logits[0].max(-1, keepdims=True)
        for s in logits[1:]:
            m = jnp.maximum(m, s.max(-1, keepdims=True))
        l = jnp.zeros((t, 1), F32)
        acc = jnp.zeros((t, LANES), F32)
        for s, off in zip(logits, offs):
            p = jnp.exp(s - m)
            l = l + jnp.sum(p, -1, keepdims=True)
            acc = acc + _dot(p.astype(BF), v_ref[pl.ds(off, t), :])
        outs.append(acc / l)
        lses.append(jnp.broadcast_to(m + jnp.log(l), (t, LANES)))
    first = lane < B_HEAD_DIM
    o_ref[...] = jnp.where(first, outs[0], outs[1]).astype(BF)
    lse_ref[...] = jnp.where(first, lses[0], lses[1])


def _band_attention(qkv, bias, g, t):
    b, s, c = qkv.shape
    third = c // 3 // LANES
    pairs = B_HEADS_PER_GROUP // 2
    n = _band_tiles(g, t)
    spec = lambda base: pl.BlockSpec((None, s, LANES), lambda bi, hp, i: (bi, 0, base + g * pairs + hp))
    out_spec = pl.BlockSpec((None, t, LANES), lambda bi, hp, i: (bi, i, hp))
    o, lse = pl.pallas_call(
        functools.partial(_band_kernel, n=n, nk=s // t),
        grid=(b, pairs, s // t),
        in_specs=[pl.BlockSpec((None, t, LANES), lambda bi, hp, i: (bi, i, g * pairs + hp)),
                  spec(third), spec(2 * third),
                  pl.BlockSpec((2, 2 * n + 1, t, t), lambda bi, hp, i: (hp, 0, 0, 0))],
        out_specs=[out_spec, out_spec],
        out_shape=[jax.ShapeDtypeStruct((b, s, _B_GW), BF), jax.ShapeDtypeStruct((b, s, _B_GW), F32)],
        compiler_params=_params("parallel", "parallel", "arbitrary"),
        name="band_%d" % g,
    )(qkv, qkv, qkv, bias)
    return o.reshape(b * s, _B_GW), lse.reshape(b * s, _B_GW)


def _prep_c(lam_re, lam_im, log_dt, b_re, b_im, c_re, c_im):
    hp = lax.Precision.HIGHEST
    L, P, C = C_CHUNK, C_STATE, C_GROUP
    lr, li = lam_re.astype(F32), lam_im.astype(F32)
    dt = jnp.exp(log_dt.astype(F32))[..., None]
    mag = jnp.exp(lr * dt)
    ar, ai = mag * jnp.cos(li * dt), mag * jnp.sin(li * dt)
    den = lr * lr + li * li
    zr = ((ar - 1.0) * lr + ai * li) / den
    zi = (ai * lr - (ar - 1.0) * li) / den
    br, bi = b_re.astype(F32), b_im.astype(F32)
    bbr = zr[..., None] * br - zi[..., None] * bi
    bbi = zr[..., None] * bi + zi[..., None] * br
    cr, ci = c_re.astype(F32), c_im.astype(F32)
    prs, pis = [jnp.ones_like(ar)], [jnp.zeros_like(ai)]
    for _ in range(L):
        pr_, pi_ = prs[-1], pis[-1]
        prs.append(ar * pr_ - ai * pi_)
        pis.append(ar * pi_ + ai * pr_)
    pr, pi = jnp.stack(prs, 0), jnp.stack(pis, 0)

    def lag(pr_k, pi_k):
        tr = pr_k[..., None] * bbr - pi_k[..., None] * bbi
        ti = pr_k[..., None] * bbi + pi_k[..., None] * bbr
        return (jnp.einsum('dgop,kdgpi->kdgoi', cr, tr, precision=hp)
                - jnp.einsum('dgop,kdgpi->kdgoi', ci, ti, precision=hp))

    kern = lag(pr[:L], pi[:L])
    lags = jnp.concatenate([kern[:0:-1, 1], (kern[0, 0] + kern[0, 1])[None], kern[1:, 0]], 0)

    def summ(d, powers):
        pr_k, pi_k = pr[powers, d], pi[powers, d]
        sr = pr_k[..., None] * bbr[d] - pi_k[..., None] * bbi[d]
        si = pr_k[..., None] * bbi[d] + pi_k[..., None] * bbr[d]
        s = jnp.concatenate([sr, si], 2)
        return jnp.transpose(s, (1, 0, 3, 2)).reshape(C_N_GROUPS, L * C, 2 * P)

    w_sum = jnp.concatenate([summ(0, np.arange(L - 1, -1, -1)), summ(1, np.arange(L))], -1)

    def outw(d, powers):
        pr_k, pi_k = pr[powers, d], pi[powers, d]
        wr = cr[d][None] * pr_k[:, :, None, :] - ci[d][None] * pi_k[:, :, None, :]
        wi = -(cr[d][None] * pi_k[:, :, None, :] + ci[d][None] * pr_k[:, :, None, :])
        w = jnp.concatenate([wr, wi], -1)
        return jnp.transpose(w, (1, 3, 0, 2)).reshape(C_N_GROUPS, 2 * P, L * C)

    w_state = jnp.concatenate([outw(0, np.arange(1, L + 1)), outw(1, np.arange(L, 0, -1))], 1)

    nq = C_N_GROUPS // _C_GL
    npair = _C_GL // 2
    blk = L * _C_GL * C
    lane = jnp.arange(LANES)
    gsel = (lane[None, :] // C == jnp.arange(_C_GL)[:, None])
    src = jnp.swapaxes(lags, -1, -2).astype(BF).reshape(2 * L - 1, nq, _C_GL, C, C)
    src = jnp.where(gsel[None, None, :, None, :], jnp.tile(src, (1, 1, 1, 1, _C_GL)), 0)
    w_lag = jnp.transpose(src.reshape(2 * L - 1, nq, LANES, LANES), (1, 0, 2, 3))
    psel = (lane[None, :] // P == jnp.arange(2)[:, None])
    ksel = jnp.eye(npair, dtype=bool)
    src = jnp.tile(w_sum.astype(BF).reshape(nq, npair, 2, L, C, 1, 4, P), (1, 1, 1, 1, 1, npair, 1, 2))
    src = jnp.where(psel[None, None, :, None, None, None, None, :]
                    & ksel[None, :, None, None, None, :, None, None], src, 0)
    w_sum = jnp.transpose(src, (0, 3, 1, 2, 4, 5, 6, 7)).reshape(nq, blk, _C_GL * 4 * P)
    osel = (lane[None, None, :] // C
            == (2 * jnp.arange(npair)[:, None, None] + jnp.arange(2)[None, :, None]))
    src = jnp.tile(w_state.astype(BF).reshape(nq, npair, 2, 4, P, L, C), (1, 1, 1, 1, 1, 1, _C_GL))
    src = jnp.where(osel[None, :, :, None, None, None, :], src, 0)
    w_state = jnp.transpose(src, (0, 1, 3, 2, 4, 5, 6)).reshape(nq, _C_GL * 4 * P, blk)
    dec = jnp.stack([pr[L, 0], pi[L, 0], pr[L, 1], pi[L, 1]], 0).reshape(4, nq * npair, 2 * P)
    dec = jnp.broadcast_to(jnp.transpose(dec, (1, 0, 2))[:, :, None, :], (nq * npair, 4, 8, 2 * P))
    return w_sum.astype(BF), w_lag, w_state.astype(BF), dec


def _chunk_rows(x_ref, nb, rc):
    L = C_CHUNK
    rows = [jnp.concatenate([x_ref[b, pl.ds(j, rc, stride=L), :].astype(BF) for j in range(L)], axis=1)
            for b in range(nb)]
    return jnp.concatenate(rows, axis=0)


def _s5_sum_kernel(x_ref, w_ref, s_ref, *, nb, rc):
    s = _dot(_chunk_rows(x_ref, nb, rc), w_ref[...])
    for b in range(nb):
        for t in range(s_ref.shape[0]):
            s_ref[t, pl.ds(b, rc, stride=nb), :] = s[b * rc:(b + 1) * rc, LANES * t:LANES * (t + 1)]


def _s5_scan_kernel(s_ref, dec_ref, e_ref, *, nc, nb):
    units = s_ref.shape[0] // 4
    dec = [[dec_ref[u, k, 0:nb, :] for k in range(4)] for u in range(units)]

    def body(t, carry):
        rf = pl.ds(pl.multiple_of(t * nb, nb), nb)
        rb = pl.ds(pl.multiple_of((nc - 1 - t) * nb, nb), nb)
        new = []
        for u in range(units):
            fr, fi, br, bi = carry[4 * u:4 * u + 4]
            arf, aif, arb, aib = dec[u]
            e_ref[4 * u, rf, :] = fr
            e_ref[4 * u + 1, rf, :] = fi
            e_ref[4 * u + 2, rb, :] = br
            e_ref[4 * u + 3, rb, :] = bi
            new += [fr * arf - fi * aif + s_ref[4 * u, rf, :], fi * arf + fr * aif + s_ref[4 * u + 1, rf, :],
                    br * arb - bi * aib + s_ref[4 * u + 2, rb, :], bi * arb + br * aib + s_ref[4 * u + 3, rb, :]]
        return tuple(new)

    z = jnp.zeros((nb, LANES), F32)
    lax.fori_loop(0, nc, body, (z,) * (4 * units))


def _s5_out_kernel(x_ref, e_ref, lag_ref, wc_ref, y_ref, wi_ref, *, nb, rc):
    L = C_CHUNK
    half = pl.program_id(2)

    @pl.when((pl.program_id(1) == 0) & (half == 0))
    def _():
        for j in range(L):
            for i in range(L):
                wi_ref[i // (L // 2), LANES * j:LANES * (j + 1), LANES * (i % (L // 2)):LANES * (i % (L // 2) + 1)] = (
                    lag_ref[i - j + L - 1])

    xc = _chunk_rows(x_ref, nb, rc)
    ec = jnp.concatenate(
        [jnp.concatenate([e_ref[t, pl.ds(b, rc, stride=nb), :].astype(BF) for t in range(e_ref.shape[0])], axis=1)
         for b in range(nb)], axis=0)
    y = _dot(xc, wi_ref[half]) + _dot(ec, wc_ref[...])
    for b in range(nb):
        for ii in range(L // 2):
            y_ref[b, pl.ds(half * (L // 2) + ii, rc, stride=L), :] = y[b * rc:(b + 1) * rc, LANES * ii:LANES * (ii + 1)]


def _s5(x, b, s, w_sum, w_lag, w_state, dec):
    L = C_CHUNK
    nc = s // L
    nq = D_MODEL // LANES
    nsl = w_sum.shape[2] // LANES
    blk = L * LANES
    rc = _tile(nc, max(8, 512 // b))
    x3 = x.reshape(b, s, D_MODEL)
    x_spec = pl.BlockSpec((b, rc * L, LANES), lambda q, c, *_: (0, c, q))
    sums = pl.pallas_call(
        functools.partial(_s5_sum_kernel, nb=b, rc=rc),
        grid=(nq, nc // rc),
        in_specs=[x_spec, pl.BlockSpec((None, blk, nsl * LANES), lambda q, c: (q, 0, 0))],
        out_specs=pl.BlockSpec((nsl, rc * b, LANES), lambda q, c: (q, c, 0)),
        out_shape=jax.ShapeDtypeStruct((nq * nsl, nc * b, LANES), F32),
        compiler_params=_params("parallel", "arbitrary"),
        name="s5_sum",
    )(x3, w_sum)
    upb = 2
    ent = pl.pallas_call(
        functools.partial(_s5_scan_kernel, nc=nc, nb=b),
        grid=(nq * nsl // (4 * upb),),
        in_specs=[pl.BlockSpec((4 * upb, nc * b, LANES), lambda i: (i, 0, 0)),
                  pl.BlockSpec((upb, 4, 8, LANES), lambda i: (i, 0, 0, 0))],
        out_specs=pl.BlockSpec((4 * upb, nc * b, LANES), lambda i: (i, 0, 0)),
        out_shape=jax.ShapeDtypeStruct((nq * nsl, nc * b, LANES), F32),
        compiler_params=_params("parallel"),
        name="s5_scan",
    )(sums, dec)
    y = pl.pallas_call(
        functools.partial(_s5_out_kernel, nb=b, rc=rc),
        grid=(nq, nc // rc, 2),
        in_specs=[x_spec,
                  pl.BlockSpec((nsl, rc * b, LANES), lambda q, c, h: (q, c, 0)),
                  pl.BlockSpec((None, 2 * L - 1, LANES, LANES), lambda q, c, h: (q, 0, 0, 0)),
                  pl.BlockSpec((None, nsl * LANES, blk // 2), lambda q, c, h: (q, 0, h))],
        out_specs=pl.BlockSpec((b, rc * L, LANES), lambda q, c, h: (0, c, q)),
        out_shape=jax.ShapeDtypeStruct((b, s, D_MODEL), F32),
        scratch_shapes=[pltpu.VMEM((2, blk, blk // 2), BF)],
        compiler_params=_params("arbitrary", "arbitrary", "arbitrary"),
        name="s5_out",
    )(x3, ent, w_lag, w_state)
    return y.reshape(b * s, D_MODEL)


_D_T = 512


def _prep_d(w_qkv, norm_gain, w_o, lambda_init):
    qk_w = D_HEADS * 2 * D_QK_DIM
    wq = _pad_heads(w_qkv[:, :qk_w] * (D_QK_DIM ** -0.5 * LOG2E), D_HEADS, 2 * D_QK_DIM)
    wk = _pad_heads(w_qkv[:, qk_w:2 * qk_w], D_HEADS, 2 * D_QK_DIM)
    wv = _pad_heads(w_qkv[:, 2 * qk_w:], D_HEADS, D_V_DIM)
    w = jnp.concatenate([wq, wk, wv], 1).astype(BF)
    gain = jnp.concatenate([norm_gain * (1.0 - lambda_init), jnp.zeros((LANES - D_V_DIM,), F32)])[None, :]
    wo = w_o.reshape(D_HEADS, D_V_DIM, D_MODEL)
    wo = jnp.concatenate([wo, jnp.zeros_like(wo)], 1).reshape(D_HEADS * LANES, D_MODEL).astype(BF)
    return w, gain, wo


def _diff_bias(rel_bias, t):
    rel0 = (jnp.arange(2 * t) + t) % (2 * t) - t
    rel = jnp.arange(-2, 3)[:, None] * t + rel0[None, :]
    return jnp.transpose(rel_bias[_bucket(rel)] * LOG2E, (2, 0, 1)).astype(F32)


def _flash_d_kernel(q_ref, k_ref, v_ref, vec_ref, lam_ref, gain_ref, o_ref, m_ref, acc_ref, bias_ref, *, kt, nk,
                    lambda_init):
    t = q_ref.shape[0]
    i = pl.program_id(2)

    @pl.when(i == 0)
    def _():
        for d in range(bias_ref.shape[0]):
            full = jnp.broadcast_to(vec_ref[d:d + 1, :], (t, 2 * t))
            bias_ref[d] = pltpu.roll(full, 0, 1, stride=1, stride_axis=0)[:, :t]

    qv = q_ref[...].astype(F32)
    lane = lax.broadcasted_iota(I32, (t, LANES), 1)
    q0 = jnp.where(lane < D_QK_DIM, qv, 0.0).astype(BF)
    q1 = jnp.where((lane >= D_QK_DIM) & (lane < 2 * D_QK_DIM), qv, 0.0).astype(BF)
    q = jnp.concatenate([q0, q1], axis=0)
    _softmax_init(m_ref, acc_ref)

    def body(j, carry):
        off = pl.multiple_of(j * (kt * t), kt * t)
        bias = jnp.concatenate([bias_ref[jnp.clip(j * kt + c - i, -2, 2) + 2] for c in range(kt)], axis=1)
        s = _dot_nt(q, k_ref[pl.ds(off, kt * t), :])
        s = (s.reshape(2, t, kt * t) + bias[None]).reshape(2 * t, kt * t)
        _softmax_step(s, v_ref[pl.ds(off, kt * t), :], m_ref, acc_ref)
        return carry

    lax.fori_loop(0, nk, body, 0, unroll=4)
    lf = lam_ref[...]
    lam = (jnp.exp(jnp.sum(lf[0:1] * lf[1:2], keepdims=True))
           - jnp.exp(jnp.sum(lf[2:3] * lf[3:4], keepdims=True)) + lambda_init)
    acc = acc_ref[...]
    on = acc / acc[:, _ONE_LANE:_ONE_LANE + 1]
    o = jnp.where(lane < D_V_DIM, on[:t] - lam * on[t:], 0.0)
    ms = jnp.sum(o * o, -1, keepdims=True) * (1.0 / D_V_DIM)
    o_ref[...] = (o * lax.rsqrt(ms + RMS_EPS) * gain_ref[...]).astype(BF)


def _flash_d(qkv, bias, lam, gain, lambda_init):
    b, s, _ = qkv.shape
    t = bias.shape[-1] // 2
    kt = 2 if s % (2 * t) == 0 else 1
    return pl.pallas_call(
        functools.partial(_flash_d_kernel, kt=kt, nk=s // (kt * t), lambda_init=lambda_init),
        grid=(b, D_HEADS, s // t),
        in_specs=[pl.BlockSpec((None, t, LANES), lambda bi, h, i: (bi, i, h)),
                  pl.BlockSpec((None, s, LANES), lambda bi, h, i: (bi, 0, D_HEADS + h)),
                  pl.BlockSpec((None, s, LANES), lambda bi, h, i: (bi, 0, 2 * D_HEADS + h)),
                  pl.BlockSpec((None, 5, 2 * t), lambda bi, h, i: (h, 0, 0)),
                  pl.BlockSpec((4, D_QK_DIM), lambda bi, h, i: (0, 0)),
                  pl.BlockSpec((1, LANES), lambda bi, h, i: (0, 0))],
        out_specs=pl.BlockSpec((None, t, LANES), lambda bi, h, i: (bi, i, h)),
        out_shape=jax.ShapeDtypeStruct((b, s, D_HEADS * LANES), BF),
        scratch_shapes=[pltpu.VMEM((2 * t, LANES), F32)] * 2 + [pltpu.VMEM((5, t, t), F32)],
        compiler_params=_params("parallel", "parallel", "arbitrary"),
        name="flash_d",
    )(qkv, qkv, qkv, bias, lam, gain)


def _cross_kernel(x_ref, kv_ref, wq_ref, wo_ref, g_ref, b_ref, wr_ref, y_ref, ybf_ref, aff_ref, *, parts):
    tp = x_ref.shape[0] // parts
    wr = wr_ref[...]
    wh = wr.astype(BF)
    wl = (wr - wh.astype(F32)).astype(BF)
    for part in range(parts):
        rows = slice(tp * part, tp * (part + 1))
        x = x_ref[rows, :]
        q = (_dot(x.astype(BF), wq_ref[...]) * (X_HEAD_DIM ** -0.5)).astype(BF)
        outs = []
        for h in range(X_HEADS):
            sl = slice(X_HEAD_DIM * h, X_HEAD_DIM * (h + 1))
            s = _dot_nt(q[:, sl], kv_ref[:, sl])
            p = jnp.exp(s - jnp.max(s, -1, keepdims=True))
            l = jnp.sum(p, -1, keepdims=True)
            vh = kv_ref[:, D_MODEL + X_HEAD_DIM * h:D_MODEL + X_HEAD_DIM * (h + 1)]
            outs.append((_dot(p.astype(BF), vh) / l).astype(BF))
        o = jnp.concatenate(outs, axis=1)
        y = _ln_rows(ALPHA * x + _dot(o, wo_ref[...]), g_ref[...], b_ref[...])
        y_ref[rows, :] = y
        yh = y.astype(BF)
        ybf_ref[rows, :] = yh
        yl = (y - yh.astype(F32)).astype(BF)
        lg = _dot_nt(wh, yh) + _dot_nt(wh, yl) + _dot_nt(wl, yh)
        e = jnp.exp(lg - jnp.max(lg, 0, keepdims=True))
        aff = e / jnp.sum(e, 0, keepdims=True)
        for c in range(tp // LANES):
            aff_ref[part * (tp // LANES) + c] = aff[:, LANES * c:LANES * (c + 1)]


def _cross(x, kv, wq, wo, g, b, wr_t, s, mem_len, tm=1024, parts=2):
    n = x.shape[0]
    tm = _tile(s, tm)
    per = s // tm
    fixed = lambda i: (0, 0)
    return pl.pallas_call(
        functools.partial(_cross_kernel, parts=parts),
        grid=(n // tm,),
        in_specs=[pl.BlockSpec((tm, D_MODEL), lambda i: (i, 0)),
                  pl.BlockSpec((mem_len, 2 * D_MODEL), lambda i: (i // per, 0)),
                  pl.BlockSpec((D_MODEL, D_MODEL), fixed), pl.BlockSpec((D_MODEL, D_MODEL), fixed),
                  pl.BlockSpec((1, D_MODEL), fixed), pl.BlockSpec((1, D_MODEL), fixed),
                  pl.BlockSpec((N_EXPERTS, D_MODEL), fixed)],
        out_specs=[pl.BlockSpec((tm, D_MODEL), lambda i: (i, 0)),
                   pl.BlockSpec((tm, D_MODEL), lambda i: (i, 0)),
                   pl.BlockSpec((tm // LANES, N_EXPERTS, LANES), lambda i: (i, 0, 0))],
        out_shape=[jax.ShapeDtypeStruct((n, D_MODEL), F32),
                   jax.ShapeDtypeStruct((n, D_MODEL), BF),
                   jax.ShapeDtypeStruct((n // LANES, N_EXPERTS, LANES), F32)],
        compiler_params=_params("parallel"),
        name="cross",
    )(x, kv, wq, wo, g, b, wr_t)


def _select_kernel(a_ref, pos_ref, st_ref, *, k, nbits):
    nt = a_ref.shape[0]
    shape = (nt, N_EXPERTS, LANES)
    kf = float(k)

    def keys():
        return lax.bitcast_convert_type(a_ref[...], I32)

    def count(mask):
        c = jnp.sum(jnp.where(mask, 1.0, 0.0), axis=0, keepdims=True)
        return jnp.sum(c, axis=2, keepdims=True)

    def value_step(it, thr):
        cand = thr | jnp.left_shift(jnp.int32(1), 30 - it)
        return jnp.where(count(keys() >= cand) >= kf, cand, thr)

    thr = lax.fori_loop(0, 31, value_step, jnp.zeros((1, N_EXPERTS, 1), I32))
    need = kf - count(keys() > thr)
    idx = lax.broadcasted_iota(I32, shape, 0) * LANES + lax.broadcasted_iota(I32, shape, 2)

    def index_step(it, ithr):
        cand = ithr | jnp.left_shift(jnp.int32(1), nbits - 1 - it)
        return jnp.where(count((keys() == thr) & (idx < cand)) < need, cand, ithr)

    ithr = lax.fori_loop(0, nbits, index_step, jnp.zeros((1, N_EXPERTS, 1), I32))
    thr2, ithr2 = thr[0], ithr[0]
    upper = jnp.where(lax.broadcasted_iota(I32, (LANES, LANES), 0) <= lax.broadcasted_iota(I32, (LANES, LANES), 1),
                      1.0, 0.0).astype(BF)
    lane = lax.broadcasted_iota(I32, (N_EXPERTS, LANES), 1)

    def tile_step(j, carry):
        kj = lax.bitcast_convert_type(a_ref[j], I32)
        sel = (kj > thr2) | ((kj == thr2) & (j * LANES + lane <= ithr2))
        m = jnp.where(sel, 1.0, 0.0)
        inc = _dot(m.astype(BF), upper)
        pos_ref[j] = jnp.where(sel, inc - m + carry, -1.0).astype(I32)
        st_ref[j] = jnp.broadcast_to(carry, (N_EXPERTS, LANES)).astype(I32)
        return carry + inc[:, LANES - 1:LANES]

    lax.fori_loop(0, nt, tile_step, jnp.zeros((N_EXPERTS, 1), F32))


def _select(aff3, k):
    nt = aff3.shape[0]
    nbits = max(1, int(math.ceil(math.log2(nt * LANES))))
    shp = jax.ShapeDtypeStruct(aff3.shape, I32)
    return pl.pallas_call(
        functools.partial(_select_kernel, k=k, nbits=nbits),
        out_shape=[shp, shp],
        compiler_params=pltpu.CompilerParams(vmem_limit_bytes=VMEM_LIMIT),
        name="select",
    )(aff3)


_GATHER_ROWS = LANES + BF16_ROWS
_GATHER_ROWS_SMALL = 3 * BF16_ROWS

_MOE_GROUP = 2


def _moe_ffn_kernel(st_ref, x_ref, pos_ref, aff_ref, wg_ref, wu_ref, wd_ref, ye_ref, buf_ref, gate_ref, *, nb, ncf,
                    nch, sub, ck):
    ep = buf_ref.shape[0]
    grp = pl.program_id(0)
    t = pl.program_id(1)

    @pl.when(t == 0)
    def _():
        buf_ref[...] = jnp.zeros_like(buf_ref)
        gate_ref[...] = jnp.zeros_like(gate_ref)

    @pl.when(t < nb)
    def _():
        def place(s, k, e, base, nrows):
            rows = pl.ds(base, nrows)
            hit = lax.broadcasted_iota(I32, (nrows, LANES), 0) == pos_ref[s, pl.ds(e, 1), :] - base
            xs = x_ref[LANES * s:LANES * (s + 1), :]
            buf_ref[k, rows, :] = buf_ref[k, rows, :] + _dot(jnp.where(hit, 1.0, 0.0).astype(BF), xs).astype(BF)
            gate_ref[k, rows, :] = gate_ref[k, rows, :] + jnp.sum(
                jnp.where(hit, aff_ref[s, pl.ds(e, 1), :], 0.0), -1, keepdims=True)

        def base_of(s, k):
            st = st_ref[grp * ep + k, t * sub + s]
            return pl.multiple_of((st // BF16_ROWS) * BF16_ROWS, BF16_ROWS)

        pairs = [(s, k) for s in range(sub) for k in range(ep)]
        few = functools.reduce(
            jnp.logical_and,
            [st_ref[grp * ep + k, t * sub + s + 1] - base_of(s, k) <= _GATHER_ROWS_SMALL for s, k in pairs])

        @pl.when(few)
        def _():
            for s, k in pairs:
                place(s, k, grp * ep + k, base_of(s, k), _GATHER_ROWS_SMALL)

        @pl.when(jnp.logical_not(few))
        def _():
            for s, k in pairs:
                place(s, k, grp * ep + k, base_of(s, k), _GATHER_ROWS)

    @pl.when(t >= nb)
    def _():
        k = (t - nb) // nch
        c = (t - nb) - k * nch

        @pl.when(c < ncf)
        def _():
            rows = pl.ds(pl.multiple_of(c * ck, ck), ck)
            xe = buf_ref[k, rows, :]
            h = (jax.nn.silu(_dot(xe, wg_ref[...])) * _dot(xe, wu_ref[...])).astype(BF)
            ye_ref[...] = (_dot(h, wd_ref[...]) * gate_ref[k, rows, :]).astype(BF)

        @pl.when(c >= ncf)
        def _():
            ye_ref[...] = jnp.zeros_like(ye_ref)


def _moe_ffn(starts, xbf, pos3, aff3, wg, wu, wd, cap, tb=1024):
    n = xbf.shape[0]
    tb = _tile(n, tb)
    nb = n // tb
    sub = tb // LANES
    ck = _tile(cap, 1024)
    ncf = cap // ck
    nch = ncf + pl.cdiv(_WIN_ROWS, ck)
    dff = wg.shape[-1]
    ep = _MOE_GROUP
    blk = lambda g, t, st: (jnp.minimum(t, nb - 1), 0)
    blk3 = lambda g, t, st: (jnp.minimum(t, nb - 1), 0, 0)
    expert = lambda g, t: g * ep + jnp.clip((t - nb) // nch, 0, ep - 1)
    wmap = lambda g, t, st: (expert(g, t), 0, 0)
    return pl.pallas_call(
        functools.partial(_moe_ffn_kernel, nb=nb, ncf=ncf, nch=nch, sub=sub, ck=ck),
        grid_spec=pltpu.PrefetchScalarGridSpec(
            num_scalar_prefetch=1,
            grid=(N_EXPERTS // ep, nb + ep * nch),
            in_specs=[pl.BlockSpec((tb, D_MODEL), blk),
                      pl.BlockSpec((sub, N_EXPERTS, LANES), blk3),
                      pl.BlockSpec((sub, N_EXPERTS, LANES), blk3),
                      pl.BlockSpec((None, D_MODEL, dff), wmap),
                      pl.BlockSpec((None, D_MODEL, dff), wmap),
                      pl.BlockSpec((None, dff, D_MODEL), wmap)],
            out_specs=pl.BlockSpec((None, ck, D_MODEL),
                                   lambda g, t, st: (expert(g, t), jnp.maximum(t - nb, 0) % nch, 0)),
            scratch_shapes=[pltpu.VMEM((ep, cap + _GATHER_ROWS, D_MODEL), BF),
                            pltpu.VMEM((ep, cap + _GATHER_ROWS, 1), F32)]),
        out_shape=jax.ShapeDtypeStruct((N_EXPERTS, nch * ck, D_MODEL), BF),
        compiler_params=_params("arbitrary", "arbitrary"),
        name="moe_ffn",
    )(starts, xbf, pos3, aff3, wg, wu, wd)


_WIN_STEP = 128
_COMB_TILES = 2
_WIN_ROWS = _WIN_STEP + (_COMB_TILES - 1) * LANES + _GATHER_ROWS


_COMB_PACK = 256 // _GATHER_ROWS_SMALL


def _combine_kernel(st_ref, x_ref, pos_ref, g_ref, b_ref, *rest):
    ye_refs, y_ref = rest[:N_EXPERTS], rest[N_EXPERTS]
    j = pl.program_id(0)
    pairs = [(u, e) for u in range(_COMB_TILES) for e in range(N_EXPERTS)]

    def base_of(u, e):
        return (st_ref[e, j * _COMB_TILES + u] // BF16_ROWS) * BF16_ROWS

    def rows_of(u, e, nrows):
        win = (st_ref[e, j * _COMB_TILES] // _WIN_STEP) * _WIN_STEP
        return ye_refs[e][0, pl.ds(pl.multiple_of(base_of(u, e) - win, BF16_ROWS), nrows), :]

    def finish(u, acc):
        tok = slice(LANES * u, LANES * (u + 1))
        y_ref[tok, :] = _ln_rows(ALPHA * x_ref[tok, :] + acc, g_ref[...], b_ref[...])

    few = functools.reduce(
        jnp.logical_and,
        [st_ref[e, j * _COMB_TILES + u + 1] - base_of(u, e) <= _GATHER_ROWS_SMALL for u, e in pairs])

    @pl.when(few)
    def _():
        for u in range(_COMB_TILES):
            tok = slice(LANES * u, LANES * (u + 1))
            acc = None
            for e0 in range(0, N_EXPERTS, _COMB_PACK):
                es = list(range(e0, min(e0 + _COMB_PACK, N_EXPERTS)))
                col = lax.broadcasted_iota(I32, (LANES, len(es) * _GATHER_ROWS_SMALL), 1)
                hit = None
                for k, e in enumerate(es):
                    rel = pos_ref[tok, e:e + 1] - base_of(u, e)
                    h = col == jnp.where(rel >= 0, rel + k * _GATHER_ROWS_SMALL, -1)
                    hit = h if hit is None else hit | h
                slab = jnp.concatenate([rows_of(u, e, _GATHER_ROWS_SMALL) for e in es], axis=0)
                d = _dot(jnp.where(hit, 1.0, 0.0).astype(BF), slab)
                acc = d if acc is None else acc + d
            finish(u, acc)

    @pl.when(jnp.logical_not(few))
    def _():
        col = lax.broadcasted_iota(I32, (LANES, _GATHER_ROWS), 1)
        for u in range(_COMB_TILES):
            tok = slice(LANES * u, LANES * (u + 1))
            acc = None
            for e in range(N_EXPERTS):
                onehot = jnp.where(col == pos_ref[tok, e:e + 1] - base_of(u, e), 1.0, 0.0).astype(BF)
                d = _dot(onehot, rows_of(u, e, _GATHER_ROWS))
                acc = d if acc is None else acc + d
            finish(u, acc)


def _combine(starts, x, pos, g, b, ye):
    n = x.shape[0]
    assert ye.shape[1] >= EC_CAPACITY * n // N_EXPERTS + _WIN_ROWS
    tt = _COMB_TILES * LANES
    row = lambda j, st: (j, 0)
    fixed = lambda j, st: (0, 0)

    def ye_spec(e):
        return pl.BlockSpec((pl.Element(1), pl.Element(_WIN_ROWS), pl.Element(D_MODEL)),
                            lambda j, st: (e, (st[e, j * _COMB_TILES] // _WIN_STEP) * _WIN_STEP, 0))

    return pl.pallas_call(
        _combine_kernel,
        grid_spec=pltpu.PrefetchScalarGridSpec(
            num_scalar_prefetch=1,
            grid=(n // tt,),
            in_specs=[pl.BlockSpec((tt, D_MODEL), row),
                      pl.BlockSpec((tt, N_EXPERTS), row),
                      pl.BlockSpec((1, D_MODEL), fixed), pl.BlockSpec((1, D_MODEL), fixed)]
            + [ye_spec(e) for e in range(N_EXPERTS)],
            out_specs=pl.BlockSpec((tt, D_MODEL), row)),
        out_shape=jax.ShapeDtypeStruct((n, D_MODEL), F32),
        compiler_params=_params("arbitrary"),
        name="combine",
    )(starts, x, pos, g, b, *([ye] * N_EXPERTS))


def _moe(x, xg, aff3, wg, wu, wd, g, b):
    n = x.shape[0]
    cap = EC_CAPACITY * n // N_EXPERTS
    pos3, st3 = _select(aff3, cap)
    starts = jnp.concatenate([jnp.transpose(st3[:, :, 0]), jnp.full((N_EXPERTS, 1), cap, I32)], axis=1)
    ye = _moe_ffn(starts, xg, pos3, aff3, wg, wu, wd, cap)
    pos = jnp.transpose(pos3, (0, 2, 1)).reshape(n, N_EXPERTS)
    return _combine(starts, x, pos, g, b, ye)


def _prep_weights(p):
    w = {}
    w['a'] = [_prep_a(p['a_w_qkv'][j], p['a_q_gain'][j], p['a_k_gain'][j], p['a_w_o'][j])
              for j in range(p['a_w_qkv'].shape[0])]
    w['b'] = [_prep_b(p['b_w_qkv'][j], p['b_w_o'][j]) for j in range(p['b_w_qkv'].shape[0])]
    w['c'] = [_prep_c(p['c_lam_re'][j], p['c_lam_im'][j], p['c_log_dt'][j], p['c_b_re'][j], p['c_b_im'][j],
                      p['c_c_re'][j], p['c_c_im'][j]) + (p['c_d'][j][None, :], p['c_w_glu'][j].astype(BF))
              for j in range(p['c_lam_re'].shape[0])]
    w['d'] = []
    for j in range(p['d_w_qkv'].shape[0]):
        layer = N_MIXERS * j + 3
        lambda_init = 0.8 - 0.6 * math.exp(-0.3 * layer)
        w['d'].append(_prep_d(p['d_w_qkv'][j], p['d_norm_gain'][j], p['d_w_o'][j], lambda_init)
                      + (p['d_lam'][j].astype(F32), lambda_init))
    w['diff_bias'] = _diff_bias(p['rel_bias'], _D_T)
    w['x_w_q'] = p['x_w_q'].astype(BF)
    w['x_w_kv'] = p['x_w_kv'].astype(BF)
    w['x_w_o'] = p['x_w_o'].astype(BF)
    w['router_t'] = jnp.transpose(p['moe_w_router'], (0, 2, 1)).astype(F32)
    w['moe_w_gate'] = p['moe_w_gate'].astype(BF)
    w['moe_w_up'] = p['moe_w_up'].astype(BF)
    w['moe_w_down'] = p['moe_w_down'].astype(BF)
    return w


def _trunk(x, mem, p, w):
    b, s, _ = x.shape
    n = b * s
    mem_len = mem.shape[1]
    x = x.reshape(n, D_MODEL)
    mem2 = mem.reshape(b * mem_len, D_MODEL)
    ln_g, ln_b = p['ln_g'], p['ln_b']
    for i in range(DEPTH):
        m, j = i % N_MIXERS, i // N_MIXERS
        g0, b0 = ln_g[i, 0][None, :], ln_b[i, 0][None, :]
        if m == 0:
            wa, gains, wo = w['a'][j]
            qkv = _proj_a(x, wa, _rope_table(s), gains, s)
            o = _flash_a(qkv.reshape(b, s, _A_COLS))
            x = _post(x, o.reshape(n, A_HEADS * LANES), wo, g0, b0)
        elif m == 1:
            wb, wo = w['b'][j]
            qkv = _proj(x, wb).reshape(b, s, wb.shape[1])
            os_, ls_ = [], []
            for g in range(len(B_PATTERNS)):
                t = _tile(s, _B_T)
                o, lse = _band_attention(qkv, _band_bias(p['rel_bias'], g, t), g, t)
                os_.append(o)
                ls_.append(lse)
            x = _post_b(x, os_, ls_, wo, g0, b0)
        elif m == 2:
            w_sum, w_lag, w_state, dec, dskip, wglu = w['c'][j]
            ys = _s5(x, b, s, w_sum, w_lag, w_state, dec)
            x = _post_c(x, ys, dskip, wglu, g0, b0)
        else:
            wd, gain, wo, lam, lambda_init = w['d'][j]
            qkv = _proj(x, wd, ones_from=2 * D_HEADS * LANES).reshape(b, s, 3 * D_HEADS * LANES)
            o = _flash_d(qkv, w['diff_bias'], lam, gain, lambda_init)
            x = _post(x, o.reshape(n, D_HEADS * LANES), wo, g0, b0)
        kv = _proj(mem2, w['x_w_kv'][i], tm=mem_len)
        x, xbf, aff3 = _cross(x, kv, w['x_w_q'][i], w['x_w_o'][i], ln_g[i, 1][None, :], ln_b[i, 1][None, :],
                              w['router_t'][i], s, mem_len)
        x = _moe(x, xbf, aff3, w['moe_w_gate'][i], w['moe_w_up'][i], w['moe_w_down'][i],
                 ln_g[i, 2][None, :], ln_b[i, 2][None, :])
    return x.reshape(b, s, D_MODEL)


def kernel(x_prompt, x_sample, mem_prompt, mem_sample, rel_bias, ln_g, ln_b, a_w_qkv, a_q_gain, a_k_gain, a_w_o, b_w_qkv, b_w_o, c_lam_re, c_lam_im, c_log_dt, c_b_re, c_b_im, c_c_re, c_c_im, c_d, c_w_glu, d_w_qkv, d_lam, d_norm_gain, d_w_o, x_w_q, x_w_kv, x_w_o, moe_w_router, moe_w_gate, moe_w_up, moe_w_down):
    p = dict(rel_bias=rel_bias, ln_g=ln_g, ln_b=ln_b,
             a_w_qkv=a_w_qkv, a_q_gain=a_q_gain, a_k_gain=a_k_gain, a_w_o=a_w_o,
             b_w_qkv=b_w_qkv, b_w_o=b_w_o,
             c_lam_re=c_lam_re, c_lam_im=c_lam_im, c_log_dt=c_log_dt, c_b_re=c_b_re, c_b_im=c_b_im,
             c_c_re=c_c_re, c_c_im=c_c_im, c_d=c_d, c_w_glu=c_w_glu,
             d_w_qkv=d_w_qkv, d_lam=d_lam, d_norm_gain=d_norm_gain, d_w_o=d_w_o,
             x_w_q=x_w_q, x_w_kv=x_w_kv, x_w_o=x_w_o,
             moe_w_router=moe_w_router, moe_w_gate=moe_w_gate, moe_w_up=moe_w_up, moe_w_down=moe_w_down)
    w = _prep_weights(p)
    return (_trunk(x_prompt, mem_prompt, p, w), _trunk(x_sample, mem_sample, p, w))
```

```python
import functools
import math

import numpy as np
import jax
import jax.numpy as jnp
from jax import lax
from jax.experimental import pallas as pl
from jax.experimental.pallas import tpu as pltpu

F32 = jnp.float32
BF = jnp.bfloat16
I32 = jnp.int32

D_MODEL = 1024
DEPTH = 4
GRID_W = 64
N_MIXERS = 4
LN_EPS = 1e-5
RMS_EPS = 1e-6
ALPHA = (2.0 * DEPTH) ** 0.25

A_HEADS = 16
A_KV_HEADS = 4
A_HEAD_DIM = 64
ROPE_BASE = 10000.0

B_PATTERNS = ((128, 1), (512, 4), (2048, 16))
B_HEADS_PER_GROUP = 4
B_HEAD_DIM = 64

C_GROUP = 16
C_N_GROUPS = D_MODEL // C_GROUP
C_STATE = 64
C_CHUNK = 16
D_HEADS = 12
D_QK_DIM = 32
D_V_DIM = 64

REL_BUCKETS = 32
REL_MAX_DIST = 128

X_HEADS = 4
X_HEAD_DIM = D_MODEL // X_HEADS

N_EXPERTS = 16
EC_CAPACITY = 2

LANES = 128
BF16_ROWS = 16
MXU_DEPTH = 256
_C_GL = LANES // C_GROUP
VMEM_LIMIT = 56 * 1024 * 1024
NEG = -1e30
LOG2E = math.log2(math.e)


def _params(*sem):
    return pltpu.CompilerParams(dimension_semantics=sem, vmem_limit_bytes=VMEM_LIMIT)


def _tile(n, pref):
    t = min(n, pref)
    assert n % t == 0, (n, pref)
    return t


def _ln_rows(v, g, b):
    mu = jnp.mean(v, -1, keepdims=True)
    c = v - mu
    var = jnp.mean(c * c, -1, keepdims=True)
    return c * lax.rsqrt(var + LN_EPS) * g + b


def _dot_nt(a, b):
    return lax.dot_general(a, b, (((1,), (1,)), ((), ())), preferred_element_type=F32)


def _dot(a, b):
    return jnp.dot(a, b, preferred_element_type=F32)


_ONE_LANE = 64


def _with_ones_lane(y):
    lane = lax.broadcasted_iota(I32, y.shape, 1)
    return jnp.where(lane % LANES == _ONE_LANE, 1.0, y)


def _proj_kernel(x_ref, w_ref, o_ref, *, ones_from):
    y = _dot(x_ref[...].astype(BF), w_ref[...])
    if ones_from is None:
        o_ref[...] = y.astype(o_ref.dtype)
    else:
        o_ref[:, :ones_from] = y[:, :ones_from].astype(o_ref.dtype)
        o_ref[:, ones_from:] = _with_ones_lane(y[:, ones_from:]).astype(o_ref.dtype)


def _proj(x, w, tm=512, ones_from=None):
    n, k = x.shape
    m = w.shape[1]
    tm = _tile(n, tm)
    return pl.pallas_call(
        functools.partial(_proj_kernel, ones_from=ones_from),
        grid=(n // tm,),
        in_specs=[pl.BlockSpec((tm, k), lambda i: (i, 0)),
                  pl.BlockSpec((k, m), lambda i: (0, 0))],
        out_specs=pl.BlockSpec((tm, m), lambda i: (i, 0)),
        out_shape=jax.ShapeDtypeStruct((n, m), BF),
        compiler_params=_params("parallel"),
        name="proj",
    )(x, w)


def _post_kernel(x_ref, o_ref, w_ref, g_ref, b_ref, y_ref):
    h = _dot(o_ref[...], w_ref[...])
    y_ref[...] = _ln_rows(ALPHA * x_ref[...] + h, g_ref[...], b_ref[...])


def _post(x, o, w, g, b, tm=512):
    n = x.shape[0]
    ko = o.shape[1]
    tm = _tile(n, tm)
    return pl.pallas_call(
        _post_kernel,
        grid=(n // tm,),
        in_specs=[pl.BlockSpec((tm, D_MODEL), lambda i: (i, 0)),
                  pl.BlockSpec((tm, ko), lambda i: (i, 0)),
                  pl.BlockSpec((ko, D_MODEL), lambda i: (0, 0)),
                  pl.BlockSpec((1, D_MODEL), lambda i: (0, 0)),
                  pl.BlockSpec((1, D_MODEL), lambda i: (0, 0))],
        out_specs=pl.BlockSpec((tm, D_MODEL), lambda i: (i, 0)),
        out_shape=jax.ShapeDtypeStruct((n, D_MODEL), F32),
        compiler_params=_params("parallel"),
        name="post",
    )(x, o, w, g, b)


def _post_b_kernel(x_ref, o0_ref, o1_ref, o2_ref, l0_ref, l1_ref, l2_ref, w_ref, g_ref, b_ref, y_ref):
    l0, l1, l2 = l0_ref[...], l1_ref[...], l2_ref[...]
    m = jnp.maximum(jnp.maximum(l0, l1), l2)
    e0, e1, e2 = jnp.exp(l0 - m), jnp.exp(l1 - m), jnp.exp(l2 - m)
    inv = 1.0 / (e0 + e1 + e2)
    gw = B_HEADS_PER_GROUP * B_HEAD_DIM
    h = _dot((o0_ref[...].astype(F32) * (e0 * inv)).astype(BF), w_ref[0:gw, :])
    h = h + _dot((o1_ref[...].astype(F32) * (e1 * inv)).astype(BF), w_ref[gw:2 * gw, :])
    h = h + _dot((o2_ref[...].astype(F32) * (e2 * inv)).astype(BF), w_ref[2 * gw:3 * gw, :])
    y_ref[...] = _ln_rows(ALPHA * x_ref[...] + h, g_ref[...], b_ref[...])


def _post_b(x, os_, ls_, w, g, b, tm=512):
    n = x.shape[0]
    tm = _tile(n, tm)
    gw = B_HEADS_PER_GROUP * B_HEAD_DIM
    row = lambda i: (i, 0)
    fixed = lambda i: (0, 0)
    return pl.pallas_call(
        _post_b_kernel,
        grid=(n // tm,),
        in_specs=[pl.BlockSpec((tm, D_MODEL), row)] + [pl.BlockSpec((tm, gw), row)] * 6
        + [pl.BlockSpec((3 * gw, D_MODEL), fixed), pl.BlockSpec((1, D_MODEL), fixed),
           pl.BlockSpec((1, D_MODEL), fixed)],
        out_specs=pl.BlockSpec((tm, D_MODEL), row),
        out_shape=jax.ShapeDtypeStruct((n, D_MODEL), F32),
        compiler_params=_params("parallel"),
        name="post_b",
    )(x, *os_, *ls_, w, g, b)


def _post_c_kernel(x_ref, ys_ref, d_ref, w_ref, g_ref, b_ref, y_ref):
    x = x_ref[...]
    z = jax.nn.gelu(ys_ref[...] + d_ref[...] * x).astype(BF)
    h = _dot(z, w_ref[...])
    hh = h[:, :D_MODEL] * jax.nn.sigmoid(h[:, D_MODEL:])
    y_ref[...] = _ln_rows(ALPHA * x + hh, g_ref[...], b_ref[...])


def _post_c(x, ys, d, w, g, b, tm=512):
    n = x.shape[0]
    tm = _tile(n, tm)
    row = lambda i: (i, 0)
    fixed = lambda i: (0, 0)
    return pl.pallas_call(
        _post_c_kernel,
        grid=(n // tm,),
        in_specs=[pl.BlockSpec((tm, D_MODEL), row), pl.BlockSpec((tm, D_MODEL), row),
                  pl.BlockSpec((1, D_MODEL), fixed), pl.BlockSpec((D_MODEL, 2 * D_MODEL), fixed),
                  pl.BlockSpec((1, D_MODEL), fixed), pl.BlockSpec((1, D_MODEL), fixed)],
        out_specs=pl.BlockSpec((tm, D_MODEL), row),
        out_shape=jax.ShapeDtypeStruct((n, D_MODEL), F32),
        compiler_params=_params("parallel"),
        name="post_c",
    )(x, ys, d, w, g, b)


_A_QK_TILES = A_HEADS + A_KV_HEADS
_A_COLS = (_A_QK_TILES + A_KV_HEADS) * LANES


def _rope_partner():
    d = np.arange(A_HEAD_DIM)
    e = d % (A_HEAD_DIM // 2)
    lo = e < A_HEAD_DIM // 4
    return np.where(lo, d + A_HEAD_DIM // 4, d - A_HEAD_DIM // 4), np.where(lo, -1.0, 1.0).astype(np.float32)


def _prep_a(w_qkv, q_gain, k_gain, w_o):
    partner, sign = _rope_partner()
    nqk = _A_QK_TILES * A_HEAD_DIM
    wqk = w_qkv[:, :nqk].reshape(D_MODEL, _A_QK_TILES, A_HEAD_DIM)
    wsw = wqk[:, :, partner] * sign
    wqk = jnp.concatenate([wqk, wsw], -1).reshape(D_MODEL, _A_QK_TILES * LANES)
    wv = w_qkv[:, nqk:].reshape(D_MODEL, A_KV_HEADS, A_HEAD_DIM)
    wv = jnp.concatenate([wv, jnp.zeros_like(wv)], -1).reshape(D_MODEL, A_KV_HEADS * LANES)
    w = jnp.concatenate([wqk, wv], 1).astype(BF)
    gq = jnp.concatenate([q_gain, q_gain[partner]]) * (A_HEAD_DIM ** -0.5 * 0.5 * LOG2E)
    gk = jnp.concatenate([k_gain, k_gain[partner]])
    gains = jnp.stack([gq, gk], 0)
    return w, gains, w_o.astype(BF)


def _rope_table(s):
    pos = jnp.arange(s)
    rows, cols = (pos // GRID_W).astype(F32), (pos % GRID_W).astype(F32)
    half = A_HEAD_DIM // 2
    freqs = ROPE_BASE ** (-jnp.arange(0, half, 2, dtype=F32) / half)
    ang_r = rows[:, None] * freqs
    ang_c = cols[:, None] * freqs
    ang = jnp.concatenate([ang_r, ang_r, ang_c, ang_c], -1)
    return jnp.concatenate([jnp.cos(ang), jnp.sin(ang)], -1)


def _proj_a_kernel(x_ref, w_ref, cs_ref, g_ref, o_ref):
    xb = x_ref[...].astype(BF)
    cs = cs_ref[...]
    gq = cs * g_ref[0:1, :]
    gk = cs * g_ref[1:2, :]
    per = 4
    for g in range(_A_QK_TILES // per):
        y = _dot(xb, w_ref[:, LANES * per * g:LANES * per * (g + 1)])
        for hh in range(per):
            h = per * g + hh
            t = y[:, LANES * hh:LANES * (hh + 1)]
            r = lax.rsqrt(jnp.sum(t * t, -1, keepdims=True) * (1.0 / LANES) + RMS_EPS)
            e = t * r * (gq if h < A_HEADS else gk)
            o_ref[:, LANES * h:LANES * (h + 1)] = (e + pltpu.roll(e, LANES // 2, 1)).astype(BF)
    yv = _dot(xb, w_ref[:, _A_QK_TILES * LANES:])
    o_ref[:, _A_QK_TILES * LANES:] = _with_ones_lane(yv).astype(BF)


def _proj_a(x, w, cs, gains, s, tm=512):
    n = x.shape[0]
    tm = _tile(s, tm)
    per = s // tm
    return pl.pallas_call(
        _proj_a_kernel,
        grid=(n // tm,),
        in_specs=[pl.BlockSpec((tm, D_MODEL), lambda i: (i, 0)),
                  pl.BlockSpec((D_MODEL, _A_COLS), lambda i: (0, 0)),
                  pl.BlockSpec((tm, LANES), lambda i: (i % per, 0)),
                  pl.BlockSpec((2, LANES), lambda i: (0, 0))],
        out_specs=pl.BlockSpec((tm, _A_COLS), lambda i: (i, 0)),
        out_shape=jax.ShapeDtypeStruct((n, _A_COLS), BF),
        compiler_params=_params("parallel"),
        name="proj_a",
    )(x, w, cs, gains)


def _softmax_step(s, v, m_ref, acc_ref):
    tk = s.shape[1]
    m_old = m_ref[...]
    m_new = jnp.maximum(m_old, jnp.max(s, -1, keepdims=True))
    p = jnp.concatenate([jnp.exp2(s[:, LANES * c:LANES * (c + 1)] - m_new).astype(BF) for c in range(tk // LANES)],
                        axis=1)
    acc_ref[...] = jnp.exp2(m_old - m_new) * acc_ref[...] + _dot(p, v)
    m_ref[...] = m_new


def _softmax_init(m_ref, acc_ref):
    m_ref[...] = jnp.full(m_ref.shape, -jnp.inf, F32)
    acc_ref[...] = jnp.zeros_like(acc_ref)


def _flash_a_kernel(q_ref, k_ref, v_ref, o_ref, m_ref, acc_ref, *, tk, nk):
    tq = q_ref.shape[0]
    rep = A_HEADS // A_KV_HEADS
    q = jnp.concatenate([q_ref[:, LANES * r:LANES * (r + 1)] for r in range(rep)], axis=0)
    _softmax_init(m_ref, acc_ref)

    def body(j, carry):
        off = pl.multiple_of(j * tk, tk)
        _softmax_step(_dot_nt(q, k_ref[pl.ds(off, tk), :]), v_ref[pl.ds(off, tk), :], m_ref, acc_ref)
        return carry

    lax.fori_loop(0, nk, body, 0, unroll=4)
    acc = acc_ref[...]
    o = acc / acc[:, _ONE_LANE:_ONE_LANE + 1]
    first = lax.broadcasted_iota(I32, (tq, LANES), 1) < A_HEAD_DIM
    for p in range(rep // 2):
        lo = o[2 * p * tq:(2 * p + 1) * tq]
        hi = pltpu.roll(o[(2 * p + 1) * tq:(2 * p + 2) * tq], A_HEAD_DIM, 1)
        o_ref[:, LANES * p:LANES * (p + 1)] = jnp.where(first, lo, hi).astype(BF)


def _flash_a(qkv, tq=256, tk=2048):
    b, s, _ = qkv.shape
    tq, tk = _tile(s, tq), _tile(s, tk)
    rep = A_HEADS // A_KV_HEADS
    gw = rep * LANES
    return pl.pallas_call(
        functools.partial(_flash_a_kernel, tk=tk, nk=s // tk),
        grid=(b, A_KV_HEADS, s // tq),
        in_specs=[pl.BlockSpec((None, tq, gw), lambda bi, g, i: (bi, i, g)),
                  pl.BlockSpec((None, s, LANES), lambda bi, g, i: (bi, 0, A_HEADS + g)),
                  pl.BlockSpec((None, s, LANES), lambda bi, g, i: (bi, 0, _A_QK_TILES + g))],
        out_specs=pl.BlockSpec((None, tq, rep * A_HEAD_DIM), lambda bi, g, i: (bi, i, g)),
        out_shape=jax.ShapeDtypeStruct((b, s, A_HEADS * A_HEAD_DIM), BF),
        scratch_shapes=[pltpu.VMEM((rep * tq, LANES), F32)] * 2,
        compiler_params=_params("parallel", "parallel", "arbitrary"),
        name="flash_a",
    )(qkv, qkv, qkv)


def _bucket(rel):
    half = REL_BUCKETS // 2
    max_exact = half // 2
    n = jnp.abs(rel)
    large = max_exact + (jnp.log(jnp.maximum(n, 1).astype(F32) / max_exact)
                         / math.log(REL_MAX_DIST / max_exact) * (half - max_exact)).astype(I32)
    large = jnp.minimum(large, half - 1)
    return jnp.where(rel > 0, half, 0) + jnp.where(n < max_exact, n, large)


_B_GW = B_HEADS_PER_GROUP * B_HEAD_DIM
_B_T = 256


def _pad_heads(w, heads, dim):
    w = w.reshape(w.shape[0], heads, dim)
    return jnp.concatenate([w, jnp.zeros((w.shape[0], heads, LANES - dim), w.dtype)], -1).reshape(
        w.shape[0], heads * LANES)


def _prep_b(w_qkv, w_o):
    c = len(B_PATTERNS) * B_HEADS_PER_GROUP * B_HEAD_DIM
    w = jnp.concatenate([w_qkv[:, :c] * (B_HEAD_DIM ** -0.5), w_qkv[:, c:]], 1).astype(BF)
    return w, w_o.astype(BF)


def _toeplitz(vec, t):
    flat = jnp.tile(vec, (1,) * (vec.ndim - 1) + (t,))[..., :t * (2 * t - 1)]
    return flat.reshape(vec.shape[:-1] + (t, 2 * t - 1))[..., :t]


def _band_tiles(g, t):
    window, dil = B_PATTERNS[g]
    return pl.cdiv((window // (2 * dil)) * dil, t)


def _band_bias(rel_bias, g, t):
    window, dil = B_PATTERNS[g]
    reach = (window // (2 * dil)) * dil
    n = _band_tiles(g, t)
    rel0 = (jnp.arange(2 * t) + t) % (2 * t) - t
    rel = jnp.arange(-n, n + 1)[:, None] * t + rel0[None, :]
    bias = rel_bias[_bucket(rel)][:, :, g * B_HEADS_PER_GROUP:(g + 1) * B_HEADS_PER_GROUP]
    bias = jnp.where(((rel % dil == 0) & (jnp.abs(rel) <= reach))[:, :, None], bias, NEG)
    return _toeplitz(jnp.transpose(bias, (2, 0, 1)).astype(F32), t)


def _band_kernel(q_ref, k_ref, v_ref, bias_ref, o_ref, lse_ref, *, n, nk):
    t = q_ref.shape[0]
    i = pl.program_id(2)
    qf = q_ref[...].astype(F32)
    lane = lax.broadcasted_iota(I32, (t, LANES), 1)
    offs = [pl.multiple_of(jnp.clip(i + o, 0, nk - 1) * t, t) for o in range(-n, n + 1)]
    inside = [(i + o >= 0) & (i + o < nk) for o in range(-n, n + 1)]
    outs, lses = [], []
    for hh in range(2):
        q = jnp.where((lane >= B_HEAD_DIM * hh) & (lane < B_HEAD_DIM * (hh + 1)), qf, 0.0).astype(BF)
        logits = [jnp.where(ok, _dot_nt(q, k_ref[pl.ds(off, t), :]) + bias_ref[hh, c], NEG)
                  for c, (off, ok) in enumerate(zip(offs, inside))]
        m = logits[0].max(-1, keepdims=True)
        for s in logits[1:]:
            m = jnp.maximum(m, s.max(-1, keepdims=True))
        l = jnp.zeros((t, 1), F32)
        acc = jnp.zeros((t, LANES), F32)
        for s, off in zip(logits, offs):
            p = jnp.exp(s - m)
            l = l + jnp.sum(p, -1, keepdims=True)
            acc = acc + _dot(p.astype(BF), v_ref[pl.ds(off, t), :])
        outs.append(acc / l)
        lses.append(jnp.broadcast_to(m + jnp.log(l), (t, LANES)))
    first = lane < B_HEAD_DIM
    o_ref[...] = jnp.where(first, outs[0], outs[1]).astype(BF)
    lse_ref[...] = jnp.where(first, lses[0], lses[1])


def _band_attention(qkv, bias, g, t):
    b, s, c = qkv.shape
    third = c // 3 // LANES
    pairs = B_HEADS_PER_GROUP // 2
    n = _band_tiles(g, t)
    spec = lambda base: pl.BlockSpec((None, s, LANES), lambda bi, hp, i: (bi, 0, base + g * pairs + hp))
    out_spec = pl.BlockSpec((None, t, LANES), lambda bi, hp, i: (bi, i, hp))
    o, lse = pl.pallas_call(
        functools.partial(_band_kernel, n=n, nk=s // t),
        grid=(b, pairs, s // t),
        in_specs=[pl.BlockSpec((None, t, LANES), lambda bi, hp, i: (bi, i, g * pairs + hp)),
                  spec(third), spec(2 * third),
                  pl.BlockSpec((2, 2 * n + 1, t, t), lambda bi, hp, i: (hp, 0, 0, 0))],
        out_specs=[out_spec, out_spec],
        out_shape=[jax.ShapeDtypeStruct((b, s, _B_GW), BF), jax.ShapeDtypeStruct((b, s, _B_GW), F32)],
        compiler_params=_params("parallel", "parallel", "arbitrary"),
        name="band_%d" % g,
    )(qkv, qkv, qkv, bias)
    return o.reshape(b * s, _B_GW), lse.reshape(b * s, _B_GW)


def _prep_c(lam_re, lam_im, log_dt, b_re, b_im, c_re, c_im):
    hp = lax.Precision.HIGHEST
    L, P, C = C_CHUNK, C_STATE, C_GROUP
    lr, li = lam_re.astype(F32), lam_im.astype(F32)
    dt = jnp.exp(log_dt.astype(F32))[..., None]
    mag = jnp.exp(lr * dt)
    ar, ai = mag * jnp.cos(li * dt), mag * jnp.sin(li * dt)
    den = lr * lr + li * li
    zr = ((ar - 1.0) * lr + ai * li) / den
    zi = (ai * lr - (ar - 1.0) * li) / den
    br, bi = b_re.astype(F32), b_im.astype(F32)
    bbr = zr[..., None] * br - zi[..., None] * bi
    bbi = zr[..., None] * bi + zi[..., None] * br
    cr, ci = c_re.astype(F32), c_im.astype(F32)
    prs, pis = [jnp.ones_like(ar)], [jnp.zeros_like(ai)]
    for _ in range(L):
        pr_, pi_ = prs[-1], pis[-1]
        prs.append(ar * pr_ - ai * pi_)
        pis.append(ar * pi_ + ai * pr_)
    pr, pi = jnp.stack(prs, 0), jnp.stack(pis, 0)

    def lag(pr_k, pi_k):
        tr = pr_k[..., None] * bbr - pi_k[..., None] * bbi
        ti = pr_k[..., None] * bbi + pi_k[..., None] * bbr
        return (jnp.einsum('dgop,kdgpi->kdgoi', cr, tr, precision=hp)
                - jnp.einsum('dgop,kdgpi->kdgoi', ci, ti, precision=hp))

    kern = lag(pr[:L], pi[:L])
    lags = jnp.concatenate([kern[:0:-1, 1], (kern[0, 0] + kern[0, 1])[None], kern[1:, 0]], 0)

    def summ(d, powers):
        pr_k, pi_k = pr[powers, d], pi[powers, d]
        sr = pr_k[..., None] * bbr[d] - pi_k[..., None] * bbi[d]
        si = pr_k[..., None] * bbi[d] + pi_k[..., None] * bbr[d]
        s = jnp.concatenate([sr, si], 2)
        return jnp.transpose(s, (1, 0, 3, 2)).reshape(C_N_GROUPS, L * C, 2 * P)

    w_sum = jnp.concatenate([summ(0, np.arange(L - 1, -1, -1)), summ(1, np.arange(L))], -1)

    def outw(d, powers):
        pr_k, pi_k = pr[powers, d], pi[powers, d]
        wr = cr[d][None] * pr_k[:, :, None, :] - ci[d][None] * pi_k[:, :, None, :]
        wi = -(cr[d][None] * pi_k[:, :, None, :] + ci[d][None] * pr_k[:, :, None, :])
        w = jnp.concatenate([wr, wi], -1)
        return jnp.transpose(w, (1, 3, 0, 2)).reshape(C_N_GROUPS, 2 * P, L * C)

    w_state = jnp.concatenate([outw(0, np.arange(1, L + 1)), outw(1, np.arange(L, 0, -1))], 1)

    nq = C_N_GROUPS // _C_GL
    npair = _C_GL // 2
    blk = L * _C_GL * C
    lane = jnp.arange(LANES)
    gsel = (lane[None, :] // C == jnp.arange(_C_GL)[:, None])
    src = jnp.swapaxes(lags, -1, -2).astype(BF).reshape(2 * L - 1, nq, _C_GL, C, C)
    src = jnp.where(gsel[None, None, :, None, :], jnp.tile(src, (1, 1, 1, 1, _C_GL)), 0)
    w_lag = jnp.transpose(src.reshape(2 * L - 1, nq, LANES, LANES), (1, 0, 2, 3))
    psel = (lane[None, :] // P == jnp.arange(2)[:, None])
    ksel = jnp.eye(npair, dtype=bool)
    src = jnp.tile(w_sum.astype(BF).reshape(nq, npair, 2, L, C, 1, 4, P), (1, 1, 1, 1, 1, npair, 1, 2))
    src = jnp.where(psel[None, None, :, None, None, None, None, :]
                    & ksel[None, :, None, None, None, :, None, None], src, 0)
    w_sum = jnp.transpose(src, (0, 3, 1, 2, 4, 5, 6, 7)).reshape(nq, blk, _C_GL * 4 * P)
    osel = (lane[None, None, :] // C
            == (2 * jnp.arange(npair)[:, None, None] + jnp.arange(2)[None, :, None]))
    src = jnp.tile(w_state.astype(BF).reshape(nq, npair, 2, 4, P, L, C), (1, 1, 1, 1, 1, 1, _C_GL))
    src = jnp.where(osel[None, :, :, None, None, None, :], src, 0)
    w_state = jnp.transpose(src, (0, 1, 3, 2, 4, 5, 6)).reshape(nq, _C_GL * 4 * P, blk)
    dec = jnp.stack([pr[L, 0], pi[L, 0], pr[L, 1], pi[L, 1]], 0).reshape(4, nq * npair, 2 * P)
    dec = jnp.broadcast_to(jnp.transpose(dec, (1, 0, 2))[:, :, None, :], (nq * npair, 4, 8, 2 * P))
    return w_sum.astype(BF), w_lag, w_state.astype(BF), dec


def _chunk_rows(x_ref, nb, rc):
    L = C_CHUNK
    rows = [jnp.concatenate([x_ref[b, pl.ds(j, rc, stride=L), :].astype(BF) for j in range(L)], axis=1)
            for b in range(nb)]
    return jnp.concatenate(rows, axis=0)


def _s5_sum_kernel(x_ref, w_ref, s_ref, *, nb, rc):
    s = _dot(_chunk_rows(x_ref, nb, rc), w_ref[...])
    for b in range(nb):
        for t in range(s_ref.shape[0]):
            s_ref[t, pl.ds(b, rc, stride=nb), :] = s[b * rc:(b + 1) * rc, LANES * t:LANES * (t + 1)]


def _s5_scan_kernel(s_ref, dec_ref, e_ref, *, nc, nb):
    units = s_ref.shape[0] // 4
    dec = [[dec_ref[u, k, 0:nb, :] for k in range(4)] for u in range(units)]

    def body(t, carry):
        rf = pl.ds(pl.multiple_of(t * nb, nb), nb)
        rb = pl.ds(pl.multiple_of((nc - 1 - t) * nb, nb), nb)
        new = []
        for u in range(units):
            fr, fi, br, bi = carry[4 * u:4 * u + 4]
            arf, aif, arb, aib = dec[u]
            e_ref[4 * u, rf, :] = fr
            e_ref[4 * u + 1, rf, :] = fi
            e_ref[4 * u + 2, rb, :] = br
            e_ref[4 * u + 3, rb, :] = bi
            new += [fr * arf - fi * aif + s_ref[4 * u, rf, :], fi * arf + fr * aif + s_ref[4 * u + 1, rf, :],
                    br * arb - bi * aib + s_ref[4 * u + 2, rb, :], bi * arb + br * aib + s_ref[4 * u + 3, rb, :]]
        return tuple(new)

    z = jnp.zeros((nb, LANES), F32)
    lax.fori_loop(0, nc, body, (z,) * (4 * units))


def _s5_out_kernel(x_ref, e_ref, lag_ref, wc_ref, y_ref, wi_ref, *, nb, rc):
    L = C_CHUNK
    half = pl.program_id(2)

    @pl.when((pl.program_id(1) == 0) & (half == 0))
    def _():
        for j in range(L):
            for i in range(L):
                wi_ref[i // (L // 2), LANES * j:LANES * (j + 1), LANES * (i % (L // 2)):LANES * (i % (L // 2) + 1)] = (
                    lag_ref[i - j + L - 1])

    xc = _chunk_rows(x_ref, nb, rc)
    ec = jnp.concatenate(
        [jnp.concatenate([e_ref[t, pl.ds(b, rc, stride=nb), :].astype(BF) for t in range(e_ref.shape[0])], axis=1)
         for b in range(nb)], axis=0)
    y = _dot(xc, wi_ref[half]) + _dot(ec, wc_ref[...])
    for b in range(nb):
        for ii in range(L // 2):
            y_ref[b, pl.ds(half * (L // 2) + ii, rc, stride=L), :] = y[b * rc:(b + 1) * rc, LANES * ii:LANES * (ii + 1)]


def _s5(x, b, s, w_sum, w_lag, w_state, dec):
    L = C_CHUNK
    nc = s // L
    nq = D_MODEL // LANES
    nsl = w_sum.shape[2] // LANES
    blk = L * LANES
    rc = _tile(nc, max(8, 512 // b))
    x3 = x.reshape(b, s, D_MODEL)
    x_spec = pl.BlockSpec((b, rc * L, LANES), lambda q, c, *_: (0, c, q))
    sums = pl.pallas_call(
        functools.partial(_s5_sum_kernel, nb=b, rc=rc),
        grid=(nq, nc // rc),
        in_specs=[x_spec, pl.BlockSpec((None, blk, nsl * LANES), lambda q, c: (q, 0, 0))],
        out_specs=pl.BlockSpec((nsl, rc * b, LANES), lambda q, c: (q, c, 0)),
        out_shape=jax.ShapeDtypeStruct((nq * nsl, nc * b, LANES), F32),
        compiler_params=_params("parallel", "arbitrary"),
        name="s5_sum",
    )(x3, w_sum)
    upb = 2
    ent = pl.pallas_call(
        functools.partial(_s5_scan_kernel, nc=nc, nb=b),
        grid=(nq * nsl // (4 * upb),),
        in_specs=[pl.BlockSpec((4 * upb, nc * b, LANES), lambda i: (i, 0, 0)),
                  pl.BlockSpec((upb, 4, 8, LANES), lambda i: (i, 0, 0, 0))],
        out_specs=pl.BlockSpec((4 * upb, nc * b, LANES), lambda i: (i, 0, 0)),
        out_shape=jax.ShapeDtypeStruct((nq * nsl, nc * b, LANES), F32),
        compiler_params=_params("parallel"),
        name="s5_scan",
    )(sums, dec)
    y = pl.pallas_call(
        functools.partial(_s5_out_kernel, nb=b, rc=rc),
        grid=(nq, nc // rc, 2),
        in_specs=[x_spec,
                  pl.BlockSpec((nsl, rc * b, LANES), lambda q, c, h: (q, c, 0)),
                  pl.BlockSpec((None, 2 * L - 1, LANES, LANES), lambda q, c, h: (q, 0, 0, 0)),
                  pl.BlockSpec((None, nsl * LANES, blk // 2), lambda q, c, h: (q, 0, h))],
        out_specs=pl.BlockSpec((b, rc * L, LANES), lambda q, c, h: (0, c, q)),
        out_shape=jax.ShapeDtypeStruct((b, s, D_MODEL), F32),
        scratch_shapes=[pltpu.VMEM((2, blk, blk // 2), BF)],
        compiler_params=_params("arbitrary", "arbitrary", "arbitrary"),
        name="s5_out",
    )(x3, ent, w_lag, w_state)
    return y.reshape(b * s, D_MODEL)


_D_T = 512


def _prep_d(w_qkv, norm_gain, w_o, lambda_init):
    qk_w = D_HEADS * 2 * D_QK_DIM
    wq = w_qkv[:, :qk_w] * (D_QK_DIM ** -0.5 * LOG2E)
    wv = _pad_heads(w_qkv[:, 2 * qk_w:], D_HEADS, D_V_DIM)
    w = jnp.concatenate([wq, w_qkv[:, qk_w:2 * qk_w], wv], 1).astype(BF)
    gain = jnp.concatenate([norm_gain * (1.0 - lambda_init), jnp.zeros((LANES - D_V_DIM,), F32)])[None, :]
    wo = w_o.reshape(D_HEADS, D_V_DIM, D_MODEL)
    wo = jnp.concatenate([wo, jnp.zeros_like(wo)], 1).reshape(D_HEADS * LANES, D_MODEL).astype(BF)
    return w, gain, wo


def _diff_bias(rel_bias, t):
    rel0 = (jnp.arange(2 * t) + t) % (2 * t) - t
    rel = jnp.arange(-2, 3)[:, None] * t + rel0[None, :]
    return jnp.transpose(rel_bias[_bucket(rel)] * LOG2E, (2, 0, 1)).astype(F32)


def _flash_d_kernel(q_ref, k_ref, v_ref, vec_ref, lam_ref, gain_ref, o_ref, m_ref, acc_ref, bias_ref, *, kt, nk,
                    lambda_init):
    t = q_ref.shape[0]
    i = pl.program_id(2)

    @pl.when(i == 0)
    def _():
        for d in range(bias_ref.shape[0]):
            full = jnp.broadcast_to(vec_ref[d:d + 1, :], (t, 2 * t))
            bias_ref[d] = pltpu.roll(full, 0, 1, stride=1, stride_axis=0)[:, :t]

    qv = q_ref[...].astype(F32)
    lane = lax.broadcasted_iota(I32, (t, LANES), 1)
    lo = (pl.program_id(1) % 2) * (2 * D_QK_DIM)
    q0 = jnp.where((lane >= lo) & (lane < lo + D_QK_DIM), qv, 0.0).astype(BF)
    q1 = jnp.where((lane >= lo + D_QK_DIM) & (lane < lo + 2 * D_QK_DIM), qv, 0.0).astype(BF)
    q = jnp.concatenate([q0, q1], axis=0)
    _softmax_init(m_ref, acc_ref)

    def body(j, carry):
        off = pl.multiple_of(j * (kt * t), kt * t)
        bias = jnp.concatenate([bias_ref[jnp.clip(j * kt + c - i, -2, 2) + 2] for c in range(kt)], axis=1)
        s = _dot_nt(q, k_ref[pl.ds(off, kt * t), :])
        s = (s.reshape(2, t, kt * t) + bias[None]).reshape(2 * t, kt * t)
        _softmax_step(s, v_ref[pl.ds(off, kt * t), :], m_ref, acc_ref)
        return carry

    lax.fori_loop(0, nk, body, 0, unroll=4)
    lf = lam_ref[...]
    lam = (jnp.exp(jnp.sum(lf[0:1] * lf[1:2], keepdims=True))
           - jnp.exp(jnp.sum(lf[2:3] * lf[3:4], keepdims=True)) + lambda_init)
    acc = acc_ref[...]
    on = acc / acc[:, _ONE_LANE:_ONE_LANE + 1]
    o = jnp.where(lane < D_V_DIM, on[:t] - lam * on[t:], 0.0)
    ms = jnp.sum(o * o, -1, keepdims=True) * (1.0 / D_V_DIM)
    o_ref[...] = (o * lax.rsqrt(ms + RMS_EPS) * gain_ref[...]).astype(BF)


def _flash_d(qkv, bias, lam, gain, lambda_init):
    b, s, _ = qkv.shape
    t = bias.shape[-1] // 2
    kt = 2 if s % (2 * t) == 0 else 1
    return pl.pallas_call(
        functools.partial(_flash_d_kernel, kt=kt, nk=s // (kt * t), lambda_init=lambda_init),
        grid=(b, D_HEADS, s // t),
        in_specs=[pl.BlockSpec((None, t, LANES), lambda bi, h, i: (bi, i, h // 2)),
                  pl.BlockSpec((None, s, LANES), lambda bi, h, i: (bi, 0, D_HEADS // 2 + h // 2)),
                  pl.BlockSpec((None, s, LANES), lambda bi, h, i: (bi, 0, D_HEADS + h)),
                  pl.BlockSpec((None, 5, 2 * t), lambda bi, h, i: (h, 0, 0)),
                  pl.BlockSpec((4, D_QK_DIM), lambda bi, h, i: (0, 0)),
                  pl.BlockSpec((1, LANES), lambda bi, h, i: (0, 0))],
        out_specs=pl.BlockSpec((None, t, LANES), lambda bi, h, i: (bi, i, h)),
        out_shape=jax.ShapeDtypeStruct((b, s, D_HEADS * LANES), BF),
        scratch_shapes=[pltpu.VMEM((2 * t, LANES), F32)] * 2 + [pltpu.VMEM((5, t, t), F32)],
        compiler_params=_params("parallel", "parallel", "arbitrary"),
        name="flash_d",
    )(qkv, qkv, qkv, bias, lam, gain)


def _cross_kernel(x_ref, kv_ref, wq_ref, wo_ref, g_ref, b_ref, wr_ref, y_ref, ybf_ref, aff_ref, *, parts):
    tp = x_ref.shape[0] // parts
    wr = wr_ref[...]
    wh = wr.astype(BF)
    wl = (wr - wh.astype(F32)).astype(BF)
    for part in range(parts):
        rows = slice(tp * part, tp * (part + 1))
        x = x_ref[rows, :]
        q = (_dot(x.astype(BF), wq_ref[...]) * (X_HEAD_DIM ** -0.5)).astype(BF)
        outs = []
        for h in range(X_HEADS):
            sl = slice(X_HEAD_DIM * h, X_HEAD_DIM * (h + 1))
            s = _dot_nt(q[:, sl], kv_ref[:, sl])
            p = jnp.exp(s - jnp.max(s, -1, keepdims=True))
            l = jnp.sum(p, -1, keepdims=True)
            vh = kv_ref[:, D_MODEL + X_HEAD_DIM * h:D_MODEL + X_HEAD_DIM * (h + 1)]
            outs.append((_dot(p.astype(BF), vh) / l).astype(BF))
        o = jnp.concatenate(outs, axis=1)
        y = _ln_rows(ALPHA * x + _dot(o, wo_ref[...]), g_ref[...], b_ref[...])
        y_ref[rows, :] = y
        yh = y.astype(BF)
        ybf_ref[rows, :] = yh
        yl = (y - yh.astype(F32)).astype(BF)
        lg = _dot_nt(wh, yh) + _dot_nt(wh, yl) + _dot_nt(wl, yh)
        e = jnp.exp(lg - jnp.max(lg, 0, keepdims=True))
        aff = e / jnp.sum(e, 0, keepdims=True)
        for c in range(tp // LANES):
            aff_ref[part * (tp // LANES) + c] = aff[:, LANES * c:LANES * (c + 1)]


def _cross(x, kv, wq, wo, g, b, wr_t, s, mem_len, tm=1024, parts=2):
    n = x.shape[0]
    tm = _tile(s, tm)
    per = s // tm
    fixed = lambda i: (0, 0)
    return pl.pallas_call(
        functools.partial(_cross_kernel, parts=parts),
        grid=(n // tm,),
        in_specs=[pl.BlockSpec((tm, D_MODEL), lambda i: (i, 0)),
                  pl.BlockSpec((mem_len, 2 * D_MODEL), lambda i: (i // per, 0)),
                  pl.BlockSpec((D_MODEL, D_MODEL), fixed), pl.BlockSpec((D_MODEL, D_MODEL), fixed),
                  pl.BlockSpec((1, D_MODEL), fixed), pl.BlockSpec((1, D_MODEL), fixed),
                  pl.BlockSpec((N_EXPERTS, D_MODEL), fixed)],
        out_specs=[pl.BlockSpec((tm, D_MODEL), lambda i: (i, 0)),
                   pl.BlockSpec((tm, D_MODEL), lambda i: (i, 0)),
                   pl.BlockSpec((tm // LANES, N_EXPERTS, LANES), lambda i: (i, 0, 0))],
        out_shape=[jax.ShapeDtypeStruct((n, D_MODEL), F32),
                   jax.ShapeDtypeStruct((n, D_MODEL), BF),
                   jax.ShapeDtypeStruct((n // LANES, N_EXPERTS, LANES), F32)],
        compiler_params=_params("parallel"),
        name="cross",
    )(x, kv, wq, wo, g, b, wr_t)


def _select_kernel(a_ref, pos_ref, st_ref, *, k, nbits):
    nt = a_ref.shape[0]
    shape = (nt, N_EXPERTS, LANES)
    kf = float(k)

    def keys():
        return lax.bitcast_convert_type(a_ref[...], I32)

    def count(mask):
        c = jnp.sum(jnp.where(mask, 1.0, 0.0), axis=0, keepdims=True)
        return jnp.sum(c, axis=2, keepdims=True)

    def value_step(it, thr):
        cand = thr | jnp.left_shift(jnp.int32(1), 30 - it)
        return jnp.where(count(keys() >= cand) >= kf, cand, thr)

    thr = lax.fori_loop(0, 31, value_step, jnp.zeros((1, N_EXPERTS, 1), I32))
    need = kf - count(keys() > thr)
    idx = lax.broadcasted_iota(I32, shape, 0) * LANES + lax.broadcasted_iota(I32, shape, 2)

    def index_step(it, ithr):
        cand = ithr | jnp.left_shift(jnp.int32(1), nbits - 1 - it)
        return jnp.where(count((keys() == thr) & (idx < cand)) < need, cand, ithr)

    ithr = lax.fori_loop(0, nbits, index_step, jnp.zeros((1, N_EXPERTS, 1), I32))
    thr2, ithr2 = thr[0], ithr[0]
    upper = jnp.where(lax.broadcasted_iota(I32, (LANES, LANES), 0) <= lax.broadcasted_iota(I32, (LANES, LANES), 1),
                      1.0, 0.0).astype(BF)
    lane = lax.broadcasted_iota(I32, (N_EXPERTS, LANES), 1)

    def tile_step(j, carry):
        kj = lax.bitcast_convert_type(a_ref[j], I32)
        sel = (kj > thr2) | ((kj == thr2) & (j * LANES + lane <= ithr2))
        m = jnp.where(sel, 1.0, 0.0)
        inc = _dot(m.astype(BF), upper)
        pos_ref[j] = jnp.where(sel, inc - m + carry, -1.0).astype(I32)
        st_ref[j] = jnp.broadcast_to(carry, (N_EXPERTS, LANES)).astype(I32)
        return carry + inc[:, LANES - 1:LANES]

    lax.fori_loop(0, nt, tile_step, jnp.zeros((N_EXPERTS, 1), F32))


def _select(aff3, k):
    nt = aff3.shape[0]
    nbits = max(1, int(math.ceil(math.log2(nt * LANES))))
    shp = jax.ShapeDtypeStruct(aff3.shape, I32)
    return pl.pallas_call(
        functools.partial(_select_kernel, k=k, nbits=nbits),
        out_shape=[shp, shp],
        compiler_params=pltpu.CompilerParams(vmem_limit_bytes=VMEM_LIMIT),
        name="select",
    )(aff3)


_GATHER_ROWS = LANES + BF16_ROWS
_GATHER_ROWS_SMALL = 3 * BF16_ROWS

_MOE_GROUP = 2


def _moe_ffn_kernel(st_ref, x_ref, pos_ref, aff_ref, wg_ref, wu_ref, wd_ref, ye_ref, buf_ref, gate_ref, *, nb, ncf,
                    nch, sub, ck):
    ep = buf_ref.shape[0]
    grp = pl.program_id(0)
    t = pl.program_id(1)

    @pl.when(t == 0)
    def _():
        buf_ref[...] = jnp.zeros_like(buf_ref)
        gate_ref[...] = jnp.zeros_like(gate_ref)

    @pl.when(t < nb)
    def _():
        def place(s, k, e, base, nrows):
            rows = pl.ds(base, nrows)
            hit = lax.broadcasted_iota(I32, (nrows, LANES), 0) == pos_ref[s, pl.ds(e, 1), :] - base
            xs = x_ref[LANES * s:LANES * (s + 1), :]
            buf_ref[k, rows, :] = buf_ref[k, rows, :] + _dot(jnp.where(hit, 1.0, 0.0).astype(BF), xs).astype(BF)
            gate_ref[k, rows, :] = gate_ref[k, rows, :] + jnp.sum(
                jnp.where(hit, aff_ref[s, pl.ds(e, 1), :], 0.0), -1, keepdims=True)

        def base_of(s, k):
            st = st_ref[grp * ep + k, t * sub + s]
            return pl.multiple_of((st // BF16_ROWS) * BF16_ROWS, BF16_ROWS)

        pairs = [(s, k) for s in range(sub) for k in range(ep)]
        few = functools.reduce(
            jnp.logical_and,
            [st_ref[grp * ep + k, t * sub + s + 1] - base_of(s, k) <= _GATHER_ROWS_SMALL for s, k in pairs])

        @pl.when(few)
        def _():
            for s, k in pairs:
                place(s, k, grp * ep + k, base_of(s, k), _GATHER_ROWS_SMALL)

        @pl.when(jnp.logical_not(few))
        def _():
            for s, k in pairs:
                place(s, k, grp * ep + k, base_of(s, k), _GATHER_ROWS)

    @pl.when(t >= nb)
    def _():
        k = (t - nb) // nch
        c = (t - nb) - k * nch

        @pl.when(c < ncf)
        def _():
            rows = pl.ds(pl.multiple_of(c * ck, ck), ck)
            xe = buf_ref[k, rows, :]
            h = (jax.nn.silu(_dot(xe, wg_ref[...])) * _dot(xe, wu_ref[...])).astype(BF)
            ye_ref[...] = (_dot(h, wd_ref[...]) * gate_ref[k, rows, :]).astype(BF)

        @pl.when(c >= ncf)
        def _():
            ye_ref[...] = jnp.zeros_like(ye_ref)


def _moe_ffn(starts, xbf, pos3, aff3, wg, wu, wd, cap, tb=1024):
    n = xbf.shape[0]
    tb = _tile(n, tb)
    nb = n // tb
    sub = tb // LANES
    ck = _tile(cap, 1024)
    ncf = cap // ck
    nch = ncf + pl.cdiv(_WIN_ROWS, ck)
    dff = wg.shape[-1]
    ep = _MOE_GROUP
    blk = lambda g, t, st: (jnp.minimum(t, nb - 1), 0)
    blk3 = lambda g, t, st: (jnp.minimum(t, nb - 1), 0, 0)
    expert = lambda g, t: g * ep + jnp.clip((t - nb) // nch, 0, ep - 1)
    wmap = lambda g, t, st: (expert(g, t), 0, 0)
    return pl.pallas_call(
        functools.partial(_moe_ffn_kernel, nb=nb, ncf=ncf, nch=nch, sub=sub, ck=ck),
        grid_spec=pltpu.PrefetchScalarGridSpec(
            num_scalar_prefetch=1,
            grid=(N_EXPERTS // ep, nb + ep * nch),
            in_specs=[pl.BlockSpec((tb, D_MODEL), blk),
                      pl.BlockSpec((sub, N_EXPERTS, LANES), blk3),
                      pl.BlockSpec((sub, N_EXPERTS, LANES), blk3),
                      pl.BlockSpec((None, D_MODEL, dff), wmap),
                      pl.BlockSpec((None, D_MODEL, dff), wmap),
                      pl.BlockSpec((None, dff, D_MODEL), wmap)],
            out_specs=pl.BlockSpec((None, ck, D_MODEL),
                                   lambda g, t, st: (expert(g, t), jnp.maximum(t - nb, 0) % nch, 0)),
            scratch_shapes=[pltpu.VMEM((ep, cap + _GATHER_ROWS, D_MODEL), BF),
                            pltpu.VMEM((ep, cap + _GATHER_ROWS, 1), F32)]),
        out_shape=jax.ShapeDtypeStruct((N_EXPERTS, nch * ck, D_MODEL), BF),
        compiler_params=_params("arbitrary", "arbitrary"),
        name="moe_ffn",
    )(starts, xbf, pos3, aff3, wg, wu, wd)


_WIN_STEP = 128
_COMB_TILES = 2
_WIN_ROWS = _WIN_STEP + (_COMB_TILES - 1) * LANES + _GATHER_ROWS


_COMB_PACK = MXU_DEPTH // _GATHER_ROWS_SMALL


def _combine_kernel(st_ref, x_ref, pos_ref, g_ref, b_ref, *rest):
    ye_refs, y_ref = rest[:N_EXPERTS], rest[N_EXPERTS]
    j = pl.program_id(0)
    pairs = [(u, e) for u in range(_COMB_TILES) for e in range(N_EXPERTS)]

    def base_of(u, e):
        return (st_ref[e, j * _COMB_TILES + u] // BF16_ROWS) * BF16_ROWS

    def rows_of(u, e, nrows):
        win = (st_ref[e, j * _COMB_TILES] // _WIN_STEP) * _WIN_STEP
        return ye_refs[e][0, pl.ds(pl.multiple_of(base_of(u, e) - win, BF16_ROWS), nrows), :]

    def finish(u, acc):
        tok = slice(LANES * u, LANES * (u + 1))
        y_ref[tok, :] = _ln_rows(ALPHA * x_ref[tok, :] + acc, g_ref[...], b_ref[...])

    few = functools.reduce(
        jnp.logical_and,
        [st_ref[e, j * _COMB_TILES + u + 1] - base_of(u, e) <= _GATHER_ROWS_SMALL for u, e in pairs])

    @pl.when(few)
    def _():
        for u in range(_COMB_TILES):
            tok = slice(LANES * u, LANES * (u + 1))
            acc = None
            for e0 in range(0, N_EXPERTS, _COMB_PACK):
                es = list(range(e0, min(e0 + _COMB_PACK, N_EXPERTS)))
                col = lax.broadcasted_iota(I32, (LANES, len(es) * _GATHER_ROWS_SMALL), 1)
                hit = None
                for k, e in enumerate(es):
                    rel = pos_ref[tok, e:e + 1] - base_of(u, e)
                    h = col == jnp.where(rel >= 0, rel + k * _GATHER_ROWS_SMALL, -1)
                    hit = h if hit is None else hit | h
                slab = jnp.concatenate([rows_of(u, e, _GATHER_ROWS_SMALL) for e in es], axis=0)
                d = _dot(jnp.where(hit, 1.0, 0.0).astype(BF), slab)
                acc = d if acc is None else acc + d
            finish(u, acc)

    @pl.when(jnp.logical_not(few))
    def _():
        col = lax.broadcasted_iota(I32, (LANES, _GATHER_ROWS), 1)
        for u in range(_COMB_TILES):
            tok = slice(LANES * u, LANES * (u + 1))
            acc = None
            for e in range(N_EXPERTS):
                onehot = jnp.where(col == pos_ref[tok, e:e + 1] - base_of(u, e), 1.0, 0.0).astype(BF)
                d = _dot(onehot, rows_of(u, e, _GATHER_ROWS))
                acc = d if acc is None else acc + d
            finish(u, acc)


def _combine(starts, x, pos, g, b, ye):
    n = x.shape[0]
    assert ye.shape[1] >= EC_CAPACITY * n // N_EXPERTS + _WIN_ROWS
    tt = _COMB_TILES * LANES
    row = lambda j, st: (j, 0)
    fixed = lambda j, st: (0, 0)

    def ye_spec(e):
        return pl.BlockSpec((pl.Element(1), pl.Element(_WIN_ROWS), pl.Element(D_MODEL)),
                            lambda j, st: (e, (st[e, j * _COMB_TILES] // _WIN_STEP) * _WIN_STEP, 0))

    return pl.pallas_call(
        _combine_kernel,
        grid_spec=pltpu.PrefetchScalarGridSpec(
            num_scalar_prefetch=1,
            grid=(n // tt,),
            in_specs=[pl.BlockSpec((tt, D_MODEL), row),
                      pl.BlockSpec((tt, N_EXPERTS), row),
                      pl.BlockSpec((1, D_MODEL), fixed), pl.BlockSpec((1, D_MODEL), fixed)]
            + [ye_spec(e) for e in range(N_EXPERTS)],
            out_specs=pl.BlockSpec((tt, D_MODEL), row)),
        out_shape=jax.ShapeDtypeStruct((n, D_MODEL), F32),
        compiler_params=_params("arbitrary"),
        name="combine",
    )(starts, x, pos, g, b, *([ye] * N_EXPERTS))


def _moe(x, xg, aff3, wg, wu, wd, g, b):
    n = x.shape[0]
    cap = EC_CAPACITY * n // N_EXPERTS
    pos3, st3 = _select(aff3, cap)
    starts = jnp.concatenate([jnp.transpose(st3[:, :, 0]), jnp.full((N_EXPERTS, 1), cap, I32)], axis=1)
    ye = _moe_ffn(starts, xg, pos3, aff3, wg, wu, wd, cap)
    pos = jnp.transpose(pos3, (0, 2, 1)).reshape(n, N_EXPERTS)
    return _combine(starts, x, pos, g, b, ye)


def _prep_weights(p):
    w = {}
    w['a'] = [_prep_a(p['a_w_qkv'][j], p['a_q_gain'][j], p['a_k_gain'][j], p['a_w_o'][j])
              for j in range(p['a_w_qkv'].shape[0])]
    w['b'] = [_prep_b(p['b_w_qkv'][j], p['b_w_o'][j]) for j in range(p['b_w_qkv'].shape[0])]
    w['c'] = [_prep_c(p['c_lam_re'][j], p['c_lam_im'][j], p['c_log_dt'][j], p['c_b_re'][j], p['c_b_im'][j],
                      p['c_c_re'][j], p['c_c_im'][j]) + (p['c_d'][j][None, :], p['c_w_glu'][j].astype(BF))
              for j in range(p['c_lam_re'].shape[0])]
    w['d'] = []
    for j in range(p['d_w_qkv'].shape[0]):
        layer = N_MIXERS * j + 3
        lambda_init = 0.8 - 0.6 * math.exp(-0.3 * layer)
        w['d'].append(_prep_d(p['d_w_qkv'][j], p['d_norm_gain'][j], p['d_w_o'][j], lambda_init)
                      + (p['d_lam'][j].astype(F32), lambda_init))
    w['diff_bias'] = _diff_bias(p['rel_bias'], _D_T)
    w['x_w_q'] = p['x_w_q'].astype(BF)
    w['x_w_kv'] = p['x_w_kv'].astype(BF)
    w['x_w_o'] = p['x_w_o'].astype(BF)
    w['router_t'] = jnp.transpose(p['moe_w_router'], (0, 2, 1)).astype(F32)
    w['moe_w_gate'] = p['moe_w_gate'].astype(BF)
    w['moe_w_up'] = p['moe_w_up'].astype(BF)
    w['moe_w_down'] = p['moe_w_down'].astype(BF)
    return w


def _trunk(x, mem, p, w):
    b, s, _ = x.shape
    n = b * s
    mem_len = mem.shape[1]
    x = x.reshape(n, D_MODEL)
    mem2 = mem.reshape(b * mem_len, D_MODEL)
    ln_g, ln_b = p['ln_g'], p['ln_b']
    for i in range(DEPTH):
        m, j = i % N_MIXERS, i // N_MIXERS
        g0, b0 = ln_g[i, 0][None, :], ln_b[i, 0][None, :]
        if m == 0:
            wa, gains, wo = w['a'][j]
            qkv = _proj_a(x, wa, _rope_table(s), gains, s)
            o = _flash_a(qkv.reshape(b, s, _A_COLS))
            x = _post(x, o.reshape(n, A_HEADS * A_HEAD_DIM), wo, g0, b0)
        elif m == 1:
            wb, wo = w['b'][j]
            qkv = _proj(x, wb).reshape(b, s, wb.shape[1])
            os_, ls_ = [], []
            for g in range(len(B_PATTERNS)):
                t = _tile(s, _B_T)
                o, lse = _band_attention(qkv, _band_bias(p['rel_bias'], g, t), g, t)
                os_.append(o)
                ls_.append(lse)
            x = _post_b(x, os_, ls_, wo, g0, b0)
        elif m == 2:
            w_sum, w_lag, w_state, dec, dskip, wglu = w['c'][j]
            ys = _s5(x, b, s, w_sum, w_lag, w_state, dec)
            x = _post_c(x, ys, dskip, wglu, g0, b0)
        else:
            wd, gain, wo, lam, lambda_init = w['d'][j]
            qkv = _proj(x, wd, ones_from=D_HEADS * LANES).reshape(b, s, wd.shape[1])
            o = _flash_d(qkv, w['diff_bias'], lam, gain, lambda_init)
            x = _post(x, o.reshape(n, D_HEADS * LANES), wo, g0, b0)
        kv = _proj(mem2, w['x_w_kv'][i], tm=mem_len)
        x, xbf, aff3 = _cross(x, kv, w['x_w_q'][i], w['x_w_o'][i], ln_g[i, 1][None, :], ln_b[i, 1][None, :],
                              w['router_t'][i], s, mem_len)
        x = _moe(x, xbf, aff3, w['moe_w_gate'][i], w['moe_w_up'][i], w['moe_w_down'][i],
                 ln_g[i, 2][None, :], ln_b[i, 2][None, :])
    return x.reshape(b, s, D_MODEL)


def kernel(x_prompt, x_sample, mem_prompt, mem_sample, rel_bias, ln_g, ln_b, a_w_qkv, a_q_gain, a_k_gain, a_w_o, b_w_qkv, b_w_o, c_lam_re, c_lam_im, c_log_dt, c_b_re, c_b_im, c_c_re, c_c_im, c_d, c_w_glu, d_w_qkv, d_lam, d_norm_gain, d_w_o, x_w_q, x_w_kv, x_w_o, moe_w_router, moe_w_gate, moe_w_up, moe_w_down):
    p = dict(rel_bias=rel_bias, ln_g=ln_g, ln_b=ln_b,
             a_w_qkv=a_w_qkv, a_q_gain=a_q_gain, a_k_gain=a_k_gain, a_w_o=a_w_o,
             b_w_qkv=b_w_qkv, b_w_o=b_w_o,
             c_lam_re=c_lam_re, c_lam_im=c_lam_im, c_log_dt=c_log_dt, c_b_re=c_b_re, c_b_im=c_b_im,
             c_c_re=c_c_re, c_c_im=c_c_im, c_d=c_d, c_w_glu=c_w_glu,
             d_w_qkv=d_w_qkv, d_lam=d_lam, d_norm_gain=d_norm_gain, d_w_o=d_w_o,
             x_w_q=x_w_q, x_w_kv=x_w_kv, x_w_o=x_w_o,
             moe_w_router=moe_w_router, moe_w_gate=moe_w_gate, moe_w_up=moe_w_up, moe_w_down=moe_w_down)
    w = _prep_weights(p)
    return (_trunk(x_prompt, mem_prompt, p, w), _trunk(x_sample, mem_sample, p, w))
```

```python
import functools
import math

import numpy as np
import jax
import jax.numpy as jnp
from jax import lax
from jax.experimental import pallas as pl
from jax.experimental.pallas import tpu as pltpu

F32 = jnp.float32
BF = jnp.bfloat16
I32 = jnp.int32

D_MODEL = 1024
DEPTH = 4
GRID_W = 64
N_MIXERS = 4
LN_EPS = 1e-5
RMS_EPS = 1e-6
ALPHA = (2.0 * DEPTH) ** 0.25

A_HEADS = 16
A_KV_HEADS = 4
A_HEAD_DIM = 64
ROPE_BASE = 10000.0

B_PATTERNS = ((128, 1), (512, 4), (2048, 16))
B_HEADS_PER_GROUP = 4
B_HEAD_DIM = 64

C_GROUP = 16
C_N_GROUPS = D_MODEL // C_GROUP
C_STATE = 64
C_CHUNK = 16
D_HEADS = 12
D_QK_DIM = 32
D_V_DIM = 64

REL_BUCKETS = 32
REL_MAX_DIST = 128

X_HEADS = 4
X_HEAD_DIM = D_MODEL // X_HEADS

N_EXPERTS = 16
EC_CAPACITY = 2

LANES = 128
BF16_ROWS = 16
MXU_DEPTH = 256
_C_GL = LANES // C_GROUP
VMEM_LIMIT = 56 * 1024 * 1024
NEG = -1e30
LOG2E = math.log2(math.e)


def _params(*sem):
    return pltpu.CompilerParams(dimension_semantics=sem, vmem_limit_bytes=VMEM_LIMIT)


def _tile(n, pref):
    t = min(n, pref)
    assert n % t == 0, (n, pref)
    return t


def _ln_rows(v, g, b):
    mu = jnp.mean(v, -1, keepdims=True)
    c = v - mu
    var = jnp.mean(c * c, -1, keepdims=True)
    return c * lax.rsqrt(var + LN_EPS) * g + b


def _dot_nt(a, b):
    return lax.dot_general(a, b, (((1,), (1,)), ((), ())), preferred_element_type=F32)


def _dot(a, b):
    return jnp.dot(a, b, preferred_element_type=F32)


_ONE_LANE = 64


def _with_ones_lane(y):
    lane = lax.broadcasted_iota(I32, y.shape, 1)
    return jnp.where(lane % LANES == _ONE_LANE, 1.0, y)


def _proj_kernel(x_ref, w_ref, o_ref, *, ones_from):
    y = _dot(x_ref[...].astype(BF), w_ref[...])
    if ones_from is None:
        o_ref[...] = y.astype(o_ref.dtype)
    else:
        o_ref[:, :ones_from] = y[:, :ones_from].astype(o_ref.dtype)
        o_ref[:, ones_from:] = _with_ones_lane(y[:, ones_from:]).astype(o_ref.dtype)


def _proj(x, w, tm=512, ones_from=None):
    n, k = x.shape
    m = w.shape[1]
    tm = _tile(n, tm)
    return pl.pallas_call(
        functools.partial(_proj_kernel, ones_from=ones_from),
        grid=(n // tm,),
        in_specs=[pl.BlockSpec((tm, k), lambda i: (i, 0)),
                  pl.BlockSpec((k, m), lambda i: (0, 0))],
        out_specs=pl.BlockSpec((tm, m), lambda i: (i, 0)),
        out_shape=jax.ShapeDtypeStruct((n, m), BF),
        compiler_params=_params("parallel"),
        name="proj",
    )(x, w)


def _post_kernel(x_ref, o_ref, w_ref, g_ref, b_ref, y_ref):
    h = _dot(o_ref[...], w_ref[...])
    y_ref[...] = _ln_rows(ALPHA * x_ref[...] + h, g_ref[...], b_ref[...])


def _post(x, o, w, g, b, tm=512):
    n = x.shape[0]
    ko = o.shape[1]
    tm = _tile(n, tm)
    return pl.pallas_call(
        _post_kernel,
        grid=(n // tm,),
        in_specs=[pl.BlockSpec((tm, D_MODEL), lambda i: (i, 0)),
                  pl.BlockSpec((tm, ko), lambda i: (i, 0)),
                  pl.BlockSpec((ko, D_MODEL), lambda i: (0, 0)),
                  pl.BlockSpec((1, D_MODEL), lambda i: (0, 0)),
                  pl.BlockSpec((1, D_MODEL), lambda i: (0, 0))],
        out_specs=pl.BlockSpec((tm, D_MODEL), lambda i: (i, 0)),
        out_shape=jax.ShapeDtypeStruct((n, D_MODEL), F32),
        compiler_params=_params("parallel"),
        name="post",
    )(x, o, w, g, b)


def _post_b_kernel(x_ref, o0_ref, o1_ref, o2_ref, l0_ref, l1_ref, l2_ref, w_ref, g_ref, b_ref, y_ref):
    l0, l1, l2 = l0_ref[...], l1_ref[...], l2_ref[...]
    m = jnp.maximum(jnp.maximum(l0, l1), l2)
    e0, e1, e2 = jnp.exp(l0 - m), jnp.exp(l1 - m), jnp.exp(l2 - m)
    inv = 1.0 / (e0 + e1 + e2)
    gw = B_HEADS_PER_GROUP * B_HEAD_DIM
    h = _dot((o0_ref[...].astype(F32) * (e0 * inv)).astype(BF), w_ref[0:gw, :])
    h = h + _dot((o1_ref[...].astype(F32) * (e1 * inv)).astype(BF), w_ref[gw:2 * gw, :])
    h = h + _dot((o2_ref[...].astype(F32) * (e2 * inv)).astype(BF), w_ref[2 * gw:3 * gw, :])
    y_ref[...] = _ln_rows(ALPHA * x_ref[...] + h, g_ref[...], b_ref[...])


def _post_b(x, os_, ls_, w, g, b, tm=512):
    n = x.shape[0]
    tm = _tile(n, tm)
    gw = B_HEADS_PER_GROUP * B_HEAD_DIM
    row = lambda i: (i, 0)
    fixed = lambda i: (0, 0)
    return pl.pallas_call(
        _post_b_kernel,
        grid=(n // tm,),
        in_specs=[pl.BlockSpec((tm, D_MODEL), row)] + [pl.BlockSpec((tm, gw), row)] * 6
        + [pl.BlockSpec((3 * gw, D_MODEL), fixed), pl.BlockSpec((1, D_MODEL), fixed),
           pl.BlockSpec((1, D_MODEL), fixed)],
        out_specs=pl.BlockSpec((tm, D_MODEL), row),
        out_shape=jax.ShapeDtypeStruct((n, D_MODEL), F32),
        compiler_params=_params("parallel"),
        name="post_b",
    )(x, *os_, *ls_, w, g, b)


def _post_c_kernel(x_ref, ys_ref, d_ref, w_ref, g_ref, b_ref, y_ref):
    x = x_ref[...]
    z = jax.nn.gelu(ys_ref[...] + d_ref[...] * x).astype(BF)
    h = _dot(z, w_ref[...])
    hh = h[:, :D_MODEL] * jax.nn.sigmoid(h[:, D_MODEL:])
    y_ref[...] = _ln_rows(ALPHA * x + hh, g_ref[...], b_ref[...])


def _post_c(x, ys, d, w, g, b, tm=512):
    n = x.shape[0]
    tm = _tile(n, tm)
    row = lambda i: (i, 0)
    fixed = lambda i: (0, 0)
    return pl.pallas_call(
        _post_c_kernel,
        grid=(n // tm,),
        in_specs=[pl.BlockSpec((tm, D_MODEL), row), pl.BlockSpec((tm, D_MODEL), row),
                  pl.BlockSpec((1, D_MODEL), fixed), pl.BlockSpec((D_MODEL, 2 * D_MODEL), fixed),
                  pl.BlockSpec((1, D_MODEL), fixed), pl.BlockSpec((1, D_MODEL), fixed)],
        out_specs=pl.BlockSpec((tm, D_MODEL), row),
        out_shape=jax.ShapeDtypeStruct((n, D_MODEL), F32),
        compiler_params=_params("parallel"),
        name="post_c",
    )(x, ys, d, w, g, b)


_A_QK_TILES = A_HEADS + A_KV_HEADS
_A_COLS = (_A_QK_TILES + A_KV_HEADS) * LANES


def _rope_partner():
    d = np.arange(A_HEAD_DIM)
    e = d % (A_HEAD_DIM // 2)
    lo = e < A_HEAD_DIM // 4
    return np.where(lo, d + A_HEAD_DIM // 4, d - A_HEAD_DIM // 4), np.where(lo, -1.0, 1.0).astype(np.float32)


def _prep_a(w_qkv, q_gain, k_gain, w_o):
    partner, sign = _rope_partner()
    nqk = _A_QK_TILES * A_HEAD_DIM
    wqk = w_qkv[:, :nqk].reshape(D_MODEL, _A_QK_TILES, A_HEAD_DIM)
    wsw = wqk[:, :, partner] * sign
    wqk = jnp.concatenate([wqk, wsw], -1).reshape(D_MODEL, _A_QK_TILES * LANES)
    wv = w_qkv[:, nqk:].reshape(D_MODEL, A_KV_HEADS, A_HEAD_DIM)
    wv = jnp.concatenate([wv, jnp.zeros_like(wv)], -1).reshape(D_MODEL, A_KV_HEADS * LANES)
    w = jnp.concatenate([wqk, wv], 1).astype(BF)
    gq = jnp.concatenate([q_gain, q_gain[partner]]) * (A_HEAD_DIM ** -0.5 * 0.5 * LOG2E)
    gk = jnp.concatenate([k_gain, k_gain[partner]])
    gains = jnp.stack([gq, gk], 0)
    return w, gains, w_o.astype(BF)


def _rope_table(s):
    pos = jnp.arange(s)
    rows, cols = (pos // GRID_W).astype(F32), (pos % GRID_W).astype(F32)
    half = A_HEAD_DIM // 2
    freqs = ROPE_BASE ** (-jnp.arange(0, half, 2, dtype=F32) / half)
    ang_r = rows[:, None] * freqs
    ang_c = cols[:, None] * freqs
    ang = jnp.concatenate([ang_r, ang_r, ang_c, ang_c], -1)
    return jnp.concatenate([jnp.cos(ang), jnp.sin(ang)], -1)


def _proj_a_kernel(x_ref, w_ref, cs_ref, g_ref, o_ref):
    xb = x_ref[...].astype(BF)
    cs = cs_ref[...]
    gq = cs * g_ref[0:1, :]
    gk = cs * g_ref[1:2, :]
    per = 4
    for g in range(_A_QK_TILES // per):
        y = _dot(xb, w_ref[:, LANES * per * g:LANES * per * (g + 1)])
        for hh in range(per):
            h = per * g + hh
            t = y[:, LANES * hh:LANES * (hh + 1)]
            r = lax.rsqrt(jnp.sum(t * t, -1, keepdims=True) * (1.0 / LANES) + RMS_EPS)
            e = t * r * (gq if h < A_HEADS else gk)
            o_ref[:, LANES * h:LANES * (h + 1)] = (e + pltpu.roll(e, LANES // 2, 1)).astype(BF)
    yv = _dot(xb, w_ref[:, _A_QK_TILES * LANES:])
    o_ref[:, _A_QK_TILES * LANES:] = _with_ones_lane(yv).astype(BF)


def _proj_a(x, w, cs, gains, s, tm=512):
    n = x.shape[0]
    tm = _tile(s, tm)
    per = s // tm
    return pl.pallas_call(
        _proj_a_kernel,
        grid=(n // tm,),
        in_specs=[pl.BlockSpec((tm, D_MODEL), lambda i: (i, 0)),
                  pl.BlockSpec((D_MODEL, _A_COLS), lambda i: (0, 0)),
                  pl.BlockSpec((tm, LANES), lambda i: (i % per, 0)),
                  pl.BlockSpec((2, LANES), lambda i: (0, 0))],
        out_specs=pl.BlockSpec((tm, _A_COLS), lambda i: (i, 0)),
        out_shape=jax.ShapeDtypeStruct((n, _A_COLS), BF),
        compiler_params=_params("parallel"),
        name="proj_a",
    )(x, w, cs, gains)


def _softmax_step(s, v, m_ref, acc_ref):
    tk = s.shape[1]
    m_old = m_ref[...]
    m_new = jnp.maximum(m_old, jnp.max(s, -1, keepdims=True))
    p = jnp.concatenate([jnp.exp2(s[:, LANES * c:LANES * (c + 1)] - m_new).astype(BF) for c in range(tk // LANES)],
                        axis=1)
    acc_ref[...] = jnp.exp2(m_old - m_new) * acc_ref[...] + _dot(p, v)
    m_ref[...] = m_new


def _softmax_init(m_ref, acc_ref):
    m_ref[...] = jnp.full(m_ref.shape, -jnp.inf, F32)
    acc_ref[...] = jnp.zeros_like(acc_ref)


def _flash_a_kernel(q_ref, k_ref, v_ref, o_ref, m_ref, acc_ref, *, tk, nk):
    tq = q_ref.shape[0]
    rep = A_HEADS // A_KV_HEADS
    q = jnp.concatenate([q_ref[:, LANES * r:LANES * (r + 1)] for r in range(rep)], axis=0)
    _softmax_init(m_ref, acc_ref)

    def body(j, carry):
        off = pl.multiple_of(j * tk, tk)
        _softmax_step(_dot_nt(q, k_ref[pl.ds(off, tk), :]), v_ref[pl.ds(off, tk), :], m_ref, acc_ref)
        return carry

    lax.fori_loop(0, nk, body, 0, unroll=4)
    acc = acc_ref[...]
    o = acc / acc[:, _ONE_LANE:_ONE_LANE + 1]
    first = lax.broadcasted_iota(I32, (tq, LANES), 1) < A_HEAD_DIM
    for p in range(rep // 2):
        lo = o[2 * p * tq:(2 * p + 1) * tq]
        hi = pltpu.roll(o[(2 * p + 1) * tq:(2 * p + 2) * tq], A_HEAD_DIM, 1)
        o_ref[:, LANES * p:LANES * (p + 1)] = jnp.where(first, lo, hi).astype(BF)


def _flash_a(qkv, tq=256, tk=2048):
    b, s, _ = qkv.shape
    tq, tk = _tile(s, tq), _tile(s, tk)
    rep = A_HEADS // A_KV_HEADS
    gw = rep * LANES
    return pl.pallas_call(
        functools.partial(_flash_a_kernel, tk=tk, nk=s // tk),
        grid=(b, A_KV_HEADS, s // tq),
        in_specs=[pl.BlockSpec((None, tq, gw), lambda bi, g, i: (bi, i, g)),
                  pl.BlockSpec((None, s, LANES), lambda bi, g, i: (bi, 0, A_HEADS + g)),
                  pl.BlockSpec((None, s, LANES), lambda bi, g, i: (bi, 0, _A_QK_TILES + g))],
        out_specs=pl.BlockSpec((None, tq, rep * A_HEAD_DIM), lambda bi, g, i: (bi, i, g)),
        out_shape=jax.ShapeDtypeStruct((b, s, A_HEADS * A_HEAD_DIM), BF),
        scratch_shapes=[pltpu.VMEM((rep * tq, LANES), F32)] * 2,
        compiler_params=_params("parallel", "parallel", "arbitrary"),
        name="flash_a",
    )(qkv, qkv, qkv)


def _bucket(rel):
    half = REL_BUCKETS // 2
    max_exact = half // 2
    n = jnp.abs(rel)
    large = max_exact + (jnp.log(jnp.maximum(n, 1).astype(F32) / max_exact)
                         / math.log(REL_MAX_DIST / max_exact) * (half - max_exact)).astype(I32)
    large = jnp.minimum(large, half - 1)
    return jnp.where(rel > 0, half, 0) + jnp.where(n < max_exact, n, large)


_B_GW = B_HEADS_PER_GROUP * B_HEAD_DIM
_B_T = 256


def _pad_heads(w, heads, dim):
    w = w.reshape(w.shape[0], heads, dim)
    return jnp.concatenate([w, jnp.zeros((w.shape[0], heads, LANES - dim), w.dtype)], -1).reshape(
        w.shape[0], heads * LANES)


def _prep_b(w_qkv, w_o):
    c = len(B_PATTERNS) * B_HEADS_PER_GROUP * B_HEAD_DIM
    w = jnp.concatenate([w_qkv[:, :c] * (B_HEAD_DIM ** -0.5), w_qkv[:, c:]], 1).astype(BF)
    return w, w_o.astype(BF)


def _toeplitz(vec, t):
    flat = jnp.tile(vec, (1,) * (vec.ndim - 1) + (t,))[..., :t * (2 * t - 1)]
    return flat.reshape(vec.shape[:-1] + (t, 2 * t - 1))[..., :t]


def _band_tiles(g, t):
    window, dil = B_PATTERNS[g]
    return pl.cdiv((window // (2 * dil)) * dil, t)


def _band_bias(rel_bias, g, t):
    window, dil = B_PATTERNS[g]
    reach = (window // (2 * dil)) * dil
    n = _band_tiles(g, t)
    rel0 = (jnp.arange(2 * t) + t) % (2 * t) - t
    rel = jnp.arange(-n, n + 1)[:, None] * t + rel0[None, :]
    bias = rel_bias[_bucket(rel)][:, :, g * B_HEADS_PER_GROUP:(g + 1) * B_HEADS_PER_GROUP]
    bias = jnp.where(((rel % dil == 0) & (jnp.abs(rel) <= reach))[:, :, None], bias, NEG)
    return _toeplitz(jnp.transpose(bias, (2, 0, 1)).astype(F32), t)


def _band_kernel(q_ref, k_ref, v_ref, bias_ref, o_ref, lse_ref, *, n, nk):
    t = q_ref.shape[0]
    i = pl.program_id(2)
    qf = q_ref[...].astype(F32)
    lane = lax.broadcasted_iota(I32, (t, LANES), 1)
    offs = [pl.multiple_of(jnp.clip(i + o, 0, nk - 1) * t, t) for o in range(-n, n + 1)]
    inside = [(i + o >= 0) & (i + o < nk) for o in range(-n, n + 1)]
    outs, lses = [], []
    for hh in range(2):
        q = jnp.where((lane >= B_HEAD_DIM * hh) & (lane < B_HEAD_DIM * (hh + 1)), qf, 0.0).astype(BF)
        logits = [jnp.where(ok, _dot_nt(q, k_ref[pl.ds(off, t), :]) + bias_ref[hh, c], NEG)
                  for c, (off, ok) in enumerate(zip(offs, inside))]
        m = logits[0].max(-1, keepdims=True)
        for s in logits[1:]:
            m = jnp.maximum(m, s.max(-1, keepdims=True))
        l = jnp.zeros((t, 1), F32)
        acc = jnp.zeros((t, LANES), F32)
        for s, off in zip(logits, offs):
            p = jnp.exp(s - m)
            l = l + jnp.sum(p, -1, keepdims=True)
            acc = acc + _dot(p.astype(BF), v_ref[pl.ds(off, t), :])
        outs.append(acc / l)
        lses.append(jnp.broadcast_to(m + jnp.log(l), (t, LANES)))
    first = lane < B_HEAD_DIM
    o_ref[...] = jnp.where(first, outs[0], outs[1]).astype(BF)
    lse_ref[...] = jnp.where(first, lses[0], lses[1])


def _band_attention(qkv, bias, g, t):
    b, s, c = qkv.shape
    third = c // 3 // LANES
    pairs = B_HEADS_PER_GROUP // 2
    n = _band_tiles(g, t)
    spec = lambda base: pl.BlockSpec((None, s, LANES), lambda bi, hp, i: (bi, 0, base + g * pairs + hp))
    out_spec = pl.BlockSpec((None, t, LANES), lambda bi, hp, i: (bi, i, hp))
    o, lse = pl.pallas_call(
        functools.partial(_band_kernel, n=n, nk=s // t),
        grid=(b, pairs, s // t),
        in_specs=[pl.BlockSpec((None, t, LANES), lambda bi, hp, i: (bi, i, g * pairs + hp)),
                  spec(third), spec(2 * third),
                  pl.BlockSpec((2, 2 * n + 1, t, t), lambda bi, hp, i: (hp, 0, 0, 0))],
        out_specs=[out_spec, out_spec],
        out_shape=[jax.ShapeDtypeStruct((b, s, _B_GW), BF), jax.ShapeDtypeStruct((b, s, _B_GW), F32)],
        compiler_params=_params("parallel", "parallel", "arbitrary"),
        name="band_%d" % g,
    )(qkv, qkv, qkv, bias)
    return o.reshape(b * s, _B_GW), lse.reshape(b * s, _B_GW)


def _prep_c(lam_re, lam_im, log_dt, b_re, b_im, c_re, c_im):
    hp = lax.Precision.HIGHEST
    L, P, C = C_CHUNK, C_STATE, C_GROUP
    lr, li = lam_re.astype(F32), lam_im.astype(F32)
    dt = jnp.exp(log_dt.astype(F32))[..., None]
    mag = jnp.exp(lr * dt)
    ar, ai = mag * jnp.cos(li * dt), mag * jnp.sin(li * dt)
    den = lr * lr + li * li
    zr = ((ar - 1.0) * lr + ai * li) / den
    zi = (ai * lr - (ar - 1.0) * li) / den
    br, bi = b_re.astype(F32), b_im.astype(F32)
    bbr = zr[..., None] * br - zi[..., None] * bi
    bbi = zr[..., None] * bi + zi[..., None] * br
    cr, ci = c_re.astype(F32), c_im.astype(F32)
    prs, pis = [jnp.ones_like(ar)], [jnp.zeros_like(ai)]
    for _ in range(L):
        pr_, pi_ = prs[-1], pis[-1]
        prs.append(ar * pr_ - ai * pi_)
        pis.append(ar * pi_ + ai * pr_)
    pr, pi = jnp.stack(prs, 0), jnp.stack(pis, 0)

    def lag(pr_k, pi_k):
        tr = pr_k[..., None] * bbr - pi_k[..., None] * bbi
        ti = pr_k[..., None] * bbi + pi_k[..., None] * bbr
        return (jnp.einsum('dgop,kdgpi->kdgoi', cr, tr, precision=hp)
                - jnp.einsum('dgop,kdgpi->kdgoi', ci, ti, precision=hp))

    kern = lag(pr[:L], pi[:L])
    lags = jnp.concatenate([kern[:0:-1, 1], (kern[0, 0] + kern[0, 1])[None], kern[1:, 0]], 0)

    def summ(d, powers):
        pr_k, pi_k = pr[powers, d], pi[powers, d]
        sr = pr_k[..., None] * bbr[d] - pi_k[..., None] * bbi[d]
        si = pr_k[..., None] * bbi[d] + pi_k[..., None] * bbr[d]
        s = jnp.concatenate([sr, si], 2)
        return jnp.transpose(s, (1, 0, 3, 2)).reshape(C_N_GROUPS, L * C, 2 * P)

    w_sum = jnp.concatenate([summ(0, np.arange(L - 1, -1, -1)), summ(1, np.arange(L))], -1)

    def outw(d, powers):
        pr_k, pi_k = pr[powers, d], pi[powers, d]
        wr = cr[d][None] * pr_k[:, :, None, :] - ci[d][None] * pi_k[:, :, None, :]
        wi = -(cr[d][None] * pi_k[:, :, None, :] + ci[d][None] * pr_k[:, :, None, :])
        w = jnp.concatenate([wr, wi], -1)
        return jnp.transpose(w, (1, 3, 0, 2)).reshape(C_N_GROUPS, 2 * P, L * C)

    w_state = jnp.concatenate([outw(0, np.arange(1, L + 1)), outw(1, np.arange(L, 0, -1))], 1)

    nq = C_N_GROUPS // _C_GL
    npair = _C_GL // 2
    blk = L * _C_GL * C
    lane = jnp.arange(LANES)
    gsel = (lane[None, :] // C == jnp.arange(_C_GL)[:, None])
    src = jnp.swapaxes(lags, -1, -2).astype(BF).reshape(2 * L - 1, nq, _C_GL, C, C)
    src = jnp.where(gsel[None, None, :, None, :], jnp.tile(src, (1, 1, 1, 1, _C_GL)), 0)
    w_lag = jnp.transpose(src.reshape(2 * L - 1, nq, LANES, LANES), (1, 0, 2, 3))
    psel = (lane[None, :] // P == jnp.arange(2)[:, None])
    ksel = jnp.eye(npair, dtype=bool)
    src = jnp.tile(w_sum.astype(BF).reshape(nq, npair, 2, L, C, 1, 4, P), (1, 1, 1, 1, 1, npair, 1, 2))
    src = jnp.where(psel[None, None, :, None, None, None, None, :]
                    & ksel[None, :, None, None, None, :, None, None], src, 0)
    w_sum = jnp.transpose(src, (0, 3, 1, 2, 4, 5, 6, 7)).reshape(nq, blk, _C_GL * 4 * P)
    osel = (lane[None, None, :] // C
            == (2 * jnp.arange(npair)[:, None, None] + jnp.arange(2)[None, :, None]))
    src = jnp.tile(w_state.astype(BF).reshape(nq, npair, 2, 4, P, L, C), (1, 1, 1, 1, 1, 1, _C_GL))
    src = jnp.where(osel[None, :, :, None, None, None, :], src, 0)
    w_state = jnp.transpose(src, (0, 1, 3, 2, 4, 5, 6)).reshape(nq, _C_GL * 4 * P, blk)
    dec = jnp.stack([pr[L, 0], pi[L, 0], pr[L, 1], pi[L, 1]], 0).reshape(4, nq * npair, 2 * P)
    dec = jnp.broadcast_to(jnp.transpose(dec, (1, 0, 2))[:, :, None, :], (nq * npair, 4, 8, 2 * P))
    return w_sum.astype(BF), w_lag, w_state.astype(BF), dec


def _chunk_rows(x_ref, nb, rc):
    L = C_CHUNK
    rows = [jnp.concatenate([x_ref[b, pl.ds(j, rc, stride=L), :].astype(BF) for j in range(L)], axis=1)
            for b in range(nb)]
    return jnp.concatenate(rows, axis=0)


def _s5_sum_kernel(x_ref, w_ref, s_ref, *, nb, rc):
    s = _dot(_chunk_rows(x_ref, nb, rc), w_ref[...])
    for b in range(nb):
        for t in range(s_ref.shape[0]):
            s_ref[t, pl.ds(b, rc, stride=nb), :] = s[b * rc:(b + 1) * rc, LANES * t:LANES * (t + 1)]


def _s5_scan_kernel(s_ref, dec_ref, e_ref, *, nc, nb):
    units = s_ref.shape[0] // 4
    dec = [[dec_ref[u, k, 0:nb, :] for k in range(4)] for u in range(units)]

    def body(t, carry):
        rf = pl.ds(pl.multiple_of(t * nb, nb), nb)
        rb = pl.ds(pl.multiple_of((nc - 1 - t) * nb, nb), nb)
        new = []
        for u in range(units):
            fr, fi, br, bi = carry[4 * u:4 * u + 4]
            arf, aif, arb, aib = dec[u]
            e_ref[4 * u, rf, :] = fr
            e_ref[4 * u + 1, rf, :] = fi
            e_ref[4 * u + 2, rb, :] = br
            e_ref[4 * u + 3, rb, :] = bi
            new += [fr * arf - fi * aif + s_ref[4 * u, rf, :], fi * arf + fr * aif + s_ref[4 * u + 1, rf, :],
                    br * arb - bi * aib + s_ref[4 * u + 2, rb, :], bi * arb + br * aib + s_ref[4 * u + 3, rb, :]]
        return tuple(new)

    z = jnp.zeros((nb, LANES), F32)
    lax.fori_loop(0, nc, body, (z,) * (4 * units))


def _s5_out_kernel(x_ref, e_ref, lag_ref, wc_ref, y_ref, wi_ref, *, nb, rc):
    L = C_CHUNK
    half = pl.program_id(2)

    @pl.when((pl.program_id(1) == 0) & (half == 0))
    def _():
        for j in range(L):
            for i in range(L):
                wi_ref[i // (L // 2), LANES * j:LANES * (j + 1), LANES * (i % (L // 2)):LANES * (i % (L // 2) + 1)] = (
                    lag_ref[i - j + L - 1])

    xc = _chunk_rows(x_ref, nb, rc)
    ec = jnp.concatenate(
        [jnp.concatenate([e_ref[t, pl.ds(b, rc, stride=nb), :].astype(BF) for t in range(e_ref.shape[0])], axis=1)
         for b in range(nb)], axis=0)
    y = _dot(xc, wi_ref[half]) + _dot(ec, wc_ref[...])
    for b in range(nb):
        for ii in range(L // 2):
            y_ref[b, pl.ds(half * (L // 2) + ii, rc, stride=L), :] = y[b * rc:(b + 1) * rc, LANES * ii:LANES * (ii + 1)]


def _s5(x, b, s, w_sum, w_lag, w_state, dec):
    L = C_CHUNK
    nc = s // L
    nq = D_MODEL // LANES
    nsl = w_sum.shape[2] // LANES
    blk = L * LANES
    rc = _tile(nc, max(8, 512 // b))
    x3 = x.reshape(b, s, D_MODEL)
    x_spec = pl.BlockSpec((b, rc * L, LANES), lambda q, c, *_: (0, c, q))
    sums = pl.pallas_call(
        functools.partial(_s5_sum_kernel, nb=b, rc=rc),
        grid=(nq, nc // rc),
        in_specs=[x_spec, pl.BlockSpec((None, blk, nsl * LANES), lambda q, c: (q, 0, 0))],
        out_specs=pl.BlockSpec((nsl, rc * b, LANES), lambda q, c: (q, c, 0)),
        out_shape=jax.ShapeDtypeStruct((nq * nsl, nc * b, LANES), F32),
        compiler_params=_params("parallel", "arbitrary"),
        name="s5_sum",
    )(x3, w_sum)
    upb = 2
    ent = pl.pallas_call(
        functools.partial(_s5_scan_kernel, nc=nc, nb=b),
        grid=(nq * nsl // (4 * upb),),
        in_specs=[pl.BlockSpec((4 * upb, nc * b, LANES), lambda i: (i, 0, 0)),
                  pl.BlockSpec((upb, 4, 8, LANES), lambda i: (i, 0, 0, 0))],
        out_specs=pl.BlockSpec((4 * upb, nc * b, LANES), lambda i: (i, 0, 0)),
        out_shape=jax.ShapeDtypeStruct((nq * nsl, nc * b, LANES), F32),
        compiler_params=_params("parallel"),
        name="s5_scan",
    )(sums, dec)
    y = pl.pallas_call(
        functools.partial(_s5_out_kernel, nb=b, rc=rc),
        grid=(nq, nc // rc, 2),
        in_specs=[x_spec,
                  pl.BlockSpec((nsl, rc * b, LANES), lambda q, c, h: (q, c, 0)),
                  pl.BlockSpec((None, 2 * L - 1, LANES, LANES), lambda q, c, h: (q, 0, 0, 0)),
                  pl.BlockSpec((None, nsl * LANES, blk // 2), lambda q, c, h: (q, 0, h))],
        out_specs=pl.BlockSpec((b, rc * L, LANES), lambda q, c, h: (0, c, q)),
        out_shape=jax.ShapeDtypeStruct((b, s, D_MODEL), F32),
        scratch_shapes=[pltpu.VMEM((2, blk, blk // 2), BF)],
        compiler_params=_params("arbitrary", "arbitrary", "arbitrary"),
        name="s5_out",
    )(x3, ent, w_lag, w_state)
    return y.reshape(b * s, D_MODEL)


_D_T = 512


def _prep_d(w_qkv, norm_gain, w_o, lambda_init):
    qk_w = D_HEADS * 2 * D_QK_DIM
    wq = w_qkv[:, :qk_w] * (D_QK_DIM ** -0.5 * LOG2E)
    wv = _pad_heads(w_qkv[:, 2 * qk_w:], D_HEADS, D_V_DIM)
    w = jnp.concatenate([wq, w_qkv[:, qk_w:2 * qk_w], wv], 1).astype(BF)
    gain = jnp.concatenate([norm_gain * (1.0 - lambda_init), jnp.zeros((LANES - D_V_DIM,), F32)])[None, :]
    wo = w_o.reshape(D_HEADS, D_V_DIM, D_MODEL)
    wo = jnp.concatenate([wo, jnp.zeros_like(wo)], 1).reshape(D_HEADS * LANES, D_MODEL).astype(BF)
    return w, gain, wo


def _diff_bias(rel_bias, t):
    rel0 = (jnp.arange(2 * t) + t) % (2 * t) - t
    rel = jnp.arange(-2, 3)[:, None] * t + rel0[None, :]
    return jnp.transpose(rel_bias[_bucket(rel)] * LOG2E, (2, 0, 1)).astype(F32)


def _flash_d_kernel(q_ref, k_ref, v_ref, vec_ref, lam_ref, gain_ref, o_ref, m_ref, acc_ref, bias_ref, *, kt, nk,
                    lambda_init):
    t = q_ref.shape[0]
    i = pl.program_id(2)

    @pl.when(i == 0)
    def _():
        for d in range(bias_ref.shape[0]):
            full = jnp.broadcast_to(vec_ref[d:d + 1, :], (t, 2 * t))
            bias_ref[d] = pltpu.roll(full, 0, 1, stride=1, stride_axis=0)[:, :t]

    qv = q_ref[...].astype(F32)
    lane = lax.broadcasted_iota(I32, (t, LANES), 1)
    lo = (pl.program_id(1) % 2) * (2 * D_QK_DIM)
    q0 = jnp.where((lane >= lo) & (lane < lo + D_QK_DIM), qv, 0.0).astype(BF)
    q1 = jnp.where((lane >= lo + D_QK_DIM) & (lane < lo + 2 * D_QK_DIM), qv, 0.0).astype(BF)
    q = jnp.concatenate([q0, q1], axis=0)
    _softmax_init(m_ref, acc_ref)

    def body(j, carry):
        off = pl.multiple_of(j * (kt * t), kt * t)
        bias = jnp.concatenate([bias_ref[jnp.clip(j * kt + c - i, -2, 2) + 2] for c in range(kt)], axis=1)
        s = _dot_nt(q, k_ref[pl.ds(off, kt * t), :])
        s = (s.reshape(2, t, kt * t) + bias[None]).reshape(2 * t, kt * t)
        _softmax_step(s, v_ref[pl.ds(off, kt * t), :], m_ref, acc_ref)
        return carry

    lax.fori_loop(0, nk, body, 0, unroll=4)
    lf = lam_ref[...]
    lam = (jnp.exp(jnp.sum(lf[0:1] * lf[1:2], keepdims=True))
           - jnp.exp(jnp.sum(lf[2:3] * lf[3:4], keepdims=True)) + lambda_init)
    acc = acc_ref[...]
    on = acc / acc[:, _ONE_LANE:_ONE_LANE + 1]
    o = jnp.where(lane < D_V_DIM, on[:t] - lam * on[t:], 0.0)
    ms = jnp.sum(o * o, -1, keepdims=True) * (1.0 / D_V_DIM)
    o_ref[...] = (o * lax.rsqrt(ms + RMS_EPS) * gain_ref[...]).astype(BF)


def _flash_d(qkv, bias, lam, gain, lambda_init):
    b, s, _ = qkv.shape
    t = bias.shape[-1] // 2
    kt = 2 if s % (2 * t) == 0 else 1
    return pl.pallas_call(
        functools.partial(_flash_d_kernel, kt=kt, nk=s // (kt * t), lambda_init=lambda_init),
        grid=(b, D_HEADS, s // t),
        in_specs=[pl.BlockSpec((None, t, LANES), lambda bi, h, i: (bi, i, h // 2)),
                  pl.BlockSpec((None, s, LANES), lambda bi, h, i: (bi, 0, D_HEADS // 2 + h // 2)),
                  pl.BlockSpec((None, s, LANES), lambda bi, h, i: (bi, 0, D_HEADS + h)),
                  pl.BlockSpec((None, 5, 2 * t), lambda bi, h, i: (h, 0, 0)),
                  pl.BlockSpec((4, D_QK_DIM), lambda bi, h, i: (0, 0)),
                  pl.BlockSpec((1, LANES), lambda bi, h, i: (0, 0))],
        out_specs=pl.BlockSpec((None, t, LANES), lambda bi, h, i: (bi, i, h)),
        out_shape=jax.ShapeDtypeStruct((b, s, D_HEADS * LANES), BF),
        scratch_shapes=[pltpu.VMEM((2 * t, LANES), F32)] * 2 + [pltpu.VMEM((5, t, t), F32)],
        compiler_params=_params("parallel", "parallel", "arbitrary"),
        name="flash_d",
    )(qkv, qkv, qkv, bias, lam, gain)


def _cross_kernel(x_ref, kv_ref, wq_ref, wo_ref, g_ref, b_ref, wr_ref, y_ref, ybf_ref, aff_ref, *, parts):
    tp = x_ref.shape[0] // parts
    wr = wr_ref[...]
    wh = wr.astype(BF)
    wl = (wr - wh.astype(F32)).astype(BF)
    for part in range(parts):
        rows = slice(tp * part, tp * (part + 1))
        x = x_ref[rows, :]
        q = (_dot(x.astype(BF), wq_ref[...]) * (X_HEAD_DIM ** -0.5)).astype(BF)
        outs = []
        for h in range(X_HEADS):
            sl = slice(X_HEAD_DIM * h, X_HEAD_DIM * (h + 1))
            s = _dot_nt(q[:, sl], kv_ref[:, sl])
            p = jnp.exp(s - jnp.max(s, -1, keepdims=True))
            l = jnp.sum(p, -1, keepdims=True)
            vh = kv_ref[:, D_MODEL + X_HEAD_DIM * h:D_MODEL + X_HEAD_DIM * (h + 1)]
            outs.append((_dot(p.astype(BF), vh) / l).astype(BF))
        o = jnp.concatenate(outs, axis=1)
        y = _ln_rows(ALPHA * x + _dot(o, wo_ref[...]), g_ref[...], b_ref[...])
        y_ref[rows, :] = y
        yh = y.astype(BF)
        ybf_ref[rows, :] = yh
        yl = (y - yh.astype(F32)).astype(BF)
        lg = _dot_nt(wh, yh) + _dot_nt(wh, yl) + _dot_nt(wl, yh)
        e = jnp.exp(lg - jnp.max(lg, 0, keepdims=True))
        aff = e / jnp.sum(e, 0, keepdims=True)
        for c in range(tp // LANES):
            aff_ref[part * (tp // LANES) + c] = aff[:, LANES * c:LANES * (c + 1)]


def _cross(x, kv, wq, wo, g, b, wr_t, s, mem_len, tm=1024, parts=2):
    n = x.shape[0]
    tm = _tile(s, tm)
    per = s // tm
    fixed = lambda i: (0, 0)
    return pl.pallas_call(
        functools.partial(_cross_kernel, parts=parts),
        grid=(n // tm,),
        in_specs=[pl.BlockSpec((tm, D_MODEL), lambda i: (i, 0)),
                  pl.BlockSpec((mem_len, 2 * D_MODEL), lambda i: (i // per, 0)),
                  pl.BlockSpec((D_MODEL, D_MODEL), fixed), pl.BlockSpec((D_MODEL, D_MODEL), fixed),
                  pl.BlockSpec((1, D_MODEL), fixed), pl.BlockSpec((1, D_MODEL), fixed),
                  pl.BlockSpec((N_EXPERTS, D_MODEL), fixed)],
        out_specs=[pl.BlockSpec((tm, D_MODEL), lambda i: (i, 0)),
                   pl.BlockSpec((tm, D_MODEL), lambda i: (i, 0)),
                   pl.BlockSpec((tm // LANES, N_EXPERTS, LANES), lambda i: (i, 0, 0))],
        out_shape=[jax.ShapeDtypeStruct((n, D_MODEL), F32),
                   jax.ShapeDtypeStruct((n, D_MODEL), BF),
                   jax.ShapeDtypeStruct((n // LANES, N_EXPERTS, LANES), F32)],
        compiler_params=_params("parallel"),
        name="cross",
    )(x, kv, wq, wo, g, b, wr_t)


def _select_kernel(a_ref, pos_ref, st_ref, *, k, nbits):
    nt = a_ref.shape[0]
    shape = (nt, N_EXPERTS, LANES)
    kf = float(k)

    def keys():
        return lax.bitcast_convert_type(a_ref[...], I32)

    def count(mask):
        c = jnp.sum(jnp.where(mask, 1.0, 0.0), axis=0, keepdims=True)
        return jnp.sum(c, axis=2, keepdims=True)

    def value_step(it, thr):
        cand = thr | jnp.left_shift(jnp.int32(1), 30 - it)
        return jnp.where(count(keys() >= cand) >= kf, cand, thr)

    thr = lax.fori_loop(0, 31, value_step, jnp.zeros((1, N_EXPERTS, 1), I32))
    need = kf - count(keys() > thr)
    idx = lax.broadcasted_iota(I32, shape, 0) * LANES + lax.broadcasted_iota(I32, shape, 2)

    def index_step(it, ithr):
        cand = ithr | jnp.left_shift(jnp.int32(1), nbits - 1 - it)
        return jnp.where(count((keys() == thr) & (idx < cand)) < need, cand, ithr)

    ithr = lax.fori_loop(0, nbits, index_step, jnp.zeros((1, N_EXPERTS, 1), I32))
    thr2, ithr2 = thr[0], ithr[0]
    upper = jnp.where(lax.broadcasted_iota(I32, (LANES, LANES), 0) <= lax.broadcasted_iota(I32, (LANES, LANES), 1),
                      1.0, 0.0).astype(BF)
    lane = lax.broadcasted_iota(I32, (N_EXPERTS, LANES), 1)

    def tile_step(j, carry):
        kj = lax.bitcast_convert_type(a_ref[j], I32)
        sel = (kj > thr2) | ((kj == thr2) & (j * LANES + lane <= ithr2))
        m = jnp.where(sel, 1.0, 0.0)
        inc = _dot(m.astype(BF), upper)
        pos_ref[j] = jnp.where(sel, inc - m + carry, -1.0).astype(I32)
        st_ref[j] = jnp.broadcast_to(carry, (N_EXPERTS, LANES)).astype(I32)
        return carry + inc[:, LANES - 1:LANES]

    lax.fori_loop(0, nt, tile_step, jnp.zeros((N_EXPERTS, 1), F32))


def _select(aff3, k):
    nt = aff3.shape[0]
    nbits = max(1, int(math.ceil(math.log2(nt * LANES))))
    shp = jax.ShapeDtypeStruct(aff3.shape, I32)
    return pl.pallas_call(
        functools.partial(_select_kernel, k=k, nbits=nbits),
        out_shape=[shp, shp],
        compiler_params=pltpu.CompilerParams(vmem_limit_bytes=VMEM_LIMIT),
        name="select",
    )(aff3)


_GATHER_ROWS = LANES + BF16_ROWS
_GATHER_ROWS_SMALL = 3 * BF16_ROWS

_MOE_GROUP = 2


def _moe_ffn_kernel(st_ref, x_ref, pos_ref, aff_ref, wg_ref, wu_ref, wd_ref, ye_ref, buf_ref, gate_ref, *, nb, ncf,
                    nch, sub, ck):
    ep = buf_ref.shape[0]
    grp = pl.program_id(0)
    t = pl.program_id(1)

    @pl.when(t == 0)
    def _():
        buf_ref[...] = jnp.zeros_like(buf_ref)
        gate_ref[...] = jnp.zeros_like(gate_ref)

    @pl.when(t < nb)
    def _():
        def place(s, k, e, base, nrows):
            rows = pl.ds(base, nrows)
            hit = lax.broadcasted_iota(I32, (nrows, LANES), 0) == pos_ref[s, pl.ds(e, 1), :] - base
            xs = x_ref[LANES * s:LANES * (s + 1), :]
            buf_ref[k, rows, :] = buf_ref[k, rows, :] + _dot(jnp.where(hit, 1.0, 0.0).astype(BF), xs).astype(BF)
            gate_ref[k, rows, :] = gate_ref[k, rows, :] + jnp.sum(
                jnp.where(hit, aff_ref[s, pl.ds(e, 1), :], 0.0), -1, keepdims=True)

        def base_of(s, k):
            st = st_ref[grp * ep + k, t * sub + s]
            return pl.multiple_of((st // BF16_ROWS) * BF16_ROWS, BF16_ROWS)

        pairs = [(s, k) for s in range(sub) for k in range(ep)]
        few = functools.reduce(
            jnp.logical_and,
            [st_ref[grp * ep + k, t * sub + s + 1] - base_of(s, k) <= _GATHER_ROWS_SMALL for s, k in pairs])

        @pl.when(few)
        def _():
            for s, k in pairs:
                place(s, k, grp * ep + k, base_of(s, k), _GATHER_ROWS_SMALL)

        @pl.when(jnp.logical_not(few))
        def _():
            for s, k in pairs:
                place(s, k, grp * ep + k, base_of(s, k), _GATHER_ROWS)

    @pl.when(t >= nb)
    def _():
        k = (t - nb) // nch
        c = (t - nb) - k * nch

        @pl.when(c < ncf)
        def _():
            rows = pl.ds(pl.multiple_of(c * ck, ck), ck)
            xe = buf_ref[k, rows, :]
            h = (jax.nn.silu(_dot(xe, wg_ref[...])) * _dot(xe, wu_ref[...])).astype(BF)
            ye_ref[...] = (_dot(h, wd_ref[...]) * gate_ref[k, rows, :]).astype(BF)

        @pl.when(c >= ncf)
        def _():
            ye_ref[...] = jnp.zeros_like(ye_ref)


def _moe_ffn(starts, xbf, pos3, aff3, wg, wu, wd, cap, tb=2048):
    n = xbf.shape[0]
    tb = _tile(n, tb)
    nb = n // tb
    sub = tb // LANES
    ck = _tile(cap, 1024)
    ncf = cap // ck
    nch = ncf + pl.cdiv(_WIN_ROWS, ck)
    dff = wg.shape[-1]
    ep = _MOE_GROUP
    blk = lambda g, t, st: (jnp.minimum(t, nb - 1), 0)
    blk3 = lambda g, t, st: (jnp.minimum(t, nb - 1), 0, 0)
    expert = lambda g, t: g * ep + jnp.clip((t - nb) // nch, 0, ep - 1)
    wmap = lambda g, t, st: (expert(g, t), 0, 0)
    return pl.pallas_call(
        functools.partial(_moe_ffn_kernel, nb=nb, ncf=ncf, nch=nch, sub=sub, ck=ck),
        grid_spec=pltpu.PrefetchScalarGridSpec(
            num_scalar_prefetch=1,
            grid=(N_EXPERTS // ep, nb + ep * nch),
            in_specs=[pl.BlockSpec((tb, D_MODEL), blk),
                      pl.BlockSpec((sub, N_EXPERTS, LANES), blk3),
                      pl.BlockSpec((sub, N_EXPERTS, LANES), blk3),
                      pl.BlockSpec((None, D_MODEL, dff), wmap),
                      pl.BlockSpec((None, D_MODEL, dff), wmap),
                      pl.BlockSpec((None, dff, D_MODEL), wmap)],
            out_specs=pl.BlockSpec((None, ck, D_MODEL),
                                   lambda g, t, st: (expert(g, t), jnp.maximum(t - nb, 0) % nch, 0)),
            scratch_shapes=[pltpu.VMEM((ep, cap + _GATHER_ROWS, D_MODEL), BF),
                            pltpu.VMEM((ep, cap + _GATHER_ROWS, 1), F32)]),
        out_shape=jax.ShapeDtypeStruct((N_EXPERTS, nch * ck, D_MODEL), BF),
        compiler_params=_params("arbitrary", "arbitrary"),
        name="moe_ffn",
    )(starts, xbf, pos3, aff3, wg, wu, wd)


_WIN_STEP = 128
_COMB_TILES = 2
_WIN_ROWS = _WIN_STEP + (_COMB_TILES - 1) * LANES + _GATHER_ROWS


_COMB_PACK = MXU_DEPTH // _GATHER_ROWS_SMALL


def _combine_kernel(st_ref, x_ref, pos_ref, g_ref, b_ref, *rest):
    ye_refs, y_ref = rest[:N_EXPERTS], rest[N_EXPERTS]
    j = pl.program_id(0)
    pairs = [(u, e) for u in range(_COMB_TILES) for e in range(N_EXPERTS)]

    def base_of(u, e):
        return (st_ref[e, j * _COMB_TILES + u] // BF16_ROWS) * BF16_ROWS

    def rows_of(u, e, nrows):
        win = (st_ref[e, j * _COMB_TILES] // _WIN_STEP) * _WIN_STEP
        return ye_refs[e][0, pl.ds(pl.multiple_of(base_of(u, e) - win, BF16_ROWS), nrows), :]

    def finish(u, acc):
        tok = slice(LANES * u, LANES * (u + 1))
        y_ref[tok, :] = _ln_rows(ALPHA * x_ref[tok, :] + acc, g_ref[...], b_ref[...])

    few = functools.reduce(
        jnp.logical_and,
        [st_ref[e, j * _COMB_TILES + u + 1] - base_of(u, e) <= _GATHER_ROWS_SMALL for u, e in pairs])

    @pl.when(few)
    def _():
        for u in range(_COMB_TILES):
            tok = slice(LANES * u, LANES * (u + 1))
            acc = None
            for e0 in range(0, N_EXPERTS, _COMB_PACK):
                es = list(range(e0, min(e0 + _COMB_PACK, N_EXPERTS)))
                col = lax.broadcasted_iota(I32, (LANES, len(es) * _GATHER_ROWS_SMALL), 1)
                hit = None
                for k, e in enumerate(es):
                    rel = pos_ref[tok, e:e + 1] - base_of(u, e)
                    h = col == jnp.where(rel >= 0, rel + k * _GATHER_ROWS_SMALL, -1)
                    hit = h if hit is None else hit | h
                slab = jnp.concatenate([rows_of(u, e, _GATHER_ROWS_SMALL) for e in es], axis=0)
                d = _dot(jnp.where(hit, 1.0, 0.0).astype(BF), slab)
                acc = d if acc is None else acc + d
            finish(u, acc)

    @pl.when(jnp.logical_not(few))
    def _():
        col = lax.broadcasted_iota(I32, (LANES, _GATHER_ROWS), 1)
        for u in range(_COMB_TILES):
            tok = slice(LANES * u, LANES * (u + 1))
            acc = None
            for e in range(N_EXPERTS):
                onehot = jnp.where(col == pos_ref[tok, e:e + 1] - base_of(u, e), 1.0, 0.0).astype(BF)
                d = _dot(onehot, rows_of(u, e, _GATHER_ROWS))
                acc = d if acc is None else acc + d
            finish(u, acc)


def _combine(starts, x, pos, g, b, ye):
    n = x.shape[0]
    assert ye.shape[1] >= EC_CAPACITY * n // N_EXPERTS + _WIN_ROWS
    tt = _COMB_TILES * LANES
    row = lambda j, st: (j, 0)
    fixed = lambda j, st: (0, 0)

    def ye_spec(e):
        return pl.BlockSpec((pl.Element(1), pl.Element(_WIN_ROWS), pl.Element(D_MODEL)),
                            lambda j, st: (e, (st[e, j * _COMB_TILES] // _WIN_STEP) * _WIN_STEP, 0))

    return pl.pallas_call(
        _combine_kernel,
        grid_spec=pltpu.PrefetchScalarGridSpec(
            num_scalar_prefetch=1,
            grid=(n // tt,),
            in_specs=[pl.BlockSpec((tt, D_MODEL), row),
                      pl.BlockSpec((tt, N_EXPERTS), row),
                      pl.BlockSpec((1, D_MODEL), fixed), pl.BlockSpec((1, D_MODEL), fixed)]
            + [ye_spec(e) for e in range(N_EXPERTS)],
            out_specs=pl.BlockSpec((tt, D_MODEL), row)),
        out_shape=jax.ShapeDtypeStruct((n, D_MODEL), F32),
        compiler_params=_params("arbitrary"),
        name="combine",
    )(starts, x, pos, g, b, *([ye] * N_EXPERTS))


def _moe(x, xg, aff3, wg, wu, wd, g, b):
    n = x.shape[0]
    cap = EC_CAPACITY * n // N_EXPERTS
    pos3, st3 = _select(aff3, cap)
    starts = jnp.concatenate([jnp.transpose(st3[:, :, 0]), jnp.full((N_EXPERTS, 1), cap, I32)], axis=1)
    ye = _moe_ffn(starts, xg, pos3, aff3, wg, wu, wd, cap)
    pos = jnp.transpose(pos3, (0, 2, 1)).reshape(n, N_EXPERTS)
    return _combine(starts, x, pos, g, b, ye)


def _prep_weights(p):
    w = {}
    w['a'] = [_prep_a(p['a_w_qkv'][j], p['a_q_gain'][j], p['a_k_gain'][j], p['a_w_o'][j])
              for j in range(p['a_w_qkv'].shape[0])]
    w['b'] = [_prep_b(p['b_w_qkv'][j], p['b_w_o'][j]) for j in range(p['b_w_qkv'].shape[0])]
    w['c'] = [_prep_c(p['c_lam_re'][j], p['c_lam_im'][j], p['c_log_dt'][j], p['c_b_re'][j], p['c_b_im'][j],
                      p['c_c_re'][j], p['c_c_im'][j]) + (p['c_d'][j][None, :], p['c_w_glu'][j].astype(BF))
              for j in range(p['c_lam_re'].shape[0])]
    w['d'] = []
    for j in range(p['d_w_qkv'].shape[0]):
        layer = N_MIXERS * j + 3
        lambda_init = 0.8 - 0.6 * math.exp(-0.3 * layer)
        w['d'].append(_prep_d(p['d_w_qkv'][j], p['d_norm_gain'][j], p['d_w_o'][j], lambda_init)
                      + (p['d_lam'][j].astype(F32), lambda_init))
    w['diff_bias'] = _diff_bias(p['rel_bias'], _D_T)
    w['x_w_q'] = p['x_w_q'].astype(BF)
    w['x_w_kv'] = p['x_w_kv'].astype(BF)
    w['x_w_o'] = p['x_w_o'].astype(BF)
    w['router_t'] = jnp.transpose(p['moe_w_router'], (0, 2, 1)).astype(F32)
    w['moe_w_gate'] = p['moe_w_gate'].astype(BF)
    w['moe_w_up'] = p['moe_w_up'].astype(BF)
    w['moe_w_down'] = p['moe_w_down'].astype(BF)
    return w


def _trunk(x, mem, p, w):
    b, s, _ = x.shape
    n = b * s
    mem_len = mem.shape[1]
    x = x.reshape(n, D_MODEL)
    mem2 = mem.reshape(b * mem_len, D_MODEL)
    ln_g, ln_b = p['ln_g'], p['ln_b']
    for i in range(DEPTH):
        m, j = i % N_MIXERS, i // N_MIXERS
        g0, b0 = ln_g[i, 0][None, :], ln_b[i, 0][None, :]
        if m == 0:
            wa, gains, wo = w['a'][j]
            qkv = _proj_a(x, wa, _rope_table(s), gains, s)
            o = _flash_a(qkv.reshape(b, s, _A_COLS))
            x = _post(x, o.reshape(n, A_HEADS * A_HEAD_DIM), wo, g0, b0)
        elif m == 1:
            wb, wo = w['b'][j]
            qkv = _proj(x, wb).reshape(b, s, wb.shape[1])
            os_, ls_ = [], []
            for g in range(len(B_PATTERNS)):
                t = _tile(s, _B_T)
                o, lse = _band_attention(qkv, _band_bias(p['rel_bias'], g, t), g, t)
                os_.append(o)
                ls_.append(lse)
            x = _post_b(x, os_, ls_, wo, g0, b0)
        elif m == 2:
            w_sum, w_lag, w_state, dec, dskip, wglu = w['c'][j]
            ys = _s5(x, b, s, w_sum, w_lag, w_state, dec)
            x = _post_c(x, ys, dskip, wglu, g0, b0)
        else:
            wd, gain, wo, lam, lambda_init = w['d'][j]
            qkv = _proj(x, wd, ones_from=D_HEADS * LANES).reshape(b, s, wd.shape[1])
            o = _flash_d(qkv, w['diff_bias'], lam, gain, lambda_init)
            x = _post(x, o.reshape(n, D_HEADS * LANES), wo, g0, b0)
        kv = _proj(mem2, w['x_w_kv'][i], tm=mem_len)
        x, xbf, aff3 = _cross(x, kv, w['x_w_q'][i], w['x_w_o'][i], ln_g[i, 1][None, :], ln_b[i, 1][None, :],
                              w['router_t'][i], s, mem_len)
        x = _moe(x, xbf, aff3, w['moe_w_gate'][i], w['moe_w_up'][i], w['moe_w_down'][i],
                 ln_g[i, 2][None, :], ln_b[i, 2][None, :])
    return x.reshape(b, s, D_MODEL)


def kernel(x_prompt, x_sample, mem_prompt, mem_sample, rel_bias, ln_g, ln_b, a_w_qkv, a_q_gain, a_k_gain, a_w_o, b_w_qkv, b_w_o, c_lam_re, c_lam_im, c_log_dt, c_b_re, c_b_im, c_c_re, c_c_im, c_d, c_w_glu, d_w_qkv, d_lam, d_norm_gain, d_w_o, x_w_q, x_w_kv, x_w_o, moe_w_router, moe_w_gate, moe_w_up, moe_w_down):
    p = dict(rel_bias=rel_bias, ln_g=ln_g, ln_b=ln_b,
             a_w_qkv=a_w_qkv, a_q_gain=a_q_gain, a_k_gain=a_k_gain, a_w_o=a_w_o,
             b_w_qkv=b_w_qkv, b_w_o=b_w_o,
             c_lam_re=c_lam_re, c_lam_im=c_lam_im, c_log_dt=c_log_dt, c_b_re=c_b_re, c_b_im=c_b_im,
             c_c_re=c_c_re, c_c_im=c_c_im, c_d=c_d, c_w_glu=c_w_glu,
             d_w_qkv=d_w_qkv, d_lam=d_lam, d_norm_gain=d_norm_gain, d_w_o=d_w_o,
             x_w_q=x_w_q, x_w_kv=x_w_kv, x_w_o=x_w_o,
             moe_w_router=moe_w_router, moe_w_gate=moe_w_gate, moe_w_up=moe_w_up, moe_w_down=moe_w_down)
    w = _prep_weights(p)
    return (_trunk(x_prompt, mem_prompt, p, w), _trunk(x_sample, mem_sample, p, w))
```

```python
import functools
import math

import numpy as np
import jax
import jax.numpy as jnp
from jax import lax
from jax.experimental import pallas as pl
from jax.experimental.pallas import tpu as pltpu

F32 = jnp.float32
BF = jnp.bfloat16
I32 = jnp.int32

D_MODEL = 1024
DEPTH = 4
GRID_W = 64
N_MIXERS = 4
LN_EPS = 1e-5
RMS_EPS = 1e-6
ALPHA = (2.0 * DEPTH) ** 0.25

A_HEADS = 16
A_KV_HEADS = 4
A_HEAD_DIM = 64
ROPE_BASE = 10000.0

B_PATTERNS = ((128, 1), (512, 4), (2048, 16))
B_HEADS_PER_GROUP = 4
B_HEAD_DIM = 64

C_GROUP = 16
C_N_GROUPS = D_MODEL // C_GROUP
C_STATE = 64
C_CHUNK = 16
D_HEADS = 12
D_QK_DIM = 32
D_V_DIM = 64

REL_BUCKETS = 32
REL_MAX_DIST = 128

X_HEADS = 4
X_HEAD_DIM = D_MODEL // X_HEADS

N_EXPERTS = 16
EC_CAPACITY = 2

LANES = 128
BF16_ROWS = 16
MXU_DEPTH = 256
_C_GL = LANES // C_GROUP
VMEM_LIMIT = 56 * 1024 * 1024
NEG = -1e30
LOG2E = math.log2(math.e)


def _params(*sem):
    return pltpu.CompilerParams(dimension_semantics=sem, vmem_limit_bytes=VMEM_LIMIT)


def _tile(n, pref):
    t = min(n, pref)
    assert n % t == 0, (n, pref)
    return t


def _ln_rows(v, g, b):
    mu = jnp.mean(v, -1, keepdims=True)
    c = v - mu
    var = jnp.mean(c * c, -1, keepdims=True)
    return c * lax.rsqrt(var + LN_EPS) * g + b


def _dot_nt(a, b):
    return lax.dot_general(a, b, (((1,), (1,)), ((), ())), preferred_element_type=F32)


def _dot(a, b):
    return jnp.dot(a, b, preferred_element_type=F32)


_ONE_LANE = 64


def _with_ones_lane(y):
    lane = lax.broadcasted_iota(I32, y.shape, 1)
    return jnp.where(lane % LANES == _ONE_LANE, 1.0, y)


def _proj_kernel(x_ref, w_ref, o_ref, *, ones_from):
    y = _dot(x_ref[...].astype(BF), w_ref[...])
    if ones_from is None:
        o_ref[...] = y.astype(o_ref.dtype)
    else:
        o_ref[:, :ones_from] = y[:, :ones_from].astype(o_ref.dtype)
        o_ref[:, ones_from:] = _with_ones_lane(y[:, ones_from:]).astype(o_ref.dtype)


def _proj(x, w, tm=512, ones_from=None):
    n, k = x.shape
    m = w.shape[1]
    tm = _tile(n, tm)
    return pl.pallas_call(
        functools.partial(_proj_kernel, ones_from=ones_from),
        grid=(n // tm,),
        in_specs=[pl.BlockSpec((tm, k), lambda i: (i, 0)),
                  pl.BlockSpec((k, m), lambda i: (0, 0))],
        out_specs=pl.BlockSpec((tm, m), lambda i: (i, 0)),
        out_shape=jax.ShapeDtypeStruct((n, m), BF),
        compiler_params=_params("parallel"),
        name="proj",
    )(x, w)


def _post_kernel(x_ref, o_ref, w_ref, g_ref, b_ref, y_ref):
    h = _dot(o_ref[...], w_ref[...])
    y_ref[...] = _ln_rows(ALPHA * x_ref[...] + h, g_ref[...], b_ref[...])


def _post(x, o, w, g, b, tm=512):
    n = x.shape[0]
    ko = o.shape[1]
    tm = _tile(n, tm)
    return pl.pallas_call(
        _post_kernel,
        grid=(n // tm,),
        in_specs=[pl.BlockSpec((tm, D_MODEL), lambda i: (i, 0)),
                  pl.BlockSpec((tm, ko), lambda i: (i, 0)),
                  pl.BlockSpec((ko, D_MODEL), lambda i: (0, 0)),
                  pl.BlockSpec((1, D_MODEL), lambda i: (0, 0)),
                  pl.BlockSpec((1, D_MODEL), lambda i: (0, 0))],
        out_specs=pl.BlockSpec((tm, D_MODEL), lambda i: (i, 0)),
        out_shape=jax.ShapeDtypeStruct((n, D_MODEL), F32),
        compiler_params=_params("parallel"),
        name="post",
    )(x, o, w, g, b)


def _post_b_kernel(x_ref, o0_ref, o1_ref, o2_ref, l0_ref, l1_ref, l2_ref, w_ref, g_ref, b_ref, y_ref):
    l0, l1, l2 = l0_ref[...], l1_ref[...], l2_ref[...]
    m = jnp.maximum(jnp.maximum(l0, l1), l2)
    e0, e1, e2 = jnp.exp(l0 - m), jnp.exp(l1 - m), jnp.exp(l2 - m)
    inv = 1.0 / (e0 + e1 + e2)
    gw = B_HEADS_PER_GROUP * B_HEAD_DIM
    h = _dot((o0_ref[...].astype(F32) * (e0 * inv)).astype(BF), w_ref[0:gw, :])
    h = h + _dot((o1_ref[...].astype(F32) * (e1 * inv)).astype(BF), w_ref[gw:2 * gw, :])
    h = h + _dot((o2_ref[...].astype(F32) * (e2 * inv)).astype(BF), w_ref[2 * gw:3 * gw, :])
    y_ref[...] = _ln_rows(ALPHA * x_ref[...] + h, g_ref[...], b_ref[...])


def _post_b(x, os_, ls_, w, g, b, tm=512):
    n = x.shape[0]
    tm = _tile(n, tm)
    gw = B_HEADS_PER_GROUP * B_HEAD_DIM
    row = lambda i: (i, 0)
    fixed = lambda i: (0, 0)
    return pl.pallas_call(
        _post_b_kernel,
        grid=(n // tm,),
        in_specs=[pl.BlockSpec((tm, D_MODEL), row)] + [pl.BlockSpec((tm, gw), row)] * 6
        + [pl.BlockSpec((3 * gw, D_MODEL), fixed), pl.BlockSpec((1, D_MODEL), fixed),
           pl.BlockSpec((1, D_MODEL), fixed)],
        out_specs=pl.BlockSpec((tm, D_MODEL), row),
        out_shape=jax.ShapeDtypeStruct((n, D_MODEL), F32),
        compiler_params=_params("parallel"),
        name="post_b",
    )(x, *os_, *ls_, w, g, b)


def _post_c_kernel(x_ref, ys_ref, d_ref, w_ref, g_ref, b_ref, y_ref):
    x = x_ref[...]
    z = jax.nn.gelu(ys_ref[...] + d_ref[...] * x).astype(BF)
    h = _dot(z, w_ref[...])
    hh = h[:, :D_MODEL] * jax.nn.sigmoid(h[:, D_MODEL:])
    y_ref[...] = _ln_rows(ALPHA * x + hh, g_ref[...], b_ref[...])


def _post_c(x, ys, d, w, g, b, tm=512):
    n = x.shape[0]
    tm = _tile(n, tm)
    row = lambda i: (i, 0)
    fixed = lambda i: (0, 0)
    return pl.pallas_call(
        _post_c_kernel,
        grid=(n // tm,),
        in_specs=[pl.BlockSpec((tm, D_MODEL), row), pl.BlockSpec((tm, D_MODEL), row),
                  pl.BlockSpec((1, D_MODEL), fixed), pl.BlockSpec((D_MODEL, 2 * D_MODEL), fixed),
                  pl.BlockSpec((1, D_MODEL), fixed), pl.BlockSpec((1, D_MODEL), fixed)],
        out_specs=pl.BlockSpec((tm, D_MODEL), row),
        out_shape=jax.ShapeDtypeStruct((n, D_MODEL), F32),
        compiler_params=_params("parallel"),
        name="post_c",
    )(x, ys, d, w, g, b)


_A_QK_TILES = A_HEADS + A_KV_HEADS
_A_COLS = (_A_QK_TILES + A_KV_HEADS) * LANES


def _rope_partner():
    d = np.arange(A_HEAD_DIM)
    e = d % (A_HEAD_DIM // 2)
    lo = e < A_HEAD_DIM // 4
    return np.where(lo, d + A_HEAD_DIM // 4, d - A_HEAD_DIM // 4), np.where(lo, -1.0, 1.0).astype(np.float32)


def _prep_a(w_qkv, q_gain, k_gain, w_o):
    partner, sign = _rope_partner()
    nqk = _A_QK_TILES * A_HEAD_DIM
    wqk = w_qkv[:, :nqk].reshape(D_MODEL, _A_QK_TILES, A_HEAD_DIM)
    wsw = wqk[:, :, partner] * sign
    wqk = jnp.concatenate([wqk, wsw], -1).reshape(D_MODEL, _A_QK_TILES * LANES)
    wv = w_qkv[:, nqk:].reshape(D_MODEL, A_KV_HEADS, A_HEAD_DIM)
    wv = jnp.concatenate([wv, jnp.zeros_like(wv)], -1).reshape(D_MODEL, A_KV_HEADS * LANES)
    w = jnp.concatenate([wqk, wv], 1).astype(BF)
    gq = jnp.concatenate([q_gain, q_gain[partner]]) * (A_HEAD_DIM ** -0.5 * 0.5 * LOG2E)
    gk = jnp.concatenate([k_gain, k_gain[partner]])
    gains = jnp.stack([gq, gk], 0)
    return w, gains, w_o.astype(BF)


def _rope_table(s):
    pos = jnp.arange(s)
    rows, cols = (pos // GRID_W).astype(F32), (pos % GRID_W).astype(F32)
    half = A_HEAD_DIM // 2
    freqs = ROPE_BASE ** (-jnp.arange(0, half, 2, dtype=F32) / half)
    ang_r = rows[:, None] * freqs
    ang_c = cols[:, None] * freqs
    ang = jnp.concatenate([ang_r, ang_r, ang_c, ang_c], -1)
    return jnp.concatenate([jnp.cos(ang), jnp.sin(ang)], -1)


def _proj_a_kernel(x_ref, w_ref, cs_ref, g_ref, o_ref):
    xb = x_ref[...].astype(BF)
    cs = cs_ref[...]
    gq = cs * g_ref[0:1, :]
    gk = cs * g_ref[1:2, :]
    per = 4
    for g in range(_A_QK_TILES // per):
        y = _dot(xb, w_ref[:, LANES * per * g:LANES * per * (g + 1)])
        for hh in range(per):
            h = per * g + hh
            t = y[:, LANES * hh:LANES * (hh + 1)]
            r = lax.rsqrt(jnp.sum(t * t, -1, keepdims=True) * (1.0 / LANES) + RMS_EPS)
            e = t * r * (gq if h < A_HEADS else gk)
            o_ref[:, LANES * h:LANES * (h + 1)] = (e + pltpu.roll(e, LANES // 2, 1)).astype(BF)
    yv = _dot(xb, w_ref[:, _A_QK_TILES * LANES:])
    o_ref[:, _A_QK_TILES * LANES:] = _with_ones_lane(yv).astype(BF)


def _proj_a(x, w, cs, gains, s, tm=512):
    n = x.shape[0]
    tm = _tile(s, tm)
    per = s // tm
    return pl.pallas_call(
        _proj_a_kernel,
        grid=(n // tm,),
        in_specs=[pl.BlockSpec((tm, D_MODEL), lambda i: (i, 0)),
                  pl.BlockSpec((D_MODEL, _A_COLS), lambda i: (0, 0)),
                  pl.BlockSpec((tm, LANES), lambda i: (i % per, 0)),
                  pl.BlockSpec((2, LANES), lambda i: (0, 0))],
        out_specs=pl.BlockSpec((tm, _A_COLS), lambda i: (i, 0)),
        out_shape=jax.ShapeDtypeStruct((n, _A_COLS), BF),
        compiler_params=_params("parallel"),
        name="proj_a",
    )(x, w, cs, gains)


def _softmax_step(s, v, m_ref, acc_ref):
    tk = s.shape[1]
    m_old = m_ref[...]
    m_new = jnp.maximum(m_old, jnp.max(s, -1, keepdims=True))
    p = jnp.concatenate([jnp.exp2(s[:, LANES * c:LANES * (c + 1)] - m_new).astype(BF) for c in range(tk // LANES)],
                        axis=1)
    acc_ref[...] = jnp.exp2(m_old - m_new) * acc_ref[...] + _dot(p, v)
    m_ref[...] = m_new


def _softmax_init(m_ref, acc_ref):
    m_ref[...] = jnp.full(m_ref.shape, -jnp.inf, F32)
    acc_ref[...] = jnp.zeros_like(acc_ref)


def _flash_a_kernel(q_ref, k_ref, v_ref, o_ref, m_ref, acc_ref, *, tk, nk):
    tq = q_ref.shape[0]
    rep = A_HEADS // A_KV_HEADS
    q = jnp.concatenate([q_ref[:, LANES * r:LANES * (r + 1)] for r in range(rep)], axis=0)
    _softmax_init(m_ref, acc_ref)

    def body(j, carry):
        off = pl.multiple_of(j * tk, tk)
        _softmax_step(_dot_nt(q, k_ref[pl.ds(off, tk), :]), v_ref[pl.ds(off, tk), :], m_ref, acc_ref)
        return carry

    lax.fori_loop(0, nk, body, 0, unroll=4)
    acc = acc_ref[...]
    o = acc / acc[:, _ONE_LANE:_ONE_LANE + 1]
    first = lax.broadcasted_iota(I32, (tq, LANES), 1) < A_HEAD_DIM
    for p in range(rep // 2):
        lo = o[2 * p * tq:(2 * p + 1) * tq]
        hi = pltpu.roll(o[(2 * p + 1) * tq:(2 * p + 2) * tq], A_HEAD_DIM, 1)
        o_ref[:, LANES * p:LANES * (p + 1)] = jnp.where(first, lo, hi).astype(BF)


def _flash_a(qkv, tq=256, tk=2048):
    b, s, _ = qkv.shape
    tq, tk = _tile(s, tq), _tile(s, tk)
    rep = A_HEADS // A_KV_HEADS
    gw = rep * LANES
    return pl.pallas_call(
        functools.partial(_flash_a_kernel, tk=tk, nk=s // tk),
        grid=(b, A_KV_HEADS, s // tq),
        in_specs=[pl.BlockSpec((None, tq, gw), lambda bi, g, i: (bi, i, g)),
                  pl.BlockSpec((None, s, LANES), lambda bi, g, i: (bi, 0, A_HEADS + g)),
                  pl.BlockSpec((None, s, LANES), lambda bi, g, i: (bi, 0, _A_QK_TILES + g))],
        out_specs=pl.BlockSpec((None, tq, rep * A_HEAD_DIM), lambda bi, g, i: (bi, i, g)),
        out_shape=jax.ShapeDtypeStruct((b, s, A_HEADS * A_HEAD_DIM), BF),
        scratch_shapes=[pltpu.VMEM((rep * tq, LANES), F32)] * 2,
        compiler_params=_params("parallel", "parallel", "arbitrary"),
        name="flash_a",
    )(qkv, qkv, qkv)


def _bucket(rel):
    half = REL_BUCKETS // 2
    max_exact = half // 2
    n = jnp.abs(rel)
    large = max_exact + (jnp.log(jnp.maximum(n, 1).astype(F32) / max_exact)
                         / math.log(REL_MAX_DIST / max_exact) * (half - max_exact)).astype(I32)
    large = jnp.minimum(large, half - 1)
    return jnp.where(rel > 0, half, 0) + jnp.where(n < max_exact, n, large)


_B_GW = B_HEADS_PER_GROUP * B_HEAD_DIM
_B_T = 256


def _pad_heads(w, heads, dim):
    w = w.reshape(w.shape[0], heads, dim)
    return jnp.concatenate([w, jnp.zeros((w.shape[0], heads, LANES - dim), w.dtype)], -1).reshape(
        w.shape[0], heads * LANES)


def _prep_b(w_qkv, w_o):
    c = len(B_PATTERNS) * B_HEADS_PER_GROUP * B_HEAD_DIM
    w = jnp.concatenate([w_qkv[:, :c] * (B_HEAD_DIM ** -0.5), w_qkv[:, c:]], 1).astype(BF)
    return w, w_o.astype(BF)


def _toeplitz(vec, t):
    flat = jnp.tile(vec, (1,) * (vec.ndim - 1) + (t,))[..., :t * (2 * t - 1)]
    return flat.reshape(vec.shape[:-1] + (t, 2 * t - 1))[..., :t]


def _band_tiles(g, t):
    window, dil = B_PATTERNS[g]
    return pl.cdiv((window // (2 * dil)) * dil, t)


def _band_bias(rel_bias, g, t):
    window, dil = B_PATTERNS[g]
    reach = (window // (2 * dil)) * dil
    n = _band_tiles(g, t)
    rel0 = (jnp.arange(2 * t) + t) % (2 * t) - t
    rel = jnp.arange(-n, n + 1)[:, None] * t + rel0[None, :]
    bias = rel_bias[_bucket(rel)][:, :, g * B_HEADS_PER_GROUP:(g + 1) * B_HEADS_PER_GROUP]
    bias = jnp.where(((rel % dil == 0) & (jnp.abs(rel) <= reach))[:, :, None], bias, NEG)
    return _toeplitz(jnp.transpose(bias, (2, 0, 1)).astype(F32), t)


def _band_kernel(q_ref, k_ref, v_ref, bias_ref, o_ref, lse_ref, *, n, nk):
    t = q_ref.shape[0]
    i = pl.program_id(2)
    qf = q_ref[...].astype(F32)
    lane = lax.broadcasted_iota(I32, (t, LANES), 1)
    offs = [pl.multiple_of(jnp.clip(i + o, 0, nk - 1) * t, t) for o in range(-n, n + 1)]
    inside = [(i + o >= 0) & (i + o < nk) for o in range(-n, n + 1)]
    outs, lses = [], []
    for hh in range(2):
        q = jnp.where((lane >= B_HEAD_DIM * hh) & (lane < B_HEAD_DIM * (hh + 1)), qf, 0.0).astype(BF)
        logits = [jnp.where(ok, _dot_nt(q, k_ref[pl.ds(off, t), :]) + bias_ref[hh, c], NEG)
                  for c, (off, ok) in enumerate(zip(offs, inside))]
        m = logits[0].max(-1, keepdims=True)
        for s in logits[1:]:
            m = jnp.maximum(m, s.max(-1, keepdims=True))
        l = jnp.zeros((t, 1), F32)
        acc = jnp.zeros((t, LANES), F32)
        for s, off in zip(logits, offs):
            p = jnp.exp(s - m)
            l = l + jnp.sum(p, -1, keepdims=True)
            acc = acc + _dot(p.astype(BF), v_ref[pl.ds(off, t), :])
        outs.append(acc / l)
        lses.append(jnp.broadcast_to(m + jnp.log(l), (t, LANES)))
    first = lane < B_HEAD_DIM
    o_ref[...] = jnp.where(first, outs[0], outs[1]).astype(BF)
    lse_ref[...] = jnp.where(first, lses[0], lses[1])


def _band_attention(qkv, bias, g, t):
    b, s, c = qkv.shape
    third = c // 3 // LANES
    pairs = B_HEADS_PER_GROUP // 2
    n = _band_tiles(g, t)
    spec = lambda base: pl.BlockSpec((None, s, LANES), lambda bi, hp, i: (bi, 0, base + g * pairs + hp))
    out_spec = pl.BlockSpec((None, t, LANES), lambda bi, hp, i: (bi, i, hp))
    o, lse = pl.pallas_call(
        functools.partial(_band_kernel, n=n, nk=s // t),
        grid=(b, pairs, s // t),
        in_specs=[pl.BlockSpec((None, t, LANES), lambda bi, hp, i: (bi, i, g * pairs + hp)),
                  spec(third), spec(2 * third),
                  pl.BlockSpec((2, 2 * n + 1, t, t), lambda bi, hp, i: (hp, 0, 0, 0))],
        out_specs=[out_spec, out_spec],
        out_shape=[jax.ShapeDtypeStruct((b, s, _B_GW), BF), jax.ShapeDtypeStruct((b, s, _B_GW), F32)],
        compiler_params=_params("parallel", "parallel", "arbitrary"),
        name="band_%d" % g,
    )(qkv, qkv, qkv, bias)
    return o.reshape(b * s, _B_GW), lse.reshape(b * s, _B_GW)


def _prep_c(lam_re, lam_im, log_dt, b_re, b_im, c_re, c_im):
    hp = lax.Precision.HIGHEST
    L, P, C = C_CHUNK, C_STATE, C_GROUP
    lr, li = lam_re.astype(F32), lam_im.astype(F32)
    dt = jnp.exp(log_dt.astype(F32))[..., None]
    mag = jnp.exp(lr * dt)
    ar, ai = mag * jnp.cos(li * dt), mag * jnp.sin(li * dt)
    den = lr * lr + li * li
    zr = ((ar - 1.0) * lr + ai * li) / den
    zi = (ai * lr - (ar - 1.0) * li) / den
    br, bi = b_re.astype(F32), b_im.astype(F32)
    bbr = zr[..., None] * br - zi[..., None] * bi
    bbi = zr[..., None] * bi + zi[..., None] * br
    cr, ci = c_re.astype(F32), c_im.astype(F32)
    prs, pis = [jnp.ones_like(ar)], [jnp.zeros_like(ai)]
    for _ in range(L):
        pr_, pi_ = prs[-1], pis[-1]
        prs.append(ar * pr_ - ai * pi_)
        pis.append(ar * pi_ + ai * pr_)
    pr, pi = jnp.stack(prs, 0), jnp.stack(pis, 0)

    def lag(pr_k, pi_k):
        tr = pr_k[..., None] * bbr - pi_k[..., None] * bbi
        ti = pr_k[..., None] * bbi + pi_k[..., None] * bbr
        return (jnp.einsum('dgop,kdgpi->kdgoi', cr, tr, precision=hp)
                - jnp.einsum('dgop,kdgpi->kdgoi', ci, ti, precision=hp))

    kern = lag(pr[:L], pi[:L])
    lags = jnp.concatenate([kern[:0:-1, 1], (kern[0, 0] + kern[0, 1])[None], kern[1:, 0]], 0)

    def summ(d, powers):
        pr_k, pi_k = pr[powers, d], pi[powers, d]
        sr = pr_k[..., None] * bbr[d] - pi_k[..., None] * bbi[d]
        si = pr_k[..., None] * bbi[d] + pi_k[..., None] * bbr[d]
        s = jnp.concatenate([sr, si], 2)
        return jnp.transpose(s, (1, 0, 3, 2)).reshape(C_N_GROUPS, L * C, 2 * P)

    w_sum = jnp.concatenate([summ(0, np.arange(L - 1, -1, -1)), summ(1, np.arange(L))], -1)

    def outw(d, powers):
        pr_k, pi_k = pr[powers, d], pi[powers, d]
        wr = cr[d][None] * pr_k[:, :, None, :] - ci[d][None] * pi_k[:, :, None, :]
        wi = -(cr[d][None] * pi_k[:, :, None, :] + ci[d][None] * pr_k[:, :, None, :])
        w = jnp.concatenate([wr, wi], -1)
        return jnp.transpose(w, (1, 3, 0, 2)).reshape(C_N_GROUPS, 2 * P, L * C)

    w_state = jnp.concatenate([outw(0, np.arange(1, L + 1)), outw(1, np.arange(L, 0, -1))], 1)

    nq = C_N_GROUPS // _C_GL
    npair = _C_GL // 2
    blk = L * _C_GL * C
    lane = jnp.arange(LANES)
    gsel = (lane[None, :] // C == jnp.arange(_C_GL)[:, None])
    src = jnp.swapaxes(lags, -1, -2).astype(BF).reshape(2 * L - 1, nq, _C_GL, C, C)
    src = jnp.where(gsel[None, None, :, None, :], jnp.tile(src, (1, 1, 1, 1, _C_GL)), 0)
    w_lag = jnp.transpose(src.reshape(2 * L - 1, nq, LANES, LANES), (1, 0, 2, 3))
    psel = (lane[None, :] // P == jnp.arange(2)[:, None])
    ksel = jnp.eye(npair, dtype=bool)
    src = jnp.tile(w_sum.astype(BF).reshape(nq, npair, 2, L, C, 1, 4, P), (1, 1, 1, 1, 1, npair, 1, 2))
    src = jnp.where(psel[None, None, :, None, None, None, None, :]
                    & ksel[None, :, None, None, None, :, None, None], src, 0)
    w_sum = jnp.transpose(src, (0, 3, 1, 2, 4, 5, 6, 7)).reshape(nq, blk, _C_GL * 4 * P)
    osel = (lane[None, None, :] // C
            == (2 * jnp.arange(npair)[:, None, None] + jnp.arange(2)[None, :, None]))
    src = jnp.tile(w_state.astype(BF).reshape(nq, npair, 2, 4, P, L, C), (1, 1, 1, 1, 1, 1, _C_GL))
    src = jnp.where(osel[None, :, :, None, None, None, :], src, 0)
    w_state = jnp.transpose(src, (0, 1, 3, 2, 4, 5, 6)).reshape(nq, _C_GL * 4 * P, blk)
    dec = jnp.stack([pr[L, 0], pi[L, 0], pr[L, 1], pi[L, 1]], 0).reshape(4, nq * npair, 2 * P)
    dec = jnp.broadcast_to(jnp.transpose(dec, (1, 0, 2))[:, :, None, :], (nq * npair, 4, 8, 2 * P))
    return w_sum.astype(BF), w_lag, w_state.astype(BF), dec


def _chunk_rows(x_ref, nb, rc):
    L = C_CHUNK
    rows = [jnp.concatenate([x_ref[b, pl.ds(j, rc, stride=L), :].astype(BF) for j in range(L)], axis=1)
            for b in range(nb)]
    return jnp.concatenate(rows, axis=0)


def _s5_sum_kernel(x_ref, w_ref, s_ref, *, nb, rc):
    s = _dot(_chunk_rows(x_ref, nb, rc), w_ref[...])
    for b in range(nb):
        for t in range(s_ref.shape[0]):
            s_ref[t, pl.ds(b, rc, stride=nb), :] = s[b * rc:(b + 1) * rc, LANES * t:LANES * (t + 1)]


def _s5_scan_kernel(s_ref, dec_ref, e_ref, *, nc, nb):
    units = s_ref.shape[0] // 4
    dec = [[dec_ref[u, k, 0:nb, :] for k in range(4)] for u in range(units)]

    def body(t, carry):
        rf = pl.ds(pl.multiple_of(t * nb, nb), nb)
        rb = pl.ds(pl.multiple_of((nc - 1 - t) * nb, nb), nb)
        new = []
        for u in range(units):
            fr, fi, br, bi = carry[4 * u:4 * u + 4]
            arf, aif, arb, aib = dec[u]
            e_ref[4 * u, rf, :] = fr
            e_ref[4 * u + 1, rf, :] = fi
            e_ref[4 * u + 2, rb, :] = br
            e_ref[4 * u + 3, rb, :] = bi
            new += [fr * arf - fi * aif + s_ref[4 * u, rf, :], fi * arf + fr * aif + s_ref[4 * u + 1, rf, :],
                    br * arb - bi * aib + s_ref[4 * u + 2, rb, :], bi * arb + br * aib + s_ref[4 * u + 3, rb, :]]
        return tuple(new)

    z = jnp.zeros((nb, LANES), F32)
    lax.fori_loop(0, nc, body, (z,) * (4 * units))


def _s5_out_kernel(x_ref, e_ref, lag_ref, wc_ref, y_ref, wi_ref, *, nb, rc):
    L = C_CHUNK
    half = pl.program_id(2)

    @pl.when((pl.program_id(1) == 0) & (half == 0))
    def _():
        for j in range(L):
            for i in range(L):
                wi_ref[i // (L // 2), LANES * j:LANES * (j + 1), LANES * (i % (L // 2)):LANES * (i % (L // 2) + 1)] = (
                    lag_ref[i - j + L - 1])

    xc = _chunk_rows(x_ref, nb, rc)
    ec = jnp.concatenate(
        [jnp.concatenate([e_ref[t, pl.ds(b, rc, stride=nb), :].astype(BF) for t in range(e_ref.shape[0])], axis=1)
         for b in range(nb)], axis=0)
    y = _dot(xc, wi_ref[half]) + _dot(ec, wc_ref[...])
    for b in range(nb):
        for ii in range(L // 2):
            y_ref[b, pl.ds(half * (L // 2) + ii, rc, stride=L), :] = y[b * rc:(b + 1) * rc, LANES * ii:LANES * (ii + 1)]


def _s5(x, b, s, w_sum, w_lag, w_state, dec):
    L = C_CHUNK
    nc = s // L
    nq = D_MODEL // LANES
    nsl = w_sum.shape[2] // LANES
    blk = L * LANES
    rc = _tile(nc, max(8, 512 // b))
    x3 = x.reshape(b, s, D_MODEL)
    x_spec = pl.BlockSpec((b, rc * L, LANES), lambda q, c, *_: (0, c, q))
    sums = pl.pallas_call(
        functools.partial(_s5_sum_kernel, nb=b, rc=rc),
        grid=(nq, nc // rc),
        in_specs=[x_spec, pl.BlockSpec((None, blk, nsl * LANES), lambda q, c: (q, 0, 0))],
        out_specs=pl.BlockSpec((nsl, rc * b, LANES), lambda q, c: (q, c, 0)),
        out_shape=jax.ShapeDtypeStruct((nq * nsl, nc * b, LANES), F32),
        compiler_params=_params("parallel", "arbitrary"),
        name="s5_sum",
    )(x3, w_sum)
    upb = 2
    ent = pl.pallas_call(
        functools.partial(_s5_scan_kernel, nc=nc, nb=b),
        grid=(nq * nsl // (4 * upb),),
        in_specs=[pl.BlockSpec((4 * upb, nc * b, LANES), lambda i: (i, 0, 0)),
                  pl.BlockSpec((upb, 4, 8, LANES), lambda i: (i, 0, 0, 0))],
        out_specs=pl.BlockSpec((4 * upb, nc * b, LANES), lambda i: (i, 0, 0)),
        out_shape=jax.ShapeDtypeStruct((nq * nsl, nc * b, LANES), F32),
        compiler_params=_params("parallel"),
        name="s5_scan",
    )(sums, dec)
    y = pl.pallas_call(
        functools.partial(_s5_out_kernel, nb=b, rc=rc),
        grid=(nq, nc // rc, 2),
        in_specs=[x_spec,
                  pl.BlockSpec((nsl, rc * b, LANES), lambda q, c, h: (q, c, 0)),
                  pl.BlockSpec((None, 2 * L - 1, LANES, LANES), lambda q, c, h: (q, 0, 0, 0)),
                  pl.BlockSpec((None, nsl * LANES, blk // 2), lambda q, c, h: (q, 0, h))],
        out_specs=pl.BlockSpec((b, rc * L, LANES), lambda q, c, h: (0, c, q)),
        out_shape=jax.ShapeDtypeStruct((b, s, D_MODEL), F32),
        scratch_shapes=[pltpu.VMEM((2, blk, blk // 2), BF)],
        compiler_params=_params("arbitrary", "arbitrary", "arbitrary"),
        name="s5_out",
    )(x3, ent, w_lag, w_state)
    return y.reshape(b * s, D_MODEL)


_D_T = 512


def _prep_d(w_qkv, norm_gain, w_o, lambda_init):
    qk_w = D_HEADS * 2 * D_QK_DIM
    wq = w_qkv[:, :qk_w] * (D_QK_DIM ** -0.5 * LOG2E)
    wv = _pad_heads(w_qkv[:, 2 * qk_w:], D_HEADS, D_V_DIM)
    w = jnp.concatenate([wq, w_qkv[:, qk_w:2 * qk_w], wv], 1).astype(BF)
    gain = jnp.concatenate([norm_gain * (1.0 - lambda_init), jnp.zeros((LANES - D_V_DIM,), F32)])[None, :]
    wo = w_o.reshape(D_HEADS, D_V_DIM, D_MODEL)
    wo = jnp.concatenate([wo, jnp.zeros_like(wo)], 1).reshape(D_HEADS * LANES, D_MODEL).astype(BF)
    return w, gain, wo


def _diff_bias(rel_bias, t):
    rel0 = (jnp.arange(2 * t) + t) % (2 * t) - t
    rel = jnp.arange(-2, 3)[:, None] * t + rel0[None, :]
    return jnp.transpose(rel_bias[_bucket(rel)] * LOG2E, (2, 0, 1)).astype(F32)


def _flash_d_kernel(q_ref, k_ref, v_ref, vec_ref, lam_ref, gain_ref, o_ref, m_ref, acc_ref, bias_ref, *, kt, nk,
                    lambda_init):
    t = q_ref.shape[0]
    i = pl.program_id(2)

    @pl.when(i == 0)
    def _():
        for d in range(bias_ref.shape[0]):
            full = jnp.broadcast_to(vec_ref[d:d + 1, :], (t, 2 * t))
            bias_ref[d] = pltpu.roll(full, 0, 1, stride=1, stride_axis=0)[:, :t]

    qv = q_ref[...].astype(F32)
    lane = lax.broadcasted_iota(I32, (t, LANES), 1)
    lo = (pl.program_id(1) % 2) * (2 * D_QK_DIM)
    q0 = jnp.where((lane >= lo) & (lane < lo + D_QK_DIM), qv, 0.0).astype(BF)
    q1 = jnp.where((lane >= lo + D_QK_DIM) & (lane < lo + 2 * D_QK_DIM), qv, 0.0).astype(BF)
    q = jnp.concatenate([q0, q1], axis=0)
    _softmax_init(m_ref, acc_ref)

    def body(j, carry):
        off = pl.multiple_of(j * (kt * t), kt * t)
        bias = jnp.concatenate([bias_ref[jnp.clip(j * kt + c - i, -2, 2) + 2] for c in range(kt)], axis=1)
        s = _dot_nt(q, k_ref[pl.ds(off, kt * t), :])
        s = (s.reshape(2, t, kt * t) + bias[None]).reshape(2 * t, kt * t)
        _softmax_step(s, v_ref[pl.ds(off, kt * t), :], m_ref, acc_ref)
        return carry

    lax.fori_loop(0, nk, body, 0, unroll=4)
    lf = lam_ref[...]
    lam = (jnp.exp(jnp.sum(lf[0:1] * lf[1:2], keepdims=True))
           - jnp.exp(jnp.sum(lf[2:3] * lf[3:4], keepdims=True)) + lambda_init)
    acc = acc_ref[...]
    on = acc / acc[:, _ONE_LANE:_ONE_LANE + 1]
    o = jnp.where(lane < D_V_DIM, on[:t] - lam * on[t:], 0.0)
    ms = jnp.sum(o * o, -1, keepdims=True) * (1.0 / D_V_DIM)
    o_ref[...] = (o * lax.rsqrt(ms + RMS_EPS) * gain_ref[...]).astype(BF)


def _flash_d(qkv, bias, lam, gain, lambda_init):
    b, s, _ = qkv.shape
    t = bias.shape[-1] // 2
    kt = 2 if s % (2 * t) == 0 else 1
    return pl.pallas_call(
        functools.partial(_flash_d_kernel, kt=kt, nk=s // (kt * t), lambda_init=lambda_init),
        grid=(b, D_HEADS, s // t),
        in_specs=[pl.BlockSpec((None, t, LANES), lambda bi, h, i: (bi, i, h // 2)),
                  pl.BlockSpec((None, s, LANES), lambda bi, h, i: (bi, 0, D_HEADS // 2 + h // 2)),
                  pl.BlockSpec((None, s, LANES), lambda bi, h, i: (bi, 0, D_HEADS + h)),
                  pl.BlockSpec((None, 5, 2 * t), lambda bi, h, i: (h, 0, 0)),
                  pl.BlockSpec((4, D_QK_DIM), lambda bi, h, i: (0, 0)),
                  pl.BlockSpec((1, LANES), lambda bi, h, i: (0, 0))],
        out_specs=pl.BlockSpec((None, t, LANES), lambda bi, h, i: (bi, i, h)),
        out_shape=jax.ShapeDtypeStruct((b, s, D_HEADS * LANES), BF),
        scratch_shapes=[pltpu.VMEM((2 * t, LANES), F32)] * 2 + [pltpu.VMEM((5, t, t), F32)],
        compiler_params=_params("parallel", "parallel", "arbitrary"),
        name="flash_d",
    )(qkv, qkv, qkv, bias, lam, gain)


def _cross_kernel(x_ref, kv_ref, wq_ref, wo_ref, g_ref, b_ref, wr_ref, y_ref, ybf_ref, aff_ref, *, parts):
    tp = x_ref.shape[0] // parts
    wr = wr_ref[...]
    wh = wr.astype(BF)
    wl = (wr - wh.astype(F32)).astype(BF)
    for part in range(parts):
        rows = slice(tp * part, tp * (part + 1))
        x = x_ref[rows, :]
        q = (_dot(x.astype(BF), wq_ref[...]) * (X_HEAD_DIM ** -0.5)).astype(BF)
        outs = []
        for h in range(X_HEADS):
            sl = slice(X_HEAD_DIM * h, X_HEAD_DIM * (h + 1))
            s = _dot_nt(q[:, sl], kv_ref[:, sl])
            p = jnp.exp(s - jnp.max(s, -1, keepdims=True))
            l = jnp.sum(p, -1, keepdims=True)
            vh = kv_ref[:, D_MODEL + X_HEAD_DIM * h:D_MODEL + X_HEAD_DIM * (h + 1)]
            outs.append((_dot(p.astype(BF), vh) / l).astype(BF))
        o = jnp.concatenate(outs, axis=1)
        y = _ln_rows(ALPHA * x + _dot(o, wo_ref[...]), g_ref[...], b_ref[...])
        y_ref[rows, :] = y
        yh = y.astype(BF)
        ybf_ref[rows, :] = yh
        yl = (y - yh.astype(F32)).astype(BF)
        lg = _dot_nt(wh, yh) + _dot_nt(wh, yl) + _dot_nt(wl, yh)
        e = jnp.exp(lg - jnp.max(lg, 0, keepdims=True))
        aff = e / jnp.sum(e, 0, keepdims=True)
        for c in range(tp // LANES):
            aff_ref[part * (tp // LANES) + c] = aff[:, LANES * c:LANES * (c + 1)]


def _cross(x, kv, wq, wo, g, b, wr_t, s, mem_len, tm=1024, parts=2):
    n = x.shape[0]
    tm = _tile(s, tm)
    per = s // tm
    fixed = lambda i: (0, 0)
    return pl.pallas_call(
        functools.partial(_cross_kernel, parts=parts),
        grid=(n // tm,),
        in_specs=[pl.BlockSpec((tm, D_MODEL), lambda i: (i, 0)),
                  pl.BlockSpec((mem_len, 2 * D_MODEL), lambda i: (i // per, 0)),
                  pl.BlockSpec((D_MODEL, D_MODEL), fixed), pl.BlockSpec((D_MODEL, D_MODEL), fixed),
                  pl.BlockSpec((1, D_MODEL), fixed), pl.BlockSpec((1, D_MODEL), fixed),
                  pl.BlockSpec((N_EXPERTS, D_MODEL), fixed)],
        out_specs=[pl.BlockSpec((tm, D_MODEL), lambda i: (i, 0)),
                   pl.BlockSpec((tm, D_MODEL), lambda i: (i, 0)),
                   pl.BlockSpec((tm // LANES, N_EXPERTS, LANES), lambda i: (i, 0, 0))],
        out_shape=[jax.ShapeDtypeStruct((n, D_MODEL), F32),
                   jax.ShapeDtypeStruct((n, D_MODEL), BF),
                   jax.ShapeDtypeStruct((n // LANES, N_EXPERTS, LANES), F32)],
        compiler_params=_params("parallel"),
        name="cross",
    )(x, kv, wq, wo, g, b, wr_t)


def _select_kernel(a_ref, pos_ref, st_ref, *, k, nbits):
    nt = a_ref.shape[0]
    shape = (nt, N_EXPERTS, LANES)
    kf = float(k)

    def keys():
        return lax.bitcast_convert_type(a_ref[...], I32)

    def count(mask):
        c = jnp.sum(jnp.where(mask, 1.0, 0.0), axis=0, keepdims=True)
        return jnp.sum(c, axis=2, keepdims=True)

    def value_step(it, thr):
        cand = thr | jnp.left_shift(jnp.int32(1), 30 - it)
        return jnp.where(count(keys() >= cand) >= kf, cand, thr)

    thr = lax.fori_loop(0, 31, value_step, jnp.zeros((1, N_EXPERTS, 1), I32))
    need = kf - count(keys() > thr)
    idx = lax.broadcasted_iota(I32, shape, 0) * LANES + lax.broadcasted_iota(I32, shape, 2)

    def index_step(it, ithr):
        cand = ithr | jnp.left_shift(jnp.int32(1), nbits - 1 - it)
        return jnp.where(count((keys() == thr) & (idx < cand)) < need, cand, ithr)

    ithr = lax.fori_loop(0, nbits, index_step, jnp.zeros((1, N_EXPERTS, 1), I32))
    thr2, ithr2 = thr[0], ithr[0]
    upper = jnp.where(lax.broadcasted_iota(I32, (LANES, LANES), 0) <= lax.broadcasted_iota(I32, (LANES, LANES), 1),
                      1.0, 0.0).astype(BF)
    lane = lax.broadcasted_iota(I32, (N_EXPERTS, LANES), 1)

    def tile_step(j, carry):
        kj = lax.bitcast_convert_type(a_ref[j], I32)
        sel = (kj > thr2) | ((kj == thr2) & (j * LANES + lane <= ithr2))
        m = jnp.where(sel, 1.0, 0.0)
        inc = _dot(m.astype(BF), upper)
        pos_ref[j] = jnp.where(sel, inc - m + carry, -1.0).astype(I32)
        st_ref[j] = jnp.broadcast_to(carry, (N_EXPERTS, LANES)).astype(I32)
        return carry + inc[:, LANES - 1:LANES]

    lax.fori_loop(0, nt, tile_step, jnp.zeros((N_EXPERTS, 1), F32))


def _select(aff3, k):
    nt = aff3.shape[0]
    nbits = max(1, int(math.ceil(math.log2(nt * LANES))))
    shp = jax.ShapeDtypeStruct(aff3.shape, I32)
    return pl.pallas_call(
        functools.partial(_select_kernel, k=k, nbits=nbits),
        out_shape=[shp, shp],
        compiler_params=pltpu.CompilerParams(vmem_limit_bytes=VMEM_LIMIT),
        name="select",
    )(aff3)


_GATHER_ROWS = LANES + BF16_ROWS
_GATHER_ROWS_SMALL = 3 * BF16_ROWS

_MOE_GROUP = 2


def _moe_ffn_kernel(st_ref, x_ref, pos_ref, aff_ref, wg_ref, wu_ref, wd_ref, ye_ref, buf_ref, gate_ref, *, nb, ncf,
                    nch, sub, ck):
    ep = buf_ref.shape[0]
    grp = pl.program_id(0)
    t = pl.program_id(1)

    @pl.when(t == 0)
    def _():
        buf_ref[...] = jnp.zeros_like(buf_ref)
        gate_ref[...] = jnp.zeros_like(gate_ref)

    @pl.when(t < nb)
    def _():
        def place(s, k, e, base, nrows):
            rows = pl.ds(base, nrows)
            hit = lax.broadcasted_iota(I32, (nrows, LANES), 0) == pos_ref[s, pl.ds(e, 1), :] - base
            xs = x_ref[LANES * s:LANES * (s + 1), :]
            buf_ref[k, rows, :] = buf_ref[k, rows, :] + _dot(jnp.where(hit, 1.0, 0.0).astype(BF), xs).astype(BF)
            gate_ref[k, rows, :] = gate_ref[k, rows, :] + jnp.sum(
                jnp.where(hit, aff_ref[s, pl.ds(e, 1), :], 0.0), -1, keepdims=True)

        def base_of(s, k):
            st = st_ref[grp * ep + k, t * sub + s]
            return pl.multiple_of((st // BF16_ROWS) * BF16_ROWS, BF16_ROWS)

        pairs = [(s, k) for s in range(sub) for k in range(ep)]
        few = functools.reduce(
            jnp.logical_and,
            [st_ref[grp * ep + k, t * sub + s + 1] - base_of(s, k) <= _GATHER_ROWS_SMALL for s, k in pairs])

        @pl.when(few)
        def _():
            for s, k in pairs:
                place(s, k, grp * ep + k, base_of(s, k), _GATHER_ROWS_SMALL)

        @pl.when(jnp.logical_not(few))
        def _():
            for s, k in pairs:
                place(s, k, grp * ep + k, base_of(s, k), _GATHER_ROWS)

    @pl.when(t >= nb)
    def _():
        k = (t - nb) // nch
        c = (t - nb) - k * nch

        @pl.when(c < ncf)
        def _():
            rows = pl.ds(pl.multiple_of(c * ck, ck), ck)
            xe = buf_ref[k, rows, :]
            h = (jax.nn.silu(_dot(xe, wg_ref[...])) * _dot(xe, wu_ref[...])).astype(BF)
            ye_ref[...] = (_dot(h, wd_ref[...]) * gate_ref[k, rows, :]).astype(BF)

        @pl.when(c >= ncf)
        def _():
            ye_ref[...] = jnp.zeros_like(ye_ref)


def _moe_ffn(starts, xbf, pos3, aff3, wg, wu, wd, cap, tb=2048):
    n = xbf.shape[0]
    tb = _tile(n, tb)
    nb = n // tb
    sub = tb // LANES
    ck = _tile(cap, 1024)
    ncf = cap // ck
    nch = ncf + pl.cdiv(_WIN_ROWS, ck)
    dff = wg.shape[-1]
    ep = _MOE_GROUP
    blk = lambda g, t, st: (jnp.minimum(t, nb - 1), 0)
    blk3 = lambda g, t, st: (jnp.minimum(t, nb - 1), 0, 0)
    expert = lambda g, t: g * ep + jnp.clip((t - nb) // nch, 0, ep - 1)
    wmap = lambda g, t, st: (expert(g, t), 0, 0)
    return pl.pallas_call(
        functools.partial(_moe_ffn_kernel, nb=nb, ncf=ncf, nch=nch, sub=sub, ck=ck),
        grid_spec=pltpu.PrefetchScalarGridSpec(
            num_scalar_prefetch=1,
            grid=(N_EXPERTS // ep, nb + ep * nch),
            in_specs=[pl.BlockSpec((tb, D_MODEL), blk),
                      pl.BlockSpec((sub, N_EXPERTS, LANES), blk3),
                      pl.BlockSpec((sub, N_EXPERTS, LANES), blk3),
                      pl.BlockSpec((None, D_MODEL, dff), wmap),
                      pl.BlockSpec((None, D_MODEL, dff), wmap),
                      pl.BlockSpec((None, dff, D_MODEL), wmap)],
            out_specs=pl.BlockSpec((None, ck, D_MODEL),
                                   lambda g, t, st: (expert(g, t), jnp.maximum(t - nb, 0) % nch, 0)),
            scratch_shapes=[pltpu.VMEM((ep, cap + _GATHER_ROWS, D_MODEL), BF),
                            pltpu.VMEM((ep, cap + _GATHER_ROWS, 1), F32)]),
        out_shape=jax.ShapeDtypeStruct((N_EXPERTS, nch * ck, D_MODEL), BF),
        compiler_params=_params("arbitrary", "arbitrary"),
        name="moe_ffn",
    )(starts, xbf, pos3, aff3, wg, wu, wd)


_WIN_STEP = 128
_COMB_TILES = 2
_WIN_ROWS = _WIN_STEP + (_COMB_TILES - 1) * LANES + _GATHER_ROWS


_COMB_PACK = MXU_DEPTH // _GATHER_ROWS_SMALL


def _combine_kernel(st_ref, x_ref, pos_ref, g_ref, b_ref, *rest):
    ye_refs, y_ref = rest[:N_EXPERTS], rest[N_EXPERTS]
    j = pl.program_id(0)
    pairs = [(u, e) for u in range(_COMB_TILES) for e in range(N_EXPERTS)]

    def base_of(u, e):
        return (st_ref[e, j * _COMB_TILES + u] // BF16_ROWS) * BF16_ROWS

    def rows_of(u, e, nrows):
        win = (st_ref[e, j * _COMB_TILES] // _WIN_STEP) * _WIN_STEP
        return ye_refs[e][0, pl.ds(pl.multiple_of(base_of(u, e) - win, BF16_ROWS), nrows), :]

    def finish(u, acc):
        tok = slice(LANES * u, LANES * (u + 1))
        y_ref[tok, :] = _ln_rows(ALPHA * x_ref[tok, :] + acc, g_ref[...], b_ref[...])

    few = functools.reduce(
        jnp.logical_and,
        [st_ref[e, j * _COMB_TILES + u + 1] - base_of(u, e) <= _GATHER_ROWS_SMALL for u, e in pairs])

    @pl.when(few)
    def _():
        for u in range(_COMB_TILES):
            pos = jnp.transpose(pos_ref[u])
            acc = None
            for e0 in range(0, N_EXPERTS, _COMB_PACK):
                es = list(range(e0, min(e0 + _COMB_PACK, N_EXPERTS)))
                col = lax.broadcasted_iota(I32, (LANES, len(es) * _GATHER_ROWS_SMALL), 1)
                hit = None
                for k, e in enumerate(es):
                    rel = pos[:, e:e + 1] - base_of(u, e)
                    h = col == jnp.where(rel >= 0, rel + k * _GATHER_ROWS_SMALL, -1)
                    hit = h if hit is None else hit | h
                slab = jnp.concatenate([rows_of(u, e, _GATHER_ROWS_SMALL) for e in es], axis=0)
                d = _dot(jnp.where(hit, 1.0, 0.0).astype(BF), slab)
                acc = d if acc is None else acc + d
            finish(u, acc)

    @pl.when(jnp.logical_not(few))
    def _():
        col = lax.broadcasted_iota(I32, (LANES, _GATHER_ROWS), 1)
        for u in range(_COMB_TILES):
            pos = jnp.transpose(pos_ref[u])
            acc = None
            for e in range(N_EXPERTS):
                onehot = jnp.where(col == pos[:, e:e + 1] - base_of(u, e), 1.0, 0.0).astype(BF)
                d = _dot(onehot, rows_of(u, e, _GATHER_ROWS))
                acc = d if acc is None else acc + d
            finish(u, acc)


def _combine(starts, x, pos, g, b, ye):
    n = x.shape[0]
    assert ye.shape[1] >= EC_CAPACITY * n // N_EXPERTS + _WIN_ROWS
    tt = _COMB_TILES * LANES
    row = lambda j, st: (j, 0)
    fixed = lambda j, st: (0, 0)

    def ye_spec(e):
        return pl.BlockSpec((pl.Element(1), pl.Element(_WIN_ROWS), pl.Element(D_MODEL)),
                            lambda j, st: (e, (st[e, j * _COMB_TILES] // _WIN_STEP) * _WIN_STEP, 0))

    return pl.pallas_call(
        _combine_kernel,
        grid_spec=pltpu.PrefetchScalarGridSpec(
            num_scalar_prefetch=1,
            grid=(n // tt,),
            in_specs=[pl.BlockSpec((tt, D_MODEL), row),
                      pl.BlockSpec((_COMB_TILES, N_EXPERTS, LANES), lambda j, st: (j, 0, 0)),
                      pl.BlockSpec((1, D_MODEL), fixed), pl.BlockSpec((1, D_MODEL), fixed)]
            + [ye_spec(e) for e in range(N_EXPERTS)],
            out_specs=pl.BlockSpec((tt, D_MODEL), row)),
        out_shape=jax.ShapeDtypeStruct((n, D_MODEL), F32),
        compiler_params=_params("arbitrary"),
        name="combine",
    )(starts, x, pos, g, b, *([ye] * N_EXPERTS))


def _moe(x, xg, aff3, wg, wu, wd, g, b):
    n = x.shape[0]
    cap = EC_CAPACITY * n // N_EXPERTS
    pos3, st3 = _select(aff3, cap)
    starts = jnp.concatenate([jnp.transpose(st3[:, :, 0]), jnp.full((N_EXPERTS, 1), cap, I32)], axis=1)
    ye = _moe_ffn(starts, xg, pos3, aff3, wg, wu, wd, cap)
    return _combine(starts, x, pos3, g, b, ye)


def _prep_weights(p):
    w = {}
    w['a'] = [_prep_a(p['a_w_qkv'][j], p['a_q_gain'][j], p['a_k_gain'][j], p['a_w_o'][j])
              for j in range(p['a_w_qkv'].shape[0])]
    w['b'] = [_prep_b(p['b_w_qkv'][j], p['b_w_o'][j]) for j in range(p['b_w_qkv'].shape[0])]
    w['c'] = [_prep_c(p['c_lam_re'][j], p['c_lam_im'][j], p['c_log_dt'][j], p['c_b_re'][j], p['c_b_im'][j],
                      p['c_c_re'][j], p['c_c_im'][j]) + (p['c_d'][j][None, :], p['c_w_glu'][j].astype(BF))
              for j in range(p['c_lam_re'].shape[0])]
    w['d'] = []
    for j in range(p['d_w_qkv'].shape[0]):
        layer = N_MIXERS * j + 3
        lambda_init = 0.8 - 0.6 * math.exp(-0.3 * layer)
        w['d'].append(_prep_d(p['d_w_qkv'][j], p['d_norm_gain'][j], p['d_w_o'][j], lambda_init)
                      + (p['d_lam'][j].astype(F32), lambda_init))
    w['diff_bias'] = _diff_bias(p['rel_bias'], _D_T)
    w['x_w_q'] = p['x_w_q'].astype(BF)
    w['x_w_kv'] = p['x_w_kv'].astype(BF)
    w['x_w_o'] = p['x_w_o'].astype(BF)
    w['router_t'] = jnp.transpose(p['moe_w_router'], (0, 2, 1)).astype(F32)
    w['moe_w_gate'] = p['moe_w_gate'].astype(BF)
    w['moe_w_up'] = p['moe_w_up'].astype(BF)
    w['moe_w_down'] = p['moe_w_down'].astype(BF)
    return w


def _trunk(x, mem, p, w):
    b, s, _ = x.shape
    n = b * s
    mem_len = mem.shape[1]
    x = x.reshape(n, D_MODEL)
    mem2 = mem.reshape(b * mem_len, D_MODEL)
    ln_g, ln_b = p['ln_g'], p['ln_b']
    for i in range(DEPTH):
        m, j = i % N_MIXERS, i // N_MIXERS
        g0, b0 = ln_g[i, 0][None, :], ln_b[i, 0][None, :]
        if m == 0:
            wa, gains, wo = w['a'][j]
            qkv = _proj_a(x, wa, _rope_table(s), gains, s)
            o = _flash_a(qkv.reshape(b, s, _A_COLS))
            x = _post(x, o.reshape(n, A_HEADS * A_HEAD_DIM), wo, g0, b0)
        elif m == 1:
            wb, wo = w['b'][j]
            qkv = _proj(x, wb).reshape(b, s, wb.shape[1])
            os_, ls_ = [], []
            for g in range(len(B_PATTERNS)):
                t = _tile(s, _B_T)
                o, lse = _band_attention(qkv, _band_bias(p['rel_bias'], g, t), g, t)
                os_.append(o)
                ls_.append(lse)
            x = _post_b(x, os_, ls_, wo, g0, b0)
        elif m == 2:
            w_sum, w_lag, w_state, dec, dskip, wglu = w['c'][j]
            ys = _s5(x, b, s, w_sum, w_lag, w_state, dec)
            x = _post_c(x, ys, dskip, wglu, g0, b0)
        else:
            wd, gain, wo, lam, lambda_init = w['d'][j]
            qkv = _proj(x, wd, ones_from=D_HEADS * LANES).reshape(b, s, wd.shape[1])
            o = _flash_d(qkv, w['diff_bias'], lam, gain, lambda_init)
            x = _post(x, o.reshape(n, D_HEADS * LANES), wo, g0, b0)
        kv = _proj(mem2, w['x_w_kv'][i], tm=mem_len)
        x, xbf, aff3 = _cross(x, kv, w['x_w_q'][i], w['x_w_o'][i], ln_g[i, 1][None, :], ln_b[i, 1][None, :],
                              w['router_t'][i], s, mem_len)
        x = _moe(x, xbf, aff3, w['moe_w_gate'][i], w['moe_w_up'][i], w['moe_w_down'][i],
                 ln_g[i, 2][None, :], ln_b[i, 2][None, :])
    return x.reshape(b, s, D_MODEL)


def kernel(x_prompt, x_sample, mem_prompt, mem_sample, rel_bias, ln_g, ln_b, a_w_qkv, a_q_gain, a_k_gain, a_w_o, b_w_qkv, b_w_o, c_lam_re, c_lam_im, c_log_dt, c_b_re, c_b_im, c_c_re, c_c_im, c_d, c_w_glu, d_w_qkv, d_lam, d_norm_gain, d_w_o, x_w_q, x_w_kv, x_w_o, moe_w_router, moe_w_gate, moe_w_up, moe_w_down):
    p = dict(rel_bias=rel_bias, ln_g=ln_g, ln_b=ln_b,
             a_w_qkv=a_w_qkv, a_q_gain=a_q_gain, a_k_gain=a_k_gain, a_w_o=a_w_o,
             b_w_qkv=b_w_qkv, b_w_o=b_w_o,
             c_lam_re=c_lam_re, c_lam_im=c_lam_im, c_log_dt=c_log_dt, c_b_re=c_b_re, c_b_im=c_b_im,
             c_c_re=c_c_re, c_c_im=c_c_im, c_d=c_d, c_w_glu=c_w_glu,
             d_w_qkv=d_w_qkv, d_lam=d_lam, d_norm_gain=d_norm_gain, d_w_o=d_w_o,
             x_w_q=x_w_q, x_w_kv=x_w_kv, x_w_o=x_w_o,
             moe_w_router=moe_w_router, moe_w_gate=moe_w_gate, moe_w_up=moe_w_up, moe_w_down=moe_w_down)
    w = _prep_weights(p)
    return (_trunk(x_prompt, mem_prompt, p, w), _trunk(x_sample, mem_sample, p, w))
```

```python
import functools
import math

import numpy as np
import jax
import jax.numpy as jnp
from jax import lax
from jax.experimental import pallas as pl
from jax.experimental.pallas import tpu as pltpu

F32 = jnp.float32
BF = jnp.bfloat16
I32 = jnp.int32

D_MODEL = 1024
DEPTH = 4
GRID_W = 64
N_MIXERS = 4
LN_EPS = 1e-5
RMS_EPS = 1e-6
ALPHA = (2.0 * DEPTH) ** 0.25

A_HEADS = 16
A_KV_HEADS = 4
A_HEAD_DIM = 64
ROPE_BASE = 10000.0

B_PATTERNS = ((128, 1), (512, 4), (2048, 16))
B_HEADS_PER_GROUP = 4
B_HEAD_DIM = 64

C_GROUP = 16
C_N_GROUPS = D_MODEL // C_GROUP
C_STATE = 64
C_CHUNK = 16
D_HEADS = 12
D_QK_DIM = 32
D_V_DIM = 64

REL_BUCKETS = 32
REL_MAX_DIST = 128

X_HEADS = 4
X_HEAD_DIM = D_MODEL // X_HEADS

N_EXPERTS = 16
EC_CAPACITY = 2

LANES = 128
BF16_ROWS = 16
MXU_DEPTH = 256
_C_GL = LANES // C_GROUP
VMEM_LIMIT = 56 * 1024 * 1024
NEG = -1e30
LOG2E = math.log2(math.e)


def _params(*sem):
    return pltpu.CompilerParams(dimension_semantics=sem, vmem_limit_bytes=VMEM_LIMIT)


def _tile(n, pref):
    t = min(n, pref)
    assert n % t == 0, (n, pref)
    return t


def _ln_rows(v, g, b):
    mu = jnp.mean(v, -1, keepdims=True)
    c = v - mu
    var = jnp.mean(c * c, -1, keepdims=True)
    return c * lax.rsqrt(var + LN_EPS) * g + b


def _dot_nt(a, b):
    return lax.dot_general(a, b, (((1,), (1,)), ((), ())), preferred_element_type=F32)


def _dot(a, b):
    return jnp.dot(a, b, preferred_element_type=F32)


_ONE_LANE = 64


def _with_ones_lane(y):
    lane = lax.broadcasted_iota(I32, y.shape, 1)
    return jnp.where(lane % LANES == _ONE_LANE, 1.0, y)


def _proj_kernel(x_ref, w_ref, o_ref, *, ones_from):
    y = _dot(x_ref[...].astype(BF), w_ref[...])
    if ones_from is None:
        o_ref[...] = y.astype(o_ref.dtype)
    else:
        o_ref[:, :ones_from] = y[:, :ones_from].astype(o_ref.dtype)
        o_ref[:, ones_from:] = _with_ones_lane(y[:, ones_from:]).astype(o_ref.dtype)


def _proj(x, w, tm=512, ones_from=None):
    n, k = x.shape
    m = w.shape[1]
    tm = _tile(n, tm)
    return pl.pallas_call(
        functools.partial(_proj_kernel, ones_from=ones_from),
        grid=(n // tm,),
        in_specs=[pl.BlockSpec((tm, k), lambda i: (i, 0)),
                  pl.BlockSpec((k, m), lambda i: (0, 0))],
        out_specs=pl.BlockSpec((tm, m), lambda i: (i, 0)),
        out_shape=jax.ShapeDtypeStruct((n, m), BF),
        compiler_params=_params("parallel"),
        name="proj",
    )(x, w)


def _post_kernel(x_ref, o_ref, w_ref, g_ref, b_ref, y_ref):
    h = _dot(o_ref[...], w_ref[...])
    y_ref[...] = _ln_rows(ALPHA * x_ref[...] + h, g_ref[...], b_ref[...])


def _post(x, o, w, g, b, tm=512):
    n = x.shape[0]
    ko = o.shape[1]
    tm = _tile(n, tm)
    return pl.pallas_call(
        _post_kernel,
        grid=(n // tm,),
        in_specs=[pl.BlockSpec((tm, D_MODEL), lambda i: (i, 0)),
                  pl.BlockSpec((tm, ko), lambda i: (i, 0)),
                  pl.BlockSpec((ko, D_MODEL), lambda i: (0, 0)),
                  pl.BlockSpec((1, D_MODEL), lambda i: (0, 0)),
                  pl.BlockSpec((1, D_MODEL), lambda i: (0, 0))],
        out_specs=pl.BlockSpec((tm, D_MODEL), lambda i: (i, 0)),
        out_shape=jax.ShapeDtypeStruct((n, D_MODEL), F32),
        compiler_params=_params("parallel"),
        name="post",
    )(x, o, w, g, b)


def _post_b_kernel(x_ref, o0_ref, o1_ref, o2_ref, l0_ref, l1_ref, l2_ref, w_ref, g_ref, b_ref, y_ref):
    l0, l1, l2 = l0_ref[...], l1_ref[...], l2_ref[...]
    m = jnp.maximum(jnp.maximum(l0, l1), l2)
    e0, e1, e2 = jnp.exp(l0 - m), jnp.exp(l1 - m), jnp.exp(l2 - m)
    inv = 1.0 / (e0 + e1 + e2)
    gw = B_HEADS_PER_GROUP * B_HEAD_DIM
    h = _dot((o0_ref[...].astype(F32) * (e0 * inv)).astype(BF), w_ref[0:gw, :])
    h = h + _dot((o1_ref[...].astype(F32) * (e1 * inv)).astype(BF), w_ref[gw:2 * gw, :])
    h = h + _dot((o2_ref[...].astype(F32) * (e2 * inv)).astype(BF), w_ref[2 * gw:3 * gw, :])
    y_ref[...] = _ln_rows(ALPHA * x_ref[...] + h, g_ref[...], b_ref[...])


def _post_b(x, os_, ls_, w, g, b, tm=512):
    n = x.shape[0]
    tm = _tile(n, tm)
    gw = B_HEADS_PER_GROUP * B_HEAD_DIM
    row = lambda i: (i, 0)
    fixed = lambda i: (0, 0)
    return pl.pallas_call(
        _post_b_kernel,
        grid=(n // tm,),
        in_specs=[pl.BlockSpec((tm, D_MODEL), row)] + [pl.BlockSpec((tm, gw), row)] * 6
        + [pl.BlockSpec((3 * gw, D_MODEL), fixed), pl.BlockSpec((1, D_MODEL), fixed),
           pl.BlockSpec((1, D_MODEL), fixed)],
        out_specs=pl.BlockSpec((tm, D_MODEL), row),
        out_shape=jax.ShapeDtypeStruct((n, D_MODEL), F32),
        compiler_params=_params("parallel"),
        name="post_b",
    )(x, *os_, *ls_, w, g, b)


def _post_c_kernel(x_ref, ys_ref, d_ref, w_ref, g_ref, b_ref, y_ref):
    x = x_ref[...]
    z = jax.nn.gelu(ys_ref[...] + d_ref[...] * x).astype(BF)
    h = _dot(z, w_ref[...])
    hh = h[:, :D_MODEL] * jax.nn.sigmoid(h[:, D_MODEL:])
    y_ref[...] = _ln_rows(ALPHA * x + hh, g_ref[...], b_ref[...])


def _post_c(x, ys, d, w, g, b, tm=512):
    n = x.shape[0]
    tm = _tile(n, tm)
    row = lambda i: (i, 0)
    fixed = lambda i: (0, 0)
    return pl.pallas_call(
        _post_c_kernel,
        grid=(n // tm,),
        in_specs=[pl.BlockSpec((tm, D_MODEL), row), pl.BlockSpec((tm, D_MODEL), row),
                  pl.BlockSpec((1, D_MODEL), fixed), pl.BlockSpec((D_MODEL, 2 * D_MODEL), fixed),
                  pl.BlockSpec((1, D_MODEL), fixed), pl.BlockSpec((1, D_MODEL), fixed)],
        out_specs=pl.BlockSpec((tm, D_MODEL), row),
        out_shape=jax.ShapeDtypeStruct((n, D_MODEL), F32),
        compiler_params=_params("parallel"),
        name="post_c",
    )(x, ys, d, w, g, b)


_A_QK_TILES = A_HEADS + A_KV_HEADS
_A_COLS = (_A_QK_TILES + A_KV_HEADS) * LANES


def _rope_partner():
    d = np.arange(A_HEAD_DIM)
    e = d % (A_HEAD_DIM // 2)
    lo = e < A_HEAD_DIM // 4
    return np.where(lo, d + A_HEAD_DIM // 4, d - A_HEAD_DIM // 4), np.where(lo, -1.0, 1.0).astype(np.float32)


def _prep_a(w_qkv, q_gain, k_gain, w_o):
    partner, sign = _rope_partner()
    nqk = _A_QK_TILES * A_HEAD_DIM
    wqk = w_qkv[:, :nqk].reshape(D_MODEL, _A_QK_TILES, A_HEAD_DIM)
    wsw = wqk[:, :, partner] * sign
    wqk = jnp.concatenate([wqk, wsw], -1).reshape(D_MODEL, _A_QK_TILES * LANES)
    wv = w_qkv[:, nqk:].reshape(D_MODEL, A_KV_HEADS, A_HEAD_DIM)
    wv = jnp.concatenate([wv, jnp.zeros_like(wv)], -1).reshape(D_MODEL, A_KV_HEADS * LANES)
    w = jnp.concatenate([wqk, wv], 1).astype(BF)
    gq = jnp.concatenate([q_gain, q_gain[partner]]) * (A_HEAD_DIM ** -0.5 * 0.5 * LOG2E)
    gk = jnp.concatenate([k_gain, k_gain[partner]])
    gains = jnp.stack([gq, gk], 0)
    return w, gains, w_o.astype(BF)


def _rope_table(s):
    pos = jnp.arange(s)
    rows, cols = (pos // GRID_W).astype(F32), (pos % GRID_W).astype(F32)
    half = A_HEAD_DIM // 2
    freqs = ROPE_BASE ** (-jnp.arange(0, half, 2, dtype=F32) / half)
    ang_r = rows[:, None] * freqs
    ang_c = cols[:, None] * freqs
    ang = jnp.concatenate([ang_r, ang_r, ang_c, ang_c], -1)
    return jnp.concatenate([jnp.cos(ang), jnp.sin(ang)], -1)


def _proj_a_kernel(x_ref, w_ref, cs_ref, g_ref, o_ref):
    xb = x_ref[...].astype(BF)
    cs = cs_ref[...]
    gq = cs * g_ref[0:1, :]
    gk = cs * g_ref[1:2, :]
    per = 4
    ones = jnp.ones((LANES, LANES), BF)
    for g in range(_A_QK_TILES // per):
        y = _dot(xb, w_ref[:, LANES * per * g:LANES * per * (g + 1)])
        for hh in range(per):
            h = per * g + hh
            t = y[:, LANES * hh:LANES * (hh + 1)]
            t2 = t * t
            hi = t2.astype(BF)
            ss = _dot(hi, ones) + _dot((t2 - hi.astype(F32)).astype(BF), ones)
            r = lax.rsqrt(ss * (1.0 / LANES) + RMS_EPS)
            e = t * r * (gq if h < A_HEADS else gk)
            o_ref[:, LANES * h:LANES * (h + 1)] = (e + pltpu.roll(e, LANES // 2, 1)).astype(BF)
    yv = _dot(xb, w_ref[:, _A_QK_TILES * LANES:])
    o_ref[:, _A_QK_TILES * LANES:] = _with_ones_lane(yv).astype(BF)


def _proj_a(x, w, cs, gains, s, tm=512):
    n = x.shape[0]
    tm = _tile(s, tm)
    per = s // tm
    return pl.pallas_call(
        _proj_a_kernel,
        grid=(n // tm,),
        in_specs=[pl.BlockSpec((tm, D_MODEL), lambda i: (i, 0)),
                  pl.BlockSpec((D_MODEL, _A_COLS), lambda i: (0, 0)),
                  pl.BlockSpec((tm, LANES), lambda i: (i % per, 0)),
                  pl.BlockSpec((2, LANES), lambda i: (0, 0))],
        out_specs=pl.BlockSpec((tm, _A_COLS), lambda i: (i, 0)),
        out_shape=jax.ShapeDtypeStruct((n, _A_COLS), BF),
        compiler_params=_params("parallel"),
        name="proj_a",
    )(x, w, cs, gains)


def _softmax_step(s, v, m_ref, acc_ref):
    tk = s.shape[1]
    m_old = m_ref[...]
    m_new = jnp.maximum(m_old, jnp.max(s, -1, keepdims=True))
    p = jnp.concatenate([jnp.exp2(s[:, LANES * c:LANES * (c + 1)] - m_new).astype(BF) for c in range(tk // LANES)],
                        axis=1)
    acc_ref[...] = jnp.exp2(m_old - m_new) * acc_ref[...] + _dot(p, v)
    m_ref[...] = m_new


def _softmax_init(m_ref, acc_ref):
    m_ref[...] = jnp.full(m_ref.shape, -jnp.inf, F32)
    acc_ref[...] = jnp.zeros_like(acc_ref)


def _flash_a_kernel(q_ref, k_ref, v_ref, o_ref, m_ref, acc_ref, *, tk, nk):
    tq = q_ref.shape[0]
    rep = A_HEADS // A_KV_HEADS
    q = jnp.concatenate([q_ref[:, LANES * r:LANES * (r + 1)] for r in range(rep)], axis=0)
    _softmax_init(m_ref, acc_ref)

    def body(j, carry):
        off = pl.multiple_of(j * tk, tk)
        _softmax_step(_dot_nt(q, k_ref[pl.ds(off, tk), :]), v_ref[pl.ds(off, tk), :], m_ref, acc_ref)
        return carry

    lax.fori_loop(0, nk, body, 0, unroll=4)
    acc = acc_ref[...]
    o = acc / acc[:, _ONE_LANE:_ONE_LANE + 1]
    first = lax.broadcasted_iota(I32, (tq, LANES), 1) < A_HEAD_DIM
    for p in range(rep // 2):
        lo = o[2 * p * tq:(2 * p + 1) * tq]
        hi = pltpu.roll(o[(2 * p + 1) * tq:(2 * p + 2) * tq], A_HEAD_DIM, 1)
        o_ref[:, LANES * p:LANES * (p + 1)] = jnp.where(first, lo, hi).astype(BF)


def _flash_a(qkv, tq=256, tk=2048):
    b, s, _ = qkv.shape
    tq, tk = _tile(s, tq), _tile(s, tk)
    rep = A_HEADS // A_KV_HEADS
    gw = rep * LANES
    return pl.pallas_call(
        functools.partial(_flash_a_kernel, tk=tk, nk=s // tk),
        grid=(b, A_KV_HEADS, s // tq),
        in_specs=[pl.BlockSpec((None, tq, gw), lambda bi, g, i: (bi, i, g)),
                  pl.BlockSpec((None, s, LANES), lambda bi, g, i: (bi, 0, A_HEADS + g)),
                  pl.BlockSpec((None, s, LANES), lambda bi, g, i: (bi, 0, _A_QK_TILES + g))],
        out_specs=pl.BlockSpec((None, tq, rep * A_HEAD_DIM), lambda bi, g, i: (bi, i, g)),
        out_shape=jax.ShapeDtypeStruct((b, s, A_HEADS * A_HEAD_DIM), BF),
        scratch_shapes=[pltpu.VMEM((rep * tq, LANES), F32)] * 2,
        compiler_params=_params("parallel", "parallel", "arbitrary"),
        name="flash_a",
    )(qkv, qkv, qkv)


def _bucket(rel):
    half = REL_BUCKETS // 2
    max_exact = half // 2
    n = jnp.abs(rel)
    large = max_exact + (jnp.log(jnp.maximum(n, 1).astype(F32) / max_exact)
                         / math.log(REL_MAX_DIST / max_exact) * (half - max_exact)).astype(I32)
    large = jnp.minimum(large, half - 1)
    return jnp.where(rel > 0, half, 0) + jnp.where(n < max_exact, n, large)


_B_GW = B_HEADS_PER_GROUP * B_HEAD_DIM
_B_T = 256


def _pad_heads(w, heads, dim):
    w = w.reshape(w.shape[0], heads, dim)
    return jnp.concatenate([w, jnp.zeros((w.shape[0], heads, LANES - dim), w.dtype)], -1).reshape(
        w.shape[0], heads * LANES)


def _prep_b(w_qkv, w_o):
    c = len(B_PATTERNS) * B_HEADS_PER_GROUP * B_HEAD_DIM
    w = jnp.concatenate([w_qkv[:, :c] * (B_HEAD_DIM ** -0.5), w_qkv[:, c:]], 1).astype(BF)
    return w, w_o.astype(BF)


def _toeplitz(vec, t):
    flat = jnp.tile(vec, (1,) * (vec.ndim - 1) + (t,))[..., :t * (2 * t - 1)]
    return flat.reshape(vec.shape[:-1] + (t, 2 * t - 1))[..., :t]


def _band_tiles(g, t):
    window, dil = B_PATTERNS[g]
    return pl.cdiv((window // (2 * dil)) * dil, t)


def _band_bias(rel_bias, g, t):
    window, dil = B_PATTERNS[g]
    reach = (window // (2 * dil)) * dil
    n = _band_tiles(g, t)
    rel0 = (jnp.arange(2 * t) + t) % (2 * t) - t
    rel = jnp.arange(-n, n + 1)[:, None] * t + rel0[None, :]
    bias = rel_bias[_bucket(rel)][:, :, g * B_HEADS_PER_GROUP:(g + 1) * B_HEADS_PER_GROUP]
    bias = jnp.where(((rel % dil == 0) & (jnp.abs(rel) <= reach))[:, :, None], bias, NEG)
    return _toeplitz(jnp.transpose(bias, (2, 0, 1)).astype(F32), t)


def _band_kernel(q_ref, k_ref, v_ref, bias_ref, o_ref, lse_ref, *, n, nk):
    t = q_ref.shape[0]
    i = pl.program_id(2)
    qf = q_ref[...].astype(F32)
    lane = lax.broadcasted_iota(I32, (t, LANES), 1)
    offs = [pl.multiple_of(jnp.clip(i + o, 0, nk - 1) * t, t) for o in range(-n, n + 1)]
    inside = [(i + o >= 0) & (i + o < nk) for o in range(-n, n + 1)]
    outs, lses = [], []
    for hh in range(2):
        q = jnp.where((lane >= B_HEAD_DIM * hh) & (lane < B_HEAD_DIM * (hh + 1)), qf, 0.0).astype(BF)
        logits = [jnp.where(ok, _dot_nt(q, k_ref[pl.ds(off, t), :]) + bias_ref[hh, c], NEG)
                  for c, (off, ok) in enumerate(zip(offs, inside))]
        m = logits[0].max(-1, keepdims=True)
        for s in logits[1:]:
            m = jnp.maximum(m, s.max(-1, keepdims=True))
        l = jnp.zeros((t, 1), F32)
        acc = jnp.zeros((t, LANES), F32)
        for s, off in zip(logits, offs):
            p = jnp.exp(s - m)
            l = l + jnp.sum(p, -1, keepdims=True)
            acc = acc + _dot(p.astype(BF), v_ref[pl.ds(off, t), :])
        outs.append(acc / l)
        lses.append(jnp.broadcast_to(m + jnp.log(l), (t, LANES)))
    first = lane < B_HEAD_DIM
    o_ref[...] = jnp.where(first, outs[0], outs[1]).astype(BF)
    lse_ref[...] = jnp.where(first, lses[0], lses[1])


def _band_attention(qkv, bias, g, t):
    b, s, c = qkv.shape
    third = c // 3 // LANES
    pairs = B_HEADS_PER_GROUP // 2
    n = _band_tiles(g, t)
    spec = lambda base: pl.BlockSpec((None, s, LANES), lambda bi, hp, i: (bi, 0, base + g * pairs + hp))
    out_spec = pl.BlockSpec((None, t, LANES), lambda bi, hp, i: (bi, i, hp))
    o, lse = pl.pallas_call(
        functools.partial(_band_kernel, n=n, nk=s // t),
        grid=(b, pairs, s // t),
        in_specs=[pl.BlockSpec((None, t, LANES), lambda bi, hp, i: (bi, i, g * pairs + hp)),
                  spec(third), spec(2 * third),
                  pl.BlockSpec((2, 2 * n + 1, t, t), lambda bi, hp, i: (hp, 0, 0, 0))],
        out_specs=[out_spec, out_spec],
        out_shape=[jax.ShapeDtypeStruct((b, s, _B_GW), BF), jax.ShapeDtypeStruct((b, s, _B_GW), F32)],
        compiler_params=_params("parallel", "parallel", "arbitrary"),
        name="band_%d" % g,
    )(qkv, qkv, qkv, bias)
    return o.reshape(b * s, _B_GW), lse.reshape(b * s, _B_GW)


def _prep_c(lam_re, lam_im, log_dt, b_re, b_im, c_re, c_im):
    hp = lax.Precision.HIGHEST
    L, P, C = C_CHUNK, C_STATE, C_GROUP
    lr, li = lam_re.astype(F32), lam_im.astype(F32)
    dt = jnp.exp(log_dt.astype(F32))[..., None]
    mag = jnp.exp(lr * dt)
    ar, ai = mag * jnp.cos(li * dt), mag * jnp.sin(li * dt)
    den = lr * lr + li * li
    zr = ((ar - 1.0) * lr + ai * li) / den
    zi = (ai * lr - (ar - 1.0) * li) / den
    br, bi = b_re.astype(F32), b_im.astype(F32)
    bbr = zr[..., None] * br - zi[..., None] * bi
    bbi = zr[..., None] * bi + zi[..., None] * br
    cr, ci = c_re.astype(F32), c_im.astype(F32)
    prs, pis = [jnp.ones_like(ar)], [jnp.zeros_like(ai)]
    for _ in range(L):
        pr_, pi_ = prs[-1], pis[-1]
        prs.append(ar * pr_ - ai * pi_)
        pis.append(ar * pi_ + ai * pr_)
    pr, pi = jnp.stack(prs, 0), jnp.stack(pis, 0)

    def lag(pr_k, pi_k):
        tr = pr_k[..., None] * bbr - pi_k[..., None] * bbi
        ti = pr_k[..., None] * bbi + pi_k[..., None] * bbr
        return (jnp.einsum('dgop,kdgpi->kdgoi', cr, tr, precision=hp)
                - jnp.einsum('dgop,kdgpi->kdgoi', ci, ti, precision=hp))

    kern = lag(pr[:L], pi[:L])
    lags = jnp.concatenate([kern[:0:-1, 1], (kern[0, 0] + kern[0, 1])[None], kern[1:, 0]], 0)

    def summ(d, powers):
        pr_k, pi_k = pr[powers, d], pi[powers, d]
        sr = pr_k[..., None] * bbr[d] - pi_k[..., None] * bbi[d]
        si = pr_k[..., None] * bbi[d] + pi_k[..., None] * bbr[d]
        s = jnp.concatenate([sr, si], 2)
        return jnp.transpose(s, (1, 0, 3, 2)).reshape(C_N_GROUPS, L * C, 2 * P)

    w_sum = jnp.concatenate([summ(0, np.arange(L - 1, -1, -1)), summ(1, np.arange(L))], -1)

    def outw(d, powers):
        pr_k, pi_k = pr[powers, d], pi[powers, d]
        wr = cr[d][None] * pr_k[:, :, None, :] - ci[d][None] * pi_k[:, :, None, :]
        wi = -(cr[d][None] * pi_k[:, :, None, :] + ci[d][None] * pr_k[:, :, None, :])
        w = jnp.concatenate([wr, wi], -1)
        return jnp.transpose(w, (1, 3, 0, 2)).reshape(C_N_GROUPS, 2 * P, L * C)

    w_state = jnp.concatenate([outw(0, np.arange(1, L + 1)), outw(1, np.arange(L, 0, -1))], 1)

    nq = C_N_GROUPS // _C_GL
    npair = _C_GL // 2
    blk = L * _C_GL * C
    lane = jnp.arange(LANES)
    gsel = (lane[None, :] // C == jnp.arange(_C_GL)[:, None])
    src = jnp.swapaxes(lags, -1, -2).astype(BF).reshape(2 * L - 1, nq, _C_GL, C, C)
    src = jnp.where(gsel[None, None, :, None, :], jnp.tile(src, (1, 1, 1, 1, _C_GL)), 0)
    w_lag = jnp.transpose(src.reshape(2 * L - 1, nq, LANES, LANES), (1, 0, 2, 3))
    psel = (lane[None, :] // P == jnp.arange(2)[:, None])
    ksel = jnp.eye(npair, dtype=bool)
    src = jnp.tile(w_sum.astype(BF).reshape(nq, npair, 2, L, C, 1, 4, P), (1, 1, 1, 1, 1, npair, 1, 2))
    src = jnp.where(psel[None, None, :, None, None, None, None, :]
                    & ksel[None, :, None, None, None, :, None, None], src, 0)
    w_sum = jnp.transpose(src, (0, 3, 1, 2, 4, 5, 6, 7)).reshape(nq, blk, _C_GL * 4 * P)
    osel = (lane[None, None, :] // C
            == (2 * jnp.arange(npair)[:, None, None] + jnp.arange(2)[None, :, None]))
    src = jnp.tile(w_state.astype(BF).reshape(nq, npair, 2, 4, P, L, C), (1, 1, 1, 1, 1, 1, _C_GL))
    src = jnp.where(osel[None, :, :, None, None, None, :], src, 0)
    w_state = jnp.transpose(src, (0, 1, 3, 2, 4, 5, 6)).reshape(nq, _C_GL * 4 * P, blk)
    dec = jnp.stack([pr[L, 0], pi[L, 0], pr[L, 1], pi[L, 1]], 0).reshape(4, nq * npair, 2 * P)
    dec = jnp.broadcast_to(jnp.transpose(dec, (1, 0, 2))[:, :, None, :], (nq * npair, 4, 8, 2 * P))
    return w_sum.astype(BF), w_lag, w_state.astype(BF), dec


def _chunk_rows(x_ref, nb, rc):
    L = C_CHUNK
    rows = [jnp.concatenate([x_ref[b, pl.ds(j, rc, stride=L), :].astype(BF) for j in range(L)], axis=1)
            for b in range(nb)]
    return jnp.concatenate(rows, axis=0)


def _s5_sum_kernel(x_ref, w_ref, s_ref, *, nb, rc):
    s = _dot(_chunk_rows(x_ref, nb, rc), w_ref[...])
    for b in range(nb):
        for t in range(s_ref.shape[0]):
            s_ref[t, pl.ds(b, rc, stride=nb), :] = s[b * rc:(b + 1) * rc, LANES * t:LANES * (t + 1)]


def _s5_scan_kernel(s_ref, dec_ref, e_ref, *, nc, nb):
    units = s_ref.shape[0] // 4
    dec = [[dec_ref[u, k, 0:nb, :] for k in range(4)] for u in range(units)]

    def body(t, carry):
        rf = pl.ds(pl.multiple_of(t * nb, nb), nb)
        rb = pl.ds(pl.multiple_of((nc - 1 - t) * nb, nb), nb)
        new = []
        for u in range(units):
            fr, fi, br, bi = carry[4 * u:4 * u + 4]
            arf, aif, arb, aib = dec[u]
            e_ref[4 * u, rf, :] = fr
            e_ref[4 * u + 1, rf, :] = fi
            e_ref[4 * u + 2, rb, :] = br
            e_ref[4 * u + 3, rb, :] = bi
            new += [fr * arf - fi * aif + s_ref[4 * u, rf, :], fi * arf + fr * aif + s_ref[4 * u + 1, rf, :],
                    br * arb - bi * aib + s_ref[4 * u + 2, rb, :], bi * arb + br * aib + s_ref[4 * u + 3, rb, :]]
        return tuple(new)

    z = jnp.zeros((nb, LANES), F32)
    lax.fori_loop(0, nc, body, (z,) * (4 * units))


def _s5_out_kernel(x_ref, e_ref, lag_ref, wc_ref, y_ref, wi_ref, *, nb, rc):
    L = C_CHUNK
    half = pl.program_id(2)

    @pl.when((pl.program_id(1) == 0) & (half == 0))
    def _():
        for j in range(L):
            for i in range(L):
                wi_ref[i // (L // 2), LANES * j:LANES * (j + 1), LANES * (i % (L // 2)):LANES * (i % (L // 2) + 1)] = (
                    lag_ref[i - j + L - 1])

    xc = _chunk_rows(x_ref, nb, rc)
    ec = jnp.concatenate(
        [jnp.concatenate([e_ref[t, pl.ds(b, rc, stride=nb), :].astype(BF) for t in range(e_ref.shape[0])], axis=1)
         for b in range(nb)], axis=0)
    y = _dot(xc, wi_ref[half]) + _dot(ec, wc_ref[...])
    for b in range(nb):
        for ii in range(L // 2):
            y_ref[b, pl.ds(half * (L // 2) + ii, rc, stride=L), :] = y[b * rc:(b + 1) * rc, LANES * ii:LANES * (ii + 1)]


def _s5(x, b, s, w_sum, w_lag, w_state, dec):
    L = C_CHUNK
    nc = s // L
    nq = D_MODEL // LANES
    nsl = w_sum.shape[2] // LANES
    blk = L * LANES
    rc = _tile(nc, max(8, 512 // b))
    x3 = x.reshape(b, s, D_MODEL)
    x_spec = pl.BlockSpec((b, rc * L, LANES), lambda q, c, *_: (0, c, q))
    sums = pl.pallas_call(
        functools.partial(_s5_sum_kernel, nb=b, rc=rc),
        grid=(nq, nc // rc),
        in_specs=[x_spec, pl.BlockSpec((None, blk, nsl * LANES), lambda q, c: (q, 0, 0))],
        out_specs=pl.BlockSpec((nsl, rc * b, LANES), lambda q, c: (q, c, 0)),
        out_shape=jax.ShapeDtypeStruct((nq * nsl, nc * b, LANES), F32),
        compiler_params=_params("parallel", "arbitrary"),
        name="s5_sum",
    )(x3, w_sum)
    upb = 2
    ent = pl.pallas_call(
        functools.partial(_s5_scan_kernel, nc=nc, nb=b),
        grid=(nq * nsl // (4 * upb),),
        in_specs=[pl.BlockSpec((4 * upb, nc * b, LANES), lambda i: (i, 0, 0)),
                  pl.BlockSpec((upb, 4, 8, LANES), lambda i: (i, 0, 0, 0))],
        out_specs=pl.BlockSpec((4 * upb, nc * b, LANES), lambda i: (i, 0, 0)),
        out_shape=jax.ShapeDtypeStruct((nq * nsl, nc * b, LANES), F32),
        compiler_params=_params("parallel"),
        name="s5_scan",
    )(sums, dec)
    y = pl.pallas_call(
        functools.partial(_s5_out_kernel, nb=b, rc=rc),
        grid=(nq, nc // rc, 2),
        in_specs=[x_spec,
                  pl.BlockSpec((nsl, rc * b, LANES), lambda q, c, h: (q, c, 0)),
                  pl.BlockSpec((None, 2 * L - 1, LANES, LANES), lambda q, c, h: (q, 0, 0, 0)),
                  pl.BlockSpec((None, nsl * LANES, blk // 2), lambda q, c, h: (q, 0, h))],
        out_specs=pl.BlockSpec((b, rc * L, LANES), lambda q, c, h: (0, c, q)),
        out_shape=jax.ShapeDtypeStruct((b, s, D_MODEL), F32),
        scratch_shapes=[pltpu.VMEM((2, blk, blk // 2), BF)],
        compiler_params=_params("arbitrary", "arbitrary", "arbitrary"),
        name="s5_out",
    )(x3, ent, w_lag, w_state)
    return y.reshape(b * s, D_MODEL)


_D_T = 512


def _prep_d(w_qkv, norm_gain, w_o, lambda_init):
    qk_w = D_HEADS * 2 * D_QK_DIM
    wq = w_qkv[:, :qk_w] * (D_QK_DIM ** -0.5 * LOG2E)
    wv = _pad_heads(w_qkv[:, 2 * qk_w:], D_HEADS, D_V_DIM)
    w = jnp.concatenate([wq, w_qkv[:, qk_w:2 * qk_w], wv], 1).astype(BF)
    gain = jnp.concatenate([norm_gain * (1.0 - lambda_init), jnp.zeros((LANES - D_V_DIM,), F32)])[None, :]
    wo = w_o.reshape(D_HEADS, D_V_DIM, D_MODEL)
    wo = jnp.concatenate([wo, jnp.zeros_like(wo)], 1).reshape(D_HEADS * LANES, D_MODEL).astype(BF)
    return w, gain, wo


def _diff_bias(rel_bias, t):
    rel0 = (jnp.arange(2 * t) + t) % (2 * t) - t
    rel = jnp.arange(-2, 3)[:, None] * t + rel0[None, :]
    return jnp.transpose(rel_bias[_bucket(rel)] * LOG2E, (2, 0, 1)).astype(F32)


def _flash_d_kernel(q_ref, k_ref, v_ref, vec_ref, lam_ref, gain_ref, o_ref, m_ref, acc_ref, bias_ref, *, kt, nk,
                    lambda_init):
    t = q_ref.shape[0]
    i = pl.program_id(2)

    @pl.when(i == 0)
    def _():
        for d in range(bias_ref.shape[0]):
            full = jnp.broadcast_to(vec_ref[d:d + 1, :], (t, 2 * t))
            bias_ref[d] = pltpu.roll(full, 0, 1, stride=1, stride_axis=0)[:, :t]

    qv = q_ref[...].astype(F32)
    lane = lax.broadcasted_iota(I32, (t, LANES), 1)
    lo = (pl.program_id(1) % 2) * (2 * D_QK_DIM)
    q0 = jnp.where((lane >= lo) & (lane < lo + D_QK_DIM), qv, 0.0).astype(BF)
    q1 = jnp.where((lane >= lo + D_QK_DIM) & (lane < lo + 2 * D_QK_DIM), qv, 0.0).astype(BF)
    q = jnp.concatenate([q0, q1], axis=0)
    _softmax_init(m_ref, acc_ref)

    def body(j, carry):
        off = pl.multiple_of(j * (kt * t), kt * t)
        bias = jnp.concatenate([bias_ref[jnp.clip(j * kt + c - i, -2, 2) + 2] for c in range(kt)], axis=1)
        s = _dot_nt(q, k_ref[pl.ds(off, kt * t), :])
        s = (s.reshape(2, t, kt * t) + bias[None]).reshape(2 * t, kt * t)
        _softmax_step(s, v_ref[pl.ds(off, kt * t), :], m_ref, acc_ref)
        return carry

    lax.fori_loop(0, nk, body, 0, unroll=4)
    lf = lam_ref[...]
    lam = (jnp.exp(jnp.sum(lf[0:1] * lf[1:2], keepdims=True))
           - jnp.exp(jnp.sum(lf[2:3] * lf[3:4], keepdims=True)) + lambda_init)
    acc = acc_ref[...]
    on = acc / acc[:, _ONE_LANE:_ONE_LANE + 1]
    o = jnp.where(lane < D_V_DIM, on[:t] - lam * on[t:], 0.0)
    ms = jnp.sum(o * o, -1, keepdims=True) * (1.0 / D_V_DIM)
    o_ref[...] = (o * lax.rsqrt(ms + RMS_EPS) * gain_ref[...]).astype(BF)


def _flash_d(qkv, bias, lam, gain, lambda_init):
    b, s, _ = qkv.shape
    t = bias.shape[-1] // 2
    kt = 2 if s % (2 * t) == 0 else 1
    return pl.pallas_call(
        functools.partial(_flash_d_kernel, kt=kt, nk=s // (kt * t), lambda_init=lambda_init),
        grid=(b, D_HEADS, s // t),
        in_specs=[pl.BlockSpec((None, t, LANES), lambda bi, h, i: (bi, i, h // 2)),
                  pl.BlockSpec((None, s, LANES), lambda bi, h, i: (bi, 0, D_HEADS // 2 + h // 2)),
                  pl.BlockSpec((None, s, LANES), lambda bi, h, i: (bi, 0, D_HEADS + h)),
                  pl.BlockSpec((None, 5, 2 * t), lambda bi, h, i: (h, 0, 0)),
                  pl.BlockSpec((4, D_QK_DIM), lambda bi, h, i: (0, 0)),
                  pl.BlockSpec((1, LANES), lambda bi, h, i: (0, 0))],
        out_specs=pl.BlockSpec((None, t, LANES), lambda bi, h, i: (bi, i, h)),
        out_shape=jax.ShapeDtypeStruct((b, s, D_HEADS * LANES), BF),
        scratch_shapes=[pltpu.VMEM((2 * t, LANES), F32)] * 2 + [pltpu.VMEM((5, t, t), F32)],
        compiler_params=_params("parallel", "parallel", "arbitrary"),
        name="flash_d",
    )(qkv, qkv, qkv, bias, lam, gain)


def _cross_kernel(x_ref, kv_ref, wq_ref, wo_ref, g_ref, b_ref, wr_ref, y_ref, ybf_ref, aff_ref, *, parts):
    tp = x_ref.shape[0] // parts
    wr = wr_ref[...]
    wh = wr.astype(BF)
    wl = (wr - wh.astype(F32)).astype(BF)
    for part in range(parts):
        rows = slice(tp * part, tp * (part + 1))
        x = x_ref[rows, :]
        q = (_dot(x.astype(BF), wq_ref[...]) * (X_HEAD_DIM ** -0.5)).astype(BF)
        outs = []
        for h in range(X_HEADS):
            sl = slice(X_HEAD_DIM * h, X_HEAD_DIM * (h + 1))
            s = _dot_nt(q[:, sl], kv_ref[:, sl])
            p = jnp.exp(s - jnp.max(s, -1, keepdims=True))
            l = jnp.sum(p, -1, keepdims=True)
            vh = kv_ref[:, D_MODEL + X_HEAD_DIM * h:D_MODEL + X_HEAD_DIM * (h + 1)]
            outs.append((_dot(p.astype(BF), vh) / l).astype(BF))
        o = jnp.concatenate(outs, axis=1)
        y = _ln_rows(ALPHA * x + _dot(o, wo_ref[...]), g_ref[...], b_ref[...])
        y_ref[rows, :] = y
        yh = y.astype(BF)
        ybf_ref[rows, :] = yh
        yl = (y - yh.astype(F32)).astype(BF)
        lg = _dot_nt(wh, yh) + _dot_nt(wh, yl) + _dot_nt(wl, yh)
        e = jnp.exp(lg - jnp.max(lg, 0, keepdims=True))
        aff = e / jnp.sum(e, 0, keepdims=True)
        for c in range(tp // LANES):
            aff_ref[part * (tp // LANES) + c] = aff[:, LANES * c:LANES * (c + 1)]


def _cross(x, kv, wq, wo, g, b, wr_t, s, mem_len, tm=1024, parts=2):
    n = x.shape[0]
    tm = _tile(s, tm)
    per = s // tm
    fixed = lambda i: (0, 0)
    return pl.pallas_call(
        functools.partial(_cross_kernel, parts=parts),
        grid=(n // tm,),
        in_specs=[pl.BlockSpec((tm, D_MODEL), lambda i: (i, 0)),
                  pl.BlockSpec((mem_len, 2 * D_MODEL), lambda i: (i // per, 0)),
                  pl.BlockSpec((D_MODEL, D_MODEL), fixed), pl.BlockSpec((D_MODEL, D_MODEL), fixed),
                  pl.BlockSpec((1, D_MODEL), fixed), pl.BlockSpec((1, D_MODEL), fixed),
                  pl.BlockSpec((N_EXPERTS, D_MODEL), fixed)],
        out_specs=[pl.BlockSpec((tm, D_MODEL), lambda i: (i, 0)),
                   pl.BlockSpec((tm, D_MODEL), lambda i: (i, 0)),
                   pl.BlockSpec((tm // LANES, N_EXPERTS, LANES), lambda i: (i, 0, 0))],
        out_shape=[jax.ShapeDtypeStruct((n, D_MODEL), F32),
                   jax.ShapeDtypeStruct((n, D_MODEL), BF),
                   jax.ShapeDtypeStruct((n // LANES, N_EXPERTS, LANES), F32)],
        compiler_params=_params("parallel"),
        name="cross",
    )(x, kv, wq, wo, g, b, wr_t)


def _select_kernel(a_ref, pos_ref, st_ref, *, k, nbits):
    nt = a_ref.shape[0]
    shape = (nt, N_EXPERTS, LANES)
    kf = float(k)

    def keys():
        return lax.bitcast_convert_type(a_ref[...], I32)

    def count(mask):
        c = jnp.sum(jnp.where(mask, 1.0, 0.0), axis=0, keepdims=True)
        return jnp.sum(c, axis=2, keepdims=True)

    def value_step(it, thr):
        cand = thr | jnp.left_shift(jnp.int32(1), 30 - it)
        return jnp.where(count(keys() >= cand) >= kf, cand, thr)

    thr = lax.fori_loop(0, 31, value_step, jnp.zeros((1, N_EXPERTS, 1), I32))
    need = kf - count(keys() > thr)
    idx = lax.broadcasted_iota(I32, shape, 0) * LANES + lax.broadcasted_iota(I32, shape, 2)

    def index_step(it, ithr):
        cand = ithr | jnp.left_shift(jnp.int32(1), nbits - 1 - it)
        return jnp.where(count((keys() == thr) & (idx < cand)) < need, cand, ithr)

    ithr = lax.fori_loop(0, nbits, index_step, jnp.zeros((1, N_EXPERTS, 1), I32))
    thr2, ithr2 = thr[0], ithr[0]
    upper = jnp.where(lax.broadcasted_iota(I32, (LANES, LANES), 0) <= lax.broadcasted_iota(I32, (LANES, LANES), 1),
                      1.0, 0.0).astype(BF)
    lane = lax.broadcasted_iota(I32, (N_EXPERTS, LANES), 1)

    def tile_step(j, carry):
        kj = lax.bitcast_convert_type(a_ref[j], I32)
        sel = (kj > thr2) | ((kj == thr2) & (j * LANES + lane <= ithr2))
        m = jnp.where(sel, 1.0, 0.0)
        inc = _dot(m.astype(BF), upper)
        pos_ref[j] = jnp.where(sel, inc - m + carry, -1.0).astype(I32)
        st_ref[j] = jnp.broadcast_to(carry, (N_EXPERTS, LANES)).astype(I32)
        return carry + inc[:, LANES - 1:LANES]

    lax.fori_loop(0, nt, tile_step, jnp.zeros((N_EXPERTS, 1), F32))


def _select(aff3, k):
    nt = aff3.shape[0]
    nbits = max(1, int(math.ceil(math.log2(nt * LANES))))
    shp = jax.ShapeDtypeStruct(aff3.shape, I32)
    return pl.pallas_call(
        functools.partial(_select_kernel, k=k, nbits=nbits),
        out_shape=[shp, shp],
        compiler_params=pltpu.CompilerParams(vmem_limit_bytes=VMEM_LIMIT),
        name="select",
    )(aff3)


_GATHER_ROWS = LANES + BF16_ROWS
_GATHER_ROWS_SMALL = 3 * BF16_ROWS

_MOE_GROUP = 2


def _moe_ffn_kernel(st_ref, x_ref, pos_ref, aff_ref, wg_ref, wu_ref, wd_ref, ye_ref, buf_ref, gate_ref, *, nb, ncf,
                    nch, sub, ck):
    ep = buf_ref.shape[0]
    grp = pl.program_id(0)
    t = pl.program_id(1)

    @pl.when(t == 0)
    def _():
        buf_ref[...] = jnp.zeros_like(buf_ref)
        gate_ref[...] = jnp.zeros_like(gate_ref)

    @pl.when(t < nb)
    def _():
        def place(s, k, e, base, nrows):
            rows = pl.ds(base, nrows)
            hit = lax.broadcasted_iota(I32, (nrows, LANES), 0) == pos_ref[s, pl.ds(e, 1), :] - base
            xs = x_ref[LANES * s:LANES * (s + 1), :]
            buf_ref[k, rows, :] = buf_ref[k, rows, :] + _dot(jnp.where(hit, 1.0, 0.0).astype(BF), xs).astype(BF)
            gate_ref[k, rows, :] = gate_ref[k, rows, :] + jnp.sum(
                jnp.where(hit, aff_ref[s, pl.ds(e, 1), :], 0.0), -1, keepdims=True)

        def base_of(s, k):
            st = st_ref[grp * ep + k, t * sub + s]
            return pl.multiple_of((st // BF16_ROWS) * BF16_ROWS, BF16_ROWS)

        pairs = [(s, k) for s in range(sub) for k in range(ep)]
        few = functools.reduce(
            jnp.logical_and,
            [st_ref[grp * ep + k, t * sub + s + 1] - base_of(s, k) <= _GATHER_ROWS_SMALL for s, k in pairs])

        @pl.when(few)
        def _():
            for s, k in pairs:
                place(s, k, grp * ep + k, base_of(s, k), _GATHER_ROWS_SMALL)

        @pl.when(jnp.logical_not(few))
        def _():
            for s, k in pairs:
                place(s, k, grp * ep + k, base_of(s, k), _GATHER_ROWS)

    @pl.when(t >= nb)
    def _():
        k = (t - nb) // nch
        c = (t - nb) - k * nch

        @pl.when(c < ncf)
        def _():
            rows = pl.ds(pl.multiple_of(c * ck, ck), ck)
            xe = buf_ref[k, rows, :]
            h = (jax.nn.silu(_dot(xe, wg_ref[...])) * _dot(xe, wu_ref[...])).astype(BF)
            ye_ref[...] = (_dot(h, wd_ref[...]) * gate_ref[k, rows, :]).astype(BF)

        @pl.when(c >= ncf)
        def _():
            ye_ref[...] = jnp.zeros_like(ye_ref)


def _moe_ffn(starts, xbf, pos3, aff3, wg, wu, wd, cap, tb=2048):
    n = xbf.shape[0]
    tb = _tile(n, tb)
    nb = n // tb
    sub = tb // LANES
    ck = _tile(cap, 1024)
    ncf = cap // ck
    nch = ncf + pl.cdiv(_WIN_ROWS, ck)
    dff = wg.shape[-1]
    ep = _MOE_GROUP
    blk = lambda g, t, st: (jnp.minimum(t, nb - 1), 0)
    blk3 = lambda g, t, st: (jnp.minimum(t, nb - 1), 0, 0)
    expert = lambda g, t: g * ep + jnp.clip((t - nb) // nch, 0, ep - 1)
    wmap = lambda g, t, st: (expert(g, t), 0, 0)
    return pl.pallas_call(
        functools.partial(_moe_ffn_kernel, nb=nb, ncf=ncf, nch=nch, sub=sub, ck=ck),
        grid_spec=pltpu.PrefetchScalarGridSpec(
            num_scalar_prefetch=1,
            grid=(N_EXPERTS // ep, nb + ep * nch),
            in_specs=[pl.BlockSpec((tb, D_MODEL), blk),
                      pl.BlockSpec((sub, N_EXPERTS, LANES), blk3),
                      pl.BlockSpec((sub, N_EXPERTS, LANES), blk3),
                      pl.BlockSpec((None, D_MODEL, dff), wmap),
                      pl.BlockSpec((None, D_MODEL, dff), wmap),
                      pl.BlockSpec((None, dff, D_MODEL), wmap)],
            out_specs=pl.BlockSpec((None, ck, D_MODEL),
                                   lambda g, t, st: (expert(g, t), jnp.maximum(t - nb, 0) % nch, 0)),
            scratch_shapes=[pltpu.VMEM((ep, cap + _GATHER_ROWS, D_MODEL), BF),
                            pltpu.VMEM((ep, cap + _GATHER_ROWS, 1), F32)]),
        out_shape=jax.ShapeDtypeStruct((N_EXPERTS, nch * ck, D_MODEL), BF),
        compiler_params=_params("arbitrary", "arbitrary"),
        name="moe_ffn",
    )(starts, xbf, pos3, aff3, wg, wu, wd)


_WIN_STEP = 128
_COMB_TILES = 2
_WIN_ROWS = _WIN_STEP + (_COMB_TILES - 1) * LANES + _GATHER_ROWS


_COMB_PACK = MXU_DEPTH // _GATHER_ROWS_SMALL


def _combine_kernel(st_ref, x_ref, pos_ref, g_ref, b_ref, *rest):
    ye_refs, y_ref = rest[:N_EXPERTS], rest[N_EXPERTS]
    j = pl.program_id(0)
    pairs = [(u, e) for u in range(_COMB_TILES) for e in range(N_EXPERTS)]

    def base_of(u, e):
        return (st_ref[e, j * _COMB_TILES + u] // BF16_ROWS) * BF16_ROWS

    def rows_of(u, e, nrows):
        win = (st_ref[e, j * _COMB_TILES] // _WIN_STEP) * _WIN_STEP
        return ye_refs[e][0, pl.ds(pl.multiple_of(base_of(u, e) - win, BF16_ROWS), nrows), :]

    def finish(u, acc):
        tok = slice(LANES * u, LANES * (u + 1))
        y_ref[tok, :] = _ln_rows(ALPHA * x_ref[tok, :] + acc, g_ref[...], b_ref[...])

    few = functools.reduce(
        jnp.logical_and,
        [st_ref[e, j * _COMB_TILES + u + 1] - base_of(u, e) <= _GATHER_ROWS_SMALL for u, e in pairs])

    @pl.when(few)
    def _():
        for u in range(_COMB_TILES):
            pos = jnp.transpose(pos_ref[u])
            acc = None
            for e0 in range(0, N_EXPERTS, _COMB_PACK):
                es = list(range(e0, min(e0 + _COMB_PACK, N_EXPERTS)))
                col = lax.broadcasted_iota(I32, (LANES, len(es) * _GATHER_ROWS_SMALL), 1)
                hit = None
                for k, e in enumerate(es):
                    rel = pos[:, e:e + 1] - base_of(u, e)
                    h = col == jnp.where(rel >= 0, rel + k * _GATHER_ROWS_SMALL, -1)
                    hit = h if hit is None else hit | h
                slab = jnp.concatenate([rows_of(u, e, _GATHER_ROWS_SMALL) for e in es], axis=0)
                d = _dot(jnp.where(hit, 1.0, 0.0).astype(BF), slab)
                acc = d if acc is None else acc + d
            finish(u, acc)

    @pl.when(jnp.logical_not(few))
    def _():
        col = lax.broadcasted_iota(I32, (LANES, _GATHER_ROWS), 1)
        for u in range(_COMB_TILES):
            pos = jnp.transpose(pos_ref[u])
            acc = None
            for e in range(N_EXPERTS):
                onehot = jnp.where(col == pos[:, e:e + 1] - base_of(u, e), 1.0, 0.0).astype(BF)
                d = _dot(onehot, rows_of(u, e, _GATHER_ROWS))
                acc = d if acc is None else acc + d
            finish(u, acc)


def _combine(starts, x, pos, g, b, ye):
    n = x.shape[0]
    assert ye.shape[1] >= EC_CAPACITY * n // N_EXPERTS + _WIN_ROWS
    tt = _COMB_TILES * LANES
    row = lambda j, st: (j, 0)
    fixed = lambda j, st: (0, 0)

    def ye_spec(e):
        return pl.BlockSpec((pl.Element(1), pl.Element(_WIN_ROWS), pl.Element(D_MODEL)),
                            lambda j, st: (e, (st[e, j * _COMB_TILES] // _WIN_STEP) * _WIN_STEP, 0))

    return pl.pallas_call(
        _combine_kernel,
        grid_spec=pltpu.PrefetchScalarGridSpec(
            num_scalar_prefetch=1,
            grid=(n // tt,),
            in_specs=[pl.BlockSpec((tt, D_MODEL), row),
                      pl.BlockSpec((_COMB_TILES, N_EXPERTS, LANES), lambda j, st: (j, 0, 0)),
                      pl.BlockSpec((1, D_MODEL), fixed), pl.BlockSpec((1, D_MODEL), fixed)]
            + [ye_spec(e) for e in range(N_EXPERTS)],
            out_specs=pl.BlockSpec((tt, D_MODEL), row)),
        out_shape=jax.ShapeDtypeStruct((n, D_MODEL), F32),
        compiler_params=_params("arbitrary"),
        name="combine",
    )(starts, x, pos, g, b, *([ye] * N_EXPERTS))


def _moe(x, xg, aff3, wg, wu, wd, g, b):
    n = x.shape[0]
    cap = EC_CAPACITY * n // N_EXPERTS
    pos3, st3 = _select(aff3, cap)
    starts = jnp.concatenate([jnp.transpose(st3[:, :, 0]), jnp.full((N_EXPERTS, 1), cap, I32)], axis=1)
    ye = _moe_ffn(starts, xg, pos3, aff3, wg, wu, wd, cap)
    return _combine(starts, x, pos3, g, b, ye)


def _prep_weights(p):
    w = {}
    w['a'] = [_prep_a(p['a_w_qkv'][j], p['a_q_gain'][j], p['a_k_gain'][j], p['a_w_o'][j])
              for j in range(p['a_w_qkv'].shape[0])]
    w['b'] = [_prep_b(p['b_w_qkv'][j], p['b_w_o'][j]) for j in range(p['b_w_qkv'].shape[0])]
    w['c'] = [_prep_c(p['c_lam_re'][j], p['c_lam_im'][j], p['c_log_dt'][j], p['c_b_re'][j], p['c_b_im'][j],
                      p['c_c_re'][j], p['c_c_im'][j]) + (p['c_d'][j][None, :], p['c_w_glu'][j].astype(BF))
              for j in range(p['c_lam_re'].shape[0])]
    w['d'] = []
    for j in range(p['d_w_qkv'].shape[0]):
        layer = N_MIXERS * j + 3
        lambda_init = 0.8 - 0.6 * math.exp(-0.3 * layer)
        w['d'].append(_prep_d(p['d_w_qkv'][j], p['d_norm_gain'][j], p['d_w_o'][j], lambda_init)
                      + (p['d_lam'][j].astype(F32), lambda_init))
    w['diff_bias'] = _diff_bias(p['rel_bias'], _D_T)
    w['x_w_q'] = p['x_w_q'].astype(BF)
    w['x_w_kv'] = p['x_w_kv'].astype(BF)
    w['x_w_o'] = p['x_w_o'].astype(BF)
    w['router_t'] = jnp.transpose(p['moe_w_router'], (0, 2, 1)).astype(F32)
    w['moe_w_gate'] = p['moe_w_gate'].astype(BF)
    w['moe_w_up'] = p['moe_w_up'].astype(BF)
    w['moe_w_down'] = p['moe_w_down'].astype(BF)
    return w


def _trunk(x, mem, p, w):
    b, s, _ = x.shape
    n = b * s
    mem_len = mem.shape[1]
    x = x.reshape(n, D_MODEL)
    mem2 = mem.reshape(b * mem_len, D_MODEL)
    ln_g, ln_b = p['ln_g'], p['ln_b']
    for i in range(DEPTH):
        m, j = i % N_MIXERS, i // N_MIXERS
        g0, b0 = ln_g[i, 0][None, :], ln_b[i, 0][None, :]
        if m == 0:
            wa, gains, wo = w['a'][j]
            qkv = _proj_a(x, wa, _rope_table(s), gains, s)
            o = _flash_a(qkv.reshape(b, s, _A_COLS))
            x = _post(x, o.reshape(n, A_HEADS * A_HEAD_DIM), wo, g0, b0)
        elif m == 1:
            wb, wo = w['b'][j]
            qkv = _proj(x, wb).reshape(b, s, wb.shape[1])
            os_, ls_ = [], []
            for g in range(len(B_PATTERNS)):
                t = _tile(s, _B_T)
                o, lse = _band_attention(qkv, _band_bias(p['rel_bias'], g, t), g, t)
                os_.append(o)
                ls_.append(lse)
            x = _post_b(x, os_, ls_, wo, g0, b0)
        elif m == 2:
            w_sum, w_lag, w_state, dec, dskip, wglu = w['c'][j]
            ys = _s5(x, b, s, w_sum, w_lag, w_state, dec)
            x = _post_c(x, ys, dskip, wglu, g0, b0)
        else:
            wd, gain, wo, lam, lambda_init = w['d'][j]
            qkv = _proj(x, wd, ones_from=D_HEADS * LANES).reshape(b, s, wd.shape[1])
            o = _flash_d(qkv, w['diff_bias'], lam, gain, lambda_init)
            x = _post(x, o.reshape(n, D_HEADS * LANES), wo, g0, b0)
        kv = _proj(mem2, w['x_w_kv'][i], tm=mem_len)
        x, xbf, aff3 = _cross(x, kv, w['x_w_q'][i], w['x_w_o'][i], ln_g[i, 1][None, :], ln_b[i, 1][None, :],
                              w['router_t'][i], s, mem_len)
        x = _moe(x, xbf, aff3, w['moe_w_gate'][i], w['moe_w_up'][i], w['moe_w_down'][i],
                 ln_g[i, 2][None, :], ln_b[i, 2][None, :])
    return x.reshape(b, s, D_MODEL)


def kernel(x_prompt, x_sample, mem_prompt, mem_sample, rel_bias, ln_g, ln_b, a_w_qkv, a_q_gain, a_k_gain, a_w_o, b_w_qkv, b_w_o, c_lam_re, c_lam_im, c_log_dt, c_b_re, c_b_im, c_c_re, c_c_im, c_d, c_w_glu, d_w_qkv, d_lam, d_norm_gain, d_w_o, x_w_q, x_w_kv, x_w_o, moe_w_router, moe_w_gate, moe_w_up, moe_w_down):
    p = dict(rel_bias=rel_bias, ln_g=ln_g, ln_b=ln_b,
             a_w_qkv=a_w_qkv, a_q_gain=a_q_gain, a_k_gain=a_k_gain, a_w_o=a_w_o,
             b_w_qkv=b_w_qkv, b_w_o=b_w_o,
             c_lam_re=c_lam_re, c_lam_im=c_lam_im, c_log_dt=c_log_dt, c_b_re=c_b_re, c_b_im=c_b_im,
             c_c_re=c_c_re, c_c_im=c_c_im, c_d=c_d, c_w_glu=c_w_glu,
             d_w_qkv=d_w_qkv, d_lam=d_lam, d_norm_gain=d_norm_gain, d_w_o=d_w_o,
             x_w_q=x_w_q, x_w_kv=x_w_kv, x_w_o=x_w_o,
             moe_w_router=moe_w_router, moe_w_gate=moe_w_gate, moe_w_up=moe_w_up, moe_w_down=moe_w_down)
    w = _prep_weights(p)
    return (_trunk(x_prompt, mem_prompt, p, w), _trunk(x_sample, mem_sample, p, w))
```

```python
import functools
import math

import numpy as np
import jax
import jax.numpy as jnp
from jax import lax
from jax.experimental import pallas as pl
from jax.experimental.pallas import tpu as pltpu

F32 = jnp.float32
BF = jnp.bfloat16
I32 = jnp.int32

D_MODEL = 1024
DEPTH = 4
GRID_W = 64
N_MIXERS = 4
LN_EPS = 1e-5
RMS_EPS = 1e-6
ALPHA = (2.0 * DEPTH) ** 0.25

A_HEADS = 16
A_KV_HEADS = 4
A_HEAD_DIM = 64
ROPE_BASE = 10000.0

B_PATTERNS = ((128, 1), (512, 4), (2048, 16))
B_HEADS_PER_GROUP = 4
B_HEAD_DIM = 64

C_GROUP = 16
C_N_GROUPS = D_MODEL // C_GROUP
C_STATE = 64
C_CHUNK = 16
D_HEADS = 12
D_QK_DIM = 32
D_V_DIM = 64

REL_BUCKETS = 32
REL_MAX_DIST = 128

X_HEADS = 4
X_HEAD_DIM = D_MODEL // X_HEADS

N_EXPERTS = 16
EC_CAPACITY = 2

LANES = 128
BF16_ROWS = 16
MXU_DEPTH = 256
_C_GL = LANES // C_GROUP
VMEM_LIMIT = 56 * 1024 * 1024
NEG = -1e30
LOG2E = math.log2(math.e)


def _params(*sem):
    return pltpu.CompilerParams(dimension_semantics=sem, vmem_limit_bytes=VMEM_LIMIT)


def _tile(n, pref):
    t = min(n, pref)
    assert n % t == 0, (n, pref)
    return t


def _ln_rows(v, g, b):
    mu = jnp.mean(v, -1, keepdims=True)
    c = v - mu
    var = jnp.mean(c * c, -1, keepdims=True)
    return c * lax.rsqrt(var + LN_EPS) * g + b


def _dot_nt(a, b):
    return lax.dot_general(a, b, (((1,), (1,)), ((), ())), preferred_element_type=F32)


def _dot(a, b):
    return jnp.dot(a, b, preferred_element_type=F32)


_ONE_LANE = 64


def _with_ones_lane(y):
    lane = lax.broadcasted_iota(I32, y.shape, 1)
    return jnp.where(lane % LANES == _ONE_LANE, 1.0, y)


def _proj_kernel(x_ref, w_ref, o_ref, *, ones_from):
    y = _dot(x_ref[...].astype(BF), w_ref[...])
    if ones_from is None:
        o_ref[...] = y.astype(o_ref.dtype)
    else:
        o_ref[:, :ones_from] = y[:, :ones_from].astype(o_ref.dtype)
        o_ref[:, ones_from:] = _with_ones_lane(y[:, ones_from:]).astype(o_ref.dtype)


def _proj(x, w, tm=512, ones_from=None):
    n, k = x.shape
    m = w.shape[1]
    tm = _tile(n, tm)
    return pl.pallas_call(
        functools.partial(_proj_kernel, ones_from=ones_from),
        grid=(n // tm,),
        in_specs=[pl.BlockSpec((tm, k), lambda i: (i, 0)),
                  pl.BlockSpec((k, m), lambda i: (0, 0))],
        out_specs=pl.BlockSpec((tm, m), lambda i: (i, 0)),
        out_shape=jax.ShapeDtypeStruct((n, m), BF),
        compiler_params=_params("parallel"),
        name="proj",
    )(x, w)


def _post_kernel(x_ref, o_ref, w_ref, g_ref, b_ref, y_ref):
    h = _dot(o_ref[...], w_ref[...])
    y_ref[...] = _ln_rows(ALPHA * x_ref[...] + h, g_ref[...], b_ref[...])


def _post(x, o, w, g, b, tm=512):
    n = x.shape[0]
    ko = o.shape[1]
    tm = _tile(n, tm)
    return pl.pallas_call(
        _post_kernel,
        grid=(n // tm,),
        in_specs=[pl.BlockSpec((tm, D_MODEL), lambda i: (i, 0)),
                  pl.BlockSpec((tm, ko), lambda i: (i, 0)),
                  pl.BlockSpec((ko, D_MODEL), lambda i: (0, 0)),
                  pl.BlockSpec((1, D_MODEL), lambda i: (0, 0)),
                  pl.BlockSpec((1, D_MODEL), lambda i: (0, 0))],
        out_specs=pl.BlockSpec((tm, D_MODEL), lambda i: (i, 0)),
        out_shape=jax.ShapeDtypeStruct((n, D_MODEL), F32),
        compiler_params=_params("parallel"),
        name="post",
    )(x, o, w, g, b)


def _post_b_kernel(x_ref, o0_ref, o1_ref, o2_ref, l0_ref, l1_ref, l2_ref, w_ref, g_ref, b_ref, y_ref):
    l0, l1, l2 = l0_ref[...], l1_ref[...], l2_ref[...]
    m = jnp.maximum(jnp.maximum(l0, l1), l2)
    e0, e1, e2 = jnp.exp(l0 - m), jnp.exp(l1 - m), jnp.exp(l2 - m)
    inv = 1.0 / (e0 + e1 + e2)
    gw = B_HEADS_PER_GROUP * B_HEAD_DIM
    h = _dot((o0_ref[...].astype(F32) * (e0 * inv)).astype(BF), w_ref[0:gw, :])
    h = h + _dot((o1_ref[...].astype(F32) * (e1 * inv)).astype(BF), w_ref[gw:2 * gw, :])
    h = h + _dot((o2_ref[...].astype(F32) * (e2 * inv)).astype(BF), w_ref[2 * gw:3 * gw, :])
    y_ref[...] = _ln_rows(ALPHA * x_ref[...] + h, g_ref[...], b_ref[...])


def _post_b(x, os_, ls_, w, g, b, tm=512):
    n = x.shape[0]
    tm = _tile(n, tm)
    gw = B_HEADS_PER_GROUP * B_HEAD_DIM
    row = lambda i: (i, 0)
    fixed = lambda i: (0, 0)
    return pl.pallas_call(
        _post_b_kernel,
        grid=(n // tm,),
        in_specs=[pl.BlockSpec((tm, D_MODEL), row)] + [pl.BlockSpec((tm, gw), row)] * 6
        + [pl.BlockSpec((3 * gw, D_MODEL), fixed), pl.BlockSpec((1, D_MODEL), fixed),
           pl.BlockSpec((1, D_MODEL), fixed)],
        out_specs=pl.BlockSpec((tm, D_MODEL), row),
        out_shape=jax.ShapeDtypeStruct((n, D_MODEL), F32),
        compiler_params=_params("parallel"),
        name="post_b",
    )(x, *os_, *ls_, w, g, b)


def _post_c_kernel(x_ref, ys_ref, d_ref, w_ref, g_ref, b_ref, y_ref):
    x = x_ref[...]
    z = jax.nn.gelu(ys_ref[...] + d_ref[...] * x).astype(BF)
    h = _dot(z, w_ref[...])
    hh = h[:, :D_MODEL] * jax.nn.sigmoid(h[:, D_MODEL:])
    y_ref[...] = _ln_rows(ALPHA * x + hh, g_ref[...], b_ref[...])


def _post_c(x, ys, d, w, g, b, tm=512):
    n = x.shape[0]
    tm = _tile(n, tm)
    row = lambda i: (i, 0)
    fixed = lambda i: (0, 0)
    return pl.pallas_call(
        _post_c_kernel,
        grid=(n // tm,),
        in_specs=[pl.BlockSpec((tm, D_MODEL), row), pl.BlockSpec((tm, D_MODEL), row),
                  pl.BlockSpec((1, D_MODEL), fixed), pl.BlockSpec((D_MODEL, 2 * D_MODEL), fixed),
                  pl.BlockSpec((1, D_MODEL), fixed), pl.BlockSpec((1, D_MODEL), fixed)],
        out_specs=pl.BlockSpec((tm, D_MODEL), row),
        out_shape=jax.ShapeDtypeStruct((n, D_MODEL), F32),
        compiler_params=_params("parallel"),
        name="post_c",
    )(x, ys, d, w, g, b)


_A_QK_TILES = A_HEADS + A_KV_HEADS
_A_COLS = (_A_QK_TILES + A_KV_HEADS) * LANES


def _rope_partner():
    d = np.arange(A_HEAD_DIM)
    e = d % (A_HEAD_DIM // 2)
    lo = e < A_HEAD_DIM // 4
    return np.where(lo, d + A_HEAD_DIM // 4, d - A_HEAD_DIM // 4), np.where(lo, -1.0, 1.0).astype(np.float32)


def _prep_a(w_qkv, q_gain, k_gain, w_o):
    partner, sign = _rope_partner()
    nqk = _A_QK_TILES * A_HEAD_DIM
    wqk = w_qkv[:, :nqk].reshape(D_MODEL, _A_QK_TILES, A_HEAD_DIM)
    wsw = wqk[:, :, partner] * sign
    wqk = jnp.concatenate([wqk, wsw], -1).reshape(D_MODEL, _A_QK_TILES * LANES)
    wv = w_qkv[:, nqk:].reshape(D_MODEL, A_KV_HEADS, A_HEAD_DIM)
    wv = jnp.concatenate([wv, jnp.zeros_like(wv)], -1).reshape(D_MODEL, A_KV_HEADS * LANES)
    w = jnp.concatenate([wqk, wv], 1).astype(BF)
    gq = jnp.concatenate([q_gain, q_gain[partner]]) * (A_HEAD_DIM ** -0.5 * 0.5 * LOG2E)
    gk = jnp.concatenate([k_gain, k_gain[partner]])
    gains = jnp.stack([gq, gk], 0)
    return w, gains, w_o.astype(BF)


def _rope_table(s):
    pos = jnp.arange(s)
    rows, cols = (pos // GRID_W).astype(F32), (pos % GRID_W).astype(F32)
    half = A_HEAD_DIM // 2
    freqs = ROPE_BASE ** (-jnp.arange(0, half, 2, dtype=F32) / half)
    ang_r = rows[:, None] * freqs
    ang_c = cols[:, None] * freqs
    ang = jnp.concatenate([ang_r, ang_r, ang_c, ang_c], -1)
    return jnp.concatenate([jnp.cos(ang), jnp.sin(ang)], -1)


def _proj_a_kernel(x_ref, w_ref, cs_ref, g_ref, o_ref):
    xb = x_ref[...].astype(BF)
    cs = cs_ref[...]
    gq = cs * g_ref[0:1, :]
    gk = cs * g_ref[1:2, :]
    per = 4
    ones = jnp.ones((LANES, LANES), BF)
    for g in range(_A_QK_TILES // per):
        y = _dot(xb, w_ref[:, LANES * per * g:LANES * per * (g + 1)])
        for hh in range(per):
            h = per * g + hh
            t = y[:, LANES * hh:LANES * (hh + 1)]
            t2 = t * t
            hi = t2.astype(BF)
            ss = _dot(hi, ones) + _dot((t2 - hi.astype(F32)).astype(BF), ones)
            r = lax.rsqrt(ss * (1.0 / LANES) + RMS_EPS)
            e = t * r * (gq if h < A_HEADS else gk)
            o_ref[:, LANES * h:LANES * (h + 1)] = (e + pltpu.roll(e, LANES // 2, 1)).astype(BF)
    yv = _dot(xb, w_ref[:, _A_QK_TILES * LANES:])
    o_ref[:, _A_QK_TILES * LANES:] = _with_ones_lane(yv).astype(BF)


def _proj_a(x, w, cs, gains, s, tm=512):
    n = x.shape[0]
    tm = _tile(s, tm)
    per = s // tm
    return pl.pallas_call(
        _proj_a_kernel,
        grid=(n // tm,),
        in_specs=[pl.BlockSpec((tm, D_MODEL), lambda i: (i, 0)),
                  pl.BlockSpec((D_MODEL, _A_COLS), lambda i: (0, 0)),
                  pl.BlockSpec((tm, LANES), lambda i: (i % per, 0)),
                  pl.BlockSpec((2, LANES), lambda i: (0, 0))],
        out_specs=pl.BlockSpec((tm, _A_COLS), lambda i: (i, 0)),
        out_shape=jax.ShapeDtypeStruct((n, _A_COLS), BF),
        compiler_params=_params("parallel"),
        name="proj_a",
    )(x, w, cs, gains)


def _softmax_step(s, v, m_ref, acc_ref):
    tk = s.shape[1]
    m_old = m_ref[...]
    m_new = jnp.maximum(m_old, jnp.max(s, -1, keepdims=True))
    p = jnp.concatenate([jnp.exp2(s[:, LANES * c:LANES * (c + 1)] - m_new).astype(BF) for c in range(tk // LANES)],
                        axis=1)
    acc_ref[...] = jnp.exp2(m_old - m_new) * acc_ref[...] + _dot(p, v)
    m_ref[...] = m_new


def _softmax_init(m_ref, acc_ref):
    m_ref[...] = jnp.full(m_ref.shape, -jnp.inf, F32)
    acc_ref[...] = jnp.zeros_like(acc_ref)


def _flash_a_kernel(q_ref, k_ref, v_ref, o_ref, m_ref, acc_ref, *, tk, nk):
    tq = q_ref.shape[0]
    rep = A_HEADS // A_KV_HEADS
    q = jnp.concatenate([q_ref[:, LANES * r:LANES * (r + 1)] for r in range(rep)], axis=0)
    _softmax_init(m_ref, acc_ref)

    def body(j, carry):
        off = pl.multiple_of(j * tk, tk)
        _softmax_step(_dot_nt(q, k_ref[pl.ds(off, tk), :]), v_ref[pl.ds(off, tk), :], m_ref, acc_ref)
        return carry

    lax.fori_loop(0, nk, body, 0, unroll=4)
    acc = acc_ref[...]
    o = acc / acc[:, _ONE_LANE:_ONE_LANE + 1]
    first = lax.broadcasted_iota(I32, (tq, LANES), 1) < A_HEAD_DIM
    for p in range(rep // 2):
        lo = o[2 * p * tq:(2 * p + 1) * tq]
        hi = pltpu.roll(o[(2 * p + 1) * tq:(2 * p + 2) * tq], A_HEAD_DIM, 1)
        o_ref[:, LANES * p:LANES * (p + 1)] = jnp.where(first, lo, hi).astype(BF)


def _flash_a(qkv, tq=256, tk=2048):
    b, s, _ = qkv.shape
    tq, tk = _tile(s, tq), _tile(s, tk)
    rep = A_HEADS // A_KV_HEADS
    gw = rep * LANES
    return pl.pallas_call(
        functools.partial(_flash_a_kernel, tk=tk, nk=s // tk),
        grid=(b, A_KV_HEADS, s // tq),
        in_specs=[pl.BlockSpec((None, tq, gw), lambda bi, g, i: (bi, i, g)),
                  pl.BlockSpec((None, s, LANES), lambda bi, g, i: (bi, 0, A_HEADS + g)),
                  pl.BlockSpec((None, s, LANES), lambda bi, g, i: (bi, 0, _A_QK_TILES + g))],
        out_specs=pl.BlockSpec((None, tq, rep * A_HEAD_DIM), lambda bi, g, i: (bi, i, g)),
        out_shape=jax.ShapeDtypeStruct((b, s, A_HEADS * A_HEAD_DIM), BF),
        scratch_shapes=[pltpu.VMEM((rep * tq, LANES), F32)] * 2,
        compiler_params=_params("parallel", "parallel", "arbitrary"),
        name="flash_a",
    )(qkv, qkv, qkv)


def _bucket(rel):
    half = REL_BUCKETS // 2
    max_exact = half // 2
    n = jnp.abs(rel)
    large = max_exact + (jnp.log(jnp.maximum(n, 1).astype(F32) / max_exact)
                         / math.log(REL_MAX_DIST / max_exact) * (half - max_exact)).astype(I32)
    large = jnp.minimum(large, half - 1)
    return jnp.where(rel > 0, half, 0) + jnp.where(n < max_exact, n, large)


_B_GW = B_HEADS_PER_GROUP * B_HEAD_DIM
_B_T = 256


def _pad_heads(w, heads, dim):
    w = w.reshape(w.shape[0], heads, dim)
    return jnp.concatenate([w, jnp.zeros((w.shape[0], heads, LANES - dim), w.dtype)], -1).reshape(
        w.shape[0], heads * LANES)


def _prep_b(w_qkv, w_o):
    c = len(B_PATTERNS) * B_HEADS_PER_GROUP * B_HEAD_DIM
    w = jnp.concatenate([w_qkv[:, :c] * (B_HEAD_DIM ** -0.5), w_qkv[:, c:]], 1).astype(BF)
    return w, w_o.astype(BF)


def _toeplitz(vec, t):
    flat = jnp.tile(vec, (1,) * (vec.ndim - 1) + (t,))[..., :t * (2 * t - 1)]
    return flat.reshape(vec.shape[:-1] + (t, 2 * t - 1))[..., :t]


def _band_tiles(g, t):
    window, dil = B_PATTERNS[g]
    return pl.cdiv((window // (2 * dil)) * dil, t)


def _band_bias(rel_bias, g, t):
    window, dil = B_PATTERNS[g]
    reach = (window // (2 * dil)) * dil
    n = _band_tiles(g, t)
    rel0 = (jnp.arange(2 * t) + t) % (2 * t) - t
    rel = jnp.arange(-n, n + 1)[:, None] * t + rel0[None, :]
    bias = rel_bias[_bucket(rel)][:, :, g * B_HEADS_PER_GROUP:(g + 1) * B_HEADS_PER_GROUP]
    bias = jnp.where(((rel % dil == 0) & (jnp.abs(rel) <= reach))[:, :, None], bias, NEG)
    return _toeplitz(jnp.transpose(bias, (2, 0, 1)).astype(F32), t)


def _band_kernel(q_ref, k_ref, v_ref, bias_ref, o_ref, lse_ref, *, n, nk):
    t = q_ref.shape[0]
    i = pl.program_id(2)
    qf = q_ref[...].astype(F32)
    lane = lax.broadcasted_iota(I32, (t, LANES), 1)
    offs = [pl.multiple_of(jnp.clip(i + o, 0, nk - 1) * t, t) for o in range(-n, n + 1)]
    inside = [(i + o >= 0) & (i + o < nk) for o in range(-n, n + 1)]

    def run(at_edge):
        outs, lses = [], []
        for hh in range(2):
            q = jnp.where((lane >= B_HEAD_DIM * hh) & (lane < B_HEAD_DIM * (hh + 1)), qf, 0.0).astype(BF)
            logits = [_dot_nt(q, k_ref[pl.ds(off, t), :]) + bias_ref[hh, c] for c, off in enumerate(offs)]
            if at_edge:
                logits = [jnp.where(ok, s, NEG) for s, ok in zip(logits, inside)]
            m = logits[0].max(-1, keepdims=True)
            for s in logits[1:]:
                m = jnp.maximum(m, s.max(-1, keepdims=True))
            l = jnp.zeros((t, 1), F32)
            acc = jnp.zeros((t, LANES), F32)
            for s, off in zip(logits, offs):
                p = jnp.exp(s - m)
                l = l + jnp.sum(p, -1, keepdims=True)
                acc = acc + _dot(p.astype(BF), v_ref[pl.ds(off, t), :])
            outs.append(acc / l)
            lses.append(jnp.broadcast_to(m + jnp.log(l), (t, LANES)))
        first = lane < B_HEAD_DIM
        o_ref[...] = jnp.where(first, outs[0], outs[1]).astype(BF)
        lse_ref[...] = jnp.where(first, lses[0], lses[1])

    interior = (i >= n) & (i + n < nk)

    @pl.when(interior)
    def _():
        run(False)

    @pl.when(jnp.logical_not(interior))
    def _():
        run(True)


def _band_attention(qkv, bias, g, t):
    b, s, c = qkv.shape
    third = c // 3 // LANES
    pairs = B_HEADS_PER_GROUP // 2
    n = _band_tiles(g, t)
    spec = lambda base: pl.BlockSpec((None, s, LANES), lambda bi, hp, i: (bi, 0, base + g * pairs + hp))
    out_spec = pl.BlockSpec((None, t, LANES), lambda bi, hp, i: (bi, i, hp))
    o, lse = pl.pallas_call(
        functools.partial(_band_kernel, n=n, nk=s // t),
        grid=(b, pairs, s // t),
        in_specs=[pl.BlockSpec((None, t, LANES), lambda bi, hp, i: (bi, i, g * pairs + hp)),
                  spec(third), spec(2 * third),
                  pl.BlockSpec((2, 2 * n + 1, t, t), lambda bi, hp, i: (hp, 0, 0, 0))],
        out_specs=[out_spec, out_spec],
        out_shape=[jax.ShapeDtypeStruct((b, s, _B_GW), BF), jax.ShapeDtypeStruct((b, s, _B_GW), F32)],
        compiler_params=_params("parallel", "parallel", "arbitrary"),
        name="band_%d" % g,
    )(qkv, qkv, qkv, bias)
    return o.reshape(b * s, _B_GW), lse.reshape(b * s, _B_GW)


def _prep_c(lam_re, lam_im, log_dt, b_re, b_im, c_re, c_im):
    hp = lax.Precision.HIGHEST
    L, P, C = C_CHUNK, C_STATE, C_GROUP
    lr, li = lam_re.astype(F32), lam_im.astype(F32)
    dt = jnp.exp(log_dt.astype(F32))[..., None]
    mag = jnp.exp(lr * dt)
    ar, ai = mag * jnp.cos(li * dt), mag * jnp.sin(li * dt)
    den = lr * lr + li * li
    zr = ((ar - 1.0) * lr + ai * li) / den
    zi = (ai * lr - (ar - 1.0) * li) / den
    br, bi = b_re.astype(F32), b_im.astype(F32)
    bbr = zr[..., None] * br - zi[..., None] * bi
    bbi = zr[..., None] * bi + zi[..., None] * br
    cr, ci = c_re.astype(F32), c_im.astype(F32)
    prs, pis = [jnp.ones_like(ar)], [jnp.zeros_like(ai)]
    for _ in range(L):
        pr_, pi_ = prs[-1], pis[-1]
        prs.append(ar * pr_ - ai * pi_)
        pis.append(ar * pi_ + ai * pr_)
    pr, pi = jnp.stack(prs, 0), jnp.stack(pis, 0)

    def lag(pr_k, pi_k):
        tr = pr_k[..., None] * bbr - pi_k[..., None] * bbi
        ti = pr_k[..., None] * bbi + pi_k[..., None] * bbr
        return (jnp.einsum('dgop,kdgpi->kdgoi', cr, tr, precision=hp)
                - jnp.einsum('dgop,kdgpi->kdgoi', ci, ti, precision=hp))

    kern = lag(pr[:L], pi[:L])
    lags = jnp.concatenate([kern[:0:-1, 1], (kern[0, 0] + kern[0, 1])[None], kern[1:, 0]], 0)

    def summ(d, powers):
        pr_k, pi_k = pr[powers, d], pi[powers, d]
        sr = pr_k[..., None] * bbr[d] - pi_k[..., None] * bbi[d]
        si = pr_k[..., None] * bbi[d] + pi_k[..., None] * bbr[d]
        s = jnp.concatenate([sr, si], 2)
        return jnp.transpose(s, (1, 0, 3, 2)).reshape(C_N_GROUPS, L * C, 2 * P)

    w_sum = jnp.concatenate([summ(0, np.arange(L - 1, -1, -1)), summ(1, np.arange(L))], -1)

    def outw(d, powers):
        pr_k, pi_k = pr[powers, d], pi[powers, d]
        wr = cr[d][None] * pr_k[:, :, None, :] - ci[d][None] * pi_k[:, :, None, :]
        wi = -(cr[d][None] * pi_k[:, :, None, :] + ci[d][None] * pr_k[:, :, None, :])
        w = jnp.concatenate([wr, wi], -1)
        return jnp.transpose(w, (1, 3, 0, 2)).reshape(C_N_GROUPS, 2 * P, L * C)

    w_state = jnp.concatenate([outw(0, np.arange(1, L + 1)), outw(1, np.arange(L, 0, -1))], 1)

    nq = C_N_GROUPS // _C_GL
    npair = _C_GL // 2
    blk = L * _C_GL * C
    lane = jnp.arange(LANES)
    gsel = (lane[None, :] // C == jnp.arange(_C_GL)[:, None])
    src = jnp.swapaxes(lags, -1, -2).astype(BF).reshape(2 * L - 1, nq, _C_GL, C, C)
    src = jnp.where(gsel[None, None, :, None, :], jnp.tile(src, (1, 1, 1, 1, _C_GL)), 0)
    w_lag = jnp.transpose(src.reshape(2 * L - 1, nq, LANES, LANES), (1, 0, 2, 3))
    psel = (lane[None, :] // P == jnp.arange(2)[:, None])
    ksel = jnp.eye(npair, dtype=bool)
    src = jnp.tile(w_sum.astype(BF).reshape(nq, npair, 2, L, C, 1, 4, P), (1, 1, 1, 1, 1, npair, 1, 2))
    src = jnp.where(psel[None, None, :, None, None, None, None, :]
                    & ksel[None, :, None, None, None, :, None, None], src, 0)
    w_sum = jnp.transpose(src, (0, 3, 1, 2, 4, 5, 6, 7)).reshape(nq, blk, _C_GL * 4 * P)
    osel = (lane[None, None, :] // C
            == (2 * jnp.arange(npair)[:, None, None] + jnp.arange(2)[None, :, None]))
    src = jnp.tile(w_state.astype(BF).reshape(nq, npair, 2, 4, P, L, C), (1, 1, 1, 1, 1, 1, _C_GL))
    src = jnp.where(osel[None, :, :, None, None, None, :], src, 0)
    w_state = jnp.transpose(src, (0, 1, 3, 2, 4, 5, 6)).reshape(nq, _C_GL * 4 * P, blk)
    dec = jnp.stack([pr[L, 0], pi[L, 0], pr[L, 1], pi[L, 1]], 0).reshape(4, nq * npair, 2 * P)
    dec = jnp.broadcast_to(jnp.transpose(dec, (1, 0, 2))[:, :, None, :], (nq * npair, 4, 8, 2 * P))
    return w_sum.astype(BF), w_lag, w_state.astype(BF), dec


def _chunk_rows(x_ref, nb, rc):
    L = C_CHUNK
    rows = [jnp.concatenate([x_ref[b, pl.ds(j, rc, stride=L), :].astype(BF) for j in range(L)], axis=1)
            for b in range(nb)]
    return jnp.concatenate(rows, axis=0)


def _s5_sum_kernel(x_ref, w_ref, s_ref, *, nb, rc):
    s = _dot(_chunk_rows(x_ref, nb, rc), w_ref[...])
    for b in range(nb):
        for t in range(s_ref.shape[0]):
            s_ref[t, pl.ds(b, rc, stride=nb), :] = s[b * rc:(b + 1) * rc, LANES * t:LANES * (t + 1)]


def _s5_scan_kernel(s_ref, dec_ref, e_ref, *, nc, nb):
    units = s_ref.shape[0] // 4
    dec = [[dec_ref[u, k, 0:nb, :] for k in range(4)] for u in range(units)]

    def body(t, carry):
        rf = pl.ds(pl.multiple_of(t * nb, nb), nb)
        rb = pl.ds(pl.multiple_of((nc - 1 - t) * nb, nb), nb)
        new = []
        for u in range(units):
            fr, fi, br, bi = carry[4 * u:4 * u + 4]
            arf, aif, arb, aib = dec[u]
            e_ref[4 * u, rf, :] = fr
            e_ref[4 * u + 1, rf, :] = fi
            e_ref[4 * u + 2, rb, :] = br
            e_ref[4 * u + 3, rb, :] = bi
            new += [fr * arf - fi * aif + s_ref[4 * u, rf, :], fi * arf + fr * aif + s_ref[4 * u + 1, rf, :],
                    br * arb - bi * aib + s_ref[4 * u + 2, rb, :], bi * arb + br * aib + s_ref[4 * u + 3, rb, :]]
        return tuple(new)

    z = jnp.zeros((nb, LANES), F32)
    lax.fori_loop(0, nc, body, (z,) * (4 * units))


def _s5_out_kernel(x_ref, e_ref, lag_ref, wc_ref, y_ref, wi_ref, *, nb, rc):
    L = C_CHUNK
    half = pl.program_id(2)

    @pl.when((pl.program_id(1) == 0) & (half == 0))
    def _():
        for j in range(L):
            for i in range(L):
                wi_ref[i // (L // 2), LANES * j:LANES * (j + 1), LANES * (i % (L // 2)):LANES * (i % (L // 2) + 1)] = (
                    lag_ref[i - j + L - 1])

    xc = _chunk_rows(x_ref, nb, rc)
    ec = jnp.concatenate(
        [jnp.concatenate([e_ref[t, pl.ds(b, rc, stride=nb), :].astype(BF) for t in range(e_ref.shape[0])], axis=1)
         for b in range(nb)], axis=0)
    y = _dot(xc, wi_ref[half]) + _dot(ec, wc_ref[...])
    for b in range(nb):
        for ii in range(L // 2):
            y_ref[b, pl.ds(half * (L // 2) + ii, rc, stride=L), :] = y[b * rc:(b + 1) * rc, LANES * ii:LANES * (ii + 1)]


def _s5(x, b, s, w_sum, w_lag, w_state, dec):
    L = C_CHUNK
    nc = s // L
    nq = D_MODEL // LANES
    nsl = w_sum.shape[2] // LANES
    blk = L * LANES
    rc = _tile(nc, max(8, 512 // b))
    x3 = x.reshape(b, s, D_MODEL)
    x_spec = pl.BlockSpec((b, rc * L, LANES), lambda q, c, *_: (0, c, q))
    sums = pl.pallas_call(
        functools.partial(_s5_sum_kernel, nb=b, rc=rc),
        grid=(nq, nc // rc),
        in_specs=[x_spec, pl.BlockSpec((None, blk, nsl * LANES), lambda q, c: (q, 0, 0))],
        out_specs=pl.BlockSpec((nsl, rc * b, LANES), lambda q, c: (q, c, 0)),
        out_shape=jax.ShapeDtypeStruct((nq * nsl, nc * b, LANES), F32),
        compiler_params=_params("parallel", "arbitrary"),
        name="s5_sum",
    )(x3, w_sum)
    upb = 2
    ent = pl.pallas_call(
        functools.partial(_s5_scan_kernel, nc=nc, nb=b),
        grid=(nq * nsl // (4 * upb),),
        in_specs=[pl.BlockSpec((4 * upb, nc * b, LANES), lambda i: (i, 0, 0)),
                  pl.BlockSpec((upb, 4, 8, LANES), lambda i: (i, 0, 0, 0))],
        out_specs=pl.BlockSpec((4 * upb, nc * b, LANES), lambda i: (i, 0, 0)),
        out_shape=jax.ShapeDtypeStruct((nq * nsl, nc * b, LANES), F32),
        compiler_params=_params("parallel"),
        name="s5_scan",
    )(sums, dec)
    y = pl.pallas_call(
        functools.partial(_s5_out_kernel, nb=b, rc=rc),
        grid=(nq, nc // rc, 2),
        in_specs=[x_spec,
                  pl.BlockSpec((nsl, rc * b, LANES), lambda q, c, h: (q, c, 0)),
                  pl.BlockSpec((None, 2 * L - 1, LANES, LANES), lambda q, c, h: (q, 0, 0, 0)),
                  pl.BlockSpec((None, nsl * LANES, blk // 2), lambda q, c, h: (q, 0, h))],
        out_specs=pl.BlockSpec((b, rc * L, LANES), lambda q, c, h: (0, c, q)),
        out_shape=jax.ShapeDtypeStruct((b, s, D_MODEL), F32),
        scratch_shapes=[pltpu.VMEM((2, blk, blk // 2), BF)],
        compiler_params=_params("arbitrary", "arbitrary", "arbitrary"),
        name="s5_out",
    )(x3, ent, w_lag, w_state)
    return y.reshape(b * s, D_MODEL)


_D_T = 512


def _prep_d(w_qkv, norm_gain, w_o, lambda_init):
    qk_w = D_HEADS * 2 * D_QK_DIM
    wq = w_qkv[:, :qk_w] * (D_QK_DIM ** -0.5 * LOG2E)
    wv = _pad_heads(w_qkv[:, 2 * qk_w:], D_HEADS, D_V_DIM)
    w = jnp.concatenate([wq, w_qkv[:, qk_w:2 * qk_w], wv], 1).astype(BF)
    gain = jnp.concatenate([norm_gain * (1.0 - lambda_init), jnp.zeros((LANES - D_V_DIM,), F32)])[None, :]
    wo = w_o.reshape(D_HEADS, D_V_DIM, D_MODEL)
    wo = jnp.concatenate([wo, jnp.zeros_like(wo)], 1).reshape(D_HEADS * LANES, D_MODEL).astype(BF)
    return w, gain, wo


def _diff_bias(rel_bias, t):
    rel0 = (jnp.arange(2 * t) + t) % (2 * t) - t
    rel = jnp.arange(-2, 3)[:, None] * t + rel0[None, :]
    return jnp.transpose(rel_bias[_bucket(rel)] * LOG2E, (2, 0, 1)).astype(F32)


def _flash_d_kernel(q_ref, k_ref, v_ref, vec_ref, lam_ref, gain_ref, o_ref, m_ref, acc_ref, bias_ref, *, kt, nk,
                    lambda_init):
    t = q_ref.shape[0]
    i = pl.program_id(2)

    @pl.when(i == 0)
    def _():
        for d in range(bias_ref.shape[0]):
            full = jnp.broadcast_to(vec_ref[d:d + 1, :], (t, 2 * t))
            bias_ref[d] = pltpu.roll(full, 0, 1, stride=1, stride_axis=0)[:, :t]

    qv = q_ref[...].astype(F32)
    lane = lax.broadcasted_iota(I32, (t, LANES), 1)
    lo = (pl.program_id(1) % 2) * (2 * D_QK_DIM)
    q0 = jnp.where((lane >= lo) & (lane < lo + D_QK_DIM), qv, 0.0).astype(BF)
    q1 = jnp.where((lane >= lo + D_QK_DIM) & (lane < lo + 2 * D_QK_DIM), qv, 0.0).astype(BF)
    q = jnp.concatenate([q0, q1], axis=0)
    _softmax_init(m_ref, acc_ref)

    def body(j, carry):
        off = pl.multiple_of(j * (kt * t), kt * t)
        bias = jnp.concatenate([bias_ref[jnp.clip(j * kt + c - i, -2, 2) + 2] for c in range(kt)], axis=1)
        s = _dot_nt(q, k_ref[pl.ds(off, kt * t), :])
        s = (s.reshape(2, t, kt * t) + bias[None]).reshape(2 * t, kt * t)
        _softmax_step(s, v_ref[pl.ds(off, kt * t), :], m_ref, acc_ref)
        return carry

    lax.fori_loop(0, nk, body, 0, unroll=4)
    lf = lam_ref[...]
    lam = (jnp.exp(jnp.sum(lf[0:1] * lf[1:2], keepdims=True))
           - jnp.exp(jnp.sum(lf[2:3] * lf[3:4], keepdims=True)) + lambda_init)
    acc = acc_ref[...]
    on = acc / acc[:, _ONE_LANE:_ONE_LANE + 1]
    o = jnp.where(lane < D_V_DIM, on[:t] - lam * on[t:], 0.0)
    ms = jnp.sum(o * o, -1, keepdims=True) * (1.0 / D_V_DIM)
    o_ref[...] = (o * lax.rsqrt(ms + RMS_EPS) * gain_ref[...]).astype(BF)


def _flash_d(qkv, bias, lam, gain, lambda_init):
    b, s, _ = qkv.shape
    t = bias.shape[-1] // 2
    kt = 2 if s % (2 * t) == 0 else 1
    return pl.pallas_call(
        functools.partial(_flash_d_kernel, kt=kt, nk=s // (kt * t), lambda_init=lambda_init),
        grid=(b, D_HEADS, s // t),
        in_specs=[pl.BlockSpec((None, t, LANES), lambda bi, h, i: (bi, i, h // 2)),
                  pl.BlockSpec((None, s, LANES), lambda bi, h, i: (bi, 0, D_HEADS // 2 + h // 2)),
                  pl.BlockSpec((None, s, LANES), lambda bi, h, i: (bi, 0, D_HEADS + h)),
                  pl.BlockSpec((None, 5, 2 * t), lambda bi, h, i: (h, 0, 0)),
                  pl.BlockSpec((4, D_QK_DIM), lambda bi, h, i: (0, 0)),
                  pl.BlockSpec((1, LANES), lambda bi, h, i: (0, 0))],
        out_specs=pl.BlockSpec((None, t, LANES), lambda bi, h, i: (bi, i, h)),
        out_shape=jax.ShapeDtypeStruct((b, s, D_HEADS * LANES), BF),
        scratch_shapes=[pltpu.VMEM((2 * t, LANES), F32)] * 2 + [pltpu.VMEM((5, t, t), F32)],
        compiler_params=_params("parallel", "parallel", "arbitrary"),
        name="flash_d",
    )(qkv, qkv, qkv, bias, lam, gain)


def _cross_kernel(x_ref, kv_ref, wq_ref, wo_ref, g_ref, b_ref, wr_ref, y_ref, ybf_ref, aff_ref, *, parts):
    tp = x_ref.shape[0] // parts
    wr = wr_ref[...]
    wh = wr.astype(BF)
    wl = (wr - wh.astype(F32)).astype(BF)
    for part in range(parts):
        rows = slice(tp * part, tp * (part + 1))
        x = x_ref[rows, :]
        q = (_dot(x.astype(BF), wq_ref[...]) * (X_HEAD_DIM ** -0.5)).astype(BF)
        outs = []
        for h in range(X_HEADS):
            sl = slice(X_HEAD_DIM * h, X_HEAD_DIM * (h + 1))
            s = _dot_nt(q[:, sl], kv_ref[:, sl])
            p = jnp.exp(s - jnp.max(s, -1, keepdims=True))
            l = jnp.sum(p, -1, keepdims=True)
            vh = kv_ref[:, D_MODEL + X_HEAD_DIM * h:D_MODEL + X_HEAD_DIM * (h + 1)]
            outs.append((_dot(p.astype(BF), vh) / l).astype(BF))
        o = jnp.concatenate(outs, axis=1)
        y = _ln_rows(ALPHA * x + _dot(o, wo_ref[...]), g_ref[...], b_ref[...])
        y_ref[rows, :] = y
        yh = y.astype(BF)
        ybf_ref[rows, :] = yh
        yl = (y - yh.astype(F32)).astype(BF)
        lg = _dot_nt(wh, yh) + _dot_nt(wh, yl) + _dot_nt(wl, yh)
        e = jnp.exp(lg - jnp.max(lg, 0, keepdims=True))
        aff = e / jnp.sum(e, 0, keepdims=True)
        for c in range(tp // LANES):
            aff_ref[part * (tp // LANES) + c] = aff[:, LANES * c:LANES * (c + 1)]


def _cross(x, kv, wq, wo, g, b, wr_t, s, mem_len, tm=1024, parts=2):
    n = x.shape[0]
    tm = _tile(s, tm)
    per = s // tm
    fixed = lambda i: (0, 0)
    return pl.pallas_call(
        functools.partial(_cross_kernel, parts=parts),
        grid=(n // tm,),
        in_specs=[pl.BlockSpec((tm, D_MODEL), lambda i: (i, 0)),
                  pl.BlockSpec((mem_len, 2 * D_MODEL), lambda i: (i // per, 0)),
                  pl.BlockSpec((D_MODEL, D_MODEL), fixed), pl.BlockSpec((D_MODEL, D_MODEL), fixed),
                  pl.BlockSpec((1, D_MODEL), fixed), pl.BlockSpec((1, D_MODEL), fixed),
                  pl.BlockSpec((N_EXPERTS, D_MODEL), fixed)],
        out_specs=[pl.BlockSpec((tm, D_MODEL), lambda i: (i, 0)),
                   pl.BlockSpec((tm, D_MODEL), lambda i: (i, 0)),
                   pl.BlockSpec((tm // LANES, N_EXPERTS, LANES), lambda i: (i, 0, 0))],
        out_shape=[jax.ShapeDtypeStruct((n, D_MODEL), F32),
                   jax.ShapeDtypeStruct((n, D_MODEL), BF),
                   jax.ShapeDtypeStruct((n // LANES, N_EXPERTS, LANES), F32)],
        compiler_params=_params("parallel"),
        name="cross",
    )(x, kv, wq, wo, g, b, wr_t)


def _select_kernel(a_ref, pos_ref, st_ref, *, k, nbits):
    nt = a_ref.shape[0]
    shape = (nt, N_EXPERTS, LANES)
    kf = float(k)

    def keys():
        return lax.bitcast_convert_type(a_ref[...], I32)

    def count(mask):
        c = jnp.sum(jnp.where(mask, 1.0, 0.0), axis=0, keepdims=True)
        return jnp.sum(c, axis=2, keepdims=True)

    def value_step(it, thr):
        cand = thr | jnp.left_shift(jnp.int32(1), 30 - it)
        return jnp.where(count(keys() >= cand) >= kf, cand, thr)

    thr = lax.fori_loop(0, 31, value_step, jnp.zeros((1, N_EXPERTS, 1), I32))
    need = kf - count(keys() > thr)
    idx = lax.broadcasted_iota(I32, shape, 0) * LANES + lax.broadcasted_iota(I32, shape, 2)

    def index_step(it, ithr):
        cand = ithr | jnp.left_shift(jnp.int32(1), nbits - 1 - it)
        return jnp.where(count((keys() == thr) & (idx < cand)) < need, cand, ithr)

    ithr = lax.fori_loop(0, nbits, index_step, jnp.zeros((1, N_EXPERTS, 1), I32))
    thr2, ithr2 = thr[0], ithr[0]
    upper = jnp.where(lax.broadcasted_iota(I32, (LANES, LANES), 0) <= lax.broadcasted_iota(I32, (LANES, LANES), 1),
                      1.0, 0.0).astype(BF)
    lane = lax.broadcasted_iota(I32, (N_EXPERTS, LANES), 1)

    def tile_step(j, carry):
        kj = lax.bitcast_convert_type(a_ref[j], I32)
        sel = (kj > thr2) | ((kj == thr2) & (j * LANES + lane <= ithr2))
        m = jnp.where(sel, 1.0, 0.0)
        inc = _dot(m.astype(BF), upper)
        pos_ref[j] = jnp.where(sel, inc - m + carry, -1.0).astype(I32)
        st_ref[j] = jnp.broadcast_to(carry, (N_EXPERTS, LANES)).astype(I32)
        return carry + inc[:, LANES - 1:LANES]

    lax.fori_loop(0, nt, tile_step, jnp.zeros((N_EXPERTS, 1), F32))


def _select(aff3, k):
    nt = aff3.shape[0]
    nbits = max(1, int(math.ceil(math.log2(nt * LANES))))
    shp = jax.ShapeDtypeStruct(aff3.shape, I32)
    return pl.pallas_call(
        functools.partial(_select_kernel, k=k, nbits=nbits),
        out_shape=[shp, shp],
        compiler_params=pltpu.CompilerParams(vmem_limit_bytes=VMEM_LIMIT),
        name="select",
    )(aff3)


_GATHER_ROWS = LANES + BF16_ROWS
_GATHER_ROWS_SMALL = 3 * BF16_ROWS

_MOE_GROUP = 2


def _moe_ffn_kernel(st_ref, x_ref, pos_ref, aff_ref, wg_ref, wu_ref, wd_ref, ye_ref, buf_ref, gate_ref, *, nb, ncf,
                    nch, sub, ck):
    ep = buf_ref.shape[0]
    grp = pl.program_id(0)
    t = pl.program_id(1)

    @pl.when(t == 0)
    def _():
        buf_ref[...] = jnp.zeros_like(buf_ref)
        gate_ref[...] = jnp.zeros_like(gate_ref)

    @pl.when(t < nb)
    def _():
        def place(s, k, e, base, nrows):
            rows = pl.ds(base, nrows)
            hit = lax.broadcasted_iota(I32, (nrows, LANES), 0) == pos_ref[s, pl.ds(e, 1), :] - base
            xs = x_ref[LANES * s:LANES * (s + 1), :]
            buf_ref[k, rows, :] = buf_ref[k, rows, :] + _dot(jnp.where(hit, 1.0, 0.0).astype(BF), xs).astype(BF)
            gate_ref[k, rows, :] = gate_ref[k, rows, :] + jnp.sum(
                jnp.where(hit, aff_ref[s, pl.ds(e, 1), :], 0.0), -1, keepdims=True)

        def base_of(s, k):
            st = st_ref[grp * ep + k, t * sub + s]
            return pl.multiple_of((st // BF16_ROWS) * BF16_ROWS, BF16_ROWS)

        pairs = [(s, k) for s in range(sub) for k in range(ep)]
        few = functools.reduce(
            jnp.logical_and,
            [st_ref[grp * ep + k, t * sub + s + 1] - base_of(s, k) <= _GATHER_ROWS_SMALL for s, k in pairs])

        @pl.when(few)
        def _():
            for s, k in pairs:
                place(s, k, grp * ep + k, base_of(s, k), _GATHER_ROWS_SMALL)

        @pl.when(jnp.logical_not(few))
        def _():
            for s, k in pairs:
                place(s, k, grp * ep + k, base_of(s, k), _GATHER_ROWS)

    @pl.when(t >= nb)
    def _():
        k = (t - nb) // nch
        c = (t - nb) - k * nch

        @pl.when(c < ncf)
        def _():
            rows = pl.ds(pl.multiple_of(c * ck, ck), ck)
            xe = buf_ref[k, rows, :]
            h = (jax.nn.silu(_dot(xe, wg_ref[...])) * _dot(xe, wu_ref[...])).astype(BF)
            ye_ref[...] = (_dot(h, wd_ref[...]) * gate_ref[k, rows, :]).astype(BF)

        @pl.when(c >= ncf)
        def _():
            ye_ref[...] = jnp.zeros_like(ye_ref)


def _moe_ffn(starts, xbf, pos3, aff3, wg, wu, wd, cap, tb=2048):
    n = xbf.shape[0]
    tb = _tile(n, tb)
    nb = n // tb
    sub = tb // LANES
    ck = _tile(cap, 1024)
    ncf = cap // ck
    nch = ncf + pl.cdiv(_WIN_ROWS, ck)
    dff = wg.shape[-1]
    ep = _MOE_GROUP
    blk = lambda g, t, st: (jnp.minimum(t, nb - 1), 0)
    blk3 = lambda g, t, st: (jnp.minimum(t, nb - 1), 0, 0)
    expert = lambda g, t: g * ep + jnp.clip((t - nb) // nch, 0, ep - 1)
    wmap = lambda g, t, st: (expert(g, t), 0, 0)
    return pl.pallas_call(
        functools.partial(_moe_ffn_kernel, nb=nb, ncf=ncf, nch=nch, sub=sub, ck=ck),
        grid_spec=pltpu.PrefetchScalarGridSpec(
            num_scalar_prefetch=1,
            grid=(N_EXPERTS // ep, nb + ep * nch),
            in_specs=[pl.BlockSpec((tb, D_MODEL), blk),
                      pl.BlockSpec((sub, N_EXPERTS, LANES), blk3),
                      pl.BlockSpec((sub, N_EXPERTS, LANES), blk3),
                      pl.BlockSpec((None, D_MODEL, dff), wmap),
                      pl.BlockSpec((None, D_MODEL, dff), wmap),
                      pl.BlockSpec((None, dff, D_MODEL), wmap)],
            out_specs=pl.BlockSpec((None, ck, D_MODEL),
                                   lambda g, t, st: (expert(g, t), jnp.maximum(t - nb, 0) % nch, 0)),
            scratch_shapes=[pltpu.VMEM((ep, cap + _GATHER_ROWS, D_MODEL), BF),
                            pltpu.VMEM((ep, cap + _GATHER_ROWS, 1), F32)]),
        out_shape=jax.ShapeDtypeStruct((N_EXPERTS, nch * ck, D_MODEL), BF),
        compiler_params=_params("arbitrary", "arbitrary"),
        name="moe_ffn",
    )(starts, xbf, pos3, aff3, wg, wu, wd)


_WIN_STEP = 128
_COMB_TILES = 2
_WIN_ROWS = _WIN_STEP + (_COMB_TILES - 1) * LANES + _GATHER_ROWS


_COMB_PACK = MXU_DEPTH // _GATHER_ROWS_SMALL


def _combine_kernel(st_ref, x_ref, pos_ref, g_ref, b_ref, *rest):
    ye_refs, y_ref = rest[:N_EXPERTS], rest[N_EXPERTS]
    j = pl.program_id(0)
    pairs = [(u, e) for u in range(_COMB_TILES) for e in range(N_EXPERTS)]

    def base_of(u, e):
        return (st_ref[e, j * _COMB_TILES + u] // BF16_ROWS) * BF16_ROWS

    def rows_of(u, e, nrows):
        win = (st_ref[e, j * _COMB_TILES] // _WIN_STEP) * _WIN_STEP
        return ye_refs[e][0, pl.ds(pl.multiple_of(base_of(u, e) - win, BF16_ROWS), nrows), :]

    def finish(u, acc):
        tok = slice(LANES * u, LANES * (u + 1))
        y_ref[tok, :] = _ln_rows(ALPHA * x_ref[tok, :] + acc, g_ref[...], b_ref[...])

    few = functools.reduce(
        jnp.logical_and,
        [st_ref[e, j * _COMB_TILES + u + 1] - base_of(u, e) <= _GATHER_ROWS_SMALL for u, e in pairs])

    @pl.when(few)
    def _():
        for u in range(_COMB_TILES):
            pos = jnp.transpose(pos_ref[u])
            acc = None
            for e0 in range(0, N_EXPERTS, _COMB_PACK):
                es = list(range(e0, min(e0 + _COMB_PACK, N_EXPERTS)))
                col = lax.broadcasted_iota(I32, (LANES, len(es) * _GATHER_ROWS_SMALL), 1)
                hit = None
                for k, e in enumerate(es):
                    rel = pos[:, e:e + 1] - base_of(u, e)
                    h = col == jnp.where(rel >= 0, rel + k * _GATHER_ROWS_SMALL, -1)
                    hit = h if hit is None else hit | h
                slab = jnp.concatenate([rows_of(u, e, _GATHER_ROWS_SMALL) for e in es], axis=0)
                d = _dot(jnp.where(hit, 1.0, 0.0).astype(BF), slab)
                acc = d if acc is None else acc + d
            finish(u, acc)

    @pl.when(jnp.logical_not(few))
    def _():
        col = lax.broadcasted_iota(I32, (LANES, _GATHER_ROWS), 1)
        for u in range(_COMB_TILES):
            pos = jnp.transpose(pos_ref[u])
            acc = None
            for e in range(N_EXPERTS):
                onehot = jnp.where(col == pos[:, e:e + 1] - base_of(u, e), 1.0, 0.0).astype(BF)
                d = _dot(onehot, rows_of(u, e, _GATHER_ROWS))
                acc = d if acc is None else acc + d
            finish(u, acc)


def _combine(starts, x, pos, g, b, ye):
    n = x.shape[0]
    assert ye.shape[1] >= EC_CAPACITY * n // N_EXPERTS + _WIN_ROWS
    tt = _COMB_TILES * LANES
    row = lambda j, st: (j, 0)
    fixed = lambda j, st: (0, 0)

    def ye_spec(e):
        return pl.BlockSpec((pl.Element(1), pl.Element(_WIN_ROWS), pl.Element(D_MODEL)),
                            lambda j, st: (e, (st[e, j * _COMB_TILES] // _WIN_STEP) * _WIN_STEP, 0))

    return pl.pallas_call(
        _combine_kernel,
        grid_spec=pltpu.PrefetchScalarGridSpec(
            num_scalar_prefetch=1,
            grid=(n // tt,),
            in_specs=[pl.BlockSpec((tt, D_MODEL), row),
                      pl.BlockSpec((_COMB_TILES, N_EXPERTS, LANES), lambda j, st: (j, 0, 0)),
                      pl.BlockSpec((1, D_MODEL), fixed), pl.BlockSpec((1, D_MODEL), fixed)]
            + [ye_spec(e) for e in range(N_EXPERTS)],
            out_specs=pl.BlockSpec((tt, D_MODEL), row)),
        out_shape=jax.ShapeDtypeStruct((n, D_MODEL), F32),
        compiler_params=_params("arbitrary"),
        name="combine",
    )(starts, x, pos, g, b, *([ye] * N_EXPERTS))


def _moe(x, xg, aff3, wg, wu, wd, g, b):
    n = x.shape[0]
    cap = EC_CAPACITY * n // N_EXPERTS
    pos3, st3 = _select(aff3, cap)
    starts = jnp.concatenate([jnp.transpose(st3[:, :, 0]), jnp.full((N_EXPERTS, 1), cap, I32)], axis=1)
    ye = _moe_ffn(starts, xg, pos3, aff3, wg, wu, wd, cap)
    return _combine(starts, x, pos3, g, b, ye)


def _prep_weights(p):
    w = {}
    w['a'] = [_prep_a(p['a_w_qkv'][j], p['a_q_gain'][j], p['a_k_gain'][j], p['a_w_o'][j])
              for j in range(p['a_w_qkv'].shape[0])]
    w['b'] = [_prep_b(p['b_w_qkv'][j], p['b_w_o'][j]) for j in range(p['b_w_qkv'].shape[0])]
    w['c'] = [_prep_c(p['c_lam_re'][j], p['c_lam_im'][j], p['c_log_dt'][j], p['c_b_re'][j], p['c_b_im'][j],
                      p['c_c_re'][j], p['c_c_im'][j]) + (p['c_d'][j][None, :], p['c_w_glu'][j].astype(BF))
              for j in range(p['c_lam_re'].shape[0])]
    w['d'] = []
    for j in range(p['d_w_qkv'].shape[0]):
        layer = N_MIXERS * j + 3
        lambda_init = 0.8 - 0.6 * math.exp(-0.3 * layer)
        w['d'].append(_prep_d(p['d_w_qkv'][j], p['d_norm_gain'][j], p['d_w_o'][j], lambda_init)
                      + (p['d_lam'][j].astype(F32), lambda_init))
    w['diff_bias'] = _diff_bias(p['rel_bias'], _D_T)
    w['x_w_q'] = p['x_w_q'].astype(BF)
    w['x_w_kv'] = p['x_w_kv'].astype(BF)
    w['x_w_o'] = p['x_w_o'].astype(BF)
    w['router_t'] = jnp.transpose(p['moe_w_router'], (0, 2, 1)).astype(F32)
    w['moe_w_gate'] = p['moe_w_gate'].astype(BF)
    w['moe_w_up'] = p['moe_w_up'].astype(BF)
    w['moe_w_down'] = p['moe_w_down'].astype(BF)
    return w


def _trunk(x, mem, p, w):
    b, s, _ = x.shape
    n = b * s
    mem_len = mem.shape[1]
    x = x.reshape(n, D_MODEL)
    mem2 = mem.reshape(b * mem_len, D_MODEL)
    ln_g, ln_b = p['ln_g'], p['ln_b']
    for i in range(DEPTH):
        m, j = i % N_MIXERS, i // N_MIXERS
        g0, b0 = ln_g[i, 0][None, :], ln_b[i, 0][None, :]
        if m == 0:
            wa, gains, wo = w['a'][j]
            qkv = _proj_a(x, wa, _rope_table(s), gains, s)
            o = _flash_a(qkv.reshape(b, s, _A_COLS))
            x = _post(x, o.reshape(n, A_HEADS * A_HEAD_DIM), wo, g0, b0)
        elif m == 1:
            wb, wo = w['b'][j]
            qkv = _proj(x, wb).reshape(b, s, wb.shape[1])
            os_, ls_ = [], []
            for g in range(len(B_PATTERNS)):
                t = _tile(s, _B_T)
                o, lse = _band_attention(qkv, _band_bias(p['rel_bias'], g, t), g, t)
                os_.append(o)
                ls_.append(lse)
            x = _post_b(x, os_, ls_, wo, g0, b0)
        elif m == 2:
            w_sum, w_lag, w_state, dec, dskip, wglu = w['c'][j]
            ys = _s5(x, b, s, w_sum, w_lag, w_state, dec)
            x = _post_c(x, ys, dskip, wglu, g0, b0)
        else:
            wd, gain, wo, lam, lambda_init = w['d'][j]
            qkv = _proj(x, wd, ones_from=D_HEADS * LANES).reshape(b, s, wd.shape[1])
            o = _flash_d(qkv, w['diff_bias'], lam, gain, lambda_init)
            x = _post(x, o.reshape(n, D_HEADS * LANES), wo, g0, b0)
        kv = _proj(mem2, w['x_w_kv'][i], tm=mem_len)
        x, xbf, aff3 = _cross(x, kv, w['x_w_q'][i], w['x_w_o'][i], ln_g[i, 1][None, :], ln_b[i, 1][None, :],
                              w['router_t'][i], s, mem_len)
        x = _moe(x, xbf, aff3, w['moe_w_gate'][i], w['moe_w_up'][i], w['moe_w_down'][i],
                 ln_g[i, 2][None, :], ln_b[i, 2][None, :])
    return x.reshape(b, s, D_MODEL)


def kernel(x_prompt, x_sample, mem_prompt, mem_sample, rel_bias, ln_g, ln_b, a_w_qkv, a_q_gain, a_k_gain, a_w_o, b_w_qkv, b_w_o, c_lam_re, c_lam_im, c_log_dt, c_b_re, c_b_im, c_c_re, c_c_im, c_d, c_w_glu, d_w_qkv, d_lam, d_norm_gain, d_w_o, x_w_q, x_w_kv, x_w_o, moe_w_router, moe_w_gate, moe_w_up, moe_w_down):
    p = dict(rel_bias=rel_bias, ln_g=ln_g, ln_b=ln_b,
             a_w_qkv=a_w_qkv, a_q_gain=a_q_gain, a_k_gain=a_k_gain, a_w_o=a_w_o,
             b_w_qkv=b_w_qkv, b_w_o=b_w_o,
             c_lam_re=c_lam_re, c_lam_im=c_lam_im, c_log_dt=c_log_dt, c_b_re=c_b_re, c_b_im=c_b_im,
             c_c_re=c_c_re, c_c_im=c_c_im, c_d=c_d, c_w_glu=c_w_glu,
             d_w_qkv=d_w_qkv, d_lam=d_lam, d_norm_gain=d_norm_gain, d_w_o=d_w_o,
             x_w_q=x_w_q, x_w_kv=x_w_kv, x_w_o=x_w_o,
             moe_w_router=moe_w_router, moe_w_gate=moe_w_gate, moe_w_up=moe_w_up, moe_w_down=moe_w_down)
    w = _prep_weights(p)
    return (_trunk(x_prompt, mem_prompt, p, w), _trunk(x_sample, mem_sample, p, w))
```
